```python
import math
import jax, jax.numpy as jnp
from jax import lax
import numpy as np

D_MODEL = 1024
BATCH = 4
SEQ = 8192
DEPTH = 1

HEAD_DIM = 64
N_ATTN_HEADS = 8
N_KV_HEADS = 2
GQA_RATIO = N_ATTN_HEADS // N_KV_HEADS
ATTN_WIDTH = N_ATTN_HEADS * HEAD_DIM
KV_WIDTH = N_KV_HEADS * HEAD_DIM
CONV_GROUPS = 8
CONV_WIDTH = D_MODEL - ATTN_WIDTH
CONV_K = 3
CMP_BLOCK = 32
CMP_STRIDE = 16
CMP_HIDDEN = 256
SEL_BLOCK = 64
N_SELECT = 16
WINDOW = 512
Q_BLOCK = 64
SEL_FORCE = 1.0e6
REL_BUCKETS = 32
REL_MAX_DIST = 1024
D_FF = 4 * D_MODEL
N_MOD = 6
EPS = 1e-6
NEG = -1e30
IN_SIZES = [ATTN_WIDTH] + [KV_WIDTH] * 6 + [3 * N_ATTN_HEADS] + [CONV_WIDTH] * 3
IN_COLS = sum(IN_SIZES)
IN_OFFSETS = [int(v) for v in np.cumsum(IN_SIZES)[:-1]]

kernel_name = "hymba_nsa_shortconv_adaln_layer"


def rms_norm(x, g):
    x32 = x.astype(jnp.float32)
    y = x32 * lax.rsqrt(jnp.mean(x32 * x32, axis=-1, keepdims=True) + EPS)
    return (y * g.astype(jnp.float32)).astype(x.dtype)


def rel_bucket(dist):
    n = jnp.maximum(dist, 0)
    max_exact = REL_BUCKETS // 2
    nf = jnp.maximum(n, max_exact).astype(jnp.float32)
    large = max_exact + (jnp.log(nf / max_exact) / math.log(REL_MAX_DIST / max_exact)
                         * (REL_BUCKETS - max_exact)).astype(jnp.int32)
    large = jnp.minimum(large, REL_BUCKETS - 1)
    return jnp.where(n < max_exact, n, large)


def masked_softmax(s, mask):
    s = jnp.where(mask, s.astype(jnp.float32), NEG)
    s = s - jnp.max(s, axis=-1, keepdims=True)
    e = jnp.exp(s) * mask
    den = jnp.sum(e, axis=-1, keepdims=True)
    return e / jnp.where(den > 0, den, 1.0)


def compress(blocks, pe, w1, w2):
    b, n = blocks.shape[:2]
    h = blocks + pe[None, None, :, None, :]
    h = jnp.moveaxis(h, 3, 2).reshape(b, n, N_KV_HEADS, CMP_BLOCK * HEAD_DIM)
    return jax.nn.gelu(h @ w1) @ w2


def nsa_attention(q, kc, vc, ks, vs, kw, vw, gates, rel_bias):
    b, t = q.shape[:2]
    G, R = N_KV_HEADS, GQA_RATIO
    n_cmp = kc.shape[1]
    n_sb = t // SEL_BLOCK
    top_n = min(N_SELECT, n_sb)
    span_w = WINDOW + Q_BLOCK
    scale = HEAD_DIM ** -0.5
    cmp_end = jnp.arange(n_cmp) * CMP_STRIDE + CMP_BLOCK - 1
    c_start = np.arange(n_cmp)[:, None] * CMP_STRIDE
    s_start = np.arange(n_sb)[None, :] * SEL_BLOCK
    ov = np.clip(np.minimum(c_start + CMP_BLOCK, s_start + SEL_BLOCK) - np.maximum(c_start, s_start), 0, None)
    overlap = jnp.asarray(ov / CMP_BLOCK, dtype=jnp.float32)
    ks_blocks = ks.reshape(b, n_sb, SEL_BLOCK, G, HEAD_DIM).transpose(0, 3, 1, 2, 4)
    vs_blocks = vs.reshape(b, n_sb, SEL_BLOCK, G, HEAD_DIM).transpose(0, 3, 1, 2, 4)
    kw_pad = jnp.pad(kw, ((0, 0), (WINDOW, 0), (0, 0), (0, 0)))
    vw_pad = jnp.pad(vw, ((0, 0), (WINDOW, 0), (0, 0), (0, 0)))
    bias_gr = rel_bias.reshape(REL_BUCKETS, G, R)
    bi = jnp.arange(b)[:, None, None, None]
    gi = jnp.arange(G)[None, :, None, None]
    gi5 = jnp.arange(G)[None, :, None, None, None]
    jb = jnp.arange(n_sb)

    def block_fn(ci):
        t0 = ci * Q_BLOCK
        tpos = t0 + jnp.arange(Q_BLOCK)
        qc = lax.dynamic_slice_in_dim(q, t0, Q_BLOCK, 1).reshape(b, Q_BLOCK, G, R, HEAD_DIM)
        gc = lax.dynamic_slice_in_dim(gates, t0, Q_BLOCK, 1).reshape(b, Q_BLOCK, G, R, 3)
        dist_c = tpos[:, None] - cmp_end[None, :]
        bias_c = rel_bias[rel_bucket(dist_c)].reshape(Q_BLOCK, n_cmp, G, R).transpose(2, 3, 0, 1)
        s_c = jnp.einsum('bqgrd,bngd->bgrqn', qc, kc).astype(jnp.float32) * scale + bias_c
        p_c = masked_softmax(s_c, dist_c >= 0)
        o_c = jnp.einsum('bgrqn,bngd->bqgrd', p_c.astype(vc.dtype), vc)
        imp = jnp.einsum('bgrqn,nj->bgqj', p_c, overlap)
        cur = tpos // SEL_BLOCK
        valid = jb[None, :] <= cur[:, None]
        forced = (jb[None, :] == 0) | (jb[None, :] == cur[:, None]) | (jb[None, :] == cur[:, None] - 1)
        score = jnp.where(valid, jnp.where(forced, SEL_FORCE, imp), -1.0)
        _, idx = lax.top_k(score, top_n)
        ks_sel = ks_blocks[bi, gi, idx]
        vs_sel = vs_blocks[bi, gi, idx]
        kpos = idx[..., None] * SEL_BLOCK + jnp.arange(SEL_BLOCK)
        dist_s = tpos[:, None, None] - kpos
        bias_s = jnp.moveaxis(bias_gr[rel_bucket(dist_s), gi5], -1, 2)
        s_s = jnp.einsum('bqgrd,bgqnld->bgrqnl', qc, ks_sel).astype(jnp.float32) * scale + bias_s
        p_s = masked_softmax(s_s.reshape(b, G, R, Q_BLOCK, top_n * SEL_BLOCK),
                             (dist_s >= 0).reshape(b, G, 1, Q_BLOCK, top_n * SEL_BLOCK))
        o_s = jnp.einsum('bgrqnl,bgqnld->bqgrd', p_s.reshape(s_s.shape).astype(vs.dtype), vs_sel)
        kwc = lax.dynamic_slice_in_dim(kw_pad, t0, span_w, 1)
        vwc = lax.dynamic_slice_in_dim(vw_pad, t0, span_w, 1)
        kpos_w = t0 - WINDOW + jnp.arange(span_w)
        dist_w = tpos[:, None] - kpos_w[None, :]
        mask_w = (dist_w >= 0) & (dist_w < WINDOW) & (kpos_w >= 0)[None, :]
        bias_w = rel_bias[rel_bucket(dist_w)].reshape(Q_BLOCK, span_w, G, R).transpose(2, 3, 0, 1)
        s_w = jnp.einsum('bqgrd,bsgd->bgrqs', qc, kwc).astype(jnp.float32) * scale + bias_w
        p_w = masked_softmax(s_w, mask_w)
        o_w = jnp.einsum('bgrqs,bsgd->bqgrd', p_w.astype(vw.dtype), vwc)
        o = gc[..., 0:1] * o_c + gc[..., 1:2] * o_s + gc[..., 2:3] * o_w
        return o.reshape(b, Q_BLOCK, ATTN_WIDTH)

    out = lax.map(block_fn, jnp.arange(t // Q_BLOCK))
    return out.transpose(1, 0, 2, 3).reshape(b, t, ATTN_WIDTH)


def hybrid_layer(x, c, w_in, q_norm, k_norm, cmp_pe_k, cmp_w1_k, cmp_w2_k, cmp_pe_v, cmp_w1_v, cmp_w2_v,
                 rel_bias, conv_w, w_out, norm1, norm2, w_ada, b_ada, w_ff1, w_ff2):
    b, t, d = x.shape
    G = N_KV_HEADS
    mod = (jax.nn.silu(c) @ w_ada + b_ada).reshape(b, N_MOD, 1, d)
    shift1, scale1, gate1, shift2, scale2, gate2 = (mod[:, i] for i in range(N_MOD))
    h = rms_norm(x, norm1) * (1 + scale1) + shift1
    proj = h @ w_in
    q, kc_raw, vc_raw, ks, vs, kw, vw, g, cgate, bgate, u = jnp.split(proj, IN_OFFSETS, axis=-1)
    q = rms_norm(q.reshape(b, t, N_ATTN_HEADS, HEAD_DIM), q_norm)
    ks = rms_norm(ks.reshape(b, t, G, HEAD_DIM), k_norm)
    kw = rms_norm(kw.reshape(b, t, G, HEAD_DIM), k_norm)
    vs = vs.reshape(b, t, G, HEAD_DIM)
    vw = vw.reshape(b, t, G, HEAD_DIM)
    n_cmp = (t - CMP_BLOCK) // CMP_STRIDE + 1
    cidx = (jnp.arange(n_cmp) * CMP_STRIDE)[:, None] + jnp.arange(CMP_BLOCK)[None, :]
    kc_blocks = kc_raw.reshape(b, t, G, HEAD_DIM)[:, cidx]
    vc_blocks = vc_raw.reshape(b, t, G, HEAD_DIM)[:, cidx]
    kc = rms_norm(compress(kc_blocks, cmp_pe_k, cmp_w1_k, cmp_w2_k), k_norm)
    vc = compress(vc_blocks, cmp_pe_v, cmp_w1_v, cmp_w2_v)
    gates = jax.nn.sigmoid(g).reshape(b, t, N_ATTN_HEADS, 3)
    attn_out = nsa_attention(q, kc, vc, ks, vs, kw, vw, gates, rel_bias)
    z = cgate * u
    zc = lax.conv_general_dilated(z, conv_w.reshape(CONV_K, 1, CONV_WIDTH).astype(z.dtype),
                                  window_strides=(1,), padding=[(CONV_K - 1, 0)],
                                  dimension_numbers=('NWC', 'WIO', 'NWC'),
                                  feature_group_count=CONV_WIDTH)
    conv_out = bgate * zc
    mix = jnp.concatenate([attn_out, conv_out], axis=-1) @ w_out
    x = x + gate1 * mix
    h2 = rms_norm(x, norm2) * (1 + scale2) + shift2
    ff = jnp.square(jax.nn.relu(h2 @ w_ff1)) @ w_ff2
    return x + gate2 * ff


def setup_inputs(seed: int = 0) -> dict:
    key = jax.random.key(seed)
    ks = jax.random.split(key, 24)
    L = DEPTH

    def nrm(k, shape, s):
        return jax.random.normal(k, shape, jnp.float32) * s

    return {
        "x": nrm(ks[0], (BATCH, SEQ, D_MODEL), 1.0),
        "c": nrm(ks[1], (BATCH, D_MODEL), 1.0),
        "w_in": nrm(ks[2], (L, D_MODEL, IN_COLS), D_MODEL ** -0.5),
        "q_norm": 1.0 + nrm(ks[3], (L, HEAD_DIM), 0.02),
        "k_norm": 1.0 + nrm(ks[4], (L, HEAD_DIM), 0.02),
        "cmp_pe_k": nrm(ks[5], (L, CMP_BLOCK, HEAD_DIM), 0.1),
        "cmp_w1_k": nrm(ks[6], (L, CMP_BLOCK * HEAD_DIM, CMP_HIDDEN), (CMP_BLOCK * HEAD_DIM) ** -0.5),
        "cmp_w2_k": nrm(ks[7], (L, CMP_HIDDEN, HEAD_DIM), CMP_HIDDEN ** -0.5),
        "cmp_pe_v": nrm(ks[8], (L, CMP_BLOCK, HEAD_DIM), 0.1),
        "cmp_w1_v": nrm(ks[9], (L, CMP_BLOCK * HEAD_DIM, CMP_HIDDEN), (CMP_BLOCK * HEAD_DIM) ** -0.5),
        "cmp_w2_v": nrm(ks[10], (L, CMP_HIDDEN, HEAD_DIM), CMP_HIDDEN ** -0.5),
        "rel_bias": nrm(ks[11], (REL_BUCKETS, N_ATTN_HEADS), 0.2),
        "conv_w": nrm(ks[12], (L, CONV_K, CONV_WIDTH), CONV_K ** -0.5),
        "w_out": nrm(ks[13], (L, D_MODEL, D_MODEL), D_MODEL ** -0.5),
        "norm1": 1.0 + nrm(ks[14], (L, D_MODEL), 0.02),
        "norm2": 1.0 + nrm(ks[15], (L, D_MODEL), 0.02),
        "w_ada": nrm(ks[16], (L, D_MODEL, N_MOD * D_MODEL), 0.5 * D_MODEL ** -0.5),
        "b_ada": nrm(ks[17], (L, N_MOD * D_MODEL), 0.02),
        "w_ff1": nrm(ks[18], (L, D_MODEL, D_FF), D_MODEL ** -0.5),
        "w_ff2": nrm(ks[19], (L, D_FF, D_MODEL), D_FF ** -0.5),
    }


def reference(x, c, w_in, q_norm, k_norm, cmp_pe_k, cmp_w1_k, cmp_w2_k, cmp_pe_v, cmp_w1_v, cmp_w2_v,
              rel_bias, conv_w, w_out, norm1, norm2, w_ada, b_ada, w_ff1, w_ff2):
    for l in range(DEPTH):
        x = hybrid_layer(x, c, w_in[l], q_norm[l], k_norm[l], cmp_pe_k[l], cmp_w1_k[l], cmp_w2_k[l],
                         cmp_pe_v[l], cmp_w1_v[l], cmp_w2_v[l], rel_bias, conv_w[l], w_out[l],
                         norm1[l], norm2[l], w_ada[l], b_ada[l], w_ff1[l], w_ff2[l])
    return x
```

```python
import functools
import math

import numpy as np
import jax
import jax.numpy as jnp
from jax import lax
from jax.experimental import pallas as pl
from jax.experimental.pallas import tpu as pltpu

HEAD_DIM = 64
N_HEADS = 8
N_KV = 2
GQA = N_HEADS // N_KV
ATTN_W = N_HEADS * HEAD_DIM
KV_W = N_KV * HEAD_DIM
CONV_K = 3
CMP_BLOCK = 32
CMP_STRIDE = 16
CMP_HIDDEN = 256
SEL_BLOCK = 64
N_SELECT = 16
WINDOW = 512
Q_BLOCK = 64
SEL_FORCE = 1.0e6
REL_BUCKETS = 32
REL_MAX_DIST = 1024
N_MOD = 6
EPS = 1e-6
NEG = -1e30

LANES = 128
GQ = GQA * Q_BLOCK
KEY_CHUNK = 256
BLOCKS_PER_CHUNK = KEY_CHUNK // SEL_BLOCK
V_TILE = 128
WIN_BLOCKS = WINDOW // SEL_BLOCK
WIN_SPAN = (WIN_BLOCKS + 2) * SEL_BLOCK
NEAR_TILES = (REL_MAX_DIST + Q_BLOCK - 1) // SEL_BLOCK + 1
ROW_TILE = 512
FF_CHUNK = 1024
VMEM_LIMIT = 56 * 1024 * 1024

F32 = jnp.float32
BF16 = jnp.bfloat16


def _cparams(n_axes):
    return pltpu.CompilerParams(dimension_semantics=("arbitrary",) * n_axes,
                                vmem_limit_bytes=VMEM_LIMIT)


def _mod_kernel(c_ref, w_ref, b_ref, o_ref):
    c = c_ref[...]
    a = c * jax.nn.sigmoid(c)
    o_ref[...] = jnp.dot(a, w_ref[...], preferred_element_type=F32,
                         precision=lax.Precision.HIGHEST) + b_ref[...]


def _mod_call(c_pad, w_ada, b_ada):
    rows, d = c_pad.shape
    n = w_ada.shape[1]
    tn = 1024
    return pl.pallas_call(
        _mod_kernel,
        grid=(n // tn,),
        in_specs=[pl.BlockSpec((rows, d), lambda j: (0, 0)),
                  pl.BlockSpec((d, tn), lambda j: (0, j)),
                  pl.BlockSpec((1, tn), lambda j: (0, j))],
        out_specs=pl.BlockSpec((rows, tn), lambda j: (0, j)),
        out_shape=jax.ShapeDtypeStruct((rows, n), F32),
        compiler_params=_cparams(1),
        name="adaln_mod",
    )(c_pad, w_ada, b_ada.reshape(1, n))


_C_Q = 0
_C_KC = ATTN_W
_C_VC = _C_KC + KV_W
_C_KS = _C_VC + KV_W
_C_VS = _C_KS + KV_W
_C_KW = _C_VS + KV_W
_C_VW = _C_KW + KV_W
_C_CG = _C_VW + KV_W


def _inproj_kernel(x_ref, mod_ref, n1_ref, w_ref, qn_ref, kn_ref, cw_ref, bdq_ref, bdk_ref,
                   q_out, kc_out, vc_out, ks_out, vs_out, kw_out, vw_out, g_out, conv_out,
                   carry, *, conv_w):
    t = pl.program_id(1)
    tm = x_ref.shape[1]
    x = x_ref[0]
    ms = jnp.mean(x * x, axis=-1, keepdims=True)
    y = x * lax.rsqrt(ms + EPS) * n1_ref[...]
    h = (y * (1.0 + mod_ref[0, 1:2, :]) + mod_ref[0, 0:1, :]).astype(BF16)

    def proj(a, b):
        return jnp.dot(h, w_ref[:, a:b], preferred_element_type=F32)

    def head_norm(v, bd_ref, gain):
        ssq = jnp.dot((v * v).astype(BF16), bd_ref[...], preferred_element_type=F32)
        return v * lax.rsqrt(ssq * (1.0 / HEAD_DIM) + EPS) * gain

    q = proj(_C_Q, _C_KC)
    q_out[0] = head_norm(q, bdq_ref, qn_ref[...]).astype(BF16)
    kc_out[0] = proj(_C_KC, _C_VC)
    vc_out[0] = proj(_C_VC, _C_KS)
    ks_out[0] = head_norm(proj(_C_KS, _C_VS), bdk_ref, kn_ref[...]).astype(BF16)
    vs_out[0] = proj(_C_VS, _C_KW).astype(BF16)
    kw_out[0] = head_norm(proj(_C_KW, _C_VW), bdk_ref, kn_ref[...]).astype(BF16)
    vw_out[0] = proj(_C_VW, _C_CG).astype(BF16)

    c_bg = _C_CG + conv_w
    c_u = c_bg + conv_w
    c_g = c_u + conv_w
    z = proj(_C_CG, c_bg) * proj(c_u, c_g)
    g_out[0] = jax.nn.sigmoid(proj(c_g, c_g + LANES))

    @pl.when(t == 0)
    def _():
        carry[...] = jnp.zeros_like(carry)

    row = lax.broadcasted_iota(jnp.int32, z.shape, 0)
    p2 = carry[6:7, :]
    p1 = carry[7:8, :]
    z1 = jnp.where(row == 0, p1, pltpu.roll(z, 1, 0))
    z2 = jnp.where(row == 0, p2, jnp.where(row == 1, p1, pltpu.roll(z, 2, 0)))
    zc = cw_ref[0:1, :] * z2 + cw_ref[1:2, :] * z1 + cw_ref[2:3, :] * z
    conv_out[0] = (proj(c_bg, c_u) * zc).astype(BF16)
    carry[...] = z[tm - 8:tm, :]


def _inproj_call(x, mod, norm1, w_pack, qn_t, kn_t, conv_w, bdq, bdk):
    b, t, d = x.shape
    tm = min(ROW_TILE, t)
    cw = conv_w.shape[1]
    ncols = w_pack.shape[1]
    row_spec = lambda w: pl.BlockSpec((1, tm, w), lambda i, j: (i, j, 0))
    full = lambda shp: pl.BlockSpec(shp, lambda i, j: (0,) * len(shp))
    outs = [
        (ATTN_W, BF16), (KV_W, F32), (KV_W, F32), (KV_W, BF16), (KV_W, BF16), (KV_W, BF16),
        (KV_W, BF16), (LANES, F32), (cw, BF16)]
    return pl.pallas_call(
        functools.partial(_inproj_kernel, conv_w=cw),
        grid=(b, t // tm),
        in_specs=[row_spec(d),
                  pl.BlockSpec((1, N_MOD, d), lambda i, j: (i, 0, 0)),
                  full((1, d)), full((d, ncols)), full((1, ATTN_W)), full((1, KV_W)),
                  full((CONV_K, cw)), full((ATTN_W, ATTN_W)), full((KV_W, KV_W))],
        out_specs=[row_spec(w) for w, _ in outs],
        out_shape=[jax.ShapeDtypeStruct((b, t, w), dt) for w, dt in outs],
        scratch_shapes=[pltpu.VMEM((8, cw), F32)],
        compiler_params=_cparams(2),
        name="inproj",
    )(x, mod, norm1, w_pack, qn_t, kn_t, conv_w, bdq, bdk)


def _compress_kernel(kx_ref, vx_ref, pek_ref, pev_ref, w1k_ref, w1v_ref, w2k_ref, w2v_ref,
                     kn_ref, bdk_ref, kc_out, vc_out):
    def mlp(x_ref, pe_ref, w1_ref, w2_ref):
        xv = x_ref[0]
        u = jnp.dot((xv + pe_ref[0:1, :]).astype(BF16), w1_ref[0], preferred_element_type=F32)
        v = jnp.dot((xv + pe_ref[1:2, :]).astype(BF16), w1_ref[1], preferred_element_type=F32)
        n = u.shape[0]
        hid = u + pltpu.roll(v, n - 1, 0)
        act = jax.nn.gelu(hid, approximate=True).astype(BF16)
        return jnp.dot(act, w2_ref[...], preferred_element_type=F32)

    kc = mlp(kx_ref, pek_ref, w1k_ref, w2k_ref)
    ssq = jnp.dot((kc * kc).astype(BF16), bdk_ref[...], preferred_element_type=F32)
    kc_out[0] = (kc * lax.rsqrt(ssq * (1.0 / HEAD_DIM) + EPS) * kn_ref[...]).astype(BF16)
    vc_out[0] = mlp(vx_ref, pev_ref, w1v_ref, w2v_ref).astype(BF16)


def _compress_call(kx, vx, pek, pev, w1k, w1v, w2k, w2v, kn_t, bdk):
    b, nrow, wide = kx.shape
    hid2 = w1k.shape[2]
    full = lambda shp: pl.BlockSpec(shp, lambda i: (0,) * len(shp))
    xs = pl.BlockSpec((1, nrow, wide), lambda i: (i, 0, 0))
    os_ = pl.BlockSpec((1, nrow, KV_W), lambda i: (i, 0, 0))
    return pl.pallas_call(
        _compress_kernel,
        grid=(b,),
        in_specs=[xs, xs, full((2, wide)), full((2, wide)), full((2, wide, hid2)),
                  full((2, wide, hid2)), full((hid2, KV_W)), full((hid2, KV_W)),
                  full((1, KV_W)), full((KV_W, KV_W))],
        out_specs=[os_, os_],
        out_shape=[jax.ShapeDtypeStruct((b, nrow, KV_W), BF16)] * 2,
        compiler_params=_cparams(1),
        name="compress",
    )(kx, vx, pek, pev, w1k, w1v, w2k, w2v, kn_t, bdk)


def _attn_kernel(q_ref, kc_ref, vct_ref, ks_ref, vst_ref, kw_ref, vwt_ref, gt_ref, ovt_ref,
                 eb_ref, sb_ref, wb_ref, far_ref, o_ref, sel_scr, acc_scr,
                 *, n_cmp_rows, eb_shift):
    ci = pl.program_id(1)
    par = lax.rem(ci, 2)
    n_sb = sel_scr.shape[1]

    cmp_c = n_cmp_rows - 4
    e0 = cmp_c - 4 * ci + jnp.where(par == 0, eb_shift[0], eb_shift[1])
    e0 = pl.multiple_of(e0, 8)
    o_cmp = []
    imp = []
    for g in range(N_KV):
        qt = q_ref[0, 0, g]
        sc = jnp.dot(kc_ref[0, g], qt, preferred_element_type=F32)
        sc = sc + eb_ref[par, g, pl.ds(e0, n_cmp_rows), :]
        m = jnp.max(sc, axis=0, keepdims=True)
        e = jnp.exp(sc - m)
        den = jnp.sum(e, axis=0, keepdims=True)
        inv = jnp.where(m > 0.5 * NEG, 1.0 / den, 0.0)
        pn = (e * inv).astype(BF16)
        o_cmp.append(jnp.dot(vct_ref[0, g], pn, preferred_element_type=F32))
        ir = jnp.dot(ovt_ref[...], pn, preferred_element_type=F32)
        a = ir[:, :LANES] + ir[:, LANES:]
        imp.append(a + pltpu.roll(a, Q_BLOCK, 1))

    lane = lax.broadcasted_iota(jnp.int32, (n_sb, LANES), 1)
    jidx = lax.broadcasted_iota(jnp.int32, (n_sb, LANES), 0)
    jf = jidx.astype(F32)
    valid = jidx <= ci
    forced = (jidx == 0) | (jidx == ci) | (jidx == ci - 1)
    score = jnp.where(valid, jnp.where(forced, SEL_FORCE, jnp.where(lane < Q_BLOCK, imp[0], imp[1])),
                      -1.0)
    sel = jnp.zeros((n_sb, LANES), F32)
    for _ in range(min(N_SELECT, n_sb)):
        mx = jnp.max(score, axis=0, keepdims=True)
        first = jnp.min(jnp.where(score == mx, jf, float(n_sb)), axis=0, keepdims=True)
        hit = jf == first
        sel = jnp.where(hit, 1.0, sel)
        score = jnp.where(hit, -2.0, score)
    selneg = jnp.where((sel > 0.5) & valid, 0.0, NEG)
    swapped = pltpu.roll(selneg, Q_BLOCK, 1)
    left = jnp.where(lane < Q_BLOCK, selneg, swapped)
    right = jnp.where(lane < Q_BLOCK, swapped, selneg)
    sel_scr[0] = jnp.concatenate([left, left], axis=1)
    sel_scr[1] = jnp.concatenate([right, right], axis=1)

    n_chunks = ci // BLOCKS_PER_CHUNK + 1
    c_near = jnp.maximum(ci - (NEAR_TILES - 1), 0) // BLOCKS_PER_CHUNK

    for g in range(N_KV):
        qt = q_ref[0, 0, g]
        far_row = far_ref[g]
        acc_scr[g] = jnp.zeros((HEAD_DIM, GQ), F32)

        def chunk_step(c, carry, near):
            m_run, l_run = carry
            kk = ks_ref[0, g, pl.ds(pl.multiple_of(c * KEY_CHUNK, KEY_CHUNK), KEY_CHUNK), :]
            s = jnp.dot(kk, qt, preferred_element_type=F32)
            parts = []
            for i in range(BLOCKS_PER_CHUNK):
                kb = c * BLOCKS_PER_CHUNK + i
                blk = s[i * SEL_BLOCK:(i + 1) * SEL_BLOCK, :]
                mrow = sel_scr[g, pl.ds(kb, 1), :]
                if near:
                    tile = jnp.clip(ci - kb, 0, NEAR_TILES)
                    blk = blk + sb_ref[g, tile] + mrow
                else:
                    blk = blk + (mrow + far_row)
                parts.append(blk)
            s = jnp.concatenate(parts, axis=0)
            m_new = jnp.maximum(m_run, jnp.max(s, axis=0, keepdims=True))
            alpha = jnp.exp(m_run - m_new)
            p = jnp.exp(s - m_new)
            l_new = alpha * l_run + jnp.sum(p, axis=0, keepdims=True)
            pb = p.astype(BF16)
            pv = jnp.zeros((HEAD_DIM, GQ), F32)
            for j in range(KEY_CHUNK // V_TILE):
                vt = vst_ref[0, g, c * (KEY_CHUNK // V_TILE) + j]
                pv = pv + jnp.dot(vt, pb[j * V_TILE:(j + 1) * V_TILE, :],
                                  preferred_element_type=F32)
            acc_scr[g] = alpha * acc_scr[g] + pv
            return m_new, l_new

        init = (jnp.full((1, GQ), NEG, F32), jnp.zeros((1, GQ), F32))
        carry = lax.fori_loop(0, c_near, functools.partial(chunk_step, near=False), init)
        _, l_sel = lax.fori_loop(c_near, n_chunks, functools.partial(chunk_step, near=True), carry)
        o_sel = acc_scr[g] * (1.0 / l_sel)

        w0 = ci // 2
        kwin = kw_ref[0, g, pl.ds(pl.multiple_of(w0 * V_TILE, V_TILE), WIN_SPAN), :]
        s = jnp.dot(kwin, qt, preferred_element_type=F32)
        parts = []
        for i in range(WIN_SPAN // SEL_BLOCK):
            delta = par + WIN_BLOCKS - i
            ok = (delta >= 0) & (delta <= WIN_BLOCKS) & (delta <= ci)
            tile = jnp.where(ok, delta, WIN_BLOCKS + 1)
            parts.append(s[i * SEL_BLOCK:(i + 1) * SEL_BLOCK, :] + wb_ref[g, tile])
        s = jnp.concatenate(parts, axis=0)
        m = jnp.max(s, axis=0, keepdims=True)
        p = jnp.exp(s - m)
        l_win = jnp.sum(p, axis=0, keepdims=True)
        pb = p.astype(BF16)
        o_win = jnp.zeros((HEAD_DIM, GQ), F32)
        for j in range(WIN_SPAN // V_TILE):
            o_win = o_win + jnp.dot(vwt_ref[0, g, w0 + j], pb[j * V_TILE:(j + 1) * V_TILE, :],
                                    preferred_element_type=F32)
        o_win = o_win * (1.0 / l_win)

        out = (gt_ref[0, 0, 0, g:g + 1, :] * o_cmp[g] + gt_ref[0, 0, 1, g:g + 1, :] * o_sel
               + gt_ref[0, 0, 2, g:g + 1, :] * o_win)
        o_ref[0, 0, g] = out.astype(BF16)


def _attn_call(qt, kc, vct, ks, vst, kwp, vwt, gt, ovt, ebank, sbank, wbank, far, eb_shift):
    b, nq = qt.shape[0], qt.shape[1]
    t = ks.shape[2]
    n_cmp_rows = kc.shape[2]
    n_sb = t // SEL_BLOCK
    per_b = lambda shp: pl.BlockSpec((1,) + shp[1:], lambda i, j: (i,) + (0,) * (len(shp) - 1))
    per_q = lambda shp: pl.BlockSpec((1, 1) + shp[2:], lambda i, j: (i, j) + (0,) * (len(shp) - 2))
    full = lambda shp: pl.BlockSpec(shp, lambda i, j: (0,) * len(shp))
    args = (qt, kc, vct, ks, vst, kwp, vwt, gt, ovt, ebank, sbank, wbank, far)
    specs = [per_q(qt.shape), per_b(kc.shape), per_b(vct.shape), per_b(ks.shape), per_b(vst.shape),
             per_b(kwp.shape), per_b(vwt.shape), per_q(gt.shape), full(ovt.shape),
             full(ebank.shape), full(sbank.shape), full(wbank.shape), full(far.shape)]
    return pl.pallas_call(
        functools.partial(_attn_kernel, n_cmp_rows=n_cmp_rows, eb_shift=eb_shift),
        grid=(b, nq),
        in_specs=specs,
        out_specs=per_q(qt.shape),
        out_shape=jax.ShapeDtypeStruct(qt.shape, BF16),
        scratch_shapes=[pltpu.VMEM((N_KV, n_sb, GQ), F32),
                        pltpu.VMEM((N_KV, HEAD_DIM, GQ), F32)],
        compiler_params=_cparams(2),
        name="nsa_attention",
    )(*args)


def _ffn_kernel(x_ref, a_ref, c_ref, mod_ref, n2_ref, wo_ref, w1_ref, w2_ref, o_ref):
    aw = a_ref.shape[2]
    mix = jnp.dot(a_ref[0], wo_ref[0:aw, :], preferred_element_type=F32)
    mix = mix + jnp.dot(c_ref[0], wo_ref[aw:, :], preferred_element_type=F32)
    x1 = x_ref[0] + mod_ref[0, 2:3, :] * mix
    ms = jnp.mean(x1 * x1, axis=-1, keepdims=True)
    y = x1 * lax.rsqrt(ms + EPS) * n2_ref[...]
    h2 = (y * (1.0 + mod_ref[0, 4:5, :]) + mod_ref[0, 3:4, :]).astype(BF16)
    d_ff = w1_ref.shape[1]
    ff = jnp.zeros(x1.shape, F32)
    for j in range(d_ff // FF_CHUNK):
        a = jnp.dot(h2, w1_ref[:, j * FF_CHUNK:(j + 1) * FF_CHUNK], preferred_element_type=F32)
        a = jnp.maximum(a, 0.0)
        ff = ff + jnp.dot((a * a).astype(BF16), w2_ref[j * FF_CHUNK:(j + 1) * FF_CHUNK, :],
                          preferred_element_type=F32)
    o_ref[0] = x1 + mod_ref[0, 5:6, :] * ff


def _ffn_call(x, attn, conv, mod, norm2, w_out, w_ff1, w_ff2):
    b, t, d = x.shape
    tm = min(ROW_TILE, t)
    row_spec = lambda w: pl.BlockSpec((1, tm, w), lambda i, j: (i, j, 0))
    full = lambda shp: pl.BlockSpec(shp, lambda i, j: (0,) * len(shp),
                                    pipeline_mode=pl.Buffered(1))
    return pl.pallas_call(
        _ffn_kernel,
        grid=(b, t // tm),
        in_specs=[row_spec(d), row_spec(attn.shape[2]), row_spec(conv.shape[2]),
                  pl.BlockSpec((1, N_MOD, d), lambda i, j: (i, 0, 0)),
                  full((1, d)), full(w_out.shape), full(w_ff1.shape), full(w_ff2.shape)],
        out_specs=row_spec(d),
        out_shape=jax.ShapeDtypeStruct((b, t, d), F32),
        compiler_params=_cparams(2),
        name="outproj_mlp",
    )(x, attn, conv, mod, norm2, w_out, w_ff1, w_ff2)


def _block_diag_ones(n):
    idx = np.arange(n) // HEAD_DIM
    return jnp.asarray(idx[:, None] == idx[None, :], dtype=BF16)


def _pack_w_in(w_in):
    d = w_in.shape[0]
    conv_w = d - ATTN_W
    sizes = [ATTN_W] + [KV_W] * 6 + [3 * N_HEADS] + [conv_w] * 3
    offs = np.concatenate([[0], np.cumsum(sizes)])
    part = lambda i: w_in[:, offs[i]:offs[i + 1]]
    g_pad = jnp.pad(part(7), ((0, 0), (0, LANES - 3 * N_HEADS)))
    return jnp.concatenate([part(i) for i in range(7)] + [part(8), part(9), part(10), g_pad],
                           axis=1).astype(BF16)


def _expand_w1(w1):
    hid = w1.shape[1]
    w = w1.reshape(2, CMP_STRIDE, 1, HEAD_DIM, 1, hid)
    eye = jnp.eye(N_KV, dtype=w1.dtype).reshape(1, 1, N_KV, 1, N_KV, 1)
    return (w * eye).reshape(2, CMP_STRIDE * KV_W, N_KV * hid).astype(BF16)


def _expand_w2(w2):
    hid = w2.shape[0]
    eye = jnp.eye(N_KV, dtype=w2.dtype).reshape(N_KV, 1, N_KV, 1)
    return (w2.reshape(1, hid, 1, HEAD_DIM) * eye).reshape(N_KV * hid, KV_W).astype(BF16)


def _expand_pe(pe):
    p = pe.reshape(2, CMP_STRIDE, 1, HEAD_DIM)
    return jnp.broadcast_to(p, (2, CMP_STRIDE, N_KV, HEAD_DIM)).reshape(2, CMP_STRIDE * KV_W)


def _bucket_thresholds():
    n = np.arange(2 * REL_MAX_DIST)
    max_exact = REL_BUCKETS // 2
    nf = np.maximum(n, max_exact).astype(np.float32)
    ratio = np.log(nf / np.float32(max_exact)) / np.float32(math.log(REL_MAX_DIST / max_exact))
    large = max_exact + (ratio * np.float32(REL_BUCKETS - max_exact)).astype(np.int32)
    table = np.where(n < max_exact, n, np.minimum(large, REL_BUCKETS - 1))
    return tuple(int(np.searchsorted(table, k, side="left")) for k in range(REL_BUCKETS))


def _bank_kernel(rows_ref, o_ref, *, thresholds, dist_fn):
    shape = o_ref.shape[-2:]
    row = lax.broadcasted_iota(jnp.int32, shape, 0)
    qi = lax.broadcasted_iota(jnp.int32, shape, 1) & (Q_BLOCK - 1)
    dist, ok = dist_fn(row, qi)
    v = jnp.broadcast_to(rows_ref[0, 0:1, :], shape)
    for k in range(1, REL_BUCKETS):
        v = jnp.where(dist >= thresholds[k], rows_ref[0, k:k + 1, :], v)
    o_ref[...] = jnp.where(ok, v, NEG).reshape(o_ref.shape)


def _bank_call(bias_rows, lead, n_tiles, dist_fn, name):
    grid = tuple(lead) + (N_KV, n_tiles)
    nl = len(lead)
    blk = (1,) * (nl + 2) + (SEL_BLOCK, GQ)
    thr = _bucket_thresholds()

    def body(rows_ref, o_ref):
        ids = [pl.program_id(a) for a in range(nl + 2)]
        fn = lambda row, qi: dist_fn(ids[:nl], ids[nl + 1], row, qi)
        _bank_kernel(rows_ref, o_ref, thresholds=thr, dist_fn=fn)

    return pl.pallas_call(
        body,
        grid=grid,
        in_specs=[pl.BlockSpec((1, REL_BUCKETS, GQ), lambda *i: (i[nl], 0, 0))],
        out_specs=pl.BlockSpec(blk, lambda *i: tuple(i) + (0, 0)),
        out_shape=jax.ShapeDtypeStruct(tuple(lead) + (N_KV, n_tiles, SEL_BLOCK, GQ), F32),
        compiler_params=_cparams(nl + 2),
        name=name,
    )(bias_rows)


def _bias_banks(rel_bias, t):
    n_cmp_rows = t // CMP_STRIDE
    rows = rel_bias.reshape(REL_BUCKETS, N_KV, GQA).transpose(1, 0, 2)
    rows = jnp.repeat(rows, Q_BLOCK, axis=2)

    def sel_dist(lead, tile, row, qi):
        dist = SEL_BLOCK * tile + qi - row
        return dist, dist >= 0

    sbank = _bank_call(rows, (), NEAR_TILES + 1, sel_dist, "bias_bank_sel")
    far = sbank[:, NEAR_TILES, 0:1, :]

    def win_dist(lead, tile, row, qi):
        dist = SEL_BLOCK * tile + qi - row
        return dist, (dist >= 0) & (dist < WINDOW)

    wbank = _bank_call(rows, (), WIN_BLOCKS + 2, win_dist, "bias_bank_win")

    cmp_c = n_cmp_rows - 4
    shifts = tuple(int((-(cmp_c - 4 * p)) % 8) for p in range(2))
    n_tiles = (cmp_c + n_cmp_rows + 8 + SEL_BLOCK - 1) // SEL_BLOCK

    def cmp_dist(lead, tile, row, qi):
        e = SEL_BLOCK * tile + row - jnp.where(lead[0] == 0, shifts[0], shifts[1])
        dist = qi - CMP_STRIDE * e + (CMP_STRIDE * cmp_c - (CMP_BLOCK - 1))
        return dist, (dist >= 0) & (e >= 0)

    ebank = _bank_call(rows, (2,), n_tiles, cmp_dist, "bias_bank_cmp")
    ebank = ebank.reshape(2, N_KV, n_tiles * SEL_BLOCK, GQ)
    return ebank, sbank, wbank, far, shifts


def _overlap_t(t):
    n_cmp_rows = t // CMP_STRIDE
    n_sb = t // SEL_BLOCK
    c_start = np.arange(n_cmp_rows)[None, :] * CMP_STRIDE
    s_start = np.arange(n_sb)[:, None] * SEL_BLOCK
    ov = np.clip(np.minimum(c_start + CMP_BLOCK, s_start + SEL_BLOCK)
                 - np.maximum(c_start, s_start), 0, None) / CMP_BLOCK
    ov[:, n_cmp_rows - 1] = 0.0
    return jnp.asarray(ov, dtype=BF16)


def _layer(x, c_pad, w_in, q_norm, k_norm, cmp_pe_k, cmp_w1_k, cmp_w2_k, cmp_pe_v, cmp_w1_v,
           cmp_w2_v, rel_bias, conv_w, w_out, norm1, norm2, w_ada, b_ada, w_ff1, w_ff2):
    b, t, d = x.shape
    nq = t // Q_BLOCK
    scale = HEAD_DIM ** -0.5

    mod = _mod_call(c_pad, w_ada, b_ada)[:b].reshape(b, N_MOD, d)

    qn_t = (jnp.tile(q_norm, N_HEADS) * scale).reshape(1, ATTN_W)
    kn_t = jnp.tile(k_norm, N_KV).reshape(1, KV_W)
    bdq = _block_diag_ones(ATTN_W)
    bdk = _block_diag_ones(KV_W)
    q, kc_raw, vc_raw, ks, vs, kw, vw, gates, conv = _inproj_call(
        x, mod, norm1.reshape(1, d), _pack_w_in(w_in), qn_t, kn_t, conv_w, bdq, bdk)

    n_cmp_rows = t // CMP_STRIDE
    kc, vc = _compress_call(
        kc_raw.reshape(b, n_cmp_rows, CMP_STRIDE * KV_W), vc_raw.reshape(b, n_cmp_rows, CMP_STRIDE * KV_W),
        _expand_pe(cmp_pe_k), _expand_pe(cmp_pe_v), _expand_w1(cmp_w1_k), _expand_w1(cmp_w1_v),
        _expand_w2(cmp_w2_k), _expand_w2(cmp_w2_v), kn_t, bdk)

    def heads_t(a):
        return a.reshape(b, a.shape[1], N_KV, HEAD_DIM).transpose(0, 2, 1, 3)

    def tiles_t(a):
        n = a.shape[1]
        return a.reshape(b, n // V_TILE, V_TILE, N_KV, HEAD_DIM).transpose(0, 3, 1, 4, 2)

    qt = q.reshape(b, nq, Q_BLOCK, N_KV, GQA, HEAD_DIM).transpose(0, 1, 3, 5, 4, 2)
    qt = qt.reshape(b, nq, N_KV, HEAD_DIM, GQ)
    gt = gates[:, :, :3 * N_HEADS].reshape(b, nq, Q_BLOCK, N_KV, GQA, 3).transpose(0, 1, 5, 3, 4, 2)
    gt = gt.reshape(b, nq, 3, N_KV, GQ)
    pad = ((0, 0), (WINDOW, 0), (0, 0))
    vct = vc.reshape(b, n_cmp_rows, N_KV, HEAD_DIM).transpose(0, 2, 3, 1)
    ebank, sbank, wbank, far, eb_shift = _bias_banks(rel_bias, t)
    ot = _attn_call(qt, heads_t(kc), vct, heads_t(ks), tiles_t(vs), heads_t(jnp.pad(kw, pad)),
                    tiles_t(jnp.pad(vw, pad)), gt, _overlap_t(t), ebank, sbank, wbank, far, eb_shift)
    attn = ot.reshape(b, nq, N_KV, HEAD_DIM, GQA, Q_BLOCK).transpose(0, 1, 5, 2, 4, 3)
    attn = attn.reshape(b, t, ATTN_W)

    return _ffn_call(x, attn, conv, mod, norm2.reshape(1, d), w_out.astype(BF16),
                     w_ff1.astype(BF16), w_ff2.astype(BF16))


def kernel(x, c, w_in, q_norm, k_norm, cmp_pe_k, cmp_w1_k, cmp_w2_k, cmp_pe_v, cmp_w1_v, cmp_w2_v,
           rel_bias, conv_w, w_out, norm1, norm2, w_ada, b_ada, w_ff1, w_ff2):
    b = x.shape[0]
    c_pad = jnp.pad(c, ((0, (-b) % 8), (0, 0)))
    for l in range(w_in.shape[0]):
        x = _layer(x, c_pad, w_in[l], q_norm[l], k_norm[l], cmp_pe_k[l], cmp_w1_k[l], cmp_w2_k[l],
                   cmp_pe_v[l], cmp_w1_v[l], cmp_w2_v[l], rel_bias, conv_w[l], w_out[l],
                   norm1[l], norm2[l], w_ada[l], b_ada[l], w_ff1[l], w_ff2[l])
    return x
```

```python
import functools
import math

import numpy as np
import jax
import jax.numpy as jnp
from jax import lax
from jax.experimental import pallas as pl
from jax.experimental.pallas import tpu as pltpu

HEAD_DIM = 64
N_HEADS = 8
N_KV = 2
GQA = N_HEADS // N_KV
ATTN_W = N_HEADS * HEAD_DIM
KV_W = N_KV * HEAD_DIM
CONV_K = 3
CMP_BLOCK = 32
CMP_STRIDE = 16
CMP_HIDDEN = 256
SEL_BLOCK = 64
N_SELECT = 16
WINDOW = 512
Q_BLOCK = 64
SEL_FORCE = 1.0e6
REL_BUCKETS = 32
REL_MAX_DIST = 1024
N_MOD = 6
EPS = 1e-6
NEG = -1e30

LANES = 128
GQ = GQA * Q_BLOCK
KEY_CHUNK = 256
BLOCKS_PER_CHUNK = KEY_CHUNK // SEL_BLOCK
V_TILE = 128
WIN_BLOCKS = WINDOW // SEL_BLOCK
WIN_SPAN = (WIN_BLOCKS + 2) * SEL_BLOCK
NEAR_TILES = (REL_MAX_DIST + Q_BLOCK - 1) // SEL_BLOCK + 1
ROW_TILE = 512
FAR_CHUNKS = 4
QK_AHEAD = 4
LOG2E = math.log2(math.e)
FF_CHUNK = 1024
VMEM_LIMIT = 56 * 1024 * 1024

F32 = jnp.float32
BF16 = jnp.bfloat16


def _cparams(n_axes):
    return pltpu.CompilerParams(dimension_semantics=("arbitrary",) * n_axes,
                                vmem_limit_bytes=VMEM_LIMIT)


def _mod_kernel(c_ref, w_ref, b_ref, o_ref):
    c = c_ref[...]
    a = c * jax.nn.sigmoid(c)
    o_ref[...] = jnp.dot(a, w_ref[...], preferred_element_type=F32,
                         precision=lax.Precision.HIGHEST) + b_ref[...]


def _mod_call(c_pad, w_ada, b_ada):
    rows, d = c_pad.shape
    n = w_ada.shape[1]
    tn = 1024
    return pl.pallas_call(
        _mod_kernel,
        grid=(n // tn,),
        in_specs=[pl.BlockSpec((rows, d), lambda j: (0, 0)),
                  pl.BlockSpec((d, tn), lambda j: (0, j)),
                  pl.BlockSpec((1, tn), lambda j: (0, j))],
        out_specs=pl.BlockSpec((rows, tn), lambda j: (0, j)),
        out_shape=jax.ShapeDtypeStruct((rows, n), F32),
        compiler_params=_cparams(1),
        name="adaln_mod",
    )(c_pad, w_ada, b_ada.reshape(1, n))


_C_Q = 0
_C_KC = ATTN_W
_C_VC = _C_KC + KV_W
_C_KS = _C_VC + KV_W
_C_VS = _C_KS + KV_W
_C_KW = _C_VS + KV_W
_C_VW = _C_KW + KV_W
_C_CG = _C_VW + KV_W


def _inproj_kernel(x_ref, mod_ref, n1_ref, w_ref, qn_ref, kn_ref, cw_ref, bdq_ref, bdk_ref,
                   q_out, kc_out, vc_out, ks_out, vs_out, kw_out, vw_out, g_out, conv_out,
                   carry, *, conv_w):
    t = pl.program_id(1)
    tm = x_ref.shape[1]
    x = x_ref[0]
    ms = jnp.mean(x * x, axis=-1, keepdims=True)
    y = x * lax.rsqrt(ms + EPS) * n1_ref[...]
    h = (y * (1.0 + mod_ref[0, 1:2, :]) + mod_ref[0, 0:1, :]).astype(BF16)

    def proj(a, b):
        return jnp.dot(h, w_ref[:, a:b], preferred_element_type=F32)

    def head_norm(v, bd_ref, gain):
        ssq = jnp.dot((v * v).astype(BF16), bd_ref[...], preferred_element_type=F32)
        return v * lax.rsqrt(ssq * (1.0 / HEAD_DIM) + EPS) * gain

    q = proj(_C_Q, _C_KC)
    q_out[0] = head_norm(q, bdq_ref, qn_ref[...]).astype(BF16)
    kc_out[0] = proj(_C_KC, _C_VC)
    vc_out[0] = proj(_C_VC, _C_KS)
    ks_out[0] = head_norm(proj(_C_KS, _C_VS), bdk_ref, kn_ref[...]).astype(BF16)
    vs_out[0] = proj(_C_VS, _C_KW).astype(BF16)
    kw_out[0] = head_norm(proj(_C_KW, _C_VW), bdk_ref, kn_ref[...]).astype(BF16)
    vw_out[0] = proj(_C_VW, _C_CG).astype(BF16)

    c_bg = _C_CG + conv_w
    c_u = c_bg + conv_w
    c_g = c_u + conv_w
    z = proj(_C_CG, c_bg) * proj(c_u, c_g)
    g_out[0] = jax.nn.sigmoid(proj(c_g, c_g + LANES))

    @pl.when(t == 0)
    def _():
        carry[...] = jnp.zeros_like(carry)

    row = lax.broadcasted_iota(jnp.int32, z.shape, 0)
    p2 = carry[6:7, :]
    p1 = carry[7:8, :]
    z1 = jnp.where(row == 0, p1, pltpu.roll(z, 1, 0))
    z2 = jnp.where(row == 0, p2, jnp.where(row == 1, p1, pltpu.roll(z, 2, 0)))
    zc = cw_ref[0:1, :] * z2 + cw_ref[1:2, :] * z1 + cw_ref[2:3, :] * z
    conv_out[0] = (proj(c_bg, c_u) * zc).astype(BF16)
    carry[...] = z[tm - 8:tm, :]


def _inproj_call(x, mod, norm1, w_pack, qn_t, kn_t, conv_w, bdq, bdk):
    b, t, d = x.shape
    tm = min(ROW_TILE, t)
    cw = conv_w.shape[1]
    ncols = w_pack.shape[1]
    row_spec = lambda w: pl.BlockSpec((1, tm, w), lambda i, j: (i, j, 0))
    full = lambda shp: pl.BlockSpec(shp, lambda i, j: (0,) * len(shp))
    outs = [
        (ATTN_W, BF16), (KV_W, F32), (KV_W, F32), (KV_W, BF16), (KV_W, BF16), (KV_W, BF16),
        (KV_W, BF16), (LANES, F32), (cw, BF16)]
    return pl.pallas_call(
        functools.partial(_inproj_kernel, conv_w=cw),
        grid=(b, t // tm),
        in_specs=[row_spec(d),
                  pl.BlockSpec((1, N_MOD, d), lambda i, j: (i, 0, 0)),
                  full((1, d)), full((d, ncols)), full((1, ATTN_W)), full((1, KV_W)),
                  full((CONV_K, cw)), full((ATTN_W, ATTN_W)), full((KV_W, KV_W))],
        out_specs=[row_spec(w) for w, _ in outs],
        out_shape=[jax.ShapeDtypeStruct((b, t, w), dt) for w, dt in outs],
        scratch_shapes=[pltpu.VMEM((8, cw), F32)],
        compiler_params=_cparams(2),
        name="inproj",
    )(x, mod, norm1, w_pack, qn_t, kn_t, conv_w, bdq, bdk)


def _compress_kernel(kx_ref, vx_ref, pek_ref, pev_ref, w1k_ref, w1v_ref, w2k_ref, w2v_ref,
                     kn_ref, bdk_ref, kc_out, vc_out):
    def mlp(x_ref, pe_ref, w1_ref, w2_ref):
        xv = x_ref[0]
        u = jnp.dot((xv + pe_ref[0:1, :]).astype(BF16), w1_ref[0], preferred_element_type=F32)
        v = jnp.dot((xv + pe_ref[1:2, :]).astype(BF16), w1_ref[1], preferred_element_type=F32)
        n = u.shape[0]
        hid = u + pltpu.roll(v, n - 1, 0)
        act = jax.nn.gelu(hid, approximate=True).astype(BF16)
        return jnp.dot(act, w2_ref[...], preferred_element_type=F32)

    kc = mlp(kx_ref, pek_ref, w1k_ref, w2k_ref)
    ssq = jnp.dot((kc * kc).astype(BF16), bdk_ref[...], preferred_element_type=F32)
    kc_out[0] = (kc * lax.rsqrt(ssq * (1.0 / HEAD_DIM) + EPS) * kn_ref[...]).astype(BF16)
    vc_out[0] = mlp(vx_ref, pev_ref, w1v_ref, w2v_ref).astype(BF16)


def _compress_call(kx, vx, pek, pev, w1k, w1v, w2k, w2v, kn_t, bdk):
    b, nrow, wide = kx.shape
    hid2 = w1k.shape[2]
    full = lambda shp: pl.BlockSpec(shp, lambda i: (0,) * len(shp))
    xs = pl.BlockSpec((1, nrow, wide), lambda i: (i, 0, 0))
    os_ = pl.BlockSpec((1, nrow, KV_W), lambda i: (i, 0, 0))
    return pl.pallas_call(
        _compress_kernel,
        grid=(b,),
        in_specs=[xs, xs, full((2, wide)), full((2, wide)), full((2, wide, hid2)),
                  full((2, wide, hid2)), full((hid2, KV_W)), full((hid2, KV_W)),
                  full((1, KV_W)), full((KV_W, KV_W))],
        out_specs=[os_, os_],
        out_shape=[jax.ShapeDtypeStruct((b, nrow, KV_W), BF16)] * 2,
        compiler_params=_cparams(1),
        name="compress",
    )(kx, vx, pek, pev, w1k, w1v, w2k, w2v, kn_t, bdk)


def _attn_kernel(q_ref, kc_ref, vct_ref, ks_ref, vst_ref, kw_ref, vwt_ref, gt_ref, ovt_ref,
                 eb_ref, sb_ref, wb_ref, far_ref, o_ref, sel_scr, acc_scr, out_scr,
                 *, n_cmp_rows, eb_shift):
    ci = pl.program_id(1)
    par = lax.rem(ci, 2)
    n_sb = sel_scr.shape[1]

    cmp_c = n_cmp_rows - 4
    e0 = cmp_c - 4 * ci + jnp.where(par == 0, eb_shift[0], eb_shift[1])
    e0 = pl.multiple_of(e0, 8)
    o_cmp = []
    imp = []
    w0 = ci // 2
    s_cmp = [jnp.dot(kc_ref[0, g], q_ref[0, 0, g], preferred_element_type=F32) for g in range(N_KV)]
    s_win = [jnp.dot(kw_ref[0, g, pl.ds(pl.multiple_of(w0 * V_TILE, V_TILE), WIN_SPAN), :],
                     q_ref[0, 0, g], preferred_element_type=F32) for g in range(N_KV)]
    for g in range(N_KV):
        sc = s_cmp[g] + eb_ref[par, g, pl.ds(e0, n_cmp_rows), :]
        m = jnp.max(sc, axis=0, keepdims=True)
        e = jnp.exp2(sc - m)
        den = jnp.sum(e, axis=0, keepdims=True)
        inv = jnp.where(m > 0.5 * NEG, 1.0 / den, 0.0)
        pn = (e * inv).astype(BF16)
        o_cmp.append(jnp.dot(vct_ref[0, g], pn, preferred_element_type=F32))
        ir = jnp.dot(ovt_ref[...], pn, preferred_element_type=F32)
        a = ir[:, :LANES] + ir[:, LANES:]
        imp.append(a + pltpu.roll(a, Q_BLOCK, 1))

    lane = lax.broadcasted_iota(jnp.int32, (n_sb, LANES), 1)
    jidx = lax.broadcasted_iota(jnp.int32, (n_sb, LANES), 0)
    jf = jidx.astype(F32)
    valid = jidx <= ci
    forced = (jidx == 0) | (jidx == ci) | (jidx == ci - 1)
    score = jnp.where(valid, jnp.where(forced, SEL_FORCE, jnp.where(lane < Q_BLOCK, imp[0], imp[1])),
                      -1.0)
    sel = jnp.zeros((n_sb, LANES), F32)
    for _ in range(min(N_SELECT, n_sb)):
        mx = jnp.max(score, axis=0, keepdims=True)
        first = jnp.min(jnp.where(score == mx, jf, float(n_sb)), axis=0, keepdims=True)
        hit = jf == first
        sel = jnp.where(hit, 1.0, sel)
        score = jnp.where(hit, -2.0, score)
    selneg = jnp.where((sel > 0.5) & valid, 0.0, NEG)
    swapped = pltpu.roll(selneg, Q_BLOCK, 1)
    left = jnp.where(lane < Q_BLOCK, selneg, swapped)
    right = jnp.where(lane < Q_BLOCK, swapped, selneg)
    sel_scr[0] = jnp.concatenate([left, left], axis=1)
    sel_scr[1] = jnp.concatenate([right, right], axis=1)

    for g in range(N_KV):
        s = s_win[g]
        parts = []
        for i in range(WIN_SPAN // SEL_BLOCK):
            delta = par + WIN_BLOCKS - i
            ok = (delta >= 0) & (delta <= WIN_BLOCKS) & (delta <= ci)
            tile = jnp.where(ok, delta, WIN_BLOCKS + 1)
            parts.append(s[i * SEL_BLOCK:(i + 1) * SEL_BLOCK, :] + wb_ref[g, tile])
        s = jnp.concatenate(parts, axis=0)
        m = jnp.max(s, axis=0, keepdims=True)
        p = jnp.exp2(s - m)
        l_win = jnp.sum(p, axis=0, keepdims=True)
        pb = p.astype(BF16)
        o_win = jnp.zeros((HEAD_DIM, GQ), F32)
        for j in range(WIN_SPAN // V_TILE):
            o_win = o_win + jnp.dot(vwt_ref[0, g, w0 + j], pb[j * V_TILE:(j + 1) * V_TILE, :],
                                    preferred_element_type=F32)
        out_scr[g] = (gt_ref[0, 0, 0, g:g + 1, :] * o_cmp[g]
                      + (gt_ref[0, 0, 2, g:g + 1, :] * (1.0 / l_win)) * o_win)
        acc_scr[g] = jnp.zeros((HEAD_DIM, GQ), F32)

    n_chunks = ci // BLOCKS_PER_CHUNK + 1
    n_total = ks_ref.shape[2] // KEY_CHUNK
    n_far = jnp.maximum(ci - (NEAR_TILES - 1), 0) // (BLOCKS_PER_CHUNK * FAR_CHUNKS)

    def chunk_qk(g, c):
        kk = ks_ref[0, g, pl.ds(pl.multiple_of(c * KEY_CHUNK, KEY_CHUNK), KEY_CHUNK), :]
        return jnp.dot(kk, q_ref[0, 0, g], preferred_element_type=F32)

    def chunk_softmax(s, g, c, near, extra):
        parts = []
        for i in range(BLOCKS_PER_CHUNK):
            kb = c * BLOCKS_PER_CHUNK + i
            blk = s[i * SEL_BLOCK:(i + 1) * SEL_BLOCK, :]
            mrow = sel_scr[g, pl.ds(kb, 1), :]
            if extra is not None:
                mrow = mrow + extra
            if near:
                tile = jnp.clip(ci - kb, 0, NEAR_TILES)
                blk = blk + sb_ref[g, tile] + mrow
            else:
                blk = blk + (mrow + far_ref[g])
            parts.append(blk)
        s = jnp.concatenate(parts, axis=0)
        m_c = jnp.max(s, axis=0, keepdims=True)
        p = jnp.exp2(s - m_c)
        l_c = jnp.sum(p, axis=0, keepdims=True)
        return m_c, l_c, p.astype(BF16)

    def chunk_pv(pb, g, c):
        pv = jnp.zeros((HEAD_DIM, GQ), F32)
        for j in range(KEY_CHUNK // V_TILE):
            vt = vst_ref[0, g, c * (KEY_CHUNK // V_TILE) + j]
            pv = pv + jnp.dot(vt, pb[j * V_TILE:(j + 1) * V_TILE, :], preferred_element_type=F32)
        return pv

    def merge_step(chunks, near, carry):
        units = [(g, c, extra) for g in range(N_KV) for c, extra in chunks]
        scores = {k: chunk_qk(units[k][0], units[k][1]) for k in range(min(QK_AHEAD, len(units)))}
        results = []
        for k, (g, c, extra) in enumerate(units):
            m_c, l_c, pb = chunk_softmax(scores.pop(k), g, c, near, extra)
            if k + QK_AHEAD < len(units):
                scores[k + QK_AHEAD] = chunk_qk(units[k + QK_AHEAD][0], units[k + QK_AHEAD][1])
            results.append((m_c, l_c, chunk_pv(pb, g, c)))
        new = []
        for g in range(N_KV):
            m_run, l_run = carry[2 * g], carry[2 * g + 1]
            stats = results[g * len(chunks):(g + 1) * len(chunks)]
            m_new = m_run
            for m_c, _, _ in stats:
                m_new = jnp.maximum(m_new, m_c)
            alpha = jnp.exp2(m_run - m_new)
            l_new = alpha * l_run
            acc = alpha * acc_scr[g]
            for m_c, l_c, pv in stats:
                beta = jnp.exp2(m_c - m_new)
                l_new = l_new + beta * l_c
                acc = acc + beta * pv
            acc_scr[g] = acc
            new += [m_new, l_new]
        return tuple(new)

    def far_step(i, carry):
        return merge_step([(i * FAR_CHUNKS + j, None) for j in range(FAR_CHUNKS)], False, carry)

    c0 = n_far * FAR_CHUNKS

    def near_step(i, carry):
        ca = c0 + 2 * i
        cb = ca + 1
        extra = jnp.where(cb < n_chunks, 0.0, NEG)
        return merge_step([(ca, None), (jnp.minimum(cb, n_total - 1), extra)], True, carry)

    init = (jnp.full((1, GQ), NEG, F32), jnp.zeros((1, GQ), F32)) * N_KV
    carry = lax.fori_loop(0, n_far, far_step, init)
    carry = lax.fori_loop(0, (n_chunks - c0 + 1) // 2, near_step, carry)

    for g in range(N_KV):
        scale = gt_ref[0, 0, 1, g:g + 1, :] * (1.0 / carry[2 * g + 1])
        o_ref[0, 0, g] = (out_scr[g] + scale * acc_scr[g]).astype(BF16)


def _attn_call(qt, kc, vct, ks, vst, kwp, vwt, gt, ovt, ebank, sbank, wbank, far, eb_shift):
    b, nq = qt.shape[0], qt.shape[1]
    t = ks.shape[2]
    n_cmp_rows = kc.shape[2]
    n_sb = t // SEL_BLOCK
    per_b = lambda shp: pl.BlockSpec((1,) + shp[1:], lambda i, j: (i,) + (0,) * (len(shp) - 1))
    per_q = lambda shp: pl.BlockSpec((1, 1) + shp[2:], lambda i, j: (i, j) + (0,) * (len(shp) - 2))
    full = lambda shp: pl.BlockSpec(shp, lambda i, j: (0,) * len(shp))
    args = (qt, kc, vct, ks, vst, kwp, vwt, gt, ovt, ebank, sbank, wbank, far)
    specs = [per_q(qt.shape), per_b(kc.shape), per_b(vct.shape), per_b(ks.shape), per_b(vst.shape),
             per_b(kwp.shape), per_b(vwt.shape), per_q(gt.shape), full(ovt.shape),
             full(ebank.shape), full(sbank.shape), full(wbank.shape), full(far.shape)]
    return pl.pallas_call(
        functools.partial(_attn_kernel, n_cmp_rows=n_cmp_rows, eb_shift=eb_shift),
        grid=(b, nq),
        in_specs=specs,
        out_specs=per_q(qt.shape),
        out_shape=jax.ShapeDtypeStruct(qt.shape, BF16),
        scratch_shapes=[pltpu.VMEM((N_KV, n_sb, GQ), F32),
                        pltpu.VMEM((N_KV, HEAD_DIM, GQ), F32),
                        pltpu.VMEM((N_KV, HEAD_DIM, GQ), F32)],
        compiler_params=_cparams(2),
        name="nsa_attention",
    )(*args)


def _ffn_kernel(x_ref, a_ref, c_ref, mod_ref, n2_ref, wo_ref, w1_ref, w2_ref, o_ref):
    aw = a_ref.shape[2]
    mix = jnp.dot(a_ref[0], wo_ref[0:aw, :], preferred_element_type=F32)
    mix = mix + jnp.dot(c_ref[0], wo_ref[aw:, :], preferred_element_type=F32)
    x1 = x_ref[0] + mod_ref[0, 2:3, :] * mix
    ms = jnp.mean(x1 * x1, axis=-1, keepdims=True)
    y = x1 * lax.rsqrt(ms + EPS) * n2_ref[...]
    h2 = (y * (1.0 + mod_ref[0, 4:5, :]) + mod_ref[0, 3:4, :]).astype(BF16)
    d_ff = w1_ref.shape[1]
    ff = jnp.zeros(x1.shape, F32)
    for j in range(d_ff // FF_CHUNK):
        a = jnp.dot(h2, w1_ref[:, j * FF_CHUNK:(j + 1) * FF_CHUNK], preferred_element_type=F32)
        a = jnp.maximum(a, 0.0)
        ff = ff + jnp.dot((a * a).astype(BF16), w2_ref[j * FF_CHUNK:(j + 1) * FF_CHUNK, :],
                          preferred_element_type=F32)
    o_ref[0] = x1 + mod_ref[0, 5:6, :] * ff


def _ffn_call(x, attn, conv, mod, norm2, w_out, w_ff1, w_ff2):
    b, t, d = x.shape
    tm = min(ROW_TILE, t)
    row_spec = lambda w: pl.BlockSpec((1, tm, w), lambda i, j: (i, j, 0))
    full = lambda shp: pl.BlockSpec(shp, lambda i, j: (0,) * len(shp),
                                    pipeline_mode=pl.Buffered(1))
    return pl.pallas_call(
        _ffn_kernel,
        grid=(b, t // tm),
        in_specs=[row_spec(d), row_spec(attn.shape[2]), row_spec(conv.shape[2]),
                  pl.BlockSpec((1, N_MOD, d), lambda i, j: (i, 0, 0)),
                  full((1, d)), full(w_out.shape), full(w_ff1.shape), full(w_ff2.shape)],
        out_specs=row_spec(d),
        out_shape=jax.ShapeDtypeStruct((b, t, d), F32),
        compiler_params=_cparams(2),
        name="outproj_mlp",
    )(x, attn, conv, mod, norm2, w_out, w_ff1, w_ff2)


def _block_diag_ones(n):
    idx = np.arange(n) // HEAD_DIM
    return jnp.asarray(idx[:, None] == idx[None, :], dtype=BF16)


def _pack_w_in(w_in):
    d = w_in.shape[0]
    conv_w = d - ATTN_W
    sizes = [ATTN_W] + [KV_W] * 6 + [3 * N_HEADS] + [conv_w] * 3
    offs = np.concatenate([[0], np.cumsum(sizes)])
    part = lambda i: w_in[:, offs[i]:offs[i + 1]]
    g_pad = jnp.pad(part(7), ((0, 0), (0, LANES - 3 * N_HEADS)))
    return jnp.concatenate([part(i) for i in range(7)] + [part(8), part(9), part(10), g_pad],
                           axis=1).astype(BF16)


def _expand_w1(w1):
    hid = w1.shape[1]
    w = w1.reshape(2, CMP_STRIDE, 1, HEAD_DIM, 1, hid)
    eye = jnp.eye(N_KV, dtype=w1.dtype).reshape(1, 1, N_KV, 1, N_KV, 1)
    return (w * eye).reshape(2, CMP_STRIDE * KV_W, N_KV * hid).astype(BF16)


def _expand_w2(w2):
    hid = w2.shape[0]
    eye = jnp.eye(N_KV, dtype=w2.dtype).reshape(N_KV, 1, N_KV, 1)
    return (w2.reshape(1, hid, 1, HEAD_DIM) * eye).reshape(N_KV * hid, KV_W).astype(BF16)


def _expand_pe(pe):
    p = pe.reshape(2, CMP_STRIDE, 1, HEAD_DIM)
    return jnp.broadcast_to(p, (2, CMP_STRIDE, N_KV, HEAD_DIM)).reshape(2, CMP_STRIDE * KV_W)


def _bucket_thresholds():
    n = np.arange(2 * REL_MAX_DIST)
    max_exact = REL_BUCKETS // 2
    nf = np.maximum(n, max_exact).astype(np.float32)
    ratio = np.log(nf / np.float32(max_exact)) / np.float32(math.log(REL_MAX_DIST / max_exact))
    large = max_exact + (ratio * np.float32(REL_BUCKETS - max_exact)).astype(np.int32)
    table = np.where(n < max_exact, n, np.minimum(large, REL_BUCKETS - 1))
    return tuple(int(np.searchsorted(table, k, side="left")) for k in range(REL_BUCKETS))


def _bank_kernel(rows_ref, o_ref, *, thresholds, dist_fn):
    shape = o_ref.shape[-2:]
    row = lax.broadcasted_iota(jnp.int32, shape, 0)
    qi = lax.broadcasted_iota(jnp.int32, shape, 1) & (Q_BLOCK - 1)
    dist, ok = dist_fn(row, qi)
    v = jnp.broadcast_to(rows_ref[0, 0:1, :], shape)
    for k in range(1, REL_BUCKETS):
        v = jnp.where(dist >= thresholds[k], rows_ref[0, k:k + 1, :], v)
    o_ref[...] = jnp.where(ok, v, NEG).reshape(o_ref.shape)


def _bank_call(bias_rows, lead, n_tiles, dist_fn, name):
    grid = tuple(lead) + (N_KV, n_tiles)
    nl = len(lead)
    blk = (1,) * (nl + 2) + (SEL_BLOCK, GQ)
    thr = _bucket_thresholds()

    def body(rows_ref, o_ref):
        ids = [pl.program_id(a) for a in range(nl + 2)]
        fn = lambda row, qi: dist_fn(ids[:nl], ids[nl + 1], row, qi)
        _bank_kernel(rows_ref, o_ref, thresholds=thr, dist_fn=fn)

    return pl.pallas_call(
        body,
        grid=grid,
        in_specs=[pl.BlockSpec((1, REL_BUCKETS, GQ), lambda *i: (i[nl], 0, 0))],
        out_specs=pl.BlockSpec(blk, lambda *i: tuple(i) + (0, 0)),
        out_shape=jax.ShapeDtypeStruct(tuple(lead) + (N_KV, n_tiles, SEL_BLOCK, GQ), F32),
        compiler_params=_cparams(nl + 2),
        name=name,
    )(bias_rows)


def _bias_banks(rel_bias, t):
    n_cmp_rows = t // CMP_STRIDE
    rows = rel_bias.reshape(REL_BUCKETS, N_KV, GQA).transpose(1, 0, 2)
    rows = jnp.repeat(rows, Q_BLOCK, axis=2) * LOG2E

    def sel_dist(lead, tile, row, qi):
        dist = SEL_BLOCK * tile + qi - row
        return dist, dist >= 0

    sbank = _bank_call(rows, (), NEAR_TILES + 1, sel_dist, "bias_bank_sel")
    far = sbank[:, NEAR_TILES, 0:1, :]

    def win_dist(lead, tile, row, qi):
        dist = SEL_BLOCK * tile + qi - row
        return dist, (dist >= 0) & (dist < WINDOW)

    wbank = _bank_call(rows, (), WIN_BLOCKS + 2, win_dist, "bias_bank_win")

    cmp_c = n_cmp_rows - 4
    shifts = tuple(int((-(cmp_c - 4 * p)) % 8) for p in range(2))
    n_tiles = (cmp_c + n_cmp_rows + 8 + SEL_BLOCK - 1) // SEL_BLOCK

    def cmp_dist(lead, tile, row, qi):
        e = SEL_BLOCK * tile + row - jnp.where(lead[0] == 0, shifts[0], shifts[1])
        dist = qi - CMP_STRIDE * e + (CMP_STRIDE * cmp_c - (CMP_BLOCK - 1))
        return dist, (dist >= 0) & (e >= 0)

    ebank = _bank_call(rows, (2,), n_tiles, cmp_dist, "bias_bank_cmp")
    ebank = ebank.reshape(2, N_KV, n_tiles * SEL_BLOCK, GQ)
    return ebank, sbank, wbank, far, shifts


def _overlap_t(t):
    n_cmp_rows = t // CMP_STRIDE
    n_sb = t // SEL_BLOCK
    c_start = np.arange(n_cmp_rows)[None, :] * CMP_STRIDE
    s_start = np.arange(n_sb)[:, None] * SEL_BLOCK
    ov = np.clip(np.minimum(c_start + CMP_BLOCK, s_start + SEL_BLOCK)
                 - np.maximum(c_start, s_start), 0, None) / CMP_BLOCK
    ov[:, n_cmp_rows - 1] = 0.0
    return jnp.asarray(ov, dtype=BF16)


def _layer(x, c_pad, w_in, q_norm, k_norm, cmp_pe_k, cmp_w1_k, cmp_w2_k, cmp_pe_v, cmp_w1_v,
           cmp_w2_v, rel_bias, conv_w, w_out, norm1, norm2, w_ada, b_ada, w_ff1, w_ff2):
    b, t, d = x.shape
    nq = t // Q_BLOCK
    scale = HEAD_DIM ** -0.5

    mod = _mod_call(c_pad, w_ada, b_ada)[:b].reshape(b, N_MOD, d)

    qn_t = (jnp.tile(q_norm, N_HEADS) * (scale * LOG2E)).reshape(1, ATTN_W)
    kn_t = jnp.tile(k_norm, N_KV).reshape(1, KV_W)
    bdq = _block_diag_ones(ATTN_W)
    bdk = _block_diag_ones(KV_W)
    q, kc_raw, vc_raw, ks, vs, kw, vw, gates, conv = _inproj_call(
        x, mod, norm1.reshape(1, d), _pack_w_in(w_in), qn_t, kn_t, conv_w, bdq, bdk)

    n_cmp_rows = t // CMP_STRIDE
    kc, vc = _compress_call(
        kc_raw.reshape(b, n_cmp_rows, CMP_STRIDE * KV_W), vc_raw.reshape(b, n_cmp_rows, CMP_STRIDE * KV_W),
        _expand_pe(cmp_pe_k), _expand_pe(cmp_pe_v), _expand_w1(cmp_w1_k), _expand_w1(cmp_w1_v),
        _expand_w2(cmp_w2_k), _expand_w2(cmp_w2_v), kn_t, bdk)

    def heads_t(a):
        return a.reshape(b, a.shape[1], N_KV, HEAD_DIM).transpose(0, 2, 1, 3)

    def tiles_t(a):
        n = a.shape[1]
        return a.reshape(b, n // V_TILE, V_TILE, N_KV, HEAD_DIM).transpose(0, 3, 1, 4, 2)

    qt = q.reshape(b, nq, Q_BLOCK, N_KV, GQA, HEAD_DIM).transpose(0, 1, 3, 5, 4, 2)
    qt = qt.reshape(b, nq, N_KV, HEAD_DIM, GQ)
    gt = gates[:, :, :3 * N_HEADS].reshape(b, nq, Q_BLOCK, N_KV, GQA, 3).transpose(0, 1, 5, 3, 4, 2)
    gt = gt.reshape(b, nq, 3, N_KV, GQ)
    pad = ((0, 0), (WINDOW, 0), (0, 0))
    vct = vc.reshape(b, n_cmp_rows, N_KV, HEAD_DIM).transpose(0, 2, 3, 1)
    ebank, sbank, wbank, far, eb_shift = _bias_banks(rel_bias, t)
    ot = _attn_call(qt, heads_t(kc), vct, heads_t(ks), tiles_t(vs), heads_t(jnp.pad(kw, pad)),
                    tiles_t(jnp.pad(vw, pad)), gt, _overlap_t(t), ebank, sbank, wbank, far, eb_shift)
    attn = ot.reshape(b, nq, N_KV, HEAD_DIM, GQA, Q_BLOCK).transpose(0, 1, 5, 2, 4, 3)
    attn = attn.reshape(b, t, ATTN_W)

    return _ffn_call(x, attn, conv, mod, norm2.reshape(1, d), w_out.astype(BF16),
                     w_ff1.astype(BF16), w_ff2.astype(BF16))


def kernel(x, c, w_in, q_norm, k_norm, cmp_pe_k, cmp_w1_k, cmp_w2_k, cmp_pe_v, cmp_w1_v, cmp_w2_v,
           rel_bias, conv_w, w_out, norm1, norm2, w_ada, b_ada, w_ff1, w_ff2):
    b = x.shape[0]
    c_pad = jnp.pad(c, ((0, (-b) % 8), (0, 0)))
    for l in range(w_in.shape[0]):
        x = _layer(x, c_pad, w_in[l], q_norm[l], k_norm[l], cmp_pe_k[l], cmp_w1_k[l], cmp_w2_k[l],
                   cmp_pe_v[l], cmp_w1_v[l], cmp_w2_v[l], rel_bias, conv_w[l], w_out[l],
                   norm1[l], norm2[l], w_ada[l], b_ada[l], w_ff1[l], w_ff2[l])
    return x
```

```python
import functools
import math

import numpy as np
import jax
import jax.numpy as jnp
from jax import lax
from jax.experimental import pallas as pl
from jax.experimental.pallas import tpu as pltpu

HEAD_DIM = 64
N_HEADS = 8
N_KV = 2
GQA = N_HEADS // N_KV
ATTN_W = N_HEADS * HEAD_DIM
KV_W = N_KV * HEAD_DIM
CONV_K = 3
CMP_BLOCK = 32
CMP_STRIDE = 16
CMP_HIDDEN = 256
SEL_BLOCK = 64
N_SELECT = 16
WINDOW = 512
Q_BLOCK = 64
SEL_FORCE = 1.0e6
REL_BUCKETS = 32
REL_MAX_DIST = 1024
N_MOD = 6
N_BRANCH = 3
EPS = 1e-6
NEG = -1e30

LANES = 128
SUBLANES = 8
GQ = GQA * Q_BLOCK
KEY_CHUNK = 256
BLOCKS_PER_CHUNK = KEY_CHUNK // SEL_BLOCK
V_TILE = 128
WIN_BLOCKS = WINDOW // SEL_BLOCK
WIN_TILES = WINDOW // V_TILE + 1
NEAR_TILES = (REL_MAX_DIST + Q_BLOCK - 1) // SEL_BLOCK + 1
ROW_TILE = 512
FAR_CHUNKS = 4
NEAR_CHUNKS = 2
QK_AHEAD = 4
LOG2E = math.log2(math.e)
FF_CHUNK = 1024
VMEM_LIMIT = 56 * 1024 * 1024

F32 = jnp.float32
BF16 = jnp.bfloat16
_NT = (((1,), (1,)), ((), ()))


def _cparams(n_axes):
    return pltpu.CompilerParams(dimension_semantics=("arbitrary",) * n_axes,
                                vmem_limit_bytes=VMEM_LIMIT)


def _swap_halves(p0, p1):
    low = lax.broadcasted_iota(jnp.int32, p0.shape, 1) < LANES // 2
    return (jnp.where(low, p0, pltpu.roll(p1, LANES // 2, 1)),
            jnp.where(low, pltpu.roll(p0, LANES // 2, 1), p1))


def _mod_kernel(c_ref, w_ref, b_ref, o_ref):
    c = c_ref[...]
    a = c * jax.nn.sigmoid(c)
    o_ref[...] = jnp.dot(a, w_ref[...], preferred_element_type=F32,
                         precision=lax.Precision.HIGHEST) + b_ref[...]


def _mod_call(c_pad, w_ada, b_ada):
    rows, d = c_pad.shape
    n = w_ada.shape[1]
    tn = 1024
    return pl.pallas_call(
        _mod_kernel,
        grid=(n // tn,),
        in_specs=[pl.BlockSpec((rows, d), lambda j: (0, 0)),
                  pl.BlockSpec((d, tn), lambda j: (0, j)),
                  pl.BlockSpec((1, tn), lambda j: (0, j))],
        out_specs=pl.BlockSpec((rows, tn), lambda j: (0, j)),
        out_shape=jax.ShapeDtypeStruct((rows, n), F32),
        compiler_params=_cparams(1),
        name="adaln_mod",
    )(c_pad, w_ada, b_ada.reshape(1, n))


_N_KC, _N_VC, _N_KS, _N_KW, _N_CONV = 0, KV_W, 2 * KV_W, 3 * KV_W, 4 * KV_W
_T_Q, _T_VS, _T_VW, _T_G = 0, ATTN_W, ATTN_W + KV_W, ATTN_W + 2 * KV_W
_G_ROWS = GQA * SUBLANES


def _inproj_kernel(x_ref, mod_ref, n1_ref, wn_ref, wt_ref, qn_ref, kn_ref, cw_ref, bdq_ref, bdk_ref,
                   qt_out, kc_out, vc_out, ks_out, vst_out, kw_out, vwt_out, gt_out, conv_out,
                   carry, *, conv_w):
    t = pl.program_id(1)
    tm = x_ref.shape[1]
    x = x_ref[0]
    ms = jnp.mean(x * x, axis=-1, keepdims=True)
    y = x * lax.rsqrt(ms + EPS) * n1_ref[...]
    h = (y * (1.0 + mod_ref[0, 1:2, :]) + mod_ref[0, 0:1, :]).astype(BF16)

    def proj(a, b):
        return jnp.dot(h, wn_ref[:, a:b], preferred_element_type=F32)

    def proj_t(a, b):
        return lax.dot_general(wt_ref[a:b, :], h, _NT, preferred_element_type=F32)

    def head_norm(v, gain):
        ssq = jnp.dot((v * v).astype(BF16), bdk_ref[...], preferred_element_type=F32)
        return v * lax.rsqrt(ssq * (1.0 / HEAD_DIM) + EPS) * gain

    qf = proj_t(_T_Q, _T_VS)
    ssq = jnp.dot(bdq_ref[...], (qf * qf).astype(BF16), preferred_element_type=F32)
    qf = qf * lax.rsqrt(ssq * (1.0 / HEAD_DIM) + EPS) * qn_ref[...]
    for c in range(tm // LANES):
        for g in range(N_KV):
            pc = [qf[(g * GQA + r) * HEAD_DIM:(g * GQA + r + 1) * HEAD_DIM, c * LANES:(c + 1) * LANES]
                  for r in range(GQA)]
            lo01, hi01 = _swap_halves(pc[0], pc[1])
            lo23, hi23 = _swap_halves(pc[2], pc[3])
            qt_out[0, 2 * c, g] = jnp.concatenate([lo01, lo23], axis=1).astype(BF16)
            qt_out[0, 2 * c + 1, g] = jnp.concatenate([hi01, hi23], axis=1).astype(BF16)

    gf = jax.nn.sigmoid(proj_t(_T_G, _T_G + _G_ROWS))
    for c in range(tm // LANES):
        pc = [gf[r * SUBLANES:(r + 1) * SUBLANES, c * LANES:(c + 1) * LANES] for r in range(GQA)]
        lo01, hi01 = _swap_halves(pc[0], pc[1])
        lo23, hi23 = _swap_halves(pc[2], pc[3])
        gt_out[0, 2 * c] = jnp.concatenate([lo01, lo23], axis=1)
        gt_out[0, 2 * c + 1] = jnp.concatenate([hi01, hi23], axis=1)

    vs_f = proj_t(_T_VS, _T_VW).astype(BF16)
    vw_f = proj_t(_T_VW, _T_G).astype(BF16)
    for j in range(tm // V_TILE):
        vst_out[0, j] = vs_f[:, j * V_TILE:(j + 1) * V_TILE]
        vwt_out[0, j] = vw_f[:, j * V_TILE:(j + 1) * V_TILE]

    kc_out[0] = proj(_N_KC, _N_VC)
    vc_out[0] = proj(_N_VC, _N_KS)
    ks_out[0] = head_norm(proj(_N_KS, _N_KW), kn_ref[...]).astype(BF16)
    kw_out[0] = head_norm(proj(_N_KW, _N_CONV), kn_ref[...]).astype(BF16)

    c_bg = _N_CONV + conv_w
    c_u = c_bg + conv_w
    z = proj(_N_CONV, c_bg) * proj(c_u, c_u + conv_w)

    @pl.when(t == 0)
    def _():
        carry[...] = jnp.zeros_like(carry)

    row = lax.broadcasted_iota(jnp.int32, z.shape, 0)
    p2 = carry[6:7, :]
    p1 = carry[7:8, :]
    z1 = jnp.where(row == 0, p1, pltpu.roll(z, 1, 0))
    z2 = jnp.where(row == 0, p2, jnp.where(row == 1, p1, pltpu.roll(z, 2, 0)))
    zc = cw_ref[0:1, :] * z2 + cw_ref[1:2, :] * z1 + cw_ref[2:3, :] * z
    conv_out[0] = (proj(c_bg, c_u) * zc).astype(BF16)
    carry[...] = z[tm - SUBLANES:tm, :]


def _inproj_call(x, mod, norm1, w_nat, w_tr, qn_col, kn_t, conv_w, bdq, bdk):
    b, t, d = x.shape
    tm = min(ROW_TILE, t)
    cw = conv_w.shape[1]
    nq, nv = t // Q_BLOCK, t // V_TILE
    row_spec = lambda w: pl.BlockSpec((1, tm, w), lambda i, j: (i, j, 0))
    full = lambda shp: pl.BlockSpec(shp, lambda i, j: (0,) * len(shp))
    vt_spec = pl.BlockSpec((1, tm // V_TILE, KV_W, V_TILE), lambda i, j: (i, j, 0, 0))
    vt_shape = jax.ShapeDtypeStruct((b, nv, KV_W, V_TILE), BF16)
    kv = lambda dt: jax.ShapeDtypeStruct((b, t, KV_W), dt)
    out_specs = [pl.BlockSpec((1, tm // Q_BLOCK, N_KV, HEAD_DIM, GQ), lambda i, j: (i, j, 0, 0, 0)),
                 row_spec(KV_W), row_spec(KV_W), row_spec(KV_W), vt_spec, row_spec(KV_W), vt_spec,
                 pl.BlockSpec((1, tm // Q_BLOCK, SUBLANES, GQ), lambda i, j: (i, j, 0, 0)),
                 row_spec(cw)]
    out_shape = [jax.ShapeDtypeStruct((b, nq, N_KV, HEAD_DIM, GQ), BF16),
                 kv(F32), kv(F32), kv(BF16), vt_shape, kv(BF16), vt_shape,
                 jax.ShapeDtypeStruct((b, nq, SUBLANES, GQ), F32),
                 jax.ShapeDtypeStruct((b, t, cw), BF16)]
    return pl.pallas_call(
        functools.partial(_inproj_kernel, conv_w=cw),
        grid=(b, t // tm),
        in_specs=[row_spec(d),
                  pl.BlockSpec((1, N_MOD, d), lambda i, j: (i, 0, 0)),
                  full((1, d)), full(w_nat.shape), full(w_tr.shape), full((ATTN_W, 1)),
                  full((1, KV_W)), full((CONV_K, cw)), full((ATTN_W, ATTN_W)), full((KV_W, KV_W))],
        out_specs=out_specs,
        out_shape=out_shape,
        scratch_shapes=[pltpu.VMEM((SUBLANES, cw), F32)],
        compiler_params=_cparams(2),
        name="inproj",
    )(x, mod, norm1, w_nat, w_tr, qn_col, kn_t, conv_w, bdq, bdk)


def _compress_kernel(kx_ref, vx_ref, pek_ref, pev_ref, w1k_ref, w1v_ref, w2k_ref, w2vt_ref,
                     kn_ref, bdk_ref, kc_out, vct_out):
    def hidden(x_ref, pe_ref, w1_ref):
        xv = x_ref[0]
        u = jnp.dot((xv + pe_ref[0:1, :]).astype(BF16), w1_ref[0], preferred_element_type=F32)
        v = jnp.dot((xv + pe_ref[1:2, :]).astype(BF16), w1_ref[1], preferred_element_type=F32)
        n = u.shape[0]
        hid = u + pltpu.roll(v, n - 1, 0)
        return jax.nn.gelu(hid, approximate=True).astype(BF16)

    kc = jnp.dot(hidden(kx_ref, pek_ref, w1k_ref), w2k_ref[...], preferred_element_type=F32)
    ssq = jnp.dot((kc * kc).astype(BF16), bdk_ref[...], preferred_element_type=F32)
    kc_out[0] = (kc * lax.rsqrt(ssq * (1.0 / HEAD_DIM) + EPS) * kn_ref[...]).astype(BF16)
    vct_out[0] = lax.dot_general(w2vt_ref[...], hidden(vx_ref, pev_ref, w1v_ref), _NT,
                                 preferred_element_type=F32).astype(BF16)


def _compress_call(kx, vx, pek, pev, w1k, w1v, w2k, w2vt, kn_t, bdk):
    b, nrow, wide = kx.shape
    hid2 = w1k.shape[2]
    full = lambda shp: pl.BlockSpec(shp, lambda i: (0,) * len(shp))
    xs = pl.BlockSpec((1, nrow, wide), lambda i: (i, 0, 0))
    return pl.pallas_call(
        _compress_kernel,
        grid=(b,),
        in_specs=[xs, xs, full((2, wide)), full((2, wide)), full((2, wide, hid2)),
                  full((2, wide, hid2)), full((hid2, KV_W)), full((KV_W, hid2)),
                  full((1, KV_W)), full((KV_W, KV_W))],
        out_specs=[pl.BlockSpec((1, nrow, KV_W), lambda i: (i, 0, 0)),
                   pl.BlockSpec((1, KV_W, nrow), lambda i: (i, 0, 0))],
        out_shape=[jax.ShapeDtypeStruct((b, nrow, KV_W), BF16),
                   jax.ShapeDtypeStruct((b, KV_W, nrow), BF16)],
        compiler_params=_cparams(1),
        name="compress",
    )(kx, vx, pek, pev, w1k, w1v, w2k, w2vt, kn_t, bdk)


def _attn_kernel(q_ref, kc_ref, vct_ref, ks_ref, vst_ref, kw_ref, vwt_ref, gt_ref, ovt_ref,
                 eb_ref, sb_ref, wb_ref, far_ref, o_ref, sel_scr, acc_scr, out_scr, qp_scr, s_scr,
                 *, n_cmp_rows, eb_shift):
    ci = pl.program_id(1)
    par = lax.rem(ci, 2)
    n_sb = sel_scr.shape[1]
    n_total = ks_ref.shape[1] // KEY_CHUNK
    gsl = lambda g: slice(g * HEAD_DIM, (g + 1) * HEAD_DIM)

    zeros_q = jnp.zeros((HEAD_DIM, GQ), BF16)
    qp_scr[0] = jnp.concatenate([q_ref[0, 0, 0], zeros_q], axis=0)
    qp_scr[1] = jnp.concatenate([zeros_q, q_ref[0, 0, 1]], axis=0)

    def chunk_qk(g, c):
        kk = ks_ref[0, pl.ds(pl.multiple_of(c * KEY_CHUNK, KEY_CHUNK), KEY_CHUNK), :]
        return jnp.dot(kk, qp_scr[g], preferred_element_type=F32)

    n_chunks = ci // BLOCKS_PER_CHUNK + 1
    n_far = jnp.maximum(ci - (NEAR_TILES - 1), 0) // (BLOCKS_PER_CHUNK * FAR_CHUNKS)
    c0 = n_far * FAR_CHUNKS
    n_near = (n_chunks - c0 + NEAR_CHUNKS - 1) // NEAR_CHUNKS

    def near_units(i):
        return [(g, jnp.minimum(c0 + NEAR_CHUNKS * i + j, n_total - 1))
                for g in range(N_KV) for j in range(NEAR_CHUNKS)]

    def far_units(i):
        return [(g, i * FAR_CHUNKS + j) for g in range(N_KV) for j in range(FAR_CHUNKS)]

    def prefetch_scores(units):
        for k in range(QK_AHEAD):
            s_scr[k] = chunk_qk(*units[k])

    cmp_c = n_cmp_rows - 4
    e0 = cmp_c - 4 * ci + jnp.where(par == 0, eb_shift[0], eb_shift[1])
    e0 = pl.multiple_of(e0, 8)
    o_cmp = []
    imp = []
    w0 = ci // 2 - (WIN_TILES - 1)
    win_tiles = [jnp.maximum(w0 + j, 0) for j in range(WIN_TILES)]
    s_cmp = [jnp.dot(kc_ref[0], qp_scr[g], preferred_element_type=F32) for g in range(N_KV)]
    for g in range(N_KV):
        sc = s_cmp[g] + eb_ref[par, g, pl.ds(e0, n_cmp_rows), :]
        m = jnp.max(sc, axis=0, keepdims=True)
        e = jnp.exp2(sc - m)
        den = jnp.sum(e, axis=0, keepdims=True)
        inv = jnp.where(m > 0.5 * NEG, 1.0 / den, 0.0)
        pn = (e * inv).astype(BF16)
        o_cmp.append(jnp.dot(vct_ref[0, gsl(g), :], pn, preferred_element_type=F32))
        ir = jnp.dot(ovt_ref[...], pn, preferred_element_type=F32)
        a = ir[:, :LANES] + ir[:, LANES:]
        imp.append(a + pltpu.roll(a, Q_BLOCK, 1))

    s_win = [[jnp.dot(kw_ref[0, pl.ds(pl.multiple_of(tj * V_TILE, V_TILE), V_TILE), :], qp_scr[g],
                      preferred_element_type=F32) for tj in win_tiles] for g in range(N_KV)]

    first_near = near_units(0)

    def next_units(use_far, far_list):
        return [(jnp.where(use_far, g_f, g_n), jnp.where(use_far, jnp.minimum(c_f, n_total - 1), c_n))
                for (g_f, c_f), (g_n, c_n) in zip(_far_first(far_list), first_near)]

    prefetch_scores(next_units(n_far > 0, far_units(0)))

    lane = lax.broadcasted_iota(jnp.int32, (n_sb, LANES), 1)
    jidx = lax.broadcasted_iota(jnp.int32, (n_sb, LANES), 0)
    jf = jidx.astype(F32)
    valid = jidx <= ci
    forced = (jidx == 0) | (jidx == ci) | (jidx == ci - 1)
    score = jnp.where(valid, jnp.where(forced, SEL_FORCE, jnp.where(lane < Q_BLOCK, imp[0], imp[1])),
                      -1.0)
    sel = jnp.zeros((n_sb, LANES), F32)
    for _ in range(min(N_SELECT, n_sb)):
        mx = jnp.max(score, axis=0, keepdims=True)
        first = jnp.min(jnp.where(score == mx, jf, float(n_sb)), axis=0, keepdims=True)
        hit = jf == first
        sel = jnp.where(hit, 1.0, sel)
        score = jnp.where(hit, -2.0, score)
    selneg = jnp.where((sel > 0.5) & valid, 0.0, NEG)
    swapped = pltpu.roll(selneg, Q_BLOCK, 1)
    left = jnp.where(lane < Q_BLOCK, selneg, swapped)
    right = jnp.where(lane < Q_BLOCK, swapped, selneg)
    sel_scr[0] = jnp.concatenate([left, left], axis=1)
    sel_scr[1] = jnp.concatenate([right, right], axis=1)

    for g in range(N_KV):
        parts = []
        for i in range(2 * WIN_TILES):
            delta = par + WIN_BLOCKS - i
            ok = (delta >= 0) & (delta <= WIN_BLOCKS) & (delta <= ci)
            tile = jnp.where(ok, delta, WIN_BLOCKS + 1)
            half = s_win[g][i // 2][(i % 2) * SEL_BLOCK:(i % 2 + 1) * SEL_BLOCK, :]
            parts.append(half + wb_ref[g, tile])
        s = jnp.concatenate(parts, axis=0)
        m = jnp.max(s, axis=0, keepdims=True)
        p = jnp.exp2(s - m)
        l_win = jnp.sum(p, axis=0, keepdims=True)
        pb = p.astype(BF16)
        o_win = jnp.zeros((HEAD_DIM, GQ), F32)
        for j, tj in enumerate(win_tiles):
            o_win = o_win + jnp.dot(vwt_ref[0, tj, gsl(g), :], pb[j * V_TILE:(j + 1) * V_TILE, :],
                                    preferred_element_type=F32)
        out_scr[g] = (gt_ref[0, 0, g:g + 1, :] * o_cmp[g]
                      + (gt_ref[0, 0, 2 * N_KV + g:2 * N_KV + g + 1, :] * (1.0 / l_win)) * o_win)
        acc_scr[g] = jnp.zeros((HEAD_DIM, GQ), F32)

    def chunk_softmax(s, g, c, near, extra):
        parts = []
        for i in range(BLOCKS_PER_CHUNK):
            kb = c * BLOCKS_PER_CHUNK + i
            blk = s[i * SEL_BLOCK:(i + 1) * SEL_BLOCK, :]
            mrow = sel_scr[g, pl.ds(kb, 1), :]
            if extra is not None:
                mrow = mrow + extra
            if near:
                tile = jnp.clip(ci - kb, 0, NEAR_TILES)
                blk = blk + sb_ref[g, tile] + mrow
            else:
                blk = blk + (mrow + far_ref[g])
            parts.append(blk)
        s = jnp.concatenate(parts, axis=0)
        m_c = jnp.max(s, axis=0, keepdims=True)
        p = jnp.exp2(s - m_c)
        l_c = jnp.sum(p, axis=0, keepdims=True)
        return m_c, l_c, p.astype(BF16)

    def chunk_pv(pb, g, c):
        pv = jnp.zeros((HEAD_DIM, GQ), F32)
        for j in range(KEY_CHUNK // V_TILE):
            vt = vst_ref[0, c * (KEY_CHUNK // V_TILE) + j, gsl(g), :]
            pv = pv + jnp.dot(vt, pb[j * V_TILE:(j + 1) * V_TILE, :], preferred_element_type=F32)
        return pv

    def merge_step(units, extras, near, next_units, carry):
        scores = {k: s_scr[k] for k in range(QK_AHEAD)}
        results = []
        for k, (g, c) in enumerate(units):
            m_c, l_c, pb = chunk_softmax(scores.pop(k), g, c, near, extras[k])
            if k + QK_AHEAD < len(units):
                scores[k + QK_AHEAD] = chunk_qk(*units[k + QK_AHEAD])
            elif k + QK_AHEAD - len(units) < QK_AHEAD:
                j = k + QK_AHEAD - len(units)
                s_scr[j] = chunk_qk(*next_units[j])
            results.append((m_c, l_c, chunk_pv(pb, g, c)))
        per_g = len(units) // N_KV
        new = []
        for g in range(N_KV):
            m_run, l_run = carry[2 * g], carry[2 * g + 1]
            stats = results[g * per_g:(g + 1) * per_g]
            m_new = m_run
            for m_c, _, _ in stats:
                m_new = jnp.maximum(m_new, m_c)
            alpha = jnp.exp2(m_run - m_new)
            l_new = alpha * l_run
            acc = alpha * acc_scr[g]
            for m_c, l_c, pv in stats:
                beta = jnp.exp2(m_c - m_new)
                l_new = l_new + beta * l_c
                acc = acc + beta * pv
            acc_scr[g] = acc
            new += [m_new, l_new]
        return tuple(new)

    def far_step(i, carry):
        units = far_units(i)
        nxt = next_units(i + 1 < n_far, far_units(i + 1))
        return merge_step(units, [None] * len(units), False, nxt, carry)

    def near_step(i, carry):
        units = near_units(i)
        extras = [None if j == 0 else jnp.where(c0 + NEAR_CHUNKS * i + j < n_chunks, 0.0, NEG)
                  for _ in range(N_KV) for j in range(NEAR_CHUNKS)]
        return merge_step(units, extras, True, near_units(i + 1), carry)

    init = (jnp.full((1, GQ), NEG, F32), jnp.zeros((1, GQ), F32)) * N_KV
    carry = lax.fori_loop(0, n_far, far_step, init)
    carry = lax.fori_loop(0, n_near, near_step, carry)

    pieces = []
    for g in range(N_KV):
        scale = gt_ref[0, 0, N_KV + g:N_KV + g + 1, :] * (1.0 / carry[2 * g + 1])
        out = out_scr[g] + scale * acc_scr[g]
        for half in range(GQ // LANES):
            a = out[:, half * LANES:(half + 1) * LANES]
            stacked = jnp.concatenate([a, pltpu.roll(a, Q_BLOCK, 1)], axis=0)
            pieces.append(stacked.T[:Q_BLOCK, :])
    o_ref[0] = jnp.concatenate(pieces, axis=1).astype(BF16)


def _far_first(units):
    assert QK_AHEAD == N_KV * NEAR_CHUNKS and FAR_CHUNKS >= QK_AHEAD
    return units[:QK_AHEAD]


def _attn_call(qt, kc, vct, ks, vst, kw, vwt, gt, ovt, ebank, sbank, wbank, far, eb_shift):
    b, nq = qt.shape[0], qt.shape[1]
    t = ks.shape[1]
    n_cmp_rows = kc.shape[1]
    n_sb = t // SEL_BLOCK
    per_b = lambda shp: pl.BlockSpec((1,) + shp[1:], lambda i, j: (i,) + (0,) * (len(shp) - 1))
    per_q = lambda shp: pl.BlockSpec((1, 1) + shp[2:], lambda i, j: (i, j) + (0,) * (len(shp) - 2))
    full = lambda shp: pl.BlockSpec(shp, lambda i, j: (0,) * len(shp))
    args = (qt, kc, vct, ks, vst, kw, vwt, gt, ovt, ebank, sbank, wbank, far)
    specs = [per_q(qt.shape), per_b(kc.shape), per_b(vct.shape), per_b(ks.shape), per_b(vst.shape),
             per_b(kw.shape), per_b(vwt.shape), per_q(gt.shape), full(ovt.shape),
             full(ebank.shape), full(sbank.shape), full(wbank.shape), full(far.shape)]
    return pl.pallas_call(
        functools.partial(_attn_kernel, n_cmp_rows=n_cmp_rows, eb_shift=eb_shift),
        grid=(b, nq),
        in_specs=specs,
        out_specs=pl.BlockSpec((1, Q_BLOCK, ATTN_W), lambda i, j: (i, j, 0)),
        out_shape=jax.ShapeDtypeStruct((b, t, ATTN_W), BF16),
        scratch_shapes=[pltpu.VMEM((N_KV, n_sb, GQ), F32),
                        pltpu.VMEM((N_KV, HEAD_DIM, GQ), F32),
                        pltpu.VMEM((N_KV, HEAD_DIM, GQ), F32),
                        pltpu.VMEM((N_KV, KV_W, GQ), BF16),
                        pltpu.VMEM((QK_AHEAD, KEY_CHUNK, GQ), F32)],
        compiler_params=_cparams(2),
        name="nsa_attention",
    )(*args)


def _ffn_kernel(x_ref, a_ref, c_ref, mod_ref, n2_ref, wo_ref, w1_ref, w2_ref, o_ref):
    aw = a_ref.shape[2]
    mix = jnp.dot(a_ref[0], wo_ref[0:aw, :], preferred_element_type=F32)
    mix = mix + jnp.dot(c_ref[0], wo_ref[aw:, :], preferred_element_type=F32)
    x1 = x_ref[0] + mod_ref[0, 2:3, :] * mix
    ms = jnp.mean(x1 * x1, axis=-1, keepdims=True)
    y = x1 * lax.rsqrt(ms + EPS) * n2_ref[...]
    h2 = (y * (1.0 + mod_ref[0, 4:5, :]) + mod_ref[0, 3:4, :]).astype(BF16)
    d_ff = w1_ref.shape[1]
    ff = jnp.zeros(x1.shape, F32)
    for j in range(d_ff // FF_CHUNK):
        a = jnp.dot(h2, w1_ref[:, j * FF_CHUNK:(j + 1) * FF_CHUNK], preferred_element_type=F32)
        a = jnp.maximum(a, 0.0)
        ff = ff + jnp.dot((a * a).astype(BF16), w2_ref[j * FF_CHUNK:(j + 1) * FF_CHUNK, :],
                          preferred_element_type=F32)
    o_ref[0] = x1 + mod_ref[0, 5:6, :] * ff


def _ffn_call(x, attn, conv, mod, norm2, w_out, w_ff1, w_ff2):
    b, t, d = x.shape
    tm = min(ROW_TILE, t)
    row_spec = lambda w: pl.BlockSpec((1, tm, w), lambda i, j: (i, j, 0))
    full = lambda shp: pl.BlockSpec(shp, lambda i, j: (0,) * len(shp),
                                    pipeline_mode=pl.Buffered(1))
    return pl.pallas_call(
        _ffn_kernel,
        grid=(b, t // tm),
        in_specs=[row_spec(d), row_spec(attn.shape[2]), row_spec(conv.shape[2]),
                  pl.BlockSpec((1, N_MOD, d), lambda i, j: (i, 0, 0)),
                  full((1, d)), full(w_out.shape), full(w_ff1.shape), full(w_ff2.shape)],
        out_specs=row_spec(d),
        out_shape=jax.ShapeDtypeStruct((b, t, d), F32),
        compiler_params=_cparams(2),
        name="outproj_mlp",
    )(x, attn, conv, mod, norm2, w_out, w_ff1, w_ff2)


def _block_diag_ones(n):
    idx = np.arange(n) // HEAD_DIM
    return jnp.asarray(idx[:, None] == idx[None, :], dtype=BF16)


def _pack_w_in(w_in):
    d = w_in.shape[0]
    conv_w = d - ATTN_W
    sizes = [ATTN_W] + [KV_W] * 6 + [N_BRANCH * N_HEADS] + [conv_w] * 3
    offs = np.concatenate([[0], np.cumsum(sizes)])
    part = lambda i: w_in[:, offs[i]:offs[i + 1]]
    q, kc, vc, ks, vs, kw, vw, g, cgate, bgate, u = (part(i) for i in range(11))
    w_nat = jnp.concatenate([kc, vc, ks, kw, cgate, bgate, u], axis=1).astype(BF16)
    gt = g.reshape(d, N_KV, GQA, N_BRANCH).transpose(2, 3, 1, 0).reshape(GQA, N_BRANCH * N_KV, d)
    gt = jnp.pad(gt, ((0, 0), (0, SUBLANES - N_BRANCH * N_KV), (0, 0))).reshape(_G_ROWS, d)
    w_tr = jnp.concatenate([q.T, vs.T, vw.T, gt], axis=0).astype(BF16)
    return w_nat, w_tr


def _expand_w1(w1):
    hid = w1.shape[1]
    w = w1.reshape(2, CMP_STRIDE, 1, HEAD_DIM, 1, hid)
    eye = jnp.eye(N_KV, dtype=w1.dtype).reshape(1, 1, N_KV, 1, N_KV, 1)
    return (w * eye).reshape(2, CMP_STRIDE * KV_W, N_KV * hid).astype(BF16)


def _expand_w2(w2):
    hid = w2.shape[0]
    eye = jnp.eye(N_KV, dtype=w2.dtype).reshape(N_KV, 1, N_KV, 1)
    return (w2.reshape(1, hid, 1, HEAD_DIM) * eye).reshape(N_KV * hid, KV_W).astype(BF16)


def _expand_pe(pe):
    p = pe.reshape(2, CMP_STRIDE, 1, HEAD_DIM)
    return jnp.broadcast_to(p, (2, CMP_STRIDE, N_KV, HEAD_DIM)).reshape(2, CMP_STRIDE * KV_W)


def _bucket_thresholds():
    n = np.arange(2 * REL_MAX_DIST)
    max_exact = REL_BUCKETS // 2
    nf = np.maximum(n, max_exact).astype(np.float32)
    ratio = np.log(nf / np.float32(max_exact)) / np.float32(math.log(REL_MAX_DIST / max_exact))
    large = max_exact + (ratio * np.float32(REL_BUCKETS - max_exact)).astype(np.int32)
    table = np.where(n < max_exact, n, np.minimum(large, REL_BUCKETS - 1))
    return tuple(int(np.searchsorted(table, k, side="left")) for k in range(REL_BUCKETS))


def _bank_kernel(rows_ref, o_ref, *, thresholds, dist_fn):
    shape = o_ref.shape[-2:]
    row = lax.broadcasted_iota(jnp.int32, shape, 0)
    qi = lax.broadcasted_iota(jnp.int32, shape, 1) & (Q_BLOCK - 1)
    dist, ok = dist_fn(row, qi)
    v = jnp.broadcast_to(rows_ref[0, 0:1, :], shape)
    for k in range(1, REL_BUCKETS):
        v = jnp.where(dist >= thresholds[k], rows_ref[0, k:k + 1, :], v)
    o_ref[...] = jnp.where(ok, v, NEG).reshape(o_ref.shape)


def _bank_call(bias_rows, lead, n_tiles, dist_fn, name):
    grid = tuple(lead) + (N_KV, n_tiles)
    nl = len(lead)
    blk = (1,) * (nl + 2) + (SEL_BLOCK, GQ)
    thr = _bucket_thresholds()

    def body(rows_ref, o_ref):
        ids = [pl.program_id(a) for a in range(nl + 2)]
        fn = lambda row, qi: dist_fn(ids[:nl], ids[nl + 1], row, qi)
        _bank_kernel(rows_ref, o_ref, thresholds=thr, dist_fn=fn)

    return pl.pallas_call(
        body,
        grid=grid,
        in_specs=[pl.BlockSpec((1, REL_BUCKETS, GQ), lambda *i: (i[nl], 0, 0))],
        out_specs=pl.BlockSpec(blk, lambda *i: tuple(i) + (0, 0)),
        out_shape=jax.ShapeDtypeStruct(tuple(lead) + (N_KV, n_tiles, SEL_BLOCK, GQ), F32),
        compiler_params=_cparams(nl + 2),
        name=name,
    )(bias_rows)


def _bias_banks(rel_bias, t):
    n_cmp_rows = t // CMP_STRIDE
    rows = rel_bias.reshape(REL_BUCKETS, N_KV, GQA).transpose(1, 0, 2)
    rows = jnp.repeat(rows, Q_BLOCK, axis=2) * LOG2E

    def sel_dist(lead, tile, row, qi):
        dist = SEL_BLOCK * tile + qi - row
        return dist, dist >= 0

    sbank = _bank_call(rows, (), NEAR_TILES + 1, sel_dist, "bias_bank_sel")
    far = sbank[:, NEAR_TILES, 0:1, :]

    def win_dist(lead, tile, row, qi):
        dist = SEL_BLOCK * tile + qi - row
        return dist, (dist >= 0) & (dist < WINDOW)

    wbank = _bank_call(rows, (), WIN_BLOCKS + 2, win_dist, "bias_bank_win")

    cmp_c = n_cmp_rows - 4
    shifts = tuple(int((-(cmp_c - 4 * p)) % 8) for p in range(2))
    n_tiles = (cmp_c + n_cmp_rows + 8 + SEL_BLOCK - 1) // SEL_BLOCK

    def cmp_dist(lead, tile, row, qi):
        e = SEL_BLOCK * tile + row - jnp.where(lead[0] == 0, shifts[0], shifts[1])
        dist = qi - CMP_STRIDE * e + (CMP_STRIDE * cmp_c - (CMP_BLOCK - 1))
        return dist, (dist >= 0) & (e >= 0)

    ebank = _bank_call(rows, (2,), n_tiles, cmp_dist, "bias_bank_cmp")
    ebank = ebank.reshape(2, N_KV, n_tiles * SEL_BLOCK, GQ)
    return ebank, sbank, wbank, far, shifts


def _overlap_t(t):
    n_cmp_rows = t // CMP_STRIDE
    n_sb = t // SEL_BLOCK
    c_start = np.arange(n_cmp_rows)[None, :] * CMP_STRIDE
    s_start = np.arange(n_sb)[:, None] * SEL_BLOCK
    ov = np.clip(np.minimum(c_start + CMP_BLOCK, s_start + SEL_BLOCK)
                 - np.maximum(c_start, s_start), 0, None) / CMP_BLOCK
    ov[:, n_cmp_rows - 1] = 0.0
    return jnp.asarray(ov, dtype=BF16)


def _layer(x, c_pad, w_in, q_norm, k_norm, cmp_pe_k, cmp_w1_k, cmp_w2_k, cmp_pe_v, cmp_w1_v,
           cmp_w2_v, rel_bias, conv_w, w_out, norm1, norm2, w_ada, b_ada, w_ff1, w_ff2):
    b, t, d = x.shape
    scale = HEAD_DIM ** -0.5

    mod = _mod_call(c_pad, w_ada, b_ada)[:b].reshape(b, N_MOD, d)

    qn_col = (jnp.tile(q_norm, N_HEADS) * (scale * LOG2E)).reshape(ATTN_W, 1)
    kn_t = jnp.tile(k_norm, N_KV).reshape(1, KV_W)
    bdq = _block_diag_ones(ATTN_W)
    bdk = _block_diag_ones(KV_W)
    w_nat, w_tr = _pack_w_in(w_in)
    qt, kc_raw, vc_raw, ks, vst, kw, vwt, gt, conv = _inproj_call(
        x, mod, norm1.reshape(1, d), w_nat, w_tr, qn_col, kn_t, conv_w, bdq, bdk)

    n_cmp_rows = t // CMP_STRIDE
    kc, vct = _compress_call(
        kc_raw.reshape(b, n_cmp_rows, CMP_STRIDE * KV_W), vc_raw.reshape(b, n_cmp_rows, CMP_STRIDE * KV_W),
        _expand_pe(cmp_pe_k), _expand_pe(cmp_pe_v), _expand_w1(cmp_w1_k), _expand_w1(cmp_w1_v),
        _expand_w2(cmp_w2_k), _expand_w2(cmp_w2_v).T, kn_t, bdk)

    ebank, sbank, wbank, far, eb_shift = _bias_banks(rel_bias, t)
    attn = _attn_call(qt, kc, vct, ks, vst, kw, vwt, gt, _overlap_t(t), ebank, sbank, wbank, far,
                      eb_shift)

    return _ffn_call(x, attn, conv, mod, norm2.reshape(1, d), w_out.astype(BF16),
                     w_ff1.astype(BF16), w_ff2.astype(BF16))


def kernel(x, c, w_in, q_norm, k_norm, cmp_pe_k, cmp_w1_k, cmp_w2_k, cmp_pe_v, cmp_w1_v, cmp_w2_v,
           rel_bias, conv_w, w_out, norm1, norm2, w_ada, b_ada, w_ff1, w_ff2):
    b = x.shape[0]
    c_pad = jnp.pad(c, ((0, (-b) % 8), (0, 0)))
    for l in range(w_in.shape[0]):
        x = _layer(x, c_pad, w_in[l], q_norm[l], k_norm[l], cmp_pe_k[l], cmp_w1_k[l], cmp_w2_k[l],
                   cmp_pe_v[l], cmp_w1_v[l], cmp_w2_v[l], rel_bias, conv_w[l], w_out[l],
                   norm1[l], norm2[l], w_ada[l], b_ada[l], w_ff1[l], w_ff2[l])
    return x
```

```python
import functools
import math

import numpy as np
import jax
import jax.numpy as jnp
from jax import lax
from jax.experimental import pallas as pl
from jax.experimental.pallas import tpu as pltpu

HEAD_DIM = 64
N_HEADS = 8
N_KV = 2
GQA = N_HEADS // N_KV
ATTN_W = N_HEADS * HEAD_DIM
KV_W = N_KV * HEAD_DIM
CONV_K = 3
CMP_BLOCK = 32
CMP_STRIDE = 16
CMP_HIDDEN = 256
SEL_BLOCK = 64
N_SELECT = 16
WINDOW = 512
Q_BLOCK = 64
REL_BUCKETS = 32
REL_MAX_DIST = 1024
N_MOD = 6
N_BRANCH = 3
EPS = 1e-6
NEG = -1e30

LANES = 128
SUBLANES = 8
GQ = GQA * Q_BLOCK
KEY_CHUNK = 256
BLOCKS_PER_CHUNK = KEY_CHUNK // SEL_BLOCK
V_TILE = 128
BF16_ROWS = 16
V_ROWS = HEAD_DIM + BF16_ROWS
V_TILE_ROWS = N_KV * V_ROWS
WIN_BLOCKS = WINDOW // SEL_BLOCK
WIN_TILES = WINDOW // V_TILE + 1
NEAR_TILES = (REL_MAX_DIST + Q_BLOCK - 1) // SEL_BLOCK + 1
ROW_TILE = 512
FAR_CHUNKS = 4
NEAR_CHUNKS = 2
QK_AHEAD = 4
LOG2E = math.log2(math.e)
FF_CHUNK = 1024
VMEM_LIMIT = 56 * 1024 * 1024

F32 = jnp.float32
BF16 = jnp.bfloat16
_NT = (((1,), (1,)), ((), ()))


def _cparams(n_axes):
    return pltpu.CompilerParams(dimension_semantics=("arbitrary",) * n_axes,
                                vmem_limit_bytes=VMEM_LIMIT)


def _with_ones_rows(vt):
    ones = jnp.ones((BF16_ROWS, vt.shape[1]), vt.dtype)
    parts = []
    for g in range(N_KV):
        parts += [vt[g * HEAD_DIM:(g + 1) * HEAD_DIM, :], ones]
    return jnp.concatenate(parts, axis=0)


def _swap_halves(p0, p1):
    low = lax.broadcasted_iota(jnp.int32, p0.shape, 1) < LANES // 2
    return (jnp.where(low, p0, pltpu.roll(p1, LANES // 2, 1)),
            jnp.where(low, pltpu.roll(p0, LANES // 2, 1), p1))


def _mod_kernel(c_ref, w_ref, b_ref, o_ref):
    c = c_ref[...]
    a = c * jax.nn.sigmoid(c)
    o_ref[...] = jnp.dot(a, w_ref[...], preferred_element_type=F32,
                         precision=lax.Precision.HIGHEST) + b_ref[...]


def _mod_call(c_pad, w_ada, b_ada):
    rows, d = c_pad.shape
    n = w_ada.shape[1]
    tn = 1024
    return pl.pallas_call(
        _mod_kernel,
        grid=(n // tn,),
        in_specs=[pl.BlockSpec((rows, d), lambda j: (0, 0)),
                  pl.BlockSpec((d, tn), lambda j: (0, j)),
                  pl.BlockSpec((1, tn), lambda j: (0, j))],
        out_specs=pl.BlockSpec((rows, tn), lambda j: (0, j)),
        out_shape=jax.ShapeDtypeStruct((rows, n), F32),
        compiler_params=_cparams(1),
        name="adaln_mod",
    )(c_pad, w_ada, b_ada.reshape(1, n))


_N_KC, _N_VC, _N_KS, _N_KW, _N_CONV = 0, KV_W, 2 * KV_W, 3 * KV_W, 4 * KV_W
_T_Q, _T_VS, _T_VW, _T_G = 0, ATTN_W, ATTN_W + KV_W, ATTN_W + 2 * KV_W
_G_ROWS = GQA * SUBLANES


def _inproj_kernel(x_ref, mod_ref, n1_ref, wn_ref, wt_ref, qn_ref, kn_ref, cw_ref, bdq_ref, bdk_ref,
                   qt_out, kc_out, vc_out, ks_out, vst_out, kw_out, vwt_out, gt_out, conv_out,
                   carry, *, conv_w):
    t = pl.program_id(1)
    tm = x_ref.shape[1]
    x = x_ref[0]
    ms = jnp.mean(x * x, axis=-1, keepdims=True)
    y = x * lax.rsqrt(ms + EPS) * n1_ref[...]
    h = (y * (1.0 + mod_ref[0, 1:2, :]) + mod_ref[0, 0:1, :]).astype(BF16)

    def proj(a, b):
        return jnp.dot(h, wn_ref[:, a:b], preferred_element_type=F32)

    def proj_t(a, b):
        return lax.dot_general(wt_ref[a:b, :], h, _NT, preferred_element_type=F32)

    def head_norm(v, gain):
        ssq = jnp.dot((v * v).astype(BF16), bdk_ref[...], preferred_element_type=F32)
        return v * lax.rsqrt(ssq * (1.0 / HEAD_DIM) + EPS) * gain

    qf = proj_t(_T_Q, _T_VS)
    ssq = jnp.dot(bdq_ref[...], (qf * qf).astype(BF16), preferred_element_type=F32)
    qf = qf * lax.rsqrt(ssq * (1.0 / HEAD_DIM) + EPS) * qn_ref[...]
    for c in range(tm // LANES):
        for g in range(N_KV):
            pc = [qf[(g * GQA + r) * HEAD_DIM:(g * GQA + r + 1) * HEAD_DIM, c * LANES:(c + 1) * LANES]
                  for r in range(GQA)]
            lo01, hi01 = _swap_halves(pc[0], pc[1])
            lo23, hi23 = _swap_halves(pc[2], pc[3])
            qt_out[0, 2 * c, g] = jnp.concatenate([lo01, lo23], axis=1).astype(BF16)
            qt_out[0, 2 * c + 1, g] = jnp.concatenate([hi01, hi23], axis=1).astype(BF16)

    gf = jax.nn.sigmoid(proj_t(_T_G, _T_G + _G_ROWS))
    for c in range(tm // LANES):
        pc = [gf[r * SUBLANES:(r + 1) * SUBLANES, c * LANES:(c + 1) * LANES] for r in range(GQA)]
        lo01, hi01 = _swap_halves(pc[0], pc[1])
        lo23, hi23 = _swap_halves(pc[2], pc[3])
        gt_out[0, 2 * c] = jnp.concatenate([lo01, lo23], axis=1)
        gt_out[0, 2 * c + 1] = jnp.concatenate([hi01, hi23], axis=1)

    vs_f = _with_ones_rows(proj_t(_T_VS, _T_VW).astype(BF16))
    vw_f = _with_ones_rows(proj_t(_T_VW, _T_G).astype(BF16))
    for j in range(tm // KEY_CHUNK):
        vst_out[0, j] = vs_f[:, j * KEY_CHUNK:(j + 1) * KEY_CHUNK]
    for j in range(tm // V_TILE):
        vwt_out[0, j] = vw_f[:, j * V_TILE:(j + 1) * V_TILE]

    kc_out[0] = proj(_N_KC, _N_VC)
    vc_out[0] = proj(_N_VC, _N_KS)
    ks_out[0] = head_norm(proj(_N_KS, _N_KW), kn_ref[...]).astype(BF16)
    kw_out[0] = head_norm(proj(_N_KW, _N_CONV), kn_ref[...]).astype(BF16)

    c_bg = _N_CONV + conv_w
    c_u = c_bg + conv_w
    z = proj(_N_CONV, c_bg) * proj(c_u, c_u + conv_w)

    @pl.when(t == 0)
    def _():
        carry[...] = jnp.zeros_like(carry)

    row = lax.broadcasted_iota(jnp.int32, z.shape, 0)
    p2 = carry[6:7, :]
    p1 = carry[7:8, :]
    z1 = jnp.where(row == 0, p1, pltpu.roll(z, 1, 0))
    z2 = jnp.where(row == 0, p2, jnp.where(row == 1, p1, pltpu.roll(z, 2, 0)))
    zc = cw_ref[0:1, :] * z2 + cw_ref[1:2, :] * z1 + cw_ref[2:3, :] * z
    conv_out[0] = (proj(c_bg, c_u) * zc).astype(BF16)
    carry[...] = z[tm - SUBLANES:tm, :]


def _inproj_call(x, mod, norm1, w_nat, w_tr, qn_col, kn_t, conv_w, bdq, bdk):
    b, t, d = x.shape
    tm = min(ROW_TILE, t)
    cw = conv_w.shape[1]
    nq = t // Q_BLOCK
    row_spec = lambda w: pl.BlockSpec((1, tm, w), lambda i, j: (i, j, 0))
    full = lambda shp: pl.BlockSpec(shp, lambda i, j: (0,) * len(shp))
    vt_spec = lambda w: pl.BlockSpec((1, tm // w, V_TILE_ROWS, w), lambda i, j: (i, j, 0, 0))
    vt_shape = lambda w: jax.ShapeDtypeStruct((b, t // w, V_TILE_ROWS, w), BF16)
    kv = lambda dt: jax.ShapeDtypeStruct((b, t, KV_W), dt)
    out_specs = [pl.BlockSpec((1, tm // Q_BLOCK, N_KV, HEAD_DIM, GQ), lambda i, j: (i, j, 0, 0, 0)),
                 row_spec(KV_W), row_spec(KV_W), row_spec(KV_W), vt_spec(KEY_CHUNK), row_spec(KV_W),
                 vt_spec(V_TILE),
                 pl.BlockSpec((1, tm // Q_BLOCK, SUBLANES, GQ), lambda i, j: (i, j, 0, 0)),
                 row_spec(cw)]
    out_shape = [jax.ShapeDtypeStruct((b, nq, N_KV, HEAD_DIM, GQ), BF16),
                 kv(F32), kv(F32), kv(BF16), vt_shape(KEY_CHUNK), kv(BF16), vt_shape(V_TILE),
                 jax.ShapeDtypeStruct((b, nq, SUBLANES, GQ), F32),
                 jax.ShapeDtypeStruct((b, t, cw), BF16)]
    return pl.pallas_call(
        functools.partial(_inproj_kernel, conv_w=cw),
        grid=(b, t // tm),
        in_specs=[row_spec(d),
                  pl.BlockSpec((1, N_MOD, d), lambda i, j: (i, 0, 0)),
                  full((1, d)), full(w_nat.shape), full(w_tr.shape), full((ATTN_W, 1)),
                  full((1, KV_W)), full((CONV_K, cw)), full((ATTN_W, ATTN_W)), full((KV_W, KV_W))],
        out_specs=out_specs,
        out_shape=out_shape,
        scratch_shapes=[pltpu.VMEM((SUBLANES, cw), F32)],
        compiler_params=_cparams(2),
        name="inproj",
    )(x, mod, norm1, w_nat, w_tr, qn_col, kn_t, conv_w, bdq, bdk)


def _compress_kernel(kx_ref, vx_ref, pek_ref, pev_ref, w1k_ref, w1v_ref, w2k_ref, w2vt_ref,
                     kn_ref, bdk_ref, kc_out, vct_out):
    def hidden(x_ref, pe_ref, w1_ref):
        xv = x_ref[0]
        u = jnp.dot((xv + pe_ref[0:1, :]).astype(BF16), w1_ref[0], preferred_element_type=F32)
        v = jnp.dot((xv + pe_ref[1:2, :]).astype(BF16), w1_ref[1], preferred_element_type=F32)
        n = u.shape[0]
        hid = u + pltpu.roll(v, n - 1, 0)
        return jax.nn.gelu(hid, approximate=True).astype(BF16)

    kc = jnp.dot(hidden(kx_ref, pek_ref, w1k_ref), w2k_ref[...], preferred_element_type=F32)
    ssq = jnp.dot((kc * kc).astype(BF16), bdk_ref[...], preferred_element_type=F32)
    kc_out[0] = (kc * lax.rsqrt(ssq * (1.0 / HEAD_DIM) + EPS) * kn_ref[...]).astype(BF16)
    vct_out[0] = _with_ones_rows(lax.dot_general(w2vt_ref[...], hidden(vx_ref, pev_ref, w1v_ref), _NT,
                                                 preferred_element_type=F32).astype(BF16))


def _compress_call(kx, vx, pek, pev, w1k, w1v, w2k, w2vt, kn_t, bdk):
    b, nrow, wide = kx.shape
    hid2 = w1k.shape[2]
    full = lambda shp: pl.BlockSpec(shp, lambda i: (0,) * len(shp))
    xs = pl.BlockSpec((1, nrow, wide), lambda i: (i, 0, 0))
    return pl.pallas_call(
        _compress_kernel,
        grid=(b,),
        in_specs=[xs, xs, full((2, wide)), full((2, wide)), full((2, wide, hid2)),
                  full((2, wide, hid2)), full((hid2, KV_W)), full((KV_W, hid2)),
                  full((1, KV_W)), full((KV_W, KV_W))],
        out_specs=[pl.BlockSpec((1, nrow, KV_W), lambda i: (i, 0, 0)),
                   pl.BlockSpec((1, V_TILE_ROWS, nrow), lambda i: (i, 0, 0))],
        out_shape=[jax.ShapeDtypeStruct((b, nrow, KV_W), BF16),
                   jax.ShapeDtypeStruct((b, V_TILE_ROWS, nrow), BF16)],
        compiler_params=_cparams(1),
        name="compress",
    )(kx, vx, pek, pev, w1k, w1v, w2k, w2vt, kn_t, bdk)


def _attn_kernel(q_ref, kc_ref, vct_ref, ks_ref, vst_ref, kw_ref, vwt_ref, gt_ref, ovt_ref,
                 eb_ref, sb_ref, wb_ref, far_ref, o_ref, sel_scr, acc_scr, out_scr, qp_scr, s_scr,
                 fin_scr, *, n_cmp_rows, eb_shift):
    n_sb = sel_scr.shape[1]
    ci = jnp.minimum(pl.program_id(1), n_sb - 1)
    par = lax.rem(ci, 2)
    n_total = ks_ref.shape[1] // KEY_CHUNK
    vsl = lambda g: slice(g * V_ROWS, (g + 1) * V_ROWS)

    @pl.when((pl.program_id(0) == 0) & (pl.program_id(1) == 0))
    def _():
        fin_scr[...] = jnp.zeros_like(fin_scr)

    pieces = []
    for g in range(N_KV):
        for half in range(GQ // LANES):
            a = fin_scr[g, :, half * LANES:(half + 1) * LANES]
            stacked = jnp.concatenate([a, pltpu.roll(a, Q_BLOCK, 1)], axis=0)
            pieces.append(stacked.T[:Q_BLOCK, :])
    o_ref[0] = jnp.concatenate(pieces, axis=1).astype(BF16)

    zeros_q = jnp.zeros((HEAD_DIM, GQ), BF16)
    qp_scr[0] = jnp.concatenate([q_ref[0, 0, 0], zeros_q], axis=0)
    qp_scr[1] = jnp.concatenate([zeros_q, q_ref[0, 0, 1]], axis=0)

    def chunk_qk(g, c):
        kk = ks_ref[0, pl.ds(pl.multiple_of(c * KEY_CHUNK, KEY_CHUNK), KEY_CHUNK), :]
        return jnp.dot(kk, qp_scr[g], preferred_element_type=F32)

    n_chunks = ci // BLOCKS_PER_CHUNK + 1
    n_far = jnp.maximum(ci - (NEAR_TILES - 1), 0) // (BLOCKS_PER_CHUNK * FAR_CHUNKS)
    c0 = n_far * FAR_CHUNKS
    n_near = (n_chunks - c0 + NEAR_CHUNKS - 1) // NEAR_CHUNKS

    def near_units(i):
        return [(g, jnp.minimum(c0 + NEAR_CHUNKS * i + j, n_total - 1))
                for g in range(N_KV) for j in range(NEAR_CHUNKS)]

    def far_units(i):
        return [(g, i * FAR_CHUNKS + j) for g in range(N_KV) for j in range(FAR_CHUNKS)]

    def prefetch_scores(units):
        for k in range(QK_AHEAD):
            s_scr[k] = chunk_qk(*units[k])

    cmp_c = n_cmp_rows - 4
    e0 = cmp_c - 4 * ci + jnp.where(par == 0, eb_shift[0], eb_shift[1])
    e0 = pl.multiple_of(e0, 8)
    o_cmp = []
    imp = []
    w0 = ci // 2 - (WIN_TILES - 1)
    win_tiles = [jnp.maximum(w0 + j, 0) for j in range(WIN_TILES)]
    s_cmp = [jnp.dot(kc_ref[0], qp_scr[g], preferred_element_type=F32) for g in range(N_KV)]
    for g in range(N_KV):
        sc = s_cmp[g] + eb_ref[par, g, pl.ds(e0, n_cmp_rows), :]
        m = jnp.max(sc, axis=0, keepdims=True)
        e = jnp.exp2(sc - m).astype(BF16)
        ov = jnp.dot(vct_ref[0, vsl(g), :], e, preferred_element_type=F32)
        inv = jnp.where(m > 0.5 * NEG, 1.0 / ov[HEAD_DIM:HEAD_DIM + 1, :], 0.0)
        o_cmp.append(ov[:HEAD_DIM, :] * inv)
        ir = jnp.dot(ovt_ref[...], e, preferred_element_type=F32) * inv
        a = ir[:, :LANES] + ir[:, LANES:]
        imp.append(a + pltpu.roll(a, Q_BLOCK, 1))

    s_win = [[jnp.dot(kw_ref[0, pl.ds(pl.multiple_of(tj * V_TILE, V_TILE), V_TILE), :], qp_scr[g],
                      preferred_element_type=F32) for tj in win_tiles] for g in range(N_KV)]

    first_near = near_units(0)

    def next_units(use_far, far_list):
        return [(jnp.where(use_far, g_f, g_n), jnp.where(use_far, jnp.minimum(c_f, n_total - 1), c_n))
                for (g_f, c_f), (g_n, c_n) in zip(_far_first(far_list), first_near)]

    prefetch_scores(next_units(n_far > 0, far_units(0)))

    lane = lax.broadcasted_iota(jnp.int32, (n_sb, LANES), 1)
    jidx = lax.broadcasted_iota(jnp.int32, (n_sb, LANES), 0)
    jf = jidx.astype(F32)
    valid = jidx <= ci
    forced = (jidx == 0) | (jidx == ci) | (jidx == ci - 1)
    score = jnp.where(valid, jnp.where(forced, -2.0, jnp.where(lane < Q_BLOCK, imp[0], imp[1])),
                      -1.0)
    sel = jnp.where(forced, 1.0, 0.0)
    for _ in range(min(N_SELECT, n_sb) - 3):
        mx = jnp.max(score, axis=0, keepdims=True)
        first = jnp.min(jnp.where(score == mx, jf, float(n_sb)), axis=0, keepdims=True)
        hit = jf == first
        sel = jnp.where(hit, 1.0, sel)
        score = jnp.where(hit, -2.0, score)
    selneg = jnp.where((sel > 0.5) & valid, 0.0, NEG)
    swapped = pltpu.roll(selneg, Q_BLOCK, 1)
    left = jnp.where(lane < Q_BLOCK, selneg, swapped)
    right = jnp.where(lane < Q_BLOCK, swapped, selneg)
    sel_scr[0] = jnp.concatenate([left, left], axis=1)
    sel_scr[1] = jnp.concatenate([right, right], axis=1)

    for g in range(N_KV):
        parts = []
        for i in range(2 * WIN_TILES):
            delta = par + WIN_BLOCKS - i
            ok = (delta >= 0) & (delta <= WIN_BLOCKS) & (delta <= ci)
            tile = jnp.where(ok, delta, WIN_BLOCKS + 1)
            half = s_win[g][i // 2][(i % 2) * SEL_BLOCK:(i % 2 + 1) * SEL_BLOCK, :]
            parts.append(half + wb_ref[g, tile])
        s = jnp.concatenate(parts, axis=0)
        m = jnp.max(s, axis=0, keepdims=True)
        pb = jnp.exp2(s - m).astype(BF16)
        o_win = jnp.zeros((V_ROWS, GQ), F32)
        for j, tj in enumerate(win_tiles):
            o_win = o_win + jnp.dot(vwt_ref[0, tj, vsl(g), :], pb[j * V_TILE:(j + 1) * V_TILE, :],
                                    preferred_element_type=F32)
        w_scale = gt_ref[0, 0, 2 * N_KV + g:2 * N_KV + g + 1, :] * (1.0 / o_win[HEAD_DIM:HEAD_DIM + 1, :])
        out_scr[g] = gt_ref[0, 0, g:g + 1, :] * o_cmp[g] + w_scale * o_win[:HEAD_DIM, :]
        acc_scr[g] = jnp.zeros((V_ROWS, GQ), F32)

    def chunk_softmax(s, g, c, near, extra):
        parts = []
        for i in range(BLOCKS_PER_CHUNK):
            kb = c * BLOCKS_PER_CHUNK + i
            blk = s[i * SEL_BLOCK:(i + 1) * SEL_BLOCK, :]
            mrow = sel_scr[g, pl.ds(kb, 1), :]
            if extra is not None:
                mrow = mrow + extra
            if near:
                tile = jnp.clip(ci - kb, 0, NEAR_TILES)
                blk = blk + sb_ref[g, tile] + mrow
            else:
                blk = blk + (mrow + far_ref[g])
            parts.append(blk)
        s = jnp.concatenate(parts, axis=0)
        m_c = jnp.max(s, axis=0, keepdims=True)
        return m_c, jnp.exp2(s - m_c).astype(BF16)

    def chunk_pv(pb, g, c):
        return jnp.dot(vst_ref[0, c, vsl(g), :], pb, preferred_element_type=F32)

    def merge_step(units, extras, near, next_units, carry):
        scores = {k: s_scr[k] for k in range(QK_AHEAD)}
        results = []
        for k, (g, c) in enumerate(units):
            m_c, pb = chunk_softmax(scores.pop(k), g, c, near, extras[k])
            if k + QK_AHEAD < len(units):
                scores[k + QK_AHEAD] = chunk_qk(*units[k + QK_AHEAD])
            else:
                j = k + QK_AHEAD - len(units)
                s_scr[j] = chunk_qk(*next_units[j])
            results.append((m_c, chunk_pv(pb, g, c)))
        per_g = len(units) // N_KV
        new = []
        for g in range(N_KV):
            m_run = carry[g]
            stats = results[g * per_g:(g + 1) * per_g]
            m_new = m_run
            for m_c, _ in stats:
                m_new = jnp.maximum(m_new, m_c)
            acc = jnp.exp2(m_run - m_new) * acc_scr[g]
            for m_c, pv in stats:
                acc = acc + jnp.exp2(m_c - m_new) * pv
            acc_scr[g] = acc
            new.append(m_new)
        return tuple(new)

    def far_step(i, carry):
        units = far_units(i)
        nxt = next_units(i + 1 < n_far, far_units(i + 1))
        return merge_step(units, [None] * len(units), False, nxt, carry)

    def near_step(i, carry):
        units = near_units(i)
        extras = [None if j == 0 else jnp.where(c0 + NEAR_CHUNKS * i + j < n_chunks, 0.0, NEG)
                  for _ in range(N_KV) for j in range(NEAR_CHUNKS)]
        return merge_step(units, extras, True, near_units(i + 1), carry)

    init = (jnp.full((1, GQ), NEG, F32),) * N_KV
    carry = lax.fori_loop(0, n_far, far_step, init)
    lax.fori_loop(0, n_near, near_step, carry)

    for g in range(N_KV):
        acc = acc_scr[g]
        scale = gt_ref[0, 0, N_KV + g:N_KV + g + 1, :] * (1.0 / acc[HEAD_DIM:HEAD_DIM + 1, :])
        fin_scr[g] = out_scr[g] + scale * acc[:HEAD_DIM, :]


def _far_first(units):
    assert QK_AHEAD == N_KV * NEAR_CHUNKS and FAR_CHUNKS >= QK_AHEAD
    return units[:QK_AHEAD]


def _attn_call(qt, kc, vct, ks, vst, kw, vwt, gt, ovt, ebank, sbank, wbank, far, eb_shift):
    b, nq = qt.shape[0], qt.shape[1]
    t = ks.shape[1]
    n_cmp_rows = kc.shape[1]
    n_sb = t // SEL_BLOCK
    per_b = lambda shp: pl.BlockSpec((1,) + shp[1:], lambda i, j: (i,) + (0,) * (len(shp) - 1))
    per_q = lambda shp: pl.BlockSpec(
        (1, 1) + shp[2:], lambda i, j: (i, jnp.minimum(j, nq - 1)) + (0,) * (len(shp) - 2))
    full = lambda shp: pl.BlockSpec(shp, lambda i, j: (0,) * len(shp))
    args = (qt, kc, vct, ks, vst, kw, vwt, gt, ovt, ebank, sbank, wbank, far)
    specs = [per_q(qt.shape), per_b(kc.shape), per_b(vct.shape), per_b(ks.shape), per_b(vst.shape),
             per_b(kw.shape), per_b(vwt.shape), per_q(gt.shape), full(ovt.shape),
             full(ebank.shape), full(sbank.shape), full(wbank.shape), full(far.shape)]
    return pl.pallas_call(
        functools.partial(_attn_kernel, n_cmp_rows=n_cmp_rows, eb_shift=eb_shift),
        grid=(b, nq + 1),
        in_specs=specs,
        out_specs=pl.BlockSpec((1, Q_BLOCK, ATTN_W), lambda i, j: (i, jnp.maximum(j - 1, 0), 0)),
        out_shape=jax.ShapeDtypeStruct((b, t, ATTN_W), BF16),
        scratch_shapes=[pltpu.VMEM((N_KV, n_sb, GQ), F32),
                        pltpu.VMEM((N_KV, V_ROWS, GQ), F32),
                        pltpu.VMEM((N_KV, HEAD_DIM, GQ), F32),
                        pltpu.VMEM((N_KV, KV_W, GQ), BF16),
                        pltpu.VMEM((QK_AHEAD, KEY_CHUNK, GQ), F32),
                        pltpu.VMEM((N_KV, HEAD_DIM, GQ), F32)],
        compiler_params=_cparams(2),
        name="nsa_attention",
    )(*args)


def _ffn_kernel(x_ref, a_ref, c_ref, mod_ref, n2_ref, wo_ref, w1_ref, w2_ref, o_ref):
    aw = a_ref.shape[2]
    mix = jnp.dot(a_ref[0], wo_ref[0:aw, :], preferred_element_type=F32)
    mix = mix + jnp.dot(c_ref[0], wo_ref[aw:, :], preferred_element_type=F32)
    x1 = x_ref[0] + mod_ref[0, 2:3, :] * mix
    ms = jnp.mean(x1 * x1, axis=-1, keepdims=True)
    y = x1 * lax.rsqrt(ms + EPS) * n2_ref[...]
    h2 = (y * (1.0 + mod_ref[0, 4:5, :]) + mod_ref[0, 3:4, :]).astype(BF16)
    d_ff = w1_ref.shape[1]
    ff = jnp.zeros(x1.shape, F32)
    for j in range(d_ff // FF_CHUNK):
        a = jnp.dot(h2, w1_ref[:, j * FF_CHUNK:(j + 1) * FF_CHUNK], preferred_element_type=F32)
        a = jnp.maximum(a, 0.0)
        ff = ff + jnp.dot((a * a).astype(BF16), w2_ref[j * FF_CHUNK:(j + 1) * FF_CHUNK, :],
                          preferred_element_type=F32)
    o_ref[0] = x1 + mod_ref[0, 5:6, :] * ff


def _ffn_call(x, attn, conv, mod, norm2, w_out, w_ff1, w_ff2):
    b, t, d = x.shape
    tm = min(ROW_TILE, t)
    row_spec = lambda w: pl.BlockSpec((1, tm, w), lambda i, j: (i, j, 0))
    full = lambda shp: pl.BlockSpec(shp, lambda i, j: (0,) * len(shp),
                                    pipeline_mode=pl.Buffered(1))
    return pl.pallas_call(
        _ffn_kernel,
        grid=(b, t // tm),
        in_specs=[row_spec(d), row_spec(attn.shape[2]), row_spec(conv.shape[2]),
                  pl.BlockSpec((1, N_MOD, d), lambda i, j: (i, 0, 0)),
                  full((1, d)), full(w_out.shape), full(w_ff1.shape), full(w_ff2.shape)],
        out_specs=row_spec(d),
        out_shape=jax.ShapeDtypeStruct((b, t, d), F32),
        compiler_params=_cparams(2),
        name="outproj_mlp",
    )(x, attn, conv, mod, norm2, w_out, w_ff1, w_ff2)


def _block_diag_ones(n):
    idx = np.arange(n) // HEAD_DIM
    return jnp.asarray(idx[:, None] == idx[None, :], dtype=BF16)


def _pack_w_in(w_in):
    d = w_in.shape[0]
    conv_w = d - ATTN_W
    sizes = [ATTN_W] + [KV_W] * 6 + [N_BRANCH * N_HEADS] + [conv_w] * 3
    offs = np.concatenate([[0], np.cumsum(sizes)])
    part = lambda i: w_in[:, offs[i]:offs[i + 1]]
    q, kc, vc, ks, vs, kw, vw, g, cgate, bgate, u = (part(i) for i in range(11))
    w_nat = jnp.concatenate([kc, vc, ks, kw, cgate, bgate, u], axis=1).astype(BF16)
    gt = g.reshape(d, N_KV, GQA, N_BRANCH).transpose(2, 3, 1, 0).reshape(GQA, N_BRANCH * N_KV, d)
    gt = jnp.pad(gt, ((0, 0), (0, SUBLANES - N_BRANCH * N_KV), (0, 0))).reshape(_G_ROWS, d)
    w_tr = jnp.concatenate([q.T, vs.T, vw.T, gt], axis=0).astype(BF16)
    return w_nat, w_tr


def _expand_w1(w1):
    hid = w1.shape[1]
    w = w1.reshape(2, CMP_STRIDE, 1, HEAD_DIM, 1, hid)
    eye = jnp.eye(N_KV, dtype=w1.dtype).reshape(1, 1, N_KV, 1, N_KV, 1)
    return (w * eye).reshape(2, CMP_STRIDE * KV_W, N_KV * hid).astype(BF16)


def _expand_w2(w2):
    hid = w2.shape[0]
    eye = jnp.eye(N_KV, dtype=w2.dtype).reshape(N_KV, 1, N_KV, 1)
    return (w2.reshape(1, hid, 1, HEAD_DIM) * eye).reshape(N_KV * hid, KV_W).astype(BF16)


def _expand_pe(pe):
    p = pe.reshape(2, CMP_STRIDE, 1, HEAD_DIM)
    return jnp.broadcast_to(p, (2, CMP_STRIDE, N_KV, HEAD_DIM)).reshape(2, CMP_STRIDE * KV_W)


def _bucket_thresholds():
    n = np.arange(2 * REL_MAX_DIST)
    max_exact = REL_BUCKETS // 2
    nf = np.maximum(n, max_exact).astype(np.float32)
    ratio = np.log(nf / np.float32(max_exact)) / np.float32(math.log(REL_MAX_DIST / max_exact))
    large = max_exact + (ratio * np.float32(REL_BUCKETS - max_exact)).astype(np.int32)
    table = np.where(n < max_exact, n, np.minimum(large, REL_BUCKETS - 1))
    return tuple(int(np.searchsorted(table, k, side="left")) for k in range(REL_BUCKETS))


def _bank_kernel(rows_ref, o_ref, *, thresholds, dist_fn):
    shape = o_ref.shape[-2:]
    row = lax.broadcasted_iota(jnp.int32, shape, 0)
    qi = lax.broadcasted_iota(jnp.int32, shape, 1) & (Q_BLOCK - 1)
    dist, ok = dist_fn(row, qi)
    v = jnp.broadcast_to(rows_ref[0, 0:1, :], shape)
    for k in range(1, REL_BUCKETS):
        v = jnp.where(dist >= thresholds[k], rows_ref[0, k:k + 1, :], v)
    o_ref[...] = jnp.where(ok, v, NEG).reshape(o_ref.shape)


def _bank_call(bias_rows, lead, n_tiles, dist_fn, name):
    grid = tuple(lead) + (N_KV, n_tiles)
    nl = len(lead)
    blk = (1,) * (nl + 2) + (SEL_BLOCK, GQ)
    thr = _bucket_thresholds()

    def body(rows_ref, o_ref):
        ids = [pl.program_id(a) for a in range(nl + 2)]
        fn = lambda row, qi: dist_fn(ids[:nl], ids[nl + 1], row, qi)
        _bank_kernel(rows_ref, o_ref, thresholds=thr, dist_fn=fn)

    return pl.pallas_call(
        body,
        grid=grid,
        in_specs=[pl.BlockSpec((1, REL_BUCKETS, GQ), lambda *i: (i[nl], 0, 0))],
        out_specs=pl.BlockSpec(blk, lambda *i: tuple(i) + (0, 0)),
        out_shape=jax.ShapeDtypeStruct(tuple(lead) + (N_KV, n_tiles, SEL_BLOCK, GQ), F32),
        compiler_params=_cparams(nl + 2),
        name=name,
    )(bias_rows)


def _bias_banks(rel_bias, t):
    n_cmp_rows = t // CMP_STRIDE
    rows = rel_bias.reshape(REL_BUCKETS, N_KV, GQA).transpose(1, 0, 2)
    rows = jnp.repeat(rows, Q_BLOCK, axis=2) * LOG2E

    def sel_dist(lead, tile, row, qi):
        dist = SEL_BLOCK * tile + qi - row
        return dist, dist >= 0

    sbank = _bank_call(rows, (), NEAR_TILES + 1, sel_dist, "bias_bank_sel")
    far = sbank[:, NEAR_TILES, 0:1, :]

    def win_dist(lead, tile, row, qi):
        dist = SEL_BLOCK * tile + qi - row
        return dist, (dist >= 0) & (dist < WINDOW)

    wbank = _bank_call(rows, (), WIN_BLOCKS + 2, win_dist, "bias_bank_win")

    cmp_c = n_cmp_rows - 4
    shifts = tuple(int((-(cmp_c - 4 * p)) % 8) for p in range(2))
    n_tiles = (cmp_c + n_cmp_rows + 8 + SEL_BLOCK - 1) // SEL_BLOCK

    def cmp_dist(lead, tile, row, qi):
        e = SEL_BLOCK * tile + row - jnp.where(lead[0] == 0, shifts[0], shifts[1])
        dist = qi - CMP_STRIDE * e + (CMP_STRIDE * cmp_c - (CMP_BLOCK - 1))
        return dist, (dist >= 0) & (e >= 0)

    ebank = _bank_call(rows, (2,), n_tiles, cmp_dist, "bias_bank_cmp")
    ebank = ebank.reshape(2, N_KV, n_tiles * SEL_BLOCK, GQ)
    return ebank, sbank, wbank, far, shifts


def _overlap_t(t):
    n_cmp_rows = t // CMP_STRIDE
    n_sb = t // SEL_BLOCK
    c_start = np.arange(n_cmp_rows)[None, :] * CMP_STRIDE
    s_start = np.arange(n_sb)[:, None] * SEL_BLOCK
    ov = np.clip(np.minimum(c_start + CMP_BLOCK, s_start + SEL_BLOCK)
                 - np.maximum(c_start, s_start), 0, None) / CMP_BLOCK
    ov[:, n_cmp_rows - 1] = 0.0
    return jnp.asarray(ov, dtype=BF16)


def _layer(x, c_pad, w_in, q_norm, k_norm, cmp_pe_k, cmp_w1_k, cmp_w2_k, cmp_pe_v, cmp_w1_v,
           cmp_w2_v, rel_bias, conv_w, w_out, norm1, norm2, w_ada, b_ada, w_ff1, w_ff2):
    b, t, d = x.shape
    scale = HEAD_DIM ** -0.5

    mod = _mod_call(c_pad, w_ada, b_ada)[:b].reshape(b, N_MOD, d)

    qn_col = (jnp.tile(q_norm, N_HEADS) * (scale * LOG2E)).reshape(ATTN_W, 1)
    kn_t = jnp.tile(k_norm, N_KV).reshape(1, KV_W)
    bdq = _block_diag_ones(ATTN_W)
    bdk = _block_diag_ones(KV_W)
    w_nat, w_tr = _pack_w_in(w_in)
    qt, kc_raw, vc_raw, ks, vst, kw, vwt, gt, conv = _inproj_call(
        x, mod, norm1.reshape(1, d), w_nat, w_tr, qn_col, kn_t, conv_w, bdq, bdk)

    n_cmp_rows = t // CMP_STRIDE
    kc, vct = _compress_call(
        kc_raw.reshape(b, n_cmp_rows, CMP_STRIDE * KV_W), vc_raw.reshape(b, n_cmp_rows, CMP_STRIDE * KV_W),
        _expand_pe(cmp_pe_k), _expand_pe(cmp_pe_v), _expand_w1(cmp_w1_k), _expand_w1(cmp_w1_v),
        _expand_w2(cmp_w2_k), _expand_w2(cmp_w2_v).T, kn_t, bdk)

    ebank, sbank, wbank, far, eb_shift = _bias_banks(rel_bias, t)
    attn = _attn_call(qt, kc, vct, ks, vst, kw, vwt, gt, _overlap_t(t), ebank, sbank, wbank, far,
                      eb_shift)

    return _ffn_call(x, attn, conv, mod, norm2.reshape(1, d), w_out.astype(BF16),
                     w_ff1.astype(BF16), w_ff2.astype(BF16))


def kernel(x, c, w_in, q_norm, k_norm, cmp_pe_k, cmp_w1_k, cmp_w2_k, cmp_pe_v, cmp_w1_v, cmp_w2_v,
           rel_bias, conv_w, w_out, norm1, norm2, w_ada, b_ada, w_ff1, w_ff2):
    b = x.shape[0]
    c_pad = jnp.pad(c, ((0, (-b) % 8), (0, 0)))
    for l in range(w_in.shape[0]):
        x = _layer(x, c_pad, w_in[l], q_norm[l], k_norm[l], cmp_pe_k[l], cmp_w1_k[l], cmp_w2_k[l],
                   cmp_pe_v[l], cmp_w1_v[l], cmp_w2_v[l], rel_bias, conv_w[l], w_out[l],
                   norm1[l], norm2[l], w_ada[l], b_ada[l], w_ff1[l], w_ff2[l])
    return x
```

```python
import functools
import math

import numpy as np
import jax
import jax.numpy as jnp
from jax import lax
from jax.experimental import pallas as pl
from jax.experimental.pallas import tpu as pltpu

HEAD_DIM = 64
N_HEADS = 8
N_KV = 2
GQA = N_HEADS // N_KV
ATTN_W = N_HEADS * HEAD_DIM
KV_W = N_KV * HEAD_DIM
CONV_K = 3
CMP_BLOCK = 32
CMP_STRIDE = 16
CMP_HIDDEN = 256
SEL_BLOCK = 64
N_SELECT = 16
WINDOW = 512
Q_BLOCK = 64
REL_BUCKETS = 32
REL_MAX_DIST = 1024
N_MOD = 6
N_BRANCH = 3
EPS = 1e-6
NEG = -1e30

LANES = 128
SUBLANES = 8
GQ = GQA * Q_BLOCK
KEY_CHUNK = 256
BLOCKS_PER_CHUNK = KEY_CHUNK // SEL_BLOCK
V_TILE = 128
BF16_ROWS = 16
V_ROWS = HEAD_DIM + BF16_ROWS
V_TILE_ROWS = N_KV * V_ROWS
WIN_BLOCKS = WINDOW // SEL_BLOCK
WIN_TILES = WINDOW // V_TILE + 1
NEAR_TILES = (REL_MAX_DIST + Q_BLOCK - 1) // SEL_BLOCK + 1
ROW_TILE = 512
INPROJ_SPLIT = 2
FAR_CHUNKS = 4
NEAR_CHUNKS = 2
QK_AHEAD = 4
LOG2E = math.log2(math.e)
FF_CHUNK = 1024
VMEM_LIMIT = 56 * 1024 * 1024

F32 = jnp.float32
BF16 = jnp.bfloat16
_NT = (((1,), (1,)), ((), ()))


def _cparams(n_axes):
    return pltpu.CompilerParams(dimension_semantics=("arbitrary",) * n_axes,
                                vmem_limit_bytes=VMEM_LIMIT)


def _with_ones_rows(vt):
    ones = jnp.ones((BF16_ROWS, vt.shape[1]), vt.dtype)
    parts = []
    for g in range(N_KV):
        parts += [vt[g * HEAD_DIM:(g + 1) * HEAD_DIM, :], ones]
    return jnp.concatenate(parts, axis=0)


def _swap_halves(p0, p1):
    low = lax.broadcasted_iota(jnp.int32, p0.shape, 1) < LANES // 2
    return (jnp.where(low, p0, pltpu.roll(p1, LANES // 2, 1)),
            jnp.where(low, pltpu.roll(p0, LANES // 2, 1), p1))


def _mod_kernel(c_ref, w_ref, b_ref, o_ref):
    c = c_ref[...]
    a = c * jax.nn.sigmoid(c)
    o_ref[...] = jnp.dot(a, w_ref[...], preferred_element_type=F32,
                         precision=lax.Precision.HIGHEST) + b_ref[...]


def _mod_call(c_pad, w_ada, b_ada):
    rows, d = c_pad.shape
    n = w_ada.shape[1]
    tn = 1024
    return pl.pallas_call(
        _mod_kernel,
        grid=(n // tn,),
        in_specs=[pl.BlockSpec((rows, d), lambda j: (0, 0)),
                  pl.BlockSpec((d, tn), lambda j: (0, j)),
                  pl.BlockSpec((1, tn), lambda j: (0, j))],
        out_specs=pl.BlockSpec((rows, tn), lambda j: (0, j)),
        out_shape=jax.ShapeDtypeStruct((rows, n), F32),
        compiler_params=_cparams(1),
        name="adaln_mod",
    )(c_pad, w_ada, b_ada.reshape(1, n))


_N_KC, _N_VC, _N_KS, _N_KW, _N_CONV = 0, KV_W, 2 * KV_W, 3 * KV_W, 4 * KV_W
_T_Q, _T_VS, _T_VW, _T_G = 0, ATTN_W, ATTN_W + KV_W, ATTN_W + 2 * KV_W
_G_ROWS = GQA * SUBLANES


def _inproj_kernel(x_ref, mod_ref, n1_ref, wn_ref, wt_ref, qn_ref, kn_ref, cw_ref, bdq_ref, bdk_ref,
                   qt_out, kc_out, vc_out, ks_out, vst_out, kw_out, vwt_out, gt_out, conv_out,
                   carry, *, conv_w):
    t = pl.program_id(1)
    tm = x_ref.shape[1]
    sub = tm // INPROJ_SPLIT

    @pl.when(t == 0)
    def _():
        carry[...] = jnp.zeros_like(carry)

    prev2, prev1 = carry[6:7, :], carry[7:8, :]
    c_bg = _N_CONV + conv_w
    c_u = c_bg + conv_w

    for part in range(INPROJ_SPLIT):
        rows = slice(part * sub, (part + 1) * sub)
        x = x_ref[0, rows, :]
        ms = jnp.mean(x * x, axis=-1, keepdims=True)
        y = x * lax.rsqrt(ms + EPS) * n1_ref[...]
        h = (y * (1.0 + mod_ref[0, 1:2, :]) + mod_ref[0, 0:1, :]).astype(BF16)

        def proj(a, b):
            return jnp.dot(h, wn_ref[:, a:b], preferred_element_type=F32)

        def proj_t(a, b):
            return lax.dot_general(wt_ref[a:b, :], h, _NT, preferred_element_type=F32)

        def head_norm(v, gain):
            ssq = jnp.dot((v * v).astype(BF16), bdk_ref[...], preferred_element_type=F32)
            return v * lax.rsqrt(ssq * (1.0 / HEAD_DIM) + EPS) * gain

        qf = proj_t(_T_Q, _T_VS)
        ssq = jnp.dot(bdq_ref[...], (qf * qf).astype(BF16), preferred_element_type=F32)
        qf = qf * lax.rsqrt(ssq * (1.0 / HEAD_DIM) + EPS) * qn_ref[...]
        vg = proj_t(_T_VS, _T_G + _G_ROWS)
        gf = jax.nn.sigmoid(vg[2 * KV_W:, :])
        for c in range(sub // LANES):
            blk = (part * sub) // Q_BLOCK + 2 * c
            cols = slice(c * LANES, (c + 1) * LANES)
            for g in range(N_KV):
                pc = [qf[(g * GQA + r) * HEAD_DIM:(g * GQA + r + 1) * HEAD_DIM, cols]
                      for r in range(GQA)]
                lo01, hi01 = _swap_halves(pc[0], pc[1])
                lo23, hi23 = _swap_halves(pc[2], pc[3])
                qt_out[0, blk, g] = jnp.concatenate([lo01, lo23], axis=1).astype(BF16)
                qt_out[0, blk + 1, g] = jnp.concatenate([hi01, hi23], axis=1).astype(BF16)
            pc = [gf[r * SUBLANES:(r + 1) * SUBLANES, cols] for r in range(GQA)]
            lo01, hi01 = _swap_halves(pc[0], pc[1])
            lo23, hi23 = _swap_halves(pc[2], pc[3])
            gt_out[0, blk] = jnp.concatenate([lo01, lo23], axis=1)
            gt_out[0, blk + 1] = jnp.concatenate([hi01, hi23], axis=1)

        vs_f = _with_ones_rows(vg[:KV_W, :].astype(BF16))
        vw_f = _with_ones_rows(vg[KV_W:2 * KV_W, :].astype(BF16))
        for j in range(sub // KEY_CHUNK):
            vst_out[0, (part * sub) // KEY_CHUNK + j] = vs_f[:, j * KEY_CHUNK:(j + 1) * KEY_CHUNK]
        for j in range(sub // V_TILE):
            vwt_out[0, (part * sub) // V_TILE + j] = vw_f[:, j * V_TILE:(j + 1) * V_TILE]

        kv = proj(_N_KC, _N_CONV)
        kc_out[0, rows, :] = kv[:, _N_KC:_N_VC]
        vc_out[0, rows, :] = kv[:, _N_VC:_N_KS]
        ks_out[0, rows, :] = head_norm(kv[:, _N_KS:_N_KW], kn_ref[...]).astype(BF16)
        kw_out[0, rows, :] = head_norm(kv[:, _N_KW:_N_CONV], kn_ref[...]).astype(BF16)

        z = proj(_N_CONV, c_bg) * proj(c_u, c_u + conv_w)
        row = lax.broadcasted_iota(jnp.int32, z.shape, 0)
        z1 = jnp.where(row == 0, prev1, pltpu.roll(z, 1, 0))
        z2 = jnp.where(row == 0, prev2, jnp.where(row == 1, prev1, pltpu.roll(z, 2, 0)))
        zc = cw_ref[0:1, :] * z2 + cw_ref[1:2, :] * z1 + cw_ref[2:3, :] * z
        conv_out[0, rows, :] = (proj(c_bg, c_u) * zc).astype(BF16)
        prev2, prev1 = z[sub - 2:sub - 1, :], z[sub - 1:sub, :]
        if part == INPROJ_SPLIT - 1:
            carry[...] = z[sub - SUBLANES:sub, :]


def _inproj_call(x, mod, norm1, w_nat, w_tr, qn_col, kn_t, conv_w, bdq, bdk):
    b, t, d = x.shape
    tm = min(ROW_TILE, t)
    cw = conv_w.shape[1]
    nq = t // Q_BLOCK
    row_spec = lambda w: pl.BlockSpec((1, tm, w), lambda i, j: (i, j, 0))
    full = lambda shp: pl.BlockSpec(shp, lambda i, j: (0,) * len(shp))
    vt_spec = lambda w: pl.BlockSpec((1, tm // w, V_TILE_ROWS, w), lambda i, j: (i, j, 0, 0))
    vt_shape = lambda w: jax.ShapeDtypeStruct((b, t // w, V_TILE_ROWS, w), BF16)
    kv = lambda dt: jax.ShapeDtypeStruct((b, t, KV_W), dt)
    out_specs = [pl.BlockSpec((1, tm // Q_BLOCK, N_KV, HEAD_DIM, GQ), lambda i, j: (i, j, 0, 0, 0)),
                 row_spec(KV_W), row_spec(KV_W), row_spec(KV_W), vt_spec(KEY_CHUNK), row_spec(KV_W),
                 vt_spec(V_TILE),
                 pl.BlockSpec((1, tm // Q_BLOCK, SUBLANES, GQ), lambda i, j: (i, j, 0, 0)),
                 row_spec(cw)]
    out_shape = [jax.ShapeDtypeStruct((b, nq, N_KV, HEAD_DIM, GQ), BF16),
                 kv(F32), kv(F32), kv(BF16), vt_shape(KEY_CHUNK), kv(BF16), vt_shape(V_TILE),
                 jax.ShapeDtypeStruct((b, nq, SUBLANES, GQ), F32),
                 jax.ShapeDtypeStruct((b, t, cw), BF16)]
    return pl.pallas_call(
        functools.partial(_inproj_kernel, conv_w=cw),
        grid=(b, t // tm),
        in_specs=[row_spec(d),
                  pl.BlockSpec((1, N_MOD, d), lambda i, j: (i, 0, 0)),
                  full((1, d)), full(w_nat.shape), full(w_tr.shape), full((ATTN_W, 1)),
                  full((1, KV_W)), full((CONV_K, cw)), full((ATTN_W, ATTN_W)), full((KV_W, KV_W))],
        out_specs=out_specs,
        out_shape=out_shape,
        scratch_shapes=[pltpu.VMEM((SUBLANES, cw), F32)],
        compiler_params=_cparams(2),
        name="inproj",
    )(x, mod, norm1, w_nat, w_tr, qn_col, kn_t, conv_w, bdq, bdk)


def _compress_kernel(kx_ref, vx_ref, pek_ref, pev_ref, w1k_ref, w1v_ref, w2k_ref, w2vt_ref,
                     kn_ref, bdk_ref, kc_out, vct_out):
    def hidden(x_ref, pe_ref, w1_ref):
        xv = x_ref[0]
        u = jnp.dot((xv + pe_ref[0:1, :]).astype(BF16), w1_ref[0], preferred_element_type=F32)
        v = jnp.dot((xv + pe_ref[1:2, :]).astype(BF16), w1_ref[1], preferred_element_type=F32)
        n = u.shape[0]
        hid = u + pltpu.roll(v, n - 1, 0)
        return jax.nn.gelu(hid, approximate=True).astype(BF16)

    kc = jnp.dot(hidden(kx_ref, pek_ref, w1k_ref), w2k_ref[...], preferred_element_type=F32)
    ssq = jnp.dot((kc * kc).astype(BF16), bdk_ref[...], preferred_element_type=F32)
    kc_out[0] = (kc * lax.rsqrt(ssq * (1.0 / HEAD_DIM) + EPS) * kn_ref[...]).astype(BF16)
    vct_out[0] = _with_ones_rows(lax.dot_general(w2vt_ref[...], hidden(vx_ref, pev_ref, w1v_ref), _NT,
                                                 preferred_element_type=F32).astype(BF16))


def _compress_call(kx, vx, pek, pev, w1k, w1v, w2k, w2vt, kn_t, bdk):
    b, nrow, wide = kx.shape
    hid2 = w1k.shape[2]
    full = lambda shp: pl.BlockSpec(shp, lambda i: (0,) * len(shp))
    xs = pl.BlockSpec((1, nrow, wide), lambda i: (i, 0, 0))
    return pl.pallas_call(
        _compress_kernel,
        grid=(b,),
        in_specs=[xs, xs, full((2, wide)), full((2, wide)), full((2, wide, hid2)),
                  full((2, wide, hid2)), full((hid2, KV_W)), full((KV_W, hid2)),
                  full((1, KV_W)), full((KV_W, KV_W))],
        out_specs=[pl.BlockSpec((1, nrow, KV_W), lambda i: (i, 0, 0)),
                   pl.BlockSpec((1, V_TILE_ROWS, nrow), lambda i: (i, 0, 0))],
        out_shape=[jax.ShapeDtypeStruct((b, nrow, KV_W), BF16),
                   jax.ShapeDtypeStruct((b, V_TILE_ROWS, nrow), BF16)],
        compiler_params=_cparams(1),
        name="compress",
    )(kx, vx, pek, pev, w1k, w1v, w2k, w2vt, kn_t, bdk)


def _attn_kernel(q_ref, kc_ref, vct_ref, ks_ref, vst_ref, kw_ref, vwt_ref, gt_ref, ovt_ref,
                 eb_ref, sb_ref, wb_ref, far_ref, o_ref, sel_scr, acc_scr, out_scr, qp_scr, s_scr,
                 fin_scr, *, n_cmp_rows, eb_shift):
    n_sb = sel_scr.shape[1]
    ci = jnp.minimum(pl.program_id(1), n_sb - 1)
    par = lax.rem(ci, 2)
    n_total = ks_ref.shape[1] // KEY_CHUNK
    vsl = lambda g: slice(g * V_ROWS, (g + 1) * V_ROWS)

    @pl.when((pl.program_id(0) == 0) & (pl.program_id(1) == 0))
    def _():
        def zero(g, carry):
            fin_scr[g] = jnp.zeros(fin_scr.shape[1:], F32)
            return carry
        lax.fori_loop(0, N_KV, zero, 0)

    pieces = []
    for g in range(N_KV):
        for half in range(GQ // LANES):
            a = fin_scr[g, :, half * LANES:(half + 1) * LANES]
            stacked = jnp.concatenate([a, pltpu.roll(a, Q_BLOCK, 1)], axis=0)
            pieces.append(stacked.T[:Q_BLOCK, :])
    o_ref[0] = jnp.concatenate(pieces, axis=1).astype(BF16)

    zeros_q = jnp.zeros((HEAD_DIM, GQ), BF16)
    qp_scr[0] = jnp.concatenate([q_ref[0, 0, 0], zeros_q], axis=0)
    qp_scr[1] = jnp.concatenate([zeros_q, q_ref[0, 0, 1]], axis=0)

    def chunk_qk(g, c):
        kk = ks_ref[0, pl.ds(pl.multiple_of(c * KEY_CHUNK, KEY_CHUNK), KEY_CHUNK), :]
        return jnp.dot(kk, qp_scr[g], preferred_element_type=F32)

    n_chunks = ci // BLOCKS_PER_CHUNK + 1
    n_far = jnp.maximum(ci - (NEAR_TILES - 1), 0) // (BLOCKS_PER_CHUNK * FAR_CHUNKS)
    c0 = n_far * FAR_CHUNKS
    n_near = (n_chunks - c0 + NEAR_CHUNKS - 1) // NEAR_CHUNKS

    def near_units(i):
        return [(g, jnp.minimum(c0 + NEAR_CHUNKS * i + j, n_total - 1))
                for g in range(N_KV) for j in range(NEAR_CHUNKS)]

    def far_units(i):
        return [(g, i * FAR_CHUNKS + j) for g in range(N_KV) for j in range(FAR_CHUNKS)]

    def prefetch_scores(units):
        for k in range(QK_AHEAD):
            s_scr[k] = chunk_qk(*units[k])

    lane = lax.broadcasted_iota(jnp.int32, (n_sb, LANES), 1)
    cmp_c = n_cmp_rows - 4
    e0 = cmp_c - 4 * ci + jnp.where(par == 0, eb_shift[0], eb_shift[1])
    e0 = pl.multiple_of(e0, 8)
    o_cmp = []
    imp = []
    w0 = ci // 2 - (WIN_TILES - 1)
    win_tiles = [jnp.maximum(w0 + j, 0) for j in range(WIN_TILES)]
    s_cmp = [jnp.dot(kc_ref[0], qp_scr[g], preferred_element_type=F32) for g in range(N_KV)]
    for g in range(N_KV):
        sc = s_cmp[g] + eb_ref[par, g, pl.ds(e0, n_cmp_rows), :]
        m = jnp.max(sc, axis=0, keepdims=True)
        e = jnp.exp2(sc - m).astype(BF16)
        ov = jnp.dot(vct_ref[0, vsl(g), :], e, preferred_element_type=F32)
        inv = jnp.where(m > 0.5 * NEG, 1.0 / ov[HEAD_DIM:HEAD_DIM + 1, :], 0.0)
        o_cmp.append(ov[:HEAD_DIM, :] * inv)
        ir = jnp.dot(ovt_ref[...], e, preferred_element_type=F32) * inv
        a = ir[:, :LANES] + ir[:, LANES:]
        imp.append(a + pltpu.roll(a, Q_BLOCK, 1))

    s_win = [[jnp.dot(kw_ref[0, pl.ds(pl.multiple_of(tj * V_TILE, V_TILE), V_TILE), :], qp_scr[g],
                      preferred_element_type=F32) for tj in win_tiles] for g in range(N_KV)]

    first_near = near_units(0)

    def next_units(use_far, far_list):
        return [(jnp.where(use_far, g_f, g_n), jnp.where(use_far, jnp.minimum(c_f, n_total - 1), c_n))
                for (g_f, c_f), (g_n, c_n) in zip(_far_first(far_list), first_near)]

    prefetch_scores(next_units(n_far > 0, far_units(0)))

    jidx = lax.broadcasted_iota(jnp.int32, (n_sb, LANES), 0)
    jf = jidx.astype(F32)
    valid = jidx <= ci
    forced = (jidx == 0) | (jidx == ci) | (jidx == ci - 1)
    score = jnp.where(valid, jnp.where(forced, -2.0, jnp.where(lane < Q_BLOCK, imp[0], imp[1])),
                      -1.0)
    sel = jnp.where(forced, 1.0, 0.0)

    def first_max(score):
        pairs = [(score[r:r + SUBLANES], jf[r:r + SUBLANES]) for r in range(0, n_sb, SUBLANES)]
        while len(pairs) > 1:
            nxt = []
            for (va, ia), (vb, ib) in zip(pairs[0::2], pairs[1::2]):
                keep = va >= vb
                nxt.append((jnp.where(keep, va, vb), jnp.where(keep, ia, ib)))
            pairs = nxt + pairs[len(pairs) - len(pairs) % 2:]
        v8, i8 = pairs[0]
        mx = jnp.max(v8, axis=0, keepdims=True)
        return jnp.min(jnp.where(v8 == mx, i8, float(n_sb)), axis=0, keepdims=True)

    for _ in range(min(N_SELECT, n_sb) - 3):
        first = first_max(score)
        hit = jf == first
        sel = jnp.where(hit, 1.0, sel)
        score = jnp.where(hit, -2.0, score)
    selneg = jnp.where((sel > 0.5) & valid, 0.0, NEG)
    swapped = pltpu.roll(selneg, Q_BLOCK, 1)
    left = jnp.where(lane < Q_BLOCK, selneg, swapped)
    right = jnp.where(lane < Q_BLOCK, swapped, selneg)
    sel_scr[0] = jnp.concatenate([left, left], axis=1)
    sel_scr[1] = jnp.concatenate([right, right], axis=1)

    for g in range(N_KV):
        parts = []
        for i in range(2 * WIN_TILES):
            delta = par + WIN_BLOCKS - i
            ok = (delta >= 0) & (delta <= WIN_BLOCKS) & (delta <= ci)
            tile = jnp.where(ok, delta, WIN_BLOCKS + 1)
            half = s_win[g][i // 2][(i % 2) * SEL_BLOCK:(i % 2 + 1) * SEL_BLOCK, :]
            parts.append(half + wb_ref[g, tile])
        s = jnp.concatenate(parts, axis=0)
        m = jnp.max(s, axis=0, keepdims=True)
        pb = jnp.exp2(s - m).astype(BF16)
        o_win = jnp.zeros((V_ROWS, GQ), F32)
        for j, tj in enumerate(win_tiles):
            o_win = o_win + jnp.dot(vwt_ref[0, tj, vsl(g), :], pb[j * V_TILE:(j + 1) * V_TILE, :],
                                    preferred_element_type=F32)
        w_scale = gt_ref[0, 0, 2 * N_KV + g:2 * N_KV + g + 1, :] * (1.0 / o_win[HEAD_DIM:HEAD_DIM + 1, :])
        out_scr[g] = gt_ref[0, 0, g:g + 1, :] * o_cmp[g] + w_scale * o_win[:HEAD_DIM, :]
        acc_scr[g] = jnp.zeros((V_ROWS, GQ), F32)

    def chunk_softmax(s, g, c, near, extra):
        parts = []
        for i in range(BLOCKS_PER_CHUNK):
            kb = c * BLOCKS_PER_CHUNK + i
            blk = s[i * SEL_BLOCK:(i + 1) * SEL_BLOCK, :]
            mrow = sel_scr[g, pl.ds(kb, 1), :]
            if extra is not None:
                mrow = mrow + extra
            if near:
                tile = jnp.clip(ci - kb, 0, NEAR_TILES)
                blk = blk + sb_ref[g, tile] + mrow
            else:
                blk = blk + (mrow + far_ref[g])
            parts.append(blk)
        s = jnp.concatenate(parts, axis=0)
        m_c = jnp.max(s, axis=0, keepdims=True)
        return m_c, jnp.exp2(s - m_c).astype(BF16)

    def chunk_pv(pb, g, c):
        return jnp.dot(vst_ref[0, c, vsl(g), :], pb, preferred_element_type=F32)

    def merge_step(units, extras, near, next_units, carry):
        scores = {k: s_scr[k] for k in range(QK_AHEAD)}
        results = []
        for k, (g, c) in enumerate(units):
            m_c, pb = chunk_softmax(scores.pop(k), g, c, near, extras[k])
            if k + QK_AHEAD < len(units):
                scores[k + QK_AHEAD] = chunk_qk(*units[k + QK_AHEAD])
            else:
                j = k + QK_AHEAD - len(units)
                s_scr[j] = chunk_qk(*next_units[j])
            results.append((m_c, chunk_pv(pb, g, c)))
        per_g = len(units) // N_KV
        new = []
        for g in range(N_KV):
            m_run = carry[g]
            stats = results[g * per_g:(g + 1) * per_g]
            m_new = m_run
            for m_c, _ in stats:
                m_new = jnp.maximum(m_new, m_c)
            acc = jnp.exp2(m_run - m_new) * acc_scr[g]
            for m_c, pv in stats:
                acc = acc + jnp.exp2(m_c - m_new) * pv
            acc_scr[g] = acc
            new.append(m_new)
        return tuple(new)

    def far_step(i, carry):
        units = far_units(i)
        nxt = next_units(i + 1 < n_far, far_units(i + 1))
        return merge_step(units, [None] * len(units), False, nxt, carry)

    def near_step(i, carry):
        units = near_units(i)
        extras = [None if j == 0 else jnp.where(c0 + NEAR_CHUNKS * i + j < n_chunks, 0.0, NEG)
                  for _ in range(N_KV) for j in range(NEAR_CHUNKS)]
        return merge_step(units, extras, True, near_units(i + 1), carry)

    init = (jnp.full((1, GQ), NEG, F32),) * N_KV
    carry = lax.fori_loop(0, n_far, far_step, init)
    lax.fori_loop(0, n_near, near_step, carry)

    for g in range(N_KV):
        acc = acc_scr[g]
        scale = gt_ref[0, 0, N_KV + g:N_KV + g + 1, :] * (1.0 / acc[HEAD_DIM:HEAD_DIM + 1, :])
        fin_scr[g] = out_scr[g] + scale * acc[:HEAD_DIM, :]


def _far_first(units):
    assert QK_AHEAD == N_KV * NEAR_CHUNKS and FAR_CHUNKS >= QK_AHEAD
    return units[:QK_AHEAD]


def _attn_call(qt, kc, vct, ks, vst, kw, vwt, gt, ovt, ebank, sbank, wbank, far, eb_shift):
    b, nq = qt.shape[0], qt.shape[1]
    t = ks.shape[1]
    n_cmp_rows = kc.shape[1]
    n_sb = t // SEL_BLOCK
    per_b = lambda shp: pl.BlockSpec((1,) + shp[1:], lambda i, j: (i,) + (0,) * (len(shp) - 1))
    per_q = lambda shp: pl.BlockSpec(
        (1, 1) + shp[2:], lambda i, j: (i, jnp.minimum(j, nq - 1)) + (0,) * (len(shp) - 2))
    full = lambda shp: pl.BlockSpec(shp, lambda i, j: (0,) * len(shp))
    args = (qt, kc, vct, ks, vst, kw, vwt, gt, ovt, ebank, sbank, wbank, far)
    specs = [per_q(qt.shape), per_b(kc.shape), per_b(vct.shape), per_b(ks.shape), per_b(vst.shape),
             per_b(kw.shape), per_b(vwt.shape), per_q(gt.shape), full(ovt.shape),
             full(ebank.shape), full(sbank.shape), full(wbank.shape), full(far.shape)]
    acc_like = pltpu.VMEM((N_KV, HEAD_DIM, GQ), F32)
    return pl.pallas_call(
        functools.partial(_attn_kernel, n_cmp_rows=n_cmp_rows, eb_shift=eb_shift),
        grid=(b, nq + 1),
        in_specs=specs,
        out_specs=pl.BlockSpec((1, Q_BLOCK, ATTN_W), lambda i, j: (i, jnp.maximum(j - 1, 0), 0)),
        out_shape=jax.ShapeDtypeStruct((b, t, ATTN_W), BF16),
        scratch_shapes=[pltpu.VMEM((N_KV, n_sb, GQ), F32),
                        pltpu.VMEM((N_KV, V_ROWS, GQ), F32),
                        acc_like,
                        pltpu.VMEM((N_KV, KV_W, GQ), BF16),
                        pltpu.VMEM((QK_AHEAD, KEY_CHUNK, GQ), F32),
                        acc_like],
        compiler_params=_cparams(2),
        name="nsa_attention",
    )(*args)


def _ffn_kernel(x_ref, a_ref, c_ref, mod_ref, n2_ref, wo_ref, w1_ref, w2_ref, o_ref):
    aw = a_ref.shape[2]
    mix = jnp.dot(a_ref[0], wo_ref[0:aw, :], preferred_element_type=F32)
    mix = mix + jnp.dot(c_ref[0], wo_ref[aw:, :], preferred_element_type=F32)
    x1 = x_ref[0] + mod_ref[0, 2:3, :] * mix
    ms = jnp.mean(x1 * x1, axis=-1, keepdims=True)
    y = x1 * lax.rsqrt(ms + EPS) * n2_ref[...]
    h2 = (y * (1.0 + mod_ref[0, 4:5, :]) + mod_ref[0, 3:4, :]).astype(BF16)
    d_ff = w1_ref.shape[1]
    ff = jnp.zeros(x1.shape, F32)
    for j in range(d_ff // FF_CHUNK):
        a = jnp.dot(h2, w1_ref[:, j * FF_CHUNK:(j + 1) * FF_CHUNK], preferred_element_type=F32)
        a = jnp.maximum(a, 0.0)
        ff = ff + jnp.dot((a * a).astype(BF16), w2_ref[j * FF_CHUNK:(j + 1) * FF_CHUNK, :],
                          preferred_element_type=F32)
    o_ref[0] = x1 + mod_ref[0, 5:6, :] * ff


def _ffn_call(x, attn, conv, mod, norm2, w_out, w_ff1, w_ff2):
    b, t, d = x.shape
    tm = min(ROW_TILE, t)
    row_spec = lambda w: pl.BlockSpec((1, tm, w), lambda i, j: (i, j, 0))
    full = lambda shp: pl.BlockSpec(shp, lambda i, j: (0,) * len(shp),
                                    pipeline_mode=pl.Buffered(1))
    return pl.pallas_call(
        _ffn_kernel,
        grid=(b, t // tm),
        in_specs=[row_spec(d), row_spec(attn.shape[2]), row_spec(conv.shape[2]),
                  pl.BlockSpec((1, N_MOD, d), lambda i, j: (i, 0, 0)),
                  full((1, d)), full(w_out.shape), full(w_ff1.shape), full(w_ff2.shape)],
        out_specs=row_spec(d),
        out_shape=jax.ShapeDtypeStruct((b, t, d), F32),
        compiler_params=_cparams(2),
        name="outproj_mlp",
    )(x, attn, conv, mod, norm2, w_out, w_ff1, w_ff2)


def _block_diag_ones(n):
    idx = np.arange(n) // HEAD_DIM
    return jnp.asarray(idx[:, None] == idx[None, :], dtype=BF16)


def _pack_w_in(w_in):
    d = w_in.shape[0]
    conv_w = d - ATTN_W
    sizes = [ATTN_W] + [KV_W] * 6 + [N_BRANCH * N_HEADS] + [conv_w] * 3
    offs = np.concatenate([[0], np.cumsum(sizes)])
    part = lambda i: w_in[:, offs[i]:offs[i + 1]]
    q, kc, vc, ks, vs, kw, vw, g, cgate, bgate, u = (part(i) for i in range(11))
    w_nat = jnp.concatenate([kc, vc, ks, kw, cgate, bgate, u], axis=1).astype(BF16)
    gt = g.reshape(d, N_KV, GQA, N_BRANCH).transpose(2, 3, 1, 0).reshape(GQA, N_BRANCH * N_KV, d)
    gt = jnp.pad(gt, ((0, 0), (0, SUBLANES - N_BRANCH * N_KV), (0, 0))).reshape(_G_ROWS, d)
    w_tr = jnp.concatenate([q.T, vs.T, vw.T, gt], axis=0).astype(BF16)
    return w_nat, w_tr


def _expand_w1(w1):
    hid = w1.shape[1]
    w = w1.reshape(2, CMP_STRIDE, 1, HEAD_DIM, 1, hid)
    eye = jnp.eye(N_KV, dtype=w1.dtype).reshape(1, 1, N_KV, 1, N_KV, 1)
    return (w * eye).reshape(2, CMP_STRIDE * KV_W, N_KV * hid).astype(BF16)


def _expand_w2(w2):
    hid = w2.shape[0]
    eye = jnp.eye(N_KV, dtype=w2.dtype).reshape(N_KV, 1, N_KV, 1)
    return (w2.reshape(1, hid, 1, HEAD_DIM) * eye).reshape(N_KV * hid, KV_W).astype(BF16)


def _expand_pe(pe):
    p = pe.reshape(2, CMP_STRIDE, 1, HEAD_DIM)
    return jnp.broadcast_to(p, (2, CMP_STRIDE, N_KV, HEAD_DIM)).reshape(2, CMP_STRIDE * KV_W)


def _bucket_thresholds():
    n = np.arange(2 * REL_MAX_DIST)
    max_exact = REL_BUCKETS // 2
    nf = np.maximum(n, max_exact).astype(np.float32)
    ratio = np.log(nf / np.float32(max_exact)) / np.float32(math.log(REL_MAX_DIST / max_exact))
    large = max_exact + (ratio * np.float32(REL_BUCKETS - max_exact)).astype(np.int32)
    table = np.where(n < max_exact, n, np.minimum(large, REL_BUCKETS - 1))
    return tuple(int(np.searchsorted(table, k, side="left")) for k in range(REL_BUCKETS))


def _bank_kernel(rows_ref, o_ref, *, thresholds, dist_fn):
    shape = o_ref.shape[-2:]
    row = lax.broadcasted_iota(jnp.int32, shape, 0)
    qi = lax.broadcasted_iota(jnp.int32, shape, 1) & (Q_BLOCK - 1)
    dist, ok = dist_fn(row, qi)
    v = jnp.broadcast_to(rows_ref[0, 0:1, :], shape)
    for k in range(1, REL_BUCKETS):
        v = jnp.where(dist >= thresholds[k], rows_ref[0, k:k + 1, :], v)
    o_ref[...] = jnp.where(ok, v, NEG).reshape(o_ref.shape)


def _bank_call(bias_rows, lead, n_tiles, dist_fn, name):
    grid = tuple(lead) + (N_KV, n_tiles)
    nl = len(lead)
    blk = (1,) * (nl + 2) + (SEL_BLOCK, GQ)
    thr = _bucket_thresholds()

    def body(rows_ref, o_ref):
        ids = [pl.program_id(a) for a in range(nl + 2)]
        fn = lambda row, qi: dist_fn(ids[:nl], ids[nl + 1], row, qi)
        _bank_kernel(rows_ref, o_ref, thresholds=thr, dist_fn=fn)

    return pl.pallas_call(
        body,
        grid=grid,
        in_specs=[pl.BlockSpec((1, REL_BUCKETS, GQ), lambda *i: (i[nl], 0, 0))],
        out_specs=pl.BlockSpec(blk, lambda *i: tuple(i) + (0, 0)),
        out_shape=jax.ShapeDtypeStruct(tuple(lead) + (N_KV, n_tiles, SEL_BLOCK, GQ), F32),
        compiler_params=_cparams(nl + 2),
        name=name,
    )(bias_rows)


def _bias_banks(rel_bias, t):
    n_cmp_rows = t // CMP_STRIDE
    rows = rel_bias.reshape(REL_BUCKETS, N_KV, GQA).transpose(1, 0, 2)
    rows = jnp.repeat(rows, Q_BLOCK, axis=2) * LOG2E

    def sel_dist(lead, tile, row, qi):
        dist = SEL_BLOCK * tile + qi - row
        return dist, dist >= 0

    sbank = _bank_call(rows, (), NEAR_TILES + 1, sel_dist, "bias_bank_sel")
    far = sbank[:, NEAR_TILES, 0:1, :]

    def win_dist(lead, tile, row, qi):
        dist = SEL_BLOCK * tile + qi - row
        return dist, (dist >= 0) & (dist < WINDOW)

    wbank = _bank_call(rows, (), WIN_BLOCKS + 2, win_dist, "bias_bank_win")

    cmp_c = n_cmp_rows - 4
    shifts = tuple(int((-(cmp_c - 4 * p)) % 8) for p in range(2))
    n_tiles = (cmp_c + n_cmp_rows + 8 + SEL_BLOCK - 1) // SEL_BLOCK

    def cmp_dist(lead, tile, row, qi):
        e = SEL_BLOCK * tile + row - jnp.where(lead[0] == 0, shifts[0], shifts[1])
        dist = qi - CMP_STRIDE * e + (CMP_STRIDE * cmp_c - (CMP_BLOCK - 1))
        return dist, (dist >= 0) & (e >= 0)

    ebank = _bank_call(rows, (2,), n_tiles, cmp_dist, "bias_bank_cmp")
    ebank = ebank.reshape(2, N_KV, n_tiles * SEL_BLOCK, GQ)
    return ebank, sbank, wbank, far, shifts


def _overlap_t(t):
    n_cmp_rows = t // CMP_STRIDE
    n_sb = t // SEL_BLOCK
    c_start = np.arange(n_cmp_rows)[None, :] * CMP_STRIDE
    s_start = np.arange(n_sb)[:, None] * SEL_BLOCK
    ov = np.clip(np.minimum(c_start + CMP_BLOCK, s_start + SEL_BLOCK)
                 - np.maximum(c_start, s_start), 0, None) / CMP_BLOCK
    ov[:, n_cmp_rows - 1] = 0.0
    return jnp.asarray(ov, dtype=BF16)


def _layer(x, c_pad, w_in, q_norm, k_norm, cmp_pe_k, cmp_w1_k, cmp_w2_k, cmp_pe_v, cmp_w1_v,
           cmp_w2_v, rel_bias, conv_w, w_out, norm1, norm2, w_ada, b_ada, w_ff1, w_ff2):
    b, t, d = x.shape
    scale = HEAD_DIM ** -0.5

    mod = _mod_call(c_pad, w_ada, b_ada)[:b].reshape(b, N_MOD, d)

    qn_col = (jnp.tile(q_norm, N_HEADS) * (scale * LOG2E)).reshape(ATTN_W, 1)
    kn_t = jnp.tile(k_norm, N_KV).reshape(1, KV_W)
    bdq = _block_diag_ones(ATTN_W)
    bdk = _block_diag_ones(KV_W)
    w_nat, w_tr = _pack_w_in(w_in)
    qt, kc_raw, vc_raw, ks, vst, kw, vwt, gt, conv = _inproj_call(
        x, mod, norm1.reshape(1, d), w_nat, w_tr, qn_col, kn_t, conv_w, bdq, bdk)

    n_cmp_rows = t // CMP_STRIDE
    kc, vct = _compress_call(
        kc_raw.reshape(b, n_cmp_rows, CMP_STRIDE * KV_W), vc_raw.reshape(b, n_cmp_rows, CMP_STRIDE * KV_W),
        _expand_pe(cmp_pe_k), _expand_pe(cmp_pe_v), _expand_w1(cmp_w1_k), _expand_w1(cmp_w1_v),
        _expand_w2(cmp_w2_k), _expand_w2(cmp_w2_v).T, kn_t, bdk)

    ebank, sbank, wbank, far, eb_shift = _bias_banks(rel_bias, t)
    attn = _attn_call(qt, kc, vct, ks, vst, kw, vwt, gt, _overlap_t(t), ebank, sbank, wbank, far,
                      eb_shift)

    return _ffn_call(x, attn, conv, mod, norm2.reshape(1, d), w_out.astype(BF16),
                     w_ff1.astype(BF16), w_ff2.astype(BF16))


def kernel(x, c, w_in, q_norm, k_norm, cmp_pe_k, cmp_w1_k, cmp_w2_k, cmp_pe_v, cmp_w1_v, cmp_w2_v,
           rel_bias, conv_w, w_out, norm1, norm2, w_ada, b_ada, w_ff1, w_ff2):
    b = x.shape[0]
    c_pad = jnp.pad(c, ((0, (-b) % 8), (0, 0)))
    for l in range(w_in.shape[0]):
        x = _layer(x, c_pad, w_in[l], q_norm[l], k_norm[l], cmp_pe_k[l], cmp_w1_k[l], cmp_w2_k[l],
                   cmp_pe_v[l], cmp_w1_v[l], cmp_w2_v[l], rel_bias, conv_w[l], w_out[l],
                   norm1[l], norm2[l], w_ada[l], b_ada[l], w_ff1[l], w_ff2[l])
    return x
```

```python
import functools
import math

import numpy as np
import jax
import jax.numpy as jnp
from jax import lax
from jax.experimental import pallas as pl
from jax.experimental.pallas import tpu as pltpu

HEAD_DIM = 64
N_HEADS = 8
N_KV = 2
GQA = N_HEADS // N_KV
ATTN_W = N_HEADS * HEAD_DIM
KV_W = N_KV * HEAD_DIM
CONV_K = 3
CMP_BLOCK = 32
CMP_STRIDE = 16
CMP_HIDDEN = 256
SEL_BLOCK = 64
N_SELECT = 16
WINDOW = 512
Q_BLOCK = 64
REL_BUCKETS = 32
REL_MAX_DIST = 1024
N_MOD = 6
N_BRANCH = 3
EPS = 1e-6
NEG = -1e30

LANES = 128
SUBLANES = 8
GQ = GQA * Q_BLOCK
KEY_CHUNK = 256
BLOCKS_PER_CHUNK = KEY_CHUNK // SEL_BLOCK
V_TILE = 128
BF16_ROWS = 16
V_ROWS = HEAD_DIM + BF16_ROWS
V_TILE_ROWS = N_KV * V_ROWS
WIN_BLOCKS = WINDOW // SEL_BLOCK
WIN_TILES = WINDOW // V_TILE + 1
NEAR_TILES = (REL_MAX_DIST + Q_BLOCK - 1) // SEL_BLOCK + 1
ROW_TILE = 512
INPROJ_SPLIT = 2
FAR_CHUNKS = 4
NEAR_CHUNKS = 2
QK_AHEAD = 4
LOG2E = math.log2(math.e)
FF_CHUNK = 1024
VMEM_LIMIT = 56 * 1024 * 1024

F32 = jnp.float32
BF16 = jnp.bfloat16
_NT = (((1,), (1,)), ((), ()))


def _cparams(n_axes):
    return pltpu.CompilerParams(dimension_semantics=("arbitrary",) * n_axes,
                                vmem_limit_bytes=VMEM_LIMIT)


def _with_ones_rows(vt):
    ones = jnp.ones((BF16_ROWS, vt.shape[1]), vt.dtype)
    parts = []
    for g in range(N_KV):
        parts += [vt[g * HEAD_DIM:(g + 1) * HEAD_DIM, :], ones]
    return jnp.concatenate(parts, axis=0)


def _swap_halves(p0, p1):
    low = lax.broadcasted_iota(jnp.int32, p0.shape, 1) < LANES // 2
    return (jnp.where(low, p0, pltpu.roll(p1, LANES // 2, 1)),
            jnp.where(low, pltpu.roll(p0, LANES // 2, 1), p1))


def _mod_kernel(c_ref, w_ref, b_ref, o_ref):
    c = c_ref[...]
    a = c * jax.nn.sigmoid(c)
    o_ref[...] = jnp.dot(a, w_ref[...], preferred_element_type=F32,
                         precision=lax.Precision.HIGHEST) + b_ref[...]


def _mod_call(c_pad, w_ada, b_ada):
    rows, d = c_pad.shape
    n = w_ada.shape[1]
    tn = 1024
    return pl.pallas_call(
        _mod_kernel,
        grid=(n // tn,),
        in_specs=[pl.BlockSpec((rows, d), lambda j: (0, 0)),
                  pl.BlockSpec((d, tn), lambda j: (0, j)),
                  pl.BlockSpec((1, tn), lambda j: (0, j))],
        out_specs=pl.BlockSpec((rows, tn), lambda j: (0, j)),
        out_shape=jax.ShapeDtypeStruct((rows, n), F32),
        compiler_params=_cparams(1),
        name="adaln_mod",
    )(c_pad, w_ada, b_ada.reshape(1, n))


_N_KC, _N_VC, _N_KS, _N_KW, _N_CONV = 0, KV_W, 2 * KV_W, 3 * KV_W, 4 * KV_W
_T_Q, _T_VS, _T_VW, _T_G = 0, ATTN_W, ATTN_W + KV_W, ATTN_W + 2 * KV_W
_G_ROWS = GQA * SUBLANES


def _inproj_kernel(x_ref, mod_ref, n1_ref, wn_ref, wt_ref, qn_ref, kn_ref, cw_ref, bdq_ref, bdk_ref,
                   qt_out, kc_out, vc_out, ks_out, vst_out, kw_out, vwt_out, gt_out, conv_out,
                   carry, *, conv_w):
    t = pl.program_id(1)
    tm = x_ref.shape[1]
    sub = tm // INPROJ_SPLIT

    @pl.when(t == 0)
    def _():
        carry[...] = jnp.zeros_like(carry)

    prev2, prev1 = carry[6:7, :], carry[7:8, :]
    c_bg = _N_CONV + conv_w
    c_u = c_bg + conv_w

    for part in range(INPROJ_SPLIT):
        rows = slice(part * sub, (part + 1) * sub)
        x = x_ref[0, rows, :]
        ms = jnp.mean(x * x, axis=-1, keepdims=True)
        y = x * lax.rsqrt(ms + EPS) * n1_ref[...]
        h = (y * (1.0 + mod_ref[0, 1:2, :]) + mod_ref[0, 0:1, :]).astype(BF16)

        def proj(a, b):
            return jnp.dot(h, wn_ref[:, a:b], preferred_element_type=F32)

        def proj_t(a, b):
            return lax.dot_general(wt_ref[a:b, :], h, _NT, preferred_element_type=F32)

        def head_norm(v, gain):
            ssq = jnp.dot((v * v).astype(BF16), bdk_ref[...], preferred_element_type=F32)
            return v * lax.rsqrt(ssq * (1.0 / HEAD_DIM) + EPS) * gain

        qf = proj_t(_T_Q, _T_VS)
        ssq = jnp.dot(bdq_ref[...], (qf * qf).astype(BF16), preferred_element_type=F32)
        qf = qf * lax.rsqrt(ssq * (1.0 / HEAD_DIM) + EPS) * qn_ref[...]
        vg = proj_t(_T_VS, _T_G + _G_ROWS)
        gf = jax.nn.sigmoid(vg[2 * KV_W:, :])
        for c in range(sub // LANES):
            blk = (part * sub) // Q_BLOCK + 2 * c
            cols = slice(c * LANES, (c + 1) * LANES)
            for g in range(N_KV):
                pc = [qf[(g * GQA + r) * HEAD_DIM:(g * GQA + r + 1) * HEAD_DIM, cols]
                      for r in range(GQA)]
                lo01, hi01 = _swap_halves(pc[0], pc[1])
                lo23, hi23 = _swap_halves(pc[2], pc[3])
                qt_out[0, blk, g] = jnp.concatenate([lo01, lo23], axis=1).astype(BF16)
                qt_out[0, blk + 1, g] = jnp.concatenate([hi01, hi23], axis=1).astype(BF16)
            pc = [gf[r * SUBLANES:(r + 1) * SUBLANES, cols] for r in range(GQA)]
            lo01, hi01 = _swap_halves(pc[0], pc[1])
            lo23, hi23 = _swap_halves(pc[2], pc[3])
            gt_out[0, blk] = jnp.concatenate([lo01, lo23], axis=1)
            gt_out[0, blk + 1] = jnp.concatenate([hi01, hi23], axis=1)

        vs_f = _with_ones_rows(vg[:KV_W, :].astype(BF16))
        vw_f = _with_ones_rows(vg[KV_W:2 * KV_W, :].astype(BF16))
        for j in range(sub // KEY_CHUNK):
            vst_out[0, (part * sub) // KEY_CHUNK + j] = vs_f[:, j * KEY_CHUNK:(j + 1) * KEY_CHUNK]
        for j in range(sub // V_TILE):
            vwt_out[0, (part * sub) // V_TILE + j] = vw_f[:, j * V_TILE:(j + 1) * V_TILE]

        kv = proj(_N_KC, _N_CONV)
        kc_out[0, rows, :] = kv[:, _N_KC:_N_VC]
        vc_out[0, rows, :] = kv[:, _N_VC:_N_KS]
        ks_out[0, rows, :] = head_norm(kv[:, _N_KS:_N_KW], kn_ref[...]).astype(BF16)
        kw_out[0, rows, :] = head_norm(kv[:, _N_KW:_N_CONV], kn_ref[...]).astype(BF16)

        z = proj(_N_CONV, c_bg) * proj(c_u, c_u + conv_w)
        row = lax.broadcasted_iota(jnp.int32, z.shape, 0)
        z1 = jnp.where(row == 0, prev1, pltpu.roll(z, 1, 0))
        z2 = jnp.where(row == 0, prev2, jnp.where(row == 1, prev1, pltpu.roll(z, 2, 0)))
        zc = cw_ref[0:1, :] * z2 + cw_ref[1:2, :] * z1 + cw_ref[2:3, :] * z
        conv_out[0, rows, :] = (proj(c_bg, c_u) * zc).astype(BF16)
        prev2, prev1 = z[sub - 2:sub - 1, :], z[sub - 1:sub, :]
        if part == INPROJ_SPLIT - 1:
            carry[...] = z[sub - SUBLANES:sub, :]


def _inproj_call(x, mod, norm1, w_nat, w_tr, qn_col, kn_t, conv_w, bdq, bdk):
    b, t, d = x.shape
    tm = min(ROW_TILE, t)
    cw = conv_w.shape[1]
    nq = t // Q_BLOCK
    row_spec = lambda w: pl.BlockSpec((1, tm, w), lambda i, j: (i, j, 0))
    full = lambda shp: pl.BlockSpec(shp, lambda i, j: (0,) * len(shp))
    vt_spec = lambda w: pl.BlockSpec((1, tm // w, V_TILE_ROWS, w), lambda i, j: (i, j, 0, 0))
    vt_shape = lambda w: jax.ShapeDtypeStruct((b, t // w, V_TILE_ROWS, w), BF16)
    kv = lambda dt: jax.ShapeDtypeStruct((b, t, KV_W), dt)
    out_specs = [pl.BlockSpec((1, tm // Q_BLOCK, N_KV, HEAD_DIM, GQ), lambda i, j: (i, j, 0, 0, 0)),
                 row_spec(KV_W), row_spec(KV_W), row_spec(KV_W), vt_spec(KEY_CHUNK), row_spec(KV_W),
                 vt_spec(V_TILE),
                 pl.BlockSpec((1, tm // Q_BLOCK, SUBLANES, GQ), lambda i, j: (i, j, 0, 0)),
                 row_spec(cw)]
    out_shape = [jax.ShapeDtypeStruct((b, nq, N_KV, HEAD_DIM, GQ), BF16),
                 kv(F32), kv(F32), kv(BF16), vt_shape(KEY_CHUNK), kv(BF16), vt_shape(V_TILE),
                 jax.ShapeDtypeStruct((b, nq, SUBLANES, GQ), F32),
                 jax.ShapeDtypeStruct((b, t, cw), BF16)]
    return pl.pallas_call(
        functools.partial(_inproj_kernel, conv_w=cw),
        grid=(b, t // tm),
        in_specs=[row_spec(d),
                  pl.BlockSpec((1, N_MOD, d), lambda i, j: (i, 0, 0)),
                  full((1, d)), full(w_nat.shape), full(w_tr.shape), full((ATTN_W, 1)),
                  full((1, KV_W)), full((CONV_K, cw)), full((ATTN_W, ATTN_W)), full((KV_W, KV_W))],
        out_specs=out_specs,
        out_shape=out_shape,
        scratch_shapes=[pltpu.VMEM((SUBLANES, cw), F32)],
        compiler_params=_cparams(2),
        name="inproj",
    )(x, mod, norm1, w_nat, w_tr, qn_col, kn_t, conv_w, bdq, bdk)


def _compress_kernel(kx_ref, vx_ref, pek_ref, pev_ref, w1k_ref, w1v_ref, w2k_ref, w2vt_ref,
                     kn_ref, bdk_ref, kc_out, vct_out):
    def hidden(x_ref, pe_ref, w1_ref):
        xv = x_ref[0]
        u = jnp.dot((xv + pe_ref[0:1, :]).astype(BF16), w1_ref[0], preferred_element_type=F32)
        v = jnp.dot((xv + pe_ref[1:2, :]).astype(BF16), w1_ref[1], preferred_element_type=F32)
        n = u.shape[0]
        hid = u + pltpu.roll(v, n - 1, 0)
        return jax.nn.gelu(hid, approximate=True).astype(BF16)

    kc = jnp.dot(hidden(kx_ref, pek_ref, w1k_ref), w2k_ref[...], preferred_element_type=F32)
    ssq = jnp.dot((kc * kc).astype(BF16), bdk_ref[...], preferred_element_type=F32)
    kc_out[0] = (kc * lax.rsqrt(ssq * (1.0 / HEAD_DIM) + EPS) * kn_ref[...]).astype(BF16)
    vct_out[0] = _with_ones_rows(lax.dot_general(w2vt_ref[...], hidden(vx_ref, pev_ref, w1v_ref), _NT,
                                                 preferred_element_type=F32).astype(BF16))


def _compress_call(kx, vx, pek, pev, w1k, w1v, w2k, w2vt, kn_t, bdk):
    b, nrow, wide = kx.shape
    hid2 = w1k.shape[2]
    full = lambda shp: pl.BlockSpec(shp, lambda i: (0,) * len(shp))
    xs = pl.BlockSpec((1, nrow, wide), lambda i: (i, 0, 0))
    return pl.pallas_call(
        _compress_kernel,
        grid=(b,),
        in_specs=[xs, xs, full((2, wide)), full((2, wide)), full((2, wide, hid2)),
                  full((2, wide, hid2)), full((hid2, KV_W)), full((KV_W, hid2)),
                  full((1, KV_W)), full((KV_W, KV_W))],
        out_specs=[pl.BlockSpec((1, nrow, KV_W), lambda i: (i, 0, 0)),
                   pl.BlockSpec((1, V_TILE_ROWS, nrow), lambda i: (i, 0, 0))],
        out_shape=[jax.ShapeDtypeStruct((b, nrow, KV_W), BF16),
                   jax.ShapeDtypeStruct((b, V_TILE_ROWS, nrow), BF16)],
        compiler_params=_cparams(1),
        name="compress",
    )(kx, vx, pek, pev, w1k, w1v, w2k, w2vt, kn_t, bdk)


def _attn_kernel(q_ref, kc_ref, vct_ref, ks_ref, vst_ref, kw_ref, vwt_ref, gt_ref, ovt_ref,
                 eb_ref, sb_ref, wb_ref, far_ref, o_ref, sel_scr, acc_scr, out_scr, qp_scr, s_scr,
                 fin_scr, *, n_cmp_rows, eb_shift):
    n_sb = sel_scr.shape[1]
    ci = jnp.minimum(pl.program_id(1), n_sb - 1)
    par = lax.rem(ci, 2)
    n_total = ks_ref.shape[1] // KEY_CHUNK
    vsl = lambda g: slice(g * V_ROWS, (g + 1) * V_ROWS)

    @pl.when((pl.program_id(0) == 0) & (pl.program_id(1) == 0))
    def _():
        def zero(g, carry):
            fin_scr[g] = jnp.zeros(fin_scr.shape[1:], F32)
            return carry
        lax.fori_loop(0, N_KV, zero, 0)

    pieces = []
    for g in range(N_KV):
        for half in range(GQ // LANES):
            a = fin_scr[g, :, half * LANES:(half + 1) * LANES]
            stacked = jnp.concatenate([a, pltpu.roll(a, Q_BLOCK, 1)], axis=0)
            pieces.append(stacked.T[:Q_BLOCK, :])
    o_ref[0] = jnp.concatenate(pieces, axis=1).astype(BF16)

    zeros_q = jnp.zeros((HEAD_DIM, GQ), BF16)
    qp_scr[0] = jnp.concatenate([q_ref[0, 0, 0], zeros_q], axis=0)
    qp_scr[1] = jnp.concatenate([zeros_q, q_ref[0, 0, 1]], axis=0)

    def chunk_qk(g, c):
        kk = ks_ref[0, pl.ds(pl.multiple_of(c * KEY_CHUNK, KEY_CHUNK), KEY_CHUNK), :]
        return jnp.dot(kk, qp_scr[g], preferred_element_type=F32).astype(BF16)

    n_chunks = ci // BLOCKS_PER_CHUNK + 1
    far_groups = jnp.maximum(ci - (NEAR_TILES - 1), 0) // (BLOCKS_PER_CHUNK * FAR_CHUNKS)
    n_wide = far_groups // 2
    n_far = far_groups - 2 * n_wide
    c_far = n_wide * (2 * FAR_CHUNKS)
    c0 = far_groups * FAR_CHUNKS
    n_near = (n_chunks - c0 + NEAR_CHUNKS - 1) // NEAR_CHUNKS

    def near_units(i):
        return [(g, jnp.minimum(c0 + NEAR_CHUNKS * i + j, n_total - 1))
                for g in range(N_KV) for j in range(NEAR_CHUNKS)]

    def far_units(i):
        return [(g, c_far + i * FAR_CHUNKS + j) for g in range(N_KV) for j in range(FAR_CHUNKS)]

    def wide_units(i):
        return [(g, i * 2 * FAR_CHUNKS + j) for g in range(N_KV) for j in range(2 * FAR_CHUNKS)]

    def pick_units(options):
        units = options[-1][1][:QK_AHEAD]
        for cond, cand in reversed(options[:-1]):
            units = [(jnp.where(cond, g_a, g_b), jnp.where(cond, c_a, c_b))
                     for (g_a, c_a), (g_b, c_b) in zip(cand[:QK_AHEAD], units)]
        return [(g, jnp.minimum(c, n_total - 1)) for g, c in units]

    def prefetch_scores(units):
        for k in range(QK_AHEAD):
            s_scr[k] = chunk_qk(*units[k])

    lane = lax.broadcasted_iota(jnp.int32, (n_sb, LANES), 1)
    cmp_c = n_cmp_rows - 4
    e0 = cmp_c - 4 * ci + jnp.where(par == 0, eb_shift[0], eb_shift[1])
    e0 = pl.multiple_of(e0, 8)
    o_cmp = []
    imp = []
    w0 = ci // 2 - (WIN_TILES - 1)
    win_tiles = [jnp.maximum(w0 + j, 0) for j in range(WIN_TILES)]
    s_cmp = [jnp.dot(kc_ref[0], qp_scr[g], preferred_element_type=F32) for g in range(N_KV)]
    for g in range(N_KV):
        sc = s_cmp[g] + eb_ref[par, g, pl.ds(e0, n_cmp_rows), :]
        m = jnp.max(sc, axis=0, keepdims=True)
        e = jnp.exp2(sc - m).astype(BF16)
        ov = jnp.dot(vct_ref[0, vsl(g), :], e, preferred_element_type=F32)
        inv = jnp.where(m > 0.5 * NEG, 1.0 / ov[HEAD_DIM:HEAD_DIM + 1, :], 0.0)
        o_cmp.append(ov[:HEAD_DIM, :] * inv)
        ir = jnp.dot(ovt_ref[...], e, preferred_element_type=F32) * inv
        a = ir[:, :LANES] + ir[:, LANES:]
        imp.append(a + pltpu.roll(a, Q_BLOCK, 1))

    s_win = [[jnp.dot(kw_ref[0, pl.ds(pl.multiple_of(tj * V_TILE, V_TILE), V_TILE), :], qp_scr[g],
                      preferred_element_type=F32).astype(BF16) for tj in win_tiles]
             for g in range(N_KV)]

    prefetch_scores(pick_units([(n_wide > 0, wide_units(0)), (n_far > 0, far_units(0)),
                                (None, near_units(0))]))

    jidx = lax.broadcasted_iota(jnp.int32, (n_sb, LANES), 0)
    jf = jidx.astype(F32)
    valid = jidx <= ci
    forced = (jidx == 0) | (jidx == ci) | (jidx == ci - 1)
    score = jnp.where(valid, jnp.where(forced, -2.0, jnp.where(lane < Q_BLOCK, imp[0], imp[1])),
                      -1.0)
    sel = jnp.where(forced, 1.0, 0.0)

    def first_max(score):
        pairs = [(score[r:r + SUBLANES], jf[r:r + SUBLANES]) for r in range(0, n_sb, SUBLANES)]
        while len(pairs) > 1:
            nxt = []
            for (va, ia), (vb, ib) in zip(pairs[0::2], pairs[1::2]):
                keep = va >= vb
                nxt.append((jnp.where(keep, va, vb), jnp.where(keep, ia, ib)))
            pairs = nxt + pairs[len(pairs) - len(pairs) % 2:]
        v8, i8 = pairs[0]
        mx = jnp.max(v8, axis=0, keepdims=True)
        return jnp.min(jnp.where(v8 == mx, i8, float(n_sb)), axis=0, keepdims=True)

    for _ in range(min(N_SELECT, n_sb) - 3):
        first = first_max(score)
        hit = jf == first
        sel = jnp.where(hit, 1.0, sel)
        score = jnp.where(hit, -2.0, score)
    selneg = jnp.where((sel > 0.5) & valid, 0.0, NEG)
    swapped = pltpu.roll(selneg, Q_BLOCK, 1)
    left = jnp.where(lane < Q_BLOCK, selneg, swapped)
    right = jnp.where(lane < Q_BLOCK, swapped, selneg)
    sel_scr[0] = jnp.concatenate([left, left], axis=1)
    sel_scr[1] = jnp.concatenate([right, right], axis=1)

    for g in range(N_KV):
        parts = []
        for i in range(2 * WIN_TILES):
            delta = par + WIN_BLOCKS - i
            ok = (delta >= 0) & (delta <= WIN_BLOCKS) & (delta <= ci)
            tile = jnp.where(ok, delta, WIN_BLOCKS + 1)
            half = s_win[g][i // 2][(i % 2) * SEL_BLOCK:(i % 2 + 1) * SEL_BLOCK, :]
            parts.append(half + wb_ref[g, tile])
        s = jnp.concatenate(parts, axis=0)
        m = jnp.max(s, axis=0, keepdims=True)
        pb = jnp.exp2(s - m)
        o_win = jnp.zeros((V_ROWS, GQ), F32)
        for j, tj in enumerate(win_tiles):
            o_win = o_win + jnp.dot(vwt_ref[0, tj, vsl(g), :], pb[j * V_TILE:(j + 1) * V_TILE, :],
                                    preferred_element_type=F32)
        w_scale = gt_ref[0, 0, 2 * N_KV + g:2 * N_KV + g + 1, :] * (1.0 / o_win[HEAD_DIM:HEAD_DIM + 1, :])
        out_scr[g] = gt_ref[0, 0, g:g + 1, :] * o_cmp[g] + w_scale * o_win[:HEAD_DIM, :]
        acc_scr[g] = jnp.zeros((V_ROWS, GQ), F32)

    def chunk_softmax(s, g, c, near, extra):
        parts = []
        for i in range(BLOCKS_PER_CHUNK):
            kb = c * BLOCKS_PER_CHUNK + i
            blk = s[i * SEL_BLOCK:(i + 1) * SEL_BLOCK, :]
            mrow = sel_scr[g, pl.ds(kb, 1), :]
            if extra is not None:
                mrow = mrow + extra
            if near:
                tile = jnp.clip(ci - kb, 0, NEAR_TILES)
                blk = blk + sb_ref[g, tile] + mrow.astype(BF16)
            else:
                blk = blk + (mrow + far_ref[g]).astype(BF16)
            parts.append(blk)
        s = jnp.concatenate(parts, axis=0)
        m_c = jnp.max(s, axis=0, keepdims=True)
        return m_c.astype(F32), jnp.exp2(s - m_c)

    def chunk_pv(pb, g, c):
        return jnp.dot(vst_ref[0, c, vsl(g), :], pb, preferred_element_type=F32)

    def merge_step(units, extras, near, next_units, carry):
        scores = {k: s_scr[k] for k in range(QK_AHEAD)}
        results = []
        for k, (g, c) in enumerate(units):
            m_c, pb = chunk_softmax(scores.pop(k), g, c, near, extras[k])
            if k + QK_AHEAD < len(units):
                scores[k + QK_AHEAD] = chunk_qk(*units[k + QK_AHEAD])
            else:
                j = k + QK_AHEAD - len(units)
                s_scr[j] = chunk_qk(*next_units[j])
            results.append((m_c, chunk_pv(pb, g, c)))
        per_g = len(units) // N_KV
        new = []
        for g in range(N_KV):
            m_run = carry[g]
            stats = results[g * per_g:(g + 1) * per_g]
            m_new = m_run
            for m_c, _ in stats:
                m_new = jnp.maximum(m_new, m_c)
            acc = jnp.exp2(m_run - m_new) * acc_scr[g]
            for m_c, pv in stats:
                acc = acc + jnp.exp2(m_c - m_new) * pv
            acc_scr[g] = acc
            new.append(m_new)
        return tuple(new)

    def wide_step(i, carry):
        units = wide_units(i)
        nxt = pick_units([(i + 1 < n_wide, wide_units(i + 1)), (n_far > 0, far_units(0)),
                          (None, near_units(0))])
        return merge_step(units, [None] * len(units), False, nxt, carry)

    def far_step(i, carry):
        units = far_units(i)
        return merge_step(units, [None] * len(units), False, pick_units([(None, near_units(0))]), carry)

    def near_step(i, carry):
        units = near_units(i)
        extras = [None if j == 0 else jnp.where(c0 + NEAR_CHUNKS * i + j < n_chunks, 0.0, NEG)
                  for _ in range(N_KV) for j in range(NEAR_CHUNKS)]
        return merge_step(units, extras, True, near_units(i + 1), carry)

    init = (jnp.full((1, GQ), NEG, F32),) * N_KV
    carry = lax.fori_loop(0, n_wide, wide_step, init)
    carry = lax.fori_loop(0, n_far, far_step, carry)
    lax.fori_loop(0, n_near, near_step, carry)

    for g in range(N_KV):
        acc = acc_scr[g]
        scale = gt_ref[0, 0, N_KV + g:N_KV + g + 1, :] * (1.0 / acc[HEAD_DIM:HEAD_DIM + 1, :])
        fin_scr[g] = out_scr[g] + scale * acc[:HEAD_DIM, :]


def _attn_call(qt, kc, vct, ks, vst, kw, vwt, gt, ovt, ebank, sbank, wbank, far, eb_shift):
    b, nq = qt.shape[0], qt.shape[1]
    t = ks.shape[1]
    n_cmp_rows = kc.shape[1]
    n_sb = t // SEL_BLOCK
    per_b = lambda shp: pl.BlockSpec((1,) + shp[1:], lambda i, j: (i,) + (0,) * (len(shp) - 1))
    per_q = lambda shp: pl.BlockSpec(
        (1, 1) + shp[2:], lambda i, j: (i, jnp.minimum(j, nq - 1)) + (0,) * (len(shp) - 2))
    full = lambda shp: pl.BlockSpec(shp, lambda i, j: (0,) * len(shp))
    args = (qt, kc, vct, ks, vst, kw, vwt, gt, ovt, ebank, sbank, wbank, far)
    specs = [per_q(qt.shape), per_b(kc.shape), per_b(vct.shape), per_b(ks.shape), per_b(vst.shape),
             per_b(kw.shape), per_b(vwt.shape), per_q(gt.shape), full(ovt.shape),
             full(ebank.shape), full(sbank.shape), full(wbank.shape), full(far.shape)]
    acc_like = pltpu.VMEM((N_KV, HEAD_DIM, GQ), F32)
    return pl.pallas_call(
        functools.partial(_attn_kernel, n_cmp_rows=n_cmp_rows, eb_shift=eb_shift),
        grid=(b, nq + 1),
        in_specs=specs,
        out_specs=pl.BlockSpec((1, Q_BLOCK, ATTN_W), lambda i, j: (i, jnp.maximum(j - 1, 0), 0)),
        out_shape=jax.ShapeDtypeStruct((b, t, ATTN_W), BF16),
        scratch_shapes=[pltpu.VMEM((N_KV, n_sb, GQ), F32),
                        pltpu.VMEM((N_KV, V_ROWS, GQ), F32),
                        acc_like,
                        pltpu.VMEM((N_KV, KV_W, GQ), BF16),
                        pltpu.VMEM((QK_AHEAD, KEY_CHUNK, GQ), BF16),
                        acc_like],
        compiler_params=_cparams(2),
        name="nsa_attention",
    )(*args)


def _ffn_kernel(x_ref, a_ref, c_ref, mod_ref, n2_ref, wo_ref, w1_ref, w2_ref, o_ref):
    aw = a_ref.shape[2]
    mix = jnp.dot(a_ref[0], wo_ref[0:aw, :], preferred_element_type=F32)
    mix = mix + jnp.dot(c_ref[0], wo_ref[aw:, :], preferred_element_type=F32)
    x1 = x_ref[0] + mod_ref[0, 2:3, :] * mix
    ms = jnp.mean(x1 * x1, axis=-1, keepdims=True)
    y = x1 * lax.rsqrt(ms + EPS) * n2_ref[...]
    h2 = (y * (1.0 + mod_ref[0, 4:5, :]) + mod_ref[0, 3:4, :]).astype(BF16)
    d_ff = w1_ref.shape[1]
    ff = jnp.zeros(x1.shape, F32)
    for j in range(d_ff // FF_CHUNK):
        a = jnp.dot(h2, w1_ref[:, j * FF_CHUNK:(j + 1) * FF_CHUNK], preferred_element_type=F32)
        a = jnp.maximum(a, 0.0)
        ff = ff + jnp.dot((a * a).astype(BF16), w2_ref[j * FF_CHUNK:(j + 1) * FF_CHUNK, :],
                          preferred_element_type=F32)
    o_ref[0] = x1 + mod_ref[0, 5:6, :] * ff


def _ffn_call(x, attn, conv, mod, norm2, w_out, w_ff1, w_ff2):
    b, t, d = x.shape
    tm = min(ROW_TILE, t)
    row_spec = lambda w: pl.BlockSpec((1, tm, w), lambda i, j: (i, j, 0))
    full = lambda shp: pl.BlockSpec(shp, lambda i, j: (0,) * len(shp),
                                    pipeline_mode=pl.Buffered(1))
    return pl.pallas_call(
        _ffn_kernel,
        grid=(b, t // tm),
        in_specs=[row_spec(d), row_spec(attn.shape[2]), row_spec(conv.shape[2]),
                  pl.BlockSpec((1, N_MOD, d), lambda i, j: (i, 0, 0)),
                  full((1, d)), full(w_out.shape), full(w_ff1.shape), full(w_ff2.shape)],
        out_specs=row_spec(d),
        out_shape=jax.ShapeDtypeStruct((b, t, d), F32),
        compiler_params=_cparams(2),
        name="outproj_mlp",
    )(x, attn, conv, mod, norm2, w_out, w_ff1, w_ff2)


def _block_diag_ones(n):
    idx = np.arange(n) // HEAD_DIM
    return jnp.asarray(idx[:, None] == idx[None, :], dtype=BF16)


def _pack_w_in(w_in):
    d = w_in.shape[0]
    conv_w = d - ATTN_W
    sizes = [ATTN_W] + [KV_W] * 6 + [N_BRANCH * N_HEADS] + [conv_w] * 3
    offs = np.concatenate([[0], np.cumsum(sizes)])
    part = lambda i: w_in[:, offs[i]:offs[i + 1]]
    q, kc, vc, ks, vs, kw, vw, g, cgate, bgate, u = (part(i) for i in range(11))
    w_nat = jnp.concatenate([kc, vc, ks, kw, cgate, bgate, u], axis=1).astype(BF16)
    gt = g.reshape(d, N_KV, GQA, N_BRANCH).transpose(2, 3, 1, 0).reshape(GQA, N_BRANCH * N_KV, d)
    gt = jnp.pad(gt, ((0, 0), (0, SUBLANES - N_BRANCH * N_KV), (0, 0))).reshape(_G_ROWS, d)
    w_tr = jnp.concatenate([q.T, vs.T, vw.T, gt], axis=0).astype(BF16)
    return w_nat, w_tr


def _expand_w1(w1):
    hid = w1.shape[1]
    w = w1.reshape(2, CMP_STRIDE, 1, HEAD_DIM, 1, hid)
    eye = jnp.eye(N_KV, dtype=w1.dtype).reshape(1, 1, N_KV, 1, N_KV, 1)
    return (w * eye).reshape(2, CMP_STRIDE * KV_W, N_KV * hid).astype(BF16)


def _expand_w2(w2):
    hid = w2.shape[0]
    eye = jnp.eye(N_KV, dtype=w2.dtype).reshape(N_KV, 1, N_KV, 1)
    return (w2.reshape(1, hid, 1, HEAD_DIM) * eye).reshape(N_KV * hid, KV_W).astype(BF16)


def _expand_pe(pe):
    p = pe.reshape(2, CMP_STRIDE, 1, HEAD_DIM)
    return jnp.broadcast_to(p, (2, CMP_STRIDE, N_KV, HEAD_DIM)).reshape(2, CMP_STRIDE * KV_W)


def _bucket_thresholds():
    n = np.arange(2 * REL_MAX_DIST)
    max_exact = REL_BUCKETS // 2
    nf = np.maximum(n, max_exact).astype(np.float32)
    ratio = np.log(nf / np.float32(max_exact)) / np.float32(math.log(REL_MAX_DIST / max_exact))
    large = max_exact + (ratio * np.float32(REL_BUCKETS - max_exact)).astype(np.int32)
    table = np.where(n < max_exact, n, np.minimum(large, REL_BUCKETS - 1))
    return tuple(int(np.searchsorted(table, k, side="left")) for k in range(REL_BUCKETS))


def _bank_kernel(rows_ref, o_ref, *, thresholds, dist_fn):
    shape = o_ref.shape[-2:]
    row = lax.broadcasted_iota(jnp.int32, shape, 0)
    qi = lax.broadcasted_iota(jnp.int32, shape, 1) & (Q_BLOCK - 1)
    dist, ok = dist_fn(row, qi)
    v = jnp.broadcast_to(rows_ref[0, 0:1, :], shape)
    for k in range(1, REL_BUCKETS):
        v = jnp.where(dist >= thresholds[k], rows_ref[0, k:k + 1, :], v)
    o_ref[...] = jnp.where(ok, v, NEG).reshape(o_ref.shape)


def _bank_call(bias_rows, lead, n_tiles, dist_fn, name):
    grid = tuple(lead) + (N_KV, n_tiles)
    nl = len(lead)
    blk = (1,) * (nl + 2) + (SEL_BLOCK, GQ)
    thr = _bucket_thresholds()

    def body(rows_ref, o_ref):
        ids = [pl.program_id(a) for a in range(nl + 2)]
        fn = lambda row, qi: dist_fn(ids[:nl], ids[nl + 1], row, qi)
        _bank_kernel(rows_ref, o_ref, thresholds=thr, dist_fn=fn)

    return pl.pallas_call(
        body,
        grid=grid,
        in_specs=[pl.BlockSpec((1, REL_BUCKETS, GQ), lambda *i: (i[nl], 0, 0))],
        out_specs=pl.BlockSpec(blk, lambda *i: tuple(i) + (0, 0)),
        out_shape=jax.ShapeDtypeStruct(tuple(lead) + (N_KV, n_tiles, SEL_BLOCK, GQ), F32),
        compiler_params=_cparams(nl + 2),
        name=name,
    )(bias_rows)


def _bias_banks(rel_bias, t):
    n_cmp_rows = t // CMP_STRIDE
    rows = rel_bias.reshape(REL_BUCKETS, N_KV, GQA).transpose(1, 0, 2)
    rows = jnp.repeat(rows, Q_BLOCK, axis=2) * LOG2E

    def sel_dist(lead, tile, row, qi):
        dist = SEL_BLOCK * tile + qi - row
        return dist, dist >= 0

    sbank = _bank_call(rows, (), NEAR_TILES + 1, sel_dist, "bias_bank_sel")
    far = sbank[:, NEAR_TILES, 0:1, :]

    def win_dist(lead, tile, row, qi):
        dist = SEL_BLOCK * tile + qi - row
        return dist, (dist >= 0) & (dist < WINDOW)

    wbank = _bank_call(rows, (), WIN_BLOCKS + 2, win_dist, "bias_bank_win")

    cmp_c = n_cmp_rows - 4
    shifts = tuple(int((-(cmp_c - 4 * p)) % 8) for p in range(2))
    n_tiles = (cmp_c + n_cmp_rows + 8 + SEL_BLOCK - 1) // SEL_BLOCK

    def cmp_dist(lead, tile, row, qi):
        e = SEL_BLOCK * tile + row - jnp.where(lead[0] == 0, shifts[0], shifts[1])
        dist = qi - CMP_STRIDE * e + (CMP_STRIDE * cmp_c - (CMP_BLOCK - 1))
        return dist, (dist >= 0) & (e >= 0)

    ebank = _bank_call(rows, (2,), n_tiles, cmp_dist, "bias_bank_cmp")
    ebank = ebank.reshape(2, N_KV, n_tiles * SEL_BLOCK, GQ)
    return ebank, sbank, wbank, far, shifts


def _overlap_t(t):
    n_cmp_rows = t // CMP_STRIDE
    n_sb = t // SEL_BLOCK
    c_start = np.arange(n_cmp_rows)[None, :] * CMP_STRIDE
    s_start = np.arange(n_sb)[:, None] * SEL_BLOCK
    ov = np.clip(np.minimum(c_start + CMP_BLOCK, s_start + SEL_BLOCK)
                 - np.maximum(c_start, s_start), 0, None) / CMP_BLOCK
    ov[:, n_cmp_rows - 1] = 0.0
    return jnp.asarray(ov, dtype=BF16)


def _layer(x, c_pad, w_in, q_norm, k_norm, cmp_pe_k, cmp_w1_k, cmp_w2_k, cmp_pe_v, cmp_w1_v,
           cmp_w2_v, rel_bias, conv_w, w_out, norm1, norm2, w_ada, b_ada, w_ff1, w_ff2):
    b, t, d = x.shape
    scale = HEAD_DIM ** -0.5

    mod = _mod_call(c_pad, w_ada, b_ada)[:b].reshape(b, N_MOD, d)

    qn_col = (jnp.tile(q_norm, N_HEADS) * (scale * LOG2E)).reshape(ATTN_W, 1)
    kn_t = jnp.tile(k_norm, N_KV).reshape(1, KV_W)
    bdq = _block_diag_ones(ATTN_W)
    bdk = _block_diag_ones(KV_W)
    w_nat, w_tr = _pack_w_in(w_in)
    qt, kc_raw, vc_raw, ks, vst, kw, vwt, gt, conv = _inproj_call(
        x, mod, norm1.reshape(1, d), w_nat, w_tr, qn_col, kn_t, conv_w, bdq, bdk)

    n_cmp_rows = t // CMP_STRIDE
    kc, vct = _compress_call(
        kc_raw.reshape(b, n_cmp_rows, CMP_STRIDE * KV_W), vc_raw.reshape(b, n_cmp_rows, CMP_STRIDE * KV_W),
        _expand_pe(cmp_pe_k), _expand_pe(cmp_pe_v), _expand_w1(cmp_w1_k), _expand_w1(cmp_w1_v),
        _expand_w2(cmp_w2_k), _expand_w2(cmp_w2_v).T, kn_t, bdk)

    ebank, sbank, wbank, far, eb_shift = _bias_banks(rel_bias, t)
    attn = _attn_call(qt, kc, vct, ks, vst, kw, vwt, gt, _overlap_t(t), ebank, sbank.astype(BF16),
                      wbank.astype(BF16), far, eb_shift)

    return _ffn_call(x, attn, conv, mod, norm2.reshape(1, d), w_out.astype(BF16),
                     w_ff1.astype(BF16), w_ff2.astype(BF16))


def kernel(x, c, w_in, q_norm, k_norm, cmp_pe_k, cmp_w1_k, cmp_w2_k, cmp_pe_v, cmp_w1_v, cmp_w2_v,
           rel_bias, conv_w, w_out, norm1, norm2, w_ada, b_ada, w_ff1, w_ff2):
    b = x.shape[0]
    c_pad = jnp.pad(c, ((0, (-b) % 8), (0, 0)))
    for l in range(w_in.shape[0]):
        x = _layer(x, c_pad, w_in[l], q_norm[l], k_norm[l], cmp_pe_k[l], cmp_w1_k[l], cmp_w2_k[l],
                   cmp_pe_v[l], cmp_w1_v[l], cmp_w2_v[l], rel_bias, conv_w[l], w_out[l],
                   norm1[l], norm2[l], w_ada[l], b_ada[l], w_ff1[l], w_ff2[l])
    return x
```

```python
import functools
import math

import numpy as np
import jax
import jax.numpy as jnp
from jax import lax
from jax.experimental import pallas as pl
from jax.experimental.pallas import tpu as pltpu

HEAD_DIM = 64
N_HEADS = 8
N_KV = 2
GQA = N_HEADS // N_KV
ATTN_W = N_HEADS * HEAD_DIM
KV_W = N_KV * HEAD_DIM
CONV_K = 3
CMP_BLOCK = 32
CMP_STRIDE = 16
CMP_HIDDEN = 256
SEL_BLOCK = 64
N_SELECT = 16
WINDOW = 512
Q_BLOCK = 64
REL_BUCKETS = 32
REL_MAX_DIST = 1024
N_MOD = 6
N_BRANCH = 3
EPS = 1e-6
NEG = -1e30

LANES = 128
SUBLANES = 8
GQ = GQA * Q_BLOCK
KEY_CHUNK = 256
BLOCKS_PER_CHUNK = KEY_CHUNK // SEL_BLOCK
V_TILE = 128
BF16_ROWS = 16
V_ROWS = HEAD_DIM + BF16_ROWS
V_TILE_ROWS = N_KV * V_ROWS
WIN_BLOCKS = WINDOW // SEL_BLOCK
WIN_TILES = WINDOW // V_TILE + 1
NEAR_TILES = (REL_MAX_DIST + Q_BLOCK - 1) // SEL_BLOCK + 1
ROW_TILE = 512
INPROJ_SPLIT = 2
FAR_CHUNKS = 4
NEAR_CHUNKS = 2
QK_AHEAD = 4
LOG2E = math.log2(math.e)
FF_CHUNK = 1024
VMEM_LIMIT = 56 * 1024 * 1024

F32 = jnp.float32
BF16 = jnp.bfloat16
_NT = (((1,), (1,)), ((), ()))


def _cparams(n_axes):
    return pltpu.CompilerParams(dimension_semantics=("arbitrary",) * n_axes,
                                vmem_limit_bytes=VMEM_LIMIT)


def _with_ones_rows(vt):
    ones = jnp.ones((BF16_ROWS, vt.shape[1]), vt.dtype)
    parts = []
    for g in range(N_KV):
        parts += [vt[g * HEAD_DIM:(g + 1) * HEAD_DIM, :], ones]
    return jnp.concatenate(parts, axis=0)


def _swap_halves(p0, p1):
    low = lax.broadcasted_iota(jnp.int32, p0.shape, 1) < LANES // 2
    return (jnp.where(low, p0, pltpu.roll(p1, LANES // 2, 1)),
            jnp.where(low, pltpu.roll(p0, LANES // 2, 1), p1))


def _mod_kernel(c_ref, w_ref, b_ref, o_ref):
    c = c_ref[...]
    a = c * jax.nn.sigmoid(c)
    o_ref[...] = jnp.dot(a, w_ref[...], preferred_element_type=F32,
                         precision=lax.Precision.HIGHEST) + b_ref[...]


def _mod_call(c_pad, w_ada, b_ada):
    rows, d = c_pad.shape
    n = w_ada.shape[1]
    tn = 1024
    return pl.pallas_call(
        _mod_kernel,
        grid=(n // tn,),
        in_specs=[pl.BlockSpec((rows, d), lambda j: (0, 0)),
                  pl.BlockSpec((d, tn), lambda j: (0, j)),
                  pl.BlockSpec((1, tn), lambda j: (0, j))],
        out_specs=pl.BlockSpec((rows, tn), lambda j: (0, j)),
        out_shape=jax.ShapeDtypeStruct((rows, n), F32),
        compiler_params=_cparams(1),
        name="adaln_mod",
    )(c_pad, w_ada, b_ada.reshape(1, n))


_N_KC, _N_VC, _N_KS, _N_KW, _N_CONV = 0, KV_W, 2 * KV_W, 3 * KV_W, 4 * KV_W
_T_Q, _T_VS, _T_VW, _T_G = 0, ATTN_W, ATTN_W + KV_W, ATTN_W + 2 * KV_W
_G_ROWS = GQA * SUBLANES


def _inproj_kernel(x_ref, mod_ref, n1_ref, wn_ref, wt_ref, qn_ref, kn_ref, cw_ref, bdq_ref, bdk_ref,
                   qt_out, kc_out, vc_out, ks_out, vst_out, kw_out, vwt_out, gt_out, conv_out,
                   carry, *, conv_w):
    t = pl.program_id(1)
    tm = x_ref.shape[1]
    sub = tm // INPROJ_SPLIT

    @pl.when(t == 0)
    def _():
        carry[...] = jnp.zeros_like(carry)

    prev2, prev1 = carry[6:7, :], carry[7:8, :]
    c_bg = _N_CONV + conv_w
    c_u = c_bg + conv_w

    for part in range(INPROJ_SPLIT):
        rows = slice(part * sub, (part + 1) * sub)
        x = x_ref[0, rows, :]
        ms = jnp.mean(x * x, axis=-1, keepdims=True)
        y = x * lax.rsqrt(ms + EPS) * n1_ref[...]
        h = (y * (1.0 + mod_ref[0, 1:2, :]) + mod_ref[0, 0:1, :]).astype(BF16)

        def proj(a, b):
            return jnp.dot(h, wn_ref[:, a:b], preferred_element_type=F32)

        def proj_t(a, b):
            return lax.dot_general(wt_ref[a:b, :], h, _NT, preferred_element_type=F32)

        def head_norm(v, gain):
            ssq = jnp.dot((v * v).astype(BF16), bdk_ref[...], preferred_element_type=F32)
            return v * lax.rsqrt(ssq * (1.0 / HEAD_DIM) + EPS) * gain

        qf = proj_t(_T_Q, _T_VS)
        vg = proj_t(_T_VS, _T_G + _G_ROWS)
        kv = proj(_N_KC, _N_CONV)
        cv = proj(_N_CONV, c_u + conv_w)

        ssq = jnp.dot(bdq_ref[...], (qf * qf).astype(BF16), preferred_element_type=F32)
        qf = qf * lax.rsqrt(ssq * (1.0 / HEAD_DIM) + EPS) * qn_ref[...]
        gf = jax.nn.sigmoid(vg[2 * KV_W:, :])
        for c in range(sub // LANES):
            blk = (part * sub) // Q_BLOCK + 2 * c
            cols = slice(c * LANES, (c + 1) * LANES)
            for g in range(N_KV):
                pc = [qf[(g * GQA + r) * HEAD_DIM:(g * GQA + r + 1) * HEAD_DIM, cols]
                      for r in range(GQA)]
                lo01, hi01 = _swap_halves(pc[0], pc[1])
                lo23, hi23 = _swap_halves(pc[2], pc[3])
                qt_out[0, blk, g] = jnp.concatenate([lo01, lo23], axis=1).astype(BF16)
                qt_out[0, blk + 1, g] = jnp.concatenate([hi01, hi23], axis=1).astype(BF16)
            pc = [gf[r * SUBLANES:(r + 1) * SUBLANES, cols] for r in range(GQA)]
            lo01, hi01 = _swap_halves(pc[0], pc[1])
            lo23, hi23 = _swap_halves(pc[2], pc[3])
            gt_out[0, blk] = jnp.concatenate([lo01, lo23], axis=1)
            gt_out[0, blk + 1] = jnp.concatenate([hi01, hi23], axis=1)

        vs_f = _with_ones_rows(vg[:KV_W, :].astype(BF16))
        vw_f = _with_ones_rows(vg[KV_W:2 * KV_W, :].astype(BF16))
        for j in range(sub // KEY_CHUNK):
            vst_out[0, (part * sub) // KEY_CHUNK + j] = vs_f[:, j * KEY_CHUNK:(j + 1) * KEY_CHUNK]
        for j in range(sub // V_TILE):
            vwt_out[0, (part * sub) // V_TILE + j] = vw_f[:, j * V_TILE:(j + 1) * V_TILE]

        kc_out[0, rows, :] = kv[:, _N_KC:_N_VC]
        vc_out[0, rows, :] = kv[:, _N_VC:_N_KS]
        ks_out[0, rows, :] = head_norm(kv[:, _N_KS:_N_KW], kn_ref[...]).astype(BF16)
        kw_out[0, rows, :] = head_norm(kv[:, _N_KW:_N_CONV], kn_ref[...]).astype(BF16)

        z = cv[:, :conv_w] * cv[:, 2 * conv_w:]
        row = lax.broadcasted_iota(jnp.int32, z.shape, 0)
        z1 = jnp.where(row == 0, prev1, pltpu.roll(z, 1, 0))
        z2 = jnp.where(row == 0, prev2, jnp.where(row == 1, prev1, pltpu.roll(z, 2, 0)))
        zc = cw_ref[0:1, :] * z2 + cw_ref[1:2, :] * z1 + cw_ref[2:3, :] * z
        conv_out[0, rows, :] = (cv[:, conv_w:2 * conv_w] * zc).astype(BF16)
        prev2, prev1 = z[sub - 2:sub - 1, :], z[sub - 1:sub, :]
        if part == INPROJ_SPLIT - 1:
            carry[...] = z[sub - SUBLANES:sub, :]


def _inproj_call(x, mod, norm1, w_nat, w_tr, qn_col, kn_t, conv_w, bdq, bdk):
    b, t, d = x.shape
    tm = min(ROW_TILE, t)
    cw = conv_w.shape[1]
    nq = t // Q_BLOCK
    row_spec = lambda w: pl.BlockSpec((1, tm, w), lambda i, j: (i, j, 0))
    full = lambda shp: pl.BlockSpec(shp, lambda i, j: (0,) * len(shp))
    vt_spec = lambda w: pl.BlockSpec((1, tm // w, V_TILE_ROWS, w), lambda i, j: (i, j, 0, 0))
    vt_shape = lambda w: jax.ShapeDtypeStruct((b, t // w, V_TILE_ROWS, w), BF16)
    kv = lambda dt: jax.ShapeDtypeStruct((b, t, KV_W), dt)
    out_specs = [pl.BlockSpec((1, tm // Q_BLOCK, N_KV, HEAD_DIM, GQ), lambda i, j: (i, j, 0, 0, 0)),
                 row_spec(KV_W), row_spec(KV_W), row_spec(KV_W), vt_spec(KEY_CHUNK), row_spec(KV_W),
                 vt_spec(V_TILE),
                 pl.BlockSpec((1, tm // Q_BLOCK, SUBLANES, GQ), lambda i, j: (i, j, 0, 0)),
                 row_spec(cw)]
    out_shape = [jax.ShapeDtypeStruct((b, nq, N_KV, HEAD_DIM, GQ), BF16),
                 kv(F32), kv(F32), kv(BF16), vt_shape(KEY_CHUNK), kv(BF16), vt_shape(V_TILE),
                 jax.ShapeDtypeStruct((b, nq, SUBLANES, GQ), F32),
                 jax.ShapeDtypeStruct((b, t, cw), BF16)]
    return pl.pallas_call(
        functools.partial(_inproj_kernel, conv_w=cw),
        grid=(b, t // tm),
        in_specs=[row_spec(d),
                  pl.BlockSpec((1, N_MOD, d), lambda i, j: (i, 0, 0)),
                  full((1, d)), full(w_nat.shape), full(w_tr.shape), full((ATTN_W, 1)),
                  full((1, KV_W)), full((CONV_K, cw)), full((ATTN_W, ATTN_W)), full((KV_W, KV_W))],
        out_specs=out_specs,
        out_shape=out_shape,
        scratch_shapes=[pltpu.VMEM((SUBLANES, cw), F32)],
        compiler_params=_cparams(2),
        name="inproj",
    )(x, mod, norm1, w_nat, w_tr, qn_col, kn_t, conv_w, bdq, bdk)


def _compress_kernel(kx_ref, vx_ref, pek_ref, pev_ref, w1k_ref, w1v_ref, w2k_ref, w2vt_ref,
                     kn_ref, bdk_ref, kc_out, vct_out):
    def hidden(x_ref, pe_ref, w1_ref):
        xv = x_ref[0]
        u = jnp.dot((xv + pe_ref[0:1, :]).astype(BF16), w1_ref[0], preferred_element_type=F32)
        v = jnp.dot((xv + pe_ref[1:2, :]).astype(BF16), w1_ref[1], preferred_element_type=F32)
        n = u.shape[0]
        hid = u + pltpu.roll(v, n - 1, 0)
        return jax.nn.gelu(hid, approximate=True).astype(BF16)

    kc = jnp.dot(hidden(kx_ref, pek_ref, w1k_ref), w2k_ref[...], preferred_element_type=F32)
    ssq = jnp.dot((kc * kc).astype(BF16), bdk_ref[...], preferred_element_type=F32)
    kc_out[0] = (kc * lax.rsqrt(ssq * (1.0 / HEAD_DIM) + EPS) * kn_ref[...]).astype(BF16)
    vct_out[0] = _with_ones_rows(lax.dot_general(w2vt_ref[...], hidden(vx_ref, pev_ref, w1v_ref), _NT,
                                                 preferred_element_type=F32).astype(BF16))


def _compress_call(kx, vx, pek, pev, w1k, w1v, w2k, w2vt, kn_t, bdk):
    b, nrow, wide = kx.shape
    hid2 = w1k.shape[2]
    full = lambda shp: pl.BlockSpec(shp, lambda i: (0,) * len(shp))
    xs = pl.BlockSpec((1, nrow, wide), lambda i: (i, 0, 0))
    return pl.pallas_call(
        _compress_kernel,
        grid=(b,),
        in_specs=[xs, xs, full((2, wide)), full((2, wide)), full((2, wide, hid2)),
                  full((2, wide, hid2)), full((hid2, KV_W)), full((KV_W, hid2)),
                  full((1, KV_W)), full((KV_W, KV_W))],
        out_specs=[pl.BlockSpec((1, nrow, KV_W), lambda i: (i, 0, 0)),
                   pl.BlockSpec((1, V_TILE_ROWS, nrow), lambda i: (i, 0, 0))],
        out_shape=[jax.ShapeDtypeStruct((b, nrow, KV_W), BF16),
                   jax.ShapeDtypeStruct((b, V_TILE_ROWS, nrow), BF16)],
        compiler_params=_cparams(1),
        name="compress",
    )(kx, vx, pek, pev, w1k, w1v, w2k, w2vt, kn_t, bdk)


def _attn_kernel(q_ref, kc_ref, vct_ref, ks_ref, vst_ref, kw_ref, vwt_ref, gt_ref, ovt_ref,
                 eb_ref, sb_ref, wb_ref, far_ref, o_ref, sel_scr, acc_scr, out_scr, qp_scr, s_scr,
                 fin_scr, *, n_cmp_rows, eb_shift):
    n_sb = sel_scr.shape[1]
    ci = jnp.minimum(pl.program_id(1), n_sb - 1)
    par = lax.rem(ci, 2)
    n_total = ks_ref.shape[1] // KEY_CHUNK
    vsl = lambda g: slice(g * V_ROWS, (g + 1) * V_ROWS)

    @pl.when((pl.program_id(0) == 0) & (pl.program_id(1) == 0))
    def _():
        def zero(g, carry):
            fin_scr[g] = jnp.zeros(fin_scr.shape[1:], F32)
            return carry
        lax.fori_loop(0, N_KV, zero, 0)

    pieces = []
    for g in range(N_KV):
        for half in range(GQ // LANES):
            a = fin_scr[g, :, half * LANES:(half + 1) * LANES]
            stacked = jnp.concatenate([a, pltpu.roll(a, Q_BLOCK, 1)], axis=0)
            pieces.append(stacked.T[:Q_BLOCK, :])
    o_ref[0] = jnp.concatenate(pieces, axis=1).astype(BF16)

    zeros_q = jnp.zeros((HEAD_DIM, GQ), BF16)
    qp_scr[0] = jnp.concatenate([q_ref[0, 0, 0], zeros_q], axis=0)
    qp_scr[1] = jnp.concatenate([zeros_q, q_ref[0, 0, 1]], axis=0)

    def chunk_qk(g, c):
        kk = ks_ref[0, pl.ds(pl.multiple_of(c * KEY_CHUNK, KEY_CHUNK), KEY_CHUNK), :]
        return jnp.dot(kk, qp_scr[g], preferred_element_type=F32).astype(BF16)

    n_chunks = ci // BLOCKS_PER_CHUNK + 1
    far_groups = jnp.maximum(ci - (NEAR_TILES - 1), 0) // (BLOCKS_PER_CHUNK * FAR_CHUNKS)
    n_wide = far_groups // 2
    n_far = far_groups - 2 * n_wide
    c_far = n_wide * (2 * FAR_CHUNKS)
    c0 = far_groups * FAR_CHUNKS
    n_near = (n_chunks - c0 + NEAR_CHUNKS - 1) // NEAR_CHUNKS

    def near_units(i):
        return [(g, jnp.minimum(c0 + NEAR_CHUNKS * i + j, n_total - 1))
                for g in range(N_KV) for j in range(NEAR_CHUNKS)]

    def far_units(i):
        return [(g, c_far + i * FAR_CHUNKS + j) for g in range(N_KV) for j in range(FAR_CHUNKS)]

    def wide_units(i):
        return [(g, i * 2 * FAR_CHUNKS + j) for g in range(N_KV) for j in range(2 * FAR_CHUNKS)]

    def pick_units(options):
        units = options[-1][1][:QK_AHEAD]
        for cond, cand in reversed(options[:-1]):
            units = [(jnp.where(cond, g_a, g_b), jnp.where(cond, c_a, c_b))
                     for (g_a, c_a), (g_b, c_b) in zip(cand[:QK_AHEAD], units)]
        return [(g, jnp.minimum(c, n_total - 1)) for g, c in units]

    def prefetch_scores(units):
        for k in range(QK_AHEAD):
            s_scr[k] = chunk_qk(*units[k])

    lane = lax.broadcasted_iota(jnp.int32, (n_sb, LANES), 1)
    cmp_c = n_cmp_rows - 4
    e0 = cmp_c - 4 * ci + jnp.where(par == 0, eb_shift[0], eb_shift[1])
    e0 = pl.multiple_of(e0, 8)
    o_cmp = []
    imp = []
    w0 = ci // 2 - (WIN_TILES - 1)
    win_tiles = [jnp.maximum(w0 + j, 0) for j in range(WIN_TILES)]
    s_cmp = [jnp.dot(kc_ref[0], qp_scr[g], preferred_element_type=F32) for g in range(N_KV)]
    for g in range(N_KV):
        sc = s_cmp[g] + eb_ref[par, g, pl.ds(e0, n_cmp_rows), :]
        m = jnp.max(sc, axis=0, keepdims=True)
        e = jnp.exp2(sc - m).astype(BF16)
        ov = jnp.dot(vct_ref[0, vsl(g), :], e, preferred_element_type=F32)
        inv = jnp.where(m > 0.5 * NEG, 1.0 / ov[HEAD_DIM:HEAD_DIM + 1, :], 0.0)
        o_cmp.append(ov[:HEAD_DIM, :] * inv)
        ir = jnp.dot(ovt_ref[...], e, preferred_element_type=F32) * inv
        a = ir[:, :LANES] + ir[:, LANES:]
        imp.append(a + pltpu.roll(a, Q_BLOCK, 1))

    s_win = [[jnp.dot(kw_ref[0, pl.ds(pl.multiple_of(tj * V_TILE, V_TILE), V_TILE), :], qp_scr[g],
                      preferred_element_type=F32).astype(BF16) for tj in win_tiles]
             for g in range(N_KV)]

    prefetch_scores(pick_units([(n_wide > 0, wide_units(0)), (n_far > 0, far_units(0)),
                                (None, near_units(0))]))

    jidx = lax.broadcasted_iota(jnp.int32, (n_sb, LANES), 0)
    jf = jidx.astype(F32)
    valid = jidx <= ci
    forced = (jidx == 0) | (jidx == ci) | (jidx == ci - 1)
    score = jnp.where(valid, jnp.where(forced, -2.0, jnp.where(lane < Q_BLOCK, imp[0], imp[1])),
                      -1.0)
    sel = jnp.where(forced, 1.0, 0.0)

    def first_max(score):
        pairs = [(score[r:r + SUBLANES], jf[r:r + SUBLANES]) for r in range(0, n_sb, SUBLANES)]
        while len(pairs) > 1:
            nxt = []
            for (va, ia), (vb, ib) in zip(pairs[0::2], pairs[1::2]):
                keep = va >= vb
                nxt.append((jnp.where(keep, va, vb), jnp.where(keep, ia, ib)))
            pairs = nxt + pairs[len(pairs) - len(pairs) % 2:]
        v8, i8 = pairs[0]
        mx = jnp.max(v8, axis=0, keepdims=True)
        return jnp.min(jnp.where(v8 == mx, i8, float(n_sb)), axis=0, keepdims=True)

    for _ in range(min(N_SELECT, n_sb) - 3):
        first = first_max(score)
        hit = jf == first
        sel = jnp.where(hit, 1.0, sel)
        score = jnp.where(hit, -2.0, score)
    selneg = jnp.where((sel > 0.5) & valid, 0.0, NEG)
    swapped = pltpu.roll(selneg, Q_BLOCK, 1)
    left = jnp.where(lane < Q_BLOCK, selneg, swapped)
    right = jnp.where(lane < Q_BLOCK, swapped, selneg)
    sel_scr[0] = jnp.concatenate([left, left], axis=1)
    sel_scr[1] = jnp.concatenate([right, right], axis=1)

    for g in range(N_KV):
        parts = []
        for i in range(2 * WIN_TILES):
            delta = par + WIN_BLOCKS - i
            ok = (delta >= 0) & (delta <= WIN_BLOCKS) & (delta <= ci)
            tile = jnp.where(ok, delta, WIN_BLOCKS + 1)
            half = s_win[g][i // 2][(i % 2) * SEL_BLOCK:(i % 2 + 1) * SEL_BLOCK, :]
            parts.append(half + wb_ref[g, tile])
        s = jnp.concatenate(parts, axis=0)
        m = jnp.max(s, axis=0, keepdims=True)
        pb = jnp.exp2(s - m)
        o_win = jnp.zeros((V_ROWS, GQ), F32)
        for j, tj in enumerate(win_tiles):
            o_win = o_win + jnp.dot(vwt_ref[0, tj, vsl(g), :], pb[j * V_TILE:(j + 1) * V_TILE, :],
                                    preferred_element_type=F32)
        w_scale = gt_ref[0, 0, 2 * N_KV + g:2 * N_KV + g + 1, :] * (1.0 / o_win[HEAD_DIM:HEAD_DIM + 1, :])
        out_scr[g] = gt_ref[0, 0, g:g + 1, :] * o_cmp[g] + w_scale * o_win[:HEAD_DIM, :]
        acc_scr[g] = jnp.zeros((V_ROWS, GQ), F32)

    def chunk_softmax(s, g, c, near, extra):
        parts = []
        for i in range(BLOCKS_PER_CHUNK):
            kb = c * BLOCKS_PER_CHUNK + i
            blk = s[i * SEL_BLOCK:(i + 1) * SEL_BLOCK, :]
            mrow = sel_scr[g, pl.ds(kb, 1), :]
            if extra is not None:
                mrow = mrow + extra
            if near:
                tile = jnp.clip(ci - kb, 0, NEAR_TILES)
                blk = blk + sb_ref[g, tile] + mrow.astype(BF16)
            else:
                blk = blk + (mrow + far_ref[g]).astype(BF16)
            parts.append(blk)
        s = jnp.concatenate(parts, axis=0)
        m_c = jnp.max(s, axis=0, keepdims=True)
        return m_c.astype(F32), jnp.exp2(s - m_c)

    def chunk_pv(pb, g, c):
        return jnp.dot(vst_ref[0, c, vsl(g), :], pb, preferred_element_type=F32)

    def merge_step(units, extras, near, next_units, carry):
        scores = {k: s_scr[k] for k in range(QK_AHEAD)}
        results = []
        for k, (g, c) in enumerate(units):
            m_c, pb = chunk_softmax(scores.pop(k), g, c, near, extras[k])
            if k + QK_AHEAD < len(units):
                scores[k + QK_AHEAD] = chunk_qk(*units[k + QK_AHEAD])
            else:
                j = k + QK_AHEAD - len(units)
                s_scr[j] = chunk_qk(*next_units[j])
            results.append((m_c, chunk_pv(pb, g, c)))
        per_g = len(units) // N_KV
        new = []
        for g in range(N_KV):
            m_run = carry[g]
            stats = results[g * per_g:(g + 1) * per_g]
            m_new = m_run
            for m_c, _ in stats:
                m_new = jnp.maximum(m_new, m_c)
            acc = jnp.exp2(m_run - m_new) * acc_scr[g]
            for m_c, pv in stats:
                acc = acc + jnp.exp2(m_c - m_new) * pv
            acc_scr[g] = acc
            new.append(m_new)
        return tuple(new)

    def wide_step(i, carry):
        units = wide_units(i)
        nxt = pick_units([(i + 1 < n_wide, wide_units(i + 1)), (n_far > 0, far_units(0)),
                          (None, near_units(0))])
        return merge_step(units, [None] * len(units), False, nxt, carry)

    def far_step(i, carry):
        units = far_units(i)
        return merge_step(units, [None] * len(units), False, pick_units([(None, near_units(0))]), carry)

    def near_step(i, carry):
        units = near_units(i)
        extras = [None if j == 0 else jnp.where(c0 + NEAR_CHUNKS * i + j < n_chunks, 0.0, NEG)
                  for _ in range(N_KV) for j in range(NEAR_CHUNKS)]
        return merge_step(units, extras, True, near_units(i + 1), carry)

    init = (jnp.full((1, GQ), NEG, F32),) * N_KV
    carry = lax.fori_loop(0, n_wide, wide_step, init)
    carry = lax.fori_loop(0, n_far, far_step, carry)
    lax.fori_loop(0, n_near, near_step, carry)

    for g in range(N_KV):
        acc = acc_scr[g]
        scale = gt_ref[0, 0, N_KV + g:N_KV + g + 1, :] * (1.0 / acc[HEAD_DIM:HEAD_DIM + 1, :])
        fin_scr[g] = out_scr[g] + scale * acc[:HEAD_DIM, :]


def _attn_call(qt, kc, vct, ks, vst, kw, vwt, gt, ovt, ebank, sbank, wbank, far, eb_shift):
    b, nq = qt.shape[0], qt.shape[1]
    t = ks.shape[1]
    n_cmp_rows = kc.shape[1]
    n_sb = t // SEL_BLOCK
    per_b = lambda shp: pl.BlockSpec((1,) + shp[1:], lambda i, j: (i,) + (0,) * (len(shp) - 1))
    per_q = lambda shp: pl.BlockSpec(
        (1, 1) + shp[2:], lambda i, j: (i, jnp.minimum(j, nq - 1)) + (0,) * (len(shp) - 2))
    full = lambda shp: pl.BlockSpec(shp, lambda i, j: (0,) * len(shp))
    args = (qt, kc, vct, ks, vst, kw, vwt, gt, ovt, ebank, sbank, wbank, far)
    specs = [per_q(qt.shape), per_b(kc.shape), per_b(vct.shape), per_b(ks.shape), per_b(vst.shape),
             per_b(kw.shape), per_b(vwt.shape), per_q(gt.shape), full(ovt.shape),
             full(ebank.shape), full(sbank.shape), full(wbank.shape), full(far.shape)]
    acc_like = pltpu.VMEM((N_KV, HEAD_DIM, GQ), F32)
    return pl.pallas_call(
        functools.partial(_attn_kernel, n_cmp_rows=n_cmp_rows, eb_shift=eb_shift),
        grid=(b, nq + 1),
        in_specs=specs,
        out_specs=pl.BlockSpec((1, Q_BLOCK, ATTN_W), lambda i, j: (i, jnp.maximum(j - 1, 0), 0)),
        out_shape=jax.ShapeDtypeStruct((b, t, ATTN_W), BF16),
        scratch_shapes=[pltpu.VMEM((N_KV, n_sb, GQ), F32),
                        pltpu.VMEM((N_KV, V_ROWS, GQ), F32),
                        acc_like,
                        pltpu.VMEM((N_KV, KV_W, GQ), BF16),
                        pltpu.VMEM((QK_AHEAD, KEY_CHUNK, GQ), BF16),
                        acc_like],
        compiler_params=_cparams(2),
        name="nsa_attention",
    )(*args)


def _ffn_kernel(x_ref, a_ref, c_ref, mod_ref, n2_ref, wo_ref, w1_ref, w2_ref, o_ref):
    aw = a_ref.shape[2]
    mix = jnp.dot(a_ref[0], wo_ref[0:aw, :], preferred_element_type=F32)
    mix = mix + jnp.dot(c_ref[0], wo_ref[aw:, :], preferred_element_type=F32)
    x1 = x_ref[0] + mod_ref[0, 2:3, :] * mix
    ms = jnp.mean(x1 * x1, axis=-1, keepdims=True)
    y = x1 * lax.rsqrt(ms + EPS) * n2_ref[...]
    h2 = (y * (1.0 + mod_ref[0, 4:5, :]) + mod_ref[0, 3:4, :]).astype(BF16)
    d_ff = w1_ref.shape[1]
    ff = jnp.zeros(x1.shape, F32)
    for j in range(d_ff // FF_CHUNK):
        a = jnp.dot(h2, w1_ref[:, j * FF_CHUNK:(j + 1) * FF_CHUNK], preferred_element_type=F32)
        a = jnp.maximum(a, 0.0)
        ff = ff + jnp.dot((a * a).astype(BF16), w2_ref[j * FF_CHUNK:(j + 1) * FF_CHUNK, :],
                          preferred_element_type=F32)
    o_ref[0] = x1 + mod_ref[0, 5:6, :] * ff


def _ffn_call(x, attn, conv, mod, norm2, w_out, w_ff1, w_ff2):
    b, t, d = x.shape
    tm = min(ROW_TILE, t)
    row_spec = lambda w: pl.BlockSpec((1, tm, w), lambda i, j: (i, j, 0))
    full = lambda shp: pl.BlockSpec(shp, lambda i, j: (0,) * len(shp),
                                    pipeline_mode=pl.Buffered(1))
    return pl.pallas_call(
        _ffn_kernel,
        grid=(b, t // tm),
        in_specs=[row_spec(d), row_spec(attn.shape[2]), row_spec(conv.shape[2]),
                  pl.BlockSpec((1, N_MOD, d), lambda i, j: (i, 0, 0)),
                  full((1, d)), full(w_out.shape), full(w_ff1.shape), full(w_ff2.shape)],
        out_specs=row_spec(d),
        out_shape=jax.ShapeDtypeStruct((b, t, d), F32),
        compiler_params=_cparams(2),
        name="outproj_mlp",
    )(x, attn, conv, mod, norm2, w_out, w_ff1, w_ff2)


def _block_diag_ones(n):
    idx = np.arange(n) // HEAD_DIM
    return jnp.asarray(idx[:, None] == idx[None, :], dtype=BF16)


def _pack_w_in(w_in):
    d = w_in.shape[0]
    conv_w = d - ATTN_W
    sizes = [ATTN_W] + [KV_W] * 6 + [N_BRANCH * N_HEADS] + [conv_w] * 3
    offs = np.concatenate([[0], np.cumsum(sizes)])
    part = lambda i: w_in[:, offs[i]:offs[i + 1]]
    q, kc, vc, ks, vs, kw, vw, g, cgate, bgate, u = (part(i) for i in range(11))
    w_nat = jnp.concatenate([kc, vc, ks, kw, cgate, bgate, u], axis=1).astype(BF16)
    gt = g.reshape(d, N_KV, GQA, N_BRANCH).transpose(2, 3, 1, 0).reshape(GQA, N_BRANCH * N_KV, d)
    gt = jnp.pad(gt, ((0, 0), (0, SUBLANES - N_BRANCH * N_KV), (0, 0))).reshape(_G_ROWS, d)
    w_tr = jnp.concatenate([q.T, vs.T, vw.T, gt], axis=0).astype(BF16)
    return w_nat, w_tr


def _expand_w1(w1):
    hid = w1.shape[1]
    w = w1.reshape(2, CMP_STRIDE, 1, HEAD_DIM, 1, hid)
    eye = jnp.eye(N_KV, dtype=w1.dtype).reshape(1, 1, N_KV, 1, N_KV, 1)
    return (w * eye).reshape(2, CMP_STRIDE * KV_W, N_KV * hid).astype(BF16)


def _expand_w2(w2):
    hid = w2.shape[0]
    eye = jnp.eye(N_KV, dtype=w2.dtype).reshape(N_KV, 1, N_KV, 1)
    return (w2.reshape(1, hid, 1, HEAD_DIM) * eye).reshape(N_KV * hid, KV_W).astype(BF16)


def _expand_pe(pe):
    p = pe.reshape(2, CMP_STRIDE, 1, HEAD_DIM)
    return jnp.broadcast_to(p, (2, CMP_STRIDE, N_KV, HEAD_DIM)).reshape(2, CMP_STRIDE * KV_W)


def _bucket_thresholds():
    n = np.arange(2 * REL_MAX_DIST)
    max_exact = REL_BUCKETS // 2
    nf = np.maximum(n, max_exact).astype(np.float32)
    ratio = np.log(nf / np.float32(max_exact)) / np.float32(math.log(REL_MAX_DIST / max_exact))
    large = max_exact + (ratio * np.float32(REL_BUCKETS - max_exact)).astype(np.int32)
    table = np.where(n < max_exact, n, np.minimum(large, REL_BUCKETS - 1))
    return tuple(int(np.searchsorted(table, k, side="left")) for k in range(REL_BUCKETS))


def _bank_call(bias_rows, lead, n_tiles, dist_fn, name):
    nl = len(lead)
    thr = _bucket_thresholds()

    def body(rows_ref, o_ref):
        lead_ids = [pl.program_id(a) for a in range(nl)]
        row = lax.broadcasted_iota(jnp.int32, (SEL_BLOCK, GQ), 0)
        qi = lax.broadcasted_iota(jnp.int32, (SEL_BLOCK, GQ), 1) & (Q_BLOCK - 1)

        def tile(t, carry):
            dist, ok = dist_fn(lead_ids, t, row, qi)
            v = jnp.broadcast_to(rows_ref[0, 0:1, :], (SEL_BLOCK, GQ))
            for k in range(1, REL_BUCKETS):
                v = jnp.where(dist >= thr[k], rows_ref[0, k:k + 1, :], v)
            o_ref[(0,) * (nl + 1) + (t,)] = jnp.where(ok, v, NEG)
            return carry

        lax.fori_loop(0, n_tiles, tile, 0)

    return pl.pallas_call(
        body,
        grid=tuple(lead) + (N_KV,),
        in_specs=[pl.BlockSpec((1, REL_BUCKETS, GQ), lambda *i: (i[nl], 0, 0))],
        out_specs=pl.BlockSpec((1,) * (nl + 1) + (n_tiles, SEL_BLOCK, GQ),
                               lambda *i: tuple(i) + (0, 0, 0)),
        out_shape=jax.ShapeDtypeStruct(tuple(lead) + (N_KV, n_tiles, SEL_BLOCK, GQ), F32),
        compiler_params=_cparams(nl + 1),
        name=name,
    )(bias_rows)


def _bias_banks(rel_bias, t):
    n_cmp_rows = t // CMP_STRIDE
    rows = rel_bias.reshape(REL_BUCKETS, N_KV, GQA).transpose(1, 0, 2)
    rows = jnp.repeat(rows, Q_BLOCK, axis=2) * LOG2E

    def sel_dist(lead, tile, row, qi):
        dist = SEL_BLOCK * tile + qi - row
        return dist, dist >= 0

    sbank = _bank_call(rows, (), NEAR_TILES + 1, sel_dist, "bias_bank_sel")
    far = sbank[:, NEAR_TILES, 0:1, :]

    def win_dist(lead, tile, row, qi):
        dist = SEL_BLOCK * tile + qi - row
        return dist, (dist >= 0) & (dist < WINDOW)

    wbank = _bank_call(rows, (), WIN_BLOCKS + 2, win_dist, "bias_bank_win")

    cmp_c = n_cmp_rows - 4
    shifts = tuple(int((-(cmp_c - 4 * p)) % 8) for p in range(2))
    n_tiles = (cmp_c + n_cmp_rows + 8 + SEL_BLOCK - 1) // SEL_BLOCK

    def cmp_dist(lead, tile, row, qi):
        e = SEL_BLOCK * tile + row - jnp.where(lead[0] == 0, shifts[0], shifts[1])
        dist = qi - CMP_STRIDE * e + (CMP_STRIDE * cmp_c - (CMP_BLOCK - 1))
        return dist, (dist >= 0) & (e >= 0)

    ebank = _bank_call(rows, (2,), n_tiles, cmp_dist, "bias_bank_cmp")
    ebank = ebank.reshape(2, N_KV, n_tiles * SEL_BLOCK, GQ)
    return ebank, sbank, wbank, far, shifts


def _overlap_t(t):
    n_cmp_rows = t // CMP_STRIDE
    n_sb = t // SEL_BLOCK
    c_start = np.arange(n_cmp_rows)[None, :] * CMP_STRIDE
    s_start = np.arange(n_sb)[:, None] * SEL_BLOCK
    ov = np.clip(np.minimum(c_start + CMP_BLOCK, s_start + SEL_BLOCK)
                 - np.maximum(c_start, s_start), 0, None) / CMP_BLOCK
    ov[:, n_cmp_rows - 1] = 0.0
    return jnp.asarray(ov, dtype=BF16)


def _layer(x, c_pad, w_in, q_norm, k_norm, cmp_pe_k, cmp_w1_k, cmp_w2_k, cmp_pe_v, cmp_w1_v,
           cmp_w2_v, rel_bias, conv_w, w_out, norm1, norm2, w_ada, b_ada, w_ff1, w_ff2):
    b, t, d = x.shape
    scale = HEAD_DIM ** -0.5

    mod = _mod_call(c_pad, w_ada, b_ada)[:b].reshape(b, N_MOD, d)

    qn_col = (jnp.tile(q_norm, N_HEADS) * (scale * LOG2E)).reshape(ATTN_W, 1)
    kn_t = jnp.tile(k_norm, N_KV).reshape(1, KV_W)
    bdq = _block_diag_ones(ATTN_W)
    bdk = _block_diag_ones(KV_W)
    w_nat, w_tr = _pack_w_in(w_in)
    qt, kc_raw, vc_raw, ks, vst, kw, vwt, gt, conv = _inproj_call(
        x, mod, norm1.reshape(1, d), w_nat, w_tr, qn_col, kn_t, conv_w, bdq, bdk)

    n_cmp_rows = t // CMP_STRIDE
    kc, vct = _compress_call(
        kc_raw.reshape(b, n_cmp_rows, CMP_STRIDE * KV_W), vc_raw.reshape(b, n_cmp_rows, CMP_STRIDE * KV_W),
        _expand_pe(cmp_pe_k), _expand_pe(cmp_pe_v), _expand_w1(cmp_w1_k), _expand_w1(cmp_w1_v),
        _expand_w2(cmp_w2_k), _expand_w2(cmp_w2_v).T, kn_t, bdk)

    ebank, sbank, wbank, far, eb_shift = _bias_banks(rel_bias, t)
    attn = _attn_call(qt, kc, vct, ks, vst, kw, vwt, gt, _overlap_t(t), ebank, sbank.astype(BF16),
                      wbank.astype(BF16), far, eb_shift)

    return _ffn_call(x, attn, conv, mod, norm2.reshape(1, d), w_out.astype(BF16),
                     w_ff1.astype(BF16), w_ff2.astype(BF16))


def kernel(x, c, w_in, q_norm, k_norm, cmp_pe_k, cmp_w1_k, cmp_w2_k, cmp_pe_v, cmp_w1_v, cmp_w2_v,
           rel_bias, conv_w, w_out, norm1, norm2, w_ada, b_ada, w_ff1, w_ff2):
    b = x.shape[0]
    c_pad = jnp.pad(c, ((0, (-b) % 8), (0, 0)))
    for l in range(w_in.shape[0]):
        x = _layer(x, c_pad, w_in[l], q_norm[l], k_norm[l], cmp_pe_k[l], cmp_w1_k[l], cmp_w2_k[l],
                   cmp_pe_v[l], cmp_w1_v[l], cmp_w2_v[l], rel_bias, conv_w[l], w_out[l],
                   norm1[l], norm2[l], w_ada[l], b_ada[l], w_ff1[l], w_ff2[l])
    return x
```

```python
import functools
import math

import numpy as np
import jax
import jax.numpy as jnp
from jax import lax
from jax.experimental import pallas as pl
from jax.experimental.pallas import tpu as pltpu

HEAD_DIM = 64
N_HEADS = 8
N_KV = 2
GQA = N_HEADS // N_KV
ATTN_W = N_HEADS * HEAD_DIM
KV_W = N_KV * HEAD_DIM
CONV_K = 3
CMP_BLOCK = 32
CMP_STRIDE = 16
CMP_HIDDEN = 256
SEL_BLOCK = 64
N_SELECT = 16
WINDOW = 512
Q_BLOCK = 64
REL_BUCKETS = 32
REL_MAX_DIST = 1024
N_MOD = 6
N_BRANCH = 3
EPS = 1e-6
NEG = -1e30

LANES = 128
SUBLANES = 8
GQ = GQA * Q_BLOCK
KEY_CHUNK = 256
BLOCKS_PER_CHUNK = KEY_CHUNK // SEL_BLOCK
V_TILE = 128
BF16_ROWS = 16
V_ROWS = HEAD_DIM + BF16_ROWS
V_TILE_ROWS = N_KV * V_ROWS
WIN_BLOCKS = WINDOW // SEL_BLOCK
WIN_TILES = WINDOW // V_TILE + 1
NEAR_TILES = (REL_MAX_DIST + Q_BLOCK - 1) // SEL_BLOCK + 1
ROW_TILE = 512
INPROJ_SPLIT = 2
FAR_CHUNKS = 4
NEAR_CHUNKS = 4
QK_AHEAD = 4
LOG2E = math.log2(math.e)
FF_CHUNK = 1024
VMEM_LIMIT = 56 * 1024 * 1024

F32 = jnp.float32
BF16 = jnp.bfloat16
_NT = (((1,), (1,)), ((), ()))


def _cparams(n_axes):
    return pltpu.CompilerParams(dimension_semantics=("arbitrary",) * n_axes,
                                vmem_limit_bytes=VMEM_LIMIT)


def _with_ones_rows(vt):
    ones = jnp.ones((BF16_ROWS, vt.shape[1]), vt.dtype)
    parts = []
    for g in range(N_KV):
        parts += [vt[g * HEAD_DIM:(g + 1) * HEAD_DIM, :], ones]
    return jnp.concatenate(parts, axis=0)


def _swap_halves(p0, p1):
    low = lax.broadcasted_iota(jnp.int32, p0.shape, 1) < LANES // 2
    return (jnp.where(low, p0, pltpu.roll(p1, LANES // 2, 1)),
            jnp.where(low, pltpu.roll(p0, LANES // 2, 1), p1))


def _mod_kernel(c_ref, w_ref, b_ref, o_ref):
    c = c_ref[...]
    a = c * jax.nn.sigmoid(c)
    o_ref[...] = jnp.dot(a, w_ref[...], preferred_element_type=F32,
                         precision=lax.Precision.HIGHEST) + b_ref[...]


def _mod_call(c_pad, w_ada, b_ada):
    rows, d = c_pad.shape
    n = w_ada.shape[1]
    tn = 1024
    return pl.pallas_call(
        _mod_kernel,
        grid=(n // tn,),
        in_specs=[pl.BlockSpec((rows, d), lambda j: (0, 0)),
                  pl.BlockSpec((d, tn), lambda j: (0, j)),
                  pl.BlockSpec((1, tn), lambda j: (0, j))],
        out_specs=pl.BlockSpec((rows, tn), lambda j: (0, j)),
        out_shape=jax.ShapeDtypeStruct((rows, n), F32),
        compiler_params=_cparams(1),
        name="adaln_mod",
    )(c_pad, w_ada, b_ada.reshape(1, n))


_N_KC, _N_VC, _N_KS, _N_KW, _N_CONV = 0, KV_W, 2 * KV_W, 3 * KV_W, 4 * KV_W
_T_Q, _T_VS, _T_VW, _T_G = 0, ATTN_W, ATTN_W + KV_W, ATTN_W + 2 * KV_W
_G_ROWS = GQA * SUBLANES


def _inproj_kernel(x_ref, mod_ref, n1_ref, wn_ref, wt_ref, qn_ref, kn_ref, cw_ref, bdq_ref, bdk_ref,
                   qt_out, kc_out, vc_out, ks_out, vst_out, kw_out, vwt_out, gt_out, conv_out,
                   carry, *, conv_w):
    t = pl.program_id(1)
    tm = x_ref.shape[1]
    sub = tm // INPROJ_SPLIT

    @pl.when(t == 0)
    def _():
        carry[...] = jnp.zeros_like(carry)

    prev2, prev1 = carry[6:7, :], carry[7:8, :]
    c_bg = _N_CONV + conv_w
    c_u = c_bg + conv_w

    for part in range(INPROJ_SPLIT):
        rows = slice(part * sub, (part + 1) * sub)
        x = x_ref[0, rows, :]
        ms = jnp.mean(x * x, axis=-1, keepdims=True)
        y = x * lax.rsqrt(ms + EPS) * n1_ref[...]
        h = (y * (1.0 + mod_ref[0, 1:2, :]) + mod_ref[0, 0:1, :]).astype(BF16)

        def proj(a, b):
            return jnp.dot(h, wn_ref[:, a:b], preferred_element_type=F32)

        def proj_t(a, b):
            return lax.dot_general(wt_ref[a:b, :], h, _NT, preferred_element_type=F32)

        def head_norm(v, gain):
            ssq = jnp.dot((v * v).astype(BF16), bdk_ref[...], preferred_element_type=F32)
            return v * lax.rsqrt(ssq * (1.0 / HEAD_DIM) + EPS) * gain

        qf = proj_t(_T_Q, _T_VS)
        vg = proj_t(_T_VS, _T_G + _G_ROWS)
        kv = proj(_N_KC, _N_CONV)
        cv = proj(_N_CONV, c_u + conv_w)

        ssq = jnp.dot(bdq_ref[...], (qf * qf).astype(BF16), preferred_element_type=F32)
        qf = qf * lax.rsqrt(ssq * (1.0 / HEAD_DIM) + EPS) * qn_ref[...]
        gf = jax.nn.sigmoid(vg[2 * KV_W:, :])
        for c in range(sub // LANES):
            blk = (part * sub) // Q_BLOCK + 2 * c
            cols = slice(c * LANES, (c + 1) * LANES)
            for g in range(N_KV):
                pc = [qf[(g * GQA + r) * HEAD_DIM:(g * GQA + r + 1) * HEAD_DIM, cols]
                      for r in range(GQA)]
                lo01, hi01 = _swap_halves(pc[0], pc[1])
                lo23, hi23 = _swap_halves(pc[2], pc[3])
                qt_out[0, blk, g] = jnp.concatenate([lo01, lo23], axis=1).astype(BF16)
                qt_out[0, blk + 1, g] = jnp.concatenate([hi01, hi23], axis=1).astype(BF16)
            pc = [gf[r * SUBLANES:(r + 1) * SUBLANES, cols] for r in range(GQA)]
            lo01, hi01 = _swap_halves(pc[0], pc[1])
            lo23, hi23 = _swap_halves(pc[2], pc[3])
            gt_out[0, blk] = jnp.concatenate([lo01, lo23], axis=1)
            gt_out[0, blk + 1] = jnp.concatenate([hi01, hi23], axis=1)

        vs_f = _with_ones_rows(vg[:KV_W, :].astype(BF16))
        vw_f = _with_ones_rows(vg[KV_W:2 * KV_W, :].astype(BF16))
        for j in range(sub // KEY_CHUNK):
            vst_out[0, (part * sub) // KEY_CHUNK + j] = vs_f[:, j * KEY_CHUNK:(j + 1) * KEY_CHUNK]
        for j in range(sub // V_TILE):
            vwt_out[0, (part * sub) // V_TILE + j] = vw_f[:, j * V_TILE:(j + 1) * V_TILE]

        kc_out[0, rows, :] = kv[:, _N_KC:_N_VC]
        vc_out[0, rows, :] = kv[:, _N_VC:_N_KS]
        ks_out[0, rows, :] = head_norm(kv[:, _N_KS:_N_KW], kn_ref[...]).astype(BF16)
        kw_out[0, rows, :] = head_norm(kv[:, _N_KW:_N_CONV], kn_ref[...]).astype(BF16)

        z = cv[:, :conv_w] * cv[:, 2 * conv_w:]
        row = lax.broadcasted_iota(jnp.int32, z.shape, 0)
        z1 = jnp.where(row == 0, prev1, pltpu.roll(z, 1, 0))
        z2 = jnp.where(row == 0, prev2, jnp.where(row == 1, prev1, pltpu.roll(z, 2, 0)))
        zc = cw_ref[0:1, :] * z2 + cw_ref[1:2, :] * z1 + cw_ref[2:3, :] * z
        conv_out[0, rows, :] = (cv[:, conv_w:2 * conv_w] * zc).astype(BF16)
        prev2, prev1 = z[sub - 2:sub - 1, :], z[sub - 1:sub, :]
        if part == INPROJ_SPLIT - 1:
            carry[...] = z[sub - SUBLANES:sub, :]


def _inproj_call(x, mod, norm1, w_nat, w_tr, qn_col, kn_t, conv_w, bdq, bdk):
    b, t, d = x.shape
    tm = min(ROW_TILE, t)
    cw = conv_w.shape[1]
    nq = t // Q_BLOCK
    row_spec = lambda w: pl.BlockSpec((1, tm, w), lambda i, j: (i, j, 0))
    full = lambda shp: pl.BlockSpec(shp, lambda i, j: (0,) * len(shp))
    vt_spec = lambda w: pl.BlockSpec((1, tm // w, V_TILE_ROWS, w), lambda i, j: (i, j, 0, 0))
    vt_shape = lambda w: jax.ShapeDtypeStruct((b, t // w, V_TILE_ROWS, w), BF16)
    kv = lambda dt: jax.ShapeDtypeStruct((b, t, KV_W), dt)
    out_specs = [pl.BlockSpec((1, tm // Q_BLOCK, N_KV, HEAD_DIM, GQ), lambda i, j: (i, j, 0, 0, 0)),
                 row_spec(KV_W), row_spec(KV_W), row_spec(KV_W), vt_spec(KEY_CHUNK), row_spec(KV_W),
                 vt_spec(V_TILE),
                 pl.BlockSpec((1, tm // Q_BLOCK, SUBLANES, GQ), lambda i, j: (i, j, 0, 0)),
                 row_spec(cw)]
    out_shape = [jax.ShapeDtypeStruct((b, nq, N_KV, HEAD_DIM, GQ), BF16),
                 kv(F32), kv(F32), kv(BF16), vt_shape(KEY_CHUNK), kv(BF16), vt_shape(V_TILE),
                 jax.ShapeDtypeStruct((b, nq, SUBLANES, GQ), F32),
                 jax.ShapeDtypeStruct((b, t, cw), BF16)]
    return pl.pallas_call(
        functools.partial(_inproj_kernel, conv_w=cw),
        grid=(b, t // tm),
        in_specs=[row_spec(d),
                  pl.BlockSpec((1, N_MOD, d), lambda i, j: (i, 0, 0)),
                  full((1, d)), full(w_nat.shape), full(w_tr.shape), full((ATTN_W, 1)),
                  full((1, KV_W)), full((CONV_K, cw)), full((ATTN_W, ATTN_W)), full((KV_W, KV_W))],
        out_specs=out_specs,
        out_shape=out_shape,
        scratch_shapes=[pltpu.VMEM((SUBLANES, cw), F32)],
        compiler_params=_cparams(2),
        name="inproj",
    )(x, mod, norm1, w_nat, w_tr, qn_col, kn_t, conv_w, bdq, bdk)


def _compress_kernel(kx_ref, vx_ref, pek_ref, pev_ref, w1k_ref, w1v_ref, w2k_ref, w2vt_ref,
                     kn_ref, bdk_ref, kc_out, vct_out):
    def hidden(x_ref, pe_ref, w1_ref):
        xv = x_ref[0]
        u = jnp.dot((xv + pe_ref[0:1, :]).astype(BF16), w1_ref[0], preferred_element_type=F32)
        v = jnp.dot((xv + pe_ref[1:2, :]).astype(BF16), w1_ref[1], preferred_element_type=F32)
        n = u.shape[0]
        hid = u + pltpu.roll(v, n - 1, 0)
        return jax.nn.gelu(hid, approximate=True).astype(BF16)

    kc = jnp.dot(hidden(kx_ref, pek_ref, w1k_ref), w2k_ref[...], preferred_element_type=F32)
    ssq = jnp.dot((kc * kc).astype(BF16), bdk_ref[...], preferred_element_type=F32)
    kc_out[0] = (kc * lax.rsqrt(ssq * (1.0 / HEAD_DIM) + EPS) * kn_ref[...]).astype(BF16)
    vct_out[0] = _with_ones_rows(lax.dot_general(w2vt_ref[...], hidden(vx_ref, pev_ref, w1v_ref), _NT,
                                                 preferred_element_type=F32).astype(BF16))


def _compress_call(kx, vx, pek, pev, w1k, w1v, w2k, w2vt, kn_t, bdk):
    b, nrow, wide = kx.shape
    hid2 = w1k.shape[2]
    full = lambda shp: pl.BlockSpec(shp, lambda i: (0,) * len(shp))
    xs = pl.BlockSpec((1, nrow, wide), lambda i: (i, 0, 0))
    return pl.pallas_call(
        _compress_kernel,
        grid=(b,),
        in_specs=[xs, xs, full((2, wide)), full((2, wide)), full((2, wide, hid2)),
                  full((2, wide, hid2)), full((hid2, KV_W)), full((KV_W, hid2)),
                  full((1, KV_W)), full((KV_W, KV_W))],
        out_specs=[pl.BlockSpec((1, nrow, KV_W), lambda i: (i, 0, 0)),
                   pl.BlockSpec((1, V_TILE_ROWS, nrow), lambda i: (i, 0, 0))],
        out_shape=[jax.ShapeDtypeStruct((b, nrow, KV_W), BF16),
                   jax.ShapeDtypeStruct((b, V_TILE_ROWS, nrow), BF16)],
        compiler_params=_cparams(1),
        name="compress",
    )(kx, vx, pek, pev, w1k, w1v, w2k, w2vt, kn_t, bdk)


def _attn_kernel(q_ref, kc_ref, vct_ref, ks_ref, vst_ref, kw_ref, vwt_ref, gt_ref, ovt_ref,
                 eb_ref, sb_ref, wb_ref, far_ref, o_ref, sel_scr, acc_scr, out_scr, qp_scr, s_scr,
                 fin_scr, *, n_cmp_rows, eb_shift):
    n_sb = sel_scr.shape[1]
    ci = jnp.minimum(pl.program_id(1), n_sb - 1)
    par = lax.rem(ci, 2)
    n_total = ks_ref.shape[1] // KEY_CHUNK
    vsl = lambda g: slice(g * V_ROWS, (g + 1) * V_ROWS)

    @pl.when((pl.program_id(0) == 0) & (pl.program_id(1) == 0))
    def _():
        def zero(g, carry):
            fin_scr[g] = jnp.zeros(fin_scr.shape[1:], F32)
            return carry
        lax.fori_loop(0, N_KV, zero, 0)

    pieces = []
    for g in range(N_KV):
        for half in range(GQ // LANES):
            a = fin_scr[g, :, half * LANES:(half + 1) * LANES]
            stacked = jnp.concatenate([a, pltpu.roll(a, Q_BLOCK, 1)], axis=0)
            pieces.append(stacked.T[:Q_BLOCK, :])
    o_ref[0] = jnp.concatenate(pieces, axis=1).astype(BF16)

    zeros_q = jnp.zeros((HEAD_DIM, GQ), BF16)
    qp_scr[0] = jnp.concatenate([q_ref[0, 0, 0], zeros_q], axis=0)
    qp_scr[1] = jnp.concatenate([zeros_q, q_ref[0, 0, 1]], axis=0)

    def chunk_qk(g, c):
        kk = ks_ref[0, pl.ds(pl.multiple_of(c * KEY_CHUNK, KEY_CHUNK), KEY_CHUNK), :]
        return jnp.dot(kk, qp_scr[g], preferred_element_type=F32).astype(BF16)

    n_chunks = ci // BLOCKS_PER_CHUNK + 1
    far_groups = jnp.maximum(ci - (NEAR_TILES - 1), 0) // (BLOCKS_PER_CHUNK * FAR_CHUNKS)
    n_wide = far_groups // 2
    n_far = far_groups - 2 * n_wide
    c_far = n_wide * (2 * FAR_CHUNKS)
    c0 = far_groups * FAR_CHUNKS
    n_near = (n_chunks - c0 + NEAR_CHUNKS - 1) // NEAR_CHUNKS

    def near_units(i):
        return [(g, jnp.minimum(c0 + NEAR_CHUNKS * i + j, n_total - 1))
                for g in range(N_KV) for j in range(NEAR_CHUNKS)]

    def far_units(i):
        return [(g, c_far + i * FAR_CHUNKS + j) for g in range(N_KV) for j in range(FAR_CHUNKS)]

    def wide_units(i):
        return [(g, i * 2 * FAR_CHUNKS + j) for g in range(N_KV) for j in range(2 * FAR_CHUNKS)]

    def pick_units(options):
        units = options[-1][1][:QK_AHEAD]
        for cond, cand in reversed(options[:-1]):
            units = [(jnp.where(cond, g_a, g_b), jnp.where(cond, c_a, c_b))
                     for (g_a, c_a), (g_b, c_b) in zip(cand[:QK_AHEAD], units)]
        return [(g, jnp.minimum(c, n_total - 1)) for g, c in units]

    def prefetch_scores(units):
        for k in range(QK_AHEAD):
            s_scr[k] = chunk_qk(*units[k])

    lane = lax.broadcasted_iota(jnp.int32, (n_sb, LANES), 1)
    cmp_c = n_cmp_rows - 4
    e0 = cmp_c - 4 * ci + jnp.where(par == 0, eb_shift[0], eb_shift[1])
    e0 = pl.multiple_of(e0, 8)
    o_cmp = []
    imp = []
    w0 = ci // 2 - (WIN_TILES - 1)
    win_tiles = [jnp.maximum(w0 + j, 0) for j in range(WIN_TILES)]
    s_cmp = [jnp.dot(kc_ref[0], qp_scr[g], preferred_element_type=F32) for g in range(N_KV)]
    for g in range(N_KV):
        sc = s_cmp[g] + eb_ref[par, g, pl.ds(e0, n_cmp_rows), :]
        m = jnp.max(sc, axis=0, keepdims=True)
        e = jnp.exp2(sc - m).astype(BF16)
        ov = jnp.dot(vct_ref[0, vsl(g), :], e, preferred_element_type=F32)
        inv = jnp.where(m > 0.5 * NEG, 1.0 / ov[HEAD_DIM:HEAD_DIM + 1, :], 0.0)
        o_cmp.append(ov[:HEAD_DIM, :] * inv)
        ir = jnp.dot(ovt_ref[...], e, preferred_element_type=F32) * inv
        a = ir[:, :LANES] + ir[:, LANES:]
        imp.append(a + pltpu.roll(a, Q_BLOCK, 1))

    s_win = [[jnp.dot(kw_ref[0, pl.ds(pl.multiple_of(tj * V_TILE, V_TILE), V_TILE), :], qp_scr[g],
                      preferred_element_type=F32).astype(BF16) for tj in win_tiles]
             for g in range(N_KV)]

    prefetch_scores(pick_units([(n_wide > 0, wide_units(0)), (n_far > 0, far_units(0)),
                                (None, near_units(0))]))

    jidx = lax.broadcasted_iota(jnp.int32, (n_sb, LANES), 0)
    jf = jidx.astype(F32)
    valid = jidx <= ci
    forced = (jidx == 0) | (jidx == ci) | (jidx == ci - 1)
    score = jnp.where(valid, jnp.where(forced, -2.0, jnp.where(lane < Q_BLOCK, imp[0], imp[1])),
                      -1.0)
    sel = jnp.where(forced, 1.0, 0.0)

    def first_max(score):
        pairs = [(score[r:r + SUBLANES], jf[r:r + SUBLANES]) for r in range(0, n_sb, SUBLANES)]
        while len(pairs) > 1:
            nxt = []
            for (va, ia), (vb, ib) in zip(pairs[0::2], pairs[1::2]):
                keep = va >= vb
                nxt.append((jnp.where(keep, va, vb), jnp.where(keep, ia, ib)))
            pairs = nxt + pairs[len(pairs) - len(pairs) % 2:]
        v8, i8 = pairs[0]
        mx = jnp.max(v8, axis=0, keepdims=True)
        return jnp.min(jnp.where(v8 == mx, i8, float(n_sb)), axis=0, keepdims=True)

    for _ in range(min(N_SELECT, n_sb) - 3):
        first = first_max(score)
        hit = jf == first
        sel = jnp.where(hit, 1.0, sel)
        score = jnp.where(hit, -2.0, score)
    selneg = jnp.where((sel > 0.5) & valid, 0.0, NEG)
    swapped = pltpu.roll(selneg, Q_BLOCK, 1)
    left = jnp.where(lane < Q_BLOCK, selneg, swapped)
    right = jnp.where(lane < Q_BLOCK, swapped, selneg)
    sel_scr[0] = jnp.concatenate([left, left], axis=1)
    sel_scr[1] = jnp.concatenate([right, right], axis=1)

    for g in range(N_KV):
        parts = []
        for i in range(2 * WIN_TILES):
            delta = par + WIN_BLOCKS - i
            ok = (delta >= 0) & (delta <= WIN_BLOCKS) & (delta <= ci)
            tile = jnp.where(ok, delta, WIN_BLOCKS + 1)
            half = s_win[g][i // 2][(i % 2) * SEL_BLOCK:(i % 2 + 1) * SEL_BLOCK, :]
            parts.append(half + wb_ref[g, tile])
        s = jnp.concatenate(parts, axis=0)
        m = jnp.max(s, axis=0, keepdims=True)
        pb = jnp.exp2(s - m)
        o_win = jnp.zeros((V_ROWS, GQ), F32)
        for j, tj in enumerate(win_tiles):
            o_win = o_win + jnp.dot(vwt_ref[0, tj, vsl(g), :], pb[j * V_TILE:(j + 1) * V_TILE, :],
                                    preferred_element_type=F32)
        w_scale = gt_ref[0, 0, 2 * N_KV + g:2 * N_KV + g + 1, :] * (1.0 / o_win[HEAD_DIM:HEAD_DIM + 1, :])
        out_scr[g] = gt_ref[0, 0, g:g + 1, :] * o_cmp[g] + w_scale * o_win[:HEAD_DIM, :]
        acc_scr[g] = jnp.zeros((V_ROWS, GQ), F32)

    def chunk_softmax(s, g, c, near, extra):
        parts = []
        for i in range(BLOCKS_PER_CHUNK):
            kb = c * BLOCKS_PER_CHUNK + i
            blk = s[i * SEL_BLOCK:(i + 1) * SEL_BLOCK, :]
            mrow = sel_scr[g, pl.ds(kb, 1), :]
            if extra is not None:
                mrow = mrow + extra
            if near:
                tile = jnp.clip(ci - kb, 0, NEAR_TILES)
                blk = blk + sb_ref[g, tile] + mrow.astype(BF16)
            else:
                blk = blk + (mrow + far_ref[g]).astype(BF16)
            parts.append(blk)
        s = jnp.concatenate(parts, axis=0)
        m_c = jnp.max(s, axis=0, keepdims=True)
        return m_c.astype(F32), jnp.exp2(s - m_c)

    def chunk_pv(pb, g, c):
        return jnp.dot(vst_ref[0, c, vsl(g), :], pb, preferred_element_type=F32)

    def merge_step(units, extras, near, next_units, carry):
        scores = {k: s_scr[k] for k in range(QK_AHEAD)}
        results = []
        for k, (g, c) in enumerate(units):
            m_c, pb = chunk_softmax(scores.pop(k), g, c, near, extras[k])
            if k + QK_AHEAD < len(units):
                scores[k + QK_AHEAD] = chunk_qk(*units[k + QK_AHEAD])
            else:
                j = k + QK_AHEAD - len(units)
                s_scr[j] = chunk_qk(*next_units[j])
            results.append((m_c, chunk_pv(pb, g, c)))
        per_g = len(units) // N_KV
        new = []
        for g in range(N_KV):
            m_run = carry[g]
            stats = results[g * per_g:(g + 1) * per_g]
            m_new = m_run
            for m_c, _ in stats:
                m_new = jnp.maximum(m_new, m_c)
            acc = jnp.exp2(m_run - m_new) * acc_scr[g]
            for m_c, pv in stats:
                acc = acc + jnp.exp2(m_c - m_new) * pv
            acc_scr[g] = acc
            new.append(m_new)
        return tuple(new)

    def wide_step(i, carry):
        units = wide_units(i)
        nxt = pick_units([(i + 1 < n_wide, wide_units(i + 1)), (n_far > 0, far_units(0)),
                          (None, near_units(0))])
        return merge_step(units, [None] * len(units), False, nxt, carry)

    def far_step(i, carry):
        units = far_units(i)
        return merge_step(units, [None] * len(units), False, pick_units([(None, near_units(0))]), carry)

    def near_step(i, carry):
        units = near_units(i)
        extras = [None if j == 0 else jnp.where(c0 + NEAR_CHUNKS * i + j < n_chunks, 0.0, NEG)
                  for _ in range(N_KV) for j in range(NEAR_CHUNKS)]
        return merge_step(units, extras, True, near_units(i + 1), carry)

    init = (jnp.full((1, GQ), NEG, F32),) * N_KV
    carry = lax.fori_loop(0, n_wide, wide_step, init)
    carry = lax.fori_loop(0, n_far, far_step, carry)
    lax.fori_loop(0, n_near, near_step, carry)

    for g in range(N_KV):
        acc = acc_scr[g]
        scale = gt_ref[0, 0, N_KV + g:N_KV + g + 1, :] * (1.0 / acc[HEAD_DIM:HEAD_DIM + 1, :])
        fin_scr[g] = out_scr[g] + scale * acc[:HEAD_DIM, :]


def _attn_call(qt, kc, vct, ks, vst, kw, vwt, gt, ovt, ebank, sbank, wbank, far, eb_shift):
    b, nq = qt.shape[0], qt.shape[1]
    t = ks.shape[1]
    n_cmp_rows = kc.shape[1]
    n_sb = t // SEL_BLOCK
    per_b = lambda shp: pl.BlockSpec((1,) + shp[1:], lambda i, j: (i,) + (0,) * (len(shp) - 1))
    per_q = lambda shp: pl.BlockSpec(
        (1, 1) + shp[2:], lambda i, j: (i, jnp.minimum(j, nq - 1)) + (0,) * (len(shp) - 2))
    full = lambda shp: pl.BlockSpec(shp, lambda i, j: (0,) * len(shp))
    args = (qt, kc, vct, ks, vst, kw, vwt, gt, ovt, ebank, sbank, wbank, far)
    specs = [per_q(qt.shape), per_b(kc.shape), per_b(vct.shape), per_b(ks.shape), per_b(vst.shape),
             per_b(kw.shape), per_b(vwt.shape), per_q(gt.shape), full(ovt.shape),
             full(ebank.shape), full(sbank.shape), full(wbank.shape), full(far.shape)]
    acc_like = pltpu.VMEM((N_KV, HEAD_DIM, GQ), F32)
    return pl.pallas_call(
        functools.partial(_attn_kernel, n_cmp_rows=n_cmp_rows, eb_shift=eb_shift),
        grid=(b, nq + 1),
        in_specs=specs,
        out_specs=pl.BlockSpec((1, Q_BLOCK, ATTN_W), lambda i, j: (i, jnp.maximum(j - 1, 0), 0)),
        out_shape=jax.ShapeDtypeStruct((b, t, ATTN_W), BF16),
        scratch_shapes=[pltpu.VMEM((N_KV, n_sb, GQ), F32),
                        pltpu.VMEM((N_KV, V_ROWS, GQ), F32),
                        acc_like,
                        pltpu.VMEM((N_KV, KV_W, GQ), BF16),
                        pltpu.VMEM((QK_AHEAD, KEY_CHUNK, GQ), BF16),
                        acc_like],
        compiler_params=_cparams(2),
        name="nsa_attention",
    )(*args)


def _ffn_kernel(x_ref, a_ref, c_ref, mod_ref, n2_ref, wo_ref, w1_ref, w2_ref, o_ref):
    aw = a_ref.shape[2]
    mix = jnp.dot(a_ref[0], wo_ref[0:aw, :], preferred_element_type=F32)
    mix = mix + jnp.dot(c_ref[0], wo_ref[aw:, :], preferred_element_type=F32)
    x1 = x_ref[0] + mod_ref[0, 2:3, :] * mix
    ms = jnp.mean(x1 * x1, axis=-1, keepdims=True)
    y = x1 * lax.rsqrt(ms + EPS) * n2_ref[...]
    h2 = (y * (1.0 + mod_ref[0, 4:5, :]) + mod_ref[0, 3:4, :]).astype(BF16)
    d_ff = w1_ref.shape[1]
    ff = jnp.zeros(x1.shape, F32)
    for j in range(d_ff // FF_CHUNK):
        a = jnp.dot(h2, w1_ref[:, j * FF_CHUNK:(j + 1) * FF_CHUNK], preferred_element_type=F32)
        a = jnp.maximum(a, 0.0)
        ff = ff + jnp.dot((a * a).astype(BF16), w2_ref[j * FF_CHUNK:(j + 1) * FF_CHUNK, :],
                          preferred_element_type=F32)
    o_ref[0] = x1 + mod_ref[0, 5:6, :] * ff


def _ffn_call(x, attn, conv, mod, norm2, w_out, w_ff1, w_ff2):
    b, t, d = x.shape
    tm = min(ROW_TILE, t)
    row_spec = lambda w: pl.BlockSpec((1, tm, w), lambda i, j: (i, j, 0))
    full = lambda shp: pl.BlockSpec(shp, lambda i, j: (0,) * len(shp),
                                    pipeline_mode=pl.Buffered(1))
    return pl.pallas_call(
        _ffn_kernel,
        grid=(b, t // tm),
        in_specs=[row_spec(d), row_spec(attn.shape[2]), row_spec(conv.shape[2]),
                  pl.BlockSpec((1, N_MOD, d), lambda i, j: (i, 0, 0)),
                  full((1, d)), full(w_out.shape), full(w_ff1.shape), full(w_ff2.shape)],
        out_specs=row_spec(d),
        out_shape=jax.ShapeDtypeStruct((b, t, d), F32),
        compiler_params=_cparams(2),
        name="outproj_mlp",
    )(x, attn, conv, mod, norm2, w_out, w_ff1, w_ff2)


def _block_diag_ones(n):
    idx = np.arange(n) // HEAD_DIM
    return jnp.asarray(idx[:, None] == idx[None, :], dtype=BF16)


def _pack_w_in(w_in):
    d = w_in.shape[0]
    conv_w = d - ATTN_W
    sizes = [ATTN_W] + [KV_W] * 6 + [N_BRANCH * N_HEADS] + [conv_w] * 3
    offs = np.concatenate([[0], np.cumsum(sizes)])
    part = lambda i: w_in[:, offs[i]:offs[i + 1]]
    q, kc, vc, ks, vs, kw, vw, g, cgate, bgate, u = (part(i) for i in range(11))
    w_nat = jnp.concatenate([kc, vc, ks, kw, cgate, bgate, u], axis=1).astype(BF16)
    gt = g.reshape(d, N_KV, GQA, N_BRANCH).transpose(2, 3, 1, 0).reshape(GQA, N_BRANCH * N_KV, d)
    gt = jnp.pad(gt, ((0, 0), (0, SUBLANES - N_BRANCH * N_KV), (0, 0))).reshape(_G_ROWS, d)
    w_tr = jnp.concatenate([q.T, vs.T, vw.T, gt], axis=0).astype(BF16)
    return w_nat, w_tr


def _expand_w1(w1):
    hid = w1.shape[1]
    w = w1.reshape(2, CMP_STRIDE, 1, HEAD_DIM, 1, hid)
    eye = jnp.eye(N_KV, dtype=w1.dtype).reshape(1, 1, N_KV, 1, N_KV, 1)
    return (w * eye).reshape(2, CMP_STRIDE * KV_W, N_KV * hid).astype(BF16)


def _expand_w2(w2):
    hid = w2.shape[0]
    eye = jnp.eye(N_KV, dtype=w2.dtype).reshape(N_KV, 1, N_KV, 1)
    return (w2.reshape(1, hid, 1, HEAD_DIM) * eye).reshape(N_KV * hid, KV_W).astype(BF16)


def _expand_pe(pe):
    p = pe.reshape(2, CMP_STRIDE, 1, HEAD_DIM)
    return jnp.broadcast_to(p, (2, CMP_STRIDE, N_KV, HEAD_DIM)).reshape(2, CMP_STRIDE * KV_W)


def _bucket_thresholds():
    n = np.arange(2 * REL_MAX_DIST)
    max_exact = REL_BUCKETS // 2
    nf = np.maximum(n, max_exact).astype(np.float32)
    ratio = np.log(nf / np.float32(max_exact)) / np.float32(math.log(REL_MAX_DIST / max_exact))
    large = max_exact + (ratio * np.float32(REL_BUCKETS - max_exact)).astype(np.int32)
    table = np.where(n < max_exact, n, np.minimum(large, REL_BUCKETS - 1))
    return tuple(int(np.searchsorted(table, k, side="left")) for k in range(REL_BUCKETS))


def _bank_call(bias_rows, lead, n_tiles, dist_fn, name):
    nl = len(lead)
    thr = _bucket_thresholds()

    def body(rows_ref, o_ref):
        lead_ids = [pl.program_id(a) for a in range(nl)]
        row = lax.broadcasted_iota(jnp.int32, (SEL_BLOCK, GQ), 0)
        qi = lax.broadcasted_iota(jnp.int32, (SEL_BLOCK, GQ), 1) & (Q_BLOCK - 1)

        def tile(t, carry):
            dist, ok = dist_fn(lead_ids, t, row, qi)
            v = jnp.broadcast_to(rows_ref[0, 0:1, :], (SEL_BLOCK, GQ))
            for k in range(1, REL_BUCKETS):
                v = jnp.where(dist >= thr[k], rows_ref[0, k:k + 1, :], v)
            o_ref[(0,) * (nl + 1) + (t,)] = jnp.where(ok, v, NEG)
            return carry

        lax.fori_loop(0, n_tiles, tile, 0)

    return pl.pallas_call(
        body,
        grid=tuple(lead) + (N_KV,),
        in_specs=[pl.BlockSpec((1, REL_BUCKETS, GQ), lambda *i: (i[nl], 0, 0))],
        out_specs=pl.BlockSpec((1,) * (nl + 1) + (n_tiles, SEL_BLOCK, GQ),
                               lambda *i: tuple(i) + (0, 0, 0)),
        out_shape=jax.ShapeDtypeStruct(tuple(lead) + (N_KV, n_tiles, SEL_BLOCK, GQ), F32),
        compiler_params=_cparams(nl + 1),
        name=name,
    )(bias_rows)


def _bias_banks(rel_bias, t):
    n_cmp_rows = t // CMP_STRIDE
    rows = rel_bias.reshape(REL_BUCKETS, N_KV, GQA).transpose(1, 0, 2)
    rows = jnp.repeat(rows, Q_BLOCK, axis=2) * LOG2E

    def sel_dist(lead, tile, row, qi):
        dist = SEL_BLOCK * tile + qi - row
        return dist, dist >= 0

    sbank = _bank_call(rows, (), NEAR_TILES + 1, sel_dist, "bias_bank_sel")
    far = sbank[:, NEAR_TILES, 0:1, :]

    def win_dist(lead, tile, row, qi):
        dist = SEL_BLOCK * tile + qi - row
        return dist, (dist >= 0) & (dist < WINDOW)

    wbank = _bank_call(rows, (), WIN_BLOCKS + 2, win_dist, "bias_bank_win")

    cmp_c = n_cmp_rows - 4
    shifts = tuple(int((-(cmp_c - 4 * p)) % 8) for p in range(2))
    n_tiles = (cmp_c + n_cmp_rows + 8 + SEL_BLOCK - 1) // SEL_BLOCK

    def cmp_dist(lead, tile, row, qi):
        e = SEL_BLOCK * tile + row - jnp.where(lead[0] == 0, shifts[0], shifts[1])
        dist = qi - CMP_STRIDE * e + (CMP_STRIDE * cmp_c - (CMP_BLOCK - 1))
        return dist, (dist >= 0) & (e >= 0)

    ebank = _bank_call(rows, (2,), n_tiles, cmp_dist, "bias_bank_cmp")
    ebank = ebank.reshape(2, N_KV, n_tiles * SEL_BLOCK, GQ)
    return ebank, sbank, wbank, far, shifts


def _overlap_t(t):
    n_cmp_rows = t // CMP_STRIDE
    n_sb = t // SEL_BLOCK
    c_start = np.arange(n_cmp_rows)[None, :] * CMP_STRIDE
    s_start = np.arange(n_sb)[:, None] * SEL_BLOCK
    ov = np.clip(np.minimum(c_start + CMP_BLOCK, s_start + SEL_BLOCK)
                 - np.maximum(c_start, s_start), 0, None) / CMP_BLOCK
    ov[:, n_cmp_rows - 1] = 0.0
    return jnp.asarray(ov, dtype=BF16)


def _layer(x, c_pad, w_in, q_norm, k_norm, cmp_pe_k, cmp_w1_k, cmp_w2_k, cmp_pe_v, cmp_w1_v,
           cmp_w2_v, rel_bias, conv_w, w_out, norm1, norm2, w_ada, b_ada, w_ff1, w_ff2):
    b, t, d = x.shape
    scale = HEAD_DIM ** -0.5

    mod = _mod_call(c_pad, w_ada, b_ada)[:b].reshape(b, N_MOD, d)

    qn_col = (jnp.tile(q_norm, N_HEADS) * (scale * LOG2E)).reshape(ATTN_W, 1)
    kn_t = jnp.tile(k_norm, N_KV).reshape(1, KV_W)
    bdq = _block_diag_ones(ATTN_W)
    bdk = _block_diag_ones(KV_W)
    w_nat, w_tr = _pack_w_in(w_in)
    qt, kc_raw, vc_raw, ks, vst, kw, vwt, gt, conv = _inproj_call(
        x, mod, norm1.reshape(1, d), w_nat, w_tr, qn_col, kn_t, conv_w, bdq, bdk)

    n_cmp_rows = t // CMP_STRIDE
    kc, vct = _compress_call(
        kc_raw.reshape(b, n_cmp_rows, CMP_STRIDE * KV_W), vc_raw.reshape(b, n_cmp_rows, CMP_STRIDE * KV_W),
        _expand_pe(cmp_pe_k), _expand_pe(cmp_pe_v), _expand_w1(cmp_w1_k), _expand_w1(cmp_w1_v),
        _expand_w2(cmp_w2_k), _expand_w2(cmp_w2_v).T, kn_t, bdk)

    ebank, sbank, wbank, far, eb_shift = _bias_banks(rel_bias, t)
    attn = _attn_call(qt, kc, vct, ks, vst, kw, vwt, gt, _overlap_t(t), ebank, sbank.astype(BF16),
                      wbank.astype(BF16), far, eb_shift)

    return _ffn_call(x, attn, conv, mod, norm2.reshape(1, d), w_out.astype(BF16),
                     w_ff1.astype(BF16), w_ff2.astype(BF16))


def kernel(x, c, w_in, q_norm, k_norm, cmp_pe_k, cmp_w1_k, cmp_w2_k, cmp_pe_v, cmp_w1_v, cmp_w2_v,
           rel_bias, conv_w, w_out, norm1, norm2, w_ada, b_ada, w_ff1, w_ff2):
    b = x.shape[0]
    c_pad = jnp.pad(c, ((0, (-b) % 8), (0, 0)))
    for l in range(w_in.shape[0]):
        x = _layer(x, c_pad, w_in[l], q_norm[l], k_norm[l], cmp_pe_k[l], cmp_w1_k[l], cmp_w2_k[l],
                   cmp_pe_v[l], cmp_w1_v[l], cmp_w2_v[l], rel_bias, conv_w[l], w_out[l],
                   norm1[l], norm2[l], w_ada[l], b_ada[l], w_ff1[l], w_ff2[l])
    return x
```

```python
import functools
import math

import numpy as np
import jax
import jax.numpy as jnp
from jax import lax
from jax.experimental import pallas as pl
from jax.experimental.pallas import tpu as pltpu

HEAD_DIM = 64
N_HEADS = 8
N_KV = 2
GQA = N_HEADS // N_KV
ATTN_W = N_HEADS * HEAD_DIM
KV_W = N_KV * HEAD_DIM
CONV_K = 3
CMP_BLOCK = 32
CMP_STRIDE = 16
CMP_HIDDEN = 256
SEL_BLOCK = 64
N_SELECT = 16
WINDOW = 512
Q_BLOCK = 64
REL_BUCKETS = 32
REL_MAX_DIST = 1024
N_MOD = 6
N_BRANCH = 3
EPS = 1e-6
NEG = -1e30

LANES = 128
SUBLANES = 8
GQ = GQA * Q_BLOCK
KEY_CHUNK = 256
BLOCKS_PER_CHUNK = KEY_CHUNK // SEL_BLOCK
V_TILE = 128
BF16_ROWS = 16
V_ROWS = HEAD_DIM + BF16_ROWS
V_TILE_ROWS = N_KV * V_ROWS
WIN_BLOCKS = WINDOW // SEL_BLOCK
WIN_TILES = WINDOW // V_TILE + 1
NEAR_TILES = (REL_MAX_DIST + Q_BLOCK - 1) // SEL_BLOCK + 1
ROW_TILE = 512
INPROJ_SPLIT = 2
FAR_CHUNKS = 4
NEAR_CHUNKS = 4
QK_AHEAD = 4
LOG2E = math.log2(math.e)
FF_CHUNK = 1024
VMEM_LIMIT = 56 * 1024 * 1024

F32 = jnp.float32
BF16 = jnp.bfloat16
_NT = (((1,), (1,)), ((), ()))


def _cparams(n_axes):
    return pltpu.CompilerParams(dimension_semantics=("arbitrary",) * n_axes,
                                vmem_limit_bytes=VMEM_LIMIT)


def _with_ones_rows(vt):
    ones = jnp.ones((BF16_ROWS, vt.shape[1]), vt.dtype)
    parts = []
    for g in range(N_KV):
        parts += [vt[g * HEAD_DIM:(g + 1) * HEAD_DIM, :], ones]
    return jnp.concatenate(parts, axis=0)


def _swap_halves(p0, p1):
    low = lax.broadcasted_iota(jnp.int32, p0.shape, 1) < LANES // 2
    return (jnp.where(low, p0, pltpu.roll(p1, LANES // 2, 1)),
            jnp.where(low, pltpu.roll(p0, LANES // 2, 1), p1))


def _mod_kernel(c_ref, w_ref, b_ref, o_ref):
    c = c_ref[...]
    a = c * jax.nn.sigmoid(c)
    o_ref[...] = jnp.dot(a, w_ref[...], preferred_element_type=F32,
                         precision=lax.Precision.HIGHEST) + b_ref[...]


def _mod_call(c_pad, w_ada, b_ada):
    rows, d = c_pad.shape
    n = w_ada.shape[1]
    tn = 1024
    return pl.pallas_call(
        _mod_kernel,
        grid=(n // tn,),
        in_specs=[pl.BlockSpec((rows, d), lambda j: (0, 0)),
                  pl.BlockSpec((d, tn), lambda j: (0, j)),
                  pl.BlockSpec((1, tn), lambda j: (0, j))],
        out_specs=pl.BlockSpec((rows, tn), lambda j: (0, j)),
        out_shape=jax.ShapeDtypeStruct((rows, n), F32),
        compiler_params=_cparams(1),
        name="adaln_mod",
    )(c_pad, w_ada, b_ada.reshape(1, n))


_N_KC, _N_VC, _N_KS, _N_KW, _N_CONV = 0, KV_W, 2 * KV_W, 3 * KV_W, 4 * KV_W
_T_Q, _T_VS, _T_VW, _T_G = 0, ATTN_W, ATTN_W + KV_W, ATTN_W + 2 * KV_W
_G_ROWS = GQA * SUBLANES


def _inproj_kernel(x_ref, mod_ref, n1_ref, wn_ref, wt_ref, qn_ref, kn_ref, cw_ref, bdq_ref, bdk_ref,
                   qt_out, kc_out, vc_out, ks_out, vst_out, kw_out, vwt_out, gt_out, conv_out,
                   carry, *, conv_w):
    t = pl.program_id(1)
    tm = x_ref.shape[1]
    sub = tm // INPROJ_SPLIT

    @pl.when(t == 0)
    def _():
        carry[...] = jnp.zeros_like(carry)

    prev2, prev1 = carry[6:7, :], carry[7:8, :]
    c_bg = _N_CONV + conv_w
    c_u = c_bg + conv_w

    for part in range(INPROJ_SPLIT):
        rows = slice(part * sub, (part + 1) * sub)
        x = x_ref[0, rows, :]
        ms = jnp.mean(x * x, axis=-1, keepdims=True)
        y = x * lax.rsqrt(ms + EPS) * n1_ref[...]
        h = (y * (1.0 + mod_ref[0, 1:2, :]) + mod_ref[0, 0:1, :]).astype(BF16)

        def proj(a, b):
            return jnp.dot(h, wn_ref[:, a:b], preferred_element_type=F32)

        def proj_t(a, b):
            return lax.dot_general(wt_ref[a:b, :], h, _NT, preferred_element_type=F32)

        def head_norm(v, gain):
            ssq = jnp.dot((v * v).astype(BF16), bdk_ref[...], preferred_element_type=F32)
            return v * lax.rsqrt(ssq * (1.0 / HEAD_DIM) + EPS) * gain

        qf = proj_t(_T_Q, _T_VS)
        vg = proj_t(_T_VS, _T_G + _G_ROWS)
        kv = proj(_N_KC, _N_CONV)
        cv = proj(_N_CONV, c_u + conv_w)

        ssq = jnp.dot(bdq_ref[...], (qf * qf).astype(BF16), preferred_element_type=F32)
        qf = qf * lax.rsqrt(ssq * (1.0 / HEAD_DIM) + EPS) * qn_ref[...]
        gf = jax.nn.sigmoid(vg[2 * KV_W:, :])
        for c in range(sub // LANES):
            blk = (part * sub) // Q_BLOCK + 2 * c
            cols = slice(c * LANES, (c + 1) * LANES)
            for g in range(N_KV):
                pc = [qf[(g * GQA + r) * HEAD_DIM:(g * GQA + r + 1) * HEAD_DIM, cols]
                      for r in range(GQA)]
                lo01, hi01 = _swap_halves(pc[0], pc[1])
                lo23, hi23 = _swap_halves(pc[2], pc[3])
                qt_out[0, blk, g] = jnp.concatenate([lo01, lo23], axis=1).astype(BF16)
                qt_out[0, blk + 1, g] = jnp.concatenate([hi01, hi23], axis=1).astype(BF16)
            pc = [gf[r * SUBLANES:(r + 1) * SUBLANES, cols] for r in range(GQA)]
            lo01, hi01 = _swap_halves(pc[0], pc[1])
            lo23, hi23 = _swap_halves(pc[2], pc[3])
            gt_out[0, blk] = jnp.concatenate([lo01, lo23], axis=1)
            gt_out[0, blk + 1] = jnp.concatenate([hi01, hi23], axis=1)

        vs_f = _with_ones_rows(vg[:KV_W, :].astype(BF16))
        vw_f = _with_ones_rows(vg[KV_W:2 * KV_W, :].astype(BF16))
        for j in range(sub // KEY_CHUNK):
            vst_out[0, (part * sub) // KEY_CHUNK + j] = vs_f[:, j * KEY_CHUNK:(j + 1) * KEY_CHUNK]
        for j in range(sub // V_TILE):
            vwt_out[0, (part * sub) // V_TILE + j] = vw_f[:, j * V_TILE:(j + 1) * V_TILE]

        kc_out[0, rows, :] = kv[:, _N_KC:_N_VC]
        vc_out[0, rows, :] = kv[:, _N_VC:_N_KS]
        ks_out[0, rows, :] = head_norm(kv[:, _N_KS:_N_KW], kn_ref[...]).astype(BF16)
        kw_out[0, rows, :] = head_norm(kv[:, _N_KW:_N_CONV], kn_ref[...]).astype(BF16)

        z = cv[:, :conv_w] * cv[:, 2 * conv_w:]
        row = lax.broadcasted_iota(jnp.int32, z.shape, 0)
        z1 = jnp.where(row == 0, prev1, pltpu.roll(z, 1, 0))
        z2 = jnp.where(row == 0, prev2, jnp.where(row == 1, prev1, pltpu.roll(z, 2, 0)))
        zc = cw_ref[0:1, :] * z2 + cw_ref[1:2, :] * z1 + cw_ref[2:3, :] * z
        conv_out[0, rows, :] = (cv[:, conv_w:2 * conv_w] * zc).astype(BF16)
        prev2, prev1 = z[sub - 2:sub - 1, :], z[sub - 1:sub, :]
        if part == INPROJ_SPLIT - 1:
            carry[...] = z[sub - SUBLANES:sub, :]


def _inproj_call(x, mod, norm1, w_nat, w_tr, qn_col, kn_t, conv_w, bdq, bdk):
    b, t, d = x.shape
    tm = min(ROW_TILE, t)
    cw = conv_w.shape[1]
    nq = t // Q_BLOCK
    row_spec = lambda w: pl.BlockSpec((1, tm, w), lambda i, j: (i, j, 0))
    full = lambda shp: pl.BlockSpec(shp, lambda i, j: (0,) * len(shp))
    vt_spec = lambda w: pl.BlockSpec((1, tm // w, V_TILE_ROWS, w), lambda i, j: (i, j, 0, 0))
    vt_shape = lambda w: jax.ShapeDtypeStruct((b, t // w, V_TILE_ROWS, w), BF16)
    kv = lambda dt: jax.ShapeDtypeStruct((b, t, KV_W), dt)
    out_specs = [pl.BlockSpec((1, tm // Q_BLOCK, N_KV, HEAD_DIM, GQ), lambda i, j: (i, j, 0, 0, 0)),
                 row_spec(KV_W), row_spec(KV_W), row_spec(KV_W), vt_spec(KEY_CHUNK), row_spec(KV_W),
                 vt_spec(V_TILE),
                 pl.BlockSpec((1, tm // Q_BLOCK, SUBLANES, GQ), lambda i, j: (i, j, 0, 0)),
                 row_spec(cw)]
    out_shape = [jax.ShapeDtypeStruct((b, nq, N_KV, HEAD_DIM, GQ), BF16),
                 kv(F32), kv(F32), kv(BF16), vt_shape(KEY_CHUNK), kv(BF16), vt_shape(V_TILE),
                 jax.ShapeDtypeStruct((b, nq, SUBLANES, GQ), F32),
                 jax.ShapeDtypeStruct((b, t, cw), BF16)]
    return pl.pallas_call(
        functools.partial(_inproj_kernel, conv_w=cw),
        grid=(b, t // tm),
        in_specs=[row_spec(d),
                  pl.BlockSpec((1, N_MOD, d), lambda i, j: (i, 0, 0)),
                  full((1, d)), full(w_nat.shape), full(w_tr.shape), full((ATTN_W, 1)),
                  full((1, KV_W)), full((CONV_K, cw)), full((ATTN_W, ATTN_W)), full((KV_W, KV_W))],
        out_specs=out_specs,
        out_shape=out_shape,
        scratch_shapes=[pltpu.VMEM((SUBLANES, cw), F32)],
        compiler_params=_cparams(2),
        name="inproj",
    )(x, mod, norm1, w_nat, w_tr, qn_col, kn_t, conv_w, bdq, bdk)


def _compress_kernel(kx_ref, vx_ref, pek_ref, pev_ref, w1k_ref, w1v_ref, w2k_ref, w2vt_ref,
                     kn_ref, bdk_ref, kc_out, vct_out):
    def hidden(x_ref, pe_ref, w1_ref):
        n = x_ref.shape[1] // CMP_STRIDE
        u = jnp.zeros((n, w1_ref.shape[2]), F32)
        v = jnp.zeros((n, w1_ref.shape[2]), F32)
        for r in range(0, CMP_STRIDE, 2):
            tok = [x_ref[0, pl.ds(r + d, n, stride=CMP_STRIDE), :] for d in range(2)]
            cols = slice(r * KV_W, (r + 2) * KV_W)
            for a, acc in ((0, "u"), (1, "v")):
                lhs = jnp.concatenate([tok[d] + pe_ref[a:a + 1, (r + d) * KV_W:(r + d + 1) * KV_W]
                                       for d in range(2)], axis=1).astype(BF16)
                prod = jnp.dot(lhs, w1_ref[a, cols, :], preferred_element_type=F32)
                if acc == "u":
                    u = u + prod
                else:
                    v = v + prod
        hid = u + pltpu.roll(v, n - 1, 0)
        return jax.nn.gelu(hid, approximate=True).astype(BF16)

    kc = jnp.dot(hidden(kx_ref, pek_ref, w1k_ref), w2k_ref[...], preferred_element_type=F32)
    ssq = jnp.dot((kc * kc).astype(BF16), bdk_ref[...], preferred_element_type=F32)
    kc_out[0] = (kc * lax.rsqrt(ssq * (1.0 / HEAD_DIM) + EPS) * kn_ref[...]).astype(BF16)
    vct_out[0] = _with_ones_rows(lax.dot_general(w2vt_ref[...], hidden(vx_ref, pev_ref, w1v_ref), _NT,
                                                 preferred_element_type=F32).astype(BF16))


def _compress_call(kx, vx, pek, pev, w1k, w1v, w2k, w2vt, kn_t, bdk):
    b, t, _ = kx.shape
    nrow = t // CMP_STRIDE
    wide, hid2 = w1k.shape[1], w1k.shape[2]
    full = lambda shp: pl.BlockSpec(shp, lambda i: (0,) * len(shp))
    xs = pl.BlockSpec((1, t, KV_W), lambda i: (i, 0, 0))
    return pl.pallas_call(
        _compress_kernel,
        grid=(b,),
        in_specs=[xs, xs, full((2, wide)), full((2, wide)), full((2, wide, hid2)),
                  full((2, wide, hid2)), full((hid2, KV_W)), full((KV_W, hid2)),
                  full((1, KV_W)), full((KV_W, KV_W))],
        out_specs=[pl.BlockSpec((1, nrow, KV_W), lambda i: (i, 0, 0)),
                   pl.BlockSpec((1, V_TILE_ROWS, nrow), lambda i: (i, 0, 0))],
        out_shape=[jax.ShapeDtypeStruct((b, nrow, KV_W), BF16),
                   jax.ShapeDtypeStruct((b, V_TILE_ROWS, nrow), BF16)],
        compiler_params=_cparams(1),
        name="compress",
    )(kx, vx, pek, pev, w1k, w1v, w2k, w2vt, kn_t, bdk)


def _attn_kernel(q_ref, kc_ref, vct_ref, ks_ref, vst_ref, kw_ref, vwt_ref, gt_ref, ovt_ref,
                 eb_ref, sb_ref, wb_ref, far_ref, o_ref, sel_scr, acc_scr, out_scr, qp_scr, s_scr,
                 fin_scr, *, n_cmp_rows, eb_shift):
    n_sb = sel_scr.shape[1]
    ci = jnp.minimum(pl.program_id(1), n_sb - 1)
    par = lax.rem(ci, 2)
    n_total = ks_ref.shape[1] // KEY_CHUNK
    vsl = lambda g: slice(g * V_ROWS, (g + 1) * V_ROWS)

    @pl.when((pl.program_id(0) == 0) & (pl.program_id(1) == 0))
    def _():
        def zero(g, carry):
            fin_scr[g] = jnp.zeros(fin_scr.shape[1:], F32)
            return carry
        lax.fori_loop(0, N_KV, zero, 0)

    pieces = []
    for g in range(N_KV):
        for half in range(GQ // LANES):
            a = fin_scr[g, :, half * LANES:(half + 1) * LANES]
            stacked = jnp.concatenate([a, pltpu.roll(a, Q_BLOCK, 1)], axis=0)
            pieces.append(stacked.T[:Q_BLOCK, :])
    o_ref[0] = jnp.concatenate(pieces, axis=1).astype(BF16)

    zeros_q = jnp.zeros((HEAD_DIM, GQ), BF16)
    qp_scr[0] = jnp.concatenate([q_ref[0, 0, 0], zeros_q], axis=0)
    qp_scr[1] = jnp.concatenate([zeros_q, q_ref[0, 0, 1]], axis=0)

    def chunk_qk(g, c):
        kk = ks_ref[0, pl.ds(pl.multiple_of(c * KEY_CHUNK, KEY_CHUNK), KEY_CHUNK), :]
        return jnp.dot(kk, qp_scr[g], preferred_element_type=F32).astype(BF16)

    n_chunks = ci // BLOCKS_PER_CHUNK + 1
    far_groups = jnp.maximum(ci - (NEAR_TILES - 1), 0) // (BLOCKS_PER_CHUNK * FAR_CHUNKS)
    n_wide = far_groups // 2
    n_far = far_groups - 2 * n_wide
    c_far = n_wide * (2 * FAR_CHUNKS)
    c0 = far_groups * FAR_CHUNKS
    n_near = (n_chunks - c0 + NEAR_CHUNKS - 1) // NEAR_CHUNKS

    def near_units(i):
        return [(g, jnp.minimum(c0 + NEAR_CHUNKS * i + j, n_total - 1))
                for g in range(N_KV) for j in range(NEAR_CHUNKS)]

    def far_units(i):
        return [(g, c_far + i * FAR_CHUNKS + j) for g in range(N_KV) for j in range(FAR_CHUNKS)]

    def wide_units(i):
        return [(g, i * 2 * FAR_CHUNKS + j) for g in range(N_KV) for j in range(2 * FAR_CHUNKS)]

    def pick_units(options):
        units = options[-1][1][:QK_AHEAD]
        for cond, cand in reversed(options[:-1]):
            units = [(jnp.where(cond, g_a, g_b), jnp.where(cond, c_a, c_b))
                     for (g_a, c_a), (g_b, c_b) in zip(cand[:QK_AHEAD], units)]
        return [(g, jnp.minimum(c, n_total - 1)) for g, c in units]

    def prefetch_scores(units):
        for k in range(QK_AHEAD):
            s_scr[k] = chunk_qk(*units[k])

    lane = lax.broadcasted_iota(jnp.int32, (n_sb, LANES), 1)
    cmp_c = n_cmp_rows - 4
    e0 = cmp_c - 4 * ci + jnp.where(par == 0, eb_shift[0], eb_shift[1])
    e0 = pl.multiple_of(e0, 8)
    o_cmp = []
    imp = []
    w0 = ci // 2 - (WIN_TILES - 1)
    win_tiles = [jnp.maximum(w0 + j, 0) for j in range(WIN_TILES)]
    s_cmp = [jnp.dot(kc_ref[0], qp_scr[g], preferred_element_type=F32) for g in range(N_KV)]
    for g in range(N_KV):
        sc = s_cmp[g] + eb_ref[par, g, pl.ds(e0, n_cmp_rows), :]
        m = jnp.max(sc, axis=0, keepdims=True)
        e = jnp.exp2(sc - m).astype(BF16)
        ov = jnp.dot(vct_ref[0, vsl(g), :], e, preferred_element_type=F32)
        inv = jnp.where(m > 0.5 * NEG, 1.0 / ov[HEAD_DIM:HEAD_DIM + 1, :], 0.0)
        o_cmp.append(ov[:HEAD_DIM, :] * inv)
        ir = jnp.dot(ovt_ref[...], e, preferred_element_type=F32) * inv
        a = ir[:, :LANES] + ir[:, LANES:]
        imp.append(a + pltpu.roll(a, Q_BLOCK, 1))

    s_win = [[jnp.dot(kw_ref[0, pl.ds(pl.multiple_of(tj * V_TILE, V_TILE), V_TILE), :], qp_scr[g],
                      preferred_element_type=F32).astype(BF16) for tj in win_tiles]
             for g in range(N_KV)]

    prefetch_scores(pick_units([(n_wide > 0, wide_units(0)), (n_far > 0, far_units(0)),
                                (None, near_units(0))]))

    jidx = lax.broadcasted_iota(jnp.int32, (n_sb, LANES), 0)
    jf = jidx.astype(F32)
    valid = jidx <= ci
    forced = (jidx == 0) | (jidx == ci) | (jidx == ci - 1)
    score = jnp.where(valid, jnp.where(forced, -2.0, jnp.where(lane < Q_BLOCK, imp[0], imp[1])),
                      -1.0)
    sel = jnp.where(forced, 1.0, 0.0)

    def first_max(score):
        pairs = [(score[r:r + SUBLANES], jf[r:r + SUBLANES]) for r in range(0, n_sb, SUBLANES)]
        while len(pairs) > 1:
            nxt = []
            for (va, ia), (vb, ib) in zip(pairs[0::2], pairs[1::2]):
                keep = va >= vb
                nxt.append((jnp.where(keep, va, vb), jnp.where(keep, ia, ib)))
            pairs = nxt + pairs[len(pairs) - len(pairs) % 2:]
        v8, i8 = pairs[0]
        mx = jnp.max(v8, axis=0, keepdims=True)
        return jnp.min(jnp.where(v8 == mx, i8, float(n_sb)), axis=0, keepdims=True)

    for _ in range(min(N_SELECT, n_sb) - 3):
        first = first_max(score)
        hit = jf == first
        sel = jnp.where(hit, 1.0, sel)
        score = jnp.where(hit, -2.0, score)
    selneg = jnp.where((sel > 0.5) & valid, 0.0, NEG)
    swapped = pltpu.roll(selneg, Q_BLOCK, 1)
    left = jnp.where(lane < Q_BLOCK, selneg, swapped)
    right = jnp.where(lane < Q_BLOCK, swapped, selneg)
    sel_scr[0] = jnp.concatenate([left, left], axis=1)
    sel_scr[1] = jnp.concatenate([right, right], axis=1)

    for g in range(N_KV):
        parts = []
        for i in range(2 * WIN_TILES):
            delta = par + WIN_BLOCKS - i
            ok = (delta >= 0) & (delta <= WIN_BLOCKS) & (delta <= ci)
            tile = jnp.where(ok, delta, WIN_BLOCKS + 1)
            half = s_win[g][i // 2][(i % 2) * SEL_BLOCK:(i % 2 + 1) * SEL_BLOCK, :]
            parts.append(half + wb_ref[g, tile])
        s = jnp.concatenate(parts, axis=0)
        m = jnp.max(s, axis=0, keepdims=True)
        pb = jnp.exp2(s - m)
        o_win = jnp.zeros((V_ROWS, GQ), F32)
        for j, tj in enumerate(win_tiles):
            o_win = o_win + jnp.dot(vwt_ref[0, tj, vsl(g), :], pb[j * V_TILE:(j + 1) * V_TILE, :],
                                    preferred_element_type=F32)
        w_scale = gt_ref[0, 0, 2 * N_KV + g:2 * N_KV + g + 1, :] * (1.0 / o_win[HEAD_DIM:HEAD_DIM + 1, :])
        out_scr[g] = gt_ref[0, 0, g:g + 1, :] * o_cmp[g] + w_scale * o_win[:HEAD_DIM, :]
        acc_scr[g] = jnp.zeros((V_ROWS, GQ), F32)

    def chunk_softmax(s, g, c, near, extra):
        parts = []
        for i in range(BLOCKS_PER_CHUNK):
            kb = c * BLOCKS_PER_CHUNK + i
            blk = s[i * SEL_BLOCK:(i + 1) * SEL_BLOCK, :]
            mrow = sel_scr[g, pl.ds(kb, 1), :]
            if extra is not None:
                mrow = mrow + extra
            if near:
                tile = jnp.clip(ci - kb, 0, NEAR_TILES)
                blk = blk + sb_ref[g, tile] + mrow.astype(BF16)
            else:
                blk = blk + (mrow + far_ref[g]).astype(BF16)
            parts.append(blk)
        s = jnp.concatenate(parts, axis=0)
        m_c = jnp.max(s, axis=0, keepdims=True)
        return m_c.astype(F32), jnp.exp2(s - m_c)

    def chunk_pv(pb, g, c):
        return jnp.dot(vst_ref[0, c, vsl(g), :], pb, preferred_element_type=F32)

    def merge_step(units, extras, near, next_units, carry):
        scores = {k: s_scr[k] for k in range(QK_AHEAD)}
        results = []
        for k, (g, c) in enumerate(units):
            m_c, pb = chunk_softmax(scores.pop(k), g, c, near, extras[k])
            if k + QK_AHEAD < len(units):
                scores[k + QK_AHEAD] = chunk_qk(*units[k + QK_AHEAD])
            else:
                j = k + QK_AHEAD - len(units)
                s_scr[j] = chunk_qk(*next_units[j])
            results.append((m_c, chunk_pv(pb, g, c)))
        per_g = len(units) // N_KV
        new = []
        for g in range(N_KV):
            m_run = carry[g]
            stats = results[g * per_g:(g + 1) * per_g]
            m_new = m_run
            for m_c, _ in stats:
                m_new = jnp.maximum(m_new, m_c)
            acc = jnp.exp2(m_run - m_new) * acc_scr[g]
            for m_c, pv in stats:
                acc = acc + jnp.exp2(m_c - m_new) * pv
            acc_scr[g] = acc
            new.append(m_new)
        return tuple(new)

    def wide_step(i, carry):
        units = wide_units(i)
        nxt = pick_units([(i + 1 < n_wide, wide_units(i + 1)), (n_far > 0, far_units(0)),
                          (None, near_units(0))])
        return merge_step(units, [None] * len(units), False, nxt, carry)

    def far_step(i, carry):
        units = far_units(i)
        return merge_step(units, [None] * len(units), False, pick_units([(None, near_units(0))]), carry)

    def near_step(i, carry):
        units = near_units(i)
        extras = [None if j == 0 else jnp.where(c0 + NEAR_CHUNKS * i + j < n_chunks, 0.0, NEG)
                  for _ in range(N_KV) for j in range(NEAR_CHUNKS)]
        return merge_step(units, extras, True, near_units(i + 1), carry)

    init = (jnp.full((1, GQ), NEG, F32),) * N_KV
    carry = lax.fori_loop(0, n_wide, wide_step, init)
    carry = lax.fori_loop(0, n_far, far_step, carry)
    lax.fori_loop(0, n_near, near_step, carry)

    for g in range(N_KV):
        acc = acc_scr[g]
        scale = gt_ref[0, 0, N_KV + g:N_KV + g + 1, :] * (1.0 / acc[HEAD_DIM:HEAD_DIM + 1, :])
        fin_scr[g] = out_scr[g] + scale * acc[:HEAD_DIM, :]


def _attn_call(qt, kc, vct, ks, vst, kw, vwt, gt, ovt, ebank, sbank, wbank, far, eb_shift):
    b, nq = qt.shape[0], qt.shape[1]
    t = ks.shape[1]
    n_cmp_rows = kc.shape[1]
    n_sb = t // SEL_BLOCK
    per_b = lambda shp: pl.BlockSpec((1,) + shp[1:], lambda i, j: (i,) + (0,) * (len(shp) - 1))
    per_q = lambda shp: pl.BlockSpec(
        (1, 1) + shp[2:], lambda i, j: (i, jnp.minimum(j, nq - 1)) + (0,) * (len(shp) - 2))
    full = lambda shp: pl.BlockSpec(shp, lambda i, j: (0,) * len(shp))
    args = (qt, kc, vct, ks, vst, kw, vwt, gt, ovt, ebank, sbank, wbank, far)
    specs = [per_q(qt.shape), per_b(kc.shape), per_b(vct.shape), per_b(ks.shape), per_b(vst.shape),
             per_b(kw.shape), per_b(vwt.shape), per_q(gt.shape), full(ovt.shape),
             full(ebank.shape), full(sbank.shape), full(wbank.shape), full(far.shape)]
    acc_like = pltpu.VMEM((N_KV, HEAD_DIM, GQ), F32)
    return pl.pallas_call(
        functools.partial(_attn_kernel, n_cmp_rows=n_cmp_rows, eb_shift=eb_shift),
        grid=(b, nq + 1),
        in_specs=specs,
        out_specs=pl.BlockSpec((1, Q_BLOCK, ATTN_W), lambda i, j: (i, jnp.maximum(j - 1, 0), 0)),
        out_shape=jax.ShapeDtypeStruct((b, t, ATTN_W), BF16),
        scratch_shapes=[pltpu.VMEM((N_KV, n_sb, GQ), F32),
                        pltpu.VMEM((N_KV, V_ROWS, GQ), F32),
                        acc_like,
                        pltpu.VMEM((N_KV, KV_W, GQ), BF16),
                        pltpu.VMEM((QK_AHEAD, KEY_CHUNK, GQ), BF16),
                        acc_like],
        compiler_params=_cparams(2),
        name="nsa_attention",
    )(*args)


def _ffn_kernel(x_ref, a_ref, c_ref, mod_ref, n2_ref, wo_ref, w1_ref, w2_ref, o_ref):
    aw = a_ref.shape[2]
    mix = jnp.dot(a_ref[0], wo_ref[0:aw, :], preferred_element_type=F32)
    mix = mix + jnp.dot(c_ref[0], wo_ref[aw:, :], preferred_element_type=F32)
    x1 = x_ref[0] + mod_ref[0, 2:3, :] * mix
    ms = jnp.mean(x1 * x1, axis=-1, keepdims=True)
    y = x1 * lax.rsqrt(ms + EPS) * n2_ref[...]
    h2 = (y * (1.0 + mod_ref[0, 4:5, :]) + mod_ref[0, 3:4, :]).astype(BF16)
    d_ff = w1_ref.shape[1]
    ff = jnp.zeros(x1.shape, F32)
    for j in range(d_ff // FF_CHUNK):
        a = jnp.dot(h2, w1_ref[:, j * FF_CHUNK:(j + 1) * FF_CHUNK], preferred_element_type=F32)
        a = jnp.maximum(a, 0.0)
        ff = ff + jnp.dot((a * a).astype(BF16), w2_ref[j * FF_CHUNK:(j + 1) * FF_CHUNK, :],
                          preferred_element_type=F32)
    o_ref[0] = x1 + mod_ref[0, 5:6, :] * ff


def _ffn_call(x, attn, conv, mod, norm2, w_out, w_ff1, w_ff2):
    b, t, d = x.shape
    tm = min(ROW_TILE, t)
    row_spec = lambda w: pl.BlockSpec((1, tm, w), lambda i, j: (i, j, 0))
    full = lambda shp: pl.BlockSpec(shp, lambda i, j: (0,) * len(shp),
                                    pipeline_mode=pl.Buffered(1))
    return pl.pallas_call(
        _ffn_kernel,
        grid=(b, t // tm),
        in_specs=[row_spec(d), row_spec(attn.shape[2]), row_spec(conv.shape[2]),
                  pl.BlockSpec((1, N_MOD, d), lambda i, j: (i, 0, 0)),
                  full((1, d)), full(w_out.shape), full(w_ff1.shape), full(w_ff2.shape)],
        out_specs=row_spec(d),
        out_shape=jax.ShapeDtypeStruct((b, t, d), F32),
        compiler_params=_cparams(2),
        name="outproj_mlp",
    )(x, attn, conv, mod, norm2, w_out, w_ff1, w_ff2)


def _block_diag_ones(n):
    idx = np.arange(n) // HEAD_DIM
    return jnp.asarray(idx[:, None] == idx[None, :], dtype=BF16)


def _pack_w_in(w_in):
    d = w_in.shape[0]
    conv_w = d - ATTN_W
    sizes = [ATTN_W] + [KV_W] * 6 + [N_BRANCH * N_HEADS] + [conv_w] * 3
    offs = np.concatenate([[0], np.cumsum(sizes)])
    part = lambda i: w_in[:, offs[i]:offs[i + 1]]
    q, kc, vc, ks, vs, kw, vw, g, cgate, bgate, u = (part(i) for i in range(11))
    w_nat = jnp.concatenate([kc, vc, ks, kw, cgate, bgate, u], axis=1).astype(BF16)
    gt = g.reshape(d, N_KV, GQA, N_BRANCH).transpose(2, 3, 1, 0).reshape(GQA, N_BRANCH * N_KV, d)
    gt = jnp.pad(gt, ((0, 0), (0, SUBLANES - N_BRANCH * N_KV), (0, 0))).reshape(_G_ROWS, d)
    w_tr = jnp.concatenate([q.T, vs.T, vw.T, gt], axis=0).astype(BF16)
    return w_nat, w_tr


def _expand_w1(w1):
    hid = w1.shape[1]
    w = w1.reshape(2, CMP_STRIDE, 1, HEAD_DIM, 1, hid)
    eye = jnp.eye(N_KV, dtype=w1.dtype).reshape(1, 1, N_KV, 1, N_KV, 1)
    return (w * eye).reshape(2, CMP_STRIDE * KV_W, N_KV * hid).astype(BF16)


def _expand_w2(w2):
    hid = w2.shape[0]
    eye = jnp.eye(N_KV, dtype=w2.dtype).reshape(N_KV, 1, N_KV, 1)
    return (w2.reshape(1, hid, 1, HEAD_DIM) * eye).reshape(N_KV * hid, KV_W).astype(BF16)


def _expand_pe(pe):
    p = pe.reshape(2, CMP_STRIDE, 1, HEAD_DIM)
    return jnp.broadcast_to(p, (2, CMP_STRIDE, N_KV, HEAD_DIM)).reshape(2, CMP_STRIDE * KV_W)


def _bucket_thresholds():
    n = np.arange(2 * REL_MAX_DIST)
    max_exact = REL_BUCKETS // 2
    nf = np.maximum(n, max_exact).astype(np.float32)
    ratio = np.log(nf / np.float32(max_exact)) / np.float32(math.log(REL_MAX_DIST / max_exact))
    large = max_exact + (ratio * np.float32(REL_BUCKETS - max_exact)).astype(np.int32)
    table = np.where(n < max_exact, n, np.minimum(large, REL_BUCKETS - 1))
    return tuple(int(np.searchsorted(table, k, side="left")) for k in range(REL_BUCKETS))


def _bank_call(bias_rows, lead, n_tiles, dist_fn, name):
    nl = len(lead)
    thr = _bucket_thresholds()

    def body(rows_ref, o_ref):
        lead_ids = [pl.program_id(a) for a in range(nl)]
        row = lax.broadcasted_iota(jnp.int32, (SEL_BLOCK, GQ), 0)
        qi = lax.broadcasted_iota(jnp.int32, (SEL_BLOCK, GQ), 1) & (Q_BLOCK - 1)

        def tile(t, carry):
            dist, ok = dist_fn(lead_ids, t, row, qi)
            v = jnp.broadcast_to(rows_ref[0, 0:1, :], (SEL_BLOCK, GQ))
            for k in range(1, REL_BUCKETS):
                v = jnp.where(dist >= thr[k], rows_ref[0, k:k + 1, :], v)
            o_ref[(0,) * (nl + 1) + (t,)] = jnp.where(ok, v, NEG)
            return carry

        lax.fori_loop(0, n_tiles, tile, 0)

    return pl.pallas_call(
        body,
        grid=tuple(lead) + (N_KV,),
        in_specs=[pl.BlockSpec((1, REL_BUCKETS, GQ), lambda *i: (i[nl], 0, 0))],
        out_specs=pl.BlockSpec((1,) * (nl + 1) + (n_tiles, SEL_BLOCK, GQ),
                               lambda *i: tuple(i) + (0, 0, 0)),
        out_shape=jax.ShapeDtypeStruct(tuple(lead) + (N_KV, n_tiles, SEL_BLOCK, GQ), F32),
        compiler_params=_cparams(nl + 1),
        name=name,
    )(bias_rows)


def _bias_banks(rel_bias, t):
    n_cmp_rows = t // CMP_STRIDE
    rows = rel_bias.reshape(REL_BUCKETS, N_KV, GQA).transpose(1, 0, 2)
    rows = jnp.repeat(rows, Q_BLOCK, axis=2) * LOG2E

    def sel_dist(lead, tile, row, qi):
        dist = SEL_BLOCK * tile + qi - row
        return dist, dist >= 0

    sbank = _bank_call(rows, (), NEAR_TILES + 1, sel_dist, "bias_bank_sel")
    far = sbank[:, NEAR_TILES, 0:1, :]

    def win_dist(lead, tile, row, qi):
        dist = SEL_BLOCK * tile + qi - row
        return dist, (dist >= 0) & (dist < WINDOW)

    wbank = _bank_call(rows, (), WIN_BLOCKS + 2, win_dist, "bias_bank_win")

    cmp_c = n_cmp_rows - 4
    shifts = tuple(int((-(cmp_c - 4 * p)) % 8) for p in range(2))
    n_tiles = (cmp_c + n_cmp_rows + 8 + SEL_BLOCK - 1) // SEL_BLOCK

    def cmp_dist(lead, tile, row, qi):
        e = SEL_BLOCK * tile + row - jnp.where(lead[0] == 0, shifts[0], shifts[1])
        dist = qi - CMP_STRIDE * e + (CMP_STRIDE * cmp_c - (CMP_BLOCK - 1))
        return dist, (dist >= 0) & (e >= 0)

    ebank = _bank_call(rows, (2,), n_tiles, cmp_dist, "bias_bank_cmp")
    ebank = ebank.reshape(2, N_KV, n_tiles * SEL_BLOCK, GQ)
    return ebank, sbank, wbank, far, shifts


def _overlap_t(t):
    n_cmp_rows = t // CMP_STRIDE
    n_sb = t // SEL_BLOCK
    c_start = np.arange(n_cmp_rows)[None, :] * CMP_STRIDE
    s_start = np.arange(n_sb)[:, None] * SEL_BLOCK
    ov = np.clip(np.minimum(c_start + CMP_BLOCK, s_start + SEL_BLOCK)
                 - np.maximum(c_start, s_start), 0, None) / CMP_BLOCK
    ov[:, n_cmp_rows - 1] = 0.0
    return jnp.asarray(ov, dtype=BF16)


def _layer(x, c_pad, w_in, q_norm, k_norm, cmp_pe_k, cmp_w1_k, cmp_w2_k, cmp_pe_v, cmp_w1_v,
           cmp_w2_v, rel_bias, conv_w, w_out, norm1, norm2, w_ada, b_ada, w_ff1, w_ff2):
    b, t, d = x.shape
    scale = HEAD_DIM ** -0.5

    mod = _mod_call(c_pad, w_ada, b_ada)[:b].reshape(b, N_MOD, d)

    qn_col = (jnp.tile(q_norm, N_HEADS) * (scale * LOG2E)).reshape(ATTN_W, 1)
    kn_t = jnp.tile(k_norm, N_KV).reshape(1, KV_W)
    bdq = _block_diag_ones(ATTN_W)
    bdk = _block_diag_ones(KV_W)
    w_nat, w_tr = _pack_w_in(w_in)
    qt, kc_raw, vc_raw, ks, vst, kw, vwt, gt, conv = _inproj_call(
        x, mod, norm1.reshape(1, d), w_nat, w_tr, qn_col, kn_t, conv_w, bdq, bdk)

    n_cmp_rows = t // CMP_STRIDE
    kc, vct = _compress_call(
        kc_raw, vc_raw, _expand_pe(cmp_pe_k), _expand_pe(cmp_pe_v), _expand_w1(cmp_w1_k), _expand_w1(cmp_w1_v),
        _expand_w2(cmp_w2_k), _expand_w2(cmp_w2_v).T, kn_t, bdk)

    ebank, sbank, wbank, far, eb_shift = _bias_banks(rel_bias, t)
    attn = _attn_call(qt, kc, vct, ks, vst, kw, vwt, gt, _overlap_t(t), ebank, sbank.astype(BF16),
                      wbank.astype(BF16), far, eb_shift)

    return _ffn_call(x, attn, conv, mod, norm2.reshape(1, d), w_out.astype(BF16),
                     w_ff1.astype(BF16), w_ff2.astype(BF16))


def kernel(x, c, w_in, q_norm, k_norm, cmp_pe_k, cmp_w1_k, cmp_w2_k, cmp_pe_v, cmp_w1_v, cmp_w2_v,
           rel_bias, conv_w, w_out, norm1, norm2, w_ada, b_ada, w_ff1, w_ff2):
    b = x.shape[0]
    c_pad = jnp.pad(c, ((0, (-b) % 8), (0, 0)))
    for l in range(w_in.shape[0]):
        x = _layer(x, c_pad, w_in[l], q_norm[l], k_norm[l], cmp_pe_k[l], cmp_w1_k[l], cmp_w2_k[l],
                   cmp_pe_v[l], cmp_w1_v[l], cmp_w2_v[l], rel_bias, conv_w[l], w_out[l],
                   norm1[l], norm2[l], w_ada[l], b_ada[l], w_ff1[l], w_ff2[l])
    return x
```

```python
import functools
import math

import numpy as np
import jax
import jax.numpy as jnp
from jax import lax
from jax.experimental import pallas as pl
from jax.experimental.pallas import tpu as pltpu

HEAD_DIM = 64
N_HEADS = 8
N_KV = 2
GQA = N_HEADS // N_KV
ATTN_W = N_HEADS * HEAD_DIM
KV_W = N_KV * HEAD_DIM
CONV_K = 3
CMP_BLOCK = 32
CMP_STRIDE = 16
CMP_HIDDEN = 256
SEL_BLOCK = 64
N_SELECT = 16
WINDOW = 512
Q_BLOCK = 64
REL_BUCKETS = 32
REL_MAX_DIST = 1024
N_MOD = 6
N_BRANCH = 3
EPS = 1e-6
NEG = -1e30

LANES = 128
SUBLANES = 8
GQ = GQA * Q_BLOCK
KEY_CHUNK = 256
BLOCKS_PER_CHUNK = KEY_CHUNK // SEL_BLOCK
V_TILE = 128
BF16_ROWS = 16
V_ROWS = HEAD_DIM + BF16_ROWS
V_TILE_ROWS = N_KV * V_ROWS
WIN_BLOCKS = WINDOW // SEL_BLOCK
WIN_TILES = WINDOW // V_TILE + 1
NEAR_TILES = (REL_MAX_DIST + Q_BLOCK - 1) // SEL_BLOCK + 1
ROW_TILE = 512
INPROJ_TILE = 1024
INPROJ_SPLIT = 4
FAR_CHUNKS = 4
NEAR_CHUNKS = 4
QK_AHEAD = 4
LOG2E = math.log2(math.e)
FF_CHUNK = 1024
VMEM_LIMIT = 56 * 1024 * 1024

F32 = jnp.float32
BF16 = jnp.bfloat16
_NT = (((1,), (1,)), ((), ()))


def _cparams(n_axes):
    return pltpu.CompilerParams(dimension_semantics=("arbitrary",) * n_axes,
                                vmem_limit_bytes=VMEM_LIMIT)


def _with_ones_rows(vt):
    ones = jnp.ones((BF16_ROWS, vt.shape[1]), vt.dtype)
    parts = []
    for g in range(N_KV):
        parts += [vt[g * HEAD_DIM:(g + 1) * HEAD_DIM, :], ones]
    return jnp.concatenate(parts, axis=0)


def _swap_halves(p0, p1):
    low = lax.broadcasted_iota(jnp.int32, p0.shape, 1) < LANES // 2
    return (jnp.where(low, p0, pltpu.roll(p1, LANES // 2, 1)),
            jnp.where(low, pltpu.roll(p0, LANES // 2, 1), p1))


def _mod_kernel(c_ref, w_ref, b_ref, o_ref):
    c = c_ref[...]
    a = c * jax.nn.sigmoid(c)
    o_ref[...] = jnp.dot(a, w_ref[...], preferred_element_type=F32,
                         precision=lax.Precision.HIGHEST) + b_ref[...]


def _mod_call(c_pad, w_ada, b_ada):
    rows, d = c_pad.shape
    n = w_ada.shape[1]
    tn = 1024
    return pl.pallas_call(
        _mod_kernel,
        grid=(n // tn,),
        in_specs=[pl.BlockSpec((rows, d), lambda j: (0, 0)),
                  pl.BlockSpec((d, tn), lambda j: (0, j)),
                  pl.BlockSpec((1, tn), lambda j: (0, j))],
        out_specs=pl.BlockSpec((rows, tn), lambda j: (0, j)),
        out_shape=jax.ShapeDtypeStruct((rows, n), F32),
        compiler_params=_cparams(1),
        name="adaln_mod",
    )(c_pad, w_ada, b_ada.reshape(1, n))


_N_KC, _N_VC, _N_KS, _N_KW, _N_CONV = 0, KV_W, 2 * KV_W, 3 * KV_W, 4 * KV_W
_T_Q, _T_VS, _T_VW, _T_G = 0, ATTN_W, ATTN_W + KV_W, ATTN_W + 2 * KV_W
_G_ROWS = GQA * SUBLANES


def _inproj_kernel(x_ref, mod_ref, n1_ref, wn_ref, wt_ref, qn_ref, kn_ref, cw_ref, bdq_ref, bdk_ref,
                   qt_out, kc_out, vc_out, ks_out, vst_out, kw_out, vwt_out, gt_out, conv_out,
                   carry, *, conv_w):
    t = pl.program_id(1)
    tm = x_ref.shape[1]
    sub = tm // INPROJ_SPLIT

    @pl.when(t == 0)
    def _():
        carry[...] = jnp.zeros_like(carry)

    prev2, prev1 = carry[6:7, :], carry[7:8, :]
    c_bg = _N_CONV + conv_w
    c_u = c_bg + conv_w

    for part in range(INPROJ_SPLIT):
        rows = slice(part * sub, (part + 1) * sub)
        x = x_ref[0, rows, :]
        ms = jnp.mean(x * x, axis=-1, keepdims=True)
        y = x * lax.rsqrt(ms + EPS) * n1_ref[...]
        h = (y * (1.0 + mod_ref[0, 1:2, :]) + mod_ref[0, 0:1, :]).astype(BF16)

        def proj(a, b):
            return jnp.dot(h, wn_ref[:, a:b], preferred_element_type=F32)

        def proj_t(a, b):
            return lax.dot_general(wt_ref[a:b, :], h, _NT, preferred_element_type=F32)

        def head_norm(v, gain):
            ssq = jnp.dot((v * v).astype(BF16), bdk_ref[...], preferred_element_type=F32)
            return v * lax.rsqrt(ssq * (1.0 / HEAD_DIM) + EPS) * gain

        qf = proj_t(_T_Q, _T_VS)
        vg = proj_t(_T_VS, _T_G + _G_ROWS)
        kv = proj(_N_KC, _N_CONV)
        cv = proj(_N_CONV, c_u + conv_w)

        ssq = jnp.dot(bdq_ref[...], (qf * qf).astype(BF16), preferred_element_type=F32)
        qf = qf * lax.rsqrt(ssq * (1.0 / HEAD_DIM) + EPS) * qn_ref[...]
        gf = jax.nn.sigmoid(vg[2 * KV_W:, :])
        for c in range(sub // LANES):
            blk = (part * sub) // Q_BLOCK + 2 * c
            cols = slice(c * LANES, (c + 1) * LANES)
            for g in range(N_KV):
                pc = [qf[(g * GQA + r) * HEAD_DIM:(g * GQA + r + 1) * HEAD_DIM, cols]
                      for r in range(GQA)]
                lo01, hi01 = _swap_halves(pc[0], pc[1])
                lo23, hi23 = _swap_halves(pc[2], pc[3])
                qt_out[0, blk, g] = jnp.concatenate([lo01, lo23], axis=1).astype(BF16)
                qt_out[0, blk + 1, g] = jnp.concatenate([hi01, hi23], axis=1).astype(BF16)
            pc = [gf[r * SUBLANES:(r + 1) * SUBLANES, cols] for r in range(GQA)]
            lo01, hi01 = _swap_halves(pc[0], pc[1])
            lo23, hi23 = _swap_halves(pc[2], pc[3])
            gt_out[0, blk] = jnp.concatenate([lo01, lo23], axis=1)
            gt_out[0, blk + 1] = jnp.concatenate([hi01, hi23], axis=1)

        vs_f = _with_ones_rows(vg[:KV_W, :].astype(BF16))
        vw_f = _with_ones_rows(vg[KV_W:2 * KV_W, :].astype(BF16))
        for j in range(sub // KEY_CHUNK):
            vst_out[0, (part * sub) // KEY_CHUNK + j] = vs_f[:, j * KEY_CHUNK:(j + 1) * KEY_CHUNK]
        for j in range(sub // V_TILE):
            vwt_out[0, (part * sub) // V_TILE + j] = vw_f[:, j * V_TILE:(j + 1) * V_TILE]

        kc_out[0, rows, :] = kv[:, _N_KC:_N_VC]
        vc_out[0, rows, :] = kv[:, _N_VC:_N_KS]
        ks_out[0, rows, :] = head_norm(kv[:, _N_KS:_N_KW], kn_ref[...]).astype(BF16)
        kw_out[0, rows, :] = head_norm(kv[:, _N_KW:_N_CONV], kn_ref[...]).astype(BF16)

        z = cv[:, :conv_w] * cv[:, 2 * conv_w:]
        row = lax.broadcasted_iota(jnp.int32, z.shape, 0)
        z1 = jnp.where(row == 0, prev1, pltpu.roll(z, 1, 0))
        z2 = jnp.where(row == 0, prev2, jnp.where(row == 1, prev1, pltpu.roll(z, 2, 0)))
        zc = cw_ref[0:1, :] * z2 + cw_ref[1:2, :] * z1 + cw_ref[2:3, :] * z
        conv_out[0, rows, :] = (cv[:, conv_w:2 * conv_w] * zc).astype(BF16)
        prev2, prev1 = z[sub - 2:sub - 1, :], z[sub - 1:sub, :]
        if part == INPROJ_SPLIT - 1:
            carry[...] = z[sub - SUBLANES:sub, :]


def _inproj_call(x, mod, norm1, w_nat, w_tr, qn_col, kn_t, conv_w, bdq, bdk):
    b, t, d = x.shape
    tm = min(INPROJ_TILE, t)
    cw = conv_w.shape[1]
    nq = t // Q_BLOCK
    row_spec = lambda w: pl.BlockSpec((1, tm, w), lambda i, j: (i, j, 0))
    full = lambda shp: pl.BlockSpec(shp, lambda i, j: (0,) * len(shp))
    vt_spec = lambda w: pl.BlockSpec((1, tm // w, V_TILE_ROWS, w), lambda i, j: (i, j, 0, 0))
    vt_shape = lambda w: jax.ShapeDtypeStruct((b, t // w, V_TILE_ROWS, w), BF16)
    kv = lambda dt: jax.ShapeDtypeStruct((b, t, KV_W), dt)
    out_specs = [pl.BlockSpec((1, tm // Q_BLOCK, N_KV, HEAD_DIM, GQ), lambda i, j: (i, j, 0, 0, 0)),
                 row_spec(KV_W), row_spec(KV_W), row_spec(KV_W), vt_spec(KEY_CHUNK), row_spec(KV_W),
                 vt_spec(V_TILE),
                 pl.BlockSpec((1, tm // Q_BLOCK, SUBLANES, GQ), lambda i, j: (i, j, 0, 0)),
                 row_spec(cw)]
    out_shape = [jax.ShapeDtypeStruct((b, nq, N_KV, HEAD_DIM, GQ), BF16),
                 kv(F32), kv(F32), kv(BF16), vt_shape(KEY_CHUNK), kv(BF16), vt_shape(V_TILE),
                 jax.ShapeDtypeStruct((b, nq, SUBLANES, GQ), F32),
                 jax.ShapeDtypeStruct((b, t, cw), BF16)]
    return pl.pallas_call(
        functools.partial(_inproj_kernel, conv_w=cw),
        grid=(b, t // tm),
        in_specs=[row_spec(d),
                  pl.BlockSpec((1, N_MOD, d), lambda i, j: (i, 0, 0)),
                  full((1, d)), full(w_nat.shape), full(w_tr.shape), full((ATTN_W, 1)),
                  full((1, KV_W)), full((CONV_K, cw)), full((ATTN_W, ATTN_W)), full((KV_W, KV_W))],
        out_specs=out_specs,
        out_shape=out_shape,
        scratch_shapes=[pltpu.VMEM((SUBLANES, cw), F32)],
        compiler_params=_cparams(2),
        name="inproj",
    )(x, mod, norm1, w_nat, w_tr, qn_col, kn_t, conv_w, bdq, bdk)


def _compress_kernel(kx_ref, vx_ref, pek_ref, pev_ref, w1k_ref, w1v_ref, w2k_ref, w2vt_ref,
                     kn_ref, bdk_ref, kc_out, vct_out):
    def hidden(x_ref, pe_ref, w1_ref):
        n = x_ref.shape[1] // CMP_STRIDE
        u = jnp.zeros((n, w1_ref.shape[2]), F32)
        v = jnp.zeros((n, w1_ref.shape[2]), F32)
        for r in range(0, CMP_STRIDE, 2):
            tok = [x_ref[0, pl.ds(r + d, n, stride=CMP_STRIDE), :] for d in range(2)]
            cols = slice(r * KV_W, (r + 2) * KV_W)
            for a, acc in ((0, "u"), (1, "v")):
                lhs = jnp.concatenate([tok[d] + pe_ref[a:a + 1, (r + d) * KV_W:(r + d + 1) * KV_W]
                                       for d in range(2)], axis=1).astype(BF16)
                prod = jnp.dot(lhs, w1_ref[a, cols, :], preferred_element_type=F32)
                if acc == "u":
                    u = u + prod
                else:
                    v = v + prod
        hid = u + pltpu.roll(v, n - 1, 0)
        return jax.nn.gelu(hid, approximate=True).astype(BF16)

    kc = jnp.dot(hidden(kx_ref, pek_ref, w1k_ref), w2k_ref[...], preferred_element_type=F32)
    ssq = jnp.dot((kc * kc).astype(BF16), bdk_ref[...], preferred_element_type=F32)
    kc_out[0] = (kc * lax.rsqrt(ssq * (1.0 / HEAD_DIM) + EPS) * kn_ref[...]).astype(BF16)
    vct_out[0] = _with_ones_rows(lax.dot_general(w2vt_ref[...], hidden(vx_ref, pev_ref, w1v_ref), _NT,
                                                 preferred_element_type=F32).astype(BF16))


def _compress_call(kx, vx, pek, pev, w1k, w1v, w2k, w2vt, kn_t, bdk):
    b, t, _ = kx.shape
    nrow = t // CMP_STRIDE
    wide, hid2 = w1k.shape[1], w1k.shape[2]
    full = lambda shp: pl.BlockSpec(shp, lambda i: (0,) * len(shp))
    xs = pl.BlockSpec((1, t, KV_W), lambda i: (i, 0, 0))
    return pl.pallas_call(
        _compress_kernel,
        grid=(b,),
        in_specs=[xs, xs, full((2, wide)), full((2, wide)), full((2, wide, hid2)),
                  full((2, wide, hid2)), full((hid2, KV_W)), full((KV_W, hid2)),
                  full((1, KV_W)), full((KV_W, KV_W))],
        out_specs=[pl.BlockSpec((1, nrow, KV_W), lambda i: (i, 0, 0)),
                   pl.BlockSpec((1, V_TILE_ROWS, nrow), lambda i: (i, 0, 0))],
        out_shape=[jax.ShapeDtypeStruct((b, nrow, KV_W), BF16),
                   jax.ShapeDtypeStruct((b, V_TILE_ROWS, nrow), BF16)],
        compiler_params=_cparams(1),
        name="compress",
    )(kx, vx, pek, pev, w1k, w1v, w2k, w2vt, kn_t, bdk)


def _attn_kernel(q_ref, kc_ref, vct_ref, ks_ref, vst_ref, kw_ref, vwt_ref, gt_ref, ovt_ref,
                 eb_ref, sb_ref, wb_ref, far_ref, o_ref, sel_scr, acc_scr, out_scr, qp_scr, s_scr,
                 fin_scr, *, n_cmp_rows, eb_shift):
    n_sb = sel_scr.shape[1]
    ci = jnp.minimum(pl.program_id(1), n_sb - 1)
    par = lax.rem(ci, 2)
    n_total = ks_ref.shape[1] // KEY_CHUNK
    vsl = lambda g: slice(g * V_ROWS, (g + 1) * V_ROWS)

    @pl.when((pl.program_id(0) == 0) & (pl.program_id(1) == 0))
    def _():
        def zero(g, carry):
            fin_scr[g] = jnp.zeros(fin_scr.shape[1:], F32)
            return carry
        lax.fori_loop(0, N_KV, zero, 0)

    pieces = []
    for g in range(N_KV):
        for half in range(GQ // LANES):
            a = fin_scr[g, :, half * LANES:(half + 1) * LANES]
            stacked = jnp.concatenate([a, pltpu.roll(a, Q_BLOCK, 1)], axis=0)
            pieces.append(stacked.T[:Q_BLOCK, :])
    o_ref[0] = jnp.concatenate(pieces, axis=1).astype(BF16)

    zeros_q = jnp.zeros((HEAD_DIM, GQ), BF16)
    qp_scr[0] = jnp.concatenate([q_ref[0, 0, 0], zeros_q], axis=0)
    qp_scr[1] = jnp.concatenate([zeros_q, q_ref[0, 0, 1]], axis=0)

    def chunk_qk(g, c):
        kk = ks_ref[0, pl.ds(pl.multiple_of(c * KEY_CHUNK, KEY_CHUNK), KEY_CHUNK), :]
        return jnp.dot(kk, qp_scr[g], preferred_element_type=F32).astype(BF16)

    n_chunks = ci // BLOCKS_PER_CHUNK + 1
    far_groups = jnp.maximum(ci - (NEAR_TILES - 1), 0) // (BLOCKS_PER_CHUNK * FAR_CHUNKS)
    n_wide = far_groups // 2
    n_far = far_groups - 2 * n_wide
    c_far = n_wide * (2 * FAR_CHUNKS)
    c0 = far_groups * FAR_CHUNKS
    n_near = (n_chunks - c0 + NEAR_CHUNKS - 1) // NEAR_CHUNKS

    def near_units(i):
        return [(g, jnp.minimum(c0 + NEAR_CHUNKS * i + j, n_total - 1))
                for g in range(N_KV) for j in range(NEAR_CHUNKS)]

    def far_units(i):
        return [(g, c_far + i * FAR_CHUNKS + j) for g in range(N_KV) for j in range(FAR_CHUNKS)]

    def wide_units(i):
        return [(g, i * 2 * FAR_CHUNKS + j) for g in range(N_KV) for j in range(2 * FAR_CHUNKS)]

    def pick_units(options):
        units = options[-1][1][:QK_AHEAD]
        for cond, cand in reversed(options[:-1]):
            units = [(jnp.where(cond, g_a, g_b), jnp.where(cond, c_a, c_b))
                     for (g_a, c_a), (g_b, c_b) in zip(cand[:QK_AHEAD], units)]
        return [(g, jnp.minimum(c, n_total - 1)) for g, c in units]

    def prefetch_scores(units):
        for k in range(QK_AHEAD):
            s_scr[k] = chunk_qk(*units[k])

    lane = lax.broadcasted_iota(jnp.int32, (n_sb, LANES), 1)
    cmp_c = n_cmp_rows - 4
    e0 = cmp_c - 4 * ci + jnp.where(par == 0, eb_shift[0], eb_shift[1])
    e0 = pl.multiple_of(e0, 8)
    o_cmp = []
    imp = []
    w0 = ci // 2 - (WIN_TILES - 1)
    win_tiles = [jnp.maximum(w0 + j, 0) for j in range(WIN_TILES)]
    s_cmp = [jnp.dot(kc_ref[0], qp_scr[g], preferred_element_type=F32) for g in range(N_KV)]
    for g in range(N_KV):
        sc = s_cmp[g] + eb_ref[par, g, pl.ds(e0, n_cmp_rows), :]
        m = jnp.max(sc, axis=0, keepdims=True)
        e = jnp.exp2(sc - m).astype(BF16)
        ov = jnp.dot(vct_ref[0, vsl(g), :], e, preferred_element_type=F32)
        inv = jnp.where(m > 0.5 * NEG, 1.0 / ov[HEAD_DIM:HEAD_DIM + 1, :], 0.0)
        o_cmp.append(ov[:HEAD_DIM, :] * inv)
        ir = jnp.dot(ovt_ref[...], e, preferred_element_type=F32) * inv
        a = ir[:, :LANES] + ir[:, LANES:]
        imp.append(a + pltpu.roll(a, Q_BLOCK, 1))

    s_win = [[jnp.dot(kw_ref[0, pl.ds(pl.multiple_of(tj * V_TILE, V_TILE), V_TILE), :], qp_scr[g],
                      preferred_element_type=F32).astype(BF16) for tj in win_tiles]
             for g in range(N_KV)]

    prefetch_scores(pick_units([(n_wide > 0, wide_units(0)), (n_far > 0, far_units(0)),
                                (None, near_units(0))]))

    jidx = lax.broadcasted_iota(jnp.int32, (n_sb, LANES), 0)
    jf = jidx.astype(F32)
    valid = jidx <= ci
    forced = (jidx == 0) | (jidx == ci) | (jidx == ci - 1)
    score = jnp.where(valid, jnp.where(forced, -2.0, jnp.where(lane < Q_BLOCK, imp[0], imp[1])),
                      -1.0)
    sel = jnp.where(forced, 1.0, 0.0)

    def first_max(score):
        pairs = [(score[r:r + SUBLANES], jf[r:r + SUBLANES]) for r in range(0, n_sb, SUBLANES)]
        while len(pairs) > 1:
            nxt = []
            for (va, ia), (vb, ib) in zip(pairs[0::2], pairs[1::2]):
                keep = va >= vb
                nxt.append((jnp.where(keep, va, vb), jnp.where(keep, ia, ib)))
            pairs = nxt + pairs[len(pairs) - len(pairs) % 2:]
        v8, i8 = pairs[0]
        mx = jnp.max(v8, axis=0, keepdims=True)
        return jnp.min(jnp.where(v8 == mx, i8, float(n_sb)), axis=0, keepdims=True)

    for _ in range(min(N_SELECT, n_sb) - 3):
        first = first_max(score)
        hit = jf == first
        sel = jnp.where(hit, 1.0, sel)
        score = jnp.where(hit, -2.0, score)
    selneg = jnp.where((sel > 0.5) & valid, 0.0, NEG)
    swapped = pltpu.roll(selneg, Q_BLOCK, 1)
    left = jnp.where(lane < Q_BLOCK, selneg, swapped)
    right = jnp.where(lane < Q_BLOCK, swapped, selneg)
    sel_scr[0] = jnp.concatenate([left, left], axis=1)
    sel_scr[1] = jnp.concatenate([right, right], axis=1)

    for g in range(N_KV):
        parts = []
        for i in range(2 * WIN_TILES):
            delta = par + WIN_BLOCKS - i
            ok = (delta >= 0) & (delta <= WIN_BLOCKS) & (delta <= ci)
            tile = jnp.where(ok, delta, WIN_BLOCKS + 1)
            half = s_win[g][i // 2][(i % 2) * SEL_BLOCK:(i % 2 + 1) * SEL_BLOCK, :]
            parts.append(half + wb_ref[g, tile])
        s = jnp.concatenate(parts, axis=0)
        m = jnp.max(s, axis=0, keepdims=True)
        pb = jnp.exp2(s - m)
        o_win = jnp.zeros((V_ROWS, GQ), F32)
        for j, tj in enumerate(win_tiles):
            o_win = o_win + jnp.dot(vwt_ref[0, tj, vsl(g), :], pb[j * V_TILE:(j + 1) * V_TILE, :],
                                    preferred_element_type=F32)
        w_scale = gt_ref[0, 0, 2 * N_KV + g:2 * N_KV + g + 1, :] * (1.0 / o_win[HEAD_DIM:HEAD_DIM + 1, :])
        out_scr[g] = gt_ref[0, 0, g:g + 1, :] * o_cmp[g] + w_scale * o_win[:HEAD_DIM, :]
        acc_scr[g] = jnp.zeros((V_ROWS, GQ), F32)

    def chunk_softmax(s, g, c, near, extra):
        parts = []
        for i in range(BLOCKS_PER_CHUNK):
            kb = c * BLOCKS_PER_CHUNK + i
            blk = s[i * SEL_BLOCK:(i + 1) * SEL_BLOCK, :]
            mrow = sel_scr[g, pl.ds(kb, 1), :]
            if extra is not None:
                mrow = mrow + extra
            if near:
                tile = jnp.clip(ci - kb, 0, NEAR_TILES)
                blk = blk + sb_ref[g, tile] + mrow.astype(BF16)
            else:
                blk = blk + (mrow + far_ref[g]).astype(BF16)
            parts.append(blk)
        s = jnp.concatenate(parts, axis=0)
        m_c = jnp.max(s, axis=0, keepdims=True)
        return m_c.astype(F32), jnp.exp2(s - m_c)

    def chunk_pv(pb, g, c):
        return jnp.dot(vst_ref[0, c, vsl(g), :], pb, preferred_element_type=F32)

    def merge_step(units, extras, near, next_units, carry):
        scores = {k: s_scr[k] for k in range(QK_AHEAD)}
        results = []
        for k, (g, c) in enumerate(units):
            m_c, pb = chunk_softmax(scores.pop(k), g, c, near, extras[k])
            if k + QK_AHEAD < len(units):
                scores[k + QK_AHEAD] = chunk_qk(*units[k + QK_AHEAD])
            else:
                j = k + QK_AHEAD - len(units)
                s_scr[j] = chunk_qk(*next_units[j])
            results.append((m_c, chunk_pv(pb, g, c)))
        per_g = len(units) // N_KV
        new = []
        for g in range(N_KV):
            m_run = carry[g]
            stats = results[g * per_g:(g + 1) * per_g]
            m_new = m_run
            for m_c, _ in stats:
                m_new = jnp.maximum(m_new, m_c)
            acc = jnp.exp2(m_run - m_new) * acc_scr[g]
            for m_c, pv in stats:
                acc = acc + jnp.exp2(m_c - m_new) * pv
            acc_scr[g] = acc
            new.append(m_new)
        return tuple(new)

    def wide_step(i, carry):
        units = wide_units(i)
        nxt = pick_units([(i + 1 < n_wide, wide_units(i + 1)), (n_far > 0, far_units(0)),
                          (None, near_units(0))])
        return merge_step(units, [None] * len(units), False, nxt, carry)

    def far_step(i, carry):
        units = far_units(i)
        return merge_step(units, [None] * len(units), False, pick_units([(None, near_units(0))]), carry)

    def near_step(i, carry):
        units = near_units(i)
        extras = [None if j == 0 else jnp.where(c0 + NEAR_CHUNKS * i + j < n_chunks, 0.0, NEG)
                  for _ in range(N_KV) for j in range(NEAR_CHUNKS)]
        return merge_step(units, extras, True, near_units(i + 1), carry)

    init = (jnp.full((1, GQ), NEG, F32),) * N_KV
    carry = lax.fori_loop(0, n_wide, wide_step, init)
    carry = lax.fori_loop(0, n_far, far_step, carry)
    lax.fori_loop(0, n_near, near_step, carry)

    for g in range(N_KV):
        acc = acc_scr[g]
        scale = gt_ref[0, 0, N_KV + g:N_KV + g + 1, :] * (1.0 / acc[HEAD_DIM:HEAD_DIM + 1, :])
        fin_scr[g] = out_scr[g] + scale * acc[:HEAD_DIM, :]


def _attn_call(qt, kc, vct, ks, vst, kw, vwt, gt, ovt, ebank, sbank, wbank, far, eb_shift):
    b, nq = qt.shape[0], qt.shape[1]
    t = ks.shape[1]
    n_cmp_rows = kc.shape[1]
    n_sb = t // SEL_BLOCK
    per_b = lambda shp: pl.BlockSpec((1,) + shp[1:], lambda i, j: (i,) + (0,) * (len(shp) - 1))
    per_q = lambda shp: pl.BlockSpec(
        (1, 1) + shp[2:], lambda i, j: (i, jnp.minimum(j, nq - 1)) + (0,) * (len(shp) - 2))
    full = lambda shp: pl.BlockSpec(shp, lambda i, j: (0,) * len(shp))
    args = (qt, kc, vct, ks, vst, kw, vwt, gt, ovt, ebank, sbank, wbank, far)
    specs = [per_q(qt.shape), per_b(kc.shape), per_b(vct.shape), per_b(ks.shape), per_b(vst.shape),
             per_b(kw.shape), per_b(vwt.shape), per_q(gt.shape), full(ovt.shape),
             full(ebank.shape), full(sbank.shape), full(wbank.shape), full(far.shape)]
    acc_like = pltpu.VMEM((N_KV, HEAD_DIM, GQ), F32)
    return pl.pallas_call(
        functools.partial(_attn_kernel, n_cmp_rows=n_cmp_rows, eb_shift=eb_shift),
        grid=(b, nq + 1),
        in_specs=specs,
        out_specs=pl.BlockSpec((1, Q_BLOCK, ATTN_W), lambda i, j: (i, jnp.maximum(j - 1, 0), 0)),
        out_shape=jax.ShapeDtypeStruct((b, t, ATTN_W), BF16),
        scratch_shapes=[pltpu.VMEM((N_KV, n_sb, GQ), F32),
                        pltpu.VMEM((N_KV, V_ROWS, GQ), F32),
                        acc_like,
                        pltpu.VMEM((N_KV, KV_W, GQ), BF16),
                        pltpu.VMEM((QK_AHEAD, KEY_CHUNK, GQ), BF16),
                        acc_like],
        compiler_params=_cparams(2),
        name="nsa_attention",
    )(*args)


def _ffn_kernel(x_ref, a_ref, c_ref, mod_ref, n2_ref, wo_ref, w1_ref, w2_ref, o_ref):
    aw = a_ref.shape[2]
    mix = jnp.dot(a_ref[0], wo_ref[0:aw, :], preferred_element_type=F32)
    mix = mix + jnp.dot(c_ref[0], wo_ref[aw:, :], preferred_element_type=F32)
    x1 = x_ref[0] + mod_ref[0, 2:3, :] * mix
    ms = jnp.mean(x1 * x1, axis=-1, keepdims=True)
    y = x1 * lax.rsqrt(ms + EPS) * n2_ref[...]
    h2 = (y * (1.0 + mod_ref[0, 4:5, :]) + mod_ref[0, 3:4, :]).astype(BF16)
    d_ff = w1_ref.shape[1]
    ff = jnp.zeros(x1.shape, F32)
    for j in range(d_ff // FF_CHUNK):
        a = jnp.dot(h2, w1_ref[:, j * FF_CHUNK:(j + 1) * FF_CHUNK], preferred_element_type=F32)
        a = jnp.maximum(a, 0.0)
        ff = ff + jnp.dot((a * a).astype(BF16), w2_ref[j * FF_CHUNK:(j + 1) * FF_CHUNK, :],
                          preferred_element_type=F32)
    o_ref[0] = x1 + mod_ref[0, 5:6, :] * ff


def _ffn_call(x, attn, conv, mod, norm2, w_out, w_ff1, w_ff2):
    b, t, d = x.shape
    tm = min(ROW_TILE, t)
    row_spec = lambda w: pl.BlockSpec((1, tm, w), lambda i, j: (i, j, 0))
    full = lambda shp: pl.BlockSpec(shp, lambda i, j: (0,) * len(shp),
                                    pipeline_mode=pl.Buffered(1))
    return pl.pallas_call(
        _ffn_kernel,
        grid=(b, t // tm),
        in_specs=[row_spec(d), row_spec(attn.shape[2]), row_spec(conv.shape[2]),
                  pl.BlockSpec((1, N_MOD, d), lambda i, j: (i, 0, 0)),
                  full((1, d)), full(w_out.shape), full(w_ff1.shape), full(w_ff2.shape)],
        out_specs=row_spec(d),
        out_shape=jax.ShapeDtypeStruct((b, t, d), F32),
        compiler_params=_cparams(2),
        name="outproj_mlp",
    )(x, attn, conv, mod, norm2, w_out, w_ff1, w_ff2)


def _block_diag_ones(n):
    idx = np.arange(n) // HEAD_DIM
    return jnp.asarray(idx[:, None] == idx[None, :], dtype=BF16)


def _pack_w_in(w_in):
    d = w_in.shape[0]
    conv_w = d - ATTN_W
    sizes = [ATTN_W] + [KV_W] * 6 + [N_BRANCH * N_HEADS] + [conv_w] * 3
    offs = np.concatenate([[0], np.cumsum(sizes)])
    part = lambda i: w_in[:, offs[i]:offs[i + 1]]
    q, kc, vc, ks, vs, kw, vw, g, cgate, bgate, u = (part(i) for i in range(11))
    w_nat = jnp.concatenate([kc, vc, ks, kw, cgate, bgate, u], axis=1).astype(BF16)
    gt = g.reshape(d, N_KV, GQA, N_BRANCH).transpose(2, 3, 1, 0).reshape(GQA, N_BRANCH * N_KV, d)
    gt = jnp.pad(gt, ((0, 0), (0, SUBLANES - N_BRANCH * N_KV), (0, 0))).reshape(_G_ROWS, d)
    w_tr = jnp.concatenate([q.T, vs.T, vw.T, gt], axis=0).astype(BF16)
    return w_nat, w_tr


def _expand_w1(w1):
    hid = w1.shape[1]
    w = w1.reshape(2, CMP_STRIDE, HEAD_DIM, hid).astype(BF16)
    zero = jnp.zeros_like(w)
    per_group = [jnp.concatenate([w if k == g else zero for k in range(N_KV)], axis=-1)
                 for g in range(N_KV)]
    return jnp.stack(per_group, axis=2).reshape(2, CMP_STRIDE * KV_W, N_KV * hid)


def _expand_w2(w2):
    hid = w2.shape[0]
    eye = jnp.eye(N_KV, dtype=w2.dtype).reshape(N_KV, 1, N_KV, 1)
    return (w2.reshape(1, hid, 1, HEAD_DIM) * eye).reshape(N_KV * hid, KV_W).astype(BF16)


def _expand_pe(pe):
    p = pe.reshape(2, CMP_STRIDE, 1, HEAD_DIM)
    return jnp.broadcast_to(p, (2, CMP_STRIDE, N_KV, HEAD_DIM)).reshape(2, CMP_STRIDE * KV_W)


def _bucket_thresholds():
    n = np.arange(2 * REL_MAX_DIST)
    max_exact = REL_BUCKETS // 2
    nf = np.maximum(n, max_exact).astype(np.float32)
    ratio = np.log(nf / np.float32(max_exact)) / np.float32(math.log(REL_MAX_DIST / max_exact))
    large = max_exact + (ratio * np.float32(REL_BUCKETS - max_exact)).astype(np.int32)
    table = np.where(n < max_exact, n, np.minimum(large, REL_BUCKETS - 1))
    return tuple(int(np.searchsorted(table, k, side="left")) for k in range(REL_BUCKETS))


def _bank_call(bias_rows, lead, n_tiles, dist_fn, name):
    nl = len(lead)
    thr = _bucket_thresholds()

    def body(rows_ref, o_ref):
        lead_ids = [pl.program_id(a) for a in range(nl)]
        row = lax.broadcasted_iota(jnp.int32, (SEL_BLOCK, GQ), 0)
        qi = lax.broadcasted_iota(jnp.int32, (SEL_BLOCK, GQ), 1) & (Q_BLOCK - 1)

        def tile(t, carry):
            dist, ok = dist_fn(lead_ids, t, row, qi)
            v = jnp.broadcast_to(rows_ref[0, 0:1, :], (SEL_BLOCK, GQ))
            for k in range(1, REL_BUCKETS):
                v = jnp.where(dist >= thr[k], rows_ref[0, k:k + 1, :], v)
            o_ref[(0,) * (nl + 1) + (t,)] = jnp.where(ok, v, NEG)
            return carry

        lax.fori_loop(0, n_tiles, tile, 0)

    return pl.pallas_call(
        body,
        grid=tuple(lead) + (N_KV,),
        in_specs=[pl.BlockSpec((1, REL_BUCKETS, GQ), lambda *i: (i[nl], 0, 0))],
        out_specs=pl.BlockSpec((1,) * (nl + 1) + (n_tiles, SEL_BLOCK, GQ),
                               lambda *i: tuple(i) + (0, 0, 0)),
        out_shape=jax.ShapeDtypeStruct(tuple(lead) + (N_KV, n_tiles, SEL_BLOCK, GQ), F32),
        compiler_params=_cparams(nl + 1),
        name=name,
    )(bias_rows)


def _bias_banks(rel_bias, t):
    n_cmp_rows = t // CMP_STRIDE
    rows = rel_bias.reshape(REL_BUCKETS, N_KV, GQA).transpose(1, 0, 2)
    rows = jnp.repeat(rows, Q_BLOCK, axis=2) * LOG2E

    def sel_dist(lead, tile, row, qi):
        dist = SEL_BLOCK * tile + qi - row
        return dist, dist >= 0

    sbank = _bank_call(rows, (), NEAR_TILES + 1, sel_dist, "bias_bank_sel")
    far = sbank[:, NEAR_TILES, 0:1, :]

    def win_dist(lead, tile, row, qi):
        dist = SEL_BLOCK * tile + qi - row
        return dist, (dist >= 0) & (dist < WINDOW)

    wbank = _bank_call(rows, (), WIN_BLOCKS + 2, win_dist, "bias_bank_win")

    cmp_c = n_cmp_rows - 4
    shifts = tuple(int((-(cmp_c - 4 * p)) % 8) for p in range(2))
    n_tiles = (cmp_c + n_cmp_rows + 8 + SEL_BLOCK - 1) // SEL_BLOCK

    def cmp_dist(lead, tile, row, qi):
        e = SEL_BLOCK * tile + row - jnp.where(lead[0] == 0, shifts[0], shifts[1])
        dist = qi - CMP_STRIDE * e + (CMP_STRIDE * cmp_c - (CMP_BLOCK - 1))
        return dist, (dist >= 0) & (e >= 0)

    ebank = _bank_call(rows, (2,), n_tiles, cmp_dist, "bias_bank_cmp")
    ebank = ebank.reshape(2, N_KV, n_tiles * SEL_BLOCK, GQ)
    return ebank, sbank, wbank, far, shifts


def _overlap_t(t):
    n_cmp_rows = t // CMP_STRIDE
    n_sb = t // SEL_BLOCK
    c_start = np.arange(n_cmp_rows)[None, :] * CMP_STRIDE
    s_start = np.arange(n_sb)[:, None] * SEL_BLOCK
    ov = np.clip(np.minimum(c_start + CMP_BLOCK, s_start + SEL_BLOCK)
                 - np.maximum(c_start, s_start), 0, None) / CMP_BLOCK
    ov[:, n_cmp_rows - 1] = 0.0
    return jnp.asarray(ov, dtype=BF16)


def _layer(x, c_pad, w_in, q_norm, k_norm, cmp_pe_k, cmp_w1_k, cmp_w2_k, cmp_pe_v, cmp_w1_v,
           cmp_w2_v, rel_bias, conv_w, w_out, norm1, norm2, w_ada, b_ada, w_ff1, w_ff2):
    b, t, d = x.shape
    scale = HEAD_DIM ** -0.5

    mod = _mod_call(c_pad, w_ada, b_ada)[:b].reshape(b, N_MOD, d)

    qn_col = (jnp.tile(q_norm, N_HEADS) * (scale * LOG2E)).reshape(ATTN_W, 1)
    kn_t = jnp.tile(k_norm, N_KV).reshape(1, KV_W)
    bdq = _block_diag_ones(ATTN_W)
    bdk = _block_diag_ones(KV_W)
    w_nat, w_tr = _pack_w_in(w_in)
    qt, kc_raw, vc_raw, ks, vst, kw, vwt, gt, conv = _inproj_call(
        x, mod, norm1.reshape(1, d), w_nat, w_tr, qn_col, kn_t, conv_w, bdq, bdk)

    n_cmp_rows = t // CMP_STRIDE
    kc, vct = _compress_call(
        kc_raw, vc_raw, _expand_pe(cmp_pe_k), _expand_pe(cmp_pe_v), _expand_w1(cmp_w1_k), _expand_w1(cmp_w1_v),
        _expand_w2(cmp_w2_k), _expand_w2(cmp_w2_v).T, kn_t, bdk)

    ebank, sbank, wbank, far, eb_shift = _bias_banks(rel_bias, t)
    attn = _attn_call(qt, kc, vct, ks, vst, kw, vwt, gt, _overlap_t(t), ebank, sbank.astype(BF16),
                      wbank.astype(BF16), far, eb_shift)

    return _ffn_call(x, attn, conv, mod, norm2.reshape(1, d), w_out.astype(BF16),
                     w_ff1.astype(BF16), w_ff2.astype(BF16))


def kernel(x, c, w_in, q_norm, k_norm, cmp_pe_k, cmp_w1_k, cmp_w2_k, cmp_pe_v, cmp_w1_v, cmp_w2_v,
           rel_bias, conv_w, w_out, norm1, norm2, w_ada, b_ada, w_ff1, w_ff2):
    b = x.shape[0]
    c_pad = jnp.pad(c, ((0, (-b) % 8), (0, 0)))
    for l in range(w_in.shape[0]):
        x = _layer(x, c_pad, w_in[l], q_norm[l], k_norm[l], cmp_pe_k[l], cmp_w1_k[l], cmp_w2_k[l],
                   cmp_pe_v[l], cmp_w1_v[l], cmp_w2_v[l], rel_bias, conv_w[l], w_out[l],
                   norm1[l], norm2[l], w_ada[l], b_ada[l], w_ff1[l], w_ff2[l])
    return x
```

```python
import functools
import math

import numpy as np
import jax
import jax.numpy as jnp
from jax import lax
from jax.experimental import pallas as pl
from jax.experimental.pallas import tpu as pltpu

HEAD_DIM = 64
N_HEADS = 8
N_KV = 2
GQA = N_HEADS // N_KV
ATTN_W = N_HEADS * HEAD_DIM
KV_W = N_KV * HEAD_DIM
CONV_K = 3
CMP_BLOCK = 32
CMP_STRIDE = 16
CMP_HIDDEN = 256
SEL_BLOCK = 64
N_SELECT = 16
WINDOW = 512
Q_BLOCK = 64
REL_BUCKETS = 32
REL_MAX_DIST = 1024
N_MOD = 6
N_BRANCH = 3
EPS = 1e-6
NEG = -1e30

LANES = 128
SUBLANES = 8
GQ = GQA * Q_BLOCK
KEY_CHUNK = 256
BLOCKS_PER_CHUNK = KEY_CHUNK // SEL_BLOCK
V_TILE = 128
BF16_ROWS = 16
V_ROWS = HEAD_DIM + BF16_ROWS
V_TILE_ROWS = N_KV * V_ROWS
CMP_ROW_STEP = 128
WIN_BLOCKS = WINDOW // SEL_BLOCK
WIN_TILES = WINDOW // V_TILE + 1
NEAR_TILES = (REL_MAX_DIST + Q_BLOCK - 1) // SEL_BLOCK + 1
ROW_TILE = 512
INPROJ_TILE = 1024
INPROJ_SPLIT = 4
FAR_CHUNKS = 4
NEAR_CHUNKS = 4
QK_AHEAD = 4
LOG2E = math.log2(math.e)
FF_CHUNK = 1024
VMEM_LIMIT = 56 * 1024 * 1024

F32 = jnp.float32
BF16 = jnp.bfloat16
_NT = (((1,), (1,)), ((), ()))


def _cparams(n_axes):
    return pltpu.CompilerParams(dimension_semantics=("arbitrary",) * n_axes,
                                vmem_limit_bytes=VMEM_LIMIT)


def _with_ones_rows(vt):
    ones = jnp.ones((BF16_ROWS, vt.shape[1]), vt.dtype)
    parts = []
    for g in range(N_KV):
        parts += [vt[g * HEAD_DIM:(g + 1) * HEAD_DIM, :], ones]
    return jnp.concatenate(parts, axis=0)


def _swap_halves(p0, p1):
    low = lax.broadcasted_iota(jnp.int32, p0.shape, 1) < LANES // 2
    return (jnp.where(low, p0, pltpu.roll(p1, LANES // 2, 1)),
            jnp.where(low, pltpu.roll(p0, LANES // 2, 1), p1))


def _mod_kernel(c_ref, w_ref, b_ref, o_ref):
    c = c_ref[...]
    a = c * jax.nn.sigmoid(c)
    o_ref[...] = jnp.dot(a, w_ref[...], preferred_element_type=F32,
                         precision=lax.Precision.HIGHEST) + b_ref[...]


def _mod_call(c_pad, w_ada, b_ada):
    rows, d = c_pad.shape
    n = w_ada.shape[1]
    tn = 1024
    return pl.pallas_call(
        _mod_kernel,
        grid=(n // tn,),
        in_specs=[pl.BlockSpec((rows, d), lambda j: (0, 0)),
                  pl.BlockSpec((d, tn), lambda j: (0, j)),
                  pl.BlockSpec((1, tn), lambda j: (0, j))],
        out_specs=pl.BlockSpec((rows, tn), lambda j: (0, j)),
        out_shape=jax.ShapeDtypeStruct((rows, n), F32),
        compiler_params=_cparams(1),
        name="adaln_mod",
    )(c_pad, w_ada, b_ada.reshape(1, n))


_N_KC, _N_VC, _N_KS, _N_KW, _N_CONV = 0, KV_W, 2 * KV_W, 3 * KV_W, 4 * KV_W
_T_Q, _T_VS, _T_VW, _T_G = 0, ATTN_W, ATTN_W + KV_W, ATTN_W + 2 * KV_W
_G_ROWS = GQA * SUBLANES


def _inproj_kernel(x_ref, mod_ref, n1_ref, wn_ref, wt_ref, qn_ref, kn_ref, cw_ref, bdq_ref, bdk_ref,
                   qt_out, kc_out, vc_out, ks_out, vst_out, kw_out, vwt_out, gt_out, conv_out,
                   carry, *, conv_w):
    t = pl.program_id(1)
    tm = x_ref.shape[1]
    sub = tm // INPROJ_SPLIT

    @pl.when(t == 0)
    def _():
        carry[...] = jnp.zeros_like(carry)

    prev2, prev1 = carry[6:7, :], carry[7:8, :]
    c_bg = _N_CONV + conv_w
    c_u = c_bg + conv_w

    for part in range(INPROJ_SPLIT):
        rows = slice(part * sub, (part + 1) * sub)
        x = x_ref[0, rows, :]
        ms = jnp.mean(x * x, axis=-1, keepdims=True)
        y = x * lax.rsqrt(ms + EPS) * n1_ref[...]
        h = (y * (1.0 + mod_ref[0, 1:2, :]) + mod_ref[0, 0:1, :]).astype(BF16)

        def proj(a, b):
            return jnp.dot(h, wn_ref[:, a:b], preferred_element_type=F32)

        def proj_t(a, b):
            return lax.dot_general(wt_ref[a:b, :], h, _NT, preferred_element_type=F32)

        def head_norm(v, gain):
            ssq = jnp.dot((v * v).astype(BF16), bdk_ref[...], preferred_element_type=F32)
            return v * lax.rsqrt(ssq * (1.0 / HEAD_DIM) + EPS) * gain

        qf = proj_t(_T_Q, _T_VS)
        vg = proj_t(_T_VS, _T_G + _G_ROWS)
        kv = proj(_N_KC, _N_CONV)
        cv = proj(_N_CONV, c_u + conv_w)

        ssq = jnp.dot(bdq_ref[...], (qf * qf).astype(BF16), preferred_element_type=F32)
        qf = qf * lax.rsqrt(ssq * (1.0 / HEAD_DIM) + EPS) * qn_ref[...]
        gf = jax.nn.sigmoid(vg[2 * KV_W:, :])
        for c in range(sub // LANES):
            blk = (part * sub) // Q_BLOCK + 2 * c
            cols = slice(c * LANES, (c + 1) * LANES)
            for g in range(N_KV):
                pc = [qf[(g * GQA + r) * HEAD_DIM:(g * GQA + r + 1) * HEAD_DIM, cols]
                      for r in range(GQA)]
                lo01, hi01 = _swap_halves(pc[0], pc[1])
                lo23, hi23 = _swap_halves(pc[2], pc[3])
                qt_out[0, blk, g] = jnp.concatenate([lo01, lo23], axis=1).astype(BF16)
                qt_out[0, blk + 1, g] = jnp.concatenate([hi01, hi23], axis=1).astype(BF16)
            pc = [gf[r * SUBLANES:(r + 1) * SUBLANES, cols] for r in range(GQA)]
            lo01, hi01 = _swap_halves(pc[0], pc[1])
            lo23, hi23 = _swap_halves(pc[2], pc[3])
            gt_out[0, blk] = jnp.concatenate([lo01, lo23], axis=1)
            gt_out[0, blk + 1] = jnp.concatenate([hi01, hi23], axis=1)

        vs_f = _with_ones_rows(vg[:KV_W, :].astype(BF16))
        vw_f = _with_ones_rows(vg[KV_W:2 * KV_W, :].astype(BF16))
        for j in range(sub // KEY_CHUNK):
            vst_out[0, (part * sub) // KEY_CHUNK + j] = vs_f[:, j * KEY_CHUNK:(j + 1) * KEY_CHUNK]
        for j in range(sub // V_TILE):
            vwt_out[0, (part * sub) // V_TILE + j] = vw_f[:, j * V_TILE:(j + 1) * V_TILE]

        kc_out[0, rows, :] = kv[:, _N_KC:_N_VC]
        vc_out[0, rows, :] = kv[:, _N_VC:_N_KS]
        ks_out[0, rows, :] = head_norm(kv[:, _N_KS:_N_KW], kn_ref[...]).astype(BF16)
        kw_out[0, rows, :] = head_norm(kv[:, _N_KW:_N_CONV], kn_ref[...]).astype(BF16)

        z = cv[:, :conv_w] * cv[:, 2 * conv_w:]
        row = lax.broadcasted_iota(jnp.int32, z.shape, 0)
        z1 = jnp.where(row == 0, prev1, pltpu.roll(z, 1, 0))
        z2 = jnp.where(row == 0, prev2, jnp.where(row == 1, prev1, pltpu.roll(z, 2, 0)))
        zc = cw_ref[0:1, :] * z2 + cw_ref[1:2, :] * z1 + cw_ref[2:3, :] * z
        conv_out[0, rows, :] = (cv[:, conv_w:2 * conv_w] * zc).astype(BF16)
        prev2, prev1 = z[sub - 2:sub - 1, :], z[sub - 1:sub, :]
        if part == INPROJ_SPLIT - 1:
            carry[...] = z[sub - SUBLANES:sub, :]


def _inproj_call(x, mod, norm1, w_nat, w_tr, qn_col, kn_t, conv_w, bdq, bdk):
    b, t, d = x.shape
    tm = min(INPROJ_TILE, t)
    cw = conv_w.shape[1]
    nq = t // Q_BLOCK
    row_spec = lambda w: pl.BlockSpec((1, tm, w), lambda i, j: (i, j, 0))
    full = lambda shp: pl.BlockSpec(shp, lambda i, j: (0,) * len(shp))
    vt_spec = lambda w: pl.BlockSpec((1, tm // w, V_TILE_ROWS, w), lambda i, j: (i, j, 0, 0))
    vt_shape = lambda w: jax.ShapeDtypeStruct((b, t // w, V_TILE_ROWS, w), BF16)
    kv = lambda dt: jax.ShapeDtypeStruct((b, t, KV_W), dt)
    out_specs = [pl.BlockSpec((1, tm // Q_BLOCK, N_KV, HEAD_DIM, GQ), lambda i, j: (i, j, 0, 0, 0)),
                 row_spec(KV_W), row_spec(KV_W), row_spec(KV_W), vt_spec(KEY_CHUNK), row_spec(KV_W),
                 vt_spec(V_TILE),
                 pl.BlockSpec((1, tm // Q_BLOCK, SUBLANES, GQ), lambda i, j: (i, j, 0, 0)),
                 row_spec(cw)]
    out_shape = [jax.ShapeDtypeStruct((b, nq, N_KV, HEAD_DIM, GQ), BF16),
                 kv(F32), kv(F32), kv(BF16), vt_shape(KEY_CHUNK), kv(BF16), vt_shape(V_TILE),
                 jax.ShapeDtypeStruct((b, nq, SUBLANES, GQ), F32),
                 jax.ShapeDtypeStruct((b, t, cw), BF16)]
    return pl.pallas_call(
        functools.partial(_inproj_kernel, conv_w=cw),
        grid=(b, t // tm),
        in_specs=[row_spec(d),
                  pl.BlockSpec((1, N_MOD, d), lambda i, j: (i, 0, 0)),
                  full((1, d)), full(w_nat.shape), full(w_tr.shape), full((ATTN_W, 1)),
                  full((1, KV_W)), full((CONV_K, cw)), full((ATTN_W, ATTN_W)), full((KV_W, KV_W))],
        out_specs=out_specs,
        out_shape=out_shape,
        scratch_shapes=[pltpu.VMEM((SUBLANES, cw), F32)],
        compiler_params=_cparams(2),
        name="inproj",
    )(x, mod, norm1, w_nat, w_tr, qn_col, kn_t, conv_w, bdq, bdk)


def _compress_kernel(kx_ref, vx_ref, pek_ref, pev_ref, w1k_ref, w1v_ref, w2k_ref, w2vt_ref,
                     kn_ref, bdk_ref, kc_out, vct_out):
    def hidden(x_ref, pe_ref, w1_ref):
        n = x_ref.shape[1] // CMP_STRIDE
        u = jnp.zeros((n, w1_ref.shape[2]), F32)
        v = jnp.zeros((n, w1_ref.shape[2]), F32)
        for r in range(0, CMP_STRIDE, 2):
            tok = [x_ref[0, pl.ds(r + d, n, stride=CMP_STRIDE), :] for d in range(2)]
            cols = slice(r * KV_W, (r + 2) * KV_W)
            for a, acc in ((0, "u"), (1, "v")):
                lhs = jnp.concatenate([tok[d] + pe_ref[a:a + 1, (r + d) * KV_W:(r + d + 1) * KV_W]
                                       for d in range(2)], axis=1).astype(BF16)
                prod = jnp.dot(lhs, w1_ref[a, cols, :], preferred_element_type=F32)
                if acc == "u":
                    u = u + prod
                else:
                    v = v + prod
        hid = u + pltpu.roll(v, n - 1, 0)
        return jax.nn.gelu(hid, approximate=True).astype(BF16)

    kc = jnp.dot(hidden(kx_ref, pek_ref, w1k_ref), w2k_ref[...], preferred_element_type=F32)
    ssq = jnp.dot((kc * kc).astype(BF16), bdk_ref[...], preferred_element_type=F32)
    kc_out[0] = (kc * lax.rsqrt(ssq * (1.0 / HEAD_DIM) + EPS) * kn_ref[...]).astype(BF16)
    vct_out[0] = _with_ones_rows(lax.dot_general(w2vt_ref[...], hidden(vx_ref, pev_ref, w1v_ref), _NT,
                                                 preferred_element_type=F32).astype(BF16))


def _compress_call(kx, vx, pek, pev, w1k, w1v, w2k, w2vt, kn_t, bdk):
    b, t, _ = kx.shape
    nrow = t // CMP_STRIDE
    wide, hid2 = w1k.shape[1], w1k.shape[2]
    full = lambda shp: pl.BlockSpec(shp, lambda i: (0,) * len(shp))
    xs = pl.BlockSpec((1, t, KV_W), lambda i: (i, 0, 0))
    return pl.pallas_call(
        _compress_kernel,
        grid=(b,),
        in_specs=[xs, xs, full((2, wide)), full((2, wide)), full((2, wide, hid2)),
                  full((2, wide, hid2)), full((hid2, KV_W)), full((KV_W, hid2)),
                  full((1, KV_W)), full((KV_W, KV_W))],
        out_specs=[pl.BlockSpec((1, nrow, KV_W), lambda i: (i, 0, 0)),
                   pl.BlockSpec((1, V_TILE_ROWS, nrow), lambda i: (i, 0, 0))],
        out_shape=[jax.ShapeDtypeStruct((b, nrow, KV_W), BF16),
                   jax.ShapeDtypeStruct((b, V_TILE_ROWS, nrow), BF16)],
        compiler_params=_cparams(1),
        name="compress",
    )(kx, vx, pek, pev, w1k, w1v, w2k, w2vt, kn_t, bdk)


def _attn_kernel(q_ref, kc_ref, vct_ref, ks_ref, vst_ref, kw_ref, vwt_ref, gt_ref, ovt_ref,
                 eb_ref, sb_ref, wb_ref, far_ref, o_ref, sel_scr, acc_scr, out_scr, qp_scr, s_scr,
                 fin_scr, *, n_cmp_rows, eb_shift):
    n_sb = sel_scr.shape[1]
    ci = jnp.minimum(pl.program_id(1), n_sb - 1)
    par = lax.rem(ci, 2)
    n_total = ks_ref.shape[1] // KEY_CHUNK
    vsl = lambda g: slice(g * V_ROWS, (g + 1) * V_ROWS)

    @pl.when((pl.program_id(0) == 0) & (pl.program_id(1) == 0))
    def _():
        def zero(g, carry):
            fin_scr[g] = jnp.zeros(fin_scr.shape[1:], F32)
            return carry
        lax.fori_loop(0, N_KV, zero, 0)

    def chunk_qk(g, c):
        kk = ks_ref[0, pl.ds(pl.multiple_of(c * KEY_CHUNK, KEY_CHUNK), KEY_CHUNK), :]
        return jnp.dot(kk, qp_scr[g], preferred_element_type=F32).astype(BF16)

    n_chunks = ci // BLOCKS_PER_CHUNK + 1
    far_groups = jnp.maximum(ci - (NEAR_TILES - 1), 0) // (BLOCKS_PER_CHUNK * FAR_CHUNKS)
    n_wide = far_groups // 2
    n_far = far_groups - 2 * n_wide
    c_far = n_wide * (2 * FAR_CHUNKS)
    c0 = far_groups * FAR_CHUNKS
    n_near = (n_chunks - c0 + NEAR_CHUNKS - 1) // NEAR_CHUNKS

    def near_units(i):
        return [(g, jnp.minimum(c0 + NEAR_CHUNKS * i + j, n_total - 1))
                for g in range(N_KV) for j in range(NEAR_CHUNKS)]

    def far_units(i):
        return [(g, c_far + i * FAR_CHUNKS + j) for g in range(N_KV) for j in range(FAR_CHUNKS)]

    def wide_units(i):
        return [(g, i * 2 * FAR_CHUNKS + j) for g in range(N_KV) for j in range(2 * FAR_CHUNKS)]

    def pick_units(options):
        units = options[-1][1][:QK_AHEAD]
        for cond, cand in reversed(options[:-1]):
            units = [(jnp.where(cond, g_a, g_b), jnp.where(cond, c_a, c_b))
                     for (g_a, c_a), (g_b, c_b) in zip(cand[:QK_AHEAD], units)]
        return [(g, jnp.minimum(c, n_total - 1)) for g, c in units]

    def prefetch_scores(units):
        for k in range(QK_AHEAD):
            s_scr[k] = chunk_qk(*units[k])

    def before_loops(rows):
        pieces = []
        for g in range(N_KV):
            for half in range(GQ // LANES):
                a = fin_scr[g, :, half * LANES:(half + 1) * LANES]
                stacked = jnp.concatenate([a, pltpu.roll(a, Q_BLOCK, 1)], axis=0)
                pieces.append(stacked.T[:Q_BLOCK, :])
        o_ref[0] = jnp.concatenate(pieces, axis=1).astype(BF16)

        zeros_q = jnp.zeros((HEAD_DIM, GQ), BF16)
        qp_scr[0] = jnp.concatenate([q_ref[0, 0, 0], zeros_q], axis=0)
        qp_scr[1] = jnp.concatenate([zeros_q, q_ref[0, 0, 1]], axis=0)

        lane = lax.broadcasted_iota(jnp.int32, (n_sb, LANES), 1)
        cmp_c = n_cmp_rows - 4
        e0 = cmp_c - 4 * ci + jnp.where(par == 0, eb_shift[0], eb_shift[1])
        e0 = pl.multiple_of(e0, 8)
        o_cmp = []
        imp = []
        w0 = ci // 2 - (WIN_TILES - 1)
        win_tiles = [jnp.maximum(w0 + j, 0) for j in range(WIN_TILES)]
        s_cmp = [jnp.dot(kc_ref[0, :rows, :], qp_scr[g], preferred_element_type=F32)
                 for g in range(N_KV)]
        for g in range(N_KV):
            sc = s_cmp[g] + eb_ref[par, g, pl.ds(e0, rows), :]
            m = jnp.max(sc, axis=0, keepdims=True)
            e = jnp.exp2(sc - m).astype(BF16)
            ov = jnp.dot(vct_ref[0, vsl(g), :rows], e, preferred_element_type=F32)
            inv = jnp.where(m > 0.5 * NEG, 1.0 / ov[HEAD_DIM:HEAD_DIM + 1, :], 0.0)
            o_cmp.append(ov[:HEAD_DIM, :] * inv)
            ir = jnp.dot(ovt_ref[:, :rows], e, preferred_element_type=F32) * inv
            a = ir[:, :LANES] + ir[:, LANES:]
            imp.append(a + pltpu.roll(a, Q_BLOCK, 1))

        s_win = [[jnp.dot(kw_ref[0, pl.ds(pl.multiple_of(tj * V_TILE, V_TILE), V_TILE), :],
                          qp_scr[g], preferred_element_type=F32).astype(BF16) for tj in win_tiles]
                 for g in range(N_KV)]

        prefetch_scores(pick_units([(n_wide > 0, wide_units(0)), (n_far > 0, far_units(0)),
                                    (None, near_units(0))]))

        jidx = lax.broadcasted_iota(jnp.int32, (n_sb, LANES), 0)
        jf = jidx.astype(F32)
        valid = jidx <= ci
        forced = (jidx == 0) | (jidx == ci) | (jidx == ci - 1)
        score = jnp.where(valid, jnp.where(forced, -2.0, jnp.where(lane < Q_BLOCK, imp[0], imp[1])),
                          -1.0)
        sel = jnp.where(forced, 1.0, 0.0)

        def first_max(score):
            pairs = [(score[r:r + SUBLANES], jf[r:r + SUBLANES]) for r in range(0, n_sb, SUBLANES)]
            while len(pairs) > 1:
                nxt = []
                for (va, ia), (vb, ib) in zip(pairs[0::2], pairs[1::2]):
                    keep = va >= vb
                    nxt.append((jnp.where(keep, va, vb), jnp.where(keep, ia, ib)))
                pairs = nxt + pairs[len(pairs) - len(pairs) % 2:]
            v8, i8 = pairs[0]
            mx = jnp.max(v8, axis=0, keepdims=True)
            return jnp.min(jnp.where(v8 == mx, i8, float(n_sb)), axis=0, keepdims=True)

        for _ in range(min(N_SELECT, n_sb) - 3):
            first = first_max(score)
            hit = jf == first
            sel = jnp.where(hit, 1.0, sel)
            score = jnp.where(hit, -2.0, score)
        selneg = jnp.where((sel > 0.5) & valid, 0.0, NEG)
        swapped = pltpu.roll(selneg, Q_BLOCK, 1)
        left = jnp.where(lane < Q_BLOCK, selneg, swapped)
        right = jnp.where(lane < Q_BLOCK, swapped, selneg)
        sel_scr[0] = jnp.concatenate([left, left], axis=1)
        sel_scr[1] = jnp.concatenate([right, right], axis=1)

        for g in range(N_KV):
            parts = []
            for i in range(2 * WIN_TILES):
                delta = par + WIN_BLOCKS - i
                ok = (delta >= 0) & (delta <= WIN_BLOCKS) & (delta <= ci)
                tile = jnp.where(ok, delta, WIN_BLOCKS + 1)
                half = s_win[g][i // 2][(i % 2) * SEL_BLOCK:(i % 2 + 1) * SEL_BLOCK, :]
                parts.append(half + wb_ref[g, tile])
            s = jnp.concatenate(parts, axis=0)
            m = jnp.max(s, axis=0, keepdims=True)
            pb = jnp.exp2(s - m)
            o_win = jnp.zeros((V_ROWS, GQ), F32)
            for j, tj in enumerate(win_tiles):
                o_win = o_win + jnp.dot(vwt_ref[0, tj, vsl(g), :], pb[j * V_TILE:(j + 1) * V_TILE, :],
                                        preferred_element_type=F32)
            w_scale = (gt_ref[0, 0, 2 * N_KV + g:2 * N_KV + g + 1, :]
                       * (1.0 / o_win[HEAD_DIM:HEAD_DIM + 1, :]))
            out_scr[g] = gt_ref[0, 0, g:g + 1, :] * o_cmp[g] + w_scale * o_win[:HEAD_DIM, :]
            acc_scr[g] = jnp.zeros((V_ROWS, GQ), F32)

    row_steps = [r for r in range(CMP_ROW_STEP, n_cmp_rows + 1, CMP_ROW_STEP)] or [n_cmp_rows]
    variant = jnp.minimum((4 * ci + 2) // CMP_ROW_STEP, len(row_steps) - 1)
    lax.switch(variant, [functools.partial(before_loops, r) for r in row_steps])

    def chunk_softmax(s, g, c, near, extra):
        parts = []
        for i in range(BLOCKS_PER_CHUNK):
            kb = c * BLOCKS_PER_CHUNK + i
            blk = s[i * SEL_BLOCK:(i + 1) * SEL_BLOCK, :]
            mrow = sel_scr[g, pl.ds(kb, 1), :]
            if extra is not None:
                mrow = mrow + extra
            if near:
                tile = jnp.clip(ci - kb, 0, NEAR_TILES)
                blk = blk + sb_ref[g, tile] + mrow.astype(BF16)
            else:
                blk = blk + (mrow + far_ref[g]).astype(BF16)
            parts.append(blk)
        s = jnp.concatenate(parts, axis=0)
        m_c = jnp.max(s, axis=0, keepdims=True)
        return m_c.astype(F32), jnp.exp2(s - m_c)

    def chunk_pv(pb, g, c):
        return jnp.dot(vst_ref[0, c, vsl(g), :], pb, preferred_element_type=F32)

    def merge_step(units, extras, near, next_units, carry):
        scores = {k: s_scr[k] for k in range(QK_AHEAD)}
        results = []
        for k, (g, c) in enumerate(units):
            m_c, pb = chunk_softmax(scores.pop(k), g, c, near, extras[k])
            if k + QK_AHEAD < len(units):
                scores[k + QK_AHEAD] = chunk_qk(*units[k + QK_AHEAD])
            else:
                j = k + QK_AHEAD - len(units)
                s_scr[j] = chunk_qk(*next_units[j])
            results.append((m_c, chunk_pv(pb, g, c)))
        per_g = len(units) // N_KV
        new = []
        for g in range(N_KV):
            m_run = carry[g]
            stats = results[g * per_g:(g + 1) * per_g]
            m_new = m_run
            for m_c, _ in stats:
                m_new = jnp.maximum(m_new, m_c)
            acc = jnp.exp2(m_run - m_new) * acc_scr[g]
            for m_c, pv in stats:
                acc = acc + jnp.exp2(m_c - m_new) * pv
            acc_scr[g] = acc
            new.append(m_new)
        return tuple(new)

    def wide_step(i, carry):
        units = wide_units(i)
        nxt = pick_units([(i + 1 < n_wide, wide_units(i + 1)), (n_far > 0, far_units(0)),
                          (None, near_units(0))])
        return merge_step(units, [None] * len(units), False, nxt, carry)

    def far_step(i, carry):
        units = far_units(i)
        return merge_step(units, [None] * len(units), False, pick_units([(None, near_units(0))]), carry)

    def near_step(i, carry):
        units = near_units(i)
        extras = [None if j == 0 else jnp.where(c0 + NEAR_CHUNKS * i + j < n_chunks, 0.0, NEG)
                  for _ in range(N_KV) for j in range(NEAR_CHUNKS)]
        return merge_step(units, extras, True, near_units(i + 1), carry)

    init = (jnp.full((1, GQ), NEG, F32),) * N_KV
    carry = lax.fori_loop(0, n_wide, wide_step, init)
    carry = lax.fori_loop(0, n_far, far_step, carry)
    lax.fori_loop(0, n_near, near_step, carry)

    for g in range(N_KV):
        acc = acc_scr[g]
        scale = gt_ref[0, 0, N_KV + g:N_KV + g + 1, :] * (1.0 / acc[HEAD_DIM:HEAD_DIM + 1, :])
        fin_scr[g] = out_scr[g] + scale * acc[:HEAD_DIM, :]


def _attn_call(qt, kc, vct, ks, vst, kw, vwt, gt, ovt, ebank, sbank, wbank, far, eb_shift):
    b, nq = qt.shape[0], qt.shape[1]
    t = ks.shape[1]
    n_cmp_rows = kc.shape[1]
    n_sb = t // SEL_BLOCK
    per_b = lambda shp: pl.BlockSpec((1,) + shp[1:], lambda i, j: (i,) + (0,) * (len(shp) - 1))
    per_q = lambda shp: pl.BlockSpec(
        (1, 1) + shp[2:], lambda i, j: (i, jnp.minimum(j, nq - 1)) + (0,) * (len(shp) - 2))
    full = lambda shp: pl.BlockSpec(shp, lambda i, j: (0,) * len(shp))
    args = (qt, kc, vct, ks, vst, kw, vwt, gt, ovt, ebank, sbank, wbank, far)
    specs = [per_q(qt.shape), per_b(kc.shape), per_b(vct.shape), per_b(ks.shape), per_b(vst.shape),
             per_b(kw.shape), per_b(vwt.shape), per_q(gt.shape), full(ovt.shape),
             full(ebank.shape), full(sbank.shape), full(wbank.shape), full(far.shape)]
    acc_like = pltpu.VMEM((N_KV, HEAD_DIM, GQ), F32)
    return pl.pallas_call(
        functools.partial(_attn_kernel, n_cmp_rows=n_cmp_rows, eb_shift=eb_shift),
        grid=(b, nq + 1),
        in_specs=specs,
        out_specs=pl.BlockSpec((1, Q_BLOCK, ATTN_W), lambda i, j: (i, jnp.maximum(j - 1, 0), 0)),
        out_shape=jax.ShapeDtypeStruct((b, t, ATTN_W), BF16),
        scratch_shapes=[pltpu.VMEM((N_KV, n_sb, GQ), F32),
                        pltpu.VMEM((N_KV, V_ROWS, GQ), F32),
                        acc_like,
                        pltpu.VMEM((N_KV, KV_W, GQ), BF16),
                        pltpu.VMEM((QK_AHEAD, KEY_CHUNK, GQ), BF16),
                        acc_like],
        compiler_params=_cparams(2),
        name="nsa_attention",
    )(*args)


def _ffn_kernel(x_ref, a_ref, c_ref, mod_ref, n2_ref, wo_ref, w1_ref, w2_ref, o_ref):
    aw = a_ref.shape[2]
    mix = jnp.dot(a_ref[0], wo_ref[0:aw, :], preferred_element_type=F32)
    mix = mix + jnp.dot(c_ref[0], wo_ref[aw:, :], preferred_element_type=F32)
    x1 = x_ref[0] + mod_ref[0, 2:3, :] * mix
    ms = jnp.mean(x1 * x1, axis=-1, keepdims=True)
    y = x1 * lax.rsqrt(ms + EPS) * n2_ref[...]
    h2 = (y * (1.0 + mod_ref[0, 4:5, :]) + mod_ref[0, 3:4, :]).astype(BF16)
    d_ff = w1_ref.shape[1]
    ff = jnp.zeros(x1.shape, F32)
    for j in range(d_ff // FF_CHUNK):
        a = jnp.dot(h2, w1_ref[:, j * FF_CHUNK:(j + 1) * FF_CHUNK], preferred_element_type=F32)
        a = jnp.maximum(a, 0.0)
        ff = ff + jnp.dot((a * a).astype(BF16), w2_ref[j * FF_CHUNK:(j + 1) * FF_CHUNK, :],
                          preferred_element_type=F32)
    o_ref[0] = x1 + mod_ref[0, 5:6, :] * ff


def _ffn_call(x, attn, conv, mod, norm2, w_out, w_ff1, w_ff2):
    b, t, d = x.shape
    tm = min(ROW_TILE, t)
    row_spec = lambda w: pl.BlockSpec((1, tm, w), lambda i, j: (i, j, 0))
    full = lambda shp: pl.BlockSpec(shp, lambda i, j: (0,) * len(shp),
                                    pipeline_mode=pl.Buffered(1))
    return pl.pallas_call(
        _ffn_kernel,
        grid=(b, t // tm),
        in_specs=[row_spec(d), row_spec(attn.shape[2]), row_spec(conv.shape[2]),
                  pl.BlockSpec((1, N_MOD, d), lambda i, j: (i, 0, 0)),
                  full((1, d)), full(w_out.shape), full(w_ff1.shape), full(w_ff2.shape)],
        out_specs=row_spec(d),
        out_shape=jax.ShapeDtypeStruct((b, t, d), F32),
        compiler_params=_cparams(2),
        name="outproj_mlp",
    )(x, attn, conv, mod, norm2, w_out, w_ff1, w_ff2)


def _block_diag_ones(n):
    idx = np.arange(n) // HEAD_DIM
    return jnp.asarray(idx[:, None] == idx[None, :], dtype=BF16)


def _pack_w_in(w_in):
    d = w_in.shape[0]
    conv_w = d - ATTN_W
    sizes = [ATTN_W] + [KV_W] * 6 + [N_BRANCH * N_HEADS] + [conv_w] * 3
    offs = np.concatenate([[0], np.cumsum(sizes)])
    part = lambda i: w_in[:, offs[i]:offs[i + 1]]
    q, kc, vc, ks, vs, kw, vw, g, cgate, bgate, u = (part(i) for i in range(11))
    w_nat = jnp.concatenate([kc, vc, ks, kw, cgate, bgate, u], axis=1).astype(BF16)
    gt = g.reshape(d, N_KV, GQA, N_BRANCH).transpose(2, 3, 1, 0).reshape(GQA, N_BRANCH * N_KV, d)
    gt = jnp.pad(gt, ((0, 0), (0, SUBLANES - N_BRANCH * N_KV), (0, 0))).reshape(_G_ROWS, d)
    w_tr = jnp.concatenate([q.T, vs.T, vw.T, gt], axis=0).astype(BF16)
    return w_nat, w_tr


def _expand_w1(w1):
    hid = w1.shape[1]
    w = w1.reshape(2, CMP_STRIDE, HEAD_DIM, hid).astype(BF16)
    zero = jnp.zeros_like(w)
    per_group = [jnp.concatenate([w if k == g else zero for k in range(N_KV)], axis=-1)
                 for g in range(N_KV)]
    return jnp.stack(per_group, axis=2).reshape(2, CMP_STRIDE * KV_W, N_KV * hid)


def _expand_w2(w2):
    hid = w2.shape[0]
    eye = jnp.eye(N_KV, dtype=w2.dtype).reshape(N_KV, 1, N_KV, 1)
    return (w2.reshape(1, hid, 1, HEAD_DIM) * eye).reshape(N_KV * hid, KV_W).astype(BF16)


def _expand_pe(pe):
    p = pe.reshape(2, CMP_STRIDE, 1, HEAD_DIM)
    return jnp.broadcast_to(p, (2, CMP_STRIDE, N_KV, HEAD_DIM)).reshape(2, CMP_STRIDE * KV_W)


def _bucket_thresholds():
    n = np.arange(2 * REL_MAX_DIST)
    max_exact = REL_BUCKETS // 2
    nf = np.maximum(n, max_exact).astype(np.float32)
    ratio = np.log(nf / np.float32(max_exact)) / np.float32(math.log(REL_MAX_DIST / max_exact))
    large = max_exact + (ratio * np.float32(REL_BUCKETS - max_exact)).astype(np.int32)
    table = np.where(n < max_exact, n, np.minimum(large, REL_BUCKETS - 1))
    return tuple(int(np.searchsorted(table, k, side="left")) for k in range(REL_BUCKETS))


def _bank_call(bias_rows, lead, n_tiles, dist_fn, name):
    nl = len(lead)
    thr = _bucket_thresholds()

    def body(rows_ref, o_ref):
        lead_ids = [pl.program_id(a) for a in range(nl)]
        row = lax.broadcasted_iota(jnp.int32, (SEL_BLOCK, GQ), 0)
        qi = lax.broadcasted_iota(jnp.int32, (SEL_BLOCK, GQ), 1) & (Q_BLOCK - 1)

        def tile(t, carry):
            dist, ok = dist_fn(lead_ids, t, row, qi)
            v = jnp.broadcast_to(rows_ref[0, 0:1, :], (SEL_BLOCK, GQ))
            for k in range(1, REL_BUCKETS):
                v = jnp.where(dist >= thr[k], rows_ref[0, k:k + 1, :], v)
            o_ref[(0,) * (nl + 1) + (t,)] = jnp.where(ok, v, NEG)
            return carry

        lax.fori_loop(0, n_tiles, tile, 0)

    return pl.pallas_call(
        body,
        grid=tuple(lead) + (N_KV,),
        in_specs=[pl.BlockSpec((1, REL_BUCKETS, GQ), lambda *i: (i[nl], 0, 0))],
        out_specs=pl.BlockSpec((1,) * (nl + 1) + (n_tiles, SEL_BLOCK, GQ),
                               lambda *i: tuple(i) + (0, 0, 0)),
        out_shape=jax.ShapeDtypeStruct(tuple(lead) + (N_KV, n_tiles, SEL_BLOCK, GQ), F32),
        compiler_params=_cparams(nl + 1),
        name=name,
    )(bias_rows)


def _bias_banks(rel_bias, t):
    n_cmp_rows = t // CMP_STRIDE
    rows = rel_bias.reshape(REL_BUCKETS, N_KV, GQA).transpose(1, 0, 2)
    rows = jnp.repeat(rows, Q_BLOCK, axis=2) * LOG2E

    def sel_dist(lead, tile, row, qi):
        dist = SEL_BLOCK * tile + qi - row
        return dist, dist >= 0

    sbank = _bank_call(rows, (), NEAR_TILES + 1, sel_dist, "bias_bank_sel")
    far = sbank[:, NEAR_TILES, 0:1, :]

    def win_dist(lead, tile, row, qi):
        dist = SEL_BLOCK * tile + qi - row
        return dist, (dist >= 0) & (dist < WINDOW)

    wbank = _bank_call(rows, (), WIN_BLOCKS + 2, win_dist, "bias_bank_win")

    cmp_c = n_cmp_rows - 4
    shifts = tuple(int((-(cmp_c - 4 * p)) % 8) for p in range(2))
    n_tiles = (cmp_c + n_cmp_rows + 8 + SEL_BLOCK - 1) // SEL_BLOCK

    def cmp_dist(lead, tile, row, qi):
        e = SEL_BLOCK * tile + row - jnp.where(lead[0] == 0, shifts[0], shifts[1])
        dist = qi - CMP_STRIDE * e + (CMP_STRIDE * cmp_c - (CMP_BLOCK - 1))
        return dist, (dist >= 0) & (e >= 0)

    ebank = _bank_call(rows, (2,), n_tiles, cmp_dist, "bias_bank_cmp")
    ebank = ebank.reshape(2, N_KV, n_tiles * SEL_BLOCK, GQ)
    return ebank, sbank, wbank, far, shifts


def _overlap_t(t):
    n_cmp_rows = t // CMP_STRIDE
    n_sb = t // SEL_BLOCK
    c_start = np.arange(n_cmp_rows)[None, :] * CMP_STRIDE
    s_start = np.arange(n_sb)[:, None] * SEL_BLOCK
    ov = np.clip(np.minimum(c_start + CMP_BLOCK, s_start + SEL_BLOCK)
                 - np.maximum(c_start, s_start), 0, None) / CMP_BLOCK
    ov[:, n_cmp_rows - 1] = 0.0
    return jnp.asarray(ov, dtype=BF16)


def _layer(x, c_pad, w_in, q_norm, k_norm, cmp_pe_k, cmp_w1_k, cmp_w2_k, cmp_pe_v, cmp_w1_v,
           cmp_w2_v, rel_bias, conv_w, w_out, norm1, norm2, w_ada, b_ada, w_ff1, w_ff2):
    b, t, d = x.shape
    scale = HEAD_DIM ** -0.5

    mod = _mod_call(c_pad, w_ada, b_ada)[:b].reshape(b, N_MOD, d)

    qn_col = (jnp.tile(q_norm, N_HEADS) * (scale * LOG2E)).reshape(ATTN_W, 1)
    kn_t = jnp.tile(k_norm, N_KV).reshape(1, KV_W)
    bdq = _block_diag_ones(ATTN_W)
    bdk = _block_diag_ones(KV_W)
    w_nat, w_tr = _pack_w_in(w_in)
    qt, kc_raw, vc_raw, ks, vst, kw, vwt, gt, conv = _inproj_call(
        x, mod, norm1.reshape(1, d), w_nat, w_tr, qn_col, kn_t, conv_w, bdq, bdk)

    n_cmp_rows = t // CMP_STRIDE
    kc, vct = _compress_call(
        kc_raw, vc_raw, _expand_pe(cmp_pe_k), _expand_pe(cmp_pe_v), _expand_w1(cmp_w1_k), _expand_w1(cmp_w1_v),
        _expand_w2(cmp_w2_k), _expand_w2(cmp_w2_v).T, kn_t, bdk)

    ebank, sbank, wbank, far, eb_shift = _bias_banks(rel_bias, t)
    attn = _attn_call(qt, kc, vct, ks, vst, kw, vwt, gt, _overlap_t(t), ebank, sbank.astype(BF16),
                      wbank.astype(BF16), far, eb_shift)

    return _ffn_call(x, attn, conv, mod, norm2.reshape(1, d), w_out.astype(BF16),
                     w_ff1.astype(BF16), w_ff2.astype(BF16))


def kernel(x, c, w_in, q_norm, k_norm, cmp_pe_k, cmp_w1_k, cmp_w2_k, cmp_pe_v, cmp_w1_v, cmp_w2_v,
           rel_bias, conv_w, w_out, norm1, norm2, w_ada, b_ada, w_ff1, w_ff2):
    b = x.shape[0]
    c_pad = jnp.pad(c, ((0, (-b) % 8), (0, 0)))
    for l in range(w_in.shape[0]):
        x = _layer(x, c_pad, w_in[l], q_norm[l], k_norm[l], cmp_pe_k[l], cmp_w1_k[l], cmp_w2_k[l],
                   cmp_pe_v[l], cmp_w1_v[l], cmp_w2_v[l], rel_bias, conv_w[l], w_out[l],
                   norm1[l], norm2[l], w_ada[l], b_ada[l], w_ff1[l], w_ff2[l])
    return x
```

```python
import functools
import math

import numpy as np
import jax
import jax.numpy as jnp
from jax import lax
from jax.experimental import pallas as pl
from jax.experimental.pallas import tpu as pltpu

HEAD_DIM = 64
N_HEADS = 8
N_KV = 2
GQA = N_HEADS // N_KV
ATTN_W = N_HEADS * HEAD_DIM
KV_W = N_KV * HEAD_DIM
CONV_K = 3
CMP_BLOCK = 32
CMP_STRIDE = 16
CMP_HIDDEN = 256
SEL_BLOCK = 64
N_SELECT = 16
WINDOW = 512
Q_BLOCK = 64
REL_BUCKETS = 32
REL_MAX_DIST = 1024
N_MOD = 6
N_BRANCH = 3
EPS = 1e-6
NEG = -1e30

LANES = 128
SUBLANES = 8
GQ = GQA * Q_BLOCK
KEY_CHUNK = 256
BLOCKS_PER_CHUNK = KEY_CHUNK // SEL_BLOCK
V_TILE = 128
BF16_ROWS = 16
V_ROWS = HEAD_DIM + BF16_ROWS
V_TILE_ROWS = N_KV * V_ROWS
CMP_ROW_STEP = 128
WIN_BLOCKS = WINDOW // SEL_BLOCK
WIN_TILES = WINDOW // V_TILE + 1
NEAR_TILES = (REL_MAX_DIST + Q_BLOCK - 1) // SEL_BLOCK + 1
ROW_TILE = 512
INPROJ_TILE = 1024
INPROJ_SPLIT = 4
FAR_CHUNKS = 4
NEAR_CHUNKS = 4
QK_AHEAD = 4
LOG2E = math.log2(math.e)
FF_CHUNK = 1024
VMEM_LIMIT = 56 * 1024 * 1024

F32 = jnp.float32
BF16 = jnp.bfloat16
_NT = (((1,), (1,)), ((), ()))


def _cparams(n_axes):
    return pltpu.CompilerParams(dimension_semantics=("arbitrary",) * n_axes,
                                vmem_limit_bytes=VMEM_LIMIT)


def _with_ones_rows(vt):
    ones = jnp.ones((BF16_ROWS, vt.shape[1]), vt.dtype)
    parts = []
    for g in range(N_KV):
        parts += [vt[g * HEAD_DIM:(g + 1) * HEAD_DIM, :], ones]
    return jnp.concatenate(parts, axis=0)


def _swap_halves(p0, p1):
    low = lax.broadcasted_iota(jnp.int32, p0.shape, 1) < LANES // 2
    return (jnp.where(low, p0, pltpu.roll(p1, LANES // 2, 1)),
            jnp.where(low, pltpu.roll(p0, LANES // 2, 1), p1))


def _mod_kernel(c_ref, w_ref, b_ref, o_ref):
    c = c_ref[...]
    a = c * jax.nn.sigmoid(c)
    o_ref[...] = jnp.dot(a, w_ref[...], preferred_element_type=F32,
                         precision=lax.Precision.HIGHEST) + b_ref[...]


def _mod_call(c_pad, w_ada, b_ada):
    rows, d = c_pad.shape
    n = w_ada.shape[1]
    tn = 1024
    return pl.pallas_call(
        _mod_kernel,
        grid=(n // tn,),
        in_specs=[pl.BlockSpec((rows, d), lambda j: (0, 0)),
                  pl.BlockSpec((d, tn), lambda j: (0, j)),
                  pl.BlockSpec((1, tn), lambda j: (0, j))],
        out_specs=pl.BlockSpec((rows, tn), lambda j: (0, j)),
        out_shape=jax.ShapeDtypeStruct((rows, n), F32),
        compiler_params=_cparams(1),
        name="adaln_mod",
    )(c_pad, w_ada, b_ada.reshape(1, n))


_N_KC, _N_VC, _N_KS, _N_KW, _N_CONV = 0, KV_W, 2 * KV_W, 3 * KV_W, 4 * KV_W
_T_Q, _T_VS, _T_VW, _T_G = 0, ATTN_W, ATTN_W + KV_W, ATTN_W + 2 * KV_W
_G_ROWS = GQA * SUBLANES


def _inproj_kernel(x_ref, mod_ref, n1_ref, wn_ref, wt_ref, qn_ref, kn_ref, cw_ref, bdq_ref, bdk_ref,
                   qt_out, kc_out, vc_out, ks_out, vst_out, kw_out, vwt_out, gt_out, conv_out,
                   carry, *, conv_w):
    t = pl.program_id(1)
    tm = x_ref.shape[1]
    sub = tm // INPROJ_SPLIT

    @pl.when(t == 0)
    def _():
        carry[...] = jnp.zeros_like(carry)

    prev2, prev1 = carry[6:7, :], carry[7:8, :]
    c_bg = _N_CONV + conv_w
    c_u = c_bg + conv_w

    for part in range(INPROJ_SPLIT):
        rows = slice(part * sub, (part + 1) * sub)
        x = x_ref[0, rows, :]
        ms = jnp.mean(x * x, axis=-1, keepdims=True)
        y = x * lax.rsqrt(ms + EPS) * n1_ref[...]
        h = (y * (1.0 + mod_ref[0, 1:2, :]) + mod_ref[0, 0:1, :]).astype(BF16)

        def proj(a, b):
            return jnp.dot(h, wn_ref[:, a:b], preferred_element_type=F32)

        def proj_t(a, b):
            return lax.dot_general(wt_ref[a:b, :], h, _NT, preferred_element_type=F32)

        def head_norm(v, gain):
            ssq = jnp.dot((v * v).astype(BF16), bdk_ref[...], preferred_element_type=F32)
            return v * lax.rsqrt(ssq * (1.0 / HEAD_DIM) + EPS) * gain

        qf = proj_t(_T_Q, _T_VS)
        vg = proj_t(_T_VS, _T_G + _G_ROWS)
        kv = proj(_N_KC, _N_CONV)
        cv = proj(_N_CONV, c_u + conv_w)

        ssq = jnp.dot(bdq_ref[...], (qf * qf).astype(BF16), preferred_element_type=F32)
        qf = qf * lax.rsqrt(ssq * (1.0 / HEAD_DIM) + EPS) * qn_ref[...]
        gf = jax.nn.sigmoid(vg[2 * KV_W:, :])
        for c in range(sub // LANES):
            blk = (part * sub) // Q_BLOCK + 2 * c
            cols = slice(c * LANES, (c + 1) * LANES)
            for g in range(N_KV):
                pc = [qf[(g * GQA + r) * HEAD_DIM:(g * GQA + r + 1) * HEAD_DIM, cols]
                      for r in range(GQA)]
                lo01, hi01 = _swap_halves(pc[0], pc[1])
                lo23, hi23 = _swap_halves(pc[2], pc[3])
                qt_out[0, blk, g] = jnp.concatenate([lo01, lo23], axis=1).astype(BF16)
                qt_out[0, blk + 1, g] = jnp.concatenate([hi01, hi23], axis=1).astype(BF16)
            pc = [gf[r * SUBLANES:(r + 1) * SUBLANES, cols] for r in range(GQA)]
            lo01, hi01 = _swap_halves(pc[0], pc[1])
            lo23, hi23 = _swap_halves(pc[2], pc[3])
            gt_out[0, blk] = jnp.concatenate([lo01, lo23], axis=1)
            gt_out[0, blk + 1] = jnp.concatenate([hi01, hi23], axis=1)

        vs_f = _with_ones_rows(vg[:KV_W, :].astype(BF16))
        vw_f = _with_ones_rows(vg[KV_W:2 * KV_W, :].astype(BF16))
        for j in range(sub // KEY_CHUNK):
            vst_out[0, (part * sub) // KEY_CHUNK + j] = vs_f[:, j * KEY_CHUNK:(j + 1) * KEY_CHUNK]
        for j in range(sub // V_TILE):
            vwt_out[0, (part * sub) // V_TILE + j] = vw_f[:, j * V_TILE:(j + 1) * V_TILE]

        kc_out[0, rows, :] = kv[:, _N_KC:_N_VC]
        vc_out[0, rows, :] = kv[:, _N_VC:_N_KS]
        ks_out[0, rows, :] = head_norm(kv[:, _N_KS:_N_KW], kn_ref[...]).astype(BF16)
        kw_out[0, rows, :] = head_norm(kv[:, _N_KW:_N_CONV], kn_ref[...]).astype(BF16)

        z = cv[:, :conv_w] * cv[:, 2 * conv_w:]
        row = lax.broadcasted_iota(jnp.int32, z.shape, 0)
        z1 = jnp.where(row == 0, prev1, pltpu.roll(z, 1, 0))
        z2 = jnp.where(row == 0, prev2, jnp.where(row == 1, prev1, pltpu.roll(z, 2, 0)))
        zc = cw_ref[0:1, :] * z2 + cw_ref[1:2, :] * z1 + cw_ref[2:3, :] * z
        conv_out[0, rows, :] = (cv[:, conv_w:2 * conv_w] * zc).astype(BF16)
        prev2, prev1 = z[sub - 2:sub - 1, :], z[sub - 1:sub, :]
        if part == INPROJ_SPLIT - 1:
            carry[...] = z[sub - SUBLANES:sub, :]


def _inproj_call(x, mod, norm1, w_nat, w_tr, qn_col, kn_t, conv_w, bdq, bdk):
    b, t, d = x.shape
    tm = min(INPROJ_TILE, t)
    cw = conv_w.shape[1]
    nq = t // Q_BLOCK
    row_spec = lambda w: pl.BlockSpec((1, tm, w), lambda i, j: (i, j, 0))
    full = lambda shp: pl.BlockSpec(shp, lambda i, j: (0,) * len(shp))
    vt_spec = lambda w: pl.BlockSpec((1, tm // w, V_TILE_ROWS, w), lambda i, j: (i, j, 0, 0))
    vt_shape = lambda w: jax.ShapeDtypeStruct((b, t // w, V_TILE_ROWS, w), BF16)
    kv = lambda dt: jax.ShapeDtypeStruct((b, t, KV_W), dt)
    out_specs = [pl.BlockSpec((1, tm // Q_BLOCK, N_KV, HEAD_DIM, GQ), lambda i, j: (i, j, 0, 0, 0)),
                 row_spec(KV_W), row_spec(KV_W), row_spec(KV_W), vt_spec(KEY_CHUNK), row_spec(KV_W),
                 vt_spec(V_TILE),
                 pl.BlockSpec((1, tm // Q_BLOCK, SUBLANES, GQ), lambda i, j: (i, j, 0, 0)),
                 row_spec(cw)]
    out_shape = [jax.ShapeDtypeStruct((b, nq, N_KV, HEAD_DIM, GQ), BF16),
                 kv(F32), kv(F32), kv(BF16), vt_shape(KEY_CHUNK), kv(BF16), vt_shape(V_TILE),
                 jax.ShapeDtypeStruct((b, nq, SUBLANES, GQ), F32),
                 jax.ShapeDtypeStruct((b, t, cw), BF16)]
    return pl.pallas_call(
        functools.partial(_inproj_kernel, conv_w=cw),
        grid=(b, t // tm),
        in_specs=[row_spec(d),
                  pl.BlockSpec((1, N_MOD, d), lambda i, j: (i, 0, 0)),
                  full((1, d)), full(w_nat.shape), full(w_tr.shape), full((ATTN_W, 1)),
                  full((1, KV_W)), full((CONV_K, cw)), full((ATTN_W, ATTN_W)), full((KV_W, KV_W))],
        out_specs=out_specs,
        out_shape=out_shape,
        scratch_shapes=[pltpu.VMEM((SUBLANES, cw), F32)],
        compiler_params=_cparams(2),
        name="inproj",
    )(x, mod, norm1, w_nat, w_tr, qn_col, kn_t, conv_w, bdq, bdk)


def _compress_kernel(kx_ref, vx_ref, pek_ref, pev_ref, w1k_ref, w1v_ref, w2k_ref, w2vt_ref,
                     kn_ref, bdk_ref, kc_out, vct_out):
    def hidden(x_ref, pe_ref, w1_ref):
        n = x_ref.shape[1] // CMP_STRIDE
        u = jnp.zeros((n, w1_ref.shape[2]), F32)
        v = jnp.zeros((n, w1_ref.shape[2]), F32)
        for r in range(0, CMP_STRIDE, 2):
            tok = [x_ref[0, pl.ds(r + d, n, stride=CMP_STRIDE), :] for d in range(2)]
            cols = slice(r * KV_W, (r + 2) * KV_W)
            for a, acc in ((0, "u"), (1, "v")):
                lhs = jnp.concatenate([tok[d] + pe_ref[a:a + 1, (r + d) * KV_W:(r + d + 1) * KV_W]
                                       for d in range(2)], axis=1).astype(BF16)
                prod = jnp.dot(lhs, w1_ref[a, cols, :], preferred_element_type=F32)
                if acc == "u":
                    u = u + prod
                else:
                    v = v + prod
        hid = u + pltpu.roll(v, n - 1, 0)
        return jax.nn.gelu(hid, approximate=True).astype(BF16)

    kc = jnp.dot(hidden(kx_ref, pek_ref, w1k_ref), w2k_ref[...], preferred_element_type=F32)
    ssq = jnp.dot((kc * kc).astype(BF16), bdk_ref[...], preferred_element_type=F32)
    kc_out[0] = (kc * lax.rsqrt(ssq * (1.0 / HEAD_DIM) + EPS) * kn_ref[...]).astype(BF16)
    vct_out[0] = _with_ones_rows(lax.dot_general(w2vt_ref[...], hidden(vx_ref, pev_ref, w1v_ref), _NT,
                                                 preferred_element_type=F32).astype(BF16))


def _compress_call(kx, vx, pek, pev, w1k, w1v, w2k, w2vt, kn_t, bdk):
    b, t, _ = kx.shape
    nrow = t // CMP_STRIDE
    wide, hid2 = w1k.shape[1], w1k.shape[2]
    full = lambda shp: pl.BlockSpec(shp, lambda i: (0,) * len(shp))
    xs = pl.BlockSpec((1, t, KV_W), lambda i: (i, 0, 0))
    return pl.pallas_call(
        _compress_kernel,
        grid=(b,),
        in_specs=[xs, xs, full((2, wide)), full((2, wide)), full((2, wide, hid2)),
                  full((2, wide, hid2)), full((hid2, KV_W)), full((KV_W, hid2)),
                  full((1, KV_W)), full((KV_W, KV_W))],
        out_specs=[pl.BlockSpec((1, nrow, KV_W), lambda i: (i, 0, 0)),
                   pl.BlockSpec((1, V_TILE_ROWS, nrow), lambda i: (i, 0, 0))],
        out_shape=[jax.ShapeDtypeStruct((b, nrow, KV_W), BF16),
                   jax.ShapeDtypeStruct((b, V_TILE_ROWS, nrow), BF16)],
        compiler_params=_cparams(1),
        name="compress",
    )(kx, vx, pek, pev, w1k, w1v, w2k, w2vt, kn_t, bdk)


def _attn_kernel(q_ref, kc_ref, vct_ref, ks_ref, vst_ref, kw_ref, vwt_ref, gt_ref, ovt_ref,
                 eb_ref, sb_ref, wb_ref, far_ref, o_ref, sel_scr, acc_scr, out_scr, qp_scr, s_scr,
                 fin_scr, *, n_cmp_rows, eb_shift):
    n_sb = sel_scr.shape[1]
    ci = jnp.minimum(pl.program_id(1), n_sb - 1)
    par = lax.rem(ci, 2)
    n_total = ks_ref.shape[1] // KEY_CHUNK
    vsl = lambda g: slice(g * V_ROWS, (g + 1) * V_ROWS)

    @pl.when((pl.program_id(0) == 0) & (pl.program_id(1) == 0))
    def _():
        def zero(g, carry):
            fin_scr[g] = jnp.zeros(fin_scr.shape[1:], F32)
            sel_scr[g] = jnp.zeros(sel_scr.shape[1:], F32)
            return carry
        lax.fori_loop(0, N_KV, zero, 0)

    def chunk_qk(g, c):
        kk = ks_ref[0, pl.ds(pl.multiple_of(c * KEY_CHUNK, KEY_CHUNK), KEY_CHUNK), :]
        return jnp.dot(kk, qp_scr[g], preferred_element_type=F32).astype(BF16)

    n_chunks = ci // BLOCKS_PER_CHUNK + 1
    far_groups = jnp.maximum(ci - (NEAR_TILES - 1), 0) // (BLOCKS_PER_CHUNK * FAR_CHUNKS)
    n_wide = far_groups // 2
    n_far = far_groups - 2 * n_wide
    c_far = n_wide * (2 * FAR_CHUNKS)
    c0 = far_groups * FAR_CHUNKS
    n_near = (n_chunks - c0 + NEAR_CHUNKS - 1) // NEAR_CHUNKS

    def near_units(i):
        return [(g, jnp.minimum(c0 + NEAR_CHUNKS * i + j, n_total - 1))
                for g in range(N_KV) for j in range(NEAR_CHUNKS)]

    def far_units(i):
        return [(g, c_far + i * FAR_CHUNKS + j) for g in range(N_KV) for j in range(FAR_CHUNKS)]

    def wide_units(i):
        return [(g, i * 2 * FAR_CHUNKS + j) for g in range(N_KV) for j in range(2 * FAR_CHUNKS)]

    def pick_units(options):
        units = options[-1][1][:QK_AHEAD]
        for cond, cand in reversed(options[:-1]):
            units = [(jnp.where(cond, g_a, g_b), jnp.where(cond, c_a, c_b))
                     for (g_a, c_a), (g_b, c_b) in zip(cand[:QK_AHEAD], units)]
        return [(g, jnp.minimum(c, n_total - 1)) for g, c in units]

    def prefetch_scores(units):
        for k in range(QK_AHEAD):
            s_scr[k] = chunk_qk(*units[k])

    def before_loops(rows):
        pieces = []
        for g in range(N_KV):
            for half in range(GQ // LANES):
                a = fin_scr[g, :, half * LANES:(half + 1) * LANES]
                stacked = jnp.concatenate([a, pltpu.roll(a, Q_BLOCK, 1)], axis=0)
                pieces.append(stacked.T[:Q_BLOCK, :])
        o_ref[0] = jnp.concatenate(pieces, axis=1).astype(BF16)

        zeros_q = jnp.zeros((HEAD_DIM, GQ), BF16)
        qp_scr[0] = jnp.concatenate([q_ref[0, 0, 0], zeros_q], axis=0)
        qp_scr[1] = jnp.concatenate([zeros_q, q_ref[0, 0, 1]], axis=0)

        nsel = min(n_sb, rows * CMP_STRIDE // SEL_BLOCK)
        lane = lax.broadcasted_iota(jnp.int32, (nsel, LANES), 1)
        cmp_c = n_cmp_rows - 4
        e0 = cmp_c - 4 * ci + jnp.where(par == 0, eb_shift[0], eb_shift[1])
        e0 = pl.multiple_of(e0, 8)
        o_cmp = []
        imp = []
        w0 = ci // 2 - (WIN_TILES - 1)
        win_tiles = [jnp.maximum(w0 + j, 0) for j in range(WIN_TILES)]
        s_cmp = [jnp.dot(kc_ref[0, :rows, :], qp_scr[g], preferred_element_type=F32)
                 for g in range(N_KV)]
        for g in range(N_KV):
            sc = s_cmp[g] + eb_ref[par, g, pl.ds(e0, rows), :]
            m = jnp.max(sc, axis=0, keepdims=True)
            e = jnp.exp2(sc - m).astype(BF16)
            ov = jnp.dot(vct_ref[0, vsl(g), :rows], e, preferred_element_type=F32)
            inv = jnp.where(m > 0.5 * NEG, 1.0 / ov[HEAD_DIM:HEAD_DIM + 1, :], 0.0)
            o_cmp.append(ov[:HEAD_DIM, :] * inv)
            ir = jnp.dot(ovt_ref[:nsel, :rows], e, preferred_element_type=F32) * inv
            a = ir[:, :LANES] + ir[:, LANES:]
            imp.append(a + pltpu.roll(a, Q_BLOCK, 1))

        s_win = [[jnp.dot(kw_ref[0, pl.ds(pl.multiple_of(tj * V_TILE, V_TILE), V_TILE), :],
                          qp_scr[g], preferred_element_type=F32).astype(BF16) for tj in win_tiles]
                 for g in range(N_KV)]

        prefetch_scores(pick_units([(n_wide > 0, wide_units(0)), (n_far > 0, far_units(0)),
                                    (None, near_units(0))]))

        jidx = lax.broadcasted_iota(jnp.int32, (nsel, LANES), 0)
        jf = jidx.astype(F32)
        valid = jidx <= ci
        forced = (jidx == 0) | (jidx == ci) | (jidx == ci - 1)
        score = jnp.where(valid, jnp.where(forced, -2.0, jnp.where(lane < Q_BLOCK, imp[0], imp[1])),
                          -1.0)
        sel = jnp.where(forced, 1.0, 0.0)

        def first_max(score):
            pairs = [(score[r:r + SUBLANES], jf[r:r + SUBLANES]) for r in range(0, nsel, SUBLANES)]
            while len(pairs) > 1:
                nxt = []
                for (va, ia), (vb, ib) in zip(pairs[0::2], pairs[1::2]):
                    keep = va >= vb
                    nxt.append((jnp.where(keep, va, vb), jnp.where(keep, ia, ib)))
                pairs = nxt + pairs[len(pairs) - len(pairs) % 2:]
            v8, i8 = pairs[0]
            mx = jnp.max(v8, axis=0, keepdims=True)
            return jnp.min(jnp.where(v8 == mx, i8, float(nsel)), axis=0, keepdims=True)

        for _ in range(min(N_SELECT, nsel) - 3):
            first = first_max(score)
            hit = jf == first
            sel = jnp.where(hit, 1.0, sel)
            score = jnp.where(hit, -2.0, score)
        selneg = jnp.where((sel > 0.5) & valid, 0.0, NEG)
        swapped = pltpu.roll(selneg, Q_BLOCK, 1)
        left = jnp.where(lane < Q_BLOCK, selneg, swapped)
        right = jnp.where(lane < Q_BLOCK, swapped, selneg)
        sel_scr[0, :nsel] = jnp.concatenate([left, left], axis=1)
        sel_scr[1, :nsel] = jnp.concatenate([right, right], axis=1)

        for g in range(N_KV):
            parts = []
            for i in range(2 * WIN_TILES):
                delta = par + WIN_BLOCKS - i
                ok = (delta >= 0) & (delta <= WIN_BLOCKS) & (delta <= ci)
                tile = jnp.where(ok, delta, WIN_BLOCKS + 1)
                half = s_win[g][i // 2][(i % 2) * SEL_BLOCK:(i % 2 + 1) * SEL_BLOCK, :]
                parts.append(half + wb_ref[g, tile])
            s = jnp.concatenate(parts, axis=0)
            m = jnp.max(s, axis=0, keepdims=True)
            pb = jnp.exp2(s - m)
            o_win = jnp.zeros((V_ROWS, GQ), F32)
            for j, tj in enumerate(win_tiles):
                o_win = o_win + jnp.dot(vwt_ref[0, tj, vsl(g), :], pb[j * V_TILE:(j + 1) * V_TILE, :],
                                        preferred_element_type=F32)
            w_scale = (gt_ref[0, 0, 2 * N_KV + g:2 * N_KV + g + 1, :]
                       * (1.0 / o_win[HEAD_DIM:HEAD_DIM + 1, :]))
            out_scr[g] = gt_ref[0, 0, g:g + 1, :] * o_cmp[g] + w_scale * o_win[:HEAD_DIM, :]
            acc_scr[g] = jnp.zeros((V_ROWS, GQ), F32)

    row_steps = [r for r in range(CMP_ROW_STEP, n_cmp_rows + 1, CMP_ROW_STEP)] or [n_cmp_rows]
    variant = jnp.minimum((4 * ci + 2) // CMP_ROW_STEP, len(row_steps) - 1)
    lax.switch(variant, [functools.partial(before_loops, r) for r in row_steps])

    def chunk_softmax(s, g, c, near, extra):
        parts = []
        for i in range(BLOCKS_PER_CHUNK):
            kb = c * BLOCKS_PER_CHUNK + i
            blk = s[i * SEL_BLOCK:(i + 1) * SEL_BLOCK, :]
            mrow = sel_scr[g, pl.ds(kb, 1), :]
            if extra is not None:
                mrow = mrow + extra
            if near:
                tile = jnp.clip(ci - kb, 0, NEAR_TILES)
                blk = blk + sb_ref[g, tile] + mrow.astype(BF16)
            else:
                blk = blk + (mrow + far_ref[g]).astype(BF16)
            parts.append(blk)
        s = jnp.concatenate(parts, axis=0)
        m_c = jnp.max(s, axis=0, keepdims=True)
        return m_c.astype(F32), jnp.exp2(s - m_c)

    def chunk_pv(pb, g, c):
        return jnp.dot(vst_ref[0, c, vsl(g), :], pb, preferred_element_type=F32)

    def merge_step(units, extras, near, next_units, carry):
        scores = {k: s_scr[k] for k in range(QK_AHEAD)}
        results = []
        for k, (g, c) in enumerate(units):
            m_c, pb = chunk_softmax(scores.pop(k), g, c, near, extras[k])
            if k + QK_AHEAD < len(units):
                scores[k + QK_AHEAD] = chunk_qk(*units[k + QK_AHEAD])
            else:
                j = k + QK_AHEAD - len(units)
                s_scr[j] = chunk_qk(*next_units[j])
            results.append((m_c, chunk_pv(pb, g, c)))
        per_g = len(units) // N_KV
        new = []
        for g in range(N_KV):
            m_run = carry[g]
            stats = results[g * per_g:(g + 1) * per_g]
            m_new = m_run
            for m_c, _ in stats:
                m_new = jnp.maximum(m_new, m_c)
            acc = jnp.exp2(m_run - m_new) * acc_scr[g]
            for m_c, pv in stats:
                acc = acc + jnp.exp2(m_c - m_new) * pv
            acc_scr[g] = acc
            new.append(m_new)
        return tuple(new)

    def wide_step(i, carry):
        units = wide_units(i)
        nxt = pick_units([(i + 1 < n_wide, wide_units(i + 1)), (n_far > 0, far_units(0)),
                          (None, near_units(0))])
        return merge_step(units, [None] * len(units), False, nxt, carry)

    def far_step(i, carry):
        units = far_units(i)
        return merge_step(units, [None] * len(units), False, pick_units([(None, near_units(0))]), carry)

    def near_step(i, carry):
        units = near_units(i)
        extras = [None if j == 0 else jnp.where(c0 + NEAR_CHUNKS * i + j < n_chunks, 0.0, NEG)
                  for _ in range(N_KV) for j in range(NEAR_CHUNKS)]
        return merge_step(units, extras, True, near_units(i + 1), carry)

    init = (jnp.full((1, GQ), NEG, F32),) * N_KV
    carry = lax.fori_loop(0, n_wide, wide_step, init)
    carry = lax.fori_loop(0, n_far, far_step, carry)
    lax.fori_loop(0, n_near, near_step, carry)

    for g in range(N_KV):
        acc = acc_scr[g]
        scale = gt_ref[0, 0, N_KV + g:N_KV + g + 1, :] * (1.0 / acc[HEAD_DIM:HEAD_DIM + 1, :])
        fin_scr[g] = out_scr[g] + scale * acc[:HEAD_DIM, :]


def _attn_call(qt, kc, vct, ks, vst, kw, vwt, gt, ovt, ebank, sbank, wbank, far, eb_shift):
    b, nq = qt.shape[0], qt.shape[1]
    t = ks.shape[1]
    n_cmp_rows = kc.shape[1]
    n_sb = t // SEL_BLOCK
    per_b = lambda shp: pl.BlockSpec((1,) + shp[1:], lambda i, j: (i,) + (0,) * (len(shp) - 1))
    per_q = lambda shp: pl.BlockSpec(
        (1, 1) + shp[2:], lambda i, j: (i, jnp.minimum(j, nq - 1)) + (0,) * (len(shp) - 2))
    full = lambda shp: pl.BlockSpec(shp, lambda i, j: (0,) * len(shp))
    args = (qt, kc, vct, ks, vst, kw, vwt, gt, ovt, ebank, sbank, wbank, far)
    specs = [per_q(qt.shape), per_b(kc.shape), per_b(vct.shape), per_b(ks.shape), per_b(vst.shape),
             per_b(kw.shape), per_b(vwt.shape), per_q(gt.shape), full(ovt.shape),
             full(ebank.shape), full(sbank.shape), full(wbank.shape), full(far.shape)]
    acc_like = pltpu.VMEM((N_KV, HEAD_DIM, GQ), F32)
    return pl.pallas_call(
        functools.partial(_attn_kernel, n_cmp_rows=n_cmp_rows, eb_shift=eb_shift),
        grid=(b, nq + 1),
        in_specs=specs,
        out_specs=pl.BlockSpec((1, Q_BLOCK, ATTN_W), lambda i, j: (i, jnp.maximum(j - 1, 0), 0)),
        out_shape=jax.ShapeDtypeStruct((b, t, ATTN_W), BF16),
        scratch_shapes=[pltpu.VMEM((N_KV, n_sb, GQ), F32),
                        pltpu.VMEM((N_KV, V_ROWS, GQ), F32),
                        acc_like,
                        pltpu.VMEM((N_KV, KV_W, GQ), BF16),
                        pltpu.VMEM((QK_AHEAD, KEY_CHUNK, GQ), BF16),
                        acc_like],
        compiler_params=_cparams(2),
        name="nsa_attention",
    )(*args)


def _ffn_kernel(x_ref, a_ref, c_ref, mod_ref, n2_ref, wo_ref, w1_ref, w2_ref, o_ref):
    aw = a_ref.shape[2]
    mix = jnp.dot(a_ref[0], wo_ref[0:aw, :], preferred_element_type=F32)
    mix = mix + jnp.dot(c_ref[0], wo_ref[aw:, :], preferred_element_type=F32)
    x1 = x_ref[0] + mod_ref[0, 2:3, :] * mix
    ms = jnp.mean(x1 * x1, axis=-1, keepdims=True)
    y = x1 * lax.rsqrt(ms + EPS) * n2_ref[...]
    h2 = (y * (1.0 + mod_ref[0, 4:5, :]) + mod_ref[0, 3:4, :]).astype(BF16)
    d_ff = w1_ref.shape[1]
    ff = jnp.zeros(x1.shape, F32)
    for j in range(d_ff // FF_CHUNK):
        a = jnp.dot(h2, w1_ref[:, j * FF_CHUNK:(j + 1) * FF_CHUNK], preferred_element_type=F32)
        a = jnp.maximum(a, 0.0)
        ff = ff + jnp.dot((a * a).astype(BF16), w2_ref[j * FF_CHUNK:(j + 1) * FF_CHUNK, :],
                          preferred_element_type=F32)
    o_ref[0] = x1 + mod_ref[0, 5:6, :] * ff


def _ffn_call(x, attn, conv, mod, norm2, w_out, w_ff1, w_ff2):
    b, t, d = x.shape
    tm = min(ROW_TILE, t)
    row_spec = lambda w: pl.BlockSpec((1, tm, w), lambda i, j: (i, j, 0))
    full = lambda shp: pl.BlockSpec(shp, lambda i, j: (0,) * len(shp),
                                    pipeline_mode=pl.Buffered(1))
    return pl.pallas_call(
        _ffn_kernel,
        grid=(b, t // tm),
        in_specs=[row_spec(d), row_spec(attn.shape[2]), row_spec(conv.shape[2]),
                  pl.BlockSpec((1, N_MOD, d), lambda i, j: (i, 0, 0)),
                  full((1, d)), full(w_out.shape), full(w_ff1.shape), full(w_ff2.shape)],
        out_specs=row_spec(d),
        out_shape=jax.ShapeDtypeStruct((b, t, d), F32),
        compiler_params=_cparams(2),
        name="outproj_mlp",
    )(x, attn, conv, mod, norm2, w_out, w_ff1, w_ff2)


def _block_diag_ones(n):
    idx = np.arange(n) // HEAD_DIM
    return jnp.asarray(idx[:, None] == idx[None, :], dtype=BF16)


def _pack_w_in(w_in):
    d = w_in.shape[0]
    conv_w = d - ATTN_W
    sizes = [ATTN_W] + [KV_W] * 6 + [N_BRANCH * N_HEADS] + [conv_w] * 3
    offs = np.concatenate([[0], np.cumsum(sizes)])
    part = lambda i: w_in[:, offs[i]:offs[i + 1]]
    q, kc, vc, ks, vs, kw, vw, g, cgate, bgate, u = (part(i) for i in range(11))
    w_nat = jnp.concatenate([kc, vc, ks, kw, cgate, bgate, u], axis=1).astype(BF16)
    gt = g.reshape(d, N_KV, GQA, N_BRANCH).transpose(2, 3, 1, 0).reshape(GQA, N_BRANCH * N_KV, d)
    gt = jnp.pad(gt, ((0, 0), (0, SUBLANES - N_BRANCH * N_KV), (0, 0))).reshape(_G_ROWS, d)
    w_tr = jnp.concatenate([q.T, vs.T, vw.T, gt], axis=0).astype(BF16)
    return w_nat, w_tr


def _expand_w1(w1):
    hid = w1.shape[1]
    w = w1.reshape(2, CMP_STRIDE, HEAD_DIM, hid).astype(BF16)
    zero = jnp.zeros_like(w)
    per_group = [jnp.concatenate([w if k == g else zero for k in range(N_KV)], axis=-1)
                 for g in range(N_KV)]
    return jnp.stack(per_group, axis=2).reshape(2, CMP_STRIDE * KV_W, N_KV * hid)


def _expand_w2(w2):
    hid = w2.shape[0]
    eye = jnp.eye(N_KV, dtype=w2.dtype).reshape(N_KV, 1, N_KV, 1)
    return (w2.reshape(1, hid, 1, HEAD_DIM) * eye).reshape(N_KV * hid, KV_W).astype(BF16)


def _expand_pe(pe):
    p = pe.reshape(2, CMP_STRIDE, 1, HEAD_DIM)
    return jnp.broadcast_to(p, (2, CMP_STRIDE, N_KV, HEAD_DIM)).reshape(2, CMP_STRIDE * KV_W)


def _bucket_thresholds():
    n = np.arange(2 * REL_MAX_DIST)
    max_exact = REL_BUCKETS // 2
    nf = np.maximum(n, max_exact).astype(np.float32)
    ratio = np.log(nf / np.float32(max_exact)) / np.float32(math.log(REL_MAX_DIST / max_exact))
    large = max_exact + (ratio * np.float32(REL_BUCKETS - max_exact)).astype(np.int32)
    table = np.where(n < max_exact, n, np.minimum(large, REL_BUCKETS - 1))
    return tuple(int(np.searchsorted(table, k, side="left")) for k in range(REL_BUCKETS))


def _bank_call(bias_rows, lead, n_tiles, dist_fn, name):
    nl = len(lead)
    thr = _bucket_thresholds()

    def body(rows_ref, o_ref):
        lead_ids = [pl.program_id(a) for a in range(nl)]
        row = lax.broadcasted_iota(jnp.int32, (SEL_BLOCK, GQ), 0)
        qi = lax.broadcasted_iota(jnp.int32, (SEL_BLOCK, GQ), 1) & (Q_BLOCK - 1)

        def tile(t, carry):
            dist, ok = dist_fn(lead_ids, t, row, qi)
            v = jnp.broadcast_to(rows_ref[0, 0:1, :], (SEL_BLOCK, GQ))
            for k in range(1, REL_BUCKETS):
                v = jnp.where(dist >= thr[k], rows_ref[0, k:k + 1, :], v)
            o_ref[(0,) * (nl + 1) + (t,)] = jnp.where(ok, v, NEG)
            return carry

        lax.fori_loop(0, n_tiles, tile, 0)

    return pl.pallas_call(
        body,
        grid=tuple(lead) + (N_KV,),
        in_specs=[pl.BlockSpec((1, REL_BUCKETS, GQ), lambda *i: (i[nl], 0, 0))],
        out_specs=pl.BlockSpec((1,) * (nl + 1) + (n_tiles, SEL_BLOCK, GQ),
                               lambda *i: tuple(i) + (0, 0, 0)),
        out_shape=jax.ShapeDtypeStruct(tuple(lead) + (N_KV, n_tiles, SEL_BLOCK, GQ), F32),
        compiler_params=_cparams(nl + 1),
        name=name,
    )(bias_rows)


def _bias_banks(rel_bias, t):
    n_cmp_rows = t // CMP_STRIDE
    rows = rel_bias.reshape(REL_BUCKETS, N_KV, GQA).transpose(1, 0, 2)
    rows = jnp.repeat(rows, Q_BLOCK, axis=2) * LOG2E

    def sel_dist(lead, tile, row, qi):
        dist = SEL_BLOCK * tile + qi - row
        return dist, dist >= 0

    sbank = _bank_call(rows, (), NEAR_TILES + 1, sel_dist, "bias_bank_sel")
    far = sbank[:, NEAR_TILES, 0:1, :]

    def win_dist(lead, tile, row, qi):
        dist = SEL_BLOCK * tile + qi - row
        return dist, (dist >= 0) & (dist < WINDOW)

    wbank = _bank_call(rows, (), WIN_BLOCKS + 2, win_dist, "bias_bank_win")

    cmp_c = n_cmp_rows - 4
    shifts = tuple(int((-(cmp_c - 4 * p)) % 8) for p in range(2))
    n_tiles = (cmp_c + n_cmp_rows + 8 + SEL_BLOCK - 1) // SEL_BLOCK

    def cmp_dist(lead, tile, row, qi):
        e = SEL_BLOCK * tile + row - jnp.where(lead[0] == 0, shifts[0], shifts[1])
        dist = qi - CMP_STRIDE * e + (CMP_STRIDE * cmp_c - (CMP_BLOCK - 1))
        return dist, (dist >= 0) & (e >= 0)

    ebank = _bank_call(rows, (2,), n_tiles, cmp_dist, "bias_bank_cmp")
    ebank = ebank.reshape(2, N_KV, n_tiles * SEL_BLOCK, GQ)
    return ebank, sbank, wbank, far, shifts


def _overlap_t(t):
    n_cmp_rows = t // CMP_STRIDE
    n_sb = t // SEL_BLOCK
    c_start = np.arange(n_cmp_rows)[None, :] * CMP_STRIDE
    s_start = np.arange(n_sb)[:, None] * SEL_BLOCK
    ov = np.clip(np.minimum(c_start + CMP_BLOCK, s_start + SEL_BLOCK)
                 - np.maximum(c_start, s_start), 0, None) / CMP_BLOCK
    ov[:, n_cmp_rows - 1] = 0.0
    return jnp.asarray(ov, dtype=BF16)


def _layer(x, c_pad, w_in, q_norm, k_norm, cmp_pe_k, cmp_w1_k, cmp_w2_k, cmp_pe_v, cmp_w1_v,
           cmp_w2_v, rel_bias, conv_w, w_out, norm1, norm2, w_ada, b_ada, w_ff1, w_ff2):
    b, t, d = x.shape
    scale = HEAD_DIM ** -0.5

    mod = _mod_call(c_pad, w_ada, b_ada)[:b].reshape(b, N_MOD, d)

    qn_col = (jnp.tile(q_norm, N_HEADS) * (scale * LOG2E)).reshape(ATTN_W, 1)
    kn_t = jnp.tile(k_norm, N_KV).reshape(1, KV_W)
    bdq = _block_diag_ones(ATTN_W)
    bdk = _block_diag_ones(KV_W)
    w_nat, w_tr = _pack_w_in(w_in)
    qt, kc_raw, vc_raw, ks, vst, kw, vwt, gt, conv = _inproj_call(
        x, mod, norm1.reshape(1, d), w_nat, w_tr, qn_col, kn_t, conv_w, bdq, bdk)

    n_cmp_rows = t // CMP_STRIDE
    kc, vct = _compress_call(
        kc_raw, vc_raw, _expand_pe(cmp_pe_k), _expand_pe(cmp_pe_v), _expand_w1(cmp_w1_k), _expand_w1(cmp_w1_v),
        _expand_w2(cmp_w2_k), _expand_w2(cmp_w2_v).T, kn_t, bdk)

    ebank, sbank, wbank, far, eb_shift = _bias_banks(rel_bias, t)
    attn = _attn_call(qt, kc, vct, ks, vst, kw, vwt, gt, _overlap_t(t), ebank, sbank.astype(BF16),
                      wbank.astype(BF16), far, eb_shift)

    return _ffn_call(x, attn, conv, mod, norm2.reshape(1, d), w_out.astype(BF16),
                     w_ff1.astype(BF16), w_ff2.astype(BF16))


def kernel(x, c, w_in, q_norm, k_norm, cmp_pe_k, cmp_w1_k, cmp_w2_k, cmp_pe_v, cmp_w1_v, cmp_w2_v,
           rel_bias, conv_w, w_out, norm1, norm2, w_ada, b_ada, w_ff1, w_ff2):
    b = x.shape[0]
    c_pad = jnp.pad(c, ((0, (-b) % 8), (0, 0)))
    for l in range(w_in.shape[0]):
        x = _layer(x, c_pad, w_in[l], q_norm[l], k_norm[l], cmp_pe_k[l], cmp_w1_k[l], cmp_w2_k[l],
                   cmp_pe_v[l], cmp_w1_v[l], cmp_w2_v[l], rel_bias, conv_w[l], w_out[l],
                   norm1[l], norm2[l], w_ada[l], b_ada[l], w_ff1[l], w_ff2[l])
    return x
```

```python
import functools
import math

import numpy as np
import jax
import jax.numpy as jnp
from jax import lax
from jax.experimental import pallas as pl
from jax.experimental.pallas import tpu as pltpu

HEAD_DIM = 64
N_HEADS = 8
N_KV = 2
GQA = N_HEADS // N_KV
ATTN_W = N_HEADS * HEAD_DIM
KV_W = N_KV * HEAD_DIM
CONV_K = 3
CMP_BLOCK = 32
CMP_STRIDE = 16
CMP_HIDDEN = 256
SEL_BLOCK = 64
N_SELECT = 16
WINDOW = 512
Q_BLOCK = 64
REL_BUCKETS = 32
REL_MAX_DIST = 1024
N_MOD = 6
N_BRANCH = 3
EPS = 1e-6
NEG = -1e30

LANES = 128
SUBLANES = 8
GQ = GQA * Q_BLOCK
KEY_CHUNK = 256
BLOCKS_PER_CHUNK = KEY_CHUNK // SEL_BLOCK
V_TILE = 128
BF16_ROWS = 16
V_ROWS = HEAD_DIM + BF16_ROWS
V_TILE_ROWS = N_KV * V_ROWS
CMP_ROW_STEP = 64
WIN_BLOCKS = WINDOW // SEL_BLOCK
WIN_TILES = WINDOW // V_TILE + 1
NEAR_TILES = (REL_MAX_DIST + Q_BLOCK - 1) // SEL_BLOCK + 1
ROW_TILE = 512
INPROJ_TILE = 1024
INPROJ_SPLIT = 4
FAR_CHUNKS = 4
NEAR_CHUNKS = 4
QK_AHEAD = 4
LOG2E = math.log2(math.e)
FF_CHUNK = 1024
VMEM_LIMIT = 56 * 1024 * 1024

F32 = jnp.float32
BF16 = jnp.bfloat16
_NT = (((1,), (1,)), ((), ()))


def _cparams(n_axes):
    return pltpu.CompilerParams(dimension_semantics=("arbitrary",) * n_axes,
                                vmem_limit_bytes=VMEM_LIMIT)


def _with_ones_rows(vt):
    ones = jnp.ones((BF16_ROWS, vt.shape[1]), vt.dtype)
    parts = []
    for g in range(N_KV):
        parts += [vt[g * HEAD_DIM:(g + 1) * HEAD_DIM, :], ones]
    return jnp.concatenate(parts, axis=0)


def _swap_halves(p0, p1):
    low = lax.broadcasted_iota(jnp.int32, p0.shape, 1) < LANES // 2
    return (jnp.where(low, p0, pltpu.roll(p1, LANES // 2, 1)),
            jnp.where(low, pltpu.roll(p0, LANES // 2, 1), p1))


def _mod_kernel(c_ref, w_ref, b_ref, o_ref):
    c = c_ref[...]
    a = c * jax.nn.sigmoid(c)
    o_ref[...] = jnp.dot(a, w_ref[...], preferred_element_type=F32,
                         precision=lax.Precision.HIGHEST) + b_ref[...]


def _mod_call(c_pad, w_ada, b_ada):
    rows, d = c_pad.shape
    n = w_ada.shape[1]
    tn = 1024
    return pl.pallas_call(
        _mod_kernel,
        grid=(n // tn,),
        in_specs=[pl.BlockSpec((rows, d), lambda j: (0, 0)),
                  pl.BlockSpec((d, tn), lambda j: (0, j)),
                  pl.BlockSpec((1, tn), lambda j: (0, j))],
        out_specs=pl.BlockSpec((rows, tn), lambda j: (0, j)),
        out_shape=jax.ShapeDtypeStruct((rows, n), F32),
        compiler_params=_cparams(1),
        name="adaln_mod",
    )(c_pad, w_ada, b_ada.reshape(1, n))


_N_KC, _N_VC, _N_KS, _N_KW, _N_CONV = 0, KV_W, 2 * KV_W, 3 * KV_W, 4 * KV_W
_T_Q, _T_VS, _T_VW, _T_G = 0, ATTN_W, ATTN_W + KV_W, ATTN_W + 2 * KV_W
_G_ROWS = GQA * SUBLANES


def _inproj_kernel(x_ref, mod_ref, n1_ref, wn_ref, wt_ref, qn_ref, kn_ref, cw_ref, bdq_ref, bdk_ref,
                   qt_out, kc_out, vc_out, ks_out, vst_out, kw_out, vwt_out, gt_out, conv_out,
                   carry, *, conv_w):
    t = pl.program_id(1)
    tm = x_ref.shape[1]
    sub = tm // INPROJ_SPLIT

    @pl.when(t == 0)
    def _():
        carry[...] = jnp.zeros_like(carry)

    prev2, prev1 = carry[6:7, :], carry[7:8, :]
    c_bg = _N_CONV + conv_w
    c_u = c_bg + conv_w

    for part in range(INPROJ_SPLIT):
        rows = slice(part * sub, (part + 1) * sub)
        x = x_ref[0, rows, :]
        ms = jnp.mean(x * x, axis=-1, keepdims=True)
        y = x * lax.rsqrt(ms + EPS) * n1_ref[...]
        h = (y * (1.0 + mod_ref[0, 1:2, :]) + mod_ref[0, 0:1, :]).astype(BF16)

        def proj(a, b):
            return jnp.dot(h, wn_ref[:, a:b], preferred_element_type=F32)

        def proj_t(a, b):
            return lax.dot_general(wt_ref[a:b, :], h, _NT, preferred_element_type=F32)

        def head_norm(v, gain):
            ssq = jnp.dot((v * v).astype(BF16), bdk_ref[...], preferred_element_type=F32)
            return v * lax.rsqrt(ssq * (1.0 / HEAD_DIM) + EPS) * gain

        qf = proj_t(_T_Q, _T_VS)
        vg = proj_t(_T_VS, _T_G + _G_ROWS)
        kv = proj(_N_KC, _N_CONV)
        cv = proj(_N_CONV, c_u + conv_w)

        ssq = jnp.dot(bdq_ref[...], (qf * qf).astype(BF16), preferred_element_type=F32)
        qf = qf * lax.rsqrt(ssq * (1.0 / HEAD_DIM) + EPS) * qn_ref[...]
        gf = jax.nn.sigmoid(vg[2 * KV_W:, :])
        for c in range(sub // LANES):
            blk = (part * sub) // Q_BLOCK + 2 * c
            cols = slice(c * LANES, (c + 1) * LANES)
            for g in range(N_KV):
                pc = [qf[(g * GQA + r) * HEAD_DIM:(g * GQA + r + 1) * HEAD_DIM, cols]
                      for r in range(GQA)]
                lo01, hi01 = _swap_halves(pc[0], pc[1])
                lo23, hi23 = _swap_halves(pc[2], pc[3])
                qt_out[0, blk, g] = jnp.concatenate([lo01, lo23], axis=1).astype(BF16)
                qt_out[0, blk + 1, g] = jnp.concatenate([hi01, hi23], axis=1).astype(BF16)
            pc = [gf[r * SUBLANES:(r + 1) * SUBLANES, cols] for r in range(GQA)]
            lo01, hi01 = _swap_halves(pc[0], pc[1])
            lo23, hi23 = _swap_halves(pc[2], pc[3])
            gt_out[0, blk] = jnp.concatenate([lo01, lo23], axis=1)
            gt_out[0, blk + 1] = jnp.concatenate([hi01, hi23], axis=1)

        vs_f = _with_ones_rows(vg[:KV_W, :].astype(BF16))
        vw_f = _with_ones_rows(vg[KV_W:2 * KV_W, :].astype(BF16))
        for j in range(sub // KEY_CHUNK):
            vst_out[0, (part * sub) // KEY_CHUNK + j] = vs_f[:, j * KEY_CHUNK:(j + 1) * KEY_CHUNK]
        for j in range(sub // V_TILE):
            vwt_out[0, (part * sub) // V_TILE + j] = vw_f[:, j * V_TILE:(j + 1) * V_TILE]

        kc_out[0, rows, :] = kv[:, _N_KC:_N_VC]
        vc_out[0, rows, :] = kv[:, _N_VC:_N_KS]
        ks_out[0, rows, :] = head_norm(kv[:, _N_KS:_N_KW], kn_ref[...]).astype(BF16)
        kw_out[0, rows, :] = head_norm(kv[:, _N_KW:_N_CONV], kn_ref[...]).astype(BF16)

        z = cv[:, :conv_w] * cv[:, 2 * conv_w:]
        row = lax.broadcasted_iota(jnp.int32, z.shape, 0)
        z1 = jnp.where(row == 0, prev1, pltpu.roll(z, 1, 0))
        z2 = jnp.where(row == 0, prev2, jnp.where(row == 1, prev1, pltpu.roll(z, 2, 0)))
        zc = cw_ref[0:1, :] * z2 + cw_ref[1:2, :] * z1 + cw_ref[2:3, :] * z
        conv_out[0, rows, :] = (cv[:, conv_w:2 * conv_w] * zc).astype(BF16)
        prev2, prev1 = z[sub - 2:sub - 1, :], z[sub - 1:sub, :]
        if part == INPROJ_SPLIT - 1:
            carry[...] = z[sub - SUBLANES:sub, :]


def _inproj_call(x, mod, norm1, w_nat, w_tr, qn_col, kn_t, conv_w, bdq, bdk):
    b, t, d = x.shape
    tm = min(INPROJ_TILE, t)
    cw = conv_w.shape[1]
    nq = t // Q_BLOCK
    row_spec = lambda w: pl.BlockSpec((1, tm, w), lambda i, j: (i, j, 0))
    full = lambda shp: pl.BlockSpec(shp, lambda i, j: (0,) * len(shp))
    vt_spec = lambda w: pl.BlockSpec((1, tm // w, V_TILE_ROWS, w), lambda i, j: (i, j, 0, 0))
    vt_shape = lambda w: jax.ShapeDtypeStruct((b, t // w, V_TILE_ROWS, w), BF16)
    kv = lambda dt: jax.ShapeDtypeStruct((b, t, KV_W), dt)
    out_specs = [pl.BlockSpec((1, tm // Q_BLOCK, N_KV, HEAD_DIM, GQ), lambda i, j: (i, j, 0, 0, 0)),
                 row_spec(KV_W), row_spec(KV_W), row_spec(KV_W), vt_spec(KEY_CHUNK), row_spec(KV_W),
                 vt_spec(V_TILE),
                 pl.BlockSpec((1, tm // Q_BLOCK, SUBLANES, GQ), lambda i, j: (i, j, 0, 0)),
                 row_spec(cw)]
    out_shape = [jax.ShapeDtypeStruct((b, nq, N_KV, HEAD_DIM, GQ), BF16),
                 kv(F32), kv(F32), kv(BF16), vt_shape(KEY_CHUNK), kv(BF16), vt_shape(V_TILE),
                 jax.ShapeDtypeStruct((b, nq, SUBLANES, GQ), F32),
                 jax.ShapeDtypeStruct((b, t, cw), BF16)]
    return pl.pallas_call(
        functools.partial(_inproj_kernel, conv_w=cw),
        grid=(b, t // tm),
        in_specs=[row_spec(d),
                  pl.BlockSpec((1, N_MOD, d), lambda i, j: (i, 0, 0)),
                  full((1, d)), full(w_nat.shape), full(w_tr.shape), full((ATTN_W, 1)),
                  full((1, KV_W)), full((CONV_K, cw)), full((ATTN_W, ATTN_W)), full((KV_W, KV_W))],
        out_specs=out_specs,
        out_shape=out_shape,
        scratch_shapes=[pltpu.VMEM((SUBLANES, cw), F32)],
        compiler_params=_cparams(2),
        name="inproj",
    )(x, mod, norm1, w_nat, w_tr, qn_col, kn_t, conv_w, bdq, bdk)


def _compress_kernel(kx_ref, vx_ref, pek_ref, pev_ref, w1k_ref, w1v_ref, w2k_ref, w2vt_ref,
                     kn_ref, bdk_ref, kc_out, vct_out):
    def hidden(x_ref, pe_ref, w1_ref):
        n = x_ref.shape[1] // CMP_STRIDE
        u = jnp.zeros((n, w1_ref.shape[2]), F32)
        v = jnp.zeros((n, w1_ref.shape[2]), F32)
        for r in range(0, CMP_STRIDE, 2):
            tok = [x_ref[0, pl.ds(r + d, n, stride=CMP_STRIDE), :] for d in range(2)]
            cols = slice(r * KV_W, (r + 2) * KV_W)
            for a, acc in ((0, "u"), (1, "v")):
                lhs = jnp.concatenate([tok[d] + pe_ref[a:a + 1, (r + d) * KV_W:(r + d + 1) * KV_W]
                                       for d in range(2)], axis=1).astype(BF16)
                prod = jnp.dot(lhs, w1_ref[a, cols, :], preferred_element_type=F32)
                if acc == "u":
                    u = u + prod
                else:
                    v = v + prod
        hid = u + pltpu.roll(v, n - 1, 0)
        return jax.nn.gelu(hid, approximate=True).astype(BF16)

    kc = jnp.dot(hidden(kx_ref, pek_ref, w1k_ref), w2k_ref[...], preferred_element_type=F32)
    ssq = jnp.dot((kc * kc).astype(BF16), bdk_ref[...], preferred_element_type=F32)
    kc_out[0] = (kc * lax.rsqrt(ssq * (1.0 / HEAD_DIM) + EPS) * kn_ref[...]).astype(BF16)
    vct_out[0] = _with_ones_rows(lax.dot_general(w2vt_ref[...], hidden(vx_ref, pev_ref, w1v_ref), _NT,
                                                 preferred_element_type=F32).astype(BF16))


def _compress_call(kx, vx, pek, pev, w1k, w1v, w2k, w2vt, kn_t, bdk):
    b, t, _ = kx.shape
    nrow = t // CMP_STRIDE
    wide, hid2 = w1k.shape[1], w1k.shape[2]
    full = lambda shp: pl.BlockSpec(shp, lambda i: (0,) * len(shp))
    xs = pl.BlockSpec((1, t, KV_W), lambda i: (i, 0, 0))
    return pl.pallas_call(
        _compress_kernel,
        grid=(b,),
        in_specs=[xs, xs, full((2, wide)), full((2, wide)), full((2, wide, hid2)),
                  full((2, wide, hid2)), full((hid2, KV_W)), full((KV_W, hid2)),
                  full((1, KV_W)), full((KV_W, KV_W))],
        out_specs=[pl.BlockSpec((1, nrow, KV_W), lambda i: (i, 0, 0)),
                   pl.BlockSpec((1, V_TILE_ROWS, nrow), lambda i: (i, 0, 0))],
        out_shape=[jax.ShapeDtypeStruct((b, nrow, KV_W), BF16),
                   jax.ShapeDtypeStruct((b, V_TILE_ROWS, nrow), BF16)],
        compiler_params=_cparams(1),
        name="compress",
    )(kx, vx, pek, pev, w1k, w1v, w2k, w2vt, kn_t, bdk)


def _attn_kernel(q_ref, kc_ref, vct_ref, ks_ref, vst_ref, kw_ref, vwt_ref, gt_ref, ovt_ref,
                 eb_ref, sb_ref, wb_ref, far_ref, o_ref, sel_scr, acc_scr, out_scr, qp_scr, s_scr,
                 fin_scr, *, n_cmp_rows, eb_shift):
    n_sb = sel_scr.shape[1]
    ci = jnp.minimum(pl.program_id(1), n_sb - 1)
    par = lax.rem(ci, 2)
    n_total = ks_ref.shape[1] // KEY_CHUNK
    vsl = lambda g: slice(g * V_ROWS, (g + 1) * V_ROWS)

    @pl.when((pl.program_id(0) == 0) & (pl.program_id(1) == 0))
    def _():
        def zero(g, carry):
            fin_scr[g] = jnp.zeros(fin_scr.shape[1:], F32)
            sel_scr[g] = jnp.zeros(sel_scr.shape[1:], F32)
            return carry
        lax.fori_loop(0, N_KV, zero, 0)

    def chunk_qk(g, c):
        kk = ks_ref[0, pl.ds(pl.multiple_of(c * KEY_CHUNK, KEY_CHUNK), KEY_CHUNK), :]
        return jnp.dot(kk, qp_scr[g], preferred_element_type=F32).astype(BF16)

    n_chunks = ci // BLOCKS_PER_CHUNK + 1
    far_groups = jnp.maximum(ci - (NEAR_TILES - 1), 0) // (BLOCKS_PER_CHUNK * FAR_CHUNKS)
    n_wide = far_groups // 2
    n_far = far_groups - 2 * n_wide
    c_far = n_wide * (2 * FAR_CHUNKS)
    c0 = far_groups * FAR_CHUNKS
    n_near = (n_chunks - c0 + NEAR_CHUNKS - 1) // NEAR_CHUNKS

    def near_units(i):
        return [(g, jnp.minimum(c0 + NEAR_CHUNKS * i + j, n_total - 1))
                for g in range(N_KV) for j in range(NEAR_CHUNKS)]

    def far_units(i):
        return [(g, c_far + i * FAR_CHUNKS + j) for g in range(N_KV) for j in range(FAR_CHUNKS)]

    def wide_units(i):
        return [(g, i * 2 * FAR_CHUNKS + j) for g in range(N_KV) for j in range(2 * FAR_CHUNKS)]

    def pick_units(options):
        units = options[-1][1][:QK_AHEAD]
        for cond, cand in reversed(options[:-1]):
            units = [(jnp.where(cond, g_a, g_b), jnp.where(cond, c_a, c_b))
                     for (g_a, c_a), (g_b, c_b) in zip(cand[:QK_AHEAD], units)]
        return [(g, jnp.minimum(c, n_total - 1)) for g, c in units]

    def prefetch_scores(units):
        for k in range(QK_AHEAD):
            s_scr[k] = chunk_qk(*units[k])

    def before_loops(rows):
        pieces = []
        for g in range(N_KV):
            for half in range(GQ // LANES):
                a = fin_scr[g, :, half * LANES:(half + 1) * LANES]
                stacked = jnp.concatenate([a, pltpu.roll(a, Q_BLOCK, 1)], axis=0)
                pieces.append(stacked.T[:Q_BLOCK, :])
        o_ref[0] = jnp.concatenate(pieces, axis=1).astype(BF16)

        zeros_q = jnp.zeros((HEAD_DIM, GQ), BF16)
        qp_scr[0] = jnp.concatenate([q_ref[0, 0, 0], zeros_q], axis=0)
        qp_scr[1] = jnp.concatenate([zeros_q, q_ref[0, 0, 1]], axis=0)

        nsel = min(n_sb, rows * CMP_STRIDE // SEL_BLOCK)
        lane = lax.broadcasted_iota(jnp.int32, (nsel, LANES), 1)
        cmp_c = n_cmp_rows - 4
        e0 = cmp_c - 4 * ci + jnp.where(par == 0, eb_shift[0], eb_shift[1])
        e0 = pl.multiple_of(e0, 8)
        o_cmp = []
        imp = []
        w0 = ci // 2 - (WIN_TILES - 1)
        win_tiles = [jnp.maximum(w0 + j, 0) for j in range(WIN_TILES)]
        s_cmp = [jnp.dot(kc_ref[0, :rows, :], qp_scr[g], preferred_element_type=F32)
                 for g in range(N_KV)]
        for g in range(N_KV):
            sc = s_cmp[g] + eb_ref[par, g, pl.ds(e0, rows), :]
            m = jnp.max(sc, axis=0, keepdims=True)
            e = jnp.exp2(sc - m).astype(BF16)
            ov = jnp.dot(vct_ref[0, vsl(g), :rows], e, preferred_element_type=F32)
            inv = jnp.where(m > 0.5 * NEG, 1.0 / ov[HEAD_DIM:HEAD_DIM + 1, :], 0.0)
            o_cmp.append(ov[:HEAD_DIM, :] * inv)
            ir = jnp.dot(ovt_ref[:nsel, :rows], e, preferred_element_type=F32) * inv
            a = ir[:, :LANES] + ir[:, LANES:]
            imp.append(a + pltpu.roll(a, Q_BLOCK, 1))

        s_win = [[jnp.dot(kw_ref[0, pl.ds(pl.multiple_of(tj * V_TILE, V_TILE), V_TILE), :],
                          qp_scr[g], preferred_element_type=F32).astype(BF16) for tj in win_tiles]
                 for g in range(N_KV)]

        prefetch_scores(pick_units([(n_wide > 0, wide_units(0)), (n_far > 0, far_units(0)),
                                    (None, near_units(0))]))

        jidx = lax.broadcasted_iota(jnp.int32, (nsel, LANES), 0)
        jf = jidx.astype(F32)
        valid = jidx <= ci
        forced = (jidx == 0) | (jidx == ci) | (jidx == ci - 1)
        score = jnp.where(valid, jnp.where(forced, -2.0, jnp.where(lane < Q_BLOCK, imp[0], imp[1])),
                          -1.0)
        sel = jnp.where(forced, 1.0, 0.0)

        def first_max(score):
            pairs = [(score[r:r + SUBLANES], jf[r:r + SUBLANES]) for r in range(0, nsel, SUBLANES)]
            while len(pairs) > 1:
                nxt = []
                for (va, ia), (vb, ib) in zip(pairs[0::2], pairs[1::2]):
                    keep = va >= vb
                    nxt.append((jnp.where(keep, va, vb), jnp.where(keep, ia, ib)))
                pairs = nxt + pairs[len(pairs) - len(pairs) % 2:]
            v8, i8 = pairs[0]
            mx = jnp.max(v8, axis=0, keepdims=True)
            return jnp.min(jnp.where(v8 == mx, i8, float(nsel)), axis=0, keepdims=True)

        for _ in range(min(N_SELECT, nsel) - 3):
            first = first_max(score)
            hit = jf == first
            sel = jnp.where(hit, 1.0, sel)
            score = jnp.where(hit, -2.0, score)
        selneg = jnp.where((sel > 0.5) & valid, 0.0, NEG)
        swapped = pltpu.roll(selneg, Q_BLOCK, 1)
        left = jnp.where(lane < Q_BLOCK, selneg, swapped)
        right = jnp.where(lane < Q_BLOCK, swapped, selneg)
        sel_scr[0, :nsel] = jnp.concatenate([left, left], axis=1)
        sel_scr[1, :nsel] = jnp.concatenate([right, right], axis=1)

        for g in range(N_KV):
            parts = []
            for i in range(2 * WIN_TILES):
                delta = par + WIN_BLOCKS - i
                ok = (delta >= 0) & (delta <= WIN_BLOCKS) & (delta <= ci)
                tile = jnp.where(ok, delta, WIN_BLOCKS + 1)
                half = s_win[g][i // 2][(i % 2) * SEL_BLOCK:(i % 2 + 1) * SEL_BLOCK, :]
                parts.append(half + wb_ref[g, tile])
            s = jnp.concatenate(parts, axis=0)
            m = jnp.max(s, axis=0, keepdims=True)
            pb = jnp.exp2(s - m)
            o_win = jnp.zeros((V_ROWS, GQ), F32)
            for j, tj in enumerate(win_tiles):
                o_win = o_win + jnp.dot(vwt_ref[0, tj, vsl(g), :], pb[j * V_TILE:(j + 1) * V_TILE, :],
                                        preferred_element_type=F32)
            w_scale = (gt_ref[0, 0, 2 * N_KV + g:2 * N_KV + g + 1, :]
                       * (1.0 / o_win[HEAD_DIM:HEAD_DIM + 1, :]))
            out_scr[g] = gt_ref[0, 0, g:g + 1, :] * o_cmp[g] + w_scale * o_win[:HEAD_DIM, :]
            acc_scr[g] = jnp.zeros((V_ROWS, GQ), F32)

    row_steps = [r for r in range(CMP_ROW_STEP, n_cmp_rows + 1, CMP_ROW_STEP)] or [n_cmp_rows]
    variant = jnp.minimum((4 * ci + 2) // CMP_ROW_STEP, len(row_steps) - 1)
    lax.switch(variant, [functools.partial(before_loops, r) for r in row_steps])

    def chunk_softmax(s, g, c, near, extra):
        parts = []
        for i in range(BLOCKS_PER_CHUNK):
            kb = c * BLOCKS_PER_CHUNK + i
            blk = s[i * SEL_BLOCK:(i + 1) * SEL_BLOCK, :]
            mrow = sel_scr[g, pl.ds(kb, 1), :]
            if extra is not None:
                mrow = mrow + extra
            if near:
                tile = jnp.clip(ci - kb, 0, NEAR_TILES)
                blk = blk + sb_ref[g, tile] + mrow.astype(BF16)
            else:
                blk = blk + (mrow + far_ref[g]).astype(BF16)
            parts.append(blk)
        s = jnp.concatenate(parts, axis=0)
        m_c = jnp.max(s, axis=0, keepdims=True)
        return m_c.astype(F32), jnp.exp2(s - m_c)

    def chunk_pv(pb, g, c):
        return jnp.dot(vst_ref[0, c, vsl(g), :], pb, preferred_element_type=F32)

    def merge_step(units, extras, near, next_units, carry):
        scores = {k: s_scr[k] for k in range(QK_AHEAD)}
        results = []
        for k, (g, c) in enumerate(units):
            m_c, pb = chunk_softmax(scores.pop(k), g, c, near, extras[k])
            if k + QK_AHEAD < len(units):
                scores[k + QK_AHEAD] = chunk_qk(*units[k + QK_AHEAD])
            else:
                j = k + QK_AHEAD - len(units)
                s_scr[j] = chunk_qk(*next_units[j])
            results.append((m_c, chunk_pv(pb, g, c)))
        per_g = len(units) // N_KV
        new = []
        for g in range(N_KV):
            m_run = carry[g]
            stats = results[g * per_g:(g + 1) * per_g]
            m_new = m_run
            for m_c, _ in stats:
                m_new = jnp.maximum(m_new, m_c)
            acc = jnp.exp2(m_run - m_new) * acc_scr[g]
            for m_c, pv in stats:
                acc = acc + jnp.exp2(m_c - m_new) * pv
            acc_scr[g] = acc
            new.append(m_new)
        return tuple(new)

    def wide_step(i, carry):
        units = wide_units(i)
        nxt = pick_units([(i + 1 < n_wide, wide_units(i + 1)), (n_far > 0, far_units(0)),
                          (None, near_units(0))])
        return merge_step(units, [None] * len(units), False, nxt, carry)

    def far_step(i, carry):
        units = far_units(i)
        return merge_step(units, [None] * len(units), False, pick_units([(None, near_units(0))]), carry)

    def near_step(i, carry):
        units = near_units(i)
        extras = [None if j == 0 else jnp.where(c0 + NEAR_CHUNKS * i + j < n_chunks, 0.0, NEG)
                  for _ in range(N_KV) for j in range(NEAR_CHUNKS)]
        return merge_step(units, extras, True, near_units(i + 1), carry)

    init = (jnp.full((1, GQ), NEG, F32),) * N_KV
    carry = lax.fori_loop(0, n_wide, wide_step, init)
    carry = lax.fori_loop(0, n_far, far_step, carry)
    lax.fori_loop(0, n_near, near_step, carry)

    for g in range(N_KV):
        acc = acc_scr[g]
        scale = gt_ref[0, 0, N_KV + g:N_KV + g + 1, :] * (1.0 / acc[HEAD_DIM:HEAD_DIM + 1, :])
        fin_scr[g] = out_scr[g] + scale * acc[:HEAD_DIM, :]


def _attn_call(qt, kc, vct, ks, vst, kw, vwt, gt, ovt, ebank, sbank, wbank, far, eb_shift):
    b, nq = qt.shape[0], qt.shape[1]
    t = ks.shape[1]
    n_cmp_rows = kc.shape[1]
    n_sb = t // SEL_BLOCK
    per_b = lambda shp: pl.BlockSpec((1,) + shp[1:], lambda i, j: (i,) + (0,) * (len(shp) - 1))
    per_q = lambda shp: pl.BlockSpec(
        (1, 1) + shp[2:], lambda i, j: (i, jnp.minimum(j, nq - 1)) + (0,) * (len(shp) - 2))
    full = lambda shp: pl.BlockSpec(shp, lambda i, j: (0,) * len(shp))
    args = (qt, kc, vct, ks, vst, kw, vwt, gt, ovt, ebank, sbank, wbank, far)
    specs = [per_q(qt.shape), per_b(kc.shape), per_b(vct.shape), per_b(ks.shape), per_b(vst.shape),
             per_b(kw.shape), per_b(vwt.shape), per_q(gt.shape), full(ovt.shape),
             full(ebank.shape), full(sbank.shape), full(wbank.shape), full(far.shape)]
    acc_like = pltpu.VMEM((N_KV, HEAD_DIM, GQ), F32)
    return pl.pallas_call(
        functools.partial(_attn_kernel, n_cmp_rows=n_cmp_rows, eb_shift=eb_shift),
        grid=(b, nq + 1),
        in_specs=specs,
        out_specs=pl.BlockSpec((1, Q_BLOCK, ATTN_W), lambda i, j: (i, jnp.maximum(j - 1, 0), 0)),
        out_shape=jax.ShapeDtypeStruct((b, t, ATTN_W), BF16),
        scratch_shapes=[pltpu.VMEM((N_KV, n_sb, GQ), F32),
                        pltpu.VMEM((N_KV, V_ROWS, GQ), F32),
                        acc_like,
                        pltpu.VMEM((N_KV, KV_W, GQ), BF16),
                        pltpu.VMEM((QK_AHEAD, KEY_CHUNK, GQ), BF16),
                        acc_like],
        compiler_params=_cparams(2),
        name="nsa_attention",
    )(*args)


def _ffn_kernel(x_ref, a_ref, c_ref, mod_ref, n2_ref, wo_ref, w1_ref, w2_ref, o_ref):
    aw = a_ref.shape[2]
    mix = jnp.dot(a_ref[0], wo_ref[0:aw, :], preferred_element_type=F32)
    mix = mix + jnp.dot(c_ref[0], wo_ref[aw:, :], preferred_element_type=F32)
    x1 = x_ref[0] + mod_ref[0, 2:3, :] * mix
    ms = jnp.mean(x1 * x1, axis=-1, keepdims=True)
    y = x1 * lax.rsqrt(ms + EPS) * n2_ref[...]
    h2 = (y * (1.0 + mod_ref[0, 4:5, :]) + mod_ref[0, 3:4, :]).astype(BF16)
    d_ff = w1_ref.shape[1]
    ff = jnp.zeros(x1.shape, F32)
    for j in range(d_ff // FF_CHUNK):
        a = jnp.dot(h2, w1_ref[:, j * FF_CHUNK:(j + 1) * FF_CHUNK], preferred_element_type=F32)
        a = jnp.maximum(a, 0.0)
        ff = ff + jnp.dot((a * a).astype(BF16), w2_ref[j * FF_CHUNK:(j + 1) * FF_CHUNK, :],
                          preferred_element_type=F32)
    o_ref[0] = x1 + mod_ref[0, 5:6, :] * ff


def _ffn_call(x, attn, conv, mod, norm2, w_out, w_ff1, w_ff2):
    b, t, d = x.shape
    tm = min(ROW_TILE, t)
    row_spec = lambda w: pl.BlockSpec((1, tm, w), lambda i, j: (i, j, 0))
    full = lambda shp: pl.BlockSpec(shp, lambda i, j: (0,) * len(shp),
                                    pipeline_mode=pl.Buffered(1))
    return pl.pallas_call(
        _ffn_kernel,
        grid=(b, t // tm),
        in_specs=[row_spec(d), row_spec(attn.shape[2]), row_spec(conv.shape[2]),
                  pl.BlockSpec((1, N_MOD, d), lambda i, j: (i, 0, 0)),
                  full((1, d)), full(w_out.shape), full(w_ff1.shape), full(w_ff2.shape)],
        out_specs=row_spec(d),
        out_shape=jax.ShapeDtypeStruct((b, t, d), F32),
        compiler_params=_cparams(2),
        name="outproj_mlp",
    )(x, attn, conv, mod, norm2, w_out, w_ff1, w_ff2)


def _block_diag_ones(n):
    idx = np.arange(n) // HEAD_DIM
    return jnp.asarray(idx[:, None] == idx[None, :], dtype=BF16)


def _pack_w_in(w_in):
    d = w_in.shape[0]
    conv_w = d - ATTN_W
    sizes = [ATTN_W] + [KV_W] * 6 + [N_BRANCH * N_HEADS] + [conv_w] * 3
    offs = np.concatenate([[0], np.cumsum(sizes)])
    part = lambda i: w_in[:, offs[i]:offs[i + 1]]
    q, kc, vc, ks, vs, kw, vw, g, cgate, bgate, u = (part(i) for i in range(11))
    w_nat = jnp.concatenate([kc, vc, ks, kw, cgate, bgate, u], axis=1).astype(BF16)
    gt = g.reshape(d, N_KV, GQA, N_BRANCH).transpose(2, 3, 1, 0).reshape(GQA, N_BRANCH * N_KV, d)
    gt = jnp.pad(gt, ((0, 0), (0, SUBLANES - N_BRANCH * N_KV), (0, 0))).reshape(_G_ROWS, d)
    w_tr = jnp.concatenate([q.T, vs.T, vw.T, gt], axis=0).astype(BF16)
    return w_nat, w_tr


def _expand_w1(w1):
    hid = w1.shape[1]
    w = w1.reshape(2, CMP_STRIDE, HEAD_DIM, hid).astype(BF16)
    zero = jnp.zeros_like(w)
    per_group = [jnp.concatenate([w if k == g else zero for k in range(N_KV)], axis=-1)
                 for g in range(N_KV)]
    return jnp.stack(per_group, axis=2).reshape(2, CMP_STRIDE * KV_W, N_KV * hid)


def _expand_w2(w2):
    hid = w2.shape[0]
    eye = jnp.eye(N_KV, dtype=w2.dtype).reshape(N_KV, 1, N_KV, 1)
    return (w2.reshape(1, hid, 1, HEAD_DIM) * eye).reshape(N_KV * hid, KV_W).astype(BF16)


def _expand_pe(pe):
    p = pe.reshape(2, CMP_STRIDE, 1, HEAD_DIM)
    return jnp.broadcast_to(p, (2, CMP_STRIDE, N_KV, HEAD_DIM)).reshape(2, CMP_STRIDE * KV_W)


def _bucket_thresholds():
    n = np.arange(2 * REL_MAX_DIST)
    max_exact = REL_BUCKETS // 2
    nf = np.maximum(n, max_exact).astype(np.float32)
    ratio = np.log(nf / np.float32(max_exact)) / np.float32(math.log(REL_MAX_DIST / max_exact))
    large = max_exact + (ratio * np.float32(REL_BUCKETS - max_exact)).astype(np.int32)
    table = np.where(n < max_exact, n, np.minimum(large, REL_BUCKETS - 1))
    return tuple(int(np.searchsorted(table, k, side="left")) for k in range(REL_BUCKETS))


def _bank_call(bias_rows, lead, n_tiles, dist_fn, name):
    nl = len(lead)
    thr = _bucket_thresholds()

    def body(rows_ref, o_ref):
        lead_ids = [pl.program_id(a) for a in range(nl)]
        row = lax.broadcasted_iota(jnp.int32, (SEL_BLOCK, GQ), 0)
        qi = lax.broadcasted_iota(jnp.int32, (SEL_BLOCK, GQ), 1) & (Q_BLOCK - 1)

        def tile(t, carry):
            dist, ok = dist_fn(lead_ids, t, row, qi)
            v = jnp.broadcast_to(rows_ref[0, 0:1, :], (SEL_BLOCK, GQ))
            for k in range(1, REL_BUCKETS):
                v = jnp.where(dist >= thr[k], rows_ref[0, k:k + 1, :], v)
            o_ref[(0,) * (nl + 1) + (t,)] = jnp.where(ok, v, NEG)
            return carry

        lax.fori_loop(0, n_tiles, tile, 0)

    return pl.pallas_call(
        body,
        grid=tuple(lead) + (N_KV,),
        in_specs=[pl.BlockSpec((1, REL_BUCKETS, GQ), lambda *i: (i[nl], 0, 0))],
        out_specs=pl.BlockSpec((1,) * (nl + 1) + (n_tiles, SEL_BLOCK, GQ),
                               lambda *i: tuple(i) + (0, 0, 0)),
        out_shape=jax.ShapeDtypeStruct(tuple(lead) + (N_KV, n_tiles, SEL_BLOCK, GQ), F32),
        compiler_params=_cparams(nl + 1),
        name=name,
    )(bias_rows)


def _bias_banks(rel_bias, t):
    n_cmp_rows = t // CMP_STRIDE
    rows = rel_bias.reshape(REL_BUCKETS, N_KV, GQA).transpose(1, 0, 2)
    rows = jnp.repeat(rows, Q_BLOCK, axis=2) * LOG2E

    def sel_dist(lead, tile, row, qi):
        dist = SEL_BLOCK * tile + qi - row
        return dist, dist >= 0

    sbank = _bank_call(rows, (), NEAR_TILES + 1, sel_dist, "bias_bank_sel")
    far = sbank[:, NEAR_TILES, 0:1, :]

    def win_dist(lead, tile, row, qi):
        dist = SEL_BLOCK * tile + qi - row
        return dist, (dist >= 0) & (dist < WINDOW)

    wbank = _bank_call(rows, (), WIN_BLOCKS + 2, win_dist, "bias_bank_win")

    cmp_c = n_cmp_rows - 4
    shifts = tuple(int((-(cmp_c - 4 * p)) % 8) for p in range(2))
    n_tiles = (cmp_c + n_cmp_rows + 8 + SEL_BLOCK - 1) // SEL_BLOCK

    def cmp_dist(lead, tile, row, qi):
        e = SEL_BLOCK * tile + row - jnp.where(lead[0] == 0, shifts[0], shifts[1])
        dist = qi - CMP_STRIDE * e + (CMP_STRIDE * cmp_c - (CMP_BLOCK - 1))
        return dist, (dist >= 0) & (e >= 0)

    ebank = _bank_call(rows, (2,), n_tiles, cmp_dist, "bias_bank_cmp")
    ebank = ebank.reshape(2, N_KV, n_tiles * SEL_BLOCK, GQ)
    return ebank, sbank, wbank, far, shifts


def _overlap_t(t):
    n_cmp_rows = t // CMP_STRIDE
    n_sb = t // SEL_BLOCK
    c_start = np.arange(n_cmp_rows)[None, :] * CMP_STRIDE
    s_start = np.arange(n_sb)[:, None] * SEL_BLOCK
    ov = np.clip(np.minimum(c_start + CMP_BLOCK, s_start + SEL_BLOCK)
                 - np.maximum(c_start, s_start), 0, None) / CMP_BLOCK
    ov[:, n_cmp_rows - 1] = 0.0
    return jnp.asarray(ov, dtype=BF16)


def _layer(x, c_pad, w_in, q_norm, k_norm, cmp_pe_k, cmp_w1_k, cmp_w2_k, cmp_pe_v, cmp_w1_v,
           cmp_w2_v, rel_bias, conv_w, w_out, norm1, norm2, w_ada, b_ada, w_ff1, w_ff2):
    b, t, d = x.shape
    scale = HEAD_DIM ** -0.5

    mod = _mod_call(c_pad, w_ada, b_ada)[:b].reshape(b, N_MOD, d)

    qn_col = (jnp.tile(q_norm, N_HEADS) * (scale * LOG2E)).reshape(ATTN_W, 1)
    kn_t = jnp.tile(k_norm, N_KV).reshape(1, KV_W)
    bdq = _block_diag_ones(ATTN_W)
    bdk = _block_diag_ones(KV_W)
    w_nat, w_tr = _pack_w_in(w_in)
    qt, kc_raw, vc_raw, ks, vst, kw, vwt, gt, conv = _inproj_call(
        x, mod, norm1.reshape(1, d), w_nat, w_tr, qn_col, kn_t, conv_w, bdq, bdk)

    n_cmp_rows = t // CMP_STRIDE
    kc, vct = _compress_call(
        kc_raw, vc_raw, _expand_pe(cmp_pe_k), _expand_pe(cmp_pe_v), _expand_w1(cmp_w1_k), _expand_w1(cmp_w1_v),
        _expand_w2(cmp_w2_k), _expand_w2(cmp_w2_v).T, kn_t, bdk)

    ebank, sbank, wbank, far, eb_shift = _bias_banks(rel_bias, t)
    attn = _attn_call(qt, kc, vct, ks, vst, kw, vwt, gt, _overlap_t(t), ebank, sbank.astype(BF16),
                      wbank.astype(BF16), far, eb_shift)

    return _ffn_call(x, attn, conv, mod, norm2.reshape(1, d), w_out.astype(BF16),
                     w_ff1.astype(BF16), w_ff2.astype(BF16))


def kernel(x, c, w_in, q_norm, k_norm, cmp_pe_k, cmp_w1_k, cmp_w2_k, cmp_pe_v, cmp_w1_v, cmp_w2_v,
           rel_bias, conv_w, w_out, norm1, norm2, w_ada, b_ada, w_ff1, w_ff2):
    b = x.shape[0]
    c_pad = jnp.pad(c, ((0, (-b) % 8), (0, 0)))
    for l in range(w_in.shape[0]):
        x = _layer(x, c_pad, w_in[l], q_norm[l], k_norm[l], cmp_pe_k[l], cmp_w1_k[l], cmp_w2_k[l],
                   cmp_pe_v[l], cmp_w1_v[l], cmp_w2_v[l], rel_bias, conv_w[l], w_out[l],
                   norm1[l], norm2[l], w_ada[l], b_ada[l], w_ff1[l], w_ff2[l])
    return x
```

```python
import functools
import math

import numpy as np
import jax
import jax.numpy as jnp
from jax import lax
from jax.experimental import pallas as pl
from jax.experimental.pallas import tpu as pltpu

HEAD_DIM = 64
N_HEADS = 8
N_KV = 2
GQA = N_HEADS // N_KV
ATTN_W = N_HEADS * HEAD_DIM
KV_W = N_KV * HEAD_DIM
CONV_K = 3
CMP_BLOCK = 32
CMP_STRIDE = 16
SEL_BLOCK = 64
N_SELECT = 16
WINDOW = 512
Q_BLOCK = 64
REL_BUCKETS = 32
REL_MAX_DIST = 1024
N_MOD = 6
N_BRANCH = 3
EPS = 1e-6
NEG = -1e30

LANES = 128
SUBLANES = 8
GQ = GQA * Q_BLOCK
KEY_CHUNK = 256
BLOCKS_PER_CHUNK = KEY_CHUNK // SEL_BLOCK
V_TILE = 128
BF16_ROWS = 16
V_ROWS = HEAD_DIM + BF16_ROWS
V_TILE_ROWS = N_KV * V_ROWS
CMP_ROW_STEP = 128
WIN_BLOCKS = WINDOW // SEL_BLOCK
WIN_TILES = WINDOW // V_TILE + 1
NEAR_TILES = (REL_MAX_DIST + Q_BLOCK - 1) // SEL_BLOCK + 1
ROW_TILE = 512
INPROJ_TILE = 1024
INPROJ_SPLIT = 4
FAR_CHUNKS = 4
NEAR_CHUNKS = 4
QK_AHEAD = 4
LOG2E = math.log2(math.e)
FF_CHUNK = 1024
VMEM_LIMIT = 56 * 1024 * 1024

F32 = jnp.float32
BF16 = jnp.bfloat16
_NT = (((1,), (1,)), ((), ()))


def _cparams(n_axes):
    return pltpu.CompilerParams(dimension_semantics=("arbitrary",) * n_axes,
                                vmem_limit_bytes=VMEM_LIMIT)


def _with_ones_rows(vt):
    ones = jnp.ones((BF16_ROWS, vt.shape[1]), vt.dtype)
    parts = []
    for g in range(N_KV):
        parts += [vt[g * HEAD_DIM:(g + 1) * HEAD_DIM, :], ones]
    return jnp.concatenate(parts, axis=0)


def _swap_halves(p0, p1):
    low = lax.broadcasted_iota(jnp.int32, p0.shape, 1) < LANES // 2
    return (jnp.where(low, p0, pltpu.roll(p1, LANES // 2, 1)),
            jnp.where(low, pltpu.roll(p0, LANES // 2, 1), p1))


def _mod_kernel(c_ref, w_ref, b_ref, o_ref):
    c = c_ref[...]
    a = c * jax.nn.sigmoid(c)
    o_ref[...] = jnp.dot(a, w_ref[...], preferred_element_type=F32,
                         precision=lax.Precision.HIGHEST) + b_ref[...]


def _mod_call(c_pad, w_ada, b_ada):
    rows, d = c_pad.shape
    n = w_ada.shape[1]
    tn = n // N_MOD
    return pl.pallas_call(
        _mod_kernel,
        grid=(n // tn,),
        in_specs=[pl.BlockSpec((rows, d), lambda j: (0, 0)),
                  pl.BlockSpec((d, tn), lambda j: (0, j)),
                  pl.BlockSpec((1, tn), lambda j: (0, j))],
        out_specs=pl.BlockSpec((rows, tn), lambda j: (0, j)),
        out_shape=jax.ShapeDtypeStruct((rows, n), F32),
        compiler_params=_cparams(1),
        name="adaln_mod",
    )(c_pad, w_ada, b_ada.reshape(1, n))


_N_KC, _N_VC, _N_KS, _N_KW, _N_CONV = 0, KV_W, 2 * KV_W, 3 * KV_W, 4 * KV_W
_T_Q, _T_VS, _T_VW, _T_G = 0, ATTN_W, ATTN_W + KV_W, ATTN_W + 2 * KV_W
_G_ROWS = GQA * SUBLANES


def _inproj_kernel(x_ref, mod_ref, n1_ref, wn_ref, wt_ref, qn_ref, kn_ref, cw_ref, bdq_ref, bdk_ref,
                   qt_out, kc_out, vc_out, ks_out, vst_out, kw_out, vwt_out, gt_out, conv_out,
                   carry, *, conv_w):
    t = pl.program_id(1)
    tm = x_ref.shape[1]
    sub = tm // INPROJ_SPLIT

    @pl.when(t == 0)
    def _():
        carry[...] = jnp.zeros_like(carry)

    prev2, prev1 = carry[6:7, :], carry[7:8, :]
    c_bg = _N_CONV + conv_w
    c_u = c_bg + conv_w

    for part in range(INPROJ_SPLIT):
        rows = slice(part * sub, (part + 1) * sub)
        x = x_ref[0, rows, :]
        ms = jnp.mean(x * x, axis=-1, keepdims=True)
        y = x * lax.rsqrt(ms + EPS) * n1_ref[...]
        h = (y * (1.0 + mod_ref[0, 1:2, :]) + mod_ref[0, 0:1, :]).astype(BF16)

        def proj(a, b):
            return jnp.dot(h, wn_ref[:, a:b], preferred_element_type=F32)

        def proj_t(a, b):
            return lax.dot_general(wt_ref[a:b, :], h, _NT, preferred_element_type=F32)

        def head_norm(v, gain):
            ssq = jnp.dot((v * v).astype(BF16), bdk_ref[...], preferred_element_type=F32)
            return v * lax.rsqrt(ssq * (1.0 / HEAD_DIM) + EPS) * gain

        qf = proj_t(_T_Q, _T_VS)
        vg = proj_t(_T_VS, _T_G + _G_ROWS)
        kv = proj(_N_KC, _N_CONV)
        cv = proj(_N_CONV, c_u + conv_w)

        ssq = jnp.dot(bdq_ref[...], (qf * qf).astype(BF16), preferred_element_type=F32)
        qf = qf * lax.rsqrt(ssq * (1.0 / HEAD_DIM) + EPS) * qn_ref[...]
        gf = jax.nn.sigmoid(vg[2 * KV_W:, :])
        for c in range(sub // LANES):
            blk = (part * sub) // Q_BLOCK + 2 * c
            cols = slice(c * LANES, (c + 1) * LANES)
            for g in range(N_KV):
                pc = [qf[(g * GQA + r) * HEAD_DIM:(g * GQA + r + 1) * HEAD_DIM, cols]
                      for r in range(GQA)]
                lo01, hi01 = _swap_halves(pc[0], pc[1])
                lo23, hi23 = _swap_halves(pc[2], pc[3])
                qt_out[0, blk, g] = jnp.concatenate([lo01, lo23], axis=1).astype(BF16)
                qt_out[0, blk + 1, g] = jnp.concatenate([hi01, hi23], axis=1).astype(BF16)
            pc = [gf[r * SUBLANES:(r + 1) * SUBLANES, cols] for r in range(GQA)]
            lo01, hi01 = _swap_halves(pc[0], pc[1])
            lo23, hi23 = _swap_halves(pc[2], pc[3])
            gt_out[0, blk] = jnp.concatenate([lo01, lo23], axis=1)
            gt_out[0, blk + 1] = jnp.concatenate([hi01, hi23], axis=1)

        vs_f = _with_ones_rows(vg[:KV_W, :].astype(BF16))
        vw_f = _with_ones_rows(vg[KV_W:2 * KV_W, :].astype(BF16))
        for j in range(sub // KEY_CHUNK):
            vst_out[0, (part * sub) // KEY_CHUNK + j] = vs_f[:, j * KEY_CHUNK:(j + 1) * KEY_CHUNK]
        for j in range(sub // V_TILE):
            vwt_out[0, (part * sub) // V_TILE + j] = vw_f[:, j * V_TILE:(j + 1) * V_TILE]

        kc_out[0, rows, :] = kv[:, _N_KC:_N_VC]
        vc_out[0, rows, :] = kv[:, _N_VC:_N_KS]
        ks_out[0, rows, :] = head_norm(kv[:, _N_KS:_N_KW], kn_ref[...]).astype(BF16)
        kw_out[0, rows, :] = head_norm(kv[:, _N_KW:_N_CONV], kn_ref[...]).astype(BF16)

        z = cv[:, :conv_w] * cv[:, 2 * conv_w:]
        row = lax.broadcasted_iota(jnp.int32, z.shape, 0)
        z1 = jnp.where(row == 0, prev1, pltpu.roll(z, 1, 0))
        z2 = jnp.where(row == 0, prev2, jnp.where(row == 1, prev1, pltpu.roll(z, 2, 0)))
        zc = cw_ref[0:1, :] * z2 + cw_ref[1:2, :] * z1 + cw_ref[2:3, :] * z
        conv_out[0, rows, :] = (cv[:, conv_w:2 * conv_w] * zc).astype(BF16)
        prev2, prev1 = z[sub - 2:sub - 1, :], z[sub - 1:sub, :]
        if part == INPROJ_SPLIT - 1:
            carry[...] = z[sub - SUBLANES:sub, :]


def _inproj_call(x, mod, norm1, w_nat, w_tr, qn_col, kn_t, conv_w, bdq, bdk):
    b, t, d = x.shape
    tm = min(INPROJ_TILE, t)
    cw = conv_w.shape[1]
    nq = t // Q_BLOCK
    row_spec = lambda w: pl.BlockSpec((1, tm, w), lambda i, j: (i, j, 0))
    full = lambda shp: pl.BlockSpec(shp, lambda i, j: (0,) * len(shp))
    vt_spec = lambda w: pl.BlockSpec((1, tm // w, V_TILE_ROWS, w), lambda i, j: (i, j, 0, 0))
    vt_shape = lambda w: jax.ShapeDtypeStruct((b, t // w, V_TILE_ROWS, w), BF16)
    kv = lambda dt: jax.ShapeDtypeStruct((b, t, KV_W), dt)
    out_specs = [pl.BlockSpec((1, tm // Q_BLOCK, N_KV, HEAD_DIM, GQ), lambda i, j: (i, j, 0, 0, 0)),
                 row_spec(KV_W), row_spec(KV_W), row_spec(KV_W), vt_spec(KEY_CHUNK), row_spec(KV_W),
                 vt_spec(V_TILE),
                 pl.BlockSpec((1, tm // Q_BLOCK, SUBLANES, GQ), lambda i, j: (i, j, 0, 0)),
                 row_spec(cw)]
    out_shape = [jax.ShapeDtypeStruct((b, nq, N_KV, HEAD_DIM, GQ), BF16),
                 kv(F32), kv(F32), kv(BF16), vt_shape(KEY_CHUNK), kv(BF16), vt_shape(V_TILE),
                 jax.ShapeDtypeStruct((b, nq, SUBLANES, GQ), F32),
                 jax.ShapeDtypeStruct((b, t, cw), BF16)]
    return pl.pallas_call(
        functools.partial(_inproj_kernel, conv_w=cw),
        grid=(b, t // tm),
        in_specs=[row_spec(d),
                  pl.BlockSpec((1, N_MOD, d), lambda i, j: (i, 0, 0)),
                  full((1, d)), full(w_nat.shape), full(w_tr.shape), full((ATTN_W, 1)),
                  full((1, KV_W)), full((CONV_K, cw)), full((ATTN_W, ATTN_W)), full((KV_W, KV_W))],
        out_specs=out_specs,
        out_shape=out_shape,
        scratch_shapes=[pltpu.VMEM((SUBLANES, cw), F32)],
        compiler_params=_cparams(2),
        name="inproj",
    )(x, mod, norm1, w_nat, w_tr, qn_col, kn_t, conv_w, bdq, bdk)


def _compress_kernel(kx_ref, vx_ref, pek_ref, pev_ref, w1k_ref, w1v_ref, w2k_ref, w2vt_ref,
                     kn_ref, bdk_ref, kc_out, vct_out):
    def hidden(x_ref, pe_ref, w1_ref):
        n = x_ref.shape[1] // CMP_STRIDE
        u = jnp.zeros((n, w1_ref.shape[2]), F32)
        v = jnp.zeros((n, w1_ref.shape[2]), F32)
        for r in range(0, CMP_STRIDE, 2):
            tok = [x_ref[0, pl.ds(r + d, n, stride=CMP_STRIDE), :] for d in range(2)]
            cols = slice(r * KV_W, (r + 2) * KV_W)
            for a, acc in ((0, "u"), (1, "v")):
                lhs = jnp.concatenate([tok[d] + pe_ref[a:a + 1, (r + d) * KV_W:(r + d + 1) * KV_W]
                                       for d in range(2)], axis=1).astype(BF16)
                prod = jnp.dot(lhs, w1_ref[a, cols, :], preferred_element_type=F32)
                if acc == "u":
                    u = u + prod
                else:
                    v = v + prod
        hid = u + pltpu.roll(v, n - 1, 0)
        return jax.nn.gelu(hid, approximate=True).astype(BF16)

    kc = jnp.dot(hidden(kx_ref, pek_ref, w1k_ref), w2k_ref[...], preferred_element_type=F32)
    ssq = jnp.dot((kc * kc).astype(BF16), bdk_ref[...], preferred_element_type=F32)
    kc_out[0] = (kc * lax.rsqrt(ssq * (1.0 / HEAD_DIM) + EPS) * kn_ref[...]).astype(BF16)
    vct_out[0] = _with_ones_rows(lax.dot_general(w2vt_ref[...], hidden(vx_ref, pev_ref, w1v_ref), _NT,
                                                 preferred_element_type=F32).astype(BF16))


def _compress_call(kx, vx, pek, pev, w1k, w1v, w2k, w2vt, kn_t, bdk):
    b, t, _ = kx.shape
    nrow = t // CMP_STRIDE
    wide, hid2 = w1k.shape[1], w1k.shape[2]
    full = lambda shp: pl.BlockSpec(shp, lambda i: (0,) * len(shp))
    xs = pl.BlockSpec((1, t, KV_W), lambda i: (i, 0, 0))
    return pl.pallas_call(
        _compress_kernel,
        grid=(b,),
        in_specs=[xs, xs, full((2, wide)), full((2, wide)), full((2, wide, hid2)),
                  full((2, wide, hid2)), full((hid2, KV_W)), full((KV_W, hid2)),
                  full((1, KV_W)), full((KV_W, KV_W))],
        out_specs=[pl.BlockSpec((1, nrow, KV_W), lambda i: (i, 0, 0)),
                   pl.BlockSpec((1, V_TILE_ROWS, nrow), lambda i: (i, 0, 0))],
        out_shape=[jax.ShapeDtypeStruct((b, nrow, KV_W), BF16),
                   jax.ShapeDtypeStruct((b, V_TILE_ROWS, nrow), BF16)],
        compiler_params=_cparams(1),
        name="compress",
    )(kx, vx, pek, pev, w1k, w1v, w2k, w2vt, kn_t, bdk)


def _attn_kernel(q_ref, kc_ref, vct_ref, ks_ref, vst_ref, kw_ref, vwt_ref, gt_ref, ovt_ref,
                 eb_ref, sb_ref, wb_ref, far_ref, o_ref, sel_scr, acc_scr, out_scr, qp_scr, s_scr,
                 fin_scr, *, n_cmp_rows, eb_shift):
    n_sb = sel_scr.shape[1]
    ci = jnp.minimum(pl.program_id(1), n_sb - 1)
    par = lax.rem(ci, 2)
    n_total = ks_ref.shape[1] // KEY_CHUNK
    vsl = lambda g: slice(g * V_ROWS, (g + 1) * V_ROWS)

    @pl.when((pl.program_id(0) == 0) & (pl.program_id(1) == 0))
    def _():
        def zero(g, carry):
            fin_scr[g] = jnp.zeros(fin_scr.shape[1:], F32)
            sel_scr[g] = jnp.zeros(sel_scr.shape[1:], F32)
            return carry
        lax.fori_loop(0, N_KV, zero, 0)

    def chunk_qk(g, c):
        kk = ks_ref[0, pl.ds(pl.multiple_of(c * KEY_CHUNK, KEY_CHUNK), KEY_CHUNK), :]
        return jnp.dot(kk, qp_scr[g], preferred_element_type=F32).astype(BF16)

    n_chunks = ci // BLOCKS_PER_CHUNK + 1
    far_groups = jnp.maximum(ci - (NEAR_TILES - 1), 0) // (BLOCKS_PER_CHUNK * FAR_CHUNKS)
    n_wide = far_groups // 2
    n_far = far_groups - 2 * n_wide
    c_far = n_wide * (2 * FAR_CHUNKS)
    c0 = far_groups * FAR_CHUNKS
    n_near = (n_chunks - c0 + NEAR_CHUNKS - 1) // NEAR_CHUNKS

    def near_units(i):
        return [(g, jnp.minimum(c0 + NEAR_CHUNKS * i + j, n_total - 1))
                for g in range(N_KV) for j in range(NEAR_CHUNKS)]

    def far_units(i):
        return [(g, c_far + i * FAR_CHUNKS + j) for g in range(N_KV) for j in range(FAR_CHUNKS)]

    def wide_units(i):
        return [(g, i * 2 * FAR_CHUNKS + j) for g in range(N_KV) for j in range(2 * FAR_CHUNKS)]

    def pick_units(options):
        units = options[-1][1][:QK_AHEAD]
        for cond, cand in reversed(options[:-1]):
            units = [(jnp.where(cond, g_a, g_b), jnp.where(cond, c_a, c_b))
                     for (g_a, c_a), (g_b, c_b) in zip(cand[:QK_AHEAD], units)]
        return [(g, jnp.minimum(c, n_total - 1)) for g, c in units]

    def prefetch_scores(units):
        for k in range(QK_AHEAD):
            s_scr[k] = chunk_qk(*units[k])

    def before_loops(rows):
        pieces = []
        for g in range(N_KV):
            for half in range(GQ // LANES):
                a = fin_scr[g, :, half * LANES:(half + 1) * LANES]
                stacked = jnp.concatenate([a, pltpu.roll(a, Q_BLOCK, 1)], axis=0)
                pieces.append(stacked.T[:Q_BLOCK, :])
        o_ref[0] = jnp.concatenate(pieces, axis=1).astype(BF16)

        zeros_q = jnp.zeros((HEAD_DIM, GQ), BF16)
        qp_scr[0] = jnp.concatenate([q_ref[0, 0, 0], zeros_q], axis=0)
        qp_scr[1] = jnp.concatenate([zeros_q, q_ref[0, 0, 1]], axis=0)

        nsel = min(n_sb, rows * CMP_STRIDE // SEL_BLOCK)
        lane = lax.broadcasted_iota(jnp.int32, (nsel, LANES), 1)
        cmp_c = n_cmp_rows - 4
        e0 = cmp_c - 4 * ci + jnp.where(par == 0, eb_shift[0], eb_shift[1])
        e0 = pl.multiple_of(e0, 8)
        o_cmp = []
        imp = []
        w0 = ci // 2 - (WIN_TILES - 1)
        win_tiles = [jnp.maximum(w0 + j, 0) for j in range(WIN_TILES)]
        s_cmp = [jnp.dot(kc_ref[0, :rows, :], qp_scr[g], preferred_element_type=F32)
                 for g in range(N_KV)]
        for g in range(N_KV):
            sc = s_cmp[g] + eb_ref[par, g, pl.ds(e0, rows), :]
            m = jnp.max(sc, axis=0, keepdims=True)
            e = jnp.exp2(sc - m).astype(BF16)
            ov = jnp.dot(vct_ref[0, vsl(g), :rows], e, preferred_element_type=F32)
            inv = jnp.where(m > 0.5 * NEG, 1.0 / ov[HEAD_DIM:HEAD_DIM + 1, :], 0.0)
            o_cmp.append(ov[:HEAD_DIM, :] * inv)
            ir = jnp.dot(ovt_ref[:nsel, :rows], e, preferred_element_type=F32) * inv
            a = ir[:, :LANES] + ir[:, LANES:]
            imp.append(a + pltpu.roll(a, Q_BLOCK, 1))

        s_win = [[jnp.dot(kw_ref[0, pl.ds(pl.multiple_of(tj * V_TILE, V_TILE), V_TILE), :],
                          qp_scr[g], preferred_element_type=F32).astype(BF16) for tj in win_tiles]
                 for g in range(N_KV)]

        prefetch_scores(pick_units([(n_wide > 0, wide_units(0)), (n_far > 0, far_units(0)),
                                    (None, near_units(0))]))

        jidx = lax.broadcasted_iota(jnp.int32, (nsel, LANES), 0)
        jf = jidx.astype(F32)
        valid = jidx <= ci
        forced = (jidx == 0) | (jidx == ci) | (jidx == ci - 1)
        score = jnp.where(valid, jnp.where(forced, -2.0, jnp.where(lane < Q_BLOCK, imp[0], imp[1])),
                          -1.0)
        sel = jnp.where(forced, 1.0, 0.0)

        def first_max(score):
            pairs = [(score[r:r + SUBLANES], jf[r:r + SUBLANES]) for r in range(0, nsel, SUBLANES)]
            while len(pairs) > 1:
                nxt = []
                for (va, ia), (vb, ib) in zip(pairs[0::2], pairs[1::2]):
                    keep = va >= vb
                    nxt.append((jnp.where(keep, va, vb), jnp.where(keep, ia, ib)))
                pairs = nxt + pairs[len(pairs) - len(pairs) % 2:]
            v8, i8 = pairs[0]
            mx = jnp.max(v8, axis=0, keepdims=True)
            return jnp.min(jnp.where(v8 == mx, i8, float(nsel)), axis=0, keepdims=True)

        for _ in range(min(N_SELECT, nsel) - 3):
            first = first_max(score)
            hit = jf == first
            sel = jnp.where(hit, 1.0, sel)
            score = jnp.where(hit, -2.0, score)
        selneg = jnp.where((sel > 0.5) & valid, 0.0, NEG)
        swapped = pltpu.roll(selneg, Q_BLOCK, 1)
        left = jnp.where(lane < Q_BLOCK, selneg, swapped)
        right = jnp.where(lane < Q_BLOCK, swapped, selneg)
        sel_scr[0, :nsel] = jnp.concatenate([left, left], axis=1)
        sel_scr[1, :nsel] = jnp.concatenate([right, right], axis=1)

        for g in range(N_KV):
            parts = []
            for i in range(2 * WIN_TILES):
                delta = par + WIN_BLOCKS - i
                ok = (delta >= 0) & (delta <= WIN_BLOCKS) & (delta <= ci)
                tile = jnp.where(ok, delta, WIN_BLOCKS + 1)
                half = s_win[g][i // 2][(i % 2) * SEL_BLOCK:(i % 2 + 1) * SEL_BLOCK, :]
                parts.append(half + wb_ref[g, tile])
            s = jnp.concatenate(parts, axis=0)
            m = jnp.max(s, axis=0, keepdims=True)
            pb = jnp.exp2(s - m)
            o_win = jnp.zeros((V_ROWS, GQ), F32)
            for j, tj in enumerate(win_tiles):
                o_win = o_win + jnp.dot(vwt_ref[0, tj, vsl(g), :], pb[j * V_TILE:(j + 1) * V_TILE, :],
                                        preferred_element_type=F32)
            w_scale = (gt_ref[0, 0, 2 * N_KV + g:2 * N_KV + g + 1, :]
                       * (1.0 / o_win[HEAD_DIM:HEAD_DIM + 1, :]))
            out_scr[g] = gt_ref[0, 0, g:g + 1, :] * o_cmp[g] + w_scale * o_win[:HEAD_DIM, :]
            acc_scr[g] = jnp.zeros((V_ROWS, GQ), F32)

    row_steps = [r for r in range(CMP_ROW_STEP, n_cmp_rows + 1, CMP_ROW_STEP)] or [n_cmp_rows]
    variant = jnp.minimum((4 * ci + 2) // CMP_ROW_STEP, len(row_steps) - 1)
    lax.switch(variant, [functools.partial(before_loops, r) for r in row_steps])

    def chunk_softmax(s, g, c, near, extra):
        parts = []
        for i in range(BLOCKS_PER_CHUNK):
            kb = c * BLOCKS_PER_CHUNK + i
            blk = s[i * SEL_BLOCK:(i + 1) * SEL_BLOCK, :]
            mrow = sel_scr[g, pl.ds(kb, 1), :]
            if extra is not None:
                mrow = mrow + extra
            if near:
                tile = jnp.clip(ci - kb, 0, NEAR_TILES)
                blk = blk + sb_ref[g, tile] + mrow.astype(BF16)
            else:
                blk = blk + (mrow + far_ref[g]).astype(BF16)
            parts.append(blk)
        s = jnp.concatenate(parts, axis=0)
        m_c = jnp.max(s, axis=0, keepdims=True)
        return m_c.astype(F32), jnp.exp2(s - m_c)

    def chunk_pv(pb, g, c):
        return jnp.dot(vst_ref[0, c, vsl(g), :], pb, preferred_element_type=F32)

    def merge_step(units, extras, near, next_units, carry):
        scores = {k: s_scr[k] for k in range(QK_AHEAD)}
        results = []
        for k, (g, c) in enumerate(units):
            m_c, pb = chunk_softmax(scores.pop(k), g, c, near, extras[k])
            if k + QK_AHEAD < len(units):
                scores[k + QK_AHEAD] = chunk_qk(*units[k + QK_AHEAD])
            else:
                j = k + QK_AHEAD - len(units)
                s_scr[j] = chunk_qk(*next_units[j])
            results.append((m_c, chunk_pv(pb, g, c)))
        per_g = len(units) // N_KV
        new = []
        for g in range(N_KV):
            m_run = carry[g]
            stats = results[g * per_g:(g + 1) * per_g]
            m_new = m_run
            for m_c, _ in stats:
                m_new = jnp.maximum(m_new, m_c)
            acc = jnp.exp2(m_run - m_new) * acc_scr[g]
            for m_c, pv in stats:
                acc = acc + jnp.exp2(m_c - m_new) * pv
            acc_scr[g] = acc
            new.append(m_new)
        return tuple(new)

    def wide_step(i, carry):
        units = wide_units(i)
        nxt = pick_units([(i + 1 < n_wide, wide_units(i + 1)), (n_far > 0, far_units(0)),
                          (None, near_units(0))])
        return merge_step(units, [None] * len(units), False, nxt, carry)

    def far_step(i, carry):
        units = far_units(i)
        return merge_step(units, [None] * len(units), False, pick_units([(None, near_units(0))]), carry)

    def near_step(i, carry):
        units = near_units(i)
        extras = [None if j == 0 else jnp.where(c0 + NEAR_CHUNKS * i + j < n_chunks, 0.0, NEG)
                  for _ in range(N_KV) for j in range(NEAR_CHUNKS)]
        return merge_step(units, extras, True, near_units(i + 1), carry)

    init = (jnp.full((1, GQ), NEG, F32),) * N_KV
    carry = lax.fori_loop(0, n_wide, wide_step, init)
    carry = lax.fori_loop(0, n_far, far_step, carry)
    lax.fori_loop(0, n_near, near_step, carry)

    for g in range(N_KV):
        acc = acc_scr[g]
        scale = gt_ref[0, 0, N_KV + g:N_KV + g + 1, :] * (1.0 / acc[HEAD_DIM:HEAD_DIM + 1, :])
        fin_scr[g] = out_scr[g] + scale * acc[:HEAD_DIM, :]


def _attn_call(qt, kc, vct, ks, vst, kw, vwt, gt, ovt, ebank, sbank, wbank, far, eb_shift):
    b, nq = qt.shape[0], qt.shape[1]
    t = ks.shape[1]
    n_cmp_rows = kc.shape[1]
    n_sb = t // SEL_BLOCK
    per_b = lambda shp: pl.BlockSpec((1,) + shp[1:], lambda i, j: (i,) + (0,) * (len(shp) - 1))
    per_q = lambda shp: pl.BlockSpec(
        (1, 1) + shp[2:], lambda i, j: (i, jnp.minimum(j, nq - 1)) + (0,) * (len(shp) - 2))
    full = lambda shp: pl.BlockSpec(shp, lambda i, j: (0,) * len(shp))
    args = (qt, kc, vct, ks, vst, kw, vwt, gt, ovt, ebank, sbank, wbank, far)
    specs = [per_q(qt.shape), per_b(kc.shape), per_b(vct.shape), per_b(ks.shape), per_b(vst.shape),
             per_b(kw.shape), per_b(vwt.shape), per_q(gt.shape), full(ovt.shape),
             full(ebank.shape), full(sbank.shape), full(wbank.shape), full(far.shape)]
    acc_like = pltpu.VMEM((N_KV, HEAD_DIM, GQ), F32)
    return pl.pallas_call(
        functools.partial(_attn_kernel, n_cmp_rows=n_cmp_rows, eb_shift=eb_shift),
        grid=(b, nq + 1),
        in_specs=specs,
        out_specs=pl.BlockSpec((1, Q_BLOCK, ATTN_W), lambda i, j: (i, jnp.maximum(j - 1, 0), 0)),
        out_shape=jax.ShapeDtypeStruct((b, t, ATTN_W), BF16),
        scratch_shapes=[pltpu.VMEM((N_KV, n_sb, GQ), F32),
                        pltpu.VMEM((N_KV, V_ROWS, GQ), F32),
                        acc_like,
                        pltpu.VMEM((N_KV, KV_W, GQ), BF16),
                        pltpu.VMEM((QK_AHEAD, KEY_CHUNK, GQ), BF16),
                        acc_like],
        compiler_params=_cparams(2),
        name="nsa_attention",
    )(*args)


def _ffn_kernel(x_ref, a_ref, c_ref, mod_ref, n2_ref, wo_ref, w1_ref, w2_ref, o_ref):
    aw = a_ref.shape[2]
    mix = jnp.dot(a_ref[0], wo_ref[0:aw, :], preferred_element_type=F32)
    mix = mix + jnp.dot(c_ref[0], wo_ref[aw:, :], preferred_element_type=F32)
    x1 = x_ref[0] + mod_ref[0, 2:3, :] * mix
    ms = jnp.mean(x1 * x1, axis=-1, keepdims=True)
    y = x1 * lax.rsqrt(ms + EPS) * n2_ref[...]
    h2 = (y * (1.0 + mod_ref[0, 4:5, :]) + mod_ref[0, 3:4, :]).astype(BF16)
    d_ff = w1_ref.shape[1]
    ff = jnp.zeros(x1.shape, F32)
    for j in range(d_ff // FF_CHUNK):
        a = jnp.dot(h2, w1_ref[:, j * FF_CHUNK:(j + 1) * FF_CHUNK], preferred_element_type=F32)
        a = jnp.maximum(a, 0.0)
        ff = ff + jnp.dot((a * a).astype(BF16), w2_ref[j * FF_CHUNK:(j + 1) * FF_CHUNK, :],
                          preferred_element_type=F32)
    o_ref[0] = x1 + mod_ref[0, 5:6, :] * ff


def _ffn_call(x, attn, conv, mod, norm2, w_out, w_ff1, w_ff2):
    b, t, d = x.shape
    tm = min(ROW_TILE, t)
    row_spec = lambda w: pl.BlockSpec((1, tm, w), lambda i, j: (i, j, 0))
    full = lambda shp: pl.BlockSpec(shp, lambda i, j: (0,) * len(shp),
                                    pipeline_mode=pl.Buffered(1))
    return pl.pallas_call(
        _ffn_kernel,
        grid=(b, t // tm),
        in_specs=[row_spec(d), row_spec(attn.shape[2]), row_spec(conv.shape[2]),
                  pl.BlockSpec((1, N_MOD, d), lambda i, j: (i, 0, 0)),
                  full((1, d)), full(w_out.shape), full(w_ff1.shape), full(w_ff2.shape)],
        out_specs=row_spec(d),
        out_shape=jax.ShapeDtypeStruct((b, t, d), F32),
        compiler_params=_cparams(2),
        name="outproj_mlp",
    )(x, attn, conv, mod, norm2, w_out, w_ff1, w_ff2)


def _block_diag_ones(n):
    idx = np.arange(n) // HEAD_DIM
    return jnp.asarray(idx[:, None] == idx[None, :], dtype=BF16)


def _pack_w_in(w_in):
    d = w_in.shape[0]
    conv_w = d - ATTN_W
    sizes = [ATTN_W] + [KV_W] * 6 + [N_BRANCH * N_HEADS] + [conv_w] * 3
    offs = np.concatenate([[0], np.cumsum(sizes)])
    part = lambda i: w_in[:, offs[i]:offs[i + 1]]
    q, kc, vc, ks, vs, kw, vw, g, cgate, bgate, u = (part(i) for i in range(11))
    w_nat = jnp.concatenate([kc, vc, ks, kw, cgate, bgate, u], axis=1).astype(BF16)
    gt = g.reshape(d, N_KV, GQA, N_BRANCH).transpose(2, 3, 1, 0).reshape(GQA, N_BRANCH * N_KV, d)
    gt = jnp.pad(gt, ((0, 0), (0, SUBLANES - N_BRANCH * N_KV), (0, 0))).reshape(_G_ROWS, d)
    w_tr = jnp.concatenate([q.T, vs.T, vw.T, gt], axis=0).astype(BF16)
    return w_nat, w_tr


def _expand_w1(w1):
    hid = w1.shape[1]
    w = w1.reshape(2, CMP_STRIDE, HEAD_DIM, hid).astype(BF16)
    zero = jnp.zeros_like(w)
    per_group = [jnp.concatenate([w if k == g else zero for k in range(N_KV)], axis=-1)
                 for g in range(N_KV)]
    return jnp.stack(per_group, axis=2).reshape(2, CMP_STRIDE * KV_W, N_KV * hid)


def _expand_w2(w2):
    hid = w2.shape[0]
    eye = jnp.eye(N_KV, dtype=w2.dtype).reshape(N_KV, 1, N_KV, 1)
    return (w2.reshape(1, hid, 1, HEAD_DIM) * eye).reshape(N_KV * hid, KV_W).astype(BF16)


def _expand_pe(pe):
    p = pe.reshape(2, CMP_STRIDE, 1, HEAD_DIM)
    return jnp.broadcast_to(p, (2, CMP_STRIDE, N_KV, HEAD_DIM)).reshape(2, CMP_STRIDE * KV_W)


def _bucket_thresholds():
    n = np.arange(2 * REL_MAX_DIST)
    max_exact = REL_BUCKETS // 2
    nf = np.maximum(n, max_exact).astype(np.float32)
    ratio = np.log(nf / np.float32(max_exact)) / np.float32(math.log(REL_MAX_DIST / max_exact))
    large = max_exact + (ratio * np.float32(REL_BUCKETS - max_exact)).astype(np.int32)
    table = np.where(n < max_exact, n, np.minimum(large, REL_BUCKETS - 1))
    return tuple(int(np.searchsorted(table, k, side="left")) for k in range(REL_BUCKETS))


def _bank_call(bias_rows, lead, n_tiles, dist_fn, name):
    nl = len(lead)
    thr = _bucket_thresholds()

    def body(rows_ref, o_ref):
        lead_ids = [pl.program_id(a) for a in range(nl)]
        row = lax.broadcasted_iota(jnp.int32, (SEL_BLOCK, GQ), 0)
        qi = lax.broadcasted_iota(jnp.int32, (SEL_BLOCK, GQ), 1) & (Q_BLOCK - 1)

        def tile(t, carry):
            dist, ok = dist_fn(lead_ids, t, row, qi)
            v = jnp.broadcast_to(rows_ref[0, 0:1, :], (SEL_BLOCK, GQ))
            for k in range(1, REL_BUCKETS):
                v = jnp.where(dist >= thr[k], rows_ref[0, k:k + 1, :], v)
            o_ref[(0,) * (nl + 1) + (t,)] = jnp.where(ok, v, NEG)
            return carry

        lax.fori_loop(0, n_tiles, tile, 0)

    return pl.pallas_call(
        body,
        grid=tuple(lead) + (N_KV,),
        in_specs=[pl.BlockSpec((1, REL_BUCKETS, GQ), lambda *i: (i[nl], 0, 0))],
        out_specs=pl.BlockSpec((1,) * (nl + 1) + (n_tiles, SEL_BLOCK, GQ),
                               lambda *i: tuple(i) + (0, 0, 0)),
        out_shape=jax.ShapeDtypeStruct(tuple(lead) + (N_KV, n_tiles, SEL_BLOCK, GQ), F32),
        compiler_params=_cparams(nl + 1),
        name=name,
    )(bias_rows)


def _bias_banks(rel_bias, t):
    n_cmp_rows = t // CMP_STRIDE
    rows = rel_bias.reshape(REL_BUCKETS, N_KV, GQA).transpose(1, 0, 2)
    rows = jnp.repeat(rows, Q_BLOCK, axis=2) * LOG2E

    def sel_dist(lead, tile, row, qi):
        dist = SEL_BLOCK * tile + qi - row
        return dist, dist >= 0

    sbank = _bank_call(rows, (), NEAR_TILES + 1, sel_dist, "bias_bank_sel")
    far = sbank[:, NEAR_TILES, 0:1, :]

    def win_dist(lead, tile, row, qi):
        dist = SEL_BLOCK * tile + qi - row
        return dist, (dist >= 0) & (dist < WINDOW)

    wbank = _bank_call(rows, (), WIN_BLOCKS + 2, win_dist, "bias_bank_win")

    cmp_c = n_cmp_rows - 4
    shifts = tuple(int((-(cmp_c - 4 * p)) % 8) for p in range(2))
    n_tiles = (cmp_c + n_cmp_rows + 8 + SEL_BLOCK - 1) // SEL_BLOCK

    def cmp_dist(lead, tile, row, qi):
        e = SEL_BLOCK * tile + row - jnp.where(lead[0] == 0, shifts[0], shifts[1])
        dist = qi - CMP_STRIDE * e + (CMP_STRIDE * cmp_c - (CMP_BLOCK - 1))
        return dist, (dist >= 0) & (e >= 0)

    ebank = _bank_call(rows, (2,), n_tiles, cmp_dist, "bias_bank_cmp")
    ebank = ebank.reshape(2, N_KV, n_tiles * SEL_BLOCK, GQ)
    return ebank, sbank, wbank, far, shifts


def _overlap_t(t):
    n_cmp_rows = t // CMP_STRIDE
    n_sb = t // SEL_BLOCK
    c_start = np.arange(n_cmp_rows)[None, :] * CMP_STRIDE
    s_start = np.arange(n_sb)[:, None] * SEL_BLOCK
    ov = np.clip(np.minimum(c_start + CMP_BLOCK, s_start + SEL_BLOCK)
                 - np.maximum(c_start, s_start), 0, None) / CMP_BLOCK
    ov[:, n_cmp_rows - 1] = 0.0
    return jnp.asarray(ov, dtype=BF16)


def _layer(x, c_pad, w_in, q_norm, k_norm, cmp_pe_k, cmp_w1_k, cmp_w2_k, cmp_pe_v, cmp_w1_v,
           cmp_w2_v, rel_bias, conv_w, w_out, norm1, norm2, w_ada, b_ada, w_ff1, w_ff2):
    b, t, d = x.shape
    scale = HEAD_DIM ** -0.5

    mod = _mod_call(c_pad, w_ada, b_ada)[:b].reshape(b, N_MOD, d)

    qn_col = (jnp.tile(q_norm, N_HEADS) * (scale * LOG2E)).reshape(ATTN_W, 1)
    kn_t = jnp.tile(k_norm, N_KV).reshape(1, KV_W)
    bdq = _block_diag_ones(ATTN_W)
    bdk = _block_diag_ones(KV_W)
    w_nat, w_tr = _pack_w_in(w_in)
    qt, kc_raw, vc_raw, ks, vst, kw, vwt, gt, conv = _inproj_call(
        x, mod, norm1.reshape(1, d), w_nat, w_tr, qn_col, kn_t, conv_w, bdq, bdk)

    kc, vct = _compress_call(
        kc_raw, vc_raw, _expand_pe(cmp_pe_k), _expand_pe(cmp_pe_v), _expand_w1(cmp_w1_k), _expand_w1(cmp_w1_v),
        _expand_w2(cmp_w2_k), _expand_w2(cmp_w2_v).T, kn_t, bdk)

    ebank, sbank, wbank, far, eb_shift = _bias_banks(rel_bias, t)
    attn = _attn_call(qt, kc, vct, ks, vst, kw, vwt, gt, _overlap_t(t), ebank, sbank.astype(BF16),
                      wbank.astype(BF16), far, eb_shift)

    return _ffn_call(x, attn, conv, mod, norm2.reshape(1, d), w_out.astype(BF16),
                     w_ff1.astype(BF16), w_ff2.astype(BF16))


def kernel(x, c, w_in, q_norm, k_norm, cmp_pe_k, cmp_w1_k, cmp_w2_k, cmp_pe_v, cmp_w1_v, cmp_w2_v,
           rel_bias, conv_w, w_out, norm1, norm2, w_ada, b_ada, w_ff1, w_ff2):
    b = x.shape[0]
    c_pad = jnp.pad(c, ((0, (-b) % 8), (0, 0)))
    for l in range(w_in.shape[0]):
        x = _layer(x, c_pad, w_in[l], q_norm[l], k_norm[l], cmp_pe_k[l], cmp_w1_k[l], cmp_w2_k[l],
                   cmp_pe_v[l], cmp_w1_v[l], cmp_w2_v[l], rel_bias, conv_w[l], w_out[l],
                   norm1[l], norm2[l], w_ada[l], b_ada[l], w_ff1[l], w_ff2[l])
    return x
```

```python
import functools
import math

import numpy as np
import jax
import jax.numpy as jnp
from jax import lax
from jax.experimental import pallas as pl
from jax.experimental.pallas import tpu as pltpu

HEAD_DIM = 64
N_HEADS = 8
N_KV = 2
GQA = N_HEADS // N_KV
ATTN_W = N_HEADS * HEAD_DIM
KV_W = N_KV * HEAD_DIM
CONV_K = 3
CMP_BLOCK = 32
CMP_STRIDE = 16
SEL_BLOCK = 64
N_SELECT = 16
WINDOW = 512
Q_BLOCK = 64
REL_BUCKETS = 32
REL_MAX_DIST = 1024
N_MOD = 6
N_BRANCH = 3
EPS = 1e-6
NEG = -1e30

LANES = 128
SUBLANES = 8
GQ = GQA * Q_BLOCK
KEY_CHUNK = 256
BLOCKS_PER_CHUNK = KEY_CHUNK // SEL_BLOCK
V_TILE = 128
BF16_ROWS = 16
V_ROWS = HEAD_DIM + BF16_ROWS
V_TILE_ROWS = N_KV * V_ROWS
CMP_ROW_STEP = 128
WIN_BLOCKS = WINDOW // SEL_BLOCK
WIN_TILES = WINDOW // V_TILE + 1
NEAR_TILES = (REL_MAX_DIST + Q_BLOCK - 1) // SEL_BLOCK + 1
ROW_TILE = 512
INPROJ_TILE = 1024
INPROJ_SPLIT = 4
FAR_CHUNKS = 4
NEAR_CHUNKS = 4
TAIL_CHUNKS = 2
QK_AHEAD = 4
LOG2E = math.log2(math.e)
FF_CHUNK = 1024
VMEM_LIMIT = 56 * 1024 * 1024

F32 = jnp.float32
BF16 = jnp.bfloat16
_NT = (((1,), (1,)), ((), ()))


def _cparams(n_axes):
    return pltpu.CompilerParams(dimension_semantics=("arbitrary",) * n_axes,
                                vmem_limit_bytes=VMEM_LIMIT)


def _with_ones_rows(vt):
    ones = jnp.ones((BF16_ROWS, vt.shape[1]), vt.dtype)
    parts = []
    for g in range(N_KV):
        parts += [vt[g * HEAD_DIM:(g + 1) * HEAD_DIM, :], ones]
    return jnp.concatenate(parts, axis=0)


def _swap_halves(p0, p1):
    low = lax.broadcasted_iota(jnp.int32, p0.shape, 1) < LANES // 2
    return (jnp.where(low, p0, pltpu.roll(p1, LANES // 2, 1)),
            jnp.where(low, pltpu.roll(p0, LANES // 2, 1), p1))


def _mod_kernel(c_ref, w_ref, b_ref, o_ref):
    c = c_ref[...]
    a = c * jax.nn.sigmoid(c)
    o_ref[...] = jnp.dot(a, w_ref[...], preferred_element_type=F32,
                         precision=lax.Precision.HIGHEST) + b_ref[...]


def _mod_call(c_pad, w_ada, b_ada):
    rows, d = c_pad.shape
    n = w_ada.shape[1]
    tn = n // N_MOD
    return pl.pallas_call(
        _mod_kernel,
        grid=(n // tn,),
        in_specs=[pl.BlockSpec((rows, d), lambda j: (0, 0)),
                  pl.BlockSpec((d, tn), lambda j: (0, j)),
                  pl.BlockSpec((1, tn), lambda j: (0, j))],
        out_specs=pl.BlockSpec((rows, tn), lambda j: (0, j)),
        out_shape=jax.ShapeDtypeStruct((rows, n), F32),
        compiler_params=_cparams(1),
        name="adaln_mod",
    )(c_pad, w_ada, b_ada.reshape(1, n))


_N_KC, _N_VC, _N_KS, _N_KW, _N_CONV = 0, KV_W, 2 * KV_W, 3 * KV_W, 4 * KV_W
_T_Q, _T_VS, _T_VW, _T_G = 0, ATTN_W, ATTN_W + KV_W, ATTN_W + 2 * KV_W
_G_ROWS = GQA * SUBLANES


def _inproj_kernel(x_ref, mod_ref, n1_ref, wn_ref, wt_ref, qn_ref, kn_ref, cw_ref, bdq_ref, bdk_ref,
                   qt_out, kc_out, vc_out, ks_out, vst_out, kw_out, vwt_out, gt_out, conv_out,
                   carry, *, conv_w):
    t = pl.program_id(1)
    tm = x_ref.shape[1]
    sub = tm // INPROJ_SPLIT

    @pl.when(t == 0)
    def _():
        carry[...] = jnp.zeros_like(carry)

    prev2, prev1 = carry[6:7, :], carry[7:8, :]
    c_bg = _N_CONV + conv_w
    c_u = c_bg + conv_w

    for part in range(INPROJ_SPLIT):
        rows = slice(part * sub, (part + 1) * sub)
        x = x_ref[0, rows, :]
        ms = jnp.mean(x * x, axis=-1, keepdims=True)
        y = x * lax.rsqrt(ms + EPS) * n1_ref[...]
        h = (y * (1.0 + mod_ref[0, 1:2, :]) + mod_ref[0, 0:1, :]).astype(BF16)

        def proj(a, b):
            return jnp.dot(h, wn_ref[:, a:b], preferred_element_type=F32)

        def proj_t(a, b):
            return lax.dot_general(wt_ref[a:b, :], h, _NT, preferred_element_type=F32)

        def head_norm(v, gain):
            ssq = jnp.dot((v * v).astype(BF16), bdk_ref[...], preferred_element_type=F32)
            return v * lax.rsqrt(ssq * (1.0 / HEAD_DIM) + EPS) * gain

        qf = proj_t(_T_Q, _T_VS)
        vg = proj_t(_T_VS, _T_G + _G_ROWS)
        kv = proj(_N_KC, _N_CONV)
        cv = proj(_N_CONV, c_u + conv_w)

        ssq = jnp.dot(bdq_ref[...], (qf * qf).astype(BF16), preferred_element_type=F32)
        qf = qf * lax.rsqrt(ssq * (1.0 / HEAD_DIM) + EPS) * qn_ref[...]
        gf = jax.nn.sigmoid(vg[2 * KV_W:, :])
        for c in range(sub // LANES):
            blk = (part * sub) // Q_BLOCK + 2 * c
            cols = slice(c * LANES, (c + 1) * LANES)
            for g in range(N_KV):
                pc = [qf[(g * GQA + r) * HEAD_DIM:(g * GQA + r + 1) * HEAD_DIM, cols]
                      for r in range(GQA)]
                lo01, hi01 = _swap_halves(pc[0], pc[1])
                lo23, hi23 = _swap_halves(pc[2], pc[3])
                qt_out[0, blk, g] = jnp.concatenate([lo01, lo23], axis=1).astype(BF16)
                qt_out[0, blk + 1, g] = jnp.concatenate([hi01, hi23], axis=1).astype(BF16)
            pc = [gf[r * SUBLANES:(r + 1) * SUBLANES, cols] for r in range(GQA)]
            lo01, hi01 = _swap_halves(pc[0], pc[1])
            lo23, hi23 = _swap_halves(pc[2], pc[3])
            gt_out[0, blk] = jnp.concatenate([lo01, lo23], axis=1)
            gt_out[0, blk + 1] = jnp.concatenate([hi01, hi23], axis=1)

        vs_f = _with_ones_rows(vg[:KV_W, :].astype(BF16))
        vw_f = _with_ones_rows(vg[KV_W:2 * KV_W, :].astype(BF16))
        for j in range(sub // KEY_CHUNK):
            vst_out[0, (part * sub) // KEY_CHUNK + j] = vs_f[:, j * KEY_CHUNK:(j + 1) * KEY_CHUNK]
        for j in range(sub // V_TILE):
            vwt_out[0, (part * sub) // V_TILE + j] = vw_f[:, j * V_TILE:(j + 1) * V_TILE]

        kc_out[0, rows, :] = kv[:, _N_KC:_N_VC]
        vc_out[0, rows, :] = kv[:, _N_VC:_N_KS]
        ks_out[0, rows, :] = head_norm(kv[:, _N_KS:_N_KW], kn_ref[...]).astype(BF16)
        kw_out[0, rows, :] = head_norm(kv[:, _N_KW:_N_CONV], kn_ref[...]).astype(BF16)

        z = cv[:, :conv_w] * cv[:, 2 * conv_w:]
        row = lax.broadcasted_iota(jnp.int32, z.shape, 0)
        z1 = jnp.where(row == 0, prev1, pltpu.roll(z, 1, 0))
        z2 = jnp.where(row == 0, prev2, jnp.where(row == 1, prev1, pltpu.roll(z, 2, 0)))
        zc = cw_ref[0:1, :] * z2 + cw_ref[1:2, :] * z1 + cw_ref[2:3, :] * z
        conv_out[0, rows, :] = (cv[:, conv_w:2 * conv_w] * zc).astype(BF16)
        prev2, prev1 = z[sub - 2:sub - 1, :], z[sub - 1:sub, :]
        if part == INPROJ_SPLIT - 1:
            carry[...] = z[sub - SUBLANES:sub, :]


def _inproj_call(x, mod, norm1, w_nat, w_tr, qn_col, kn_t, conv_w, bdq, bdk):
    b, t, d = x.shape
    tm = min(INPROJ_TILE, t)
    cw = conv_w.shape[1]
    nq = t // Q_BLOCK
    row_spec = lambda w: pl.BlockSpec((1, tm, w), lambda i, j: (i, j, 0))
    full = lambda shp: pl.BlockSpec(shp, lambda i, j: (0,) * len(shp))
    vt_spec = lambda w: pl.BlockSpec((1, tm // w, V_TILE_ROWS, w), lambda i, j: (i, j, 0, 0))
    vt_shape = lambda w: jax.ShapeDtypeStruct((b, t // w, V_TILE_ROWS, w), BF16)
    kv = lambda dt: jax.ShapeDtypeStruct((b, t, KV_W), dt)
    out_specs = [pl.BlockSpec((1, tm // Q_BLOCK, N_KV, HEAD_DIM, GQ), lambda i, j: (i, j, 0, 0, 0)),
                 row_spec(KV_W), row_spec(KV_W), row_spec(KV_W), vt_spec(KEY_CHUNK), row_spec(KV_W),
                 vt_spec(V_TILE),
                 pl.BlockSpec((1, tm // Q_BLOCK, SUBLANES, GQ), lambda i, j: (i, j, 0, 0)),
                 row_spec(cw)]
    out_shape = [jax.ShapeDtypeStruct((b, nq, N_KV, HEAD_DIM, GQ), BF16),
                 kv(F32), kv(F32), kv(BF16), vt_shape(KEY_CHUNK), kv(BF16), vt_shape(V_TILE),
                 jax.ShapeDtypeStruct((b, nq, SUBLANES, GQ), F32),
                 jax.ShapeDtypeStruct((b, t, cw), BF16)]
    return pl.pallas_call(
        functools.partial(_inproj_kernel, conv_w=cw),
        grid=(b, t // tm),
        in_specs=[row_spec(d),
                  pl.BlockSpec((1, N_MOD, d), lambda i, j: (i, 0, 0)),
                  full((1, d)), full(w_nat.shape), full(w_tr.shape), full((ATTN_W, 1)),
                  full((1, KV_W)), full((CONV_K, cw)), full((ATTN_W, ATTN_W)), full((KV_W, KV_W))],
        out_specs=out_specs,
        out_shape=out_shape,
        scratch_shapes=[pltpu.VMEM((SUBLANES, cw), F32)],
        compiler_params=_cparams(2),
        name="inproj",
    )(x, mod, norm1, w_nat, w_tr, qn_col, kn_t, conv_w, bdq, bdk)


def _compress_kernel(kx_ref, vx_ref, pek_ref, pev_ref, w1k_ref, w1v_ref, w2k_ref, w2vt_ref,
                     kn_ref, bdk_ref, kc_out, vct_out):
    def hidden(x_ref, pe_ref, w1_ref):
        n = x_ref.shape[1] // CMP_STRIDE
        u = jnp.zeros((n, w1_ref.shape[2]), F32)
        v = jnp.zeros((n, w1_ref.shape[2]), F32)
        for r in range(0, CMP_STRIDE, 2):
            tok = [x_ref[0, pl.ds(r + d, n, stride=CMP_STRIDE), :] for d in range(2)]
            cols = slice(r * KV_W, (r + 2) * KV_W)
            for a, acc in ((0, "u"), (1, "v")):
                lhs = jnp.concatenate([tok[d] + pe_ref[a:a + 1, (r + d) * KV_W:(r + d + 1) * KV_W]
                                       for d in range(2)], axis=1).astype(BF16)
                prod = jnp.dot(lhs, w1_ref[a, cols, :], preferred_element_type=F32)
                if acc == "u":
                    u = u + prod
                else:
                    v = v + prod
        hid = u + pltpu.roll(v, n - 1, 0)
        return jax.nn.gelu(hid, approximate=True).astype(BF16)

    kc = jnp.dot(hidden(kx_ref, pek_ref, w1k_ref), w2k_ref[...], preferred_element_type=F32)
    ssq = jnp.dot((kc * kc).astype(BF16), bdk_ref[...], preferred_element_type=F32)
    kc_out[0] = (kc * lax.rsqrt(ssq * (1.0 / HEAD_DIM) + EPS) * kn_ref[...]).astype(BF16)
    vct_out[0] = _with_ones_rows(lax.dot_general(w2vt_ref[...], hidden(vx_ref, pev_ref, w1v_ref), _NT,
                                                 preferred_element_type=F32).astype(BF16))


def _compress_call(kx, vx, pek, pev, w1k, w1v, w2k, w2vt, kn_t, bdk):
    b, t, _ = kx.shape
    nrow = t // CMP_STRIDE
    wide, hid2 = w1k.shape[1], w1k.shape[2]
    full = lambda shp: pl.BlockSpec(shp, lambda i: (0,) * len(shp))
    xs = pl.BlockSpec((1, t, KV_W), lambda i: (i, 0, 0))
    return pl.pallas_call(
        _compress_kernel,
        grid=(b,),
        in_specs=[xs, xs, full((2, wide)), full((2, wide)), full((2, wide, hid2)),
                  full((2, wide, hid2)), full((hid2, KV_W)), full((KV_W, hid2)),
                  full((1, KV_W)), full((KV_W, KV_W))],
        out_specs=[pl.BlockSpec((1, nrow, KV_W), lambda i: (i, 0, 0)),
                   pl.BlockSpec((1, V_TILE_ROWS, nrow), lambda i: (i, 0, 0))],
        out_shape=[jax.ShapeDtypeStruct((b, nrow, KV_W), BF16),
                   jax.ShapeDtypeStruct((b, V_TILE_ROWS, nrow), BF16)],
        compiler_params=_cparams(1),
        name="compress",
    )(kx, vx, pek, pev, w1k, w1v, w2k, w2vt, kn_t, bdk)


def _attn_kernel(q_ref, kc_ref, vct_ref, ks_ref, vst_ref, kw_ref, vwt_ref, gt_ref, ovt_ref,
                 eb_ref, sb_ref, wb_ref, far_ref, o_ref, sel_scr, acc_scr, out_scr, qp_scr, s_scr,
                 fin_scr, *, n_cmp_rows, eb_shift):
    n_sb = sel_scr.shape[1]
    ci = jnp.minimum(pl.program_id(1), n_sb - 1)
    par = lax.rem(ci, 2)
    n_total = ks_ref.shape[1] // KEY_CHUNK
    vsl = lambda g: slice(g * V_ROWS, (g + 1) * V_ROWS)

    @pl.when((pl.program_id(0) == 0) & (pl.program_id(1) == 0))
    def _():
        def zero(g, carry):
            fin_scr[g] = jnp.zeros(fin_scr.shape[1:], F32)
            sel_scr[g] = jnp.zeros(sel_scr.shape[1:], F32)
            return carry
        lax.fori_loop(0, N_KV, zero, 0)

    def chunk_qk(g, c):
        kk = ks_ref[0, pl.ds(pl.multiple_of(c * KEY_CHUNK, KEY_CHUNK), KEY_CHUNK), :]
        return jnp.dot(kk, qp_scr[g], preferred_element_type=F32).astype(BF16)

    n_chunks = ci // BLOCKS_PER_CHUNK + 1
    far_groups = jnp.maximum(ci - (NEAR_TILES - 1), 0) // (BLOCKS_PER_CHUNK * FAR_CHUNKS)
    n_wide = far_groups // 2
    n_far = far_groups - 2 * n_wide
    c_far = n_wide * (2 * FAR_CHUNKS)
    c0 = far_groups * FAR_CHUNKS
    left = lax.rem(n_chunks - c0, NEAR_CHUNKS)
    n_tail = ((left > 0) & (left <= TAIL_CHUNKS)).astype(jnp.int32)
    n_near = (n_chunks - c0) // NEAR_CHUNKS + (left > TAIL_CHUNKS).astype(jnp.int32)
    c_tail = c0 + NEAR_CHUNKS * n_near

    def near_units(i):
        return [(g, jnp.minimum(c0 + NEAR_CHUNKS * i + j, n_total - 1))
                for g in range(N_KV) for j in range(NEAR_CHUNKS)]

    def tail_units(i):
        return [(g, jnp.minimum(c_tail + TAIL_CHUNKS * i + j, n_total - 1))
                for g in range(N_KV) for j in range(TAIL_CHUNKS)]

    def far_units(i):
        return [(g, c_far + i * FAR_CHUNKS + j) for g in range(N_KV) for j in range(FAR_CHUNKS)]

    def wide_units(i):
        return [(g, i * 2 * FAR_CHUNKS + j) for g in range(N_KV) for j in range(2 * FAR_CHUNKS)]

    def pick_units(options):
        units = options[-1][1][:QK_AHEAD]
        for cond, cand in reversed(options[:-1]):
            units = [(jnp.where(cond, g_a, g_b), jnp.where(cond, c_a, c_b))
                     for (g_a, c_a), (g_b, c_b) in zip(cand[:QK_AHEAD], units)]
        return [(g, jnp.minimum(c, n_total - 1)) for g, c in units]

    def prefetch_scores(units):
        for k in range(QK_AHEAD):
            s_scr[k] = chunk_qk(*units[k])

    def before_loops(rows):
        pieces = []
        for g in range(N_KV):
            for half in range(GQ // LANES):
                a = fin_scr[g, :, half * LANES:(half + 1) * LANES]
                stacked = jnp.concatenate([a, pltpu.roll(a, Q_BLOCK, 1)], axis=0)
                pieces.append(stacked.T[:Q_BLOCK, :])
        o_ref[0] = jnp.concatenate(pieces, axis=1).astype(BF16)

        zeros_q = jnp.zeros((HEAD_DIM, GQ), BF16)
        qp_scr[0] = jnp.concatenate([q_ref[0, 0, 0], zeros_q], axis=0)
        qp_scr[1] = jnp.concatenate([zeros_q, q_ref[0, 0, 1]], axis=0)

        nsel = min(n_sb, rows * CMP_STRIDE // SEL_BLOCK)
        lane = lax.broadcasted_iota(jnp.int32, (nsel, LANES), 1)
        cmp_c = n_cmp_rows - 4
        e0 = cmp_c - 4 * ci + jnp.where(par == 0, eb_shift[0], eb_shift[1])
        e0 = pl.multiple_of(e0, 8)
        o_cmp = []
        imp = []
        w0 = ci // 2 - (WIN_TILES - 1)
        win_tiles = [jnp.maximum(w0 + j, 0) for j in range(WIN_TILES)]
        s_cmp = [jnp.dot(kc_ref[0, :rows, :], qp_scr[g], preferred_element_type=F32)
                 for g in range(N_KV)]
        for g in range(N_KV):
            sc = s_cmp[g] + eb_ref[par, g, pl.ds(e0, rows), :]
            m = jnp.max(sc, axis=0, keepdims=True)
            e = jnp.exp2(sc - m).astype(BF16)
            ov = jnp.dot(vct_ref[0, vsl(g), :rows], e, preferred_element_type=F32)
            inv = jnp.where(m > 0.5 * NEG, 1.0 / ov[HEAD_DIM:HEAD_DIM + 1, :], 0.0)
            o_cmp.append(ov[:HEAD_DIM, :] * inv)
            ir = jnp.dot(ovt_ref[:nsel, :rows], e, preferred_element_type=F32) * inv
            a = ir[:, :LANES] + ir[:, LANES:]
            imp.append(a + pltpu.roll(a, Q_BLOCK, 1))

        s_win = [[jnp.dot(kw_ref[0, pl.ds(pl.multiple_of(tj * V_TILE, V_TILE), V_TILE), :],
                          qp_scr[g], preferred_element_type=F32).astype(BF16) for tj in win_tiles]
                 for g in range(N_KV)]

        prefetch_scores(pick_units([(n_wide > 0, wide_units(0)), (n_far > 0, far_units(0)),
                                    (n_near > 0, near_units(0)), (None, tail_units(0))]))

        jidx = lax.broadcasted_iota(jnp.int32, (nsel, LANES), 0)
        jf = jidx.astype(F32)
        valid = jidx <= ci
        forced = (jidx == 0) | (jidx == ci) | (jidx == ci - 1)
        score = jnp.where(valid, jnp.where(forced, -2.0, jnp.where(lane < Q_BLOCK, imp[0], imp[1])),
                          -1.0)
        sel = jnp.where(forced, 1.0, 0.0)

        def first_max(score):
            pairs = [(score[r:r + SUBLANES], jf[r:r + SUBLANES]) for r in range(0, nsel, SUBLANES)]
            while len(pairs) > 1:
                nxt = []
                for (va, ia), (vb, ib) in zip(pairs[0::2], pairs[1::2]):
                    keep = va >= vb
                    nxt.append((jnp.where(keep, va, vb), jnp.where(keep, ia, ib)))
                pairs = nxt + pairs[len(pairs) - len(pairs) % 2:]
            v8, i8 = pairs[0]
            mx = jnp.max(v8, axis=0, keepdims=True)
            return jnp.min(jnp.where(v8 == mx, i8, float(nsel)), axis=0, keepdims=True)

        for _ in range(min(N_SELECT, nsel) - 3):
            first = first_max(score)
            hit = jf == first
            sel = jnp.where(hit, 1.0, sel)
            score = jnp.where(hit, -2.0, score)
        selneg = jnp.where((sel > 0.5) & valid, 0.0, NEG)
        swapped = pltpu.roll(selneg, Q_BLOCK, 1)
        left = jnp.where(lane < Q_BLOCK, selneg, swapped)
        right = jnp.where(lane < Q_BLOCK, swapped, selneg)
        sel_scr[0, :nsel] = jnp.concatenate([left, left], axis=1)
        sel_scr[1, :nsel] = jnp.concatenate([right, right], axis=1)

        for g in range(N_KV):
            parts = []
            for i in range(2 * WIN_TILES):
                delta = par + WIN_BLOCKS - i
                ok = (delta >= 0) & (delta <= WIN_BLOCKS) & (delta <= ci)
                tile = jnp.where(ok, delta, WIN_BLOCKS + 1)
                half = s_win[g][i // 2][(i % 2) * SEL_BLOCK:(i % 2 + 1) * SEL_BLOCK, :]
                parts.append(half + wb_ref[g, tile])
            s = jnp.concatenate(parts, axis=0)
            m = jnp.max(s, axis=0, keepdims=True)
            pb = jnp.exp2(s - m)
            o_win = jnp.zeros((V_ROWS, GQ), F32)
            for j, tj in enumerate(win_tiles):
                o_win = o_win + jnp.dot(vwt_ref[0, tj, vsl(g), :], pb[j * V_TILE:(j + 1) * V_TILE, :],
                                        preferred_element_type=F32)
            w_scale = (gt_ref[0, 0, 2 * N_KV + g:2 * N_KV + g + 1, :]
                       * (1.0 / o_win[HEAD_DIM:HEAD_DIM + 1, :]))
            out_scr[g] = gt_ref[0, 0, g:g + 1, :] * o_cmp[g] + w_scale * o_win[:HEAD_DIM, :]
            acc_scr[g] = jnp.zeros((V_ROWS, GQ), F32)

    row_steps = [r for r in range(CMP_ROW_STEP, n_cmp_rows + 1, CMP_ROW_STEP)] or [n_cmp_rows]
    variant = jnp.minimum((4 * ci + 2) // CMP_ROW_STEP, len(row_steps) - 1)
    lax.switch(variant, [functools.partial(before_loops, r) for r in row_steps])

    def chunk_softmax(s, g, c, near, extra):
        parts = []
        for i in range(BLOCKS_PER_CHUNK):
            kb = c * BLOCKS_PER_CHUNK + i
            blk = s[i * SEL_BLOCK:(i + 1) * SEL_BLOCK, :]
            mrow = sel_scr[g, pl.ds(kb, 1), :]
            if extra is not None:
                mrow = mrow + extra
            if near:
                tile = jnp.clip(ci - kb, 0, NEAR_TILES)
                blk = blk + sb_ref[g, tile] + mrow.astype(BF16)
            else:
                blk = blk + (mrow + far_ref[g]).astype(BF16)
            parts.append(blk)
        s = jnp.concatenate(parts, axis=0)
        m_c = jnp.max(s, axis=0, keepdims=True)
        return m_c.astype(F32), jnp.exp2(s - m_c)

    def chunk_pv(pb, g, c):
        return jnp.dot(vst_ref[0, c, vsl(g), :], pb, preferred_element_type=F32)

    def merge_step(units, extras, near, next_units, carry):
        scores = {k: s_scr[k] for k in range(QK_AHEAD)}
        results = []
        for k, (g, c) in enumerate(units):
            m_c, pb = chunk_softmax(scores.pop(k), g, c, near, extras[k])
            if k + QK_AHEAD < len(units):
                scores[k + QK_AHEAD] = chunk_qk(*units[k + QK_AHEAD])
            else:
                j = k + QK_AHEAD - len(units)
                s_scr[j] = chunk_qk(*next_units[j])
            results.append((m_c, chunk_pv(pb, g, c)))
        per_g = len(units) // N_KV
        new = []
        for g in range(N_KV):
            m_run = carry[g]
            stats = results[g * per_g:(g + 1) * per_g]
            m_new = m_run
            for m_c, _ in stats:
                m_new = jnp.maximum(m_new, m_c)
            acc = jnp.exp2(m_run - m_new) * acc_scr[g]
            for m_c, pv in stats:
                acc = acc + jnp.exp2(m_c - m_new) * pv
            acc_scr[g] = acc
            new.append(m_new)
        return tuple(new)

    after_far = [(n_near > 0, near_units(0)), (None, tail_units(0))]

    def wide_step(i, carry):
        units = wide_units(i)
        nxt = pick_units([(i + 1 < n_wide, wide_units(i + 1)), (n_far > 0, far_units(0))] + after_far)
        return merge_step(units, [None] * len(units), False, nxt, carry)

    def far_step(i, carry):
        units = far_units(i)
        return merge_step(units, [None] * len(units), False, pick_units(after_far), carry)

    def masked_past_end(first_chunk, per_step):
        return [None if j == 0 else jnp.where(first_chunk + j < n_chunks, 0.0, NEG)
                for _ in range(N_KV) for j in range(per_step)]

    def near_step(i, carry):
        nxt = pick_units([(i + 1 < n_near, near_units(i + 1)), (None, tail_units(0))])
        extras = masked_past_end(c0 + NEAR_CHUNKS * i, NEAR_CHUNKS)
        return merge_step(near_units(i), extras, True, nxt, carry)

    def tail_step(i, carry):
        extras = masked_past_end(c_tail + TAIL_CHUNKS * i, TAIL_CHUNKS)
        return merge_step(tail_units(i), extras, True, tail_units(i + 1), carry)

    init = (jnp.full((1, GQ), NEG, F32),) * N_KV
    carry = lax.fori_loop(0, n_wide, wide_step, init)
    carry = lax.fori_loop(0, n_far, far_step, carry)
    carry = lax.fori_loop(0, n_near, near_step, carry)
    lax.fori_loop(0, n_tail, tail_step, carry)

    for g in range(N_KV):
        acc = acc_scr[g]
        scale = gt_ref[0, 0, N_KV + g:N_KV + g + 1, :] * (1.0 / acc[HEAD_DIM:HEAD_DIM + 1, :])
        fin_scr[g] = out_scr[g] + scale * acc[:HEAD_DIM, :]


def _attn_call(qt, kc, vct, ks, vst, kw, vwt, gt, ovt, ebank, sbank, wbank, far, eb_shift):
    b, nq = qt.shape[0], qt.shape[1]
    t = ks.shape[1]
    n_cmp_rows = kc.shape[1]
    n_sb = t // SEL_BLOCK
    per_b = lambda shp: pl.BlockSpec((1,) + shp[1:], lambda i, j: (i,) + (0,) * (len(shp) - 1))
    per_q = lambda shp: pl.BlockSpec(
        (1, 1) + shp[2:], lambda i, j: (i, jnp.minimum(j, nq - 1)) + (0,) * (len(shp) - 2))
    full = lambda shp: pl.BlockSpec(shp, lambda i, j: (0,) * len(shp))
    args = (qt, kc, vct, ks, vst, kw, vwt, gt, ovt, ebank, sbank, wbank, far)
    specs = [per_q(qt.shape), per_b(kc.shape), per_b(vct.shape), per_b(ks.shape), per_b(vst.shape),
             per_b(kw.shape), per_b(vwt.shape), per_q(gt.shape), full(ovt.shape),
             full(ebank.shape), full(sbank.shape), full(wbank.shape), full(far.shape)]
    acc_like = pltpu.VMEM((N_KV, HEAD_DIM, GQ), F32)
    return pl.pallas_call(
        functools.partial(_attn_kernel, n_cmp_rows=n_cmp_rows, eb_shift=eb_shift),
        grid=(b, nq + 1),
        in_specs=specs,
        out_specs=pl.BlockSpec((1, Q_BLOCK, ATTN_W), lambda i, j: (i, jnp.maximum(j - 1, 0), 0)),
        out_shape=jax.ShapeDtypeStruct((b, t, ATTN_W), BF16),
        scratch_shapes=[pltpu.VMEM((N_KV, n_sb, GQ), F32),
                        pltpu.VMEM((N_KV, V_ROWS, GQ), F32),
                        acc_like,
                        pltpu.VMEM((N_KV, KV_W, GQ), BF16),
                        pltpu.VMEM((QK_AHEAD, KEY_CHUNK, GQ), BF16),
                        acc_like],
        compiler_params=_cparams(2),
        name="nsa_attention",
    )(*args)


def _ffn_kernel(x_ref, a_ref, c_ref, mod_ref, n2_ref, wo_ref, w1_ref, w2_ref, o_ref):
    aw = a_ref.shape[2]
    mix = jnp.dot(a_ref[0], wo_ref[0:aw, :], preferred_element_type=F32)
    mix = mix + jnp.dot(c_ref[0], wo_ref[aw:, :], preferred_element_type=F32)
    x1 = x_ref[0] + mod_ref[0, 2:3, :] * mix
    ms = jnp.mean(x1 * x1, axis=-1, keepdims=True)
    y = x1 * lax.rsqrt(ms + EPS) * n2_ref[...]
    h2 = (y * (1.0 + mod_ref[0, 4:5, :]) + mod_ref[0, 3:4, :]).astype(BF16)
    d_ff = w1_ref.shape[1]
    ff = jnp.zeros(x1.shape, F32)
    for j in range(d_ff // FF_CHUNK):
        a = jnp.dot(h2, w1_ref[:, j * FF_CHUNK:(j + 1) * FF_CHUNK], preferred_element_type=F32)
        a = jnp.maximum(a, 0.0)
        ff = ff + jnp.dot((a * a).astype(BF16), w2_ref[j * FF_CHUNK:(j + 1) * FF_CHUNK, :],
                          preferred_element_type=F32)
    o_ref[0] = x1 + mod_ref[0, 5:6, :] * ff


def _ffn_call(x, attn, conv, mod, norm2, w_out, w_ff1, w_ff2):
    b, t, d = x.shape
    tm = min(ROW_TILE, t)
    row_spec = lambda w: pl.BlockSpec((1, tm, w), lambda i, j: (i, j, 0))
    full = lambda shp: pl.BlockSpec(shp, lambda i, j: (0,) * len(shp),
                                    pipeline_mode=pl.Buffered(1))
    return pl.pallas_call(
        _ffn_kernel,
        grid=(b, t // tm),
        in_specs=[row_spec(d), row_spec(attn.shape[2]), row_spec(conv.shape[2]),
                  pl.BlockSpec((1, N_MOD, d), lambda i, j: (i, 0, 0)),
                  full((1, d)), full(w_out.shape), full(w_ff1.shape), full(w_ff2.shape)],
        out_specs=row_spec(d),
        out_shape=jax.ShapeDtypeStruct((b, t, d), F32),
        compiler_params=_cparams(2),
        name="outproj_mlp",
    )(x, attn, conv, mod, norm2, w_out, w_ff1, w_ff2)


def _block_diag_ones(n):
    idx = np.arange(n) // HEAD_DIM
    return jnp.asarray(idx[:, None] == idx[None, :], dtype=BF16)


def _pack_w_in(w_in):
    d = w_in.shape[0]
    conv_w = d - ATTN_W
    sizes = [ATTN_W] + [KV_W] * 6 + [N_BRANCH * N_HEADS] + [conv_w] * 3
    offs = np.concatenate([[0], np.cumsum(sizes)])
    part = lambda i: w_in[:, offs[i]:offs[i + 1]]
    q, kc, vc, ks, vs, kw, vw, g, cgate, bgate, u = (part(i) for i in range(11))
    w_nat = jnp.concatenate([kc, vc, ks, kw, cgate, bgate, u], axis=1).astype(BF16)
    gt = g.reshape(d, N_KV, GQA, N_BRANCH).transpose(2, 3, 1, 0).reshape(GQA, N_BRANCH * N_KV, d)
    gt = jnp.pad(gt, ((0, 0), (0, SUBLANES - N_BRANCH * N_KV), (0, 0))).reshape(_G_ROWS, d)
    w_tr = jnp.concatenate([q.T, vs.T, vw.T, gt], axis=0).astype(BF16)
    return w_nat, w_tr


def _expand_w1(w1):
    hid = w1.shape[1]
    w = w1.reshape(2, CMP_STRIDE, HEAD_DIM, hid).astype(BF16)
    zero = jnp.zeros_like(w)
    per_group = [jnp.concatenate([w if k == g else zero for k in range(N_KV)], axis=-1)
                 for g in range(N_KV)]
    return jnp.stack(per_group, axis=2).reshape(2, CMP_STRIDE * KV_W, N_KV * hid)


def _expand_w2(w2):
    hid = w2.shape[0]
    eye = jnp.eye(N_KV, dtype=w2.dtype).reshape(N_KV, 1, N_KV, 1)
    return (w2.reshape(1, hid, 1, HEAD_DIM) * eye).reshape(N_KV * hid, KV_W).astype(BF16)


def _expand_pe(pe):
    p = pe.reshape(2, CMP_STRIDE, 1, HEAD_DIM)
    return jnp.broadcast_to(p, (2, CMP_STRIDE, N_KV, HEAD_DIM)).reshape(2, CMP_STRIDE * KV_W)


def _bucket_thresholds():
    n = np.arange(2 * REL_MAX_DIST)
    max_exact = REL_BUCKETS // 2
    nf = np.maximum(n, max_exact).astype(np.float32)
    ratio = np.log(nf / np.float32(max_exact)) / np.float32(math.log(REL_MAX_DIST / max_exact))
    large = max_exact + (ratio * np.float32(REL_BUCKETS - max_exact)).astype(np.int32)
    table = np.where(n < max_exact, n, np.minimum(large, REL_BUCKETS - 1))
    return tuple(int(np.searchsorted(table, k, side="left")) for k in range(REL_BUCKETS))


def _bank_call(bias_rows, lead, n_tiles, dist_fn, name):
    nl = len(lead)
    thr = _bucket_thresholds()

    def body(rows_ref, o_ref):
        lead_ids = [pl.program_id(a) for a in range(nl)]
        row = lax.broadcasted_iota(jnp.int32, (SEL_BLOCK, GQ), 0)
        qi = lax.broadcasted_iota(jnp.int32, (SEL_BLOCK, GQ), 1) & (Q_BLOCK - 1)

        def tile(t, carry):
            dist, ok = dist_fn(lead_ids, t, row, qi)
            v = jnp.broadcast_to(rows_ref[0, 0:1, :], (SEL_BLOCK, GQ))
            for k in range(1, REL_BUCKETS):
                v = jnp.where(dist >= thr[k], rows_ref[0, k:k + 1, :], v)
            o_ref[(0,) * (nl + 1) + (t,)] = jnp.where(ok, v, NEG)
            return carry

        lax.fori_loop(0, n_tiles, tile, 0)

    return pl.pallas_call(
        body,
        grid=tuple(lead) + (N_KV,),
        in_specs=[pl.BlockSpec((1, REL_BUCKETS, GQ), lambda *i: (i[nl], 0, 0))],
        out_specs=pl.BlockSpec((1,) * (nl + 1) + (n_tiles, SEL_BLOCK, GQ),
                               lambda *i: tuple(i) + (0, 0, 0)),
        out_shape=jax.ShapeDtypeStruct(tuple(lead) + (N_KV, n_tiles, SEL_BLOCK, GQ), F32),
        compiler_params=_cparams(nl + 1),
        name=name,
    )(bias_rows)


def _bias_banks(rel_bias, t):
    n_cmp_rows = t // CMP_STRIDE
    rows = rel_bias.reshape(REL_BUCKETS, N_KV, GQA).transpose(1, 0, 2)
    rows = jnp.repeat(rows, Q_BLOCK, axis=2) * LOG2E

    def sel_dist(lead, tile, row, qi):
        dist = SEL_BLOCK * tile + qi - row
        return dist, dist >= 0

    sbank = _bank_call(rows, (), NEAR_TILES + 1, sel_dist, "bias_bank_sel")
    far = sbank[:, NEAR_TILES, 0:1, :]

    def win_dist(lead, tile, row, qi):
        dist = SEL_BLOCK * tile + qi - row
        return dist, (dist >= 0) & (dist < WINDOW)

    wbank = _bank_call(rows, (), WIN_BLOCKS + 2, win_dist, "bias_bank_win")

    cmp_c = n_cmp_rows - 4
    shifts = tuple(int((-(cmp_c - 4 * p)) % 8) for p in range(2))
    n_tiles = (cmp_c + n_cmp_rows + 8 + SEL_BLOCK - 1) // SEL_BLOCK

    def cmp_dist(lead, tile, row, qi):
        e = SEL_BLOCK * tile + row - jnp.where(lead[0] == 0, shifts[0], shifts[1])
        dist = qi - CMP_STRIDE * e + (CMP_STRIDE * cmp_c - (CMP_BLOCK - 1))
        return dist, (dist >= 0) & (e >= 0)

    ebank = _bank_call(rows, (2,), n_tiles, cmp_dist, "bias_bank_cmp")
    ebank = ebank.reshape(2, N_KV, n_tiles * SEL_BLOCK, GQ)
    return ebank, sbank, wbank, far, shifts


def _overlap_t(t):
    n_cmp_rows = t // CMP_STRIDE
    n_sb = t // SEL_BLOCK
    c_start = np.arange(n_cmp_rows)[None, :] * CMP_STRIDE
    s_start = np.arange(n_sb)[:, None] * SEL_BLOCK
    ov = np.clip(np.minimum(c_start + CMP_BLOCK, s_start + SEL_BLOCK)
                 - np.maximum(c_start, s_start), 0, None) / CMP_BLOCK
    ov[:, n_cmp_rows - 1] = 0.0
    return jnp.asarray(ov, dtype=BF16)


def _layer(x, c_pad, w_in, q_norm, k_norm, cmp_pe_k, cmp_w1_k, cmp_w2_k, cmp_pe_v, cmp_w1_v,
           cmp_w2_v, rel_bias, conv_w, w_out, norm1, norm2, w_ada, b_ada, w_ff1, w_ff2):
    b, t, d = x.shape
    scale = HEAD_DIM ** -0.5

    mod = _mod_call(c_pad, w_ada, b_ada)[:b].reshape(b, N_MOD, d)

    qn_col = (jnp.tile(q_norm, N_HEADS) * (scale * LOG2E)).reshape(ATTN_W, 1)
    kn_t = jnp.tile(k_norm, N_KV).reshape(1, KV_W)
    bdq = _block_diag_ones(ATTN_W)
    bdk = _block_diag_ones(KV_W)
    w_nat, w_tr = _pack_w_in(w_in)
    qt, kc_raw, vc_raw, ks, vst, kw, vwt, gt, conv = _inproj_call(
        x, mod, norm1.reshape(1, d), w_nat, w_tr, qn_col, kn_t, conv_w, bdq, bdk)

    kc, vct = _compress_call(
        kc_raw, vc_raw, _expand_pe(cmp_pe_k), _expand_pe(cmp_pe_v), _expand_w1(cmp_w1_k), _expand_w1(cmp_w1_v),
        _expand_w2(cmp_w2_k), _expand_w2(cmp_w2_v).T, kn_t, bdk)

    ebank, sbank, wbank, far, eb_shift = _bias_banks(rel_bias, t)
    attn = _attn_call(qt, kc, vct, ks, vst, kw, vwt, gt, _overlap_t(t), ebank, sbank.astype(BF16),
                      wbank.astype(BF16), far, eb_shift)

    return _ffn_call(x, attn, conv, mod, norm2.reshape(1, d), w_out.astype(BF16),
                     w_ff1.astype(BF16), w_ff2.astype(BF16))


def kernel(x, c, w_in, q_norm, k_norm, cmp_pe_k, cmp_w1_k, cmp_w2_k, cmp_pe_v, cmp_w1_v, cmp_w2_v,
           rel_bias, conv_w, w_out, norm1, norm2, w_ada, b_ada, w_ff1, w_ff2):
    b = x.shape[0]
    c_pad = jnp.pad(c, ((0, (-b) % 8), (0, 0)))
    for l in range(w_in.shape[0]):
        x = _layer(x, c_pad, w_in[l], q_norm[l], k_norm[l], cmp_pe_k[l], cmp_w1_k[l], cmp_w2_k[l],
                   cmp_pe_v[l], cmp_w1_v[l], cmp_w2_v[l], rel_bias, conv_w[l], w_out[l],
                   norm1[l], norm2[l], w_ada[l], b_ada[l], w_ff1[l], w_ff2[l])
    return x
```

```python
import functools
import math

import numpy as np
import jax
import jax.numpy as jnp
from jax import lax
from jax.experimental import pallas as pl
from jax.experimental.pallas import tpu as pltpu

HEAD_DIM = 64
N_HEADS = 8
N_KV = 2
GQA = N_HEADS // N_KV
ATTN_W = N_HEADS * HEAD_DIM
KV_W = N_KV * HEAD_DIM
CONV_K = 3
CMP_BLOCK = 32
CMP_STRIDE = 16
SEL_BLOCK = 64
N_SELECT = 16
WINDOW = 512
Q_BLOCK = 64
REL_BUCKETS = 32
REL_MAX_DIST = 1024
N_MOD = 6
N_BRANCH = 3
EPS = 1e-6
NEG = -1e30

LANES = 128
SUBLANES = 8
GQ = GQA * Q_BLOCK
KEY_CHUNK = 256
BLOCKS_PER_CHUNK = KEY_CHUNK // SEL_BLOCK
V_TILE = 128
BF16_ROWS = 16
V_ROWS = HEAD_DIM + BF16_ROWS
V_TILE_ROWS = N_KV * V_ROWS
CMP_ROW_STEP = 128
WIN_BLOCKS = WINDOW // SEL_BLOCK
WIN_TILES = WINDOW // V_TILE + 1
NEAR_TILES = (REL_MAX_DIST + Q_BLOCK - 1) // SEL_BLOCK + 1
ROW_TILE = 512
INPROJ_TILE = 1024
INPROJ_SPLIT = 4
FAR_CHUNKS = 4
NEAR_CHUNKS = 4
TAIL_CHUNKS = 2
LAST_CHUNKS = 2
QK_AHEAD = 4
LOG2E = math.log2(math.e)
FF_CHUNK = 1024
VMEM_LIMIT = 56 * 1024 * 1024

F32 = jnp.float32
BF16 = jnp.bfloat16
_NT = (((1,), (1,)), ((), ()))


def _cparams(n_axes):
    return pltpu.CompilerParams(dimension_semantics=("arbitrary",) * n_axes,
                                vmem_limit_bytes=VMEM_LIMIT)


def _with_ones_rows(vt):
    ones = jnp.ones((BF16_ROWS, vt.shape[1]), vt.dtype)
    parts = []
    for g in range(N_KV):
        parts += [vt[g * HEAD_DIM:(g + 1) * HEAD_DIM, :], ones]
    return jnp.concatenate(parts, axis=0)


def _swap_halves(p0, p1):
    low = lax.broadcasted_iota(jnp.int32, p0.shape, 1) < LANES // 2
    return (jnp.where(low, p0, pltpu.roll(p1, LANES // 2, 1)),
            jnp.where(low, pltpu.roll(p0, LANES // 2, 1), p1))


def _mod_kernel(c_ref, w_ref, b_ref, o_ref):
    c = c_ref[...]
    a = c * jax.nn.sigmoid(c)
    o_ref[...] = jnp.dot(a, w_ref[...], preferred_element_type=F32,
                         precision=lax.Precision.HIGHEST) + b_ref[...]


def _mod_call(c_pad, w_ada, b_ada):
    rows, d = c_pad.shape
    n = w_ada.shape[1]
    tn = n // N_MOD
    return pl.pallas_call(
        _mod_kernel,
        grid=(n // tn,),
        in_specs=[pl.BlockSpec((rows, d), lambda j: (0, 0)),
                  pl.BlockSpec((d, tn), lambda j: (0, j)),
                  pl.BlockSpec((1, tn), lambda j: (0, j))],
        out_specs=pl.BlockSpec((rows, tn), lambda j: (0, j)),
        out_shape=jax.ShapeDtypeStruct((rows, n), F32),
        compiler_params=_cparams(1),
        name="adaln_mod",
    )(c_pad, w_ada, b_ada.reshape(1, n))


_N_KC, _N_VC, _N_KS, _N_KW, _N_CONV = 0, KV_W, 2 * KV_W, 3 * KV_W, 4 * KV_W
_T_Q, _T_VS, _T_VW, _T_G = 0, ATTN_W, ATTN_W + KV_W, ATTN_W + 2 * KV_W
_G_ROWS = GQA * SUBLANES


def _inproj_kernel(x_ref, mod_ref, n1_ref, wn_ref, wt_ref, qn_ref, kn_ref, cw_ref, bdq_ref, bdk_ref,
                   qt_out, kc_out, vc_out, ks_out, vst_out, kw_out, vwt_out, gt_out, conv_out,
                   carry, *, conv_w):
    t = pl.program_id(1)
    tm = x_ref.shape[1]
    sub = tm // INPROJ_SPLIT

    @pl.when(t == 0)
    def _():
        carry[...] = jnp.zeros_like(carry)

    prev2, prev1 = carry[6:7, :], carry[7:8, :]
    c_bg = _N_CONV + conv_w
    c_u = c_bg + conv_w

    for part in range(INPROJ_SPLIT):
        rows = slice(part * sub, (part + 1) * sub)
        x = x_ref[0, rows, :]
        ms = jnp.mean(x * x, axis=-1, keepdims=True)
        y = x * lax.rsqrt(ms + EPS) * n1_ref[...]
        h = (y * (1.0 + mod_ref[0, 1:2, :]) + mod_ref[0, 0:1, :]).astype(BF16)

        def proj(a, b):
            return jnp.dot(h, wn_ref[:, a:b], preferred_element_type=F32)

        def proj_t(a, b):
            return lax.dot_general(wt_ref[a:b, :], h, _NT, preferred_element_type=F32)

        def head_norm(v, gain):
            ssq = jnp.dot((v * v).astype(BF16), bdk_ref[...], preferred_element_type=F32)
            return v * lax.rsqrt(ssq * (1.0 / HEAD_DIM) + EPS) * gain

        qf = proj_t(_T_Q, _T_VS)
        vg = proj_t(_T_VS, _T_G + _G_ROWS)
        kv = proj(_N_KC, _N_CONV)
        cv = proj(_N_CONV, c_u + conv_w)

        ssq = jnp.dot(bdq_ref[...], (qf * qf).astype(BF16), preferred_element_type=F32)
        qf = qf * lax.rsqrt(ssq * (1.0 / HEAD_DIM) + EPS) * qn_ref[...]
        gf = jax.nn.sigmoid(vg[2 * KV_W:, :])
        for c in range(sub // LANES):
            blk = (part * sub) // Q_BLOCK + 2 * c
            cols = slice(c * LANES, (c + 1) * LANES)
            for g in range(N_KV):
                pc = [qf[(g * GQA + r) * HEAD_DIM:(g * GQA + r + 1) * HEAD_DIM, cols]
                      for r in range(GQA)]
                lo01, hi01 = _swap_halves(pc[0], pc[1])
                lo23, hi23 = _swap_halves(pc[2], pc[3])
                qt_out[0, blk, g] = jnp.concatenate([lo01, lo23], axis=1).astype(BF16)
                qt_out[0, blk + 1, g] = jnp.concatenate([hi01, hi23], axis=1).astype(BF16)
            pc = [gf[r * SUBLANES:(r + 1) * SUBLANES, cols] for r in range(GQA)]
            lo01, hi01 = _swap_halves(pc[0], pc[1])
            lo23, hi23 = _swap_halves(pc[2], pc[3])
            gt_out[0, blk] = jnp.concatenate([lo01, lo23], axis=1)
            gt_out[0, blk + 1] = jnp.concatenate([hi01, hi23], axis=1)

        vs_f = _with_ones_rows(vg[:KV_W, :].astype(BF16))
        vw_f = _with_ones_rows(vg[KV_W:2 * KV_W, :].astype(BF16))
        for j in range(sub // KEY_CHUNK):
            vst_out[0, (part * sub) // KEY_CHUNK + j] = vs_f[:, j * KEY_CHUNK:(j + 1) * KEY_CHUNK]
        for j in range(sub // V_TILE):
            vwt_out[0, (part * sub) // V_TILE + j] = vw_f[:, j * V_TILE:(j + 1) * V_TILE]

        kc_out[0, rows, :] = kv[:, _N_KC:_N_VC]
        vc_out[0, rows, :] = kv[:, _N_VC:_N_KS]
        ks_out[0, rows, :] = head_norm(kv[:, _N_KS:_N_KW], kn_ref[...]).astype(BF16)
        kw_out[0, rows, :] = head_norm(kv[:, _N_KW:_N_CONV], kn_ref[...]).astype(BF16)

        z = cv[:, :conv_w] * cv[:, 2 * conv_w:]
        row = lax.broadcasted_iota(jnp.int32, z.shape, 0)
        z1 = jnp.where(row == 0, prev1, pltpu.roll(z, 1, 0))
        z2 = jnp.where(row == 0, prev2, jnp.where(row == 1, prev1, pltpu.roll(z, 2, 0)))
        zc = cw_ref[0:1, :] * z2 + cw_ref[1:2, :] * z1 + cw_ref[2:3, :] * z
        conv_out[0, rows, :] = (cv[:, conv_w:2 * conv_w] * zc).astype(BF16)
        prev2, prev1 = z[sub - 2:sub - 1, :], z[sub - 1:sub, :]
        if part == INPROJ_SPLIT - 1:
            carry[...] = z[sub - SUBLANES:sub, :]


def _inproj_call(x, mod, norm1, w_nat, w_tr, qn_col, kn_t, conv_w, bdq, bdk):
    b, t, d = x.shape
    tm = min(INPROJ_TILE, t)
    cw = conv_w.shape[1]
    nq = t // Q_BLOCK
    row_spec = lambda w: pl.BlockSpec((1, tm, w), lambda i, j: (i, j, 0))
    full = lambda shp: pl.BlockSpec(shp, lambda i, j: (0,) * len(shp))
    vt_spec = lambda w: pl.BlockSpec((1, tm // w, V_TILE_ROWS, w), lambda i, j: (i, j, 0, 0))
    vt_shape = lambda w: jax.ShapeDtypeStruct((b, t // w, V_TILE_ROWS, w), BF16)
    kv = lambda dt: jax.ShapeDtypeStruct((b, t, KV_W), dt)
    out_specs = [pl.BlockSpec((1, tm // Q_BLOCK, N_KV, HEAD_DIM, GQ), lambda i, j: (i, j, 0, 0, 0)),
                 row_spec(KV_W), row_spec(KV_W), row_spec(KV_W), vt_spec(KEY_CHUNK), row_spec(KV_W),
                 vt_spec(V_TILE),
                 pl.BlockSpec((1, tm // Q_BLOCK, SUBLANES, GQ), lambda i, j: (i, j, 0, 0)),
                 row_spec(cw)]
    out_shape = [jax.ShapeDtypeStruct((b, nq, N_KV, HEAD_DIM, GQ), BF16),
                 kv(F32), kv(F32), kv(BF16), vt_shape(KEY_CHUNK), kv(BF16), vt_shape(V_TILE),
                 jax.ShapeDtypeStruct((b, nq, SUBLANES, GQ), F32),
                 jax.ShapeDtypeStruct((b, t, cw), BF16)]
    return pl.pallas_call(
        functools.partial(_inproj_kernel, conv_w=cw),
        grid=(b, t // tm),
        in_specs=[row_spec(d),
                  pl.BlockSpec((1, N_MOD, d), lambda i, j: (i, 0, 0)),
                  full((1, d)), full(w_nat.shape), full(w_tr.shape), full((ATTN_W, 1)),
                  full((1, KV_W)), full((CONV_K, cw)), full((ATTN_W, ATTN_W)), full((KV_W, KV_W))],
        out_specs=out_specs,
        out_shape=out_shape,
        scratch_shapes=[pltpu.VMEM((SUBLANES, cw), F32)],
        compiler_params=_cparams(2),
        name="inproj",
    )(x, mod, norm1, w_nat, w_tr, qn_col, kn_t, conv_w, bdq, bdk)


def _compress_kernel(kx_ref, vx_ref, pek_ref, pev_ref, w1k_ref, w1v_ref, w2k_ref, w2vt_ref,
                     kn_ref, bdk_ref, kc_out, vct_out):
    def hidden(x_ref, pe_ref, w1_ref):
        n = x_ref.shape[1] // CMP_STRIDE
        u = jnp.zeros((n, w1_ref.shape[2]), F32)
        v = jnp.zeros((n, w1_ref.shape[2]), F32)
        for r in range(0, CMP_STRIDE, 2):
            tok = [x_ref[0, pl.ds(r + d, n, stride=CMP_STRIDE), :] for d in range(2)]
            cols = slice(r * KV_W, (r + 2) * KV_W)
            for a, acc in ((0, "u"), (1, "v")):
                lhs = jnp.concatenate([tok[d] + pe_ref[a:a + 1, (r + d) * KV_W:(r + d + 1) * KV_W]
                                       for d in range(2)], axis=1).astype(BF16)
                prod = jnp.dot(lhs, w1_ref[a, cols, :], preferred_element_type=F32)
                if acc == "u":
                    u = u + prod
                else:
                    v = v + prod
        hid = u + pltpu.roll(v, n - 1, 0)
        return jax.nn.gelu(hid, approximate=True).astype(BF16)

    kc = jnp.dot(hidden(kx_ref, pek_ref, w1k_ref), w2k_ref[...], preferred_element_type=F32)
    ssq = jnp.dot((kc * kc).astype(BF16), bdk_ref[...], preferred_element_type=F32)
    kc_out[0] = (kc * lax.rsqrt(ssq * (1.0 / HEAD_DIM) + EPS) * kn_ref[...]).astype(BF16)
    vct_out[0] = _with_ones_rows(lax.dot_general(w2vt_ref[...], hidden(vx_ref, pev_ref, w1v_ref), _NT,
                                                 preferred_element_type=F32).astype(BF16))


def _compress_call(kx, vx, pek, pev, w1k, w1v, w2k, w2vt, kn_t, bdk):
    b, t, _ = kx.shape
    nrow = t // CMP_STRIDE
    wide, hid2 = w1k.shape[1], w1k.shape[2]
    full = lambda shp: pl.BlockSpec(shp, lambda i: (0,) * len(shp))
    xs = pl.BlockSpec((1, t, KV_W), lambda i: (i, 0, 0))
    return pl.pallas_call(
        _compress_kernel,
        grid=(b,),
        in_specs=[xs, xs, full((2, wide)), full((2, wide)), full((2, wide, hid2)),
                  full((2, wide, hid2)), full((hid2, KV_W)), full((KV_W, hid2)),
                  full((1, KV_W)), full((KV_W, KV_W))],
        out_specs=[pl.BlockSpec((1, nrow, KV_W), lambda i: (i, 0, 0)),
                   pl.BlockSpec((1, V_TILE_ROWS, nrow), lambda i: (i, 0, 0))],
        out_shape=[jax.ShapeDtypeStruct((b, nrow, KV_W), BF16),
                   jax.ShapeDtypeStruct((b, V_TILE_ROWS, nrow), BF16)],
        compiler_params=_cparams(1),
        name="compress",
    )(kx, vx, pek, pev, w1k, w1v, w2k, w2vt, kn_t, bdk)


def _attn_kernel(q_ref, kc_ref, vct_ref, ks_ref, vst_ref, kw_ref, vwt_ref, gt_ref, ovt_ref,
                 eb_ref, sb_ref, wb_ref, far_ref, o_ref, sel_scr, acc_scr, out_scr, qp_scr, s_scr,
                 m_scr, gsel_scr, *, n_cmp_rows, eb_shift):
    n_sb = sel_scr.shape[1]
    step = pl.program_id(1)
    ci = jnp.minimum(step, n_sb - 1)
    ci_prev = jnp.clip(step - 1, 0, n_sb - 1)
    par = lax.rem(ci, 2)
    n_total = ks_ref.shape[1] // KEY_CHUNK
    vsl = lambda g: slice(g * V_ROWS, (g + 1) * V_ROWS)

    @pl.when((pl.program_id(0) == 0) & (step == 0))
    def _():
        def zero(g, carry):
            sel_scr[g] = jnp.zeros(sel_scr.shape[1:], F32)
            acc_scr[g] = jnp.ones(acc_scr.shape[1:], F32)
            out_scr[g] = jnp.zeros(out_scr.shape[1:], F32)
            m_scr[g] = jnp.zeros(m_scr.shape[1:], F32)
            gsel_scr[g] = jnp.zeros(gsel_scr.shape[1:], F32)
            for k in range(QK_AHEAD // N_KV):
                s_scr[g * (QK_AHEAD // N_KV) + k] = jnp.zeros(s_scr.shape[1:], BF16)
            return carry
        lax.fori_loop(0, N_KV, zero, 0)

    def chunk_qk(g, c):
        kk = ks_ref[0, pl.ds(pl.multiple_of(c * KEY_CHUNK, KEY_CHUNK), KEY_CHUNK), :]
        return jnp.dot(kk, qp_scr[g], preferred_element_type=F32).astype(BF16)

    n_chunks = ci // BLOCKS_PER_CHUNK + 1
    n_loop = jnp.maximum(n_chunks - LAST_CHUNKS, 0)
    far_groups = jnp.maximum(ci - (NEAR_TILES - 1), 0) // (BLOCKS_PER_CHUNK * FAR_CHUNKS)
    n_wide = far_groups // 2
    n_far = far_groups - 2 * n_wide
    c_far = n_wide * (2 * FAR_CHUNKS)
    c0 = far_groups * FAR_CHUNKS
    left = lax.rem(n_loop - c0, NEAR_CHUNKS)
    n_tail = ((left > 0) & (left <= TAIL_CHUNKS)).astype(jnp.int32)
    n_near = (n_loop - c0) // NEAR_CHUNKS + (left > TAIL_CHUNKS).astype(jnp.int32)
    c_tail = c0 + NEAR_CHUNKS * n_near

    def near_units(i):
        return [(g, jnp.minimum(c0 + NEAR_CHUNKS * i + j, n_total - 1))
                for g in range(N_KV) for j in range(NEAR_CHUNKS)]

    def tail_units(i):
        return [(g, jnp.minimum(c_tail + TAIL_CHUNKS * i + j, n_total - 1))
                for g in range(N_KV) for j in range(TAIL_CHUNKS)]

    def last_units(blk):
        first = jnp.maximum(blk // BLOCKS_PER_CHUNK + 1 - LAST_CHUNKS, 0)
        return [(g, jnp.minimum(first + j, n_total - 1))
                for g in range(N_KV) for j in range(LAST_CHUNKS)]

    def far_units(i):
        return [(g, c_far + i * FAR_CHUNKS + j) for g in range(N_KV) for j in range(FAR_CHUNKS)]

    def wide_units(i):
        return [(g, i * 2 * FAR_CHUNKS + j) for g in range(N_KV) for j in range(2 * FAR_CHUNKS)]

    def pick_units(options):
        units = options[-1][1][:QK_AHEAD]
        for cond, cand in reversed(options[:-1]):
            units = [(jnp.where(cond, g_a, g_b), jnp.where(cond, c_a, c_b))
                     for (g_a, c_a), (g_b, c_b) in zip(cand[:QK_AHEAD], units)]
        return [(g, jnp.minimum(c, n_total - 1)) for g, c in units]

    def prefetch_scores(units):
        for k in range(QK_AHEAD):
            s_scr[k] = chunk_qk(*units[k])

    def before_loops(rows):
        n_prev = ci_prev // BLOCKS_PER_CHUNK + 1
        first_prev = jnp.maximum(n_prev - LAST_CHUNKS, 0)
        extras = [None if j == 0 else jnp.where(first_prev + j < n_prev, 0.0, NEG)
                  for _ in range(N_KV) for j in range(LAST_CHUNKS)]
        prev_m = tuple(m_scr[g] for g in range(N_KV))
        merge_step(last_units(ci_prev), extras, True, None, prev_m, blk=ci_prev)
        pieces = []
        for g in range(N_KV):
            acc = acc_scr[g]
            fin = out_scr[g] + (gsel_scr[g] * (1.0 / acc[HEAD_DIM:HEAD_DIM + 1, :])) * acc[:HEAD_DIM, :]
            for half in range(GQ // LANES):
                a = fin[:, half * LANES:(half + 1) * LANES]
                stacked = jnp.concatenate([a, pltpu.roll(a, Q_BLOCK, 1)], axis=0)
                pieces.append(stacked.T[:Q_BLOCK, :])
        o_ref[0] = jnp.concatenate(pieces, axis=1).astype(BF16)

        zeros_q = jnp.zeros((HEAD_DIM, GQ), BF16)
        qp_scr[0] = jnp.concatenate([q_ref[0, 0, 0], zeros_q], axis=0)
        qp_scr[1] = jnp.concatenate([zeros_q, q_ref[0, 0, 1]], axis=0)

        nsel = min(n_sb, rows * CMP_STRIDE // SEL_BLOCK)
        lane = lax.broadcasted_iota(jnp.int32, (nsel, LANES), 1)
        cmp_c = n_cmp_rows - 4
        e0 = cmp_c - 4 * ci + jnp.where(par == 0, eb_shift[0], eb_shift[1])
        e0 = pl.multiple_of(e0, 8)
        o_cmp = []
        imp = []
        w0 = ci // 2 - (WIN_TILES - 1)
        win_tiles = [jnp.maximum(w0 + j, 0) for j in range(WIN_TILES)]
        s_cmp = [jnp.dot(kc_ref[0, :rows, :], qp_scr[g], preferred_element_type=F32)
                 for g in range(N_KV)]
        for g in range(N_KV):
            sc = s_cmp[g] + eb_ref[par, g, pl.ds(e0, rows), :]
            m = jnp.max(sc, axis=0, keepdims=True)
            e = jnp.exp2(sc - m).astype(BF16)
            ov = jnp.dot(vct_ref[0, vsl(g), :rows], e, preferred_element_type=F32)
            inv = jnp.where(m > 0.5 * NEG, 1.0 / ov[HEAD_DIM:HEAD_DIM + 1, :], 0.0)
            o_cmp.append(ov[:HEAD_DIM, :] * inv)
            ir = jnp.dot(ovt_ref[:nsel, :rows], e, preferred_element_type=F32) * inv
            a = ir[:, :LANES] + ir[:, LANES:]
            imp.append(a + pltpu.roll(a, Q_BLOCK, 1))

        s_win = [[jnp.dot(kw_ref[0, pl.ds(pl.multiple_of(tj * V_TILE, V_TILE), V_TILE), :],
                          qp_scr[g], preferred_element_type=F32).astype(BF16) for tj in win_tiles]
                 for g in range(N_KV)]

        prefetch_scores(pick_units([(n_wide > 0, wide_units(0)), (n_far > 0, far_units(0))]
                                   + after_far))

        jidx = lax.broadcasted_iota(jnp.int32, (nsel, LANES), 0)
        jf = jidx.astype(F32)
        valid = jidx <= ci
        forced = (jidx == 0) | (jidx == ci) | (jidx == ci - 1)
        score = jnp.where(valid, jnp.where(forced, -2.0, jnp.where(lane < Q_BLOCK, imp[0], imp[1])),
                          -1.0)
        sel = jnp.where(forced, 1.0, 0.0)

        def first_max(score):
            pairs = [(score[r:r + SUBLANES], jf[r:r + SUBLANES]) for r in range(0, nsel, SUBLANES)]
            while len(pairs) > 1:
                nxt = []
                for (va, ia), (vb, ib) in zip(pairs[0::2], pairs[1::2]):
                    keep = va >= vb
                    nxt.append((jnp.where(keep, va, vb), jnp.where(keep, ia, ib)))
                pairs = nxt + pairs[len(pairs) - len(pairs) % 2:]
            v8, i8 = pairs[0]
            mx = jnp.max(v8, axis=0, keepdims=True)
            return jnp.min(jnp.where(v8 == mx, i8, float(nsel)), axis=0, keepdims=True)

        for _ in range(min(N_SELECT, nsel) - 3):
            first = first_max(score)
            hit = jf == first
            sel = jnp.where(hit, 1.0, sel)
            score = jnp.where(hit, -2.0, score)
        selneg = jnp.where((sel > 0.5) & valid, 0.0, NEG)
        swapped = pltpu.roll(selneg, Q_BLOCK, 1)
        left = jnp.where(lane < Q_BLOCK, selneg, swapped)
        right = jnp.where(lane < Q_BLOCK, swapped, selneg)
        sel_scr[0, :nsel] = jnp.concatenate([left, left], axis=1)
        sel_scr[1, :nsel] = jnp.concatenate([right, right], axis=1)

        for g in range(N_KV):
            parts = []
            for i in range(2 * WIN_TILES):
                delta = par + WIN_BLOCKS - i
                ok = (delta >= 0) & (delta <= WIN_BLOCKS) & (delta <= ci)
                tile = jnp.where(ok, delta, WIN_BLOCKS + 1)
                half = s_win[g][i // 2][(i % 2) * SEL_BLOCK:(i % 2 + 1) * SEL_BLOCK, :]
                parts.append(half + wb_ref[g, tile])
            s = jnp.concatenate(parts, axis=0)
            m = jnp.max(s, axis=0, keepdims=True)
            pb = jnp.exp2(s - m)
            o_win = jnp.zeros((V_ROWS, GQ), F32)
            for j, tj in enumerate(win_tiles):
                o_win = o_win + jnp.dot(vwt_ref[0, tj, vsl(g), :], pb[j * V_TILE:(j + 1) * V_TILE, :],
                                        preferred_element_type=F32)
            w_scale = (gt_ref[0, 0, 2 * N_KV + g:2 * N_KV + g + 1, :]
                       * (1.0 / o_win[HEAD_DIM:HEAD_DIM + 1, :]))
            out_scr[g] = gt_ref[0, 0, g:g + 1, :] * o_cmp[g] + w_scale * o_win[:HEAD_DIM, :]
            acc_scr[g] = jnp.zeros((V_ROWS, GQ), F32)

    def chunk_softmax(s, g, c, near, extra, qblk):
        parts = []
        for i in range(BLOCKS_PER_CHUNK):
            kb = c * BLOCKS_PER_CHUNK + i
            blk = s[i * SEL_BLOCK:(i + 1) * SEL_BLOCK, :]
            mrow = sel_scr[g, pl.ds(kb, 1), :]
            if extra is not None:
                mrow = mrow + extra
            if near:
                tile = jnp.clip(qblk - kb, 0, NEAR_TILES)
                blk = blk + sb_ref[g, tile] + mrow.astype(BF16)
            else:
                blk = blk + (mrow + far_ref[g]).astype(BF16)
            parts.append(blk)
        s = jnp.concatenate(parts, axis=0)
        m_c = jnp.max(s, axis=0, keepdims=True)
        return m_c.astype(F32), jnp.exp2(s - m_c)

    def chunk_pv(pb, g, c):
        return jnp.dot(vst_ref[0, c, vsl(g), :], pb, preferred_element_type=F32)

    def merge_step(units, extras, near, next_units, carry, blk=None):
        qblk = ci if blk is None else blk
        scores = {k: s_scr[k] for k in range(QK_AHEAD)}
        results = []
        for k, (g, c) in enumerate(units):
            m_c, pb = chunk_softmax(scores.pop(k), g, c, near, extras[k], qblk)
            if k + QK_AHEAD < len(units):
                scores[k + QK_AHEAD] = chunk_qk(*units[k + QK_AHEAD])
            elif next_units is not None:
                j = k + QK_AHEAD - len(units)
                s_scr[j] = chunk_qk(*next_units[j])
            results.append((m_c, chunk_pv(pb, g, c)))
        per_g = len(units) // N_KV
        new = []
        for g in range(N_KV):
            m_run = carry[g]
            stats = results[g * per_g:(g + 1) * per_g]
            m_new = m_run
            for m_c, _ in stats:
                m_new = jnp.maximum(m_new, m_c)
            acc = jnp.exp2(m_run - m_new) * acc_scr[g]
            for m_c, pv in stats:
                acc = acc + jnp.exp2(m_c - m_new) * pv
            acc_scr[g] = acc
            new.append(m_new)
        return tuple(new)

    after_near = [(n_tail > 0, tail_units(0)), (None, last_units(ci))]
    after_far = [(n_near > 0, near_units(0))] + after_near

    def wide_step(i, carry):
        units = wide_units(i)
        nxt = pick_units([(i + 1 < n_wide, wide_units(i + 1)), (n_far > 0, far_units(0))] + after_far)
        return merge_step(units, [None] * len(units), False, nxt, carry)

    def far_step(i, carry):
        units = far_units(i)
        return merge_step(units, [None] * len(units), False, pick_units(after_far), carry)

    def masked_past_end(first_chunk, per_step):
        return [None if j == 0 else jnp.where(first_chunk + j < n_loop, 0.0, NEG)
                for _ in range(N_KV) for j in range(per_step)]

    def near_step(i, carry):
        nxt = pick_units([(i + 1 < n_near, near_units(i + 1))] + after_near)
        extras = masked_past_end(c0 + NEAR_CHUNKS * i, NEAR_CHUNKS)
        return merge_step(near_units(i), extras, True, nxt, carry)

    def tail_step(i, carry):
        extras = masked_past_end(c_tail + TAIL_CHUNKS * i, TAIL_CHUNKS)
        return merge_step(tail_units(i), extras, True, last_units(ci), carry)

    row_steps = [r for r in range(CMP_ROW_STEP, n_cmp_rows + 1, CMP_ROW_STEP)] or [n_cmp_rows]
    variant = jnp.minimum((4 * ci + 2) // CMP_ROW_STEP, len(row_steps) - 1)
    lax.switch(variant, [functools.partial(before_loops, r) for r in row_steps])

    init = (jnp.full((1, GQ), NEG, F32),) * N_KV
    carry = lax.fori_loop(0, n_wide, wide_step, init)
    carry = lax.fori_loop(0, n_far, far_step, carry)
    carry = lax.fori_loop(0, n_near, near_step, carry)
    carry = lax.fori_loop(0, n_tail, tail_step, carry)

    for g in range(N_KV):
        m_scr[g] = carry[g]
        gsel_scr[g] = gt_ref[0, 0, N_KV + g:N_KV + g + 1, :]


def _attn_call(qt, kc, vct, ks, vst, kw, vwt, gt, ovt, ebank, sbank, wbank, far, eb_shift):
    b, nq = qt.shape[0], qt.shape[1]
    t = ks.shape[1]
    n_cmp_rows = kc.shape[1]
    n_sb = t // SEL_BLOCK
    per_b = lambda shp: pl.BlockSpec((1,) + shp[1:], lambda i, j: (i,) + (0,) * (len(shp) - 1))
    per_q = lambda shp: pl.BlockSpec(
        (1, 1) + shp[2:], lambda i, j: (i, jnp.minimum(j, nq - 1)) + (0,) * (len(shp) - 2))
    full = lambda shp: pl.BlockSpec(shp, lambda i, j: (0,) * len(shp))
    args = (qt, kc, vct, ks, vst, kw, vwt, gt, ovt, ebank, sbank, wbank, far)
    specs = [per_q(qt.shape), per_b(kc.shape), per_b(vct.shape), per_b(ks.shape), per_b(vst.shape),
             per_b(kw.shape), per_b(vwt.shape), per_q(gt.shape), full(ovt.shape),
             full(ebank.shape), full(sbank.shape), full(wbank.shape), full(far.shape)]
    acc_like = pltpu.VMEM((N_KV, HEAD_DIM, GQ), F32)
    return pl.pallas_call(
        functools.partial(_attn_kernel, n_cmp_rows=n_cmp_rows, eb_shift=eb_shift),
        grid=(b, nq + 1),
        in_specs=specs,
        out_specs=pl.BlockSpec((1, Q_BLOCK, ATTN_W), lambda i, j: (i, jnp.maximum(j - 1, 0), 0)),
        out_shape=jax.ShapeDtypeStruct((b, t, ATTN_W), BF16),
        scratch_shapes=[pltpu.VMEM((N_KV, n_sb, GQ), F32),
                        pltpu.VMEM((N_KV, V_ROWS, GQ), F32),
                        acc_like,
                        pltpu.VMEM((N_KV, KV_W, GQ), BF16),
                        pltpu.VMEM((QK_AHEAD, KEY_CHUNK, GQ), BF16),
                        pltpu.VMEM((N_KV, 1, GQ), F32),
                        pltpu.VMEM((N_KV, 1, GQ), F32)],
        compiler_params=_cparams(2),
        name="nsa_attention",
    )(*args)


def _ffn_kernel(x_ref, a_ref, c_ref, mod_ref, n2_ref, wo_ref, w1_ref, w2_ref, o_ref):
    aw = a_ref.shape[2]
    mix = jnp.dot(a_ref[0], wo_ref[0:aw, :], preferred_element_type=F32)
    mix = mix + jnp.dot(c_ref[0], wo_ref[aw:, :], preferred_element_type=F32)
    x1 = x_ref[0] + mod_ref[0, 2:3, :] * mix
    ms = jnp.mean(x1 * x1, axis=-1, keepdims=True)
    y = x1 * lax.rsqrt(ms + EPS) * n2_ref[...]
    h2 = (y * (1.0 + mod_ref[0, 4:5, :]) + mod_ref[0, 3:4, :]).astype(BF16)
    d_ff = w1_ref.shape[1]
    ff = jnp.zeros(x1.shape, F32)
    for j in range(d_ff // FF_CHUNK):
        a = jnp.dot(h2, w1_ref[:, j * FF_CHUNK:(j + 1) * FF_CHUNK], preferred_element_type=F32)
        a = jnp.maximum(a, 0.0)
        ff = ff + jnp.dot((a * a).astype(BF16), w2_ref[j * FF_CHUNK:(j + 1) * FF_CHUNK, :],
                          preferred_element_type=F32)
    o_ref[0] = x1 + mod_ref[0, 5:6, :] * ff


def _ffn_call(x, attn, conv, mod, norm2, w_out, w_ff1, w_ff2):
    b, t, d = x.shape
    tm = min(ROW_TILE, t)
    row_spec = lambda w: pl.BlockSpec((1, tm, w), lambda i, j: (i, j, 0))
    full = lambda shp: pl.BlockSpec(shp, lambda i, j: (0,) * len(shp),
                                    pipeline_mode=pl.Buffered(1))
    return pl.pallas_call(
        _ffn_kernel,
        grid=(b, t // tm),
        in_specs=[row_spec(d), row_spec(attn.shape[2]), row_spec(conv.shape[2]),
                  pl.BlockSpec((1, N_MOD, d), lambda i, j: (i, 0, 0)),
                  full((1, d)), full(w_out.shape), full(w_ff1.shape), full(w_ff2.shape)],
        out_specs=row_spec(d),
        out_shape=jax.ShapeDtypeStruct((b, t, d), F32),
        compiler_params=_cparams(2),
        name="outproj_mlp",
    )(x, attn, conv, mod, norm2, w_out, w_ff1, w_ff2)


def _block_diag_ones(n):
    idx = np.arange(n) // HEAD_DIM
    return jnp.asarray(idx[:, None] == idx[None, :], dtype=BF16)


def _pack_w_in(w_in):
    d = w_in.shape[0]
    conv_w = d - ATTN_W
    sizes = [ATTN_W] + [KV_W] * 6 + [N_BRANCH * N_HEADS] + [conv_w] * 3
    offs = np.concatenate([[0], np.cumsum(sizes)])
    part = lambda i: w_in[:, offs[i]:offs[i + 1]]
    q, kc, vc, ks, vs, kw, vw, g, cgate, bgate, u = (part(i) for i in range(11))
    w_nat = jnp.concatenate([kc, vc, ks, kw, cgate, bgate, u], axis=1).astype(BF16)
    gt = g.reshape(d, N_KV, GQA, N_BRANCH).transpose(2, 3, 1, 0).reshape(GQA, N_BRANCH * N_KV, d)
    gt = jnp.pad(gt, ((0, 0), (0, SUBLANES - N_BRANCH * N_KV), (0, 0))).reshape(_G_ROWS, d)
    w_tr = jnp.concatenate([q.T, vs.T, vw.T, gt], axis=0).astype(BF16)
    return w_nat, w_tr


def _expand_w1(w1):
    hid = w1.shape[1]
    w = w1.reshape(2, CMP_STRIDE, HEAD_DIM, hid).astype(BF16)
    zero = jnp.zeros_like(w)
    per_group = [jnp.concatenate([w if k == g else zero for k in range(N_KV)], axis=-1)
                 for g in range(N_KV)]
    return jnp.stack(per_group, axis=2).reshape(2, CMP_STRIDE * KV_W, N_KV * hid)


def _expand_w2(w2):
    hid = w2.shape[0]
    eye = jnp.eye(N_KV, dtype=w2.dtype).reshape(N_KV, 1, N_KV, 1)
    return (w2.reshape(1, hid, 1, HEAD_DIM) * eye).reshape(N_KV * hid, KV_W).astype(BF16)


def _expand_pe(pe):
    p = pe.reshape(2, CMP_STRIDE, 1, HEAD_DIM)
    return jnp.broadcast_to(p, (2, CMP_STRIDE, N_KV, HEAD_DIM)).reshape(2, CMP_STRIDE * KV_W)


def _bucket_thresholds():
    n = np.arange(2 * REL_MAX_DIST)
    max_exact = REL_BUCKETS // 2
    nf = np.maximum(n, max_exact).astype(np.float32)
    ratio = np.log(nf / np.float32(max_exact)) / np.float32(math.log(REL_MAX_DIST / max_exact))
    large = max_exact + (ratio * np.float32(REL_BUCKETS - max_exact)).astype(np.int32)
    table = np.where(n < max_exact, n, np.minimum(large, REL_BUCKETS - 1))
    return tuple(int(np.searchsorted(table, k, side="left")) for k in range(REL_BUCKETS))


def _bank_call(bias_rows, lead, n_tiles, dist_fn, name):
    nl = len(lead)
    thr = _bucket_thresholds()

    def body(rows_ref, o_ref):
        lead_ids = [pl.program_id(a) for a in range(nl)]
        row = lax.broadcasted_iota(jnp.int32, (SEL_BLOCK, GQ), 0)
        qi = lax.broadcasted_iota(jnp.int32, (SEL_BLOCK, GQ), 1) & (Q_BLOCK - 1)

        def tile(t, carry):
            dist, ok = dist_fn(lead_ids, t, row, qi)
            v = jnp.broadcast_to(rows_ref[0, 0:1, :], (SEL_BLOCK, GQ))
            for k in range(1, REL_BUCKETS):
                v = jnp.where(dist >= thr[k], rows_ref[0, k:k + 1, :], v)
            o_ref[(0,) * (nl + 1) + (t,)] = jnp.where(ok, v, NEG)
            return carry

        lax.fori_loop(0, n_tiles, tile, 0)

    return pl.pallas_call(
        body,
        grid=tuple(lead) + (N_KV,),
        in_specs=[pl.BlockSpec((1, REL_BUCKETS, GQ), lambda *i: (i[nl], 0, 0))],
        out_specs=pl.BlockSpec((1,) * (nl + 1) + (n_tiles, SEL_BLOCK, GQ),
                               lambda *i: tuple(i) + (0, 0, 0)),
        out_shape=jax.ShapeDtypeStruct(tuple(lead) + (N_KV, n_tiles, SEL_BLOCK, GQ), F32),
        compiler_params=_cparams(nl + 1),
        name=name,
    )(bias_rows)


def _bias_banks(rel_bias, t):
    n_cmp_rows = t // CMP_STRIDE
    rows = rel_bias.reshape(REL_BUCKETS, N_KV, GQA).transpose(1, 0, 2)
    rows = jnp.repeat(rows, Q_BLOCK, axis=2) * LOG2E

    def sel_dist(lead, tile, row, qi):
        dist = SEL_BLOCK * tile + qi - row
        return dist, dist >= 0

    sbank = _bank_call(rows, (), NEAR_TILES + 1, sel_dist, "bias_bank_sel")
    far = sbank[:, NEAR_TILES, 0:1, :]

    def win_dist(lead, tile, row, qi):
        dist = SEL_BLOCK * tile + qi - row
        return dist, (dist >= 0) & (dist < WINDOW)

    wbank = _bank_call(rows, (), WIN_BLOCKS + 2, win_dist, "bias_bank_win")

    cmp_c = n_cmp_rows - 4
    shifts = tuple(int((-(cmp_c - 4 * p)) % 8) for p in range(2))
    n_tiles = (cmp_c + n_cmp_rows + 8 + SEL_BLOCK - 1) // SEL_BLOCK

    def cmp_dist(lead, tile, row, qi):
        e = SEL_BLOCK * tile + row - jnp.where(lead[0] == 0, shifts[0], shifts[1])
        dist = qi - CMP_STRIDE * e + (CMP_STRIDE * cmp_c - (CMP_BLOCK - 1))
        return dist, (dist >= 0) & (e >= 0)

    ebank = _bank_call(rows, (2,), n_tiles, cmp_dist, "bias_bank_cmp")
    ebank = ebank.reshape(2, N_KV, n_tiles * SEL_BLOCK, GQ)
    return ebank, sbank, wbank, far, shifts


def _overlap_t(t):
    n_cmp_rows = t // CMP_STRIDE
    n_sb = t // SEL_BLOCK
    c_start = np.arange(n_cmp_rows)[None, :] * CMP_STRIDE
    s_start = np.arange(n_sb)[:, None] * SEL_BLOCK
    ov = np.clip(np.minimum(c_start + CMP_BLOCK, s_start + SEL_BLOCK)
                 - np.maximum(c_start, s_start), 0, None) / CMP_BLOCK
    ov[:, n_cmp_rows - 1] = 0.0
    return jnp.asarray(ov, dtype=BF16)


def _layer(x, c_pad, w_in, q_norm, k_norm, cmp_pe_k, cmp_w1_k, cmp_w2_k, cmp_pe_v, cmp_w1_v,
           cmp_w2_v, rel_bias, conv_w, w_out, norm1, norm2, w_ada, b_ada, w_ff1, w_ff2):
    b, t, d = x.shape
    scale = HEAD_DIM ** -0.5

    mod = _mod_call(c_pad, w_ada, b_ada)[:b].reshape(b, N_MOD, d)

    qn_col = (jnp.tile(q_norm, N_HEADS) * (scale * LOG2E)).reshape(ATTN_W, 1)
    kn_t = jnp.tile(k_norm, N_KV).reshape(1, KV_W)
    bdq = _block_diag_ones(ATTN_W)
    bdk = _block_diag_ones(KV_W)
    w_nat, w_tr = _pack_w_in(w_in)
    qt, kc_raw, vc_raw, ks, vst, kw, vwt, gt, conv = _inproj_call(
        x, mod, norm1.reshape(1, d), w_nat, w_tr, qn_col, kn_t, conv_w, bdq, bdk)

    kc, vct = _compress_call(
        kc_raw, vc_raw, _expand_pe(cmp_pe_k), _expand_pe(cmp_pe_v), _expand_w1(cmp_w1_k), _expand_w1(cmp_w1_v),
        _expand_w2(cmp_w2_k), _expand_w2(cmp_w2_v).T, kn_t, bdk)

    ebank, sbank, wbank, far, eb_shift = _bias_banks(rel_bias, t)
    attn = _attn_call(qt, kc, vct, ks, vst, kw, vwt, gt, _overlap_t(t), ebank, sbank.astype(BF16),
                      wbank.astype(BF16), far, eb_shift)

    return _ffn_call(x, attn, conv, mod, norm2.reshape(1, d), w_out.astype(BF16),
                     w_ff1.astype(BF16), w_ff2.astype(BF16))


def kernel(x, c, w_in, q_norm, k_norm, cmp_pe_k, cmp_w1_k, cmp_w2_k, cmp_pe_v, cmp_w1_v, cmp_w2_v,
           rel_bias, conv_w, w_out, norm1, norm2, w_ada, b_ada, w_ff1, w_ff2):
    b = x.shape[0]
    c_pad = jnp.pad(c, ((0, (-b) % 8), (0, 0)))
    for l in range(w_in.shape[0]):
        x = _layer(x, c_pad, w_in[l], q_norm[l], k_norm[l], cmp_pe_k[l], cmp_w1_k[l], cmp_w2_k[l],
                   cmp_pe_v[l], cmp_w1_v[l], cmp_w2_v[l], rel_bias, conv_w[l], w_out[l],
                   norm1[l], norm2[l], w_ada[l], b_ada[l], w_ff1[l], w_ff2[l])
    return x
```

```python
import functools
import math

import numpy as np
import jax
import jax.numpy as jnp
from jax import lax
from jax.experimental import pallas as pl
from jax.experimental.pallas import tpu as pltpu

HEAD_DIM = 64
N_HEADS = 8
N_KV = 2
GQA = N_HEADS // N_KV
ATTN_W = N_HEADS * HEAD_DIM
KV_W = N_KV * HEAD_DIM
CONV_K = 3
CMP_BLOCK = 32
CMP_STRIDE = 16
SEL_BLOCK = 64
N_SELECT = 16
WINDOW = 512
Q_BLOCK = 64
REL_BUCKETS = 32
REL_MAX_DIST = 1024
N_MOD = 6
N_BRANCH = 3
EPS = 1e-6
NEG = -1e30

LANES = 128
SUBLANES = 8
GQ = GQA * Q_BLOCK
KEY_CHUNK = 256
BLOCKS_PER_CHUNK = KEY_CHUNK // SEL_BLOCK
V_TILE = 128
BF16_ROWS = 16
V_ROWS = HEAD_DIM + BF16_ROWS
V_TILE_ROWS = N_KV * V_ROWS
CMP_ROW_STEP = 128
WIN_BLOCKS = WINDOW // SEL_BLOCK
WIN_TILES = WINDOW // V_TILE + 1
NEAR_TILES = (REL_MAX_DIST + Q_BLOCK - 1) // SEL_BLOCK + 1
ROW_TILE = 512
INPROJ_TILE = 1024
INPROJ_SPLIT = 4
STEP_CHUNKS = (16, 8, 4, 2)
assert all(a == 2 * b for a, b in zip(STEP_CHUNKS, STEP_CHUNKS[1:]))
MERGE_CHUNKS = 4
QK_AHEAD = 4
QK_DEPTH = 5
LOG2E = math.log2(math.e)
FF_CHUNK = 1024
VMEM_LIMIT = 56 * 1024 * 1024

F32 = jnp.float32
BF16 = jnp.bfloat16
_NT = (((1,), (1,)), ((), ()))


def _cparams(n_axes):
    return pltpu.CompilerParams(dimension_semantics=("arbitrary",) * n_axes,
                                vmem_limit_bytes=VMEM_LIMIT)


def _with_ones_rows(vt):
    ones = jnp.ones((BF16_ROWS, vt.shape[1]), vt.dtype)
    parts = []
    for g in range(N_KV):
        parts += [vt[g * HEAD_DIM:(g + 1) * HEAD_DIM, :], ones]
    return jnp.concatenate(parts, axis=0)


def _swap_halves(p0, p1):
    low = lax.broadcasted_iota(jnp.int32, p0.shape, 1) < LANES // 2
    return (jnp.where(low, p0, pltpu.roll(p1, LANES // 2, 1)),
            jnp.where(low, pltpu.roll(p0, LANES // 2, 1), p1))


def _mod_kernel(c_ref, w_ref, b_ref, o_ref):
    c = c_ref[...]
    a = c * jax.nn.sigmoid(c)
    o_ref[...] = jnp.dot(a, w_ref[...], preferred_element_type=F32,
                         precision=lax.Precision.HIGHEST) + b_ref[...]


def _mod_call(c_pad, w_ada, b_ada):
    rows, d = c_pad.shape
    n = w_ada.shape[1]
    tn = n // N_MOD
    return pl.pallas_call(
        _mod_kernel,
        grid=(n // tn,),
        in_specs=[pl.BlockSpec((rows, d), lambda j: (0, 0)),
                  pl.BlockSpec((d, tn), lambda j: (0, j)),
                  pl.BlockSpec((1, tn), lambda j: (0, j))],
        out_specs=pl.BlockSpec((rows, tn), lambda j: (0, j)),
        out_shape=jax.ShapeDtypeStruct((rows, n), F32),
        compiler_params=_cparams(1),
        name="adaln_mod",
    )(c_pad, w_ada, b_ada.reshape(1, n))


_N_KC, _N_VC, _N_KS, _N_KW, _N_CONV = 0, KV_W, 2 * KV_W, 3 * KV_W, 4 * KV_W
_T_Q, _T_VS, _T_VW, _T_G = 0, ATTN_W, ATTN_W + KV_W, ATTN_W + 2 * KV_W
_G_ROWS = GQA * SUBLANES


def _inproj_kernel(x_ref, mod_ref, n1_ref, wn_ref, wt_ref, qn_ref, kn_ref, cw_ref, bdq_ref, bdk_ref,
                   qt_out, kc_out, vc_out, ks_out, vst_out, kw_out, vwt_out, gt_out, conv_out,
                   carry, *, conv_w):
    t = pl.program_id(1)
    tm = x_ref.shape[1]
    sub = tm // INPROJ_SPLIT

    @pl.when(t == 0)
    def _():
        carry[...] = jnp.zeros_like(carry)

    prev2, prev1 = carry[6:7, :], carry[7:8, :]
    c_bg = _N_CONV + conv_w
    c_u = c_bg + conv_w

    for part in range(INPROJ_SPLIT):
        rows = slice(part * sub, (part + 1) * sub)
        x = x_ref[0, rows, :]
        ms = jnp.mean(x * x, axis=-1, keepdims=True)
        y = x * lax.rsqrt(ms + EPS) * n1_ref[...]
        h = (y * (1.0 + mod_ref[0, 1:2, :]) + mod_ref[0, 0:1, :]).astype(BF16)

        def proj(a, b):
            return jnp.dot(h, wn_ref[:, a:b], preferred_element_type=F32)

        def proj_t(a, b):
            return lax.dot_general(wt_ref[a:b, :], h, _NT, preferred_element_type=F32)

        def head_norm(v, gain):
            ssq = jnp.dot((v * v).astype(BF16), bdk_ref[...], preferred_element_type=F32)
            return v * lax.rsqrt(ssq * (1.0 / HEAD_DIM) + EPS) * gain

        qf = proj_t(_T_Q, _T_VS)
        vg = proj_t(_T_VS, _T_G + _G_ROWS)
        kv = proj(_N_KC, _N_CONV)
        cv = proj(_N_CONV, c_u + conv_w)

        ssq = jnp.dot(bdq_ref[...], (qf * qf).astype(BF16), preferred_element_type=F32)
        qf = qf * lax.rsqrt(ssq * (1.0 / HEAD_DIM) + EPS) * qn_ref[...]
        gf = jax.nn.sigmoid(vg[2 * KV_W:, :])
        for c in range(sub // LANES):
            blk = (part * sub) // Q_BLOCK + 2 * c
            cols = slice(c * LANES, (c + 1) * LANES)
            for g in range(N_KV):
                pc = [qf[(g * GQA + r) * HEAD_DIM:(g * GQA + r + 1) * HEAD_DIM, cols]
                      for r in range(GQA)]
                lo01, hi01 = _swap_halves(pc[0], pc[1])
                lo23, hi23 = _swap_halves(pc[2], pc[3])
                qt_out[0, blk, g] = jnp.concatenate([lo01, lo23], axis=1).astype(BF16)
                qt_out[0, blk + 1, g] = jnp.concatenate([hi01, hi23], axis=1).astype(BF16)
            pc = [gf[r * SUBLANES:(r + 1) * SUBLANES, cols] for r in range(GQA)]
            lo01, hi01 = _swap_halves(pc[0], pc[1])
            lo23, hi23 = _swap_halves(pc[2], pc[3])
            gt_out[0, blk] = jnp.concatenate([lo01, lo23], axis=1)
            gt_out[0, blk + 1] = jnp.concatenate([hi01, hi23], axis=1)

        vs_f = _with_ones_rows(vg[:KV_W, :].astype(BF16))
        vw_f = _with_ones_rows(vg[KV_W:2 * KV_W, :].astype(BF16))
        for j in range(sub // KEY_CHUNK):
            vst_out[0, (part * sub) // KEY_CHUNK + j] = vs_f[:, j * KEY_CHUNK:(j + 1) * KEY_CHUNK]
        for j in range(sub // V_TILE):
            vwt_out[0, (part * sub) // V_TILE + j] = vw_f[:, j * V_TILE:(j + 1) * V_TILE]

        kc_out[0, rows, :] = kv[:, _N_KC:_N_VC]
        vc_out[0, rows, :] = kv[:, _N_VC:_N_KS]
        ks_out[0, rows, :] = head_norm(kv[:, _N_KS:_N_KW], kn_ref[...]).astype(BF16)
        kw_out[0, rows, :] = head_norm(kv[:, _N_KW:_N_CONV], kn_ref[...]).astype(BF16)

        z = cv[:, :conv_w] * cv[:, 2 * conv_w:]
        row = lax.broadcasted_iota(jnp.int32, z.shape, 0)
        z1 = jnp.where(row == 0, prev1, pltpu.roll(z, 1, 0))
        z2 = jnp.where(row == 0, prev2, jnp.where(row == 1, prev1, pltpu.roll(z, 2, 0)))
        zc = cw_ref[0:1, :] * z2 + cw_ref[1:2, :] * z1 + cw_ref[2:3, :] * z
        conv_out[0, rows, :] = (cv[:, conv_w:2 * conv_w] * zc).astype(BF16)
        prev2, prev1 = z[sub - 2:sub - 1, :], z[sub - 1:sub, :]
        if part == INPROJ_SPLIT - 1:
            carry[...] = z[sub - SUBLANES:sub, :]


def _inproj_call(x, mod, norm1, w_nat, w_tr, qn_col, kn_t, conv_w, bdq, bdk):
    b, t, d = x.shape
    tm = min(INPROJ_TILE, t)
    cw = conv_w.shape[1]
    nq = t // Q_BLOCK
    row_spec = lambda w: pl.BlockSpec((1, tm, w), lambda i, j: (i, j, 0))
    full = lambda shp: pl.BlockSpec(shp, lambda i, j: (0,) * len(shp))
    vt_spec = lambda w: pl.BlockSpec((1, tm // w, V_TILE_ROWS, w), lambda i, j: (i, j, 0, 0))
    vt_shape = lambda w: jax.ShapeDtypeStruct((b, t // w, V_TILE_ROWS, w), BF16)
    kv = lambda dt: jax.ShapeDtypeStruct((b, t, KV_W), dt)
    out_specs = [pl.BlockSpec((1, tm // Q_BLOCK, N_KV, HEAD_DIM, GQ), lambda i, j: (i, j, 0, 0, 0)),
                 row_spec(KV_W), row_spec(KV_W), row_spec(KV_W), vt_spec(KEY_CHUNK), row_spec(KV_W),
                 vt_spec(V_TILE),
                 pl.BlockSpec((1, tm // Q_BLOCK, SUBLANES, GQ), lambda i, j: (i, j, 0, 0)),
                 row_spec(cw)]
    out_shape = [jax.ShapeDtypeStruct((b, nq, N_KV, HEAD_DIM, GQ), BF16),
                 kv(F32), kv(F32), kv(BF16), vt_shape(KEY_CHUNK), kv(BF16), vt_shape(V_TILE),
                 jax.ShapeDtypeStruct((b, nq, SUBLANES, GQ), F32),
                 jax.ShapeDtypeStruct((b, t, cw), BF16)]
    return pl.pallas_call(
        functools.partial(_inproj_kernel, conv_w=cw),
        grid=(b, t // tm),
        in_specs=[row_spec(d),
                  pl.BlockSpec((1, N_MOD, d), lambda i, j: (i, 0, 0)),
                  full((1, d)), full(w_nat.shape), full(w_tr.shape), full((ATTN_W, 1)),
                  full((1, KV_W)), full((CONV_K, cw)), full((ATTN_W, ATTN_W)), full((KV_W, KV_W))],
        out_specs=out_specs,
        out_shape=out_shape,
        scratch_shapes=[pltpu.VMEM((SUBLANES, cw), F32)],
        compiler_params=_cparams(2),
        name="inproj",
    )(x, mod, norm1, w_nat, w_tr, qn_col, kn_t, conv_w, bdq, bdk)


def _compress_kernel(kx_ref, vx_ref, pek_ref, pev_ref, w1k_ref, w1v_ref, w2k_ref, w2vt_ref,
                     kn_ref, bdk_ref, kc_out, vct_out):
    def hidden(x_ref, pe_ref, w1_ref):
        n = x_ref.shape[1] // CMP_STRIDE
        u = jnp.zeros((n, w1_ref.shape[2]), F32)
        v = jnp.zeros((n, w1_ref.shape[2]), F32)
        for r in range(0, CMP_STRIDE, 2):
            tok = [x_ref[0, pl.ds(r + d, n, stride=CMP_STRIDE), :] for d in range(2)]
            cols = slice(r * KV_W, (r + 2) * KV_W)
            for a, acc in ((0, "u"), (1, "v")):
                lhs = jnp.concatenate([tok[d] + pe_ref[a:a + 1, (r + d) * KV_W:(r + d + 1) * KV_W]
                                       for d in range(2)], axis=1).astype(BF16)
                prod = jnp.dot(lhs, w1_ref[a, cols, :], preferred_element_type=F32)
                if acc == "u":
                    u = u + prod
                else:
                    v = v + prod
        hid = u + pltpu.roll(v, n - 1, 0)
        return jax.nn.gelu(hid, approximate=True).astype(BF16)

    kc = jnp.dot(hidden(kx_ref, pek_ref, w1k_ref), w2k_ref[...], preferred_element_type=F32)
    ssq = jnp.dot((kc * kc).astype(BF16), bdk_ref[...], preferred_element_type=F32)
    kc_out[0] = (kc * lax.rsqrt(ssq * (1.0 / HEAD_DIM) + EPS) * kn_ref[...]).astype(BF16)
    vct_out[0] = _with_ones_rows(lax.dot_general(w2vt_ref[...], hidden(vx_ref, pev_ref, w1v_ref), _NT,
                                                 preferred_element_type=F32).astype(BF16))


def _compress_call(kx, vx, pek, pev, w1k, w1v, w2k, w2vt, kn_t, bdk):
    b, t, _ = kx.shape
    nrow = t // CMP_STRIDE
    wide, hid2 = w1k.shape[1], w1k.shape[2]
    full = lambda shp: pl.BlockSpec(shp, lambda i: (0,) * len(shp))
    xs = pl.BlockSpec((1, t, KV_W), lambda i: (i, 0, 0))
    return pl.pallas_call(
        _compress_kernel,
        grid=(b,),
        in_specs=[xs, xs, full((2, wide)), full((2, wide)), full((2, wide, hid2)),
                  full((2, wide, hid2)), full((hid2, KV_W)), full((KV_W, hid2)),
                  full((1, KV_W)), full((KV_W, KV_W))],
        out_specs=[pl.BlockSpec((1, nrow, KV_W), lambda i: (i, 0, 0)),
                   pl.BlockSpec((1, V_TILE_ROWS, nrow), lambda i: (i, 0, 0))],
        out_shape=[jax.ShapeDtypeStruct((b, nrow, KV_W), BF16),
                   jax.ShapeDtypeStruct((b, V_TILE_ROWS, nrow), BF16)],
        compiler_params=_cparams(1),
        name="compress",
    )(kx, vx, pek, pev, w1k, w1v, w2k, w2vt, kn_t, bdk)


def _attn_kernel(q_ref, kc_ref, vct_ref, ks_ref, vst_ref, kw_ref, vwt_ref, gt_ref, ovt_ref,
                 eb_ref, sb_ref, wb_ref, o_ref, sel_scr, acc_scr, out_scr, qp_scr, s_scr,
                 fin_scr, *, n_cmp_rows, eb_shift):
    n_sb = sel_scr.shape[1]
    ci = jnp.minimum(pl.program_id(1), n_sb - 1)
    par = lax.rem(ci, 2)
    n_total = ks_ref.shape[1] // KEY_CHUNK
    vsl = lambda g: slice(g * V_ROWS, (g + 1) * V_ROWS)

    @pl.when((pl.program_id(0) == 0) & (pl.program_id(1) == 0))
    def _():
        def zero(g, carry):
            fin_scr[g] = jnp.zeros(fin_scr.shape[1:], F32)
            sel_scr[g] = jnp.zeros(sel_scr.shape[1:], F32)
            return carry
        lax.fori_loop(0, N_KV, zero, 0)

    def chunk_qk(g, c):
        kk = ks_ref[0, pl.ds(pl.multiple_of(c * KEY_CHUNK, KEY_CHUNK), KEY_CHUNK), :]
        return jnp.dot(kk, qp_scr[g], preferred_element_type=F32).astype(BF16)

    n_chunks = ci // BLOCKS_PER_CHUNK + 1
    smallest = STEP_CHUNKS[-1]
    per_biggest = STEP_CHUNKS[0] // smallest
    n_smallest = (n_chunks + smallest - 1) // smallest
    n_smallest = jnp.where(lax.rem(n_smallest, per_biggest) == per_biggest - 1,
                           n_smallest + 1, n_smallest)
    step_counts, step_starts = [], []
    first_chunk = 0
    for k, size in enumerate(STEP_CHUNKS):
        count = n_smallest // (size // smallest)
        step_counts.append(count if k == 0 else lax.rem(count, 2))
        step_starts.append(first_chunk)
        first_chunk = first_chunk + step_counts[k] * size

    def step_units(k, i):
        return [(g, jnp.minimum(step_starts[k] + STEP_CHUNKS[k] * i + j, n_total - 1))
                for g in range(N_KV) for j in range(STEP_CHUNKS[k])]

    def first_step_from(k):
        last = len(STEP_CHUNKS) - 1
        return ([(step_counts[j] > 0, step_units(j, 0)) for j in range(k, last)]
                + [(None, step_units(last, 0))])

    def pick_units(options):
        units = options[-1][1][:QK_AHEAD]
        for cond, cand in reversed(options[:-1]):
            units = [(jnp.where(cond, g_a, g_b), jnp.where(cond, c_a, c_b))
                     for (g_a, c_a), (g_b, c_b) in zip(cand[:QK_AHEAD], units)]
        return [(g, jnp.minimum(c, n_total - 1)) for g, c in units]

    def prefetch_scores(units):
        for k in range(QK_AHEAD):
            s_scr[k] = chunk_qk(*units[k])

    def before_loops(rows):
        pieces = []
        for g in range(N_KV):
            for half in range(GQ // LANES):
                a = fin_scr[g, :, half * LANES:(half + 1) * LANES]
                stacked = jnp.concatenate([a, pltpu.roll(a, Q_BLOCK, 1)], axis=0)
                pieces.append(stacked.T[:Q_BLOCK, :])
        o_ref[0] = jnp.concatenate(pieces, axis=1).astype(BF16)

        zeros_q = jnp.zeros((HEAD_DIM, GQ), BF16)
        qp_scr[0] = jnp.concatenate([q_ref[0, 0, 0], zeros_q], axis=0)
        qp_scr[1] = jnp.concatenate([zeros_q, q_ref[0, 0, 1]], axis=0)

        nsel = min(n_sb, rows * CMP_STRIDE // SEL_BLOCK)
        lane = lax.broadcasted_iota(jnp.int32, (nsel, LANES), 1)
        cmp_c = n_cmp_rows - 4
        e0 = cmp_c - 4 * ci + jnp.where(par == 0, eb_shift[0], eb_shift[1])
        e0 = pl.multiple_of(e0, 8)
        o_cmp = []
        imp = []
        w0 = ci // 2 - (WIN_TILES - 1)
        win_tiles = [jnp.maximum(w0 + j, 0) for j in range(WIN_TILES)]
        s_cmp = [jnp.dot(kc_ref[0, :rows, :], qp_scr[g], preferred_element_type=F32)
                 for g in range(N_KV)]
        for g in range(N_KV):
            sc = s_cmp[g] + eb_ref[par, g, pl.ds(e0, rows), :]
            m = jnp.max(sc, axis=0, keepdims=True)
            e = jnp.exp2(sc - m).astype(BF16)
            ov = jnp.dot(vct_ref[0, vsl(g), :rows], e, preferred_element_type=F32)
            inv = jnp.where(m > 0.5 * NEG, 1.0 / ov[HEAD_DIM:HEAD_DIM + 1, :], 0.0)
            o_cmp.append(ov[:HEAD_DIM, :] * inv)
            ir = jnp.dot(ovt_ref[:nsel, :rows], e, preferred_element_type=F32) * inv
            a = ir[:, :LANES] + ir[:, LANES:]
            imp.append(a + pltpu.roll(a, Q_BLOCK, 1))

        s_win = [[jnp.dot(kw_ref[0, pl.ds(pl.multiple_of(tj * V_TILE, V_TILE), V_TILE), :],
                          qp_scr[g], preferred_element_type=F32).astype(BF16) for tj in win_tiles]
                 for g in range(N_KV)]

        prefetch_scores(pick_units(first_step_from(0)))

        jidx = lax.broadcasted_iota(jnp.int32, (nsel, LANES), 0)
        jf = jidx.astype(F32)
        valid = jidx <= ci
        forced = (jidx == 0) | (jidx == ci) | (jidx == ci - 1)
        score = jnp.where(valid, jnp.where(forced, -2.0, jnp.where(lane < Q_BLOCK, imp[0], imp[1])),
                          -1.0)
        sel = jnp.where(forced, 1.0, 0.0)

        def first_max(score):
            pairs = [(score[r:r + SUBLANES], jf[r:r + SUBLANES]) for r in range(0, nsel, SUBLANES)]
            while len(pairs) > 1:
                nxt = []
                for (va, ia), (vb, ib) in zip(pairs[0::2], pairs[1::2]):
                    keep = va >= vb
                    nxt.append((jnp.where(keep, va, vb), jnp.where(keep, ia, ib)))
                pairs = nxt + pairs[len(pairs) - len(pairs) % 2:]
            v8, i8 = pairs[0]
            mx = jnp.max(v8, axis=0, keepdims=True)
            return jnp.min(jnp.where(v8 == mx, i8, float(nsel)), axis=0, keepdims=True)

        for _ in range(min(N_SELECT, nsel) - 3):
            first = first_max(score)
            hit = jf == first
            sel = jnp.where(hit, 1.0, sel)
            score = jnp.where(hit, -2.0, score)
        selneg = jnp.where((sel > 0.5) & valid, 0.0, NEG)
        swapped = pltpu.roll(selneg, Q_BLOCK, 1)
        left = jnp.where(lane < Q_BLOCK, selneg, swapped)
        right = jnp.where(lane < Q_BLOCK, swapped, selneg)
        sel_scr[0, :nsel] = jnp.concatenate([left, left], axis=1)
        sel_scr[1, :nsel] = jnp.concatenate([right, right], axis=1)

        for g in range(N_KV):
            parts = []
            for i in range(2 * WIN_TILES):
                delta = par + WIN_BLOCKS - i
                ok = (delta >= 0) & (delta <= WIN_BLOCKS) & (delta <= ci)
                tile = jnp.where(ok, delta, WIN_BLOCKS + 1)
                half = s_win[g][i // 2][(i % 2) * SEL_BLOCK:(i % 2 + 1) * SEL_BLOCK, :]
                parts.append(half + wb_ref[g, tile])
            s = jnp.concatenate(parts, axis=0)
            m = jnp.max(s, axis=0, keepdims=True)
            pb = jnp.exp2(s - m)
            o_win = jnp.zeros((V_ROWS, GQ), F32)
            for j, tj in enumerate(win_tiles):
                o_win = o_win + jnp.dot(vwt_ref[0, tj, vsl(g), :], pb[j * V_TILE:(j + 1) * V_TILE, :],
                                        preferred_element_type=F32)
            w_scale = (gt_ref[0, 0, 2 * N_KV + g:2 * N_KV + g + 1, :]
                       * (1.0 / o_win[HEAD_DIM:HEAD_DIM + 1, :]))
            out_scr[g] = gt_ref[0, 0, g:g + 1, :] * o_cmp[g] + w_scale * o_win[:HEAD_DIM, :]
            acc_scr[g] = jnp.zeros((V_ROWS, GQ), F32)

    row_steps = [r for r in range(CMP_ROW_STEP, n_cmp_rows + 1, CMP_ROW_STEP)] or [n_cmp_rows]
    variant = jnp.minimum((4 * ci + 2) // CMP_ROW_STEP, len(row_steps) - 1)
    lax.switch(variant, [functools.partial(before_loops, r) for r in row_steps])

    def chunk_softmax(s, g, c, extra):
        parts = []
        for i in range(BLOCKS_PER_CHUNK):
            kb = c * BLOCKS_PER_CHUNK + i
            blk = s[i * SEL_BLOCK:(i + 1) * SEL_BLOCK, :]
            mrow = sel_scr[g, pl.ds(kb, 1), :]
            if extra is not None:
                mrow = mrow + extra
            tile = jnp.clip(ci - kb, 0, NEAR_TILES)
            parts.append(blk + sb_ref[g, tile] + mrow.astype(BF16))
        s = jnp.concatenate(parts, axis=0)
        m_c = jnp.max(s, axis=0, keepdims=True)
        return m_c.astype(F32), jnp.exp2(s - m_c)

    def chunk_pv(pb, g, c):
        return jnp.dot(vst_ref[0, c, vsl(g), :], pb, preferred_element_type=F32)

    def merge_step(units, extras, next_units, carry):
        scores = {k: s_scr[k] for k in range(QK_AHEAD)}
        pending = list(range(QK_AHEAD, len(units) + QK_AHEAD))

        def issue_scores():
            k = pending.pop(0)
            if k < len(units):
                scores[k] = chunk_qk(*units[k])
            else:
                s_scr[k - len(units)] = chunk_qk(*next_units[k - len(units)])

        for _ in range(min(QK_DEPTH - QK_AHEAD, len(pending))):
            issue_scores()
        results = []
        for k, (g, c) in enumerate(units):
            m_c, pb = chunk_softmax(scores.pop(k), g, c, extras[k])
            if pending:
                issue_scores()
            results.append((m_c, chunk_pv(pb, g, c)))
        per_g = len(units) // N_KV
        new = []
        for g in range(N_KV):
            m_run = carry[g]
            acc = acc_scr[g]
            for first in range(g * per_g, (g + 1) * per_g, MERGE_CHUNKS):
                stats = results[first:min(first + MERGE_CHUNKS, (g + 1) * per_g)]
                m_new = m_run
                for m_c, _ in stats:
                    m_new = jnp.maximum(m_new, m_c)
                acc = jnp.exp2(m_run - m_new) * acc
                for m_c, pv in stats:
                    acc = acc + jnp.exp2(m_c - m_new) * pv
                m_run = m_new
            acc_scr[g] = acc
            new.append(m_run)
        return tuple(new)

    def loop_step(k):
        def step(i, carry):
            first = step_starts[k] + STEP_CHUNKS[k] * i
            extras = [None if j == 0 else jnp.where(first + j < n_chunks, 0.0, NEG)
                      for _ in range(N_KV) for j in range(STEP_CHUNKS[k])]
            again = [(i + 1 < step_counts[k], step_units(k, i + 1))] if k == 0 else []
            nxt = pick_units(again + first_step_from(min(k + 1, len(STEP_CHUNKS) - 1)))
            return merge_step(step_units(k, i), extras, nxt, carry)
        return step

    carry = (jnp.full((1, GQ), NEG, F32),) * N_KV
    for k in range(len(STEP_CHUNKS)):
        carry = lax.fori_loop(0, step_counts[k], loop_step(k), carry)

    for g in range(N_KV):
        acc = acc_scr[g]
        scale = gt_ref[0, 0, N_KV + g:N_KV + g + 1, :] * (1.0 / acc[HEAD_DIM:HEAD_DIM + 1, :])
        fin_scr[g] = out_scr[g] + scale * acc[:HEAD_DIM, :]


def _attn_call(qt, kc, vct, ks, vst, kw, vwt, gt, ovt, ebank, sbank, wbank, eb_shift):
    b, nq = qt.shape[0], qt.shape[1]
    t = ks.shape[1]
    n_cmp_rows = kc.shape[1]
    n_sb = t // SEL_BLOCK
    per_b = lambda shp: pl.BlockSpec((1,) + shp[1:], lambda i, j: (i,) + (0,) * (len(shp) - 1))
    per_q = lambda shp: pl.BlockSpec(
        (1, 1) + shp[2:], lambda i, j: (i, jnp.minimum(j, nq - 1)) + (0,) * (len(shp) - 2))
    full = lambda shp: pl.BlockSpec(shp, lambda i, j: (0,) * len(shp))
    args = (qt, kc, vct, ks, vst, kw, vwt, gt, ovt, ebank, sbank, wbank)
    specs = [per_q(qt.shape), per_b(kc.shape), per_b(vct.shape), per_b(ks.shape), per_b(vst.shape),
             per_b(kw.shape), per_b(vwt.shape), per_q(gt.shape), full(ovt.shape),
             full(ebank.shape), full(sbank.shape), full(wbank.shape)]
    acc_like = pltpu.VMEM((N_KV, HEAD_DIM, GQ), F32)
    return pl.pallas_call(
        functools.partial(_attn_kernel, n_cmp_rows=n_cmp_rows, eb_shift=eb_shift),
        grid=(b, nq + 1),
        in_specs=specs,
        out_specs=pl.BlockSpec((1, Q_BLOCK, ATTN_W), lambda i, j: (i, jnp.maximum(j - 1, 0), 0)),
        out_shape=jax.ShapeDtypeStruct((b, t, ATTN_W), BF16),
        scratch_shapes=[pltpu.VMEM((N_KV, n_sb, GQ), F32),
                        pltpu.VMEM((N_KV, V_ROWS, GQ), F32),
                        acc_like,
                        pltpu.VMEM((N_KV, KV_W, GQ), BF16),
                        pltpu.VMEM((QK_AHEAD, KEY_CHUNK, GQ), BF16),
                        acc_like],
        compiler_params=_cparams(2),
        name="nsa_attention",
    )(*args)


def _ffn_kernel(x_ref, a_ref, c_ref, mod_ref, n2_ref, wo_ref, w1_ref, w2_ref, o_ref):
    aw = a_ref.shape[2]
    mix = jnp.dot(a_ref[0], wo_ref[0:aw, :], preferred_element_type=F32)
    mix = mix + jnp.dot(c_ref[0], wo_ref[aw:, :], preferred_element_type=F32)
    x1 = x_ref[0] + mod_ref[0, 2:3, :] * mix
    ms = jnp.mean(x1 * x1, axis=-1, keepdims=True)
    y = x1 * lax.rsqrt(ms + EPS) * n2_ref[...]
    h2 = (y * (1.0 + mod_ref[0, 4:5, :]) + mod_ref[0, 3:4, :]).astype(BF16)
    d_ff = w1_ref.shape[1]
    ff = jnp.zeros(x1.shape, F32)
    for j in range(d_ff // FF_CHUNK):
        a = jnp.dot(h2, w1_ref[:, j * FF_CHUNK:(j + 1) * FF_CHUNK], preferred_element_type=F32)
        a = jnp.maximum(a, 0.0)
        ff = ff + jnp.dot((a * a).astype(BF16), w2_ref[j * FF_CHUNK:(j + 1) * FF_CHUNK, :],
                          preferred_element_type=F32)
    o_ref[0] = x1 + mod_ref[0, 5:6, :] * ff


def _ffn_call(x, attn, conv, mod, norm2, w_out, w_ff1, w_ff2):
    b, t, d = x.shape
    tm = min(ROW_TILE, t)
    row_spec = lambda w: pl.BlockSpec((1, tm, w), lambda i, j: (i, j, 0))
    full = lambda shp: pl.BlockSpec(shp, lambda i, j: (0,) * len(shp),
                                    pipeline_mode=pl.Buffered(1))
    return pl.pallas_call(
        _ffn_kernel,
        grid=(b, t // tm),
        in_specs=[row_spec(d), row_spec(attn.shape[2]), row_spec(conv.shape[2]),
                  pl.BlockSpec((1, N_MOD, d), lambda i, j: (i, 0, 0)),
                  full((1, d)), full(w_out.shape), full(w_ff1.shape), full(w_ff2.shape)],
        out_specs=row_spec(d),
        out_shape=jax.ShapeDtypeStruct((b, t, d), F32),
        compiler_params=_cparams(2),
        name="outproj_mlp",
    )(x, attn, conv, mod, norm2, w_out, w_ff1, w_ff2)


def _block_diag_ones(n):
    idx = np.arange(n) // HEAD_DIM
    return jnp.asarray(idx[:, None] == idx[None, :], dtype=BF16)


def _pack_w_in(w_in):
    d = w_in.shape[0]
    conv_w = d - ATTN_W
    sizes = [ATTN_W] + [KV_W] * 6 + [N_BRANCH * N_HEADS] + [conv_w] * 3
    offs = np.concatenate([[0], np.cumsum(sizes)])
    part = lambda i: w_in[:, offs[i]:offs[i + 1]]
    q, kc, vc, ks, vs, kw, vw, g, cgate, bgate, u = (part(i) for i in range(11))
    w_nat = jnp.concatenate([kc, vc, ks, kw, cgate, bgate, u], axis=1).astype(BF16)
    gt = g.reshape(d, N_KV, GQA, N_BRANCH).transpose(2, 3, 1, 0).reshape(GQA, N_BRANCH * N_KV, d)
    gt = jnp.pad(gt, ((0, 0), (0, SUBLANES - N_BRANCH * N_KV), (0, 0))).reshape(_G_ROWS, d)
    w_tr = jnp.concatenate([q.T, vs.T, vw.T, gt], axis=0).astype(BF16)
    return w_nat, w_tr


def _expand_w1(w1):
    hid = w1.shape[1]
    w = w1.reshape(2, CMP_STRIDE, HEAD_DIM, hid).astype(BF16)
    zero = jnp.zeros_like(w)
    per_group = [jnp.concatenate([w if k == g else zero for k in range(N_KV)], axis=-1)
                 for g in range(N_KV)]
    return jnp.stack(per_group, axis=2).reshape(2, CMP_STRIDE * KV_W, N_KV * hid)


def _expand_w2(w2):
    hid = w2.shape[0]
    eye = jnp.eye(N_KV, dtype=w2.dtype).reshape(N_KV, 1, N_KV, 1)
    return (w2.reshape(1, hid, 1, HEAD_DIM) * eye).reshape(N_KV * hid, KV_W).astype(BF16)


def _expand_pe(pe):
    p = pe.reshape(2, CMP_STRIDE, 1, HEAD_DIM)
    return jnp.broadcast_to(p, (2, CMP_STRIDE, N_KV, HEAD_DIM)).reshape(2, CMP_STRIDE * KV_W)


def _bucket_thresholds():
    n = np.arange(2 * REL_MAX_DIST)
    max_exact = REL_BUCKETS // 2
    nf = np.maximum(n, max_exact).astype(np.float32)
    ratio = np.log(nf / np.float32(max_exact)) / np.float32(math.log(REL_MAX_DIST / max_exact))
    large = max_exact + (ratio * np.float32(REL_BUCKETS - max_exact)).astype(np.int32)
    table = np.where(n < max_exact, n, np.minimum(large, REL_BUCKETS - 1))
    return tuple(int(np.searchsorted(table, k, side="left")) for k in range(REL_BUCKETS))


def _bank_call(bias_rows, lead, n_tiles, dist_fn, name):
    nl = len(lead)
    thr = _bucket_thresholds()

    def body(rows_ref, o_ref):
        lead_ids = [pl.program_id(a) for a in range(nl)]
        row = lax.broadcasted_iota(jnp.int32, (SEL_BLOCK, GQ), 0)
        qi = lax.broadcasted_iota(jnp.int32, (SEL_BLOCK, GQ), 1) & (Q_BLOCK - 1)

        def tile(t, carry):
            dist, ok = dist_fn(lead_ids, t, row, qi)
            v = jnp.broadcast_to(rows_ref[0, 0:1, :], (SEL_BLOCK, GQ))
            for k in range(1, REL_BUCKETS):
                v = jnp.where(dist >= thr[k], rows_ref[0, k:k + 1, :], v)
            o_ref[(0,) * (nl + 1) + (t,)] = jnp.where(ok, v, NEG)
            return carry

        lax.fori_loop(0, n_tiles, tile, 0)

    return pl.pallas_call(
        body,
        grid=tuple(lead) + (N_KV,),
        in_specs=[pl.BlockSpec((1, REL_BUCKETS, GQ), lambda *i: (i[nl], 0, 0))],
        out_specs=pl.BlockSpec((1,) * (nl + 1) + (n_tiles, SEL_BLOCK, GQ),
                               lambda *i: tuple(i) + (0, 0, 0)),
        out_shape=jax.ShapeDtypeStruct(tuple(lead) + (N_KV, n_tiles, SEL_BLOCK, GQ), F32),
        compiler_params=_cparams(nl + 1),
        name=name,
    )(bias_rows)


def _bias_banks(rel_bias, t):
    n_cmp_rows = t // CMP_STRIDE
    rows = rel_bias.reshape(REL_BUCKETS, N_KV, GQA).transpose(1, 0, 2)
    rows = jnp.repeat(rows, Q_BLOCK, axis=2) * LOG2E

    def sel_dist(lead, tile, row, qi):
        dist = SEL_BLOCK * tile + qi - row
        return dist, dist >= 0

    sbank = _bank_call(rows, (), NEAR_TILES + 1, sel_dist, "bias_bank_sel")

    def win_dist(lead, tile, row, qi):
        dist = SEL_BLOCK * tile + qi - row
        return dist, (dist >= 0) & (dist < WINDOW)

    wbank = _bank_call(rows, (), WIN_BLOCKS + 2, win_dist, "bias_bank_win")

    cmp_c = n_cmp_rows - 4
    shifts = tuple(int((-(cmp_c - 4 * p)) % 8) for p in range(2))
    n_tiles = (cmp_c + n_cmp_rows + 8 + SEL_BLOCK - 1) // SEL_BLOCK

    def cmp_dist(lead, tile, row, qi):
        e = SEL_BLOCK * tile + row - jnp.where(lead[0] == 0, shifts[0], shifts[1])
        dist = qi - CMP_STRIDE * e + (CMP_STRIDE * cmp_c - (CMP_BLOCK - 1))
        return dist, (dist >= 0) & (e >= 0)

    ebank = _bank_call(rows, (2,), n_tiles, cmp_dist, "bias_bank_cmp")
    ebank = ebank.reshape(2, N_KV, n_tiles * SEL_BLOCK, GQ)
    return ebank, sbank, wbank, shifts


def _overlap_t(t):
    n_cmp_rows = t // CMP_STRIDE
    n_sb = t // SEL_BLOCK
    c_start = np.arange(n_cmp_rows)[None, :] * CMP_STRIDE
    s_start = np.arange(n_sb)[:, None] * SEL_BLOCK
    ov = np.clip(np.minimum(c_start + CMP_BLOCK, s_start + SEL_BLOCK)
                 - np.maximum(c_start, s_start), 0, None) / CMP_BLOCK
    ov[:, n_cmp_rows - 1] = 0.0
    return jnp.asarray(ov, dtype=BF16)


def _layer(x, c_pad, w_in, q_norm, k_norm, cmp_pe_k, cmp_w1_k, cmp_w2_k, cmp_pe_v, cmp_w1_v,
           cmp_w2_v, rel_bias, conv_w, w_out, norm1, norm2, w_ada, b_ada, w_ff1, w_ff2):
    b, t, d = x.shape
    scale = HEAD_DIM ** -0.5

    mod = _mod_call(c_pad, w_ada, b_ada)[:b].reshape(b, N_MOD, d)

    qn_col = (jnp.tile(q_norm, N_HEADS) * (scale * LOG2E)).reshape(ATTN_W, 1)
    kn_t = jnp.tile(k_norm, N_KV).reshape(1, KV_W)
    bdq = _block_diag_ones(ATTN_W)
    bdk = _block_diag_ones(KV_W)
    w_nat, w_tr = _pack_w_in(w_in)
    qt, kc_raw, vc_raw, ks, vst, kw, vwt, gt, conv = _inproj_call(
        x, mod, norm1.reshape(1, d), w_nat, w_tr, qn_col, kn_t, conv_w, bdq, bdk)

    kc, vct = _compress_call(
        kc_raw, vc_raw, _expand_pe(cmp_pe_k), _expand_pe(cmp_pe_v), _expand_w1(cmp_w1_k), _expand_w1(cmp_w1_v),
        _expand_w2(cmp_w2_k), _expand_w2(cmp_w2_v).T, kn_t, bdk)

    ebank, sbank, wbank, eb_shift = _bias_banks(rel_bias, t)
    attn = _attn_call(qt, kc, vct, ks, vst, kw, vwt, gt, _overlap_t(t), ebank, sbank.astype(BF16),
                      wbank.astype(BF16), eb_shift)

    return _ffn_call(x, attn, conv, mod, norm2.reshape(1, d), w_out.astype(BF16),
                     w_ff1.astype(BF16), w_ff2.astype(BF16))


def kernel(x, c, w_in, q_norm, k_norm, cmp_pe_k, cmp_w1_k, cmp_w2_k, cmp_pe_v, cmp_w1_v, cmp_w2_v,
           rel_bias, conv_w, w_out, norm1, norm2, w_ada, b_ada, w_ff1, w_ff2):
    b = x.shape[0]
    c_pad = jnp.pad(c, ((0, (-b) % 8), (0, 0)))
    for l in range(w_in.shape[0]):
        x = _layer(x, c_pad, w_in[l], q_norm[l], k_norm[l], cmp_pe_k[l], cmp_w1_k[l], cmp_w2_k[l],
                   cmp_pe_v[l], cmp_w1_v[l], cmp_w2_v[l], rel_bias, conv_w[l], w_out[l],
                   norm1[l], norm2[l], w_ada[l], b_ada[l], w_ff1[l], w_ff2[l])
    return x
```

```python
import functools
import math

import numpy as np
import jax
import jax.numpy as jnp
from jax import lax
from jax.experimental import pallas as pl
from jax.experimental.pallas import tpu as pltpu

HEAD_DIM = 64
N_HEADS = 8
N_KV = 2
GQA = N_HEADS // N_KV
ATTN_W = N_HEADS * HEAD_DIM
KV_W = N_KV * HEAD_DIM
CONV_K = 3
CMP_BLOCK = 32
CMP_STRIDE = 16
SEL_BLOCK = 64
N_SELECT = 16
WINDOW = 512
Q_BLOCK = 64
REL_BUCKETS = 32
REL_MAX_DIST = 1024
N_MOD = 6
N_BRANCH = 3
EPS = 1e-6
NEG = -1e30

LANES = 128
SUBLANES = 8
GQ = GQA * Q_BLOCK
KEY_CHUNK = 256
BLOCKS_PER_CHUNK = KEY_CHUNK // SEL_BLOCK
V_TILE = 128
BF16_ROWS = 16
V_ROWS = HEAD_DIM + BF16_ROWS
V_TILE_ROWS = N_KV * V_ROWS
CMP_ROW_STEP = 128
WIN_BLOCKS = WINDOW // SEL_BLOCK
WIN_TILES = WINDOW // V_TILE + 1
NEAR_TILES = (REL_MAX_DIST + Q_BLOCK - 1) // SEL_BLOCK + 1
ROW_TILE = 512
INPROJ_TILE = 1024
INPROJ_SPLIT = 4
STEP_CHUNKS = (16, 8, 4, 2)
assert all(a == 2 * b for a, b in zip(STEP_CHUNKS, STEP_CHUNKS[1:]))
MERGE_CHUNKS = 4
QK_AHEAD = 4
QK_DEPTH = 5
LOG2E = math.log2(math.e)
FF_CHUNK = 1024
VMEM_LIMIT = 56 * 1024 * 1024

F32 = jnp.float32
BF16 = jnp.bfloat16
_NT = (((1,), (1,)), ((), ()))


def _cparams(n_axes):
    return pltpu.CompilerParams(dimension_semantics=("arbitrary",) * n_axes,
                                vmem_limit_bytes=VMEM_LIMIT)


def _with_ones_rows(vt):
    ones = jnp.ones((BF16_ROWS, vt.shape[1]), vt.dtype)
    parts = []
    for g in range(N_KV):
        parts += [vt[g * HEAD_DIM:(g + 1) * HEAD_DIM, :], ones]
    return jnp.concatenate(parts, axis=0)


def _swap_halves(p0, p1):
    low = lax.broadcasted_iota(jnp.int32, p0.shape, 1) < LANES // 2
    return (jnp.where(low, p0, pltpu.roll(p1, LANES // 2, 1)),
            jnp.where(low, pltpu.roll(p0, LANES // 2, 1), p1))


def _mod_kernel(c_ref, w_ref, b_ref, o_ref):
    c = c_ref[...]
    a = c * jax.nn.sigmoid(c)
    o_ref[...] = jnp.dot(a, w_ref[...], preferred_element_type=F32,
                         precision=lax.Precision.HIGHEST) + b_ref[...]


def _mod_call(c_pad, w_ada, b_ada):
    rows, d = c_pad.shape
    n = w_ada.shape[1]
    tn = n // N_MOD
    return pl.pallas_call(
        _mod_kernel,
        grid=(n // tn,),
        in_specs=[pl.BlockSpec((rows, d), lambda j: (0, 0)),
                  pl.BlockSpec((d, tn), lambda j: (0, j)),
                  pl.BlockSpec((1, tn), lambda j: (0, j))],
        out_specs=pl.BlockSpec((rows, tn), lambda j: (0, j)),
        out_shape=jax.ShapeDtypeStruct((rows, n), F32),
        compiler_params=_cparams(1),
        name="adaln_mod",
    )(c_pad, w_ada, b_ada.reshape(1, n))


_N_KC, _N_VC, _N_KS, _N_KW, _N_CONV = 0, KV_W, 2 * KV_W, 3 * KV_W, 4 * KV_W
_T_Q, _T_VS, _T_VW, _T_G = 0, ATTN_W, ATTN_W + KV_W, ATTN_W + 2 * KV_W
_G_ROWS = GQA * SUBLANES


def _inproj_kernel(x_ref, mod_ref, n1_ref, wn_ref, wt_ref, qn_ref, kn_ref, cw_ref, bdq_ref, bdk_ref,
                   qt_out, kc_out, vc_out, ks_out, vst_out, kw_out, vwt_out, gt_out, conv_out,
                   carry, *, conv_w):
    t = pl.program_id(1)
    tm = x_ref.shape[1]
    sub = tm // INPROJ_SPLIT

    @pl.when(t == 0)
    def _():
        carry[...] = jnp.zeros_like(carry)

    prev2, prev1 = carry[6:7, :], carry[7:8, :]
    c_bg = _N_CONV + conv_w
    c_u = c_bg + conv_w

    for part in range(INPROJ_SPLIT):
        rows = slice(part * sub, (part + 1) * sub)
        x = x_ref[0, rows, :]
        ms = jnp.mean(x * x, axis=-1, keepdims=True)
        y = x * lax.rsqrt(ms + EPS) * n1_ref[...]
        h = (y * (1.0 + mod_ref[0, 1:2, :]) + mod_ref[0, 0:1, :]).astype(BF16)

        def proj(a, b):
            return jnp.dot(h, wn_ref[:, a:b], preferred_element_type=F32)

        def proj_t(a, b):
            return lax.dot_general(wt_ref[a:b, :], h, _NT, preferred_element_type=F32)

        def head_norm(v, gain):
            ssq = jnp.dot((v * v).astype(BF16), bdk_ref[...], preferred_element_type=F32)
            return v * lax.rsqrt(ssq * (1.0 / HEAD_DIM) + EPS) * gain

        qf = proj_t(_T_Q, _T_VS)
        vg = proj_t(_T_VS, _T_G + _G_ROWS)
        kv = proj(_N_KC, _N_CONV)
        cv = proj(_N_CONV, c_u + conv_w)

        ssq = jnp.dot(bdq_ref[...], (qf * qf).astype(BF16), preferred_element_type=F32)
        qf = qf * lax.rsqrt(ssq * (1.0 / HEAD_DIM) + EPS) * qn_ref[...]
        gf = jax.nn.sigmoid(vg[2 * KV_W:, :])
        for c in range(sub // LANES):
            blk = (part * sub) // Q_BLOCK + 2 * c
            cols = slice(c * LANES, (c + 1) * LANES)
            for g in range(N_KV):
                pc = [qf[(g * GQA + r) * HEAD_DIM:(g * GQA + r + 1) * HEAD_DIM, cols]
                      for r in range(GQA)]
                lo01, hi01 = _swap_halves(pc[0], pc[1])
                lo23, hi23 = _swap_halves(pc[2], pc[3])
                qt_out[0, blk, g] = jnp.concatenate([lo01, lo23], axis=1).astype(BF16)
                qt_out[0, blk + 1, g] = jnp.concatenate([hi01, hi23], axis=1).astype(BF16)
            pc = [gf[r * SUBLANES:(r + 1) * SUBLANES, cols] for r in range(GQA)]
            lo01, hi01 = _swap_halves(pc[0], pc[1])
            lo23, hi23 = _swap_halves(pc[2], pc[3])
            gt_out[0, blk] = jnp.concatenate([lo01, lo23], axis=1)
            gt_out[0, blk + 1] = jnp.concatenate([hi01, hi23], axis=1)

        vs_f = _with_ones_rows(vg[:KV_W, :].astype(BF16))
        vw_f = _with_ones_rows(vg[KV_W:2 * KV_W, :].astype(BF16))
        for j in range(sub // KEY_CHUNK):
            vst_out[0, (part * sub) // KEY_CHUNK + j] = vs_f[:, j * KEY_CHUNK:(j + 1) * KEY_CHUNK]
        for j in range(sub // V_TILE):
            vwt_out[0, (part * sub) // V_TILE + j] = vw_f[:, j * V_TILE:(j + 1) * V_TILE]

        kc_out[0, rows, :] = kv[:, _N_KC:_N_VC]
        vc_out[0, rows, :] = kv[:, _N_VC:_N_KS]
        ks_out[0, rows, :] = head_norm(kv[:, _N_KS:_N_KW], kn_ref[...]).astype(BF16)
        kw_out[0, rows, :] = head_norm(kv[:, _N_KW:_N_CONV], kn_ref[...]).astype(BF16)

        z = cv[:, :conv_w] * cv[:, 2 * conv_w:]
        row = lax.broadcasted_iota(jnp.int32, z.shape, 0)
        z1 = jnp.where(row == 0, prev1, pltpu.roll(z, 1, 0))
        z2 = jnp.where(row == 0, prev2, jnp.where(row == 1, prev1, pltpu.roll(z, 2, 0)))
        zc = cw_ref[0:1, :] * z2 + cw_ref[1:2, :] * z1 + cw_ref[2:3, :] * z
        conv_out[0, rows, :] = (cv[:, conv_w:2 * conv_w] * zc).astype(BF16)
        prev2, prev1 = z[sub - 2:sub - 1, :], z[sub - 1:sub, :]
        if part == INPROJ_SPLIT - 1:
            carry[...] = z[sub - SUBLANES:sub, :]


def _inproj_call(x, mod, norm1, w_nat, w_tr, qn_col, kn_t, conv_w, bdq, bdk):
    b, t, d = x.shape
    tm = min(INPROJ_TILE, t)
    cw = conv_w.shape[1]
    nq = t // Q_BLOCK
    row_spec = lambda w: pl.BlockSpec((1, tm, w), lambda i, j: (i, j, 0))
    full = lambda shp: pl.BlockSpec(shp, lambda i, j: (0,) * len(shp))
    vt_spec = lambda w: pl.BlockSpec((1, tm // w, V_TILE_ROWS, w), lambda i, j: (i, j, 0, 0))
    vt_shape = lambda w: jax.ShapeDtypeStruct((b, t // w, V_TILE_ROWS, w), BF16)
    kv = lambda dt: jax.ShapeDtypeStruct((b, t, KV_W), dt)
    out_specs = [pl.BlockSpec((1, tm // Q_BLOCK, N_KV, HEAD_DIM, GQ), lambda i, j: (i, j, 0, 0, 0)),
                 row_spec(KV_W), row_spec(KV_W), row_spec(KV_W), vt_spec(KEY_CHUNK), row_spec(KV_W),
                 vt_spec(V_TILE),
                 pl.BlockSpec((1, tm // Q_BLOCK, SUBLANES, GQ), lambda i, j: (i, j, 0, 0)),
                 row_spec(cw)]
    out_shape = [jax.ShapeDtypeStruct((b, nq, N_KV, HEAD_DIM, GQ), BF16),
                 kv(F32), kv(F32), kv(BF16), vt_shape(KEY_CHUNK), kv(BF16), vt_shape(V_TILE),
                 jax.ShapeDtypeStruct((b, nq, SUBLANES, GQ), F32),
                 jax.ShapeDtypeStruct((b, t, cw), BF16)]
    return pl.pallas_call(
        functools.partial(_inproj_kernel, conv_w=cw),
        grid=(b, t // tm),
        in_specs=[row_spec(d),
                  pl.BlockSpec((1, N_MOD, d), lambda i, j: (i, 0, 0)),
                  full((1, d)), full(w_nat.shape), full(w_tr.shape), full((ATTN_W, 1)),
                  full((1, KV_W)), full((CONV_K, cw)), full((ATTN_W, ATTN_W)), full((KV_W, KV_W))],
        out_specs=out_specs,
        out_shape=out_shape,
        scratch_shapes=[pltpu.VMEM((SUBLANES, cw), F32)],
        compiler_params=_cparams(2),
        name="inproj",
    )(x, mod, norm1, w_nat, w_tr, qn_col, kn_t, conv_w, bdq, bdk)


def _compress_kernel(kx_ref, vx_ref, pek_ref, pev_ref, w1k_ref, w1v_ref, w2k_ref, w2vt_ref,
                     kn_ref, bdk_ref, kc_out, vct_out):
    def hidden(x_ref, pe_ref, w1_ref):
        n = x_ref.shape[1] // CMP_STRIDE
        u = jnp.zeros((n, w1_ref.shape[2]), F32)
        v = jnp.zeros((n, w1_ref.shape[2]), F32)
        for r in range(0, CMP_STRIDE, 2):
            tok = [x_ref[0, pl.ds(r + d, n, stride=CMP_STRIDE), :] for d in range(2)]
            cols = slice(r * KV_W, (r + 2) * KV_W)
            for a, acc in ((0, "u"), (1, "v")):
                lhs = jnp.concatenate([tok[d] + pe_ref[a:a + 1, (r + d) * KV_W:(r + d + 1) * KV_W]
                                       for d in range(2)], axis=1).astype(BF16)
                prod = jnp.dot(lhs, w1_ref[a, cols, :], preferred_element_type=F32)
                if acc == "u":
                    u = u + prod
                else:
                    v = v + prod
        hid = u + pltpu.roll(v, n - 1, 0)
        return jax.nn.gelu(hid, approximate=True).astype(BF16)

    kc = jnp.dot(hidden(kx_ref, pek_ref, w1k_ref), w2k_ref[...], preferred_element_type=F32)
    ssq = jnp.dot((kc * kc).astype(BF16), bdk_ref[...], preferred_element_type=F32)
    kc_out[0] = (kc * lax.rsqrt(ssq * (1.0 / HEAD_DIM) + EPS) * kn_ref[...]).astype(BF16)
    vct_out[0] = _with_ones_rows(lax.dot_general(w2vt_ref[...], hidden(vx_ref, pev_ref, w1v_ref), _NT,
                                                 preferred_element_type=F32).astype(BF16))


def _compress_call(kx, vx, pek, pev, w1k, w1v, w2k, w2vt, kn_t, bdk):
    b, t, _ = kx.shape
    nrow = t // CMP_STRIDE
    wide, hid2 = w1k.shape[1], w1k.shape[2]
    full = lambda shp: pl.BlockSpec(shp, lambda i: (0,) * len(shp))
    xs = pl.BlockSpec((1, t, KV_W), lambda i: (i, 0, 0))
    return pl.pallas_call(
        _compress_kernel,
        grid=(b,),
        in_specs=[xs, xs, full((2, wide)), full((2, wide)), full((2, wide, hid2)),
                  full((2, wide, hid2)), full((hid2, KV_W)), full((KV_W, hid2)),
                  full((1, KV_W)), full((KV_W, KV_W))],
        out_specs=[pl.BlockSpec((1, nrow, KV_W), lambda i: (i, 0, 0)),
                   pl.BlockSpec((1, V_TILE_ROWS, nrow), lambda i: (i, 0, 0))],
        out_shape=[jax.ShapeDtypeStruct((b, nrow, KV_W), BF16),
                   jax.ShapeDtypeStruct((b, V_TILE_ROWS, nrow), BF16)],
        compiler_params=_cparams(1),
        name="compress",
    )(kx, vx, pek, pev, w1k, w1v, w2k, w2vt, kn_t, bdk)


def _attn_kernel(q_ref, kc_ref, vct_ref, ks_ref, vst_ref, kw_ref, vwt_ref, gt_ref, ovt_ref,
                 eb_ref, sb_ref, wb_ref, o_ref, sel_scr, acc_scr, out_scr, qp_scr, s_scr,
                 fin_scr, *, n_cmp_rows, eb_shift):
    n_sb = sel_scr.shape[1]
    ci = jnp.minimum(pl.program_id(1), n_sb - 1)
    par = lax.rem(ci, 2)
    n_total = ks_ref.shape[1] // KEY_CHUNK
    vsl = lambda g: slice(g * V_ROWS, (g + 1) * V_ROWS)

    @pl.when((pl.program_id(0) == 0) & (pl.program_id(1) == 0))
    def _():
        def zero(g, carry):
            fin_scr[g] = jnp.zeros(fin_scr.shape[1:], F32)
            sel_scr[g] = jnp.zeros(sel_scr.shape[1:], F32)
            return carry
        lax.fori_loop(0, N_KV, zero, 0)

    def chunk_qk(g, c):
        kk = ks_ref[0, pl.ds(pl.multiple_of(c * KEY_CHUNK, KEY_CHUNK), KEY_CHUNK), :]
        return jnp.dot(kk, qp_scr[g], preferred_element_type=F32).astype(BF16)

    n_chunks = ci // BLOCKS_PER_CHUNK + 1
    smallest = STEP_CHUNKS[-1]
    per_biggest = STEP_CHUNKS[0] // smallest
    n_smallest = (n_chunks + smallest - 1) // smallest
    n_smallest = jnp.where(lax.rem(n_smallest, per_biggest) == per_biggest - 1,
                           n_smallest + 1, n_smallest)
    step_counts, step_starts = [], []
    first_chunk = 0
    for k, size in enumerate(STEP_CHUNKS):
        count = n_smallest // (size // smallest)
        step_counts.append(count if k == 0 else lax.rem(count, 2))
        step_starts.append(first_chunk)
        first_chunk = first_chunk + step_counts[k] * size

    def step_units(k, i):
        return [(g, jnp.minimum(step_starts[k] + STEP_CHUNKS[k] * i + j, n_total - 1))
                for g in range(N_KV) for j in range(STEP_CHUNKS[k])]

    def first_step_from(k):
        last = len(STEP_CHUNKS) - 1
        return ([(step_counts[j] > 0, step_units(j, 0)) for j in range(k, last)]
                + [(None, step_units(last, 0))])

    def pick_units(options):
        units = options[-1][1][:QK_AHEAD]
        for cond, cand in reversed(options[:-1]):
            units = [(jnp.where(cond, g_a, g_b), jnp.where(cond, c_a, c_b))
                     for (g_a, c_a), (g_b, c_b) in zip(cand[:QK_AHEAD], units)]
        return [(g, jnp.minimum(c, n_total - 1)) for g, c in units]

    def prefetch_scores(units):
        for k in range(QK_AHEAD):
            s_scr[k] = chunk_qk(*units[k])

    def before_loops(rows):
        pieces = []
        for g in range(N_KV):
            for half in range(GQ // LANES):
                a = fin_scr[g, :, half * LANES:(half + 1) * LANES]
                stacked = jnp.concatenate([a, pltpu.roll(a, Q_BLOCK, 1)], axis=0)
                pieces.append(stacked.T[:Q_BLOCK, :])
        o_ref[0] = jnp.concatenate(pieces, axis=1).astype(BF16)

        zeros_q = jnp.zeros((HEAD_DIM, GQ), BF16)
        q_padded = [jnp.concatenate([q_ref[0, 0, 0], zeros_q], axis=0),
                    jnp.concatenate([zeros_q, q_ref[0, 0, 1]], axis=0)]
        for g in range(N_KV):
            qp_scr[g] = q_padded[g]

        nsel = min(n_sb, rows * CMP_STRIDE // SEL_BLOCK)
        lane = lax.broadcasted_iota(jnp.int32, (nsel, LANES), 1)
        cmp_c = n_cmp_rows - 4
        e0 = cmp_c - 4 * ci + jnp.where(par == 0, eb_shift[0], eb_shift[1])
        e0 = pl.multiple_of(e0, 8)
        o_cmp = []
        imp = []
        w0 = ci // 2 - (WIN_TILES - 1)
        win_tiles = [jnp.maximum(w0 + j, 0) for j in range(WIN_TILES)]
        s_cmp = jnp.dot(kc_ref[0, :rows, :], jnp.concatenate(q_padded, axis=1),
                        preferred_element_type=F32)
        for g in range(N_KV):
            sc = s_cmp[:, g * GQ:(g + 1) * GQ] + eb_ref[par, g, pl.ds(e0, rows), :]
            m = jnp.max(sc, axis=0, keepdims=True)
            e = jnp.exp2(sc - m).astype(BF16)
            ov = jnp.dot(vct_ref[0, vsl(g), :rows], e, preferred_element_type=F32)
            inv = jnp.where(m > 0.5 * NEG, 1.0 / ov[HEAD_DIM:HEAD_DIM + 1, :], 0.0)
            o_cmp.append(ov[:HEAD_DIM, :] * inv)
            ir = jnp.dot(ovt_ref[:nsel, :rows], e, preferred_element_type=F32) * inv
            a = ir[:, :LANES] + ir[:, LANES:]
            imp.append(a + pltpu.roll(a, Q_BLOCK, 1))

        s_win = [[jnp.dot(kw_ref[0, pl.ds(pl.multiple_of(tj * V_TILE, V_TILE), V_TILE), :],
                          qp_scr[g], preferred_element_type=F32).astype(BF16) for tj in win_tiles]
                 for g in range(N_KV)]

        prefetch_scores(pick_units(first_step_from(0)))

        jidx = lax.broadcasted_iota(jnp.int32, (nsel, LANES), 0)
        jf = jidx.astype(F32)
        valid = jidx <= ci
        forced = (jidx == 0) | (jidx == ci) | (jidx == ci - 1)
        score = jnp.where(valid, jnp.where(forced, -2.0, jnp.where(lane < Q_BLOCK, imp[0], imp[1])),
                          -1.0)

        def first_max(score):
            pairs = [(score[r:r + SUBLANES], jf[r:r + SUBLANES]) for r in range(0, nsel, SUBLANES)]
            while len(pairs) > 1:
                nxt = []
                for (va, ia), (vb, ib) in zip(pairs[0::2], pairs[1::2]):
                    keep = va >= vb
                    nxt.append((jnp.where(keep, va, vb), jnp.where(keep, ia, ib)))
                pairs = nxt + pairs[len(pairs) - len(pairs) % 2:]
            v8, i8 = pairs[0]
            mx = jnp.max(v8, axis=0, keepdims=True)
            return jnp.min(jnp.where(v8 == mx, i8, float(nsel)), axis=0, keepdims=True)

        for _ in range(min(N_SELECT, nsel) - 3):
            first = first_max(score)
            score = jnp.where(jf == first, -2.0, score)
        selneg = jnp.where((score < -1.5) & valid, 0.0, NEG)
        swapped = pltpu.roll(selneg, Q_BLOCK, 1)
        left = jnp.where(lane < Q_BLOCK, selneg, swapped)
        right = jnp.where(lane < Q_BLOCK, swapped, selneg)
        sel_scr[0, :nsel] = jnp.concatenate([left, left], axis=1)
        sel_scr[1, :nsel] = jnp.concatenate([right, right], axis=1)

        for g in range(N_KV):
            parts = []
            for i in range(2 * WIN_TILES):
                delta = par + WIN_BLOCKS - i
                ok = (delta >= 0) & (delta <= WIN_BLOCKS) & (delta <= ci)
                tile = jnp.where(ok, delta, WIN_BLOCKS + 1)
                half = s_win[g][i // 2][(i % 2) * SEL_BLOCK:(i % 2 + 1) * SEL_BLOCK, :]
                parts.append(half + wb_ref[g, tile])
            s = jnp.concatenate(parts, axis=0)
            m = jnp.max(s, axis=0, keepdims=True)
            pb = jnp.exp2(s - m)
            o_win = jnp.zeros((V_ROWS, GQ), F32)
            for j, tj in enumerate(win_tiles):
                o_win = o_win + jnp.dot(vwt_ref[0, tj, vsl(g), :], pb[j * V_TILE:(j + 1) * V_TILE, :],
                                        preferred_element_type=F32)
            w_scale = (gt_ref[0, 0, 2 * N_KV + g:2 * N_KV + g + 1, :]
                       * (1.0 / o_win[HEAD_DIM:HEAD_DIM + 1, :]))
            out_scr[g] = gt_ref[0, 0, g:g + 1, :] * o_cmp[g] + w_scale * o_win[:HEAD_DIM, :]
            acc_scr[g] = jnp.zeros((V_ROWS, GQ), F32)

    row_steps = [r for r in range(CMP_ROW_STEP, n_cmp_rows + 1, CMP_ROW_STEP)] or [n_cmp_rows]
    variant = jnp.minimum((4 * ci + 2) // CMP_ROW_STEP, len(row_steps) - 1)
    lax.switch(variant, [functools.partial(before_loops, r) for r in row_steps])

    def chunk_softmax(s, g, c, extra):
        parts = []
        for i in range(BLOCKS_PER_CHUNK):
            kb = c * BLOCKS_PER_CHUNK + i
            blk = s[i * SEL_BLOCK:(i + 1) * SEL_BLOCK, :]
            mrow = sel_scr[g, pl.ds(kb, 1), :]
            if extra is not None:
                mrow = mrow + extra
            tile = jnp.clip(ci - kb, 0, NEAR_TILES)
            parts.append(blk + sb_ref[g, tile] + mrow.astype(BF16))
        s = jnp.concatenate(parts, axis=0)
        m_c = jnp.max(s, axis=0, keepdims=True)
        return m_c.astype(F32), jnp.exp2(s - m_c)

    def chunk_pv(pb, g, c):
        return jnp.dot(vst_ref[0, c, vsl(g), :], pb, preferred_element_type=F32)

    def merge_step(units, extras, next_units, carry):
        scores = {k: s_scr[k] for k in range(QK_AHEAD)}
        pending = list(range(QK_AHEAD, len(units) + QK_AHEAD))

        def issue_scores():
            k = pending.pop(0)
            if k < len(units):
                scores[k] = chunk_qk(*units[k])
            else:
                s_scr[k - len(units)] = chunk_qk(*next_units[k - len(units)])

        for _ in range(min(QK_DEPTH - QK_AHEAD, len(pending))):
            issue_scores()
        results = []
        for k, (g, c) in enumerate(units):
            m_c, pb = chunk_softmax(scores.pop(k), g, c, extras[k])
            if pending:
                issue_scores()
            results.append((m_c, chunk_pv(pb, g, c)))
        per_g = len(units) // N_KV
        new = []
        for g in range(N_KV):
            m_run = carry[g]
            acc = acc_scr[g]
            for first in range(g * per_g, (g + 1) * per_g, MERGE_CHUNKS):
                stats = results[first:min(first + MERGE_CHUNKS, (g + 1) * per_g)]
                m_new = m_run
                for m_c, _ in stats:
                    m_new = jnp.maximum(m_new, m_c)
                acc = jnp.exp2(m_run - m_new) * acc
                for m_c, pv in stats:
                    acc = acc + jnp.exp2(m_c - m_new) * pv
                m_run = m_new
            acc_scr[g] = acc
            new.append(m_run)
        return tuple(new)

    def loop_step(k):
        def step(i, carry):
            first = step_starts[k] + STEP_CHUNKS[k] * i
            extras = [None if j == 0 else jnp.where(first + j < n_chunks, 0.0, NEG)
                      for _ in range(N_KV) for j in range(STEP_CHUNKS[k])]
            again = [(i + 1 < step_counts[k], step_units(k, i + 1))] if k == 0 else []
            nxt = pick_units(again + first_step_from(min(k + 1, len(STEP_CHUNKS) - 1)))
            return merge_step(step_units(k, i), extras, nxt, carry)
        return step

    carry = (jnp.full((1, GQ), NEG, F32),) * N_KV
    for k in range(len(STEP_CHUNKS)):
        carry = lax.fori_loop(0, step_counts[k], loop_step(k), carry)

    for g in range(N_KV):
        acc = acc_scr[g]
        scale = gt_ref[0, 0, N_KV + g:N_KV + g + 1, :] * (1.0 / acc[HEAD_DIM:HEAD_DIM + 1, :])
        fin_scr[g] = out_scr[g] + scale * acc[:HEAD_DIM, :]


def _attn_call(qt, kc, vct, ks, vst, kw, vwt, gt, ovt, ebank, sbank, wbank, eb_shift):
    b, nq = qt.shape[0], qt.shape[1]
    t = ks.shape[1]
    n_cmp_rows = kc.shape[1]
    n_sb = t // SEL_BLOCK
    per_b = lambda shp: pl.BlockSpec((1,) + shp[1:], lambda i, j: (i,) + (0,) * (len(shp) - 1))
    per_q = lambda shp: pl.BlockSpec(
        (1, 1) + shp[2:], lambda i, j: (i, jnp.minimum(j, nq - 1)) + (0,) * (len(shp) - 2))
    full = lambda shp: pl.BlockSpec(shp, lambda i, j: (0,) * len(shp))
    args = (qt, kc, vct, ks, vst, kw, vwt, gt, ovt, ebank, sbank, wbank)
    specs = [per_q(qt.shape), per_b(kc.shape), per_b(vct.shape), per_b(ks.shape), per_b(vst.shape),
             per_b(kw.shape), per_b(vwt.shape), per_q(gt.shape), full(ovt.shape),
             full(ebank.shape), full(sbank.shape), full(wbank.shape)]
    acc_like = pltpu.VMEM((N_KV, HEAD_DIM, GQ), F32)
    return pl.pallas_call(
        functools.partial(_attn_kernel, n_cmp_rows=n_cmp_rows, eb_shift=eb_shift),
        grid=(b, nq + 1),
        in_specs=specs,
        out_specs=pl.BlockSpec((1, Q_BLOCK, ATTN_W), lambda i, j: (i, jnp.maximum(j - 1, 0), 0)),
        out_shape=jax.ShapeDtypeStruct((b, t, ATTN_W), BF16),
        scratch_shapes=[pltpu.VMEM((N_KV, n_sb, GQ), F32),
                        pltpu.VMEM((N_KV, V_ROWS, GQ), F32),
                        acc_like,
                        pltpu.VMEM((N_KV, KV_W, GQ), BF16),
                        pltpu.VMEM((QK_AHEAD, KEY_CHUNK, GQ), BF16),
                        acc_like],
        compiler_params=_cparams(2),
        name="nsa_attention",
    )(*args)


def _ffn_kernel(x_ref, a_ref, c_ref, mod_ref, n2_ref, wo_ref, w1_ref, w2_ref, o_ref):
    aw = a_ref.shape[2]
    mix = jnp.dot(a_ref[0], wo_ref[0:aw, :], preferred_element_type=F32)
    mix = mix + jnp.dot(c_ref[0], wo_ref[aw:, :], preferred_element_type=F32)
    x1 = x_ref[0] + mod_ref[0, 2:3, :] * mix
    ms = jnp.mean(x1 * x1, axis=-1, keepdims=True)
    y = x1 * lax.rsqrt(ms + EPS) * n2_ref[...]
    h2 = (y * (1.0 + mod_ref[0, 4:5, :]) + mod_ref[0, 3:4, :]).astype(BF16)
    d_ff = w1_ref.shape[1]
    ff = jnp.zeros(x1.shape, F32)
    for j in range(d_ff // FF_CHUNK):
        a = jnp.dot(h2, w1_ref[:, j * FF_CHUNK:(j + 1) * FF_CHUNK], preferred_element_type=F32)
        a = jnp.maximum(a, 0.0)
        ff = ff + jnp.dot((a * a).astype(BF16), w2_ref[j * FF_CHUNK:(j + 1) * FF_CHUNK, :],
                          preferred_element_type=F32)
    o_ref[0] = x1 + mod_ref[0, 5:6, :] * ff


def _ffn_call(x, attn, conv, mod, norm2, w_out, w_ff1, w_ff2):
    b, t, d = x.shape
    tm = min(ROW_TILE, t)
    row_spec = lambda w: pl.BlockSpec((1, tm, w), lambda i, j: (i, j, 0))
    full = lambda shp: pl.BlockSpec(shp, lambda i, j: (0,) * len(shp),
                                    pipeline_mode=pl.Buffered(1))
    return pl.pallas_call(
        _ffn_kernel,
        grid=(b, t // tm),
        in_specs=[row_spec(d), row_spec(attn.shape[2]), row_spec(conv.shape[2]),
                  pl.BlockSpec((1, N_MOD, d), lambda i, j: (i, 0, 0)),
                  full((1, d)), full(w_out.shape), full(w_ff1.shape), full(w_ff2.shape)],
        out_specs=row_spec(d),
        out_shape=jax.ShapeDtypeStruct((b, t, d), F32),
        compiler_params=_cparams(2),
        name="outproj_mlp",
    )(x, attn, conv, mod, norm2, w_out, w_ff1, w_ff2)


def _block_diag_ones(n):
    idx = np.arange(n) // HEAD_DIM
    return jnp.asarray(idx[:, None] == idx[None, :], dtype=BF16)


def _pack_w_in(w_in):
    d = w_in.shape[0]
    conv_w = d - ATTN_W
    sizes = [ATTN_W] + [KV_W] * 6 + [N_BRANCH * N_HEADS] + [conv_w] * 3
    offs = np.concatenate([[0], np.cumsum(sizes)])
    part = lambda i: w_in[:, offs[i]:offs[i + 1]]
    q, kc, vc, ks, vs, kw, vw, g, cgate, bgate, u = (part(i) for i in range(11))
    w_nat = jnp.concatenate([kc, vc, ks, kw, cgate, bgate, u], axis=1).astype(BF16)
    gt = g.reshape(d, N_KV, GQA, N_BRANCH).transpose(2, 3, 1, 0).reshape(GQA, N_BRANCH * N_KV, d)
    gt = jnp.pad(gt, ((0, 0), (0, SUBLANES - N_BRANCH * N_KV), (0, 0))).reshape(_G_ROWS, d)
    w_tr = jnp.concatenate([q.T, vs.T, vw.T, gt], axis=0).astype(BF16)
    return w_nat, w_tr


def _expand_w1(w1):
    hid = w1.shape[1]
    w = w1.reshape(2, CMP_STRIDE, HEAD_DIM, hid).astype(BF16)
    zero = jnp.zeros_like(w)
    per_group = [jnp.concatenate([w if k == g else zero for k in range(N_KV)], axis=-1)
                 for g in range(N_KV)]
    return jnp.stack(per_group, axis=2).reshape(2, CMP_STRIDE * KV_W, N_KV * hid)


def _expand_w2(w2):
    hid = w2.shape[0]
    eye = jnp.eye(N_KV, dtype=w2.dtype).reshape(N_KV, 1, N_KV, 1)
    return (w2.reshape(1, hid, 1, HEAD_DIM) * eye).reshape(N_KV * hid, KV_W).astype(BF16)


def _expand_pe(pe):
    p = pe.reshape(2, CMP_STRIDE, 1, HEAD_DIM)
    return jnp.broadcast_to(p, (2, CMP_STRIDE, N_KV, HEAD_DIM)).reshape(2, CMP_STRIDE * KV_W)


def _bucket_thresholds():
    n = np.arange(2 * REL_MAX_DIST)
    max_exact = REL_BUCKETS // 2
    nf = np.maximum(n, max_exact).astype(np.float32)
    ratio = np.log(nf / np.float32(max_exact)) / np.float32(math.log(REL_MAX_DIST / max_exact))
    large = max_exact + (ratio * np.float32(REL_BUCKETS - max_exact)).astype(np.int32)
    table = np.where(n < max_exact, n, np.minimum(large, REL_BUCKETS - 1))
    return tuple(int(np.searchsorted(table, k, side="left")) for k in range(REL_BUCKETS))


def _bank_call(bias_rows, lead, n_tiles, dist_fn, name):
    nl = len(lead)
    thr = _bucket_thresholds()

    def body(rows_ref, o_ref):
        lead_ids = [pl.program_id(a) for a in range(nl)]
        row = lax.broadcasted_iota(jnp.int32, (SEL_BLOCK, GQ), 0)
        qi = lax.broadcasted_iota(jnp.int32, (SEL_BLOCK, GQ), 1) & (Q_BLOCK - 1)

        def tile(t, carry):
            dist, ok = dist_fn(lead_ids, t, row, qi)
            v = jnp.broadcast_to(rows_ref[0, 0:1, :], (SEL_BLOCK, GQ))
            for k in range(1, REL_BUCKETS):
                v = jnp.where(dist >= thr[k], rows_ref[0, k:k + 1, :], v)
            o_ref[(0,) * (nl + 1) + (t,)] = jnp.where(ok, v, NEG)
            return carry

        lax.fori_loop(0, n_tiles, tile, 0)

    return pl.pallas_call(
        body,
        grid=tuple(lead) + (N_KV,),
        in_specs=[pl.BlockSpec((1, REL_BUCKETS, GQ), lambda *i: (i[nl], 0, 0))],
        out_specs=pl.BlockSpec((1,) * (nl + 1) + (n_tiles, SEL_BLOCK, GQ),
                               lambda *i: tuple(i) + (0, 0, 0)),
        out_shape=jax.ShapeDtypeStruct(tuple(lead) + (N_KV, n_tiles, SEL_BLOCK, GQ), F32),
        compiler_params=_cparams(nl + 1),
        name=name,
    )(bias_rows)


def _bias_banks(rel_bias, t):
    n_cmp_rows = t // CMP_STRIDE
    rows = rel_bias.reshape(REL_BUCKETS, N_KV, GQA).transpose(1, 0, 2)
    rows = jnp.repeat(rows, Q_BLOCK, axis=2) * LOG2E

    def sel_dist(lead, tile, row, qi):
        dist = SEL_BLOCK * tile + qi - row
        return dist, dist >= 0

    sbank = _bank_call(rows, (), NEAR_TILES + 1, sel_dist, "bias_bank_sel")

    def win_dist(lead, tile, row, qi):
        dist = SEL_BLOCK * tile + qi - row
        return dist, (dist >= 0) & (dist < WINDOW)

    wbank = _bank_call(rows, (), WIN_BLOCKS + 2, win_dist, "bias_bank_win")

    cmp_c = n_cmp_rows - 4
    shifts = tuple(int((-(cmp_c - 4 * p)) % 8) for p in range(2))
    n_tiles = (cmp_c + n_cmp_rows + 8 + SEL_BLOCK - 1) // SEL_BLOCK

    def cmp_dist(lead, tile, row, qi):
        e = SEL_BLOCK * tile + row - jnp.where(lead[0] == 0, shifts[0], shifts[1])
        dist = qi - CMP_STRIDE * e + (CMP_STRIDE * cmp_c - (CMP_BLOCK - 1))
        return dist, (dist >= 0) & (e >= 0)

    ebank = _bank_call(rows, (2,), n_tiles, cmp_dist, "bias_bank_cmp")
    ebank = ebank.reshape(2, N_KV, n_tiles * SEL_BLOCK, GQ)
    return ebank, sbank, wbank, shifts


def _overlap_t(t):
    n_cmp_rows = t // CMP_STRIDE
    n_sb = t // SEL_BLOCK
    c_start = np.arange(n_cmp_rows)[None, :] * CMP_STRIDE
    s_start = np.arange(n_sb)[:, None] * SEL_BLOCK
    ov = np.clip(np.minimum(c_start + CMP_BLOCK, s_start + SEL_BLOCK)
                 - np.maximum(c_start, s_start), 0, None) / CMP_BLOCK
    ov[:, n_cmp_rows - 1] = 0.0
    return jnp.asarray(ov, dtype=BF16)


def _layer(x, c_pad, w_in, q_norm, k_norm, cmp_pe_k, cmp_w1_k, cmp_w2_k, cmp_pe_v, cmp_w1_v,
           cmp_w2_v, rel_bias, conv_w, w_out, norm1, norm2, w_ada, b_ada, w_ff1, w_ff2):
    b, t, d = x.shape
    scale = HEAD_DIM ** -0.5

    mod = _mod_call(c_pad, w_ada, b_ada)[:b].reshape(b, N_MOD, d)

    qn_col = (jnp.tile(q_norm, N_HEADS) * (scale * LOG2E)).reshape(ATTN_W, 1)
    kn_t = jnp.tile(k_norm, N_KV).reshape(1, KV_W)
    bdq = _block_diag_ones(ATTN_W)
    bdk = _block_diag_ones(KV_W)
    w_nat, w_tr = _pack_w_in(w_in)
    qt, kc_raw, vc_raw, ks, vst, kw, vwt, gt, conv = _inproj_call(
        x, mod, norm1.reshape(1, d), w_nat, w_tr, qn_col, kn_t, conv_w, bdq, bdk)

    kc, vct = _compress_call(
        kc_raw, vc_raw, _expand_pe(cmp_pe_k), _expand_pe(cmp_pe_v), _expand_w1(cmp_w1_k), _expand_w1(cmp_w1_v),
        _expand_w2(cmp_w2_k), _expand_w2(cmp_w2_v).T, kn_t, bdk)

    ebank, sbank, wbank, eb_shift = _bias_banks(rel_bias, t)
    attn = _attn_call(qt, kc, vct, ks, vst, kw, vwt, gt, _overlap_t(t), ebank, sbank.astype(BF16),
                      wbank.astype(BF16), eb_shift)

    return _ffn_call(x, attn, conv, mod, norm2.reshape(1, d), w_out.astype(BF16),
                     w_ff1.astype(BF16), w_ff2.astype(BF16))


def kernel(x, c, w_in, q_norm, k_norm, cmp_pe_k, cmp_w1_k, cmp_w2_k, cmp_pe_v, cmp_w1_v, cmp_w2_v,
           rel_bias, conv_w, w_out, norm1, norm2, w_ada, b_ada, w_ff1, w_ff2):
    b = x.shape[0]
    c_pad = jnp.pad(c, ((0, (-b) % 8), (0, 0)))
    for l in range(w_in.shape[0]):
        x = _layer(x, c_pad, w_in[l], q_norm[l], k_norm[l], cmp_pe_k[l], cmp_w1_k[l], cmp_w2_k[l],
                   cmp_pe_v[l], cmp_w1_v[l], cmp_w2_v[l], rel_bias, conv_w[l], w_out[l],
                   norm1[l], norm2[l], w_ada[l], b_ada[l], w_ff1[l], w_ff2[l])
    return x
```

```python
import functools
import math

import numpy as np
import jax
import jax.numpy as jnp
from jax import lax
from jax.experimental import pallas as pl
from jax.experimental.pallas import tpu as pltpu

HEAD_DIM = 64
N_HEADS = 8
N_KV = 2
GQA = N_HEADS // N_KV
ATTN_W = N_HEADS * HEAD_DIM
KV_W = N_KV * HEAD_DIM
CONV_K = 3
CMP_BLOCK = 32
CMP_STRIDE = 16
SEL_BLOCK = 64
N_SELECT = 16
WINDOW = 512
Q_BLOCK = 64
REL_BUCKETS = 32
REL_MAX_DIST = 1024
N_MOD = 6
N_BRANCH = 3
EPS = 1e-6
NEG = -1e30

LANES = 128
SUBLANES = 8
GQ = GQA * Q_BLOCK
KEY_CHUNK = 256
BLOCKS_PER_CHUNK = KEY_CHUNK // SEL_BLOCK
V_TILE = 128
BF16_ROWS = 16
V_ROWS = HEAD_DIM + BF16_ROWS
V_TILE_ROWS = N_KV * V_ROWS
CMP_ROW_STEP = 128
WIN_BLOCKS = WINDOW // SEL_BLOCK
WIN_TILES = WINDOW // V_TILE + 1
NEAR_TILES = (REL_MAX_DIST + Q_BLOCK - 1) // SEL_BLOCK + 1
ROW_TILE = 512
INPROJ_TILE = 1024
INPROJ_SPLIT = 4
BIG_CHUNKS = 16
TAIL_CHUNKS = 1
assert BIG_CHUNKS % TAIL_CHUNKS == 0
MERGE_CHUNKS = 4
QK_AHEAD = 4
QK_DEPTH = 5
LOG2E = math.log2(math.e)
FF_CHUNK = 1024
VMEM_LIMIT = 56 * 1024 * 1024

F32 = jnp.float32
BF16 = jnp.bfloat16
_NT = (((1,), (1,)), ((), ()))


def _cparams(n_axes):
    return pltpu.CompilerParams(dimension_semantics=("arbitrary",) * n_axes,
                                vmem_limit_bytes=VMEM_LIMIT)


def _with_ones_rows(vt):
    ones = jnp.ones((BF16_ROWS, vt.shape[1]), vt.dtype)
    parts = []
    for g in range(N_KV):
        parts += [vt[g * HEAD_DIM:(g + 1) * HEAD_DIM, :], ones]
    return jnp.concatenate(parts, axis=0)


def _swap_halves(p0, p1):
    low = lax.broadcasted_iota(jnp.int32, p0.shape, 1) < LANES // 2
    return (jnp.where(low, p0, pltpu.roll(p1, LANES // 2, 1)),
            jnp.where(low, pltpu.roll(p0, LANES // 2, 1), p1))


def _mod_kernel(c_ref, w_ref, b_ref, o_ref):
    c = c_ref[...]
    a = c * jax.nn.sigmoid(c)
    o_ref[...] = jnp.dot(a, w_ref[...], preferred_element_type=F32,
                         precision=lax.Precision.HIGHEST) + b_ref[...]


def _mod_call(c_pad, w_ada, b_ada):
    rows, d = c_pad.shape
    n = w_ada.shape[1]
    tn = n // N_MOD
    return pl.pallas_call(
        _mod_kernel,
        grid=(n // tn,),
        in_specs=[pl.BlockSpec((rows, d), lambda j: (0, 0)),
                  pl.BlockSpec((d, tn), lambda j: (0, j)),
                  pl.BlockSpec((1, tn), lambda j: (0, j))],
        out_specs=pl.BlockSpec((rows, tn), lambda j: (0, j)),
        out_shape=jax.ShapeDtypeStruct((rows, n), F32),
        compiler_params=_cparams(1),
        name="adaln_mod",
    )(c_pad, w_ada, b_ada.reshape(1, n))


_N_KC, _N_VC, _N_KS, _N_KW, _N_CONV = 0, KV_W, 2 * KV_W, 3 * KV_W, 4 * KV_W
_T_Q, _T_VS, _T_VW, _T_G = 0, ATTN_W, ATTN_W + KV_W, ATTN_W + 2 * KV_W
_G_ROWS = GQA * SUBLANES


def _inproj_kernel(x_ref, mod_ref, n1_ref, wn_ref, wt_ref, qn_ref, kn_ref, cw_ref, bdq_ref, bdk_ref,
                   qt_out, kc_out, vc_out, ks_out, vst_out, kw_out, vwt_out, gt_out, conv_out,
                   carry, *, conv_w):
    t = pl.program_id(1)
    tm = x_ref.shape[1]
    sub = tm // INPROJ_SPLIT

    @pl.when(t == 0)
    def _():
        carry[...] = jnp.zeros_like(carry)

    prev2, prev1 = carry[6:7, :], carry[7:8, :]
    c_bg = _N_CONV + conv_w
    c_u = c_bg + conv_w

    for part in range(INPROJ_SPLIT):
        rows = slice(part * sub, (part + 1) * sub)
        x = x_ref[0, rows, :]
        ms = jnp.mean(x * x, axis=-1, keepdims=True)
        y = x * lax.rsqrt(ms + EPS) * n1_ref[...]
        h = (y * (1.0 + mod_ref[0, 1:2, :]) + mod_ref[0, 0:1, :]).astype(BF16)

        def proj(a, b):
            return jnp.dot(h, wn_ref[:, a:b], preferred_element_type=F32)

        def proj_t(a, b):
            return lax.dot_general(wt_ref[a:b, :], h, _NT, preferred_element_type=F32)

        def head_norm(v, gain):
            ssq = jnp.dot((v * v).astype(BF16), bdk_ref[...], preferred_element_type=F32)
            return v * lax.rsqrt(ssq * (1.0 / HEAD_DIM) + EPS) * gain

        qf = proj_t(_T_Q, _T_VS)
        vg = proj_t(_T_VS, _T_G + _G_ROWS)
        kv = proj(_N_KC, _N_CONV)
        cv = proj(_N_CONV, c_u + conv_w)

        ssq = jnp.dot(bdq_ref[...], (qf * qf).astype(BF16), preferred_element_type=F32)
        qf = qf * lax.rsqrt(ssq * (1.0 / HEAD_DIM) + EPS) * qn_ref[...]
        gf = jax.nn.sigmoid(vg[2 * KV_W:, :])
        for c in range(sub // LANES):
            blk = (part * sub) // Q_BLOCK + 2 * c
            cols = slice(c * LANES, (c + 1) * LANES)
            for g in range(N_KV):
                pc = [qf[(g * GQA + r) * HEAD_DIM:(g * GQA + r + 1) * HEAD_DIM, cols]
                      for r in range(GQA)]
                lo01, hi01 = _swap_halves(pc[0], pc[1])
                lo23, hi23 = _swap_halves(pc[2], pc[3])
                qt_out[0, blk, g] = jnp.concatenate([lo01, lo23], axis=1).astype(BF16)
                qt_out[0, blk + 1, g] = jnp.concatenate([hi01, hi23], axis=1).astype(BF16)
            pc = [gf[r * SUBLANES:(r + 1) * SUBLANES, cols] for r in range(GQA)]
            lo01, hi01 = _swap_halves(pc[0], pc[1])
            lo23, hi23 = _swap_halves(pc[2], pc[3])
            gt_out[0, blk] = jnp.concatenate([lo01, lo23], axis=1)
            gt_out[0, blk + 1] = jnp.concatenate([hi01, hi23], axis=1)

        vs_f = _with_ones_rows(vg[:KV_W, :].astype(BF16))
        vw_f = _with_ones_rows(vg[KV_W:2 * KV_W, :].astype(BF16))
        for j in range(sub // KEY_CHUNK):
            vst_out[0, (part * sub) // KEY_CHUNK + j] = vs_f[:, j * KEY_CHUNK:(j + 1) * KEY_CHUNK]
        for j in range(sub // V_TILE):
            vwt_out[0, (part * sub) // V_TILE + j] = vw_f[:, j * V_TILE:(j + 1) * V_TILE]

        kc_out[0, rows, :] = kv[:, _N_KC:_N_VC]
        vc_out[0, rows, :] = kv[:, _N_VC:_N_KS]
        ks_out[0, rows, :] = head_norm(kv[:, _N_KS:_N_KW], kn_ref[...]).astype(BF16)
        kw_out[0, rows, :] = head_norm(kv[:, _N_KW:_N_CONV], kn_ref[...]).astype(BF16)

        z = cv[:, :conv_w] * cv[:, 2 * conv_w:]
        row = lax.broadcasted_iota(jnp.int32, z.shape, 0)
        z1 = jnp.where(row == 0, prev1, pltpu.roll(z, 1, 0))
        z2 = jnp.where(row == 0, prev2, jnp.where(row == 1, prev1, pltpu.roll(z, 2, 0)))
        zc = cw_ref[0:1, :] * z2 + cw_ref[1:2, :] * z1 + cw_ref[2:3, :] * z
        conv_out[0, rows, :] = (cv[:, conv_w:2 * conv_w] * zc).astype(BF16)
        prev2, prev1 = z[sub - 2:sub - 1, :], z[sub - 1:sub, :]
        if part == INPROJ_SPLIT - 1:
            carry[...] = z[sub - SUBLANES:sub, :]


def _inproj_call(x, mod, norm1, w_nat, w_tr, qn_col, kn_t, conv_w, bdq, bdk):
    b, t, d = x.shape
    tm = min(INPROJ_TILE, t)
    cw = conv_w.shape[1]
    nq = t // Q_BLOCK
    row_spec = lambda w: pl.BlockSpec((1, tm, w), lambda i, j: (i, j, 0))
    full = lambda shp: pl.BlockSpec(shp, lambda i, j: (0,) * len(shp))
    vt_spec = lambda w: pl.BlockSpec((1, tm // w, V_TILE_ROWS, w), lambda i, j: (i, j, 0, 0))
    vt_shape = lambda w: jax.ShapeDtypeStruct((b, t // w, V_TILE_ROWS, w), BF16)
    kv = lambda dt: jax.ShapeDtypeStruct((b, t, KV_W), dt)
    out_specs = [pl.BlockSpec((1, tm // Q_BLOCK, N_KV, HEAD_DIM, GQ), lambda i, j: (i, j, 0, 0, 0)),
                 row_spec(KV_W), row_spec(KV_W), row_spec(KV_W), vt_spec(KEY_CHUNK), row_spec(KV_W),
                 vt_spec(V_TILE),
                 pl.BlockSpec((1, tm // Q_BLOCK, SUBLANES, GQ), lambda i, j: (i, j, 0, 0)),
                 row_spec(cw)]
    out_shape = [jax.ShapeDtypeStruct((b, nq, N_KV, HEAD_DIM, GQ), BF16),
                 kv(F32), kv(F32), kv(BF16), vt_shape(KEY_CHUNK), kv(BF16), vt_shape(V_TILE),
                 jax.ShapeDtypeStruct((b, nq, SUBLANES, GQ), F32),
                 jax.ShapeDtypeStruct((b, t, cw), BF16)]
    return pl.pallas_call(
        functools.partial(_inproj_kernel, conv_w=cw),
        grid=(b, t // tm),
        in_specs=[row_spec(d),
                  pl.BlockSpec((1, N_MOD, d), lambda i, j: (i, 0, 0)),
                  full((1, d)), full(w_nat.shape), full(w_tr.shape), full((ATTN_W, 1)),
                  full((1, KV_W)), full((CONV_K, cw)), full((ATTN_W, ATTN_W)), full((KV_W, KV_W))],
        out_specs=out_specs,
        out_shape=out_shape,
        scratch_shapes=[pltpu.VMEM((SUBLANES, cw), F32)],
        compiler_params=_cparams(2),
        name="inproj",
    )(x, mod, norm1, w_nat, w_tr, qn_col, kn_t, conv_w, bdq, bdk)


def _compress_kernel(kx_ref, vx_ref, pek_ref, pev_ref, w1k_ref, w1v_ref, w2k_ref, w2vt_ref,
                     kn_ref, bdk_ref, kc_out, vct_out):
    def hidden(x_ref, pe_ref, w1_ref):
        n = x_ref.shape[1] // CMP_STRIDE
        u = jnp.zeros((n, w1_ref.shape[2]), F32)
        v = jnp.zeros((n, w1_ref.shape[2]), F32)
        for r in range(0, CMP_STRIDE, 2):
            tok = [x_ref[0, pl.ds(r + d, n, stride=CMP_STRIDE), :] for d in range(2)]
            cols = slice(r * KV_W, (r + 2) * KV_W)
            for a, acc in ((0, "u"), (1, "v")):
                lhs = jnp.concatenate([tok[d] + pe_ref[a:a + 1, (r + d) * KV_W:(r + d + 1) * KV_W]
                                       for d in range(2)], axis=1).astype(BF16)
                prod = jnp.dot(lhs, w1_ref[a, cols, :], preferred_element_type=F32)
                if acc == "u":
                    u = u + prod
                else:
                    v = v + prod
        hid = u + pltpu.roll(v, n - 1, 0)
        return jax.nn.gelu(hid, approximate=True).astype(BF16)

    kc = jnp.dot(hidden(kx_ref, pek_ref, w1k_ref), w2k_ref[...], preferred_element_type=F32)
    ssq = jnp.dot((kc * kc).astype(BF16), bdk_ref[...], preferred_element_type=F32)
    kc_out[0] = (kc * lax.rsqrt(ssq * (1.0 / HEAD_DIM) + EPS) * kn_ref[...]).astype(BF16)
    vct_out[0] = _with_ones_rows(lax.dot_general(w2vt_ref[...], hidden(vx_ref, pev_ref, w1v_ref), _NT,
                                                 preferred_element_type=F32).astype(BF16))


def _compress_call(kx, vx, pek, pev, w1k, w1v, w2k, w2vt, kn_t, bdk):
    b, t, _ = kx.shape
    nrow = t // CMP_STRIDE
    wide, hid2 = w1k.shape[1], w1k.shape[2]
    full = lambda shp: pl.BlockSpec(shp, lambda i: (0,) * len(shp))
    xs = pl.BlockSpec((1, t, KV_W), lambda i: (i, 0, 0))
    return pl.pallas_call(
        _compress_kernel,
        grid=(b,),
        in_specs=[xs, xs, full((2, wide)), full((2, wide)), full((2, wide, hid2)),
                  full((2, wide, hid2)), full((hid2, KV_W)), full((KV_W, hid2)),
                  full((1, KV_W)), full((KV_W, KV_W))],
        out_specs=[pl.BlockSpec((1, nrow, KV_W), lambda i: (i, 0, 0)),
                   pl.BlockSpec((1, V_TILE_ROWS, nrow), lambda i: (i, 0, 0))],
        out_shape=[jax.ShapeDtypeStruct((b, nrow, KV_W), BF16),
                   jax.ShapeDtypeStruct((b, V_TILE_ROWS, nrow), BF16)],
        compiler_params=_cparams(1),
        name="compress",
    )(kx, vx, pek, pev, w1k, w1v, w2k, w2vt, kn_t, bdk)


def _attn_kernel(q_ref, kc_ref, vct_ref, ks_ref, vst_ref, kw_ref, vwt_ref, gt_ref, ovt_ref,
                 eb_ref, sb_ref, wb_ref, o_ref, sel_scr, acc_scr, out_scr, qp_scr, s_scr,
                 fin_scr, *, n_cmp_rows, eb_shift):
    n_sb = sel_scr.shape[1]
    ci = jnp.minimum(pl.program_id(1), n_sb - 1)
    par = lax.rem(ci, 2)
    n_total = ks_ref.shape[1] // KEY_CHUNK
    vsl = lambda g: slice(g * V_ROWS, (g + 1) * V_ROWS)

    @pl.when((pl.program_id(0) == 0) & (pl.program_id(1) == 0))
    def _():
        def zero(g, carry):
            fin_scr[g] = jnp.zeros(fin_scr.shape[1:], F32)
            sel_scr[g] = jnp.zeros(sel_scr.shape[1:], F32)
            return carry
        lax.fori_loop(0, N_KV, zero, 0)

    def chunk_qk(g, c):
        kk = ks_ref[0, pl.ds(pl.multiple_of(c * KEY_CHUNK, KEY_CHUNK), KEY_CHUNK), :]
        return jnp.dot(kk, qp_scr[g], preferred_element_type=F32).astype(BF16)

    n_chunks = ci // BLOCKS_PER_CHUNK + 1
    tails_per_big = BIG_CHUNKS // TAIL_CHUNKS
    n_tails = (n_chunks + TAIL_CHUNKS - 1) // TAIL_CHUNKS
    n_big = n_tails // tails_per_big
    tail = lax.rem(n_tails, tails_per_big)
    tail_start = n_big * BIG_CHUNKS

    def big_units(i):
        return [(g, jnp.minimum(BIG_CHUNKS * i + j, n_total - 1))
                for g in range(N_KV) for j in range(BIG_CHUNKS)]

    def tail_units(size):
        return [(g, jnp.minimum(tail_start + j, n_total - 1))
                for g in range(N_KV) for j in range(size * TAIL_CHUNKS)]

    short = [s for s in range(1, tails_per_big) if s * TAIL_CHUNKS < QK_AHEAD]
    tail_options = [(tail == s, tail_units(s)) for s in short] + [(None, tail_units(tails_per_big - 1))]

    def pick_units(options):
        pad = lambda units: (units + [units[-1]] * QK_AHEAD)[:QK_AHEAD]
        units = pad(options[-1][1])
        for cond, cand in reversed(options[:-1]):
            units = [(jnp.where(cond, g_a, g_b), jnp.where(cond, c_a, c_b))
                     for (g_a, c_a), (g_b, c_b) in zip(pad(cand), units)]
        return [(g, jnp.minimum(c, n_total - 1)) for g, c in units]

    def prefetch_scores(units):
        for k in range(QK_AHEAD):
            s_scr[k] = chunk_qk(*units[k])

    def before_loops(rows):
        pieces = []
        for g in range(N_KV):
            for half in range(GQ // LANES):
                a = fin_scr[g, :, half * LANES:(half + 1) * LANES]
                stacked = jnp.concatenate([a, pltpu.roll(a, Q_BLOCK, 1)], axis=0)
                pieces.append(stacked.T[:Q_BLOCK, :])
        o_ref[0] = jnp.concatenate(pieces, axis=1).astype(BF16)

        zeros_q = jnp.zeros((HEAD_DIM, GQ), BF16)
        q_padded = [jnp.concatenate([q_ref[0, 0, 0], zeros_q], axis=0),
                    jnp.concatenate([zeros_q, q_ref[0, 0, 1]], axis=0)]
        for g in range(N_KV):
            qp_scr[g] = q_padded[g]

        nsel = min(n_sb, rows * CMP_STRIDE // SEL_BLOCK)
        lane = lax.broadcasted_iota(jnp.int32, (nsel, LANES), 1)
        cmp_c = n_cmp_rows - 4
        e0 = cmp_c - 4 * ci + jnp.where(par == 0, eb_shift[0], eb_shift[1])
        e0 = pl.multiple_of(e0, 8)
        o_cmp = []
        imp = []
        w0 = ci // 2 - (WIN_TILES - 1)
        win_tiles = [jnp.maximum(w0 + j, 0) for j in range(WIN_TILES)]
        s_cmp = jnp.dot(kc_ref[0, :rows, :], jnp.concatenate(q_padded, axis=1),
                        preferred_element_type=F32)
        for g in range(N_KV):
            sc = s_cmp[:, g * GQ:(g + 1) * GQ] + eb_ref[par, g, pl.ds(e0, rows), :]
            m = jnp.max(sc, axis=0, keepdims=True)
            e = jnp.exp2(sc - m).astype(BF16)
            ov = jnp.dot(vct_ref[0, vsl(g), :rows], e, preferred_element_type=F32)
            inv = jnp.where(m > 0.5 * NEG, 1.0 / ov[HEAD_DIM:HEAD_DIM + 1, :], 0.0)
            o_cmp.append(ov[:HEAD_DIM, :] * inv)
            ir = jnp.dot(ovt_ref[:nsel, :rows], e, preferred_element_type=F32) * inv
            a = ir[:, :LANES] + ir[:, LANES:]
            imp.append(a + pltpu.roll(a, Q_BLOCK, 1))

        s_win = [[jnp.dot(kw_ref[0, pl.ds(pl.multiple_of(tj * V_TILE, V_TILE), V_TILE), :],
                          qp_scr[g], preferred_element_type=F32).astype(BF16) for tj in win_tiles]
                 for g in range(N_KV)]

        prefetch_scores(pick_units([(n_big > 0, big_units(0))] + tail_options))

        jidx = lax.broadcasted_iota(jnp.int32, (nsel, LANES), 0)
        jf = jidx.astype(F32)
        valid = jidx <= ci
        forced = (jidx == 0) | (jidx == ci) | (jidx == ci - 1)
        score = jnp.where(valid, jnp.where(forced, -2.0, jnp.where(lane < Q_BLOCK, imp[0], imp[1])),
                          -1.0)

        def first_max(score):
            pairs = [(score[r:r + SUBLANES], jf[r:r + SUBLANES]) for r in range(0, nsel, SUBLANES)]
            while len(pairs) > 1:
                nxt = []
                for (va, ia), (vb, ib) in zip(pairs[0::2], pairs[1::2]):
                    keep = va >= vb
                    nxt.append((jnp.where(keep, va, vb), jnp.where(keep, ia, ib)))
                pairs = nxt + pairs[len(pairs) - len(pairs) % 2:]
            v8, i8 = pairs[0]
            mx = jnp.max(v8, axis=0, keepdims=True)
            return jnp.min(jnp.where(v8 == mx, i8, float(nsel)), axis=0, keepdims=True)

        for _ in range(min(N_SELECT, nsel) - 3):
            first = first_max(score)
            score = jnp.where(jf == first, -2.0, score)
        selneg = jnp.where((score < -1.5) & valid, 0.0, NEG)
        swapped = pltpu.roll(selneg, Q_BLOCK, 1)
        left = jnp.where(lane < Q_BLOCK, selneg, swapped)
        right = jnp.where(lane < Q_BLOCK, swapped, selneg)
        sel_scr[0, :nsel] = jnp.concatenate([left, left], axis=1)
        sel_scr[1, :nsel] = jnp.concatenate([right, right], axis=1)

        for g in range(N_KV):
            parts = []
            for i in range(2 * WIN_TILES):
                delta = par + WIN_BLOCKS - i
                ok = (delta >= 0) & (delta <= WIN_BLOCKS) & (delta <= ci)
                tile = jnp.where(ok, delta, WIN_BLOCKS + 1)
                half = s_win[g][i // 2][(i % 2) * SEL_BLOCK:(i % 2 + 1) * SEL_BLOCK, :]
                parts.append(half + wb_ref[g, tile])
            s = jnp.concatenate(parts, axis=0)
            m = jnp.max(s, axis=0, keepdims=True)
            pb = jnp.exp2(s - m)
            o_win = jnp.zeros((V_ROWS, GQ), F32)
            for j, tj in enumerate(win_tiles):
                o_win = o_win + jnp.dot(vwt_ref[0, tj, vsl(g), :], pb[j * V_TILE:(j + 1) * V_TILE, :],
                                        preferred_element_type=F32)
            w_scale = (gt_ref[0, 0, 2 * N_KV + g:2 * N_KV + g + 1, :]
                       * (1.0 / o_win[HEAD_DIM:HEAD_DIM + 1, :]))
            out_scr[g] = gt_ref[0, 0, g:g + 1, :] * o_cmp[g] + w_scale * o_win[:HEAD_DIM, :]
            acc_scr[g] = jnp.zeros((V_ROWS, GQ), F32)

    row_steps = [r for r in range(CMP_ROW_STEP, n_cmp_rows + 1, CMP_ROW_STEP)] or [n_cmp_rows]
    variant = jnp.minimum((4 * ci + 2) // CMP_ROW_STEP, len(row_steps) - 1)
    lax.switch(variant, [functools.partial(before_loops, r) for r in row_steps])

    def chunk_softmax(s, g, c, extra):
        parts = []
        for i in range(BLOCKS_PER_CHUNK):
            kb = c * BLOCKS_PER_CHUNK + i
            blk = s[i * SEL_BLOCK:(i + 1) * SEL_BLOCK, :]
            mrow = sel_scr[g, pl.ds(kb, 1), :]
            if extra is not None:
                mrow = mrow + extra
            tile = jnp.clip(ci - kb, 0, NEAR_TILES)
            parts.append(blk + sb_ref[g, tile] + mrow.astype(BF16))
        s = jnp.concatenate(parts, axis=0)
        m_c = jnp.max(s, axis=0, keepdims=True)
        return m_c.astype(F32), jnp.exp2(s - m_c)

    def chunk_pv(pb, g, c):
        return jnp.dot(vst_ref[0, c, vsl(g), :], pb, preferred_element_type=F32)

    def merge_step(units, extras, next_units, carry):
        parked = min(QK_AHEAD, len(units))
        scores = {k: s_scr[k] for k in range(parked)}
        pending = list(range(parked, len(units) + (0 if next_units is None else QK_AHEAD)))

        def issue_scores():
            k = pending.pop(0)
            if k < len(units):
                scores[k] = chunk_qk(*units[k])
            else:
                s_scr[k - len(units)] = chunk_qk(*next_units[k - len(units)])

        for _ in range(min(QK_DEPTH - QK_AHEAD, len(pending))):
            issue_scores()
        results = []
        for k, (g, c) in enumerate(units):
            m_c, pb = chunk_softmax(scores.pop(k), g, c, extras[k])
            if pending:
                issue_scores()
            results.append((m_c, chunk_pv(pb, g, c)))
        per_g = len(units) // N_KV
        new = []
        for g in range(N_KV):
            m_run = carry[g]
            acc = acc_scr[g]
            for first in range(g * per_g, (g + 1) * per_g, MERGE_CHUNKS):
                stats = results[first:min(first + MERGE_CHUNKS, (g + 1) * per_g)]
                m_new = m_run
                for m_c, _ in stats:
                    m_new = jnp.maximum(m_new, m_c)
                acc = jnp.exp2(m_run - m_new) * acc
                for m_c, pv in stats:
                    acc = acc + jnp.exp2(m_c - m_new) * pv
                m_run = m_new
            acc_scr[g] = acc
            new.append(m_run)
        return tuple(new)

    def big_step(i, carry):
        nxt = pick_units([(i + 1 < n_big, big_units(i + 1))] + tail_options)
        extras = [jnp.where(BIG_CHUNKS * i + j < n_chunks, 0.0, NEG)
                  if j > BIG_CHUNKS - TAIL_CHUNKS else None
                  for _ in range(N_KV) for j in range(BIG_CHUNKS)]
        return merge_step(big_units(i), extras, nxt, carry)

    carry = lax.fori_loop(0, n_big, big_step, (jnp.full((1, GQ), NEG, F32),) * N_KV)

    def tail_step(size):
        extras = [None if j == 0 else jnp.where(tail_start + j < n_chunks, 0.0, NEG)
                  for _ in range(N_KV) for j in range(size * TAIL_CHUNKS)]
        merge_step(tail_units(size), extras, None, carry)

    def run_tail(lo, hi):
        if hi - lo == 1:
            if lo > 0:
                tail_step(lo)
        else:
            mid = (lo + hi) // 2
            lax.cond(tail < mid, functools.partial(run_tail, lo, mid), functools.partial(run_tail, mid, hi))

    run_tail(0, tails_per_big)

    for g in range(N_KV):
        acc = acc_scr[g]
        scale = gt_ref[0, 0, N_KV + g:N_KV + g + 1, :] * (1.0 / acc[HEAD_DIM:HEAD_DIM + 1, :])
        fin_scr[g] = out_scr[g] + scale * acc[:HEAD_DIM, :]


def _attn_call(qt, kc, vct, ks, vst, kw, vwt, gt, ovt, ebank, sbank, wbank, eb_shift):
    b, nq = qt.shape[0], qt.shape[1]
    t = ks.shape[1]
    n_cmp_rows = kc.shape[1]
    n_sb = t // SEL_BLOCK
    per_b = lambda shp: pl.BlockSpec((1,) + shp[1:], lambda i, j: (i,) + (0,) * (len(shp) - 1))
    per_q = lambda shp: pl.BlockSpec(
        (1, 1) + shp[2:], lambda i, j: (i, jnp.minimum(j, nq - 1)) + (0,) * (len(shp) - 2))
    full = lambda shp: pl.BlockSpec(shp, lambda i, j: (0,) * len(shp))
    args = (qt, kc, vct, ks, vst, kw, vwt, gt, ovt, ebank, sbank, wbank)
    specs = [per_q(qt.shape), per_b(kc.shape), per_b(vct.shape), per_b(ks.shape), per_b(vst.shape),
             per_b(kw.shape), per_b(vwt.shape), per_q(gt.shape), full(ovt.shape),
             full(ebank.shape), full(sbank.shape), full(wbank.shape)]
    acc_like = pltpu.VMEM((N_KV, HEAD_DIM, GQ), F32)
    return pl.pallas_call(
        functools.partial(_attn_kernel, n_cmp_rows=n_cmp_rows, eb_shift=eb_shift),
        grid=(b, nq + 1),
        in_specs=specs,
        out_specs=pl.BlockSpec((1, Q_BLOCK, ATTN_W), lambda i, j: (i, jnp.maximum(j - 1, 0), 0)),
        out_shape=jax.ShapeDtypeStruct((b, t, ATTN_W), BF16),
        scratch_shapes=[pltpu.VMEM((N_KV, n_sb, GQ), F32),
                        pltpu.VMEM((N_KV, V_ROWS, GQ), F32),
                        acc_like,
                        pltpu.VMEM((N_KV, KV_W, GQ), BF16),
                        pltpu.VMEM((QK_AHEAD, KEY_CHUNK, GQ), BF16),
                        acc_like],
        compiler_params=_cparams(2),
        name="nsa_attention",
    )(*args)


def _ffn_kernel(x_ref, a_ref, c_ref, mod_ref, n2_ref, wo_ref, w1_ref, w2_ref, o_ref):
    aw = a_ref.shape[2]
    mix = jnp.dot(a_ref[0], wo_ref[0:aw, :], preferred_element_type=F32)
    mix = mix + jnp.dot(c_ref[0], wo_ref[aw:, :], preferred_element_type=F32)
    x1 = x_ref[0] + mod_ref[0, 2:3, :] * mix
    ms = jnp.mean(x1 * x1, axis=-1, keepdims=True)
    y = x1 * lax.rsqrt(ms + EPS) * n2_ref[...]
    h2 = (y * (1.0 + mod_ref[0, 4:5, :]) + mod_ref[0, 3:4, :]).astype(BF16)
    d_ff = w1_ref.shape[1]
    ff = jnp.zeros(x1.shape, F32)
    for j in range(d_ff // FF_CHUNK):
        a = jnp.dot(h2, w1_ref[:, j * FF_CHUNK:(j + 1) * FF_CHUNK], preferred_element_type=F32)
        a = jnp.maximum(a, 0.0)
        ff = ff + jnp.dot((a * a).astype(BF16), w2_ref[j * FF_CHUNK:(j + 1) * FF_CHUNK, :],
                          preferred_element_type=F32)
    o_ref[0] = x1 + mod_ref[0, 5:6, :] * ff


def _ffn_call(x, attn, conv, mod, norm2, w_out, w_ff1, w_ff2):
    b, t, d = x.shape
    tm = min(ROW_TILE, t)
    row_spec = lambda w: pl.BlockSpec((1, tm, w), lambda i, j: (i, j, 0))
    full = lambda shp: pl.BlockSpec(shp, lambda i, j: (0,) * len(shp),
                                    pipeline_mode=pl.Buffered(1))
    return pl.pallas_call(
        _ffn_kernel,
        grid=(b, t // tm),
        in_specs=[row_spec(d), row_spec(attn.shape[2]), row_spec(conv.shape[2]),
                  pl.BlockSpec((1, N_MOD, d), lambda i, j: (i, 0, 0)),
                  full((1, d)), full(w_out.shape), full(w_ff1.shape), full(w_ff2.shape)],
        out_specs=row_spec(d),
        out_shape=jax.ShapeDtypeStruct((b, t, d), F32),
        compiler_params=_cparams(2),
        name="outproj_mlp",
    )(x, attn, conv, mod, norm2, w_out, w_ff1, w_ff2)


def _block_diag_ones(n):
    idx = np.arange(n) // HEAD_DIM
    return jnp.asarray(idx[:, None] == idx[None, :], dtype=BF16)


def _pack_w_in(w_in):
    d = w_in.shape[0]
    conv_w = d - ATTN_W
    sizes = [ATTN_W] + [KV_W] * 6 + [N_BRANCH * N_HEADS] + [conv_w] * 3
    offs = np.concatenate([[0], np.cumsum(sizes)])
    part = lambda i: w_in[:, offs[i]:offs[i + 1]]
    q, kc, vc, ks, vs, kw, vw, g, cgate, bgate, u = (part(i) for i in range(11))
    w_nat = jnp.concatenate([kc, vc, ks, kw, cgate, bgate, u], axis=1).astype(BF16)
    gt = g.reshape(d, N_KV, GQA, N_BRANCH).transpose(2, 3, 1, 0).reshape(GQA, N_BRANCH * N_KV, d)
    gt = jnp.pad(gt, ((0, 0), (0, SUBLANES - N_BRANCH * N_KV), (0, 0))).reshape(_G_ROWS, d)
    w_tr = jnp.concatenate([q.T, vs.T, vw.T, gt], axis=0).astype(BF16)
    return w_nat, w_tr


def _expand_w1(w1):
    hid = w1.shape[1]
    w = w1.reshape(2, CMP_STRIDE, HEAD_DIM, hid).astype(BF16)
    zero = jnp.zeros_like(w)
    per_group = [jnp.concatenate([w if k == g else zero for k in range(N_KV)], axis=-1)
                 for g in range(N_KV)]
    return jnp.stack(per_group, axis=2).reshape(2, CMP_STRIDE * KV_W, N_KV * hid)


def _expand_w2(w2):
    hid = w2.shape[0]
    eye = jnp.eye(N_KV, dtype=w2.dtype).reshape(N_KV, 1, N_KV, 1)
    return (w2.reshape(1, hid, 1, HEAD_DIM) * eye).reshape(N_KV * hid, KV_W).astype(BF16)


def _expand_pe(pe):
    p = pe.reshape(2, CMP_STRIDE, 1, HEAD_DIM)
    return jnp.broadcast_to(p, (2, CMP_STRIDE, N_KV, HEAD_DIM)).reshape(2, CMP_STRIDE * KV_W)


def _bucket_thresholds():
    n = np.arange(2 * REL_MAX_DIST)
    max_exact = REL_BUCKETS // 2
    nf = np.maximum(n, max_exact).astype(np.float32)
    ratio = np.log(nf / np.float32(max_exact)) / np.float32(math.log(REL_MAX_DIST / max_exact))
    large = max_exact + (ratio * np.float32(REL_BUCKETS - max_exact)).astype(np.int32)
    table = np.where(n < max_exact, n, np.minimum(large, REL_BUCKETS - 1))
    return tuple(int(np.searchsorted(table, k, side="left")) for k in range(REL_BUCKETS))


def _bank_call(bias_rows, lead, n_tiles, dist_fn, name):
    nl = len(lead)
    thr = _bucket_thresholds()

    def body(rows_ref, o_ref):
        lead_ids = [pl.program_id(a) for a in range(nl)]
        row = lax.broadcasted_iota(jnp.int32, (SEL_BLOCK, GQ), 0)
        qi = lax.broadcasted_iota(jnp.int32, (SEL_BLOCK, GQ), 1) & (Q_BLOCK - 1)

        def tile(t, carry):
            dist, ok = dist_fn(lead_ids, t, row, qi)
            v = jnp.broadcast_to(rows_ref[0, 0:1, :], (SEL_BLOCK, GQ))
            for k in range(1, REL_BUCKETS):
                v = jnp.where(dist >= thr[k], rows_ref[0, k:k + 1, :], v)
            o_ref[(0,) * (nl + 1) + (t,)] = jnp.where(ok, v, NEG)
            return carry

        lax.fori_loop(0, n_tiles, tile, 0)

    return pl.pallas_call(
        body,
        grid=tuple(lead) + (N_KV,),
        in_specs=[pl.BlockSpec((1, REL_BUCKETS, GQ), lambda *i: (i[nl], 0, 0))],
        out_specs=pl.BlockSpec((1,) * (nl + 1) + (n_tiles, SEL_BLOCK, GQ),
                               lambda *i: tuple(i) + (0, 0, 0)),
        out_shape=jax.ShapeDtypeStruct(tuple(lead) + (N_KV, n_tiles, SEL_BLOCK, GQ), F32),
        compiler_params=_cparams(nl + 1),
        name=name,
    )(bias_rows)


def _bias_banks(rel_bias, t):
    n_cmp_rows = t // CMP_STRIDE
    rows = rel_bias.reshape(REL_BUCKETS, N_KV, GQA).transpose(1, 0, 2)
    rows = jnp.repeat(rows, Q_BLOCK, axis=2) * LOG2E

    def sel_dist(lead, tile, row, qi):
        dist = SEL_BLOCK * tile + qi - row
        return dist, dist >= 0

    sbank = _bank_call(rows, (), NEAR_TILES + 1, sel_dist, "bias_bank_sel")

    def win_dist(lead, tile, row, qi):
        dist = SEL_BLOCK * tile + qi - row
        return dist, (dist >= 0) & (dist < WINDOW)

    wbank = _bank_call(rows, (), WIN_BLOCKS + 2, win_dist, "bias_bank_win")

    cmp_c = n_cmp_rows - 4
    shifts = tuple(int((-(cmp_c - 4 * p)) % 8) for p in range(2))
    n_tiles = (cmp_c + n_cmp_rows + 8 + SEL_BLOCK - 1) // SEL_BLOCK

    def cmp_dist(lead, tile, row, qi):
        e = SEL_BLOCK * tile + row - jnp.where(lead[0] == 0, shifts[0], shifts[1])
        dist = qi - CMP_STRIDE * e + (CMP_STRIDE * cmp_c - (CMP_BLOCK - 1))
        return dist, (dist >= 0) & (e >= 0)

    ebank = _bank_call(rows, (2,), n_tiles, cmp_dist, "bias_bank_cmp")
    ebank = ebank.reshape(2, N_KV, n_tiles * SEL_BLOCK, GQ)
    return ebank, sbank, wbank, shifts


def _overlap_t(t):
    n_cmp_rows = t // CMP_STRIDE
    n_sb = t // SEL_BLOCK
    c_start = np.arange(n_cmp_rows)[None, :] * CMP_STRIDE
    s_start = np.arange(n_sb)[:, None] * SEL_BLOCK
    ov = np.clip(np.minimum(c_start + CMP_BLOCK, s_start + SEL_BLOCK)
                 - np.maximum(c_start, s_start), 0, None) / CMP_BLOCK
    ov[:, n_cmp_rows - 1] = 0.0
    return jnp.asarray(ov, dtype=BF16)


def _layer(x, c_pad, w_in, q_norm, k_norm, cmp_pe_k, cmp_w1_k, cmp_w2_k, cmp_pe_v, cmp_w1_v,
           cmp_w2_v, rel_bias, conv_w, w_out, norm1, norm2, w_ada, b_ada, w_ff1, w_ff2):
    b, t, d = x.shape
    scale = HEAD_DIM ** -0.5

    mod = _mod_call(c_pad, w_ada, b_ada)[:b].reshape(b, N_MOD, d)

    qn_col = (jnp.tile(q_norm, N_HEADS) * (scale * LOG2E)).reshape(ATTN_W, 1)
    kn_t = jnp.tile(k_norm, N_KV).reshape(1, KV_W)
    bdq = _block_diag_ones(ATTN_W)
    bdk = _block_diag_ones(KV_W)
    w_nat, w_tr = _pack_w_in(w_in)
    qt, kc_raw, vc_raw, ks, vst, kw, vwt, gt, conv = _inproj_call(
        x, mod, norm1.reshape(1, d), w_nat, w_tr, qn_col, kn_t, conv_w, bdq, bdk)

    kc, vct = _compress_call(
        kc_raw, vc_raw, _expand_pe(cmp_pe_k), _expand_pe(cmp_pe_v), _expand_w1(cmp_w1_k), _expand_w1(cmp_w1_v),
        _expand_w2(cmp_w2_k), _expand_w2(cmp_w2_v).T, kn_t, bdk)

    ebank, sbank, wbank, eb_shift = _bias_banks(rel_bias, t)
    attn = _attn_call(qt, kc, vct, ks, vst, kw, vwt, gt, _overlap_t(t), ebank, sbank.astype(BF16),
                      wbank.astype(BF16), eb_shift)

    return _ffn_call(x, attn, conv, mod, norm2.reshape(1, d), w_out.astype(BF16),
                     w_ff1.astype(BF16), w_ff2.astype(BF16))


def kernel(x, c, w_in, q_norm, k_norm, cmp_pe_k, cmp_w1_k, cmp_w2_k, cmp_pe_v, cmp_w1_v, cmp_w2_v,
           rel_bias, conv_w, w_out, norm1, norm2, w_ada, b_ada, w_ff1, w_ff2):
    b = x.shape[0]
    c_pad = jnp.pad(c, ((0, (-b) % 8), (0, 0)))
    for l in range(w_in.shape[0]):
        x = _layer(x, c_pad, w_in[l], q_norm[l], k_norm[l], cmp_pe_k[l], cmp_w1_k[l], cmp_w2_k[l],
                   cmp_pe_v[l], cmp_w1_v[l], cmp_w2_v[l], rel_bias, conv_w[l], w_out[l],
                   norm1[l], norm2[l], w_ada[l], b_ada[l], w_ff1[l], w_ff2[l])
    return x
```

```python
import functools
import math

import numpy as np
import jax
import jax.numpy as jnp
from jax import lax
from jax.experimental import pallas as pl
from jax.experimental.pallas import tpu as pltpu

HEAD_DIM = 64
N_HEADS = 8
N_KV = 2
GQA = N_HEADS // N_KV
ATTN_W = N_HEADS * HEAD_DIM
KV_W = N_KV * HEAD_DIM
CONV_K = 3
CMP_BLOCK = 32
CMP_STRIDE = 16
SEL_BLOCK = 64
N_SELECT = 16
WINDOW = 512
Q_BLOCK = 64
REL_BUCKETS = 32
REL_MAX_DIST = 1024
N_MOD = 6
N_BRANCH = 3
EPS = 1e-6
NEG = -1e30

LANES = 128
SUBLANES = 8
GQ = GQA * Q_BLOCK
KEY_CHUNK = 256
BLOCKS_PER_CHUNK = KEY_CHUNK // SEL_BLOCK
V_TILE = 128
BF16_ROWS = 16
V_ROWS = HEAD_DIM + BF16_ROWS
V_TILE_ROWS = N_KV * V_ROWS
CMP_ROW_STEP = 128
WIN_BLOCKS = WINDOW // SEL_BLOCK
WIN_TILES = WINDOW // V_TILE + 1
NEAR_TILES = (REL_MAX_DIST + Q_BLOCK - 1) // SEL_BLOCK + 1
ROW_TILE = 512
INPROJ_TILE = 1024
INPROJ_SPLIT = 4
BIG_CHUNKS = 16
TAIL_CHUNKS = 1
assert BIG_CHUNKS % TAIL_CHUNKS == 0
MERGE_CHUNKS = 4
QK_AHEAD = 4
QK_DEPTH = 5
SMALLEST_IMPORTANCE = 2.0 ** -100
LOG2E = math.log2(math.e)
FF_CHUNK = 1024
VMEM_LIMIT = 56 * 1024 * 1024

F32 = jnp.float32
BF16 = jnp.bfloat16
_NT = (((1,), (1,)), ((), ()))


def _cparams(n_axes):
    return pltpu.CompilerParams(dimension_semantics=("arbitrary",) * n_axes,
                                vmem_limit_bytes=VMEM_LIMIT)


def _with_ones_rows(vt):
    ones = jnp.ones((BF16_ROWS, vt.shape[1]), vt.dtype)
    parts = []
    for g in range(N_KV):
        parts += [vt[g * HEAD_DIM:(g + 1) * HEAD_DIM, :], ones]
    return jnp.concatenate(parts, axis=0)


def _swap_halves(p0, p1):
    low = lax.broadcasted_iota(jnp.int32, p0.shape, 1) < LANES // 2
    return (jnp.where(low, p0, pltpu.roll(p1, LANES // 2, 1)),
            jnp.where(low, pltpu.roll(p0, LANES // 2, 1), p1))


def _mod_kernel(c_ref, w_ref, b_ref, o_ref):
    c = c_ref[...]
    a = c * jax.nn.sigmoid(c)
    o_ref[...] = jnp.dot(a, w_ref[...], preferred_element_type=F32,
                         precision=lax.Precision.HIGHEST) + b_ref[...]


def _mod_call(c_pad, w_ada, b_ada):
    rows, d = c_pad.shape
    n = w_ada.shape[1]
    tn = n // N_MOD
    return pl.pallas_call(
        _mod_kernel,
        grid=(n // tn,),
        in_specs=[pl.BlockSpec((rows, d), lambda j: (0, 0)),
                  pl.BlockSpec((d, tn), lambda j: (0, j)),
                  pl.BlockSpec((1, tn), lambda j: (0, j))],
        out_specs=pl.BlockSpec((rows, tn), lambda j: (0, j)),
        out_shape=jax.ShapeDtypeStruct((rows, n), F32),
        compiler_params=_cparams(1),
        name="adaln_mod",
    )(c_pad, w_ada, b_ada.reshape(1, n))


_N_KC, _N_VC, _N_KS, _N_KW, _N_CONV = 0, KV_W, 2 * KV_W, 3 * KV_W, 4 * KV_W
_T_Q, _T_VS, _T_VW, _T_G = 0, ATTN_W, ATTN_W + KV_W, ATTN_W + 2 * KV_W
_G_ROWS = GQA * SUBLANES


def _inproj_kernel(x_ref, mod_ref, n1_ref, wn_ref, wt_ref, qn_ref, kn_ref, cw_ref, bdq_ref, bdk_ref,
                   qt_out, kc_out, vc_out, ks_out, vst_out, kw_out, vwt_out, gt_out, conv_out,
                   carry, *, conv_w):
    t = pl.program_id(1)
    tm = x_ref.shape[1]
    sub = tm // INPROJ_SPLIT

    @pl.when(t == 0)
    def _():
        carry[...] = jnp.zeros_like(carry)

    prev2, prev1 = carry[6:7, :], carry[7:8, :]
    c_bg = _N_CONV + conv_w
    c_u = c_bg + conv_w

    for part in range(INPROJ_SPLIT):
        rows = slice(part * sub, (part + 1) * sub)
        x = x_ref[0, rows, :]
        ms = jnp.mean(x * x, axis=-1, keepdims=True)
        y = x * lax.rsqrt(ms + EPS) * n1_ref[...]
        h = (y * (1.0 + mod_ref[0, 1:2, :]) + mod_ref[0, 0:1, :]).astype(BF16)

        def proj(a, b):
            return jnp.dot(h, wn_ref[:, a:b], preferred_element_type=F32)

        def proj_t(a, b):
            return lax.dot_general(wt_ref[a:b, :], h, _NT, preferred_element_type=F32)

        def head_norm(v, gain):
            ssq = jnp.dot((v * v).astype(BF16), bdk_ref[...], preferred_element_type=F32)
            return v * lax.rsqrt(ssq * (1.0 / HEAD_DIM) + EPS) * gain

        qf = proj_t(_T_Q, _T_VS)
        vg = proj_t(_T_VS, _T_G + _G_ROWS)
        kv = proj(_N_KC, _N_CONV)
        cv = proj(_N_CONV, c_u + conv_w)

        ssq = jnp.dot(bdq_ref[...], (qf * qf).astype(BF16), preferred_element_type=F32)
        qf = qf * lax.rsqrt(ssq * (1.0 / HEAD_DIM) + EPS) * qn_ref[...]
        gf = jax.nn.sigmoid(vg[2 * KV_W:, :])
        for c in range(sub // LANES):
            blk = (part * sub) // Q_BLOCK + 2 * c
            cols = slice(c * LANES, (c + 1) * LANES)
            for g in range(N_KV):
                pc = [qf[(g * GQA + r) * HEAD_DIM:(g * GQA + r + 1) * HEAD_DIM, cols]
                      for r in range(GQA)]
                lo01, hi01 = _swap_halves(pc[0], pc[1])
                lo23, hi23 = _swap_halves(pc[2], pc[3])
                qt_out[0, blk, g] = jnp.concatenate([lo01, lo23], axis=1).astype(BF16)
                qt_out[0, blk + 1, g] = jnp.concatenate([hi01, hi23], axis=1).astype(BF16)
            pc = [gf[r * SUBLANES:(r + 1) * SUBLANES, cols] for r in range(GQA)]
            lo01, hi01 = _swap_halves(pc[0], pc[1])
            lo23, hi23 = _swap_halves(pc[2], pc[3])
            gt_out[0, blk] = jnp.concatenate([lo01, lo23], axis=1)
            gt_out[0, blk + 1] = jnp.concatenate([hi01, hi23], axis=1)

        vs_f = _with_ones_rows(vg[:KV_W, :].astype(BF16))
        vw_f = _with_ones_rows(vg[KV_W:2 * KV_W, :].astype(BF16))
        for j in range(sub // KEY_CHUNK):
            vst_out[0, (part * sub) // KEY_CHUNK + j] = vs_f[:, j * KEY_CHUNK:(j + 1) * KEY_CHUNK]
        for j in range(sub // V_TILE):
            vwt_out[0, (part * sub) // V_TILE + j] = vw_f[:, j * V_TILE:(j + 1) * V_TILE]

        kc_out[0, rows, :] = kv[:, _N_KC:_N_VC]
        vc_out[0, rows, :] = kv[:, _N_VC:_N_KS]
        ks_out[0, rows, :] = head_norm(kv[:, _N_KS:_N_KW], kn_ref[...]).astype(BF16)
        kw_out[0, rows, :] = head_norm(kv[:, _N_KW:_N_CONV], kn_ref[...]).astype(BF16)

        z = cv[:, :conv_w] * cv[:, 2 * conv_w:]
        row = lax.broadcasted_iota(jnp.int32, z.shape, 0)
        z1 = jnp.where(row == 0, prev1, pltpu.roll(z, 1, 0))
        z2 = jnp.where(row == 0, prev2, jnp.where(row == 1, prev1, pltpu.roll(z, 2, 0)))
        zc = cw_ref[0:1, :] * z2 + cw_ref[1:2, :] * z1 + cw_ref[2:3, :] * z
        conv_out[0, rows, :] = (cv[:, conv_w:2 * conv_w] * zc).astype(BF16)
        prev2, prev1 = z[sub - 2:sub - 1, :], z[sub - 1:sub, :]
        if part == INPROJ_SPLIT - 1:
            carry[...] = z[sub - SUBLANES:sub, :]


def _inproj_call(x, mod, norm1, w_nat, w_tr, qn_col, kn_t, conv_w, bdq, bdk):
    b, t, d = x.shape
    tm = min(INPROJ_TILE, t)
    cw = conv_w.shape[1]
    nq = t // Q_BLOCK
    row_spec = lambda w: pl.BlockSpec((1, tm, w), lambda i, j: (i, j, 0))
    full = lambda shp: pl.BlockSpec(shp, lambda i, j: (0,) * len(shp))
    vt_spec = lambda w: pl.BlockSpec((1, tm // w, V_TILE_ROWS, w), lambda i, j: (i, j, 0, 0))
    vt_shape = lambda w: jax.ShapeDtypeStruct((b, t // w, V_TILE_ROWS, w), BF16)
    kv = lambda dt: jax.ShapeDtypeStruct((b, t, KV_W), dt)
    out_specs = [pl.BlockSpec((1, tm // Q_BLOCK, N_KV, HEAD_DIM, GQ), lambda i, j: (i, j, 0, 0, 0)),
                 row_spec(KV_W), row_spec(KV_W), row_spec(KV_W), vt_spec(KEY_CHUNK), row_spec(KV_W),
                 vt_spec(V_TILE),
                 pl.BlockSpec((1, tm // Q_BLOCK, SUBLANES, GQ), lambda i, j: (i, j, 0, 0)),
                 row_spec(cw)]
    out_shape = [jax.ShapeDtypeStruct((b, nq, N_KV, HEAD_DIM, GQ), BF16),
                 kv(F32), kv(F32), kv(BF16), vt_shape(KEY_CHUNK), kv(BF16), vt_shape(V_TILE),
                 jax.ShapeDtypeStruct((b, nq, SUBLANES, GQ), F32),
                 jax.ShapeDtypeStruct((b, t, cw), BF16)]
    return pl.pallas_call(
        functools.partial(_inproj_kernel, conv_w=cw),
        grid=(b, t // tm),
        in_specs=[row_spec(d),
                  pl.BlockSpec((1, N_MOD, d), lambda i, j: (i, 0, 0)),
                  full((1, d)), full(w_nat.shape), full(w_tr.shape), full((ATTN_W, 1)),
                  full((1, KV_W)), full((CONV_K, cw)), full((ATTN_W, ATTN_W)), full((KV_W, KV_W))],
        out_specs=out_specs,
        out_shape=out_shape,
        scratch_shapes=[pltpu.VMEM((SUBLANES, cw), F32)],
        compiler_params=_cparams(2),
        name="inproj",
    )(x, mod, norm1, w_nat, w_tr, qn_col, kn_t, conv_w, bdq, bdk)


def _compress_kernel(kx_ref, vx_ref, pek_ref, pev_ref, w1k_ref, w1v_ref, w2k_ref, w2vt_ref,
                     kn_ref, bdk_ref, kc_out, vct_out):
    def hidden(x_ref, pe_ref, w1_ref):
        n = x_ref.shape[1] // CMP_STRIDE
        u = jnp.zeros((n, w1_ref.shape[2]), F32)
        v = jnp.zeros((n, w1_ref.shape[2]), F32)
        for r in range(0, CMP_STRIDE, 2):
            tok = [x_ref[0, pl.ds(r + d, n, stride=CMP_STRIDE), :] for d in range(2)]
            cols = slice(r * KV_W, (r + 2) * KV_W)
            for a, acc in ((0, "u"), (1, "v")):
                lhs = jnp.concatenate([tok[d] + pe_ref[a:a + 1, (r + d) * KV_W:(r + d + 1) * KV_W]
                                       for d in range(2)], axis=1).astype(BF16)
                prod = jnp.dot(lhs, w1_ref[a, cols, :], preferred_element_type=F32)
                if acc == "u":
                    u = u + prod
                else:
                    v = v + prod
        hid = u + pltpu.roll(v, n - 1, 0)
        return jax.nn.gelu(hid, approximate=True).astype(BF16)

    kc = jnp.dot(hidden(kx_ref, pek_ref, w1k_ref), w2k_ref[...], preferred_element_type=F32)
    ssq = jnp.dot((kc * kc).astype(BF16), bdk_ref[...], preferred_element_type=F32)
    kc_out[0] = (kc * lax.rsqrt(ssq * (1.0 / HEAD_DIM) + EPS) * kn_ref[...]).astype(BF16)
    vct_out[0] = _with_ones_rows(lax.dot_general(w2vt_ref[...], hidden(vx_ref, pev_ref, w1v_ref), _NT,
                                                 preferred_element_type=F32).astype(BF16))


def _compress_call(kx, vx, pek, pev, w1k, w1v, w2k, w2vt, kn_t, bdk):
    b, t, _ = kx.shape
    nrow = t // CMP_STRIDE
    wide, hid2 = w1k.shape[1], w1k.shape[2]
    full = lambda shp: pl.BlockSpec(shp, lambda i: (0,) * len(shp))
    xs = pl.BlockSpec((1, t, KV_W), lambda i: (i, 0, 0))
    return pl.pallas_call(
        _compress_kernel,
        grid=(b,),
        in_specs=[xs, xs, full((2, wide)), full((2, wide)), full((2, wide, hid2)),
                  full((2, wide, hid2)), full((hid2, KV_W)), full((KV_W, hid2)),
                  full((1, KV_W)), full((KV_W, KV_W))],
        out_specs=[pl.BlockSpec((1, nrow, KV_W), lambda i: (i, 0, 0)),
                   pl.BlockSpec((1, V_TILE_ROWS, nrow), lambda i: (i, 0, 0))],
        out_shape=[jax.ShapeDtypeStruct((b, nrow, KV_W), BF16),
                   jax.ShapeDtypeStruct((b, V_TILE_ROWS, nrow), BF16)],
        compiler_params=_cparams(1),
        name="compress",
    )(kx, vx, pek, pev, w1k, w1v, w2k, w2vt, kn_t, bdk)


def _attn_kernel(q_ref, kc_ref, vct_ref, ks_ref, vst_ref, kw_ref, vwt_ref, gt_ref, ovt_ref,
                 eb_ref, sb_ref, wb_ref, o_ref, sel_scr, acc_scr, out_scr, qp_scr, s_scr,
                 fin_scr, *, n_cmp_rows, eb_shift):
    n_sb = sel_scr.shape[1]
    ci = jnp.minimum(pl.program_id(1), n_sb - 1)
    par = lax.rem(ci, 2)
    n_total = ks_ref.shape[1] // KEY_CHUNK
    vsl = lambda g: slice(g * V_ROWS, (g + 1) * V_ROWS)

    @pl.when((pl.program_id(0) == 0) & (pl.program_id(1) == 0))
    def _():
        def zero(g, carry):
            fin_scr[g] = jnp.zeros(fin_scr.shape[1:], F32)
            sel_scr[g] = jnp.zeros(sel_scr.shape[1:], F32)
            return carry
        lax.fori_loop(0, N_KV, zero, 0)

    def chunk_qk(g, c):
        kk = ks_ref[0, pl.ds(pl.multiple_of(c * KEY_CHUNK, KEY_CHUNK), KEY_CHUNK), :]
        return jnp.dot(kk, qp_scr[g], preferred_element_type=F32).astype(BF16)

    n_chunks = ci // BLOCKS_PER_CHUNK + 1
    tails_per_big = BIG_CHUNKS // TAIL_CHUNKS
    n_tails = (n_chunks + TAIL_CHUNKS - 1) // TAIL_CHUNKS
    n_big = n_tails // tails_per_big
    tail = lax.rem(n_tails, tails_per_big)
    tail_start = n_big * BIG_CHUNKS

    def big_units(i):
        return [(g, jnp.minimum(BIG_CHUNKS * i + j, n_total - 1))
                for g in range(N_KV) for j in range(BIG_CHUNKS)]

    def tail_units(size):
        return [(g, jnp.minimum(tail_start + j, n_total - 1))
                for g in range(N_KV) for j in range(size * TAIL_CHUNKS)]

    short = [s for s in range(1, tails_per_big) if s * TAIL_CHUNKS < QK_AHEAD]
    tail_options = [(tail == s, tail_units(s)) for s in short] + [(None, tail_units(tails_per_big - 1))]

    def pick_units(options):
        pad = lambda units: (units + [units[-1]] * QK_AHEAD)[:QK_AHEAD]
        units = pad(options[-1][1])
        for cond, cand in reversed(options[:-1]):
            units = [(jnp.where(cond, g_a, g_b), jnp.where(cond, c_a, c_b))
                     for (g_a, c_a), (g_b, c_b) in zip(pad(cand), units)]
        return [(g, jnp.minimum(c, n_total - 1)) for g, c in units]

    def prefetch_scores(units):
        for k in range(QK_AHEAD):
            s_scr[k] = chunk_qk(*units[k])

    def before_loops(rows):
        pieces = []
        for g in range(N_KV):
            for half in range(GQ // LANES):
                a = fin_scr[g, :, half * LANES:(half + 1) * LANES]
                stacked = jnp.concatenate([a, pltpu.roll(a, Q_BLOCK, 1)], axis=0)
                pieces.append(stacked.T[:Q_BLOCK, :])
        o_ref[0] = jnp.concatenate(pieces, axis=1).astype(BF16)

        zeros_q = jnp.zeros((HEAD_DIM, GQ), BF16)
        q_padded = [jnp.concatenate([q_ref[0, 0, 0], zeros_q], axis=0),
                    jnp.concatenate([zeros_q, q_ref[0, 0, 1]], axis=0)]
        for g in range(N_KV):
            qp_scr[g] = q_padded[g]

        nsel = min(n_sb, rows * CMP_STRIDE // SEL_BLOCK)
        lane = lax.broadcasted_iota(jnp.int32, (nsel, LANES), 1)
        cmp_c = n_cmp_rows - 4
        e0 = cmp_c - 4 * ci + jnp.where(par == 0, eb_shift[0], eb_shift[1])
        e0 = pl.multiple_of(e0, 8)
        o_cmp = []
        imp = []
        w0 = ci // 2 - (WIN_TILES - 1)
        win_tiles = [jnp.maximum(w0 + j, 0) for j in range(WIN_TILES)]
        s_cmp = jnp.dot(kc_ref[0, :rows, :], jnp.concatenate(q_padded, axis=1),
                        preferred_element_type=F32)

        def window_scores():
            return [[jnp.dot(kw_ref[0, pl.ds(pl.multiple_of(tj * V_TILE, V_TILE), V_TILE), :],
                             q_padded[g], preferred_element_type=F32).astype(BF16) for tj in win_tiles]
                    for g in range(N_KV)]

        window_first = rows <= 2 * CMP_ROW_STEP
        if window_first:
            s_win = window_scores()
        for g in range(N_KV):
            sc = s_cmp[:, g * GQ:(g + 1) * GQ] + eb_ref[par, g, pl.ds(e0, rows), :]
            m = jnp.max(sc, axis=0, keepdims=True)
            e = jnp.exp2(sc - m).astype(BF16)
            ov = jnp.dot(vct_ref[0, vsl(g), :rows], e, preferred_element_type=F32)
            inv = jnp.where(m > 0.5 * NEG, 1.0 / ov[HEAD_DIM:HEAD_DIM + 1, :], 0.0)
            o_cmp.append(ov[:HEAD_DIM, :] * inv)
            ir = jnp.dot(ovt_ref[:nsel, :rows], e, preferred_element_type=F32) * inv
            a = ir[:, :LANES] + ir[:, LANES:]
            imp.append(a + pltpu.roll(a, Q_BLOCK, 1))
        if not window_first:
            s_win = window_scores()

        prefetch_scores(pick_units([(n_big > 0, big_units(0))] + tail_options))

        jidx = lax.broadcasted_iota(jnp.int32, (nsel, LANES), 0)
        valid = jidx <= ci
        forced = (jidx == 0) | (jidx == ci) | (jidx == ci - 1)
        index_mask = (1 << max(1, (n_sb - 1).bit_length())) - 1
        imp_bits = lax.bitcast_convert_type(
            jnp.maximum(jnp.where(lane < Q_BLOCK, imp[0], imp[1]), SMALLEST_IMPORTANCE), jnp.int32)
        keyed = lax.bitcast_convert_type((imp_bits & ~index_mask) | (index_mask - jidx), F32)
        score = jnp.where(valid, jnp.where(forced, -2.0, keyed), -1.0)
        for _ in range(min(N_SELECT, nsel) - 3):
            score = jnp.where(score == jnp.max(score, axis=0, keepdims=True), -2.0, score)
        selneg = jnp.where((score < -1.5) & valid, 0.0, NEG)
        swapped = pltpu.roll(selneg, Q_BLOCK, 1)
        left = jnp.where(lane < Q_BLOCK, selneg, swapped)
        right = jnp.where(lane < Q_BLOCK, swapped, selneg)
        sel_scr[0, :nsel] = jnp.concatenate([left, left], axis=1)
        sel_scr[1, :nsel] = jnp.concatenate([right, right], axis=1)

        for g in range(N_KV):
            parts = []
            for i in range(2 * WIN_TILES):
                delta = par + WIN_BLOCKS - i
                ok = (delta >= 0) & (delta <= WIN_BLOCKS) & (delta <= ci)
                tile = jnp.where(ok, delta, WIN_BLOCKS + 1)
                half = s_win[g][i // 2][(i % 2) * SEL_BLOCK:(i % 2 + 1) * SEL_BLOCK, :]
                parts.append(half + wb_ref[g, tile])
            s = jnp.concatenate(parts, axis=0)
            m = jnp.max(s, axis=0, keepdims=True)
            pb = jnp.exp2(s - m)
            o_win = jnp.zeros((V_ROWS, GQ), F32)
            for j, tj in enumerate(win_tiles):
                o_win = o_win + jnp.dot(vwt_ref[0, tj, vsl(g), :], pb[j * V_TILE:(j + 1) * V_TILE, :],
                                        preferred_element_type=F32)
            w_scale = (gt_ref[0, 0, 2 * N_KV + g:2 * N_KV + g + 1, :]
                       * (1.0 / o_win[HEAD_DIM:HEAD_DIM + 1, :]))
            out_scr[g] = gt_ref[0, 0, g:g + 1, :] * o_cmp[g] + w_scale * o_win[:HEAD_DIM, :]
            acc_scr[g] = jnp.zeros((V_ROWS, GQ), F32)

    row_steps = [r for r in range(CMP_ROW_STEP, n_cmp_rows + 1, CMP_ROW_STEP)] or [n_cmp_rows]
    variant = jnp.minimum((4 * ci + 2) // CMP_ROW_STEP, len(row_steps) - 1)
    lax.switch(variant, [functools.partial(before_loops, r) for r in row_steps])

    def chunk_softmax(s, g, c, extra):
        parts = []
        for i in range(BLOCKS_PER_CHUNK):
            kb = c * BLOCKS_PER_CHUNK + i
            blk = s[i * SEL_BLOCK:(i + 1) * SEL_BLOCK, :]
            mrow = sel_scr[g, pl.ds(kb, 1), :]
            if extra is not None:
                mrow = mrow + extra
            tile = jnp.clip(ci - kb, 0, NEAR_TILES)
            parts.append(blk + sb_ref[g, tile] + mrow.astype(BF16))
        s = jnp.concatenate(parts, axis=0)
        m_c = jnp.max(s, axis=0, keepdims=True)
        return m_c.astype(F32), jnp.exp2(s - m_c)

    def chunk_pv(pb, g, c):
        return jnp.dot(vst_ref[0, c, vsl(g), :], pb, preferred_element_type=F32)

    def merge_step(units, extras, next_units, carry):
        parked = min(QK_AHEAD, len(units))
        scores = {k: s_scr[k] for k in range(parked)}
        pending = list(range(parked, len(units) + (0 if next_units is None else QK_AHEAD)))

        def issue_scores():
            k = pending.pop(0)
            if k < len(units):
                scores[k] = chunk_qk(*units[k])
            else:
                s_scr[k - len(units)] = chunk_qk(*next_units[k - len(units)])

        for _ in range(min(QK_DEPTH - QK_AHEAD, len(pending))):
            issue_scores()
        results = []
        for k, (g, c) in enumerate(units):
            m_c, pb = chunk_softmax(scores.pop(k), g, c, extras[k])
            if pending:
                issue_scores()
            results.append((m_c, chunk_pv(pb, g, c)))
        per_g = len(units) // N_KV
        new = []
        for g in range(N_KV):
            m_run = carry[g]
            acc = acc_scr[g]
            for first in range(g * per_g, (g + 1) * per_g, MERGE_CHUNKS):
                stats = results[first:min(first + MERGE_CHUNKS, (g + 1) * per_g)]
                m_new = m_run
                for m_c, _ in stats:
                    m_new = jnp.maximum(m_new, m_c)
                acc = jnp.exp2(m_run - m_new) * acc
                for m_c, pv in stats:
                    acc = acc + jnp.exp2(m_c - m_new) * pv
                m_run = m_new
            acc_scr[g] = acc
            new.append(m_run)
        return tuple(new)

    def big_step(i, carry):
        nxt = pick_units([(i + 1 < n_big, big_units(i + 1))] + tail_options)
        extras = [jnp.where(BIG_CHUNKS * i + j < n_chunks, 0.0, NEG)
                  if j > BIG_CHUNKS - TAIL_CHUNKS else None
                  for _ in range(N_KV) for j in range(BIG_CHUNKS)]
        return merge_step(big_units(i), extras, nxt, carry)

    carry = lax.fori_loop(0, n_big, big_step, (jnp.full((1, GQ), NEG, F32),) * N_KV)

    def tail_step(size):
        extras = [None if j == 0 else jnp.where(tail_start + j < n_chunks, 0.0, NEG)
                  for _ in range(N_KV) for j in range(size * TAIL_CHUNKS)]
        merge_step(tail_units(size), extras, None, carry)

    def run_tail(lo, hi):
        if hi - lo == 1:
            if lo > 0:
                tail_step(lo)
        else:
            mid = (lo + hi) // 2
            lax.cond(tail < mid, functools.partial(run_tail, lo, mid), functools.partial(run_tail, mid, hi))

    run_tail(0, tails_per_big)

    for g in range(N_KV):
        acc = acc_scr[g]
        scale = gt_ref[0, 0, N_KV + g:N_KV + g + 1, :] * (1.0 / acc[HEAD_DIM:HEAD_DIM + 1, :])
        fin_scr[g] = out_scr[g] + scale * acc[:HEAD_DIM, :]


def _attn_call(qt, kc, vct, ks, vst, kw, vwt, gt, ovt, ebank, sbank, wbank, eb_shift):
    b, nq = qt.shape[0], qt.shape[1]
    t = ks.shape[1]
    n_cmp_rows = kc.shape[1]
    n_sb = t // SEL_BLOCK
    per_b = lambda shp: pl.BlockSpec((1,) + shp[1:], lambda i, j: (i,) + (0,) * (len(shp) - 1))
    per_q = lambda shp: pl.BlockSpec(
        (1, 1) + shp[2:], lambda i, j: (i, jnp.minimum(j, nq - 1)) + (0,) * (len(shp) - 2))
    full = lambda shp: pl.BlockSpec(shp, lambda i, j: (0,) * len(shp))
    args = (qt, kc, vct, ks, vst, kw, vwt, gt, ovt, ebank, sbank, wbank)
    specs = [per_q(qt.shape), per_b(kc.shape), per_b(vct.shape), per_b(ks.shape), per_b(vst.shape),
             per_b(kw.shape), per_b(vwt.shape), per_q(gt.shape), full(ovt.shape),
             full(ebank.shape), full(sbank.shape), full(wbank.shape)]
    acc_like = pltpu.VMEM((N_KV, HEAD_DIM, GQ), F32)
    return pl.pallas_call(
        functools.partial(_attn_kernel, n_cmp_rows=n_cmp_rows, eb_shift=eb_shift),
        grid=(b, nq + 1),
        in_specs=specs,
        out_specs=pl.BlockSpec((1, Q_BLOCK, ATTN_W), lambda i, j: (i, jnp.maximum(j - 1, 0), 0)),
        out_shape=jax.ShapeDtypeStruct((b, t, ATTN_W), BF16),
        scratch_shapes=[pltpu.VMEM((N_KV, n_sb, GQ), F32),
                        pltpu.VMEM((N_KV, V_ROWS, GQ), F32),
                        acc_like,
                        pltpu.VMEM((N_KV, KV_W, GQ), BF16),
                        pltpu.VMEM((QK_AHEAD, KEY_CHUNK, GQ), BF16),
                        acc_like],
        compiler_params=_cparams(2),
        name="nsa_attention",
    )(*args)


def _ffn_kernel(x_ref, a_ref, c_ref, mod_ref, n2_ref, wo_ref, w1_ref, w2_ref, o_ref):
    aw = a_ref.shape[2]
    mix = jnp.dot(a_ref[0], wo_ref[0:aw, :], preferred_element_type=F32)
    mix = mix + jnp.dot(c_ref[0], wo_ref[aw:, :], preferred_element_type=F32)
    x1 = x_ref[0] + mod_ref[0, 2:3, :] * mix
    ms = jnp.mean(x1 * x1, axis=-1, keepdims=True)
    y = x1 * lax.rsqrt(ms + EPS) * n2_ref[...]
    h2 = (y * (1.0 + mod_ref[0, 4:5, :]) + mod_ref[0, 3:4, :]).astype(BF16)
    d_ff = w1_ref.shape[1]
    ff = jnp.zeros(x1.shape, F32)
    for j in range(d_ff // FF_CHUNK):
        a = jnp.dot(h2, w1_ref[:, j * FF_CHUNK:(j + 1) * FF_CHUNK], preferred_element_type=F32)
        a = jnp.maximum(a, 0.0)
        ff = ff + jnp.dot((a * a).astype(BF16), w2_ref[j * FF_CHUNK:(j + 1) * FF_CHUNK, :],
                          preferred_element_type=F32)
    o_ref[0] = x1 + mod_ref[0, 5:6, :] * ff


def _ffn_call(x, attn, conv, mod, norm2, w_out, w_ff1, w_ff2):
    b, t, d = x.shape
    tm = min(ROW_TILE, t)
    row_spec = lambda w: pl.BlockSpec((1, tm, w), lambda i, j: (i, j, 0))
    full = lambda shp: pl.BlockSpec(shp, lambda i, j: (0,) * len(shp),
                                    pipeline_mode=pl.Buffered(1))
    return pl.pallas_call(
        _ffn_kernel,
        grid=(b, t // tm),
        in_specs=[row_spec(d), row_spec(attn.shape[2]), row_spec(conv.shape[2]),
                  pl.BlockSpec((1, N_MOD, d), lambda i, j: (i, 0, 0)),
                  full((1, d)), full(w_out.shape), full(w_ff1.shape), full(w_ff2.shape)],
        out_specs=row_spec(d),
        out_shape=jax.ShapeDtypeStruct((b, t, d), F32),
        compiler_params=_cparams(2),
        name="outproj_mlp",
    )(x, attn, conv, mod, norm2, w_out, w_ff1, w_ff2)


def _block_diag_ones(n):
    idx = np.arange(n) // HEAD_DIM
    return jnp.asarray(idx[:, None] == idx[None, :], dtype=BF16)


def _pack_w_in(w_in):
    d = w_in.shape[0]
    conv_w = d - ATTN_W
    sizes = [ATTN_W] + [KV_W] * 6 + [N_BRANCH * N_HEADS] + [conv_w] * 3
    offs = np.concatenate([[0], np.cumsum(sizes)])
    part = lambda i: w_in[:, offs[i]:offs[i + 1]]
    q, kc, vc, ks, vs, kw, vw, g, cgate, bgate, u = (part(i) for i in range(11))
    w_nat = jnp.concatenate([kc, vc, ks, kw, cgate, bgate, u], axis=1).astype(BF16)
    gt = g.reshape(d, N_KV, GQA, N_BRANCH).transpose(2, 3, 1, 0).reshape(GQA, N_BRANCH * N_KV, d)
    gt = jnp.pad(gt, ((0, 0), (0, SUBLANES - N_BRANCH * N_KV), (0, 0))).reshape(_G_ROWS, d)
    w_tr = jnp.concatenate([q.T, vs.T, vw.T, gt], axis=0).astype(BF16)
    return w_nat, w_tr


def _expand_w1(w1):
    hid = w1.shape[1]
    w = w1.reshape(2, CMP_STRIDE, HEAD_DIM, hid).astype(BF16)
    zero = jnp.zeros_like(w)
    per_group = [jnp.concatenate([w if k == g else zero for k in range(N_KV)], axis=-1)
                 for g in range(N_KV)]
    return jnp.stack(per_group, axis=2).reshape(2, CMP_STRIDE * KV_W, N_KV * hid)


def _expand_w2(w2):
    hid = w2.shape[0]
    eye = jnp.eye(N_KV, dtype=w2.dtype).reshape(N_KV, 1, N_KV, 1)
    return (w2.reshape(1, hid, 1, HEAD_DIM) * eye).reshape(N_KV * hid, KV_W).astype(BF16)


def _expand_pe(pe):
    p = pe.reshape(2, CMP_STRIDE, 1, HEAD_DIM)
    return jnp.broadcast_to(p, (2, CMP_STRIDE, N_KV, HEAD_DIM)).reshape(2, CMP_STRIDE * KV_W)


def _bucket_thresholds():
    n = np.arange(2 * REL_MAX_DIST)
    max_exact = REL_BUCKETS // 2
    nf = np.maximum(n, max_exact).astype(np.float32)
    ratio = np.log(nf / np.float32(max_exact)) / np.float32(math.log(REL_MAX_DIST / max_exact))
    large = max_exact + (ratio * np.float32(REL_BUCKETS - max_exact)).astype(np.int32)
    table = np.where(n < max_exact, n, np.minimum(large, REL_BUCKETS - 1))
    return tuple(int(np.searchsorted(table, k, side="left")) for k in range(REL_BUCKETS))


def _bank_call(bias_rows, lead, n_tiles, dist_fn, name):
    nl = len(lead)
    thr = _bucket_thresholds()

    def body(rows_ref, o_ref):
        lead_ids = [pl.program_id(a) for a in range(nl)]
        row = lax.broadcasted_iota(jnp.int32, (SEL_BLOCK, GQ), 0)
        qi = lax.broadcasted_iota(jnp.int32, (SEL_BLOCK, GQ), 1) & (Q_BLOCK - 1)

        def tile(t, carry):
            dist, ok = dist_fn(lead_ids, t, row, qi)
            v = jnp.broadcast_to(rows_ref[0, 0:1, :], (SEL_BLOCK, GQ))
            for k in range(1, REL_BUCKETS):
                v = jnp.where(dist >= thr[k], rows_ref[0, k:k + 1, :], v)
            o_ref[(0,) * (nl + 1) + (t,)] = jnp.where(ok, v, NEG)
            return carry

        lax.fori_loop(0, n_tiles, tile, 0)

    return pl.pallas_call(
        body,
        grid=tuple(lead) + (N_KV,),
        in_specs=[pl.BlockSpec((1, REL_BUCKETS, GQ), lambda *i: (i[nl], 0, 0))],
        out_specs=pl.BlockSpec((1,) * (nl + 1) + (n_tiles, SEL_BLOCK, GQ),
                               lambda *i: tuple(i) + (0, 0, 0)),
        out_shape=jax.ShapeDtypeStruct(tuple(lead) + (N_KV, n_tiles, SEL_BLOCK, GQ), F32),
        compiler_params=_cparams(nl + 1),
        name=name,
    )(bias_rows)


def _bias_banks(rel_bias, t):
    n_cmp_rows = t // CMP_STRIDE
    rows = rel_bias.reshape(REL_BUCKETS, N_KV, GQA).transpose(1, 0, 2)
    rows = jnp.repeat(rows, Q_BLOCK, axis=2) * LOG2E

    def sel_dist(lead, tile, row, qi):
        dist = SEL_BLOCK * tile + qi - row
        return dist, dist >= 0

    sbank = _bank_call(rows, (), NEAR_TILES + 1, sel_dist, "bias_bank_sel")

    def win_dist(lead, tile, row, qi):
        dist = SEL_BLOCK * tile + qi - row
        return dist, (dist >= 0) & (dist < WINDOW)

    wbank = _bank_call(rows, (), WIN_BLOCKS + 2, win_dist, "bias_bank_win")

    cmp_c = n_cmp_rows - 4
    shifts = tuple(int((-(cmp_c - 4 * p)) % 8) for p in range(2))
    n_tiles = (cmp_c + n_cmp_rows + 8 + SEL_BLOCK - 1) // SEL_BLOCK

    def cmp_dist(lead, tile, row, qi):
        e = SEL_BLOCK * tile + row - jnp.where(lead[0] == 0, shifts[0], shifts[1])
        dist = qi - CMP_STRIDE * e + (CMP_STRIDE * cmp_c - (CMP_BLOCK - 1))
        return dist, (dist >= 0) & (e >= 0)

    ebank = _bank_call(rows, (2,), n_tiles, cmp_dist, "bias_bank_cmp")
    ebank = ebank.reshape(2, N_KV, n_tiles * SEL_BLOCK, GQ)
    return ebank, sbank, wbank, shifts


def _overlap_t(t):
    n_cmp_rows = t // CMP_STRIDE
    n_sb = t // SEL_BLOCK
    c_start = np.arange(n_cmp_rows)[None, :] * CMP_STRIDE
    s_start = np.arange(n_sb)[:, None] * SEL_BLOCK
    ov = np.clip(np.minimum(c_start + CMP_BLOCK, s_start + SEL_BLOCK)
                 - np.maximum(c_start, s_start), 0, None) / CMP_BLOCK
    ov[:, n_cmp_rows - 1] = 0.0
    return jnp.asarray(ov, dtype=BF16)


def _layer(x, c_pad, w_in, q_norm, k_norm, cmp_pe_k, cmp_w1_k, cmp_w2_k, cmp_pe_v, cmp_w1_v,
           cmp_w2_v, rel_bias, conv_w, w_out, norm1, norm2, w_ada, b_ada, w_ff1, w_ff2):
    b, t, d = x.shape
    scale = HEAD_DIM ** -0.5

    mod = _mod_call(c_pad, w_ada, b_ada)[:b].reshape(b, N_MOD, d)

    qn_col = (jnp.tile(q_norm, N_HEADS) * (scale * LOG2E)).reshape(ATTN_W, 1)
    kn_t = jnp.tile(k_norm, N_KV).reshape(1, KV_W)
    bdq = _block_diag_ones(ATTN_W)
    bdk = _block_diag_ones(KV_W)
    w_nat, w_tr = _pack_w_in(w_in)
    qt, kc_raw, vc_raw, ks, vst, kw, vwt, gt, conv = _inproj_call(
        x, mod, norm1.reshape(1, d), w_nat, w_tr, qn_col, kn_t, conv_w, bdq, bdk)

    kc, vct = _compress_call(
        kc_raw, vc_raw, _expand_pe(cmp_pe_k), _expand_pe(cmp_pe_v), _expand_w1(cmp_w1_k), _expand_w1(cmp_w1_v),
        _expand_w2(cmp_w2_k), _expand_w2(cmp_w2_v).T, kn_t, bdk)

    ebank, sbank, wbank, eb_shift = _bias_banks(rel_bias, t)
    attn = _attn_call(qt, kc, vct, ks, vst, kw, vwt, gt, _overlap_t(t), ebank, sbank.astype(BF16),
                      wbank.astype(BF16), eb_shift)

    return _ffn_call(x, attn, conv, mod, norm2.reshape(1, d), w_out.astype(BF16),
                     w_ff1.astype(BF16), w_ff2.astype(BF16))


def kernel(x, c, w_in, q_norm, k_norm, cmp_pe_k, cmp_w1_k, cmp_w2_k, cmp_pe_v, cmp_w1_v, cmp_w2_v,
           rel_bias, conv_w, w_out, norm1, norm2, w_ada, b_ada, w_ff1, w_ff2):
    b = x.shape[0]
    c_pad = jnp.pad(c, ((0, (-b) % 8), (0, 0)))
    for l in range(w_in.shape[0]):
        x = _layer(x, c_pad, w_in[l], q_norm[l], k_norm[l], cmp_pe_k[l], cmp_w1_k[l], cmp_w2_k[l],
                   cmp_pe_v[l], cmp_w1_v[l], cmp_w2_v[l], rel_bias, conv_w[l], w_out[l],
                   norm1[l], norm2[l], w_ada[l], b_ada[l], w_ff1[l], w_ff2[l])
    return x
```

```python
import functools
import math

import numpy as np
import jax
import jax.numpy as jnp
from jax import lax
from jax.experimental import pallas as pl
from jax.experimental.pallas import tpu as pltpu

HEAD_DIM = 64
N_HEADS = 8
N_KV = 2
GQA = N_HEADS // N_KV
ATTN_W = N_HEADS * HEAD_DIM
KV_W = N_KV * HEAD_DIM
CONV_K = 3
CMP_BLOCK = 32
CMP_STRIDE = 16
SEL_BLOCK = 64
N_SELECT = 16
WINDOW = 512
Q_BLOCK = 64
REL_BUCKETS = 32
REL_MAX_DIST = 1024
N_MOD = 6
N_BRANCH = 3
EPS = 1e-6
NEG = -1e30

LANES = 128
SUBLANES = 8
GQ = GQA * Q_BLOCK
KEY_CHUNK = 256
BLOCKS_PER_CHUNK = KEY_CHUNK // SEL_BLOCK
V_TILE = 128
BF16_ROWS = 16
V_ROWS = HEAD_DIM + BF16_ROWS
V_TILE_ROWS = N_KV * V_ROWS
CMP_ROW_STEP = 128
WIN_BLOCKS = WINDOW // SEL_BLOCK
WIN_TILES = WINDOW // V_TILE + 1
NEAR_TILES = (REL_MAX_DIST + Q_BLOCK - 1) // SEL_BLOCK + 1
ROW_TILE = 512
INPROJ_TILE = 1024
INPROJ_SPLIT = 4
BIG_CHUNKS = 16
TAIL_CHUNKS = 1
assert BIG_CHUNKS % TAIL_CHUNKS == 0
MERGE_CHUNKS = 2
QK_AHEAD = 4
QK_DEPTH = 5
SMALLEST_IMPORTANCE = 2.0 ** -100
LOG2E = math.log2(math.e)
FF_CHUNK = 1024
VMEM_LIMIT = 56 * 1024 * 1024

F32 = jnp.float32
BF16 = jnp.bfloat16
_NT = (((1,), (1,)), ((), ()))


def _cparams(n_axes):
    return pltpu.CompilerParams(dimension_semantics=("arbitrary",) * n_axes,
                                vmem_limit_bytes=VMEM_LIMIT)


def _with_ones_rows(vt):
    ones = jnp.ones((BF16_ROWS, vt.shape[1]), vt.dtype)
    parts = []
    for g in range(N_KV):
        parts += [vt[g * HEAD_DIM:(g + 1) * HEAD_DIM, :], ones]
    return jnp.concatenate(parts, axis=0)


def _swap_halves(p0, p1):
    low = lax.broadcasted_iota(jnp.int32, p0.shape, 1) < LANES // 2
    return (jnp.where(low, p0, pltpu.roll(p1, LANES // 2, 1)),
            jnp.where(low, pltpu.roll(p0, LANES // 2, 1), p1))


def _mod_kernel(c_ref, w_ref, b_ref, o_ref):
    c = c_ref[...]
    a = c * jax.nn.sigmoid(c)
    o_ref[...] = jnp.dot(a, w_ref[...], preferred_element_type=F32,
                         precision=lax.Precision.HIGHEST) + b_ref[...]


def _mod_call(c_pad, w_ada, b_ada):
    rows, d = c_pad.shape
    n = w_ada.shape[1]
    tn = n // N_MOD
    return pl.pallas_call(
        _mod_kernel,
        grid=(n // tn,),
        in_specs=[pl.BlockSpec((rows, d), lambda j: (0, 0)),
                  pl.BlockSpec((d, tn), lambda j: (0, j)),
                  pl.BlockSpec((1, tn), lambda j: (0, j))],
        out_specs=pl.BlockSpec((rows, tn), lambda j: (0, j)),
        out_shape=jax.ShapeDtypeStruct((rows, n), F32),
        compiler_params=_cparams(1),
        name="adaln_mod",
    )(c_pad, w_ada, b_ada.reshape(1, n))


_N_KC, _N_VC, _N_KS, _N_KW, _N_CONV = 0, KV_W, 2 * KV_W, 3 * KV_W, 4 * KV_W
_T_Q, _T_VS, _T_VW, _T_G = 0, ATTN_W, ATTN_W + KV_W, ATTN_W + 2 * KV_W
_G_ROWS = GQA * SUBLANES


def _inproj_kernel(x_ref, mod_ref, n1_ref, wn_ref, wt_ref, qn_ref, kn_ref, cw_ref, bdq_ref, bdk_ref,
                   qt_out, kc_out, vc_out, ks_out, vst_out, kw_out, vwt_out, gt_out, conv_out,
                   carry, *, conv_w):
    t = pl.program_id(1)
    tm = x_ref.shape[1]
    sub = tm // INPROJ_SPLIT

    @pl.when(t == 0)
    def _():
        carry[...] = jnp.zeros_like(carry)

    prev2, prev1 = carry[6:7, :], carry[7:8, :]
    c_bg = _N_CONV + conv_w
    c_u = c_bg + conv_w

    for part in range(INPROJ_SPLIT):
        rows = slice(part * sub, (part + 1) * sub)
        x = x_ref[0, rows, :]
        ms = jnp.mean(x * x, axis=-1, keepdims=True)
        y = x * lax.rsqrt(ms + EPS) * n1_ref[...]
        h = (y * (1.0 + mod_ref[0, 1:2, :]) + mod_ref[0, 0:1, :]).astype(BF16)

        def proj(a, b):
            return jnp.dot(h, wn_ref[:, a:b], preferred_element_type=F32)

        def proj_t(a, b):
            return lax.dot_general(wt_ref[a:b, :], h, _NT, preferred_element_type=F32)

        def head_norm(v, gain):
            ssq = jnp.dot((v * v).astype(BF16), bdk_ref[...], preferred_element_type=F32)
            return v * lax.rsqrt(ssq * (1.0 / HEAD_DIM) + EPS) * gain

        qf = proj_t(_T_Q, _T_VS)
        vg = proj_t(_T_VS, _T_G + _G_ROWS)
        kv = proj(_N_KC, _N_CONV)
        cv = proj(_N_CONV, c_u + conv_w)

        ssq = jnp.dot(bdq_ref[...], (qf * qf).astype(BF16), preferred_element_type=F32)
        qf = qf * lax.rsqrt(ssq * (1.0 / HEAD_DIM) + EPS) * qn_ref[...]
        gf = jax.nn.sigmoid(vg[2 * KV_W:, :])
        for c in range(sub // LANES):
            blk = (part * sub) // Q_BLOCK + 2 * c
            cols = slice(c * LANES, (c + 1) * LANES)
            for g in range(N_KV):
                pc = [qf[(g * GQA + r) * HEAD_DIM:(g * GQA + r + 1) * HEAD_DIM, cols]
                      for r in range(GQA)]
                lo01, hi01 = _swap_halves(pc[0], pc[1])
                lo23, hi23 = _swap_halves(pc[2], pc[3])
                qt_out[0, blk, g] = jnp.concatenate([lo01, lo23], axis=1).astype(BF16)
                qt_out[0, blk + 1, g] = jnp.concatenate([hi01, hi23], axis=1).astype(BF16)
            pc = [gf[r * SUBLANES:(r + 1) * SUBLANES, cols] for r in range(GQA)]
            lo01, hi01 = _swap_halves(pc[0], pc[1])
            lo23, hi23 = _swap_halves(pc[2], pc[3])
            gt_out[0, blk] = jnp.concatenate([lo01, lo23], axis=1)
            gt_out[0, blk + 1] = jnp.concatenate([hi01, hi23], axis=1)

        vs_f = _with_ones_rows(vg[:KV_W, :].astype(BF16))
        vw_f = _with_ones_rows(vg[KV_W:2 * KV_W, :].astype(BF16))
        for j in range(sub // KEY_CHUNK):
            vst_out[0, (part * sub) // KEY_CHUNK + j] = vs_f[:, j * KEY_CHUNK:(j + 1) * KEY_CHUNK]
        for j in range(sub // V_TILE):
            vwt_out[0, (part * sub) // V_TILE + j] = vw_f[:, j * V_TILE:(j + 1) * V_TILE]

        kc_out[0, rows, :] = kv[:, _N_KC:_N_VC]
        vc_out[0, rows, :] = kv[:, _N_VC:_N_KS]
        ks_out[0, rows, :] = head_norm(kv[:, _N_KS:_N_KW], kn_ref[...]).astype(BF16)
        kw_out[0, rows, :] = head_norm(kv[:, _N_KW:_N_CONV], kn_ref[...]).astype(BF16)

        z = cv[:, :conv_w] * cv[:, 2 * conv_w:]
        row = lax.broadcasted_iota(jnp.int32, z.shape, 0)
        z1 = jnp.where(row == 0, prev1, pltpu.roll(z, 1, 0))
        z2 = jnp.where(row == 0, prev2, jnp.where(row == 1, prev1, pltpu.roll(z, 2, 0)))
        zc = cw_ref[0:1, :] * z2 + cw_ref[1:2, :] * z1 + cw_ref[2:3, :] * z
        conv_out[0, rows, :] = (cv[:, conv_w:2 * conv_w] * zc).astype(BF16)
        prev2, prev1 = z[sub - 2:sub - 1, :], z[sub - 1:sub, :]
        if part == INPROJ_SPLIT - 1:
            carry[...] = z[sub - SUBLANES:sub, :]


def _inproj_call(x, mod, norm1, w_nat, w_tr, qn_col, kn_t, conv_w, bdq, bdk):
    b, t, d = x.shape
    tm = min(INPROJ_TILE, t)
    cw = conv_w.shape[1]
    nq = t // Q_BLOCK
    row_spec = lambda w: pl.BlockSpec((1, tm, w), lambda i, j: (i, j, 0))
    full = lambda shp: pl.BlockSpec(shp, lambda i, j: (0,) * len(shp))
    vt_spec = lambda w: pl.BlockSpec((1, tm // w, V_TILE_ROWS, w), lambda i, j: (i, j, 0, 0))
    vt_shape = lambda w: jax.ShapeDtypeStruct((b, t // w, V_TILE_ROWS, w), BF16)
    kv = lambda dt: jax.ShapeDtypeStruct((b, t, KV_W), dt)
    out_specs = [pl.BlockSpec((1, tm // Q_BLOCK, N_KV, HEAD_DIM, GQ), lambda i, j: (i, j, 0, 0, 0)),
                 row_spec(KV_W), row_spec(KV_W), row_spec(KV_W), vt_spec(KEY_CHUNK), row_spec(KV_W),
                 vt_spec(V_TILE),
                 pl.BlockSpec((1, tm // Q_BLOCK, SUBLANES, GQ), lambda i, j: (i, j, 0, 0)),
                 row_spec(cw)]
    out_shape = [jax.ShapeDtypeStruct((b, nq, N_KV, HEAD_DIM, GQ), BF16),
                 kv(F32), kv(F32), kv(BF16), vt_shape(KEY_CHUNK), kv(BF16), vt_shape(V_TILE),
                 jax.ShapeDtypeStruct((b, nq, SUBLANES, GQ), F32),
                 jax.ShapeDtypeStruct((b, t, cw), BF16)]
    return pl.pallas_call(
        functools.partial(_inproj_kernel, conv_w=cw),
        grid=(b, t // tm),
        in_specs=[row_spec(d),
                  pl.BlockSpec((1, N_MOD, d), lambda i, j: (i, 0, 0)),
                  full((1, d)), full(w_nat.shape), full(w_tr.shape), full((ATTN_W, 1)),
                  full((1, KV_W)), full((CONV_K, cw)), full((ATTN_W, ATTN_W)), full((KV_W, KV_W))],
        out_specs=out_specs,
        out_shape=out_shape,
        scratch_shapes=[pltpu.VMEM((SUBLANES, cw), F32)],
        compiler_params=_cparams(2),
        name="inproj",
    )(x, mod, norm1, w_nat, w_tr, qn_col, kn_t, conv_w, bdq, bdk)


def _compress_kernel(kx_ref, vx_ref, pek_ref, pev_ref, w1k_ref, w1v_ref, w2k_ref, w2vt_ref,
                     kn_ref, bdk_ref, kc_out, vct_out):
    def hidden(x_ref, pe_ref, w1_ref):
        n = x_ref.shape[1] // CMP_STRIDE
        u = jnp.zeros((n, w1_ref.shape[2]), F32)
        v = jnp.zeros((n, w1_ref.shape[2]), F32)
        for r in range(0, CMP_STRIDE, 2):
            tok = [x_ref[0, pl.ds(r + d, n, stride=CMP_STRIDE), :] for d in range(2)]
            cols = slice(r * KV_W, (r + 2) * KV_W)
            for a, acc in ((0, "u"), (1, "v")):
                lhs = jnp.concatenate([tok[d] + pe_ref[a:a + 1, (r + d) * KV_W:(r + d + 1) * KV_W]
                                       for d in range(2)], axis=1).astype(BF16)
                prod = jnp.dot(lhs, w1_ref[a, cols, :], preferred_element_type=F32)
                if acc == "u":
                    u = u + prod
                else:
                    v = v + prod
        hid = u + pltpu.roll(v, n - 1, 0)
        return jax.nn.gelu(hid, approximate=True).astype(BF16)

    kc = jnp.dot(hidden(kx_ref, pek_ref, w1k_ref), w2k_ref[...], preferred_element_type=F32)
    ssq = jnp.dot((kc * kc).astype(BF16), bdk_ref[...], preferred_element_type=F32)
    kc_out[0] = (kc * lax.rsqrt(ssq * (1.0 / HEAD_DIM) + EPS) * kn_ref[...]).astype(BF16)
    vct_out[0] = _with_ones_rows(lax.dot_general(w2vt_ref[...], hidden(vx_ref, pev_ref, w1v_ref), _NT,
                                                 preferred_element_type=F32).astype(BF16))


def _compress_call(kx, vx, pek, pev, w1k, w1v, w2k, w2vt, kn_t, bdk):
    b, t, _ = kx.shape
    nrow = t // CMP_STRIDE
    wide, hid2 = w1k.shape[1], w1k.shape[2]
    full = lambda shp: pl.BlockSpec(shp, lambda i: (0,) * len(shp))
    xs = pl.BlockSpec((1, t, KV_W), lambda i: (i, 0, 0))
    return pl.pallas_call(
        _compress_kernel,
        grid=(b,),
        in_specs=[xs, xs, full((2, wide)), full((2, wide)), full((2, wide, hid2)),
                  full((2, wide, hid2)), full((hid2, KV_W)), full((KV_W, hid2)),
                  full((1, KV_W)), full((KV_W, KV_W))],
        out_specs=[pl.BlockSpec((1, nrow, KV_W), lambda i: (i, 0, 0)),
                   pl.BlockSpec((1, V_TILE_ROWS, nrow), lambda i: (i, 0, 0))],
        out_shape=[jax.ShapeDtypeStruct((b, nrow, KV_W), BF16),
                   jax.ShapeDtypeStruct((b, V_TILE_ROWS, nrow), BF16)],
        compiler_params=_cparams(1),
        name="compress",
    )(kx, vx, pek, pev, w1k, w1v, w2k, w2vt, kn_t, bdk)


def _attn_kernel(q_ref, kc_ref, vct_ref, ks_ref, vst_ref, kw_ref, vwt_ref, gt_ref, ovt_ref,
                 eb_ref, sb_ref, wb_ref, o_ref, sel_scr, acc_scr, out_scr, qp_scr, s_scr,
                 fin_scr, *, n_cmp_rows, eb_shift):
    n_sb = sel_scr.shape[1]
    write_out_only = pl.program_id(1) >= n_sb
    ci = jnp.minimum(pl.program_id(1), n_sb - 1)
    par = lax.rem(ci, 2)
    n_total = ks_ref.shape[1] // KEY_CHUNK
    vsl = lambda g: slice(g * V_ROWS, (g + 1) * V_ROWS)

    @pl.when((pl.program_id(0) == 0) & (pl.program_id(1) == 0))
    def _():
        def zero(g, carry):
            fin_scr[g] = jnp.zeros(fin_scr.shape[1:], F32)
            sel_scr[g] = jnp.zeros(sel_scr.shape[1:], F32)
            return carry
        lax.fori_loop(0, N_KV, zero, 0)

    def chunk_qk(g, c):
        kk = ks_ref[0, pl.ds(pl.multiple_of(c * KEY_CHUNK, KEY_CHUNK), KEY_CHUNK), :]
        return jnp.dot(kk, qp_scr[g], preferred_element_type=F32).astype(BF16)

    n_chunks = jnp.where(write_out_only, 0, ci // BLOCKS_PER_CHUNK + 1)
    tails_per_big = BIG_CHUNKS // TAIL_CHUNKS
    n_tails = (n_chunks + TAIL_CHUNKS - 1) // TAIL_CHUNKS
    n_big = n_tails // tails_per_big
    tail = lax.rem(n_tails, tails_per_big)
    tail_start = n_big * BIG_CHUNKS

    def big_units(i):
        return [(g, jnp.minimum(BIG_CHUNKS * i + j, n_total - 1))
                for g in range(N_KV) for j in range(BIG_CHUNKS)]

    def tail_units(size):
        return [(g, jnp.minimum(tail_start + j, n_total - 1))
                for g in range(N_KV) for j in range(size * TAIL_CHUNKS)]

    short = [s for s in range(1, tails_per_big) if s * TAIL_CHUNKS < QK_AHEAD]
    tail_options = [(tail == s, tail_units(s)) for s in short] + [(None, tail_units(tails_per_big - 1))]

    def pick_units(options):
        pad = lambda units: (units + [units[-1]] * QK_AHEAD)[:QK_AHEAD]
        units = pad(options[-1][1])
        for cond, cand in reversed(options[:-1]):
            units = [(jnp.where(cond, g_a, g_b), jnp.where(cond, c_a, c_b))
                     for (g_a, c_a), (g_b, c_b) in zip(pad(cand), units)]
        return [(g, jnp.minimum(c, n_total - 1)) for g, c in units]

    def prefetch_scores(units):
        for k in range(QK_AHEAD):
            s_scr[k] = chunk_qk(*units[k])

    def write_out():
        pieces = []
        for g in range(N_KV):
            for half in range(GQ // LANES):
                a = fin_scr[g, :, half * LANES:(half + 1) * LANES]
                stacked = jnp.concatenate([a, pltpu.roll(a, Q_BLOCK, 1)], axis=0)
                pieces.append(stacked.T[:Q_BLOCK, :])
        o_ref[0] = jnp.concatenate(pieces, axis=1).astype(BF16)

    def before_loops(rows):
        write_out()

        zeros_q = jnp.zeros((HEAD_DIM, GQ), BF16)
        q_padded = [jnp.concatenate([q_ref[0, 0, 0], zeros_q], axis=0),
                    jnp.concatenate([zeros_q, q_ref[0, 0, 1]], axis=0)]
        for g in range(N_KV):
            qp_scr[g] = q_padded[g]

        nsel = min(n_sb, rows * CMP_STRIDE // SEL_BLOCK)
        lane = lax.broadcasted_iota(jnp.int32, (nsel, LANES), 1)
        cmp_c = n_cmp_rows - 4
        e0 = cmp_c - 4 * ci + jnp.where(par == 0, eb_shift[0], eb_shift[1])
        e0 = pl.multiple_of(e0, 8)
        o_cmp = []
        imp = []
        w0 = ci // 2 - (WIN_TILES - 1)
        win_tiles = [jnp.maximum(w0 + j, 0) for j in range(WIN_TILES)]
        s_cmp = jnp.dot(kc_ref[0, :rows, :], jnp.concatenate(q_padded, axis=1),
                        preferred_element_type=F32)

        def window_scores():
            return [[jnp.dot(kw_ref[0, pl.ds(pl.multiple_of(tj * V_TILE, V_TILE), V_TILE), :],
                             q_padded[g], preferred_element_type=F32).astype(BF16) for tj in win_tiles]
                    for g in range(N_KV)]

        window_first = 2 * rows <= n_cmp_rows
        if window_first:
            s_win = window_scores()
        for g in range(N_KV):
            sc = s_cmp[:, g * GQ:(g + 1) * GQ] + eb_ref[par, g, pl.ds(e0, rows), :]
            m = jnp.max(sc, axis=0, keepdims=True)
            e = jnp.exp2(sc - m).astype(BF16)
            ov = jnp.dot(vct_ref[0, vsl(g), :rows], e, preferred_element_type=F32)
            inv = jnp.where(m > 0.5 * NEG, 1.0 / ov[HEAD_DIM:HEAD_DIM + 1, :], 0.0)
            o_cmp.append(ov[:HEAD_DIM, :] * inv)
            ir = jnp.dot(ovt_ref[:nsel, :rows], e, preferred_element_type=F32) * inv
            a = ir[:, :LANES] + ir[:, LANES:]
            imp.append(a + pltpu.roll(a, Q_BLOCK, 1))
        if not window_first:
            s_win = window_scores()

        prefetch_scores(pick_units([(n_big > 0, big_units(0))] + tail_options))

        jidx = lax.broadcasted_iota(jnp.int32, (nsel, LANES), 0)
        valid = jidx <= ci
        forced = (jidx == 0) | (jidx == ci) | (jidx == ci - 1)
        index_mask = (1 << max(1, (n_sb - 1).bit_length())) - 1
        imp_bits = lax.bitcast_convert_type(
            jnp.maximum(jnp.where(lane < Q_BLOCK, imp[0], imp[1]), SMALLEST_IMPORTANCE), jnp.int32)
        keyed = lax.bitcast_convert_type((imp_bits & ~index_mask) | (index_mask - jidx), F32)
        score = jnp.where(valid, jnp.where(forced, -2.0, keyed), -1.0)
        for _ in range(min(N_SELECT, nsel) - 3):
            score = jnp.where(score == jnp.max(score, axis=0, keepdims=True), -2.0, score)
        selneg = jnp.where((score < -1.5) & valid, 0.0, NEG)
        swapped = pltpu.roll(selneg, Q_BLOCK, 1)
        left = jnp.where(lane < Q_BLOCK, selneg, swapped)
        right = jnp.where(lane < Q_BLOCK, swapped, selneg)
        sel_scr[0, :nsel] = jnp.concatenate([left, left], axis=1)
        sel_scr[1, :nsel] = jnp.concatenate([right, right], axis=1)

        for g in range(N_KV):
            parts = []
            for i in range(2 * WIN_TILES):
                delta = par + WIN_BLOCKS - i
                ok = (delta >= 0) & (delta <= WIN_BLOCKS) & (delta <= ci)
                tile = jnp.where(ok, delta, WIN_BLOCKS + 1)
                half = s_win[g][i // 2][(i % 2) * SEL_BLOCK:(i % 2 + 1) * SEL_BLOCK, :]
                parts.append(half + wb_ref[g, tile])
            s = jnp.concatenate(parts, axis=0)
            m = jnp.max(s, axis=0, keepdims=True)
            pb = jnp.exp2(s - m)
            o_win = jnp.zeros((V_ROWS, GQ), F32)
            for j, tj in enumerate(win_tiles):
                o_win = o_win + jnp.dot(vwt_ref[0, tj, vsl(g), :], pb[j * V_TILE:(j + 1) * V_TILE, :],
                                        preferred_element_type=F32)
            w_scale = (gt_ref[0, 0, 2 * N_KV + g:2 * N_KV + g + 1, :]
                       * (1.0 / o_win[HEAD_DIM:HEAD_DIM + 1, :]))
            out_scr[g] = gt_ref[0, 0, g:g + 1, :] * o_cmp[g] + w_scale * o_win[:HEAD_DIM, :]
            acc_scr[g] = jnp.zeros((V_ROWS, GQ), F32)

    row_steps = [r for r in range(CMP_ROW_STEP, n_cmp_rows + 1, CMP_ROW_STEP)] or [n_cmp_rows]
    variant = jnp.minimum((4 * ci + 2) // CMP_ROW_STEP, len(row_steps) - 1)
    variant = jnp.where(write_out_only, len(row_steps), variant)
    lax.switch(variant, [functools.partial(before_loops, r) for r in row_steps] + [write_out])

    def chunk_softmax(s, g, c, extra):
        parts = []
        for i in range(BLOCKS_PER_CHUNK):
            kb = c * BLOCKS_PER_CHUNK + i
            blk = s[i * SEL_BLOCK:(i + 1) * SEL_BLOCK, :]
            mrow = sel_scr[g, pl.ds(kb, 1), :]
            if extra is not None:
                mrow = mrow + extra
            tile = jnp.clip(ci - kb, 0, NEAR_TILES)
            parts.append(blk + sb_ref[g, tile] + mrow.astype(BF16))
        s = jnp.concatenate(parts, axis=0)
        m_c = jnp.max(s, axis=0, keepdims=True)
        return m_c.astype(F32), jnp.exp2(s - m_c)

    def chunk_pv(pb, g, c):
        return jnp.dot(vst_ref[0, c, vsl(g), :], pb, preferred_element_type=F32)

    def merge_step(units, extras, next_units, carry):
        parked = min(QK_AHEAD, len(units))
        scores = {k: s_scr[k] for k in range(parked)}
        pending = list(range(parked, len(units) + (0 if next_units is None else QK_AHEAD)))

        def issue_scores():
            k = pending.pop(0)
            if k < len(units):
                scores[k] = chunk_qk(*units[k])
            else:
                s_scr[k - len(units)] = chunk_qk(*next_units[k - len(units)])

        for _ in range(min(QK_DEPTH - QK_AHEAD, len(pending))):
            issue_scores()
        results = []
        for k, (g, c) in enumerate(units):
            m_c, pb = chunk_softmax(scores.pop(k), g, c, extras[k])
            if pending:
                issue_scores()
            results.append((m_c, chunk_pv(pb, g, c)))
        per_g = len(units) // N_KV
        new = []
        for g in range(N_KV):
            m_run = carry[g]
            acc = acc_scr[g]
            for first in range(g * per_g, (g + 1) * per_g, MERGE_CHUNKS):
                stats = results[first:min(first + MERGE_CHUNKS, (g + 1) * per_g)]
                m_new = m_run
                for m_c, _ in stats:
                    m_new = jnp.maximum(m_new, m_c)
                acc = jnp.exp2(m_run - m_new) * acc
                for m_c, pv in stats:
                    acc = acc + jnp.exp2(m_c - m_new) * pv
                m_run = m_new
            acc_scr[g] = acc
            new.append(m_run)
        return tuple(new)

    def big_step(i, carry):
        nxt = pick_units([(i + 1 < n_big, big_units(i + 1))] + tail_options)
        extras = [jnp.where(BIG_CHUNKS * i + j < n_chunks, 0.0, NEG)
                  if j > BIG_CHUNKS - TAIL_CHUNKS else None
                  for _ in range(N_KV) for j in range(BIG_CHUNKS)]
        return merge_step(big_units(i), extras, nxt, carry)

    carry = lax.fori_loop(0, n_big, big_step, (jnp.full((1, GQ), NEG, F32),) * N_KV)

    def tail_step(size):
        extras = [None if j == 0 else jnp.where(tail_start + j < n_chunks, 0.0, NEG)
                  for _ in range(N_KV) for j in range(size * TAIL_CHUNKS)]
        merge_step(tail_units(size), extras, None, carry)

    def run_tail(lo, hi):
        if hi - lo == 1:
            if lo > 0:
                tail_step(lo)
        else:
            mid = (lo + hi) // 2
            lax.cond(tail < mid, functools.partial(run_tail, lo, mid), functools.partial(run_tail, mid, hi))

    run_tail(0, tails_per_big)

    for g in range(N_KV):
        acc = acc_scr[g]
        scale = gt_ref[0, 0, N_KV + g:N_KV + g + 1, :] * (1.0 / acc[HEAD_DIM:HEAD_DIM + 1, :])
        fin_scr[g] = out_scr[g] + scale * acc[:HEAD_DIM, :]


def _attn_call(qt, kc, vct, ks, vst, kw, vwt, gt, ovt, ebank, sbank, wbank, eb_shift):
    b, nq = qt.shape[0], qt.shape[1]
    t = ks.shape[1]
    n_cmp_rows = kc.shape[1]
    n_sb = t // SEL_BLOCK
    per_b = lambda shp: pl.BlockSpec((1,) + shp[1:], lambda i, j: (i,) + (0,) * (len(shp) - 1))
    per_q = lambda shp: pl.BlockSpec(
        (1, 1) + shp[2:], lambda i, j: (i, jnp.minimum(j, nq - 1)) + (0,) * (len(shp) - 2))
    full = lambda shp: pl.BlockSpec(shp, lambda i, j: (0,) * len(shp))
    args = (qt, kc, vct, ks, vst, kw, vwt, gt, ovt, ebank, sbank, wbank)
    specs = [per_q(qt.shape), per_b(kc.shape), per_b(vct.shape), per_b(ks.shape), per_b(vst.shape),
             per_b(kw.shape), per_b(vwt.shape), per_q(gt.shape), full(ovt.shape),
             full(ebank.shape), full(sbank.shape), full(wbank.shape)]
    acc_like = pltpu.VMEM((N_KV, HEAD_DIM, GQ), F32)
    return pl.pallas_call(
        functools.partial(_attn_kernel, n_cmp_rows=n_cmp_rows, eb_shift=eb_shift),
        grid=(b, nq + 1),
        in_specs=specs,
        out_specs=pl.BlockSpec((1, Q_BLOCK, ATTN_W), lambda i, j: (i, jnp.maximum(j - 1, 0), 0)),
        out_shape=jax.ShapeDtypeStruct((b, t, ATTN_W), BF16),
        scratch_shapes=[pltpu.VMEM((N_KV, n_sb, GQ), F32),
                        pltpu.VMEM((N_KV, V_ROWS, GQ), F32),
                        acc_like,
                        pltpu.VMEM((N_KV, KV_W, GQ), BF16),
                        pltpu.VMEM((QK_AHEAD, KEY_CHUNK, GQ), BF16),
                        acc_like],
        compiler_params=_cparams(2),
        name="nsa_attention",
    )(*args)


def _ffn_kernel(x_ref, a_ref, c_ref, mod_ref, n2_ref, wo_ref, w1_ref, w2_ref, o_ref):
    aw = a_ref.shape[2]
    mix = jnp.dot(a_ref[0], wo_ref[0:aw, :], preferred_element_type=F32)
    mix = mix + jnp.dot(c_ref[0], wo_ref[aw:, :], preferred_element_type=F32)
    x1 = x_ref[0] + mod_ref[0, 2:3, :] * mix
    ms = jnp.mean(x1 * x1, axis=-1, keepdims=True)
    y = x1 * lax.rsqrt(ms + EPS) * n2_ref[...]
    h2 = (y * (1.0 + mod_ref[0, 4:5, :]) + mod_ref[0, 3:4, :]).astype(BF16)
    d_ff = w1_ref.shape[1]
    ff = jnp.zeros(x1.shape, F32)
    for j in range(d_ff // FF_CHUNK):
        a = jnp.dot(h2, w1_ref[:, j * FF_CHUNK:(j + 1) * FF_CHUNK], preferred_element_type=F32)
        a = jnp.maximum(a, 0.0)
        ff = ff + jnp.dot((a * a).astype(BF16), w2_ref[j * FF_CHUNK:(j + 1) * FF_CHUNK, :],
                          preferred_element_type=F32)
    o_ref[0] = x1 + mod_ref[0, 5:6, :] * ff


def _ffn_call(x, attn, conv, mod, norm2, w_out, w_ff1, w_ff2):
    b, t, d = x.shape
    tm = min(ROW_TILE, t)
    row_spec = lambda w: pl.BlockSpec((1, tm, w), lambda i, j: (i, j, 0))
    full = lambda shp: pl.BlockSpec(shp, lambda i, j: (0,) * len(shp),
                                    pipeline_mode=pl.Buffered(1))
    return pl.pallas_call(
        _ffn_kernel,
        grid=(b, t // tm),
        in_specs=[row_spec(d), row_spec(attn.shape[2]), row_spec(conv.shape[2]),
                  pl.BlockSpec((1, N_MOD, d), lambda i, j: (i, 0, 0)),
                  full((1, d)), full(w_out.shape), full(w_ff1.shape), full(w_ff2.shape)],
        out_specs=row_spec(d),
        out_shape=jax.ShapeDtypeStruct((b, t, d), F32),
        compiler_params=_cparams(2),
        name="outproj_mlp",
    )(x, attn, conv, mod, norm2, w_out, w_ff1, w_ff2)


def _block_diag_ones(n):
    idx = np.arange(n) // HEAD_DIM
    return jnp.asarray(idx[:, None] == idx[None, :], dtype=BF16)


def _pack_w_in(w_in):
    d = w_in.shape[0]
    conv_w = d - ATTN_W
    sizes = [ATTN_W] + [KV_W] * 6 + [N_BRANCH * N_HEADS] + [conv_w] * 3
    offs = np.concatenate([[0], np.cumsum(sizes)])
    part = lambda i: w_in[:, offs[i]:offs[i + 1]]
    q, kc, vc, ks, vs, kw, vw, g, cgate, bgate, u = (part(i) for i in range(11))
    w_nat = jnp.concatenate([kc, vc, ks, kw, cgate, bgate, u], axis=1).astype(BF16)
    gt = g.reshape(d, N_KV, GQA, N_BRANCH).transpose(2, 3, 1, 0).reshape(GQA, N_BRANCH * N_KV, d)
    gt = jnp.pad(gt, ((0, 0), (0, SUBLANES - N_BRANCH * N_KV), (0, 0))).reshape(_G_ROWS, d)
    w_tr = jnp.concatenate([q.T, vs.T, vw.T, gt], axis=0).astype(BF16)
    return w_nat, w_tr


def _expand_w1(w1):
    hid = w1.shape[1]
    w = w1.reshape(2, CMP_STRIDE, HEAD_DIM, hid).astype(BF16)
    zero = jnp.zeros_like(w)
    per_group = [jnp.concatenate([w if k == g else zero for k in range(N_KV)], axis=-1)
                 for g in range(N_KV)]
    return jnp.stack(per_group, axis=2).reshape(2, CMP_STRIDE * KV_W, N_KV * hid)


def _expand_w2(w2):
    hid = w2.shape[0]
    eye = jnp.eye(N_KV, dtype=w2.dtype).reshape(N_KV, 1, N_KV, 1)
    return (w2.reshape(1, hid, 1, HEAD_DIM) * eye).reshape(N_KV * hid, KV_W).astype(BF16)


def _expand_pe(pe):
    p = pe.reshape(2, CMP_STRIDE, 1, HEAD_DIM)
    return jnp.broadcast_to(p, (2, CMP_STRIDE, N_KV, HEAD_DIM)).reshape(2, CMP_STRIDE * KV_W)


def _bucket_thresholds():
    n = np.arange(2 * REL_MAX_DIST)
    max_exact = REL_BUCKETS // 2
    nf = np.maximum(n, max_exact).astype(np.float32)
    ratio = np.log(nf / np.float32(max_exact)) / np.float32(math.log(REL_MAX_DIST / max_exact))
    large = max_exact + (ratio * np.float32(REL_BUCKETS - max_exact)).astype(np.int32)
    table = np.where(n < max_exact, n, np.minimum(large, REL_BUCKETS - 1))
    return tuple(int(np.searchsorted(table, k, side="left")) for k in range(REL_BUCKETS))


def _bank_call(bias_rows, lead, n_tiles, dist_fn, name):
    nl = len(lead)
    thr = _bucket_thresholds()

    def body(rows_ref, o_ref):
        lead_ids = [pl.program_id(a) for a in range(nl)]
        row = lax.broadcasted_iota(jnp.int32, (SEL_BLOCK, GQ), 0)
        qi = lax.broadcasted_iota(jnp.int32, (SEL_BLOCK, GQ), 1) & (Q_BLOCK - 1)

        def tile(t, carry):
            dist, ok = dist_fn(lead_ids, t, row, qi)
            v = jnp.broadcast_to(rows_ref[0, 0:1, :], (SEL_BLOCK, GQ))
            for k in range(1, REL_BUCKETS):
                v = jnp.where(dist >= thr[k], rows_ref[0, k:k + 1, :], v)
            o_ref[(0,) * (nl + 1) + (t,)] = jnp.where(ok, v, NEG)
            return carry

        lax.fori_loop(0, n_tiles, tile, 0)

    return pl.pallas_call(
        body,
        grid=tuple(lead) + (N_KV,),
        in_specs=[pl.BlockSpec((1, REL_BUCKETS, GQ), lambda *i: (i[nl], 0, 0))],
        out_specs=pl.BlockSpec((1,) * (nl + 1) + (n_tiles, SEL_BLOCK, GQ),
                               lambda *i: tuple(i) + (0, 0, 0)),
        out_shape=jax.ShapeDtypeStruct(tuple(lead) + (N_KV, n_tiles, SEL_BLOCK, GQ), F32),
        compiler_params=_cparams(nl + 1),
        name=name,
    )(bias_rows)


def _bias_banks(rel_bias, t):
    n_cmp_rows = t // CMP_STRIDE
    rows = rel_bias.reshape(REL_BUCKETS, N_KV, GQA).transpose(1, 0, 2)
    rows = jnp.repeat(rows, Q_BLOCK, axis=2) * LOG2E

    def sel_dist(lead, tile, row, qi):
        dist = SEL_BLOCK * tile + qi - row
        return dist, dist >= 0

    sbank = _bank_call(rows, (), NEAR_TILES + 1, sel_dist, "bias_bank_sel")

    def win_dist(lead, tile, row, qi):
        dist = SEL_BLOCK * tile + qi - row
        return dist, (dist >= 0) & (dist < WINDOW)

    wbank = _bank_call(rows, (), WIN_BLOCKS + 2, win_dist, "bias_bank_win")

    cmp_c = n_cmp_rows - 4
    shifts = tuple(int((-(cmp_c - 4 * p)) % 8) for p in range(2))
    n_tiles = (cmp_c + n_cmp_rows + 8 + SEL_BLOCK - 1) // SEL_BLOCK

    def cmp_dist(lead, tile, row, qi):
        e = SEL_BLOCK * tile + row - jnp.where(lead[0] == 0, shifts[0], shifts[1])
        dist = qi - CMP_STRIDE * e + (CMP_STRIDE * cmp_c - (CMP_BLOCK - 1))
        return dist, (dist >= 0) & (e >= 0)

    ebank = _bank_call(rows, (2,), n_tiles, cmp_dist, "bias_bank_cmp")
    ebank = ebank.reshape(2, N_KV, n_tiles * SEL_BLOCK, GQ)
    return ebank, sbank, wbank, shifts


def _overlap_t(t):
    n_cmp_rows = t // CMP_STRIDE
    n_sb = t // SEL_BLOCK
    c_start = np.arange(n_cmp_rows)[None, :] * CMP_STRIDE
    s_start = np.arange(n_sb)[:, None] * SEL_BLOCK
    ov = np.clip(np.minimum(c_start + CMP_BLOCK, s_start + SEL_BLOCK)
                 - np.maximum(c_start, s_start), 0, None) / CMP_BLOCK
    ov[:, n_cmp_rows - 1] = 0.0
    return jnp.asarray(ov, dtype=BF16)


def _layer(x, c_pad, w_in, q_norm, k_norm, cmp_pe_k, cmp_w1_k, cmp_w2_k, cmp_pe_v, cmp_w1_v,
           cmp_w2_v, rel_bias, conv_w, w_out, norm1, norm2, w_ada, b_ada, w_ff1, w_ff2):
    b, t, d = x.shape
    scale = HEAD_DIM ** -0.5

    mod = _mod_call(c_pad, w_ada, b_ada)[:b].reshape(b, N_MOD, d)

    qn_col = (jnp.tile(q_norm, N_HEADS) * (scale * LOG2E)).reshape(ATTN_W, 1)
    kn_t = jnp.tile(k_norm, N_KV).reshape(1, KV_W)
    bdq = _block_diag_ones(ATTN_W)
    bdk = _block_diag_ones(KV_W)
    w_nat, w_tr = _pack_w_in(w_in)
    qt, kc_raw, vc_raw, ks, vst, kw, vwt, gt, conv = _inproj_call(
        x, mod, norm1.reshape(1, d), w_nat, w_tr, qn_col, kn_t, conv_w, bdq, bdk)

    kc, vct = _compress_call(
        kc_raw, vc_raw, _expand_pe(cmp_pe_k), _expand_pe(cmp_pe_v), _expand_w1(cmp_w1_k), _expand_w1(cmp_w1_v),
        _expand_w2(cmp_w2_k), _expand_w2(cmp_w2_v).T, kn_t, bdk)

    ebank, sbank, wbank, eb_shift = _bias_banks(rel_bias, t)
    attn = _attn_call(qt, kc, vct, ks, vst, kw, vwt, gt, _overlap_t(t), ebank, sbank.astype(BF16),
                      wbank.astype(BF16), eb_shift)

    return _ffn_call(x, attn, conv, mod, norm2.reshape(1, d), w_out.astype(BF16),
                     w_ff1.astype(BF16), w_ff2.astype(BF16))


def kernel(x, c, w_in, q_norm, k_norm, cmp_pe_k, cmp_w1_k, cmp_w2_k, cmp_pe_v, cmp_w1_v, cmp_w2_v,
           rel_bias, conv_w, w_out, norm1, norm2, w_ada, b_ada, w_ff1, w_ff2):
    b = x.shape[0]
    c_pad = jnp.pad(c, ((0, (-b) % 8), (0, 0)))
    for l in range(w_in.shape[0]):
        x = _layer(x, c_pad, w_in[l], q_norm[l], k_norm[l], cmp_pe_k[l], cmp_w1_k[l], cmp_w2_k[l],
                   cmp_pe_v[l], cmp_w1_v[l], cmp_w2_v[l], rel_bias, conv_w[l], w_out[l],
                   norm1[l], norm2[l], w_ada[l], b_ada[l], w_ff1[l], w_ff2[l])
    return x
```

```python
import functools
import math

import numpy as np
import jax
import jax.numpy as jnp
from jax import lax
from jax.experimental import pallas as pl
from jax.experimental.pallas import tpu as pltpu

HEAD_DIM = 64
N_HEADS = 8
N_KV = 2
GQA = N_HEADS // N_KV
ATTN_W = N_HEADS * HEAD_DIM
KV_W = N_KV * HEAD_DIM
CONV_K = 3
CMP_BLOCK = 32
CMP_STRIDE = 16
SEL_BLOCK = 64
N_SELECT = 16
WINDOW = 512
Q_BLOCK = 64
REL_BUCKETS = 32
REL_MAX_DIST = 1024
N_MOD = 6
N_BRANCH = 3
EPS = 1e-6
NEG = -1e30

LANES = 128
SUBLANES = 8
GQ = GQA * Q_BLOCK
KEY_CHUNK = 256
BLOCKS_PER_CHUNK = KEY_CHUNK // SEL_BLOCK
V_TILE = 128
BF16_ROWS = 16
V_ROWS = HEAD_DIM + BF16_ROWS
V_TILE_ROWS = N_KV * V_ROWS
CMP_ROW_STEP = 128
WIN_BLOCKS = WINDOW // SEL_BLOCK
WIN_TILES = WINDOW // V_TILE + 1
NEAR_TILES = (REL_MAX_DIST + Q_BLOCK - 1) // SEL_BLOCK + 1
ROW_TILE = 512
INPROJ_TILE = 1024
INPROJ_SPLIT = 4
BIG_CHUNKS = 16
TAIL_CHUNKS = 1
assert BIG_CHUNKS % TAIL_CHUNKS == 0
MERGE_CHUNKS = 2
QK_AHEAD = 4
QK_DEPTH = 5
SMALLEST_IMPORTANCE = 2.0 ** -100
LOG2E = math.log2(math.e)
FF_CHUNK = 1024
VMEM_LIMIT = 56 * 1024 * 1024

F32 = jnp.float32
BF16 = jnp.bfloat16
_NT = (((1,), (1,)), ((), ()))


def _cparams(n_axes):
    return pltpu.CompilerParams(dimension_semantics=("arbitrary",) * n_axes,
                                vmem_limit_bytes=VMEM_LIMIT)


def _with_ones_rows(vt):
    ones = jnp.ones((BF16_ROWS, vt.shape[1]), vt.dtype)
    parts = []
    for g in range(N_KV):
        parts += [vt[g * HEAD_DIM:(g + 1) * HEAD_DIM, :], ones]
    return jnp.concatenate(parts, axis=0)


def _swap_halves(p0, p1):
    low = lax.broadcasted_iota(jnp.int32, p0.shape, 1) < LANES // 2
    return (jnp.where(low, p0, pltpu.roll(p1, LANES // 2, 1)),
            jnp.where(low, pltpu.roll(p0, LANES // 2, 1), p1))


def _mod_kernel(c_ref, w_ref, b_ref, o_ref):
    c = c_ref[...]
    a = c * jax.nn.sigmoid(c)
    o_ref[...] = jnp.dot(a, w_ref[...], preferred_element_type=F32,
                         precision=lax.Precision.HIGHEST) + b_ref[...]


def _mod_call(c_pad, w_ada, b_ada):
    rows, d = c_pad.shape
    n = w_ada.shape[1]
    tn = n // N_MOD
    return pl.pallas_call(
        _mod_kernel,
        grid=(n // tn,),
        in_specs=[pl.BlockSpec((rows, d), lambda j: (0, 0)),
                  pl.BlockSpec((d, tn), lambda j: (0, j)),
                  pl.BlockSpec((1, tn), lambda j: (0, j))],
        out_specs=pl.BlockSpec((rows, tn), lambda j: (0, j)),
        out_shape=jax.ShapeDtypeStruct((rows, n), F32),
        compiler_params=_cparams(1),
        name="adaln_mod",
    )(c_pad, w_ada, b_ada.reshape(1, n))


_N_KC, _N_VC, _N_KS, _N_KW, _N_CONV = 0, KV_W, 2 * KV_W, 3 * KV_W, 4 * KV_W
_T_Q, _T_VS, _T_VW, _T_G = 0, ATTN_W, ATTN_W + KV_W, ATTN_W + 2 * KV_W
_G_ROWS = GQA * SUBLANES


def _inproj_kernel(x_ref, mod_ref, n1_ref, wn_ref, wt_ref, qn_ref, kn_ref, cw_ref, bdq_ref, bdk_ref,
                   qt_out, kc_out, vc_out, ks_out, vst_out, kw_out, vwt_out, gt_out, conv_out,
                   carry, *, conv_w):
    t = pl.program_id(1)
    tm = x_ref.shape[1]
    sub = tm // INPROJ_SPLIT

    @pl.when(t == 0)
    def _():
        carry[...] = jnp.zeros_like(carry)

    prev2, prev1 = carry[6:7, :], carry[7:8, :]
    c_bg = _N_CONV + conv_w
    c_u = c_bg + conv_w

    for part in range(INPROJ_SPLIT):
        rows = slice(part * sub, (part + 1) * sub)
        x = x_ref[0, rows, :]
        ms = jnp.mean(x * x, axis=-1, keepdims=True)
        y = x * lax.rsqrt(ms + EPS) * n1_ref[...]
        h = (y * (1.0 + mod_ref[0, 1:2, :]) + mod_ref[0, 0:1, :]).astype(BF16)

        def proj(a, b):
            return jnp.dot(h, wn_ref[:, a:b], preferred_element_type=F32)

        def proj_t(a, b):
            return lax.dot_general(wt_ref[a:b, :], h, _NT, preferred_element_type=F32)

        def head_norm(v, gain):
            ssq = jnp.dot((v * v).astype(BF16), bdk_ref[...], preferred_element_type=F32)
            return v * lax.rsqrt(ssq * (1.0 / HEAD_DIM) + EPS) * gain

        qf = proj_t(_T_Q, _T_VS)
        vg = proj_t(_T_VS, _T_G + _G_ROWS)
        kv = proj(_N_KC, _N_CONV)
        cv = proj(_N_CONV, c_u + conv_w)

        ssq = jnp.dot(bdq_ref[...], (qf * qf).astype(BF16), preferred_element_type=F32)
        qf = qf * lax.rsqrt(ssq * (1.0 / HEAD_DIM) + EPS) * qn_ref[...]
        gf = jax.nn.sigmoid(vg[2 * KV_W:, :])
        for c in range(sub // LANES):
            blk = (part * sub) // Q_BLOCK + 2 * c
            cols = slice(c * LANES, (c + 1) * LANES)
            for g in range(N_KV):
                pc = [qf[(g * GQA + r) * HEAD_DIM:(g * GQA + r + 1) * HEAD_DIM, cols]
                      for r in range(GQA)]
                lo01, hi01 = _swap_halves(pc[0], pc[1])
                lo23, hi23 = _swap_halves(pc[2], pc[3])
                qt_out[0, blk, g] = jnp.concatenate([lo01, lo23], axis=1).astype(BF16)
                qt_out[0, blk + 1, g] = jnp.concatenate([hi01, hi23], axis=1).astype(BF16)
            pc = [gf[r * SUBLANES:(r + 1) * SUBLANES, cols] for r in range(GQA)]
            lo01, hi01 = _swap_halves(pc[0], pc[1])
            lo23, hi23 = _swap_halves(pc[2], pc[3])
            gt_out[0, blk] = jnp.concatenate([lo01, lo23], axis=1)
            gt_out[0, blk + 1] = jnp.concatenate([hi01, hi23], axis=1)

        vs_f = _with_ones_rows(vg[:KV_W, :].astype(BF16))
        vw_f = _with_ones_rows(vg[KV_W:2 * KV_W, :].astype(BF16))
        for j in range(sub // KEY_CHUNK):
            vst_out[0, (part * sub) // KEY_CHUNK + j] = vs_f[:, j * KEY_CHUNK:(j + 1) * KEY_CHUNK]
        for j in range(sub // V_TILE):
            vwt_out[0, (part * sub) // V_TILE + j] = vw_f[:, j * V_TILE:(j + 1) * V_TILE]

        kc_out[0, rows, :] = kv[:, _N_KC:_N_VC]
        vc_out[0, rows, :] = kv[:, _N_VC:_N_KS]
        ks_out[0, rows, :] = head_norm(kv[:, _N_KS:_N_KW], kn_ref[...]).astype(BF16)
        kw_out[0, rows, :] = head_norm(kv[:, _N_KW:_N_CONV], kn_ref[...]).astype(BF16)

        z = cv[:, :conv_w] * cv[:, 2 * conv_w:]
        row = lax.broadcasted_iota(jnp.int32, z.shape, 0)
        z1 = jnp.where(row == 0, prev1, pltpu.roll(z, 1, 0))
        z2 = jnp.where(row == 0, prev2, jnp.where(row == 1, prev1, pltpu.roll(z, 2, 0)))
        zc = cw_ref[0:1, :] * z2 + cw_ref[1:2, :] * z1 + cw_ref[2:3, :] * z
        conv_out[0, rows, :] = (cv[:, conv_w:2 * conv_w] * zc).astype(BF16)
        prev2, prev1 = z[sub - 2:sub - 1, :], z[sub - 1:sub, :]
        if part == INPROJ_SPLIT - 1:
            carry[...] = z[sub - SUBLANES:sub, :]


def _inproj_call(x, mod, norm1, w_nat, w_tr, qn_col, kn_t, conv_w, bdq, bdk):
    b, t, d = x.shape
    tm = min(INPROJ_TILE, t)
    cw = conv_w.shape[1]
    nq = t // Q_BLOCK
    row_spec = lambda w: pl.BlockSpec((1, tm, w), lambda i, j: (i, j, 0))
    full = lambda shp: pl.BlockSpec(shp, lambda i, j: (0,) * len(shp))
    vt_spec = lambda w: pl.BlockSpec((1, tm // w, V_TILE_ROWS, w), lambda i, j: (i, j, 0, 0))
    vt_shape = lambda w: jax.ShapeDtypeStruct((b, t // w, V_TILE_ROWS, w), BF16)
    kv = lambda dt: jax.ShapeDtypeStruct((b, t, KV_W), dt)
    out_specs = [pl.BlockSpec((1, tm // Q_BLOCK, N_KV, HEAD_DIM, GQ), lambda i, j: (i, j, 0, 0, 0)),
                 row_spec(KV_W), row_spec(KV_W), row_spec(KV_W), vt_spec(KEY_CHUNK), row_spec(KV_W),
                 vt_spec(V_TILE),
                 pl.BlockSpec((1, tm // Q_BLOCK, SUBLANES, GQ), lambda i, j: (i, j, 0, 0)),
                 row_spec(cw)]
    out_shape = [jax.ShapeDtypeStruct((b, nq, N_KV, HEAD_DIM, GQ), BF16),
                 kv(F32), kv(F32), kv(BF16), vt_shape(KEY_CHUNK), kv(BF16), vt_shape(V_TILE),
                 jax.ShapeDtypeStruct((b, nq, SUBLANES, GQ), F32),
                 jax.ShapeDtypeStruct((b, t, cw), BF16)]
    return pl.pallas_call(
        functools.partial(_inproj_kernel, conv_w=cw),
        grid=(b, t // tm),
        in_specs=[row_spec(d),
                  pl.BlockSpec((1, N_MOD, d), lambda i, j: (i, 0, 0)),
                  full((1, d)), full(w_nat.shape), full(w_tr.shape), full((ATTN_W, 1)),
                  full((1, KV_W)), full((CONV_K, cw)), full((ATTN_W, ATTN_W)), full((KV_W, KV_W))],
        out_specs=out_specs,
        out_shape=out_shape,
        scratch_shapes=[pltpu.VMEM((SUBLANES, cw), F32)],
        compiler_params=_cparams(2),
        name="inproj",
    )(x, mod, norm1, w_nat, w_tr, qn_col, kn_t, conv_w, bdq, bdk)


def _compress_kernel(kx_ref, vx_ref, pek_ref, pev_ref, w1k_ref, w1v_ref, w2k_ref, w2vt_ref,
                     kn_ref, bdk_ref, kc_out, vct_out):
    def hidden(x_ref, pe_ref, w1_ref):
        n = x_ref.shape[1] // CMP_STRIDE
        u = jnp.zeros((n, w1_ref.shape[2]), F32)
        v = jnp.zeros((n, w1_ref.shape[2]), F32)
        for r in range(0, CMP_STRIDE, 2):
            tok = [x_ref[0, pl.ds(r + d, n, stride=CMP_STRIDE), :] for d in range(2)]
            cols = slice(r * KV_W, (r + 2) * KV_W)
            for a, acc in ((0, "u"), (1, "v")):
                lhs = jnp.concatenate([tok[d] + pe_ref[a:a + 1, (r + d) * KV_W:(r + d + 1) * KV_W]
                                       for d in range(2)], axis=1).astype(BF16)
                prod = jnp.dot(lhs, w1_ref[a, cols, :], preferred_element_type=F32)
                if acc == "u":
                    u = u + prod
                else:
                    v = v + prod
        hid = u + pltpu.roll(v, n - 1, 0)
        return jax.nn.gelu(hid, approximate=True).astype(BF16)

    kc = jnp.dot(hidden(kx_ref, pek_ref, w1k_ref), w2k_ref[...], preferred_element_type=F32)
    ssq = jnp.dot((kc * kc).astype(BF16), bdk_ref[...], preferred_element_type=F32)
    kc_out[0] = (kc * lax.rsqrt(ssq * (1.0 / HEAD_DIM) + EPS) * kn_ref[...]).astype(BF16)
    vct_out[0] = _with_ones_rows(lax.dot_general(w2vt_ref[...], hidden(vx_ref, pev_ref, w1v_ref), _NT,
                                                 preferred_element_type=F32).astype(BF16))


def _compress_call(kx, vx, pek, pev, w1k, w1v, w2k, w2vt, kn_t, bdk):
    b, t, _ = kx.shape
    nrow = t // CMP_STRIDE
    wide, hid2 = w1k.shape[1], w1k.shape[2]
    full = lambda shp: pl.BlockSpec(shp, lambda i: (0,) * len(shp))
    xs = pl.BlockSpec((1, t, KV_W), lambda i: (i, 0, 0))
    return pl.pallas_call(
        _compress_kernel,
        grid=(b,),
        in_specs=[xs, xs, full((2, wide)), full((2, wide)), full((2, wide, hid2)),
                  full((2, wide, hid2)), full((hid2, KV_W)), full((KV_W, hid2)),
                  full((1, KV_W)), full((KV_W, KV_W))],
        out_specs=[pl.BlockSpec((1, nrow, KV_W), lambda i: (i, 0, 0)),
                   pl.BlockSpec((1, V_TILE_ROWS, nrow), lambda i: (i, 0, 0))],
        out_shape=[jax.ShapeDtypeStruct((b, nrow, KV_W), BF16),
                   jax.ShapeDtypeStruct((b, V_TILE_ROWS, nrow), BF16)],
        compiler_params=_cparams(1),
        name="compress",
    )(kx, vx, pek, pev, w1k, w1v, w2k, w2vt, kn_t, bdk)


def _attn_kernel(q_ref, kc_ref, vct_ref, ks_ref, vst_ref, kw_ref, vwt_ref, gt_ref, ovt_ref,
                 eb_ref, sb_ref, wb_ref, o_ref, sel_scr, acc_scr, out_scr, qp_scr, s_scr,
                 fin_scr, *, n_cmp_rows, eb_shift):
    n_sb = sel_scr.shape[1]
    write_out_only = pl.program_id(1) >= n_sb
    ci = jnp.minimum(pl.program_id(1), n_sb - 1)
    par = lax.rem(ci, 2)
    n_total = ks_ref.shape[1] // KEY_CHUNK
    vsl = lambda g: slice(g * V_ROWS, (g + 1) * V_ROWS)

    @pl.when((pl.program_id(0) == 0) & (pl.program_id(1) == 0))
    def _():
        def zero(g, carry):
            fin_scr[g] = jnp.zeros(fin_scr.shape[1:], F32)
            sel_scr[g] = jnp.zeros(sel_scr.shape[1:], F32)
            return carry
        lax.fori_loop(0, N_KV, zero, 0)

    def chunk_qk(g, c):
        kk = ks_ref[0, pl.ds(pl.multiple_of(c * KEY_CHUNK, KEY_CHUNK), KEY_CHUNK), :]
        return jnp.dot(kk, qp_scr[g], preferred_element_type=F32).astype(BF16)

    n_chunks = jnp.where(write_out_only, 0, ci // BLOCKS_PER_CHUNK + 1)
    tails_per_big = BIG_CHUNKS // TAIL_CHUNKS
    n_tails = (n_chunks + TAIL_CHUNKS - 1) // TAIL_CHUNKS
    n_big = n_tails // tails_per_big
    tail = lax.rem(n_tails, tails_per_big)
    tail_start = n_big * BIG_CHUNKS

    def big_units(i):
        return [(g, jnp.minimum(BIG_CHUNKS * i + j, n_total - 1))
                for g in range(N_KV) for j in range(BIG_CHUNKS)]

    def tail_units(size):
        return [(g, jnp.minimum(tail_start + j, n_total - 1))
                for g in range(N_KV) for j in range(size * TAIL_CHUNKS)]

    short = [s for s in range(1, tails_per_big) if s * TAIL_CHUNKS < QK_AHEAD]
    tail_options = [(tail == s, tail_units(s)) for s in short] + [(None, tail_units(tails_per_big - 1))]

    def pick_units(options):
        pad = lambda units: (units + [units[-1]] * QK_AHEAD)[:QK_AHEAD]
        units = pad(options[-1][1])
        for cond, cand in reversed(options[:-1]):
            units = [(jnp.where(cond, g_a, g_b), jnp.where(cond, c_a, c_b))
                     for (g_a, c_a), (g_b, c_b) in zip(pad(cand), units)]
        return [(g, jnp.minimum(c, n_total - 1)) for g, c in units]

    def prefetch_scores(units):
        for k in range(QK_AHEAD):
            s_scr[k] = chunk_qk(*units[k])

    def write_out():
        pieces = []
        for g in range(N_KV):
            for half in range(GQ // LANES):
                a = fin_scr[g, :, half * LANES:(half + 1) * LANES]
                stacked = jnp.concatenate([a, pltpu.roll(a, Q_BLOCK, 1)], axis=0)
                pieces.append(stacked.T[:Q_BLOCK, :])
        o_ref[0] = jnp.concatenate(pieces, axis=1).astype(BF16)

    def before_loops(rows):
        write_out()

        zeros_q = jnp.zeros((HEAD_DIM, GQ), BF16)
        q_padded = [jnp.concatenate([q_ref[0, 0, 0], zeros_q], axis=0),
                    jnp.concatenate([zeros_q, q_ref[0, 0, 1]], axis=0)]
        for g in range(N_KV):
            qp_scr[g] = q_padded[g]

        nsel = min(n_sb, rows * CMP_STRIDE // SEL_BLOCK)
        lane = lax.broadcasted_iota(jnp.int32, (nsel, LANES), 1)
        cmp_c = n_cmp_rows - 4
        e0 = cmp_c - 4 * ci + jnp.where(par == 0, eb_shift[0], eb_shift[1])
        e0 = pl.multiple_of(e0, 8)
        o_cmp = []
        imp = []
        w0 = ci // 2 - (WIN_TILES - 1)
        win_tiles = [jnp.maximum(w0 + j, 0) for j in range(WIN_TILES)]
        s_cmp = jnp.dot(kc_ref[0, :rows, :], jnp.concatenate(q_padded, axis=1),
                        preferred_element_type=F32)

        def window_scores():
            return [[jnp.dot(kw_ref[0, pl.ds(pl.multiple_of(tj * V_TILE, V_TILE), V_TILE), :],
                             q_padded[g], preferred_element_type=F32).astype(BF16) for tj in win_tiles]
                    for g in range(N_KV)]

        window_first = 2 * rows <= n_cmp_rows
        if window_first:
            s_win = window_scores()
        for g in range(N_KV):
            sc = s_cmp[:, g * GQ:(g + 1) * GQ] + eb_ref[par, g, pl.ds(e0, rows), :]
            m = jnp.max(sc, axis=0, keepdims=True)
            e = jnp.exp2(sc - m).astype(BF16)
            ov = jnp.dot(vct_ref[0, vsl(g), :rows], e, preferred_element_type=F32)
            inv = jnp.where(m > 0.5 * NEG, 1.0 / ov[HEAD_DIM:HEAD_DIM + 1, :], 0.0)
            o_cmp.append(ov[:HEAD_DIM, :] * inv)
            ir = jnp.dot(ovt_ref[:nsel, :rows], e, preferred_element_type=F32) * inv
            a = ir[:, :LANES] + ir[:, LANES:]
            imp.append(a + pltpu.roll(a, Q_BLOCK, 1))
        if not window_first:
            s_win = window_scores()

        prefetch_scores(pick_units([(n_big > 0, big_units(0))] + tail_options))

        jidx = lax.broadcasted_iota(jnp.int32, (nsel, LANES), 0)
        valid = jidx <= ci
        forced = (jidx == 0) | (jidx == ci) | (jidx == ci - 1)
        index_mask = (1 << max(1, (n_sb - 1).bit_length())) - 1
        imp_bits = lax.bitcast_convert_type(
            jnp.maximum(jnp.where(lane < Q_BLOCK, imp[0], imp[1]), SMALLEST_IMPORTANCE), jnp.int32)
        keyed = lax.bitcast_convert_type((imp_bits & ~index_mask) | (index_mask - jidx), F32)
        score = jnp.where(valid, jnp.where(forced, -2.0, keyed), -1.0)
        for _ in range(min(N_SELECT, nsel) - 3):
            score = jnp.where(score == jnp.max(score, axis=0, keepdims=True), -2.0, score)
        selneg = jnp.where((score < -1.5) & valid, 0.0, NEG)
        swapped = pltpu.roll(selneg, Q_BLOCK, 1)
        left = jnp.where(lane < Q_BLOCK, selneg, swapped)
        right = jnp.where(lane < Q_BLOCK, swapped, selneg)
        sel_scr[0, :nsel] = jnp.concatenate([left, left], axis=1)
        sel_scr[1, :nsel] = jnp.concatenate([right, right], axis=1)

        for g in range(N_KV):
            parts = []
            for i in range(2 * WIN_TILES):
                delta = par + WIN_BLOCKS - i
                ok = (delta >= 0) & (delta <= WIN_BLOCKS) & (delta <= ci)
                tile = jnp.where(ok, delta, WIN_BLOCKS + 1)
                half = s_win[g][i // 2][(i % 2) * SEL_BLOCK:(i % 2 + 1) * SEL_BLOCK, :]
                parts.append(half + wb_ref[g, tile])
            s = jnp.concatenate(parts, axis=0)
            m = jnp.max(s, axis=0, keepdims=True)
            pb = jnp.exp2(s - m)
            o_win = jnp.zeros((V_ROWS, GQ), F32)
            for j, tj in enumerate(win_tiles):
                o_win = o_win + jnp.dot(vwt_ref[0, tj, vsl(g), :], pb[j * V_TILE:(j + 1) * V_TILE, :],
                                        preferred_element_type=F32)
            w_scale = (gt_ref[0, 0, 2 * N_KV + g:2 * N_KV + g + 1, :]
                       * (1.0 / o_win[HEAD_DIM:HEAD_DIM + 1, :]))
            out_scr[g] = gt_ref[0, 0, g:g + 1, :] * o_cmp[g] + w_scale * o_win[:HEAD_DIM, :]
            acc_scr[g] = jnp.zeros((V_ROWS, GQ), F32)

    row_steps = [r for r in range(CMP_ROW_STEP, n_cmp_rows + 1, CMP_ROW_STEP)] or [n_cmp_rows]
    variant = jnp.minimum((4 * ci + 2) // CMP_ROW_STEP, len(row_steps) - 1)
    variant = jnp.where(write_out_only, len(row_steps), variant)
    lax.switch(variant, [functools.partial(before_loops, r) for r in row_steps] + [write_out])

    def chunk_softmax(s, g, c, extra):
        parts = []
        for i in range(BLOCKS_PER_CHUNK):
            kb = c * BLOCKS_PER_CHUNK + i
            blk = s[i * SEL_BLOCK:(i + 1) * SEL_BLOCK, :]
            mrow = sel_scr[g, pl.ds(kb, 1), :]
            if extra is not None:
                mrow = mrow + extra
            tile = jnp.clip(ci - kb, 0, NEAR_TILES)
            parts.append(blk + sb_ref[g, tile] + mrow.astype(BF16))
        s = jnp.concatenate(parts, axis=0)
        m_c = jnp.max(s, axis=0, keepdims=True)
        return m_c.astype(F32), jnp.exp2(s - m_c)

    def chunk_pv(pb, g, c):
        return jnp.dot(vst_ref[0, c, vsl(g), :], pb, preferred_element_type=F32)

    def combine(g, acc):
        scale = gt_ref[0, 0, N_KV + g:N_KV + g + 1, :] * (1.0 / acc[HEAD_DIM:HEAD_DIM + 1, :])
        fin_scr[g] = out_scr[g] + scale * acc[:HEAD_DIM, :]

    def merge_step(units, extras, next_units, carry):
        last = next_units is None
        parked = min(QK_AHEAD, len(units))
        scores = {k: s_scr[k] for k in range(parked)}
        pending = list(range(parked, len(units) + (0 if next_units is None else QK_AHEAD)))

        def issue_scores():
            k = pending.pop(0)
            if k < len(units):
                scores[k] = chunk_qk(*units[k])
            else:
                s_scr[k - len(units)] = chunk_qk(*next_units[k - len(units)])

        for _ in range(min(QK_DEPTH - QK_AHEAD, len(pending))):
            issue_scores()
        results = []
        for k, (g, c) in enumerate(units):
            m_c, pb = chunk_softmax(scores.pop(k), g, c, extras[k])
            if pending:
                issue_scores()
            results.append((m_c, chunk_pv(pb, g, c)))
        per_g = len(units) // N_KV
        new = []
        for g in range(N_KV):
            m_run = carry[g]
            acc = acc_scr[g]
            for first in range(g * per_g, (g + 1) * per_g, MERGE_CHUNKS):
                stats = results[first:min(first + MERGE_CHUNKS, (g + 1) * per_g)]
                m_new = m_run
                for m_c, _ in stats:
                    m_new = jnp.maximum(m_new, m_c)
                acc = jnp.exp2(m_run - m_new) * acc
                for m_c, pv in stats:
                    acc = acc + jnp.exp2(m_c - m_new) * pv
                m_run = m_new
            if last:
                combine(g, acc)
            else:
                acc_scr[g] = acc
            new.append(m_run)
        return tuple(new)

    def big_step(i, carry):
        nxt = pick_units([(i + 1 < n_big, big_units(i + 1))] + tail_options)
        extras = [jnp.where(BIG_CHUNKS * i + j < n_chunks, 0.0, NEG)
                  if j > BIG_CHUNKS - TAIL_CHUNKS else None
                  for _ in range(N_KV) for j in range(BIG_CHUNKS)]
        return merge_step(big_units(i), extras, nxt, carry)

    carry = lax.fori_loop(0, n_big, big_step, (jnp.full((1, GQ), NEG, F32),) * N_KV)

    def tail_step(size):
        extras = [None if j == 0 else jnp.where(tail_start + j < n_chunks, 0.0, NEG)
                  for _ in range(N_KV) for j in range(size * TAIL_CHUNKS)]
        merge_step(tail_units(size), extras, None, carry)

    def run_tail(lo, hi):
        if hi - lo == 1:
            if lo > 0:
                tail_step(lo)
            else:
                for g in range(N_KV):
                    combine(g, acc_scr[g])
        else:
            mid = (lo + hi) // 2
            lax.cond(tail < mid, functools.partial(run_tail, lo, mid), functools.partial(run_tail, mid, hi))

    run_tail(0, tails_per_big)


def _attn_call(qt, kc, vct, ks, vst, kw, vwt, gt, ovt, ebank, sbank, wbank, eb_shift):
    b, nq = qt.shape[0], qt.shape[1]
    t = ks.shape[1]
    n_cmp_rows = kc.shape[1]
    n_sb = t // SEL_BLOCK
    per_b = lambda shp: pl.BlockSpec((1,) + shp[1:], lambda i, j: (i,) + (0,) * (len(shp) - 1))
    per_q = lambda shp: pl.BlockSpec(
        (1, 1) + shp[2:], lambda i, j: (i, jnp.minimum(j, nq - 1)) + (0,) * (len(shp) - 2))
    full = lambda shp: pl.BlockSpec(shp, lambda i, j: (0,) * len(shp))
    args = (qt, kc, vct, ks, vst, kw, vwt, gt, ovt, ebank, sbank, wbank)
    specs = [per_q(qt.shape), per_b(kc.shape), per_b(vct.shape), per_b(ks.shape), per_b(vst.shape),
             per_b(kw.shape), per_b(vwt.shape), per_q(gt.shape), full(ovt.shape),
             full(ebank.shape), full(sbank.shape), full(wbank.shape)]
    acc_like = pltpu.VMEM((N_KV, HEAD_DIM, GQ), F32)
    return pl.pallas_call(
        functools.partial(_attn_kernel, n_cmp_rows=n_cmp_rows, eb_shift=eb_shift),
        grid=(b, nq + 1),
        in_specs=specs,
        out_specs=pl.BlockSpec((1, Q_BLOCK, ATTN_W), lambda i, j: (i, jnp.maximum(j - 1, 0), 0)),
        out_shape=jax.ShapeDtypeStruct((b, t, ATTN_W), BF16),
        scratch_shapes=[pltpu.VMEM((N_KV, n_sb, GQ), F32),
                        pltpu.VMEM((N_KV, V_ROWS, GQ), F32),
                        acc_like,
                        pltpu.VMEM((N_KV, KV_W, GQ), BF16),
                        pltpu.VMEM((QK_AHEAD, KEY_CHUNK, GQ), BF16),
                        acc_like],
        compiler_params=_cparams(2),
        name="nsa_attention",
    )(*args)


def _ffn_kernel(x_ref, a_ref, c_ref, mod_ref, n2_ref, wo_ref, w1_ref, w2_ref, o_ref):
    aw = a_ref.shape[2]
    mix = jnp.dot(a_ref[0], wo_ref[0:aw, :], preferred_element_type=F32)
    mix = mix + jnp.dot(c_ref[0], wo_ref[aw:, :], preferred_element_type=F32)
    x1 = x_ref[0] + mod_ref[0, 2:3, :] * mix
    ms = jnp.mean(x1 * x1, axis=-1, keepdims=True)
    y = x1 * lax.rsqrt(ms + EPS) * n2_ref[...]
    h2 = (y * (1.0 + mod_ref[0, 4:5, :]) + mod_ref[0, 3:4, :]).astype(BF16)
    d_ff = w1_ref.shape[1]
    ff = jnp.zeros(x1.shape, F32)
    for j in range(d_ff // FF_CHUNK):
        a = jnp.dot(h2, w1_ref[:, j * FF_CHUNK:(j + 1) * FF_CHUNK], preferred_element_type=F32)
        a = jnp.maximum(a, 0.0)
        ff = ff + jnp.dot((a * a).astype(BF16), w2_ref[j * FF_CHUNK:(j + 1) * FF_CHUNK, :],
                          preferred_element_type=F32)
    o_ref[0] = x1 + mod_ref[0, 5:6, :] * ff


def _ffn_call(x, attn, conv, mod, norm2, w_out, w_ff1, w_ff2):
    b, t, d = x.shape
    tm = min(ROW_TILE, t)
    row_spec = lambda w: pl.BlockSpec((1, tm, w), lambda i, j: (i, j, 0))
    full = lambda shp: pl.BlockSpec(shp, lambda i, j: (0,) * len(shp),
                                    pipeline_mode=pl.Buffered(1))
    return pl.pallas_call(
        _ffn_kernel,
        grid=(b, t // tm),
        in_specs=[row_spec(d), row_spec(attn.shape[2]), row_spec(conv.shape[2]),
                  pl.BlockSpec((1, N_MOD, d), lambda i, j: (i, 0, 0)),
                  full((1, d)), full(w_out.shape), full(w_ff1.shape), full(w_ff2.shape)],
        out_specs=row_spec(d),
        out_shape=jax.ShapeDtypeStruct((b, t, d), F32),
        compiler_params=_cparams(2),
        name="outproj_mlp",
    )(x, attn, conv, mod, norm2, w_out, w_ff1, w_ff2)


def _block_diag_ones(n):
    idx = np.arange(n) // HEAD_DIM
    return jnp.asarray(idx[:, None] == idx[None, :], dtype=BF16)


def _pack_w_in(w_in):
    d = w_in.shape[0]
    conv_w = d - ATTN_W
    sizes = [ATTN_W] + [KV_W] * 6 + [N_BRANCH * N_HEADS] + [conv_w] * 3
    offs = np.concatenate([[0], np.cumsum(sizes)])
    part = lambda i: w_in[:, offs[i]:offs[i + 1]]
    q, kc, vc, ks, vs, kw, vw, g, cgate, bgate, u = (part(i) for i in range(11))
    w_nat = jnp.concatenate([kc, vc, ks, kw, cgate, bgate, u], axis=1).astype(BF16)
    gt = g.reshape(d, N_KV, GQA, N_BRANCH).transpose(2, 3, 1, 0).reshape(GQA, N_BRANCH * N_KV, d)
    gt = jnp.pad(gt, ((0, 0), (0, SUBLANES - N_BRANCH * N_KV), (0, 0))).reshape(_G_ROWS, d)
    w_tr = jnp.concatenate([q.T, vs.T, vw.T, gt], axis=0).astype(BF16)
    return w_nat, w_tr


def _expand_w1(w1):
    hid = w1.shape[1]
    w = w1.reshape(2, CMP_STRIDE, HEAD_DIM, hid).astype(BF16)
    zero = jnp.zeros_like(w)
    per_group = [jnp.concatenate([w if k == g else zero for k in range(N_KV)], axis=-1)
                 for g in range(N_KV)]
    return jnp.stack(per_group, axis=2).reshape(2, CMP_STRIDE * KV_W, N_KV * hid)


def _expand_w2(w2):
    hid = w2.shape[0]
    eye = jnp.eye(N_KV, dtype=w2.dtype).reshape(N_KV, 1, N_KV, 1)
    return (w2.reshape(1, hid, 1, HEAD_DIM) * eye).reshape(N_KV * hid, KV_W).astype(BF16)


def _expand_pe(pe):
    p = pe.reshape(2, CMP_STRIDE, 1, HEAD_DIM)
    return jnp.broadcast_to(p, (2, CMP_STRIDE, N_KV, HEAD_DIM)).reshape(2, CMP_STRIDE * KV_W)


def _bucket_thresholds():
    n = np.arange(2 * REL_MAX_DIST)
    max_exact = REL_BUCKETS // 2
    nf = np.maximum(n, max_exact).astype(np.float32)
    ratio = np.log(nf / np.float32(max_exact)) / np.float32(math.log(REL_MAX_DIST / max_exact))
    large = max_exact + (ratio * np.float32(REL_BUCKETS - max_exact)).astype(np.int32)
    table = np.where(n < max_exact, n, np.minimum(large, REL_BUCKETS - 1))
    return tuple(int(np.searchsorted(table, k, side="left")) for k in range(REL_BUCKETS))


def _bank_call(bias_rows, lead, n_tiles, dist_fn, name):
    nl = len(lead)
    thr = _bucket_thresholds()

    def body(rows_ref, o_ref):
        lead_ids = [pl.program_id(a) for a in range(nl)]
        row = lax.broadcasted_iota(jnp.int32, (SEL_BLOCK, GQ), 0)
        qi = lax.broadcasted_iota(jnp.int32, (SEL_BLOCK, GQ), 1) & (Q_BLOCK - 1)

        def tile(t, carry):
            dist, ok = dist_fn(lead_ids, t, row, qi)
            v = jnp.broadcast_to(rows_ref[0, 0:1, :], (SEL_BLOCK, GQ))
            for k in range(1, REL_BUCKETS):
                v = jnp.where(dist >= thr[k], rows_ref[0, k:k + 1, :], v)
            o_ref[(0,) * (nl + 1) + (t,)] = jnp.where(ok, v, NEG)
            return carry

        lax.fori_loop(0, n_tiles, tile, 0)

    return pl.pallas_call(
        body,
        grid=tuple(lead) + (N_KV,),
        in_specs=[pl.BlockSpec((1, REL_BUCKETS, GQ), lambda *i: (i[nl], 0, 0))],
        out_specs=pl.BlockSpec((1,) * (nl + 1) + (n_tiles, SEL_BLOCK, GQ),
                               lambda *i: tuple(i) + (0, 0, 0)),
        out_shape=jax.ShapeDtypeStruct(tuple(lead) + (N_KV, n_tiles, SEL_BLOCK, GQ), F32),
        compiler_params=_cparams(nl + 1),
        name=name,
    )(bias_rows)


def _bias_banks(rel_bias, t):
    n_cmp_rows = t // CMP_STRIDE
    rows = rel_bias.reshape(REL_BUCKETS, N_KV, GQA).transpose(1, 0, 2)
    rows = jnp.repeat(rows, Q_BLOCK, axis=2) * LOG2E

    def sel_dist(lead, tile, row, qi):
        dist = SEL_BLOCK * tile + qi - row
        return dist, dist >= 0

    sbank = _bank_call(rows, (), NEAR_TILES + 1, sel_dist, "bias_bank_sel")

    def win_dist(lead, tile, row, qi):
        dist = SEL_BLOCK * tile + qi - row
        return dist, (dist >= 0) & (dist < WINDOW)

    wbank = _bank_call(rows, (), WIN_BLOCKS + 2, win_dist, "bias_bank_win")

    cmp_c = n_cmp_rows - 4
    shifts = tuple(int((-(cmp_c - 4 * p)) % 8) for p in range(2))
    n_tiles = (cmp_c + n_cmp_rows + 8 + SEL_BLOCK - 1) // SEL_BLOCK

    def cmp_dist(lead, tile, row, qi):
        e = SEL_BLOCK * tile + row - jnp.where(lead[0] == 0, shifts[0], shifts[1])
        dist = qi - CMP_STRIDE * e + (CMP_STRIDE * cmp_c - (CMP_BLOCK - 1))
        return dist, (dist >= 0) & (e >= 0)

    ebank = _bank_call(rows, (2,), n_tiles, cmp_dist, "bias_bank_cmp")
    ebank = ebank.reshape(2, N_KV, n_tiles * SEL_BLOCK, GQ)
    return ebank, sbank, wbank, shifts


def _overlap_t(t):
    n_cmp_rows = t // CMP_STRIDE
    n_sb = t // SEL_BLOCK
    c_start = np.arange(n_cmp_rows)[None, :] * CMP_STRIDE
    s_start = np.arange(n_sb)[:, None] * SEL_BLOCK
    ov = np.clip(np.minimum(c_start + CMP_BLOCK, s_start + SEL_BLOCK)
                 - np.maximum(c_start, s_start), 0, None) / CMP_BLOCK
    ov[:, n_cmp_rows - 1] = 0.0
    return jnp.asarray(ov, dtype=BF16)


def _layer(x, c_pad, w_in, q_norm, k_norm, cmp_pe_k, cmp_w1_k, cmp_w2_k, cmp_pe_v, cmp_w1_v,
           cmp_w2_v, rel_bias, conv_w, w_out, norm1, norm2, w_ada, b_ada, w_ff1, w_ff2):
    b, t, d = x.shape
    scale = HEAD_DIM ** -0.5

    mod = _mod_call(c_pad, w_ada, b_ada)[:b].reshape(b, N_MOD, d)

    qn_col = (jnp.tile(q_norm, N_HEADS) * (scale * LOG2E)).reshape(ATTN_W, 1)
    kn_t = jnp.tile(k_norm, N_KV).reshape(1, KV_W)
    bdq = _block_diag_ones(ATTN_W)
    bdk = _block_diag_ones(KV_W)
    w_nat, w_tr = _pack_w_in(w_in)
    qt, kc_raw, vc_raw, ks, vst, kw, vwt, gt, conv = _inproj_call(
        x, mod, norm1.reshape(1, d), w_nat, w_tr, qn_col, kn_t, conv_w, bdq, bdk)

    kc, vct = _compress_call(
        kc_raw, vc_raw, _expand_pe(cmp_pe_k), _expand_pe(cmp_pe_v), _expand_w1(cmp_w1_k), _expand_w1(cmp_w1_v),
        _expand_w2(cmp_w2_k), _expand_w2(cmp_w2_v).T, kn_t, bdk)

    ebank, sbank, wbank, eb_shift = _bias_banks(rel_bias, t)
    attn = _attn_call(qt, kc, vct, ks, vst, kw, vwt, gt, _overlap_t(t), ebank, sbank.astype(BF16),
                      wbank.astype(BF16), eb_shift)

    return _ffn_call(x, attn, conv, mod, norm2.reshape(1, d), w_out.astype(BF16),
                     w_ff1.astype(BF16), w_ff2.astype(BF16))


def kernel(x, c, w_in, q_norm, k_norm, cmp_pe_k, cmp_w1_k, cmp_w2_k, cmp_pe_v, cmp_w1_v, cmp_w2_v,
           rel_bias, conv_w, w_out, norm1, norm2, w_ada, b_ada, w_ff1, w_ff2):
    b = x.shape[0]
    c_pad = jnp.pad(c, ((0, (-b) % 8), (0, 0)))
    for l in range(w_in.shape[0]):
        x = _layer(x, c_pad, w_in[l], q_norm[l], k_norm[l], cmp_pe_k[l], cmp_w1_k[l], cmp_w2_k[l],
                   cmp_pe_v[l], cmp_w1_v[l], cmp_w2_v[l], rel_bias, conv_w[l], w_out[l],
                   norm1[l], norm2[l], w_ada[l], b_ada[l], w_ff1[l], w_ff2[l])
    return x
```

```python
import functools
import math

import numpy as np
import jax
import jax.numpy as jnp
from jax import lax
from jax.experimental import pallas as pl
from jax.experimental.pallas import tpu as pltpu

HEAD_DIM = 64
N_HEADS = 8
N_KV = 2
GQA = N_HEADS // N_KV
ATTN_W = N_HEADS * HEAD_DIM
KV_W = N_KV * HEAD_DIM
CONV_K = 3
CMP_BLOCK = 32
CMP_STRIDE = 16
SEL_BLOCK = 64
N_SELECT = 16
WINDOW = 512
Q_BLOCK = 64
REL_BUCKETS = 32
REL_MAX_DIST = 1024
N_MOD = 6
N_BRANCH = 3
EPS = 1e-6
NEG = -1e30

LANES = 128
SUBLANES = 8
GQ = GQA * Q_BLOCK
KEY_CHUNK = 256
BLOCKS_PER_CHUNK = KEY_CHUNK // SEL_BLOCK
V_TILE = 128
BF16_ROWS = 16
V_ROWS = HEAD_DIM + BF16_ROWS
V_TILE_ROWS = N_KV * V_ROWS
CMP_ROW_STEP = 64
WIN_BLOCKS = WINDOW // SEL_BLOCK
WIN_TILES = WINDOW // V_TILE + 1
NEAR_TILES = (REL_MAX_DIST + Q_BLOCK - 1) // SEL_BLOCK + 1
ROW_TILE = 512
INPROJ_TILE = 1024
INPROJ_SPLIT = 4
BIG_CHUNKS = 16
TAIL_CHUNKS = 1
assert BIG_CHUNKS % TAIL_CHUNKS == 0
MERGE_CHUNKS = 2
QK_AHEAD = 4
QK_DEPTH = 5
SMALLEST_IMPORTANCE = 2.0 ** -100
LOG2E = math.log2(math.e)
FF_CHUNK = 1024
VMEM_LIMIT = 56 * 1024 * 1024

F32 = jnp.float32
BF16 = jnp.bfloat16
_NT = (((1,), (1,)), ((), ()))


def _cparams(n_axes):
    return pltpu.CompilerParams(dimension_semantics=("arbitrary",) * n_axes,
                                vmem_limit_bytes=VMEM_LIMIT)


def _with_ones_rows(vt):
    ones = jnp.ones((BF16_ROWS, vt.shape[1]), vt.dtype)
    parts = []
    for g in range(N_KV):
        parts += [vt[g * HEAD_DIM:(g + 1) * HEAD_DIM, :], ones]
    return jnp.concatenate(parts, axis=0)


def _swap_halves(p0, p1):
    low = lax.broadcasted_iota(jnp.int32, p0.shape, 1) < LANES // 2
    return (jnp.where(low, p0, pltpu.roll(p1, LANES // 2, 1)),
            jnp.where(low, pltpu.roll(p0, LANES // 2, 1), p1))


def _mod_kernel(c_ref, w_ref, b_ref, o_ref):
    c = c_ref[...]
    a = c * jax.nn.sigmoid(c)
    o_ref[...] = jnp.dot(a, w_ref[...], preferred_element_type=F32,
                         precision=lax.Precision.HIGHEST) + b_ref[...]


def _mod_call(c_pad, w_ada, b_ada):
    rows, d = c_pad.shape
    n = w_ada.shape[1]
    tn = n // N_MOD
    return pl.pallas_call(
        _mod_kernel,
        grid=(n // tn,),
        in_specs=[pl.BlockSpec((rows, d), lambda j: (0, 0)),
                  pl.BlockSpec((d, tn), lambda j: (0, j)),
                  pl.BlockSpec((1, tn), lambda j: (0, j))],
        out_specs=pl.BlockSpec((rows, tn), lambda j: (0, j)),
        out_shape=jax.ShapeDtypeStruct((rows, n), F32),
        compiler_params=_cparams(1),
        name="adaln_mod",
    )(c_pad, w_ada, b_ada.reshape(1, n))


_N_KC, _N_VC, _N_KS, _N_KW, _N_CONV = 0, KV_W, 2 * KV_W, 3 * KV_W, 4 * KV_W
_T_Q, _T_VS, _T_VW, _T_G = 0, ATTN_W, ATTN_W + KV_W, ATTN_W + 2 * KV_W
_G_ROWS = GQA * SUBLANES


def _inproj_kernel(x_ref, mod_ref, n1_ref, wn_ref, wt_ref, qn_ref, kn_ref, cw_ref, bdq_ref, bdk_ref,
                   qt_out, kc_out, vc_out, ks_out, vst_out, kw_out, vwt_out, gt_out, conv_out,
                   carry, *, conv_w):
    t = pl.program_id(1)
    tm = x_ref.shape[1]
    sub = tm // INPROJ_SPLIT

    @pl.when(t == 0)
    def _():
        carry[...] = jnp.zeros_like(carry)

    prev2, prev1 = carry[6:7, :], carry[7:8, :]
    c_bg = _N_CONV + conv_w
    c_u = c_bg + conv_w

    for part in range(INPROJ_SPLIT):
        rows = slice(part * sub, (part + 1) * sub)
        x = x_ref[0, rows, :]
        ms = jnp.mean(x * x, axis=-1, keepdims=True)
        y = x * lax.rsqrt(ms + EPS) * n1_ref[...]
        h = (y * (1.0 + mod_ref[0, 1:2, :]) + mod_ref[0, 0:1, :]).astype(BF16)

        def proj(a, b):
            return jnp.dot(h, wn_ref[:, a:b], preferred_element_type=F32)

        def proj_t(a, b):
            return lax.dot_general(wt_ref[a:b, :], h, _NT, preferred_element_type=F32)

        def head_norm(v, gain):
            ssq = jnp.dot((v * v).astype(BF16), bdk_ref[...], preferred_element_type=F32)
            return v * lax.rsqrt(ssq * (1.0 / HEAD_DIM) + EPS) * gain

        qf = proj_t(_T_Q, _T_VS)
        vg = proj_t(_T_VS, _T_G + _G_ROWS)
        kv = proj(_N_KC, _N_CONV)
        cv = proj(_N_CONV, c_u + conv_w)

        ssq = jnp.dot(bdq_ref[...], (qf * qf).astype(BF16), preferred_element_type=F32)
        qf = qf * lax.rsqrt(ssq * (1.0 / HEAD_DIM) + EPS) * qn_ref[...]
        gf = jax.nn.sigmoid(vg[2 * KV_W:, :])
        for c in range(sub // LANES):
            blk = (part * sub) // Q_BLOCK + 2 * c
            cols = slice(c * LANES, (c + 1) * LANES)
            for g in range(N_KV):
                pc = [qf[(g * GQA + r) * HEAD_DIM:(g * GQA + r + 1) * HEAD_DIM, cols]
                      for r in range(GQA)]
                lo01, hi01 = _swap_halves(pc[0], pc[1])
                lo23, hi23 = _swap_halves(pc[2], pc[3])
                qt_out[0, blk, g] = jnp.concatenate([lo01, lo23], axis=1).astype(BF16)
                qt_out[0, blk + 1, g] = jnp.concatenate([hi01, hi23], axis=1).astype(BF16)
            pc = [gf[r * SUBLANES:(r + 1) * SUBLANES, cols] for r in range(GQA)]
            lo01, hi01 = _swap_halves(pc[0], pc[1])
            lo23, hi23 = _swap_halves(pc[2], pc[3])
            gt_out[0, blk] = jnp.concatenate([lo01, lo23], axis=1)
            gt_out[0, blk + 1] = jnp.concatenate([hi01, hi23], axis=1)

        vs_f = _with_ones_rows(vg[:KV_W, :].astype(BF16))
        vw_f = _with_ones_rows(vg[KV_W:2 * KV_W, :].astype(BF16))
        for j in range(sub // KEY_CHUNK):
            vst_out[0, (part * sub) // KEY_CHUNK + j] = vs_f[:, j * KEY_CHUNK:(j + 1) * KEY_CHUNK]
        for j in range(sub // V_TILE):
            vwt_out[0, (part * sub) // V_TILE + j] = vw_f[:, j * V_TILE:(j + 1) * V_TILE]

        kc_out[0, rows, :] = kv[:, _N_KC:_N_VC]
        vc_out[0, rows, :] = kv[:, _N_VC:_N_KS]
        ks_out[0, rows, :] = head_norm(kv[:, _N_KS:_N_KW], kn_ref[...]).astype(BF16)
        kw_out[0, rows, :] = head_norm(kv[:, _N_KW:_N_CONV], kn_ref[...]).astype(BF16)

        z = cv[:, :conv_w] * cv[:, 2 * conv_w:]
        row = lax.broadcasted_iota(jnp.int32, z.shape, 0)
        z1 = jnp.where(row == 0, prev1, pltpu.roll(z, 1, 0))
        z2 = jnp.where(row == 0, prev2, jnp.where(row == 1, prev1, pltpu.roll(z, 2, 0)))
        zc = cw_ref[0:1, :] * z2 + cw_ref[1:2, :] * z1 + cw_ref[2:3, :] * z
        conv_out[0, rows, :] = (cv[:, conv_w:2 * conv_w] * zc).astype(BF16)
        prev2, prev1 = z[sub - 2:sub - 1, :], z[sub - 1:sub, :]
        if part == INPROJ_SPLIT - 1:
            carry[...] = z[sub - SUBLANES:sub, :]


def _inproj_call(x, mod, norm1, w_nat, w_tr, qn_col, kn_t, conv_w, bdq, bdk):
    b, t, d = x.shape
    tm = min(INPROJ_TILE, t)
    cw = conv_w.shape[1]
    nq = t // Q_BLOCK
    row_spec = lambda w: pl.BlockSpec((1, tm, w), lambda i, j: (i, j, 0))
    full = lambda shp: pl.BlockSpec(shp, lambda i, j: (0,) * len(shp))
    vt_spec = lambda w: pl.BlockSpec((1, tm // w, V_TILE_ROWS, w), lambda i, j: (i, j, 0, 0))
    vt_shape = lambda w: jax.ShapeDtypeStruct((b, t // w, V_TILE_ROWS, w), BF16)
    kv = lambda dt: jax.ShapeDtypeStruct((b, t, KV_W), dt)
    out_specs = [pl.BlockSpec((1, tm // Q_BLOCK, N_KV, HEAD_DIM, GQ), lambda i, j: (i, j, 0, 0, 0)),
                 row_spec(KV_W), row_spec(KV_W), row_spec(KV_W), vt_spec(KEY_CHUNK), row_spec(KV_W),
                 vt_spec(V_TILE),
                 pl.BlockSpec((1, tm // Q_BLOCK, SUBLANES, GQ), lambda i, j: (i, j, 0, 0)),
                 row_spec(cw)]
    out_shape = [jax.ShapeDtypeStruct((b, nq, N_KV, HEAD_DIM, GQ), BF16),
                 kv(F32), kv(F32), kv(BF16), vt_shape(KEY_CHUNK), kv(BF16), vt_shape(V_TILE),
                 jax.ShapeDtypeStruct((b, nq, SUBLANES, GQ), F32),
                 jax.ShapeDtypeStruct((b, t, cw), BF16)]
    return pl.pallas_call(
        functools.partial(_inproj_kernel, conv_w=cw),
        grid=(b, t // tm),
        in_specs=[row_spec(d),
                  pl.BlockSpec((1, N_MOD, d), lambda i, j: (i, 0, 0)),
                  full((1, d)), full(w_nat.shape), full(w_tr.shape), full((ATTN_W, 1)),
                  full((1, KV_W)), full((CONV_K, cw)), full((ATTN_W, ATTN_W)), full((KV_W, KV_W))],
        out_specs=out_specs,
        out_shape=out_shape,
        scratch_shapes=[pltpu.VMEM((SUBLANES, cw), F32)],
        compiler_params=_cparams(2),
        name="inproj",
    )(x, mod, norm1, w_nat, w_tr, qn_col, kn_t, conv_w, bdq, bdk)


def _compress_kernel(kx_ref, vx_ref, pek_ref, pev_ref, w1k_ref, w1v_ref, w2k_ref, w2vt_ref,
                     kn_ref, bdk_ref, kc_out, vct_out):
    def hidden(x_ref, pe_ref, w1_ref):
        n = x_ref.shape[1] // CMP_STRIDE
        u = jnp.zeros((n, w1_ref.shape[2]), F32)
        v = jnp.zeros((n, w1_ref.shape[2]), F32)
        for r in range(0, CMP_STRIDE, 2):
            tok = [x_ref[0, pl.ds(r + d, n, stride=CMP_STRIDE), :] for d in range(2)]
            cols = slice(r * KV_W, (r + 2) * KV_W)
            for a, acc in ((0, "u"), (1, "v")):
                lhs = jnp.concatenate([tok[d] + pe_ref[a:a + 1, (r + d) * KV_W:(r + d + 1) * KV_W]
                                       for d in range(2)], axis=1).astype(BF16)
                prod = jnp.dot(lhs, w1_ref[a, cols, :], preferred_element_type=F32)
                if acc == "u":
                    u = u + prod
                else:
                    v = v + prod
        hid = u + pltpu.roll(v, n - 1, 0)
        return jax.nn.gelu(hid, approximate=True).astype(BF16)

    kc = jnp.dot(hidden(kx_ref, pek_ref, w1k_ref), w2k_ref[...], preferred_element_type=F32)
    ssq = jnp.dot((kc * kc).astype(BF16), bdk_ref[...], preferred_element_type=F32)
    kc_out[0] = (kc * lax.rsqrt(ssq * (1.0 / HEAD_DIM) + EPS) * kn_ref[...]).astype(BF16)
    vct_out[0] = _with_ones_rows(lax.dot_general(w2vt_ref[...], hidden(vx_ref, pev_ref, w1v_ref), _NT,
                                                 preferred_element_type=F32).astype(BF16))


def _compress_call(kx, vx, pek, pev, w1k, w1v, w2k, w2vt, kn_t, bdk):
    b, t, _ = kx.shape
    nrow = t // CMP_STRIDE
    wide, hid2 = w1k.shape[1], w1k.shape[2]
    full = lambda shp: pl.BlockSpec(shp, lambda i: (0,) * len(shp))
    xs = pl.BlockSpec((1, t, KV_W), lambda i: (i, 0, 0))
    return pl.pallas_call(
        _compress_kernel,
        grid=(b,),
        in_specs=[xs, xs, full((2, wide)), full((2, wide)), full((2, wide, hid2)),
                  full((2, wide, hid2)), full((hid2, KV_W)), full((KV_W, hid2)),
                  full((1, KV_W)), full((KV_W, KV_W))],
        out_specs=[pl.BlockSpec((1, nrow, KV_W), lambda i: (i, 0, 0)),
                   pl.BlockSpec((1, V_TILE_ROWS, nrow), lambda i: (i, 0, 0))],
        out_shape=[jax.ShapeDtypeStruct((b, nrow, KV_W), BF16),
                   jax.ShapeDtypeStruct((b, V_TILE_ROWS, nrow), BF16)],
        compiler_params=_cparams(1),
        name="compress",
    )(kx, vx, pek, pev, w1k, w1v, w2k, w2vt, kn_t, bdk)


def _attn_kernel(q_ref, kc_ref, vct_ref, ks_ref, vst_ref, kw_ref, vwt_ref, gt_ref, ovt_ref,
                 eb_ref, sb_ref, wb_ref, o_ref, sel_scr, acc_scr, out_scr, qp_scr, s_scr,
                 fin_scr, *, n_cmp_rows, eb_shift):
    n_sb = sel_scr.shape[1]
    write_out_only = pl.program_id(1) >= n_sb
    ci = jnp.minimum(pl.program_id(1), n_sb - 1)
    par = lax.rem(ci, 2)
    n_total = ks_ref.shape[1] // KEY_CHUNK
    vsl = lambda g: slice(g * V_ROWS, (g + 1) * V_ROWS)

    @pl.when((pl.program_id(0) == 0) & (pl.program_id(1) == 0))
    def _():
        def zero(g, carry):
            fin_scr[g] = jnp.zeros(fin_scr.shape[1:], F32)
            sel_scr[g] = jnp.zeros(sel_scr.shape[1:], F32)
            return carry
        lax.fori_loop(0, N_KV, zero, 0)

    def chunk_qk(g, c):
        kk = ks_ref[0, pl.ds(pl.multiple_of(c * KEY_CHUNK, KEY_CHUNK), KEY_CHUNK), :]
        return jnp.dot(kk, qp_scr[g], preferred_element_type=F32).astype(BF16)

    n_chunks = jnp.where(write_out_only, 0, ci // BLOCKS_PER_CHUNK + 1)
    tails_per_big = BIG_CHUNKS // TAIL_CHUNKS
    n_tails = (n_chunks + TAIL_CHUNKS - 1) // TAIL_CHUNKS
    n_big = n_tails // tails_per_big
    tail = lax.rem(n_tails, tails_per_big)
    tail_start = n_big * BIG_CHUNKS

    def big_units(i):
        return [(g, jnp.minimum(BIG_CHUNKS * i + j, n_total - 1))
                for g in range(N_KV) for j in range(BIG_CHUNKS)]

    def tail_units(size):
        return [(g, jnp.minimum(tail_start + j, n_total - 1))
                for g in range(N_KV) for j in range(size * TAIL_CHUNKS)]

    short = [s for s in range(1, tails_per_big) if s * TAIL_CHUNKS < QK_AHEAD]
    tail_options = [(tail == s, tail_units(s)) for s in short] + [(None, tail_units(tails_per_big - 1))]

    def pick_units(options):
        pad = lambda units: (units + [units[-1]] * QK_AHEAD)[:QK_AHEAD]
        units = pad(options[-1][1])
        for cond, cand in reversed(options[:-1]):
            units = [(jnp.where(cond, g_a, g_b), jnp.where(cond, c_a, c_b))
                     for (g_a, c_a), (g_b, c_b) in zip(pad(cand), units)]
        return [(g, jnp.minimum(c, n_total - 1)) for g, c in units]

    def prefetch_scores(units):
        for k in range(QK_AHEAD):
            s_scr[k] = chunk_qk(*units[k])

    def write_out():
        pieces = []
        for g in range(N_KV):
            for half in range(GQ // LANES):
                a = fin_scr[g, :, half * LANES:(half + 1) * LANES]
                stacked = jnp.concatenate([a, pltpu.roll(a, Q_BLOCK, 1)], axis=0)
                pieces.append(stacked.T[:Q_BLOCK, :])
        o_ref[0] = jnp.concatenate(pieces, axis=1).astype(BF16)

    def before_loops(rows):
        write_out()

        zeros_q = jnp.zeros((HEAD_DIM, GQ), BF16)
        q_padded = [jnp.concatenate([q_ref[0, 0, 0], zeros_q], axis=0),
                    jnp.concatenate([zeros_q, q_ref[0, 0, 1]], axis=0)]
        for g in range(N_KV):
            qp_scr[g] = q_padded[g]

        nsel = min(n_sb, rows * CMP_STRIDE // SEL_BLOCK)
        lane = lax.broadcasted_iota(jnp.int32, (nsel, LANES), 1)
        cmp_c = n_cmp_rows - 4
        e0 = cmp_c - 4 * ci + jnp.where(par == 0, eb_shift[0], eb_shift[1])
        e0 = pl.multiple_of(e0, 8)
        o_cmp = []
        imp = []
        w0 = ci // 2 - (WIN_TILES - 1)
        win_tiles = [jnp.maximum(w0 + j, 0) for j in range(WIN_TILES)]
        s_cmp = jnp.dot(kc_ref[0, :rows, :], jnp.concatenate(q_padded, axis=1),
                        preferred_element_type=F32)

        def window_scores():
            return [[jnp.dot(kw_ref[0, pl.ds(pl.multiple_of(tj * V_TILE, V_TILE), V_TILE), :],
                             q_padded[g], preferred_element_type=F32).astype(BF16) for tj in win_tiles]
                    for g in range(N_KV)]

        window_first = 2 * rows <= n_cmp_rows
        if window_first:
            s_win = window_scores()
        for g in range(N_KV):
            sc = s_cmp[:, g * GQ:(g + 1) * GQ] + eb_ref[par, g, pl.ds(e0, rows), :]
            m = jnp.max(sc, axis=0, keepdims=True)
            e = jnp.exp2(sc - m).astype(BF16)
            ov = jnp.dot(vct_ref[0, vsl(g), :rows], e, preferred_element_type=F32)
            inv = jnp.where(m > 0.5 * NEG, 1.0 / ov[HEAD_DIM:HEAD_DIM + 1, :], 0.0)
            o_cmp.append(ov[:HEAD_DIM, :] * inv)
            ir = jnp.dot(ovt_ref[:nsel, :rows], e, preferred_element_type=F32) * inv
            a = ir[:, :LANES] + ir[:, LANES:]
            imp.append(a + pltpu.roll(a, Q_BLOCK, 1))
        if not window_first:
            s_win = window_scores()

        prefetch_scores(pick_units([(n_big > 0, big_units(0))] + tail_options))

        jidx = lax.broadcasted_iota(jnp.int32, (nsel, LANES), 0)
        valid = jidx <= ci
        forced = (jidx == 0) | (jidx == ci) | (jidx == ci - 1)
        index_mask = (1 << max(1, (n_sb - 1).bit_length())) - 1
        imp_bits = lax.bitcast_convert_type(
            jnp.maximum(jnp.where(lane < Q_BLOCK, imp[0], imp[1]), SMALLEST_IMPORTANCE), jnp.int32)
        keyed = lax.bitcast_convert_type((imp_bits & ~index_mask) | (index_mask - jidx), F32)
        score = jnp.where(valid, jnp.where(forced, -2.0, keyed), -1.0)
        for _ in range(min(N_SELECT, nsel) - 3):
            score = jnp.where(score == jnp.max(score, axis=0, keepdims=True), -2.0, score)
        selneg = jnp.where((score < -1.5) & valid, 0.0, NEG)
        swapped = pltpu.roll(selneg, Q_BLOCK, 1)
        left = jnp.where(lane < Q_BLOCK, selneg, swapped)
        right = jnp.where(lane < Q_BLOCK, swapped, selneg)
        sel_scr[0, :nsel] = jnp.concatenate([left, left], axis=1)
        sel_scr[1, :nsel] = jnp.concatenate([right, right], axis=1)

        for g in range(N_KV):
            parts = []
            for i in range(2 * WIN_TILES):
                delta = par + WIN_BLOCKS - i
                ok = (delta >= 0) & (delta <= WIN_BLOCKS) & (delta <= ci)
                tile = jnp.where(ok, delta, WIN_BLOCKS + 1)
                half = s_win[g][i // 2][(i % 2) * SEL_BLOCK:(i % 2 + 1) * SEL_BLOCK, :]
                parts.append(half + wb_ref[g, tile])
            s = jnp.concatenate(parts, axis=0)
            m = jnp.max(s, axis=0, keepdims=True)
            pb = jnp.exp2(s - m)
            o_win = jnp.zeros((V_ROWS, GQ), F32)
            for j, tj in enumerate(win_tiles):
                o_win = o_win + jnp.dot(vwt_ref[0, tj, vsl(g), :], pb[j * V_TILE:(j + 1) * V_TILE, :],
                                        preferred_element_type=F32)
            w_scale = (gt_ref[0, 0, 2 * N_KV + g:2 * N_KV + g + 1, :]
                       * (1.0 / o_win[HEAD_DIM:HEAD_DIM + 1, :]))
            out_scr[g] = gt_ref[0, 0, g:g + 1, :] * o_cmp[g] + w_scale * o_win[:HEAD_DIM, :]
            acc_scr[g] = jnp.zeros((V_ROWS, GQ), F32)

    row_steps = [r for r in range(CMP_ROW_STEP, n_cmp_rows + 1, CMP_ROW_STEP)] or [n_cmp_rows]
    variant = jnp.minimum((4 * ci + 2) // CMP_ROW_STEP, len(row_steps) - 1)
    variant = jnp.where(write_out_only, len(row_steps), variant)
    lax.switch(variant, [functools.partial(before_loops, r) for r in row_steps] + [write_out])

    def chunk_softmax(s, g, c, extra):
        parts = []
        for i in range(BLOCKS_PER_CHUNK):
            kb = c * BLOCKS_PER_CHUNK + i
            blk = s[i * SEL_BLOCK:(i + 1) * SEL_BLOCK, :]
            mrow = sel_scr[g, pl.ds(kb, 1), :]
            if extra is not None:
                mrow = mrow + extra
            tile = jnp.clip(ci - kb, 0, NEAR_TILES)
            parts.append(blk + sb_ref[g, tile] + mrow.astype(BF16))
        s = jnp.concatenate(parts, axis=0)
        m_c = jnp.max(s, axis=0, keepdims=True)
        return m_c.astype(F32), jnp.exp2(s - m_c)

    def chunk_pv(pb, g, c):
        return jnp.dot(vst_ref[0, c, vsl(g), :], pb, preferred_element_type=F32)

    def merge_step(units, extras, next_units, carry):
        parked = min(QK_AHEAD, len(units))
        scores = {k: s_scr[k] for k in range(parked)}
        pending = list(range(parked, len(units) + (0 if next_units is None else QK_AHEAD)))

        def issue_scores():
            k = pending.pop(0)
            if k < len(units):
                scores[k] = chunk_qk(*units[k])
            else:
                s_scr[k - len(units)] = chunk_qk(*next_units[k - len(units)])

        for _ in range(min(QK_DEPTH - QK_AHEAD, len(pending))):
            issue_scores()
        results = []
        for k, (g, c) in enumerate(units):
            m_c, pb = chunk_softmax(scores.pop(k), g, c, extras[k])
            if pending:
                issue_scores()
            results.append((m_c, chunk_pv(pb, g, c)))
        per_g = len(units) // N_KV
        new = []
        for g in range(N_KV):
            m_run = carry[g]
            acc = acc_scr[g]
            for first in range(g * per_g, (g + 1) * per_g, MERGE_CHUNKS):
                stats = results[first:min(first + MERGE_CHUNKS, (g + 1) * per_g)]
                m_new = m_run
                for m_c, _ in stats:
                    m_new = jnp.maximum(m_new, m_c)
                acc = jnp.exp2(m_run - m_new) * acc
                for m_c, pv in stats:
                    acc = acc + jnp.exp2(m_c - m_new) * pv
                m_run = m_new
            acc_scr[g] = acc
            new.append(m_run)
        return tuple(new)

    def big_step(i, carry):
        nxt = pick_units([(i + 1 < n_big, big_units(i + 1))] + tail_options)
        extras = [jnp.where(BIG_CHUNKS * i + j < n_chunks, 0.0, NEG)
                  if j > BIG_CHUNKS - TAIL_CHUNKS else None
                  for _ in range(N_KV) for j in range(BIG_CHUNKS)]
        return merge_step(big_units(i), extras, nxt, carry)

    carry = lax.fori_loop(0, n_big, big_step, (jnp.full((1, GQ), NEG, F32),) * N_KV)

    def tail_step(size):
        extras = [None if j == 0 else jnp.where(tail_start + j < n_chunks, 0.0, NEG)
                  for _ in range(N_KV) for j in range(size * TAIL_CHUNKS)]
        merge_step(tail_units(size), extras, None, carry)

    def run_tail(lo, hi):
        if hi - lo == 1:
            if lo > 0:
                tail_step(lo)
        else:
            mid = (lo + hi) // 2
            lax.cond(tail < mid, functools.partial(run_tail, lo, mid), functools.partial(run_tail, mid, hi))

    run_tail(0, tails_per_big)

    for g in range(N_KV):
        acc = acc_scr[g]
        scale = gt_ref[0, 0, N_KV + g:N_KV + g + 1, :] * (1.0 / acc[HEAD_DIM:HEAD_DIM + 1, :])
        fin_scr[g] = out_scr[g] + scale * acc[:HEAD_DIM, :]


def _attn_call(qt, kc, vct, ks, vst, kw, vwt, gt, ovt, ebank, sbank, wbank, eb_shift):
    b, nq = qt.shape[0], qt.shape[1]
    t = ks.shape[1]
    n_cmp_rows = kc.shape[1]
    n_sb = t // SEL_BLOCK
    per_b = lambda shp: pl.BlockSpec((1,) + shp[1:], lambda i, j: (i,) + (0,) * (len(shp) - 1))
    per_q = lambda shp: pl.BlockSpec(
        (1, 1) + shp[2:], lambda i, j: (i, jnp.minimum(j, nq - 1)) + (0,) * (len(shp) - 2))
    full = lambda shp: pl.BlockSpec(shp, lambda i, j: (0,) * len(shp))
    args = (qt, kc, vct, ks, vst, kw, vwt, gt, ovt, ebank, sbank, wbank)
    specs = [per_q(qt.shape), per_b(kc.shape), per_b(vct.shape), per_b(ks.shape), per_b(vst.shape),
             per_b(kw.shape), per_b(vwt.shape), per_q(gt.shape), full(ovt.shape),
             full(ebank.shape), full(sbank.shape), full(wbank.shape)]
    acc_like = pltpu.VMEM((N_KV, HEAD_DIM, GQ), F32)
    return pl.pallas_call(
        functools.partial(_attn_kernel, n_cmp_rows=n_cmp_rows, eb_shift=eb_shift),
        grid=(b, nq + 1),
        in_specs=specs,
        out_specs=pl.BlockSpec((1, Q_BLOCK, ATTN_W), lambda i, j: (i, jnp.maximum(j - 1, 0), 0)),
        out_shape=jax.ShapeDtypeStruct((b, t, ATTN_W), BF16),
        scratch_shapes=[pltpu.VMEM((N_KV, n_sb, GQ), F32),
                        pltpu.VMEM((N_KV, V_ROWS, GQ), F32),
                        acc_like,
                        pltpu.VMEM((N_KV, KV_W, GQ), BF16),
                        pltpu.VMEM((QK_AHEAD, KEY_CHUNK, GQ), BF16),
                        acc_like],
        compiler_params=_cparams(2),
        name="nsa_attention",
    )(*args)


def _ffn_kernel(x_ref, a_ref, c_ref, mod_ref, n2_ref, wo_ref, w1_ref, w2_ref, o_ref):
    aw = a_ref.shape[2]
    mix = jnp.dot(a_ref[0], wo_ref[0:aw, :], preferred_element_type=F32)
    mix = mix + jnp.dot(c_ref[0], wo_ref[aw:, :], preferred_element_type=F32)
    x1 = x_ref[0] + mod_ref[0, 2:3, :] * mix
    ms = jnp.mean(x1 * x1, axis=-1, keepdims=True)
    y = x1 * lax.rsqrt(ms + EPS) * n2_ref[...]
    h2 = (y * (1.0 + mod_ref[0, 4:5, :]) + mod_ref[0, 3:4, :]).astype(BF16)
    d_ff = w1_ref.shape[1]
    ff = jnp.zeros(x1.shape, F32)
    for j in range(d_ff // FF_CHUNK):
        a = jnp.dot(h2, w1_ref[:, j * FF_CHUNK:(j + 1) * FF_CHUNK], preferred_element_type=F32)
        a = jnp.maximum(a, 0.0)
        ff = ff + jnp.dot((a * a).astype(BF16), w2_ref[j * FF_CHUNK:(j + 1) * FF_CHUNK, :],
                          preferred_element_type=F32)
    o_ref[0] = x1 + mod_ref[0, 5:6, :] * ff


def _ffn_call(x, attn, conv, mod, norm2, w_out, w_ff1, w_ff2):
    b, t, d = x.shape
    tm = min(ROW_TILE, t)
    row_spec = lambda w: pl.BlockSpec((1, tm, w), lambda i, j: (i, j, 0))
    full = lambda shp: pl.BlockSpec(shp, lambda i, j: (0,) * len(shp),
                                    pipeline_mode=pl.Buffered(1))
    return pl.pallas_call(
        _ffn_kernel,
        grid=(b, t // tm),
        in_specs=[row_spec(d), row_spec(attn.shape[2]), row_spec(conv.shape[2]),
                  pl.BlockSpec((1, N_MOD, d), lambda i, j: (i, 0, 0)),
                  full((1, d)), full(w_out.shape), full(w_ff1.shape), full(w_ff2.shape)],
        out_specs=row_spec(d),
        out_shape=jax.ShapeDtypeStruct((b, t, d), F32),
        compiler_params=_cparams(2),
        name="outproj_mlp",
    )(x, attn, conv, mod, norm2, w_out, w_ff1, w_ff2)


def _block_diag_ones(n):
    idx = np.arange(n) // HEAD_DIM
    return jnp.asarray(idx[:, None] == idx[None, :], dtype=BF16)


def _pack_w_in(w_in):
    d = w_in.shape[0]
    conv_w = d - ATTN_W
    sizes = [ATTN_W] + [KV_W] * 6 + [N_BRANCH * N_HEADS] + [conv_w] * 3
    offs = np.concatenate([[0], np.cumsum(sizes)])
    part = lambda i: w_in[:, offs[i]:offs[i + 1]]
    q, kc, vc, ks, vs, kw, vw, g, cgate, bgate, u = (part(i) for i in range(11))
    w_nat = jnp.concatenate([kc, vc, ks, kw, cgate, bgate, u], axis=1).astype(BF16)
    gt = g.reshape(d, N_KV, GQA, N_BRANCH).transpose(2, 3, 1, 0).reshape(GQA, N_BRANCH * N_KV, d)
    gt = jnp.pad(gt, ((0, 0), (0, SUBLANES - N_BRANCH * N_KV), (0, 0))).reshape(_G_ROWS, d)
    w_tr = jnp.concatenate([q.T, vs.T, vw.T, gt], axis=0).astype(BF16)
    return w_nat, w_tr


def _expand_w1(w1):
    hid = w1.shape[1]
    w = w1.reshape(2, CMP_STRIDE, HEAD_DIM, hid).astype(BF16)
    zero = jnp.zeros_like(w)
    per_group = [jnp.concatenate([w if k == g else zero for k in range(N_KV)], axis=-1)
                 for g in range(N_KV)]
    return jnp.stack(per_group, axis=2).reshape(2, CMP_STRIDE * KV_W, N_KV * hid)


def _expand_w2(w2):
    hid = w2.shape[0]
    eye = jnp.eye(N_KV, dtype=w2.dtype).reshape(N_KV, 1, N_KV, 1)
    return (w2.reshape(1, hid, 1, HEAD_DIM) * eye).reshape(N_KV * hid, KV_W).astype(BF16)


def _expand_pe(pe):
    p = pe.reshape(2, CMP_STRIDE, 1, HEAD_DIM)
    return jnp.broadcast_to(p, (2, CMP_STRIDE, N_KV, HEAD_DIM)).reshape(2, CMP_STRIDE * KV_W)


def _bucket_thresholds():
    n = np.arange(2 * REL_MAX_DIST)
    max_exact = REL_BUCKETS // 2
    nf = np.maximum(n, max_exact).astype(np.float32)
    ratio = np.log(nf / np.float32(max_exact)) / np.float32(math.log(REL_MAX_DIST / max_exact))
    large = max_exact + (ratio * np.float32(REL_BUCKETS - max_exact)).astype(np.int32)
    table = np.where(n < max_exact, n, np.minimum(large, REL_BUCKETS - 1))
    return tuple(int(np.searchsorted(table, k, side="left")) for k in range(REL_BUCKETS))


def _bank_call(bias_rows, lead, n_tiles, dist_fn, name):
    nl = len(lead)
    thr = _bucket_thresholds()

    def body(rows_ref, o_ref):
        lead_ids = [pl.program_id(a) for a in range(nl)]
        row = lax.broadcasted_iota(jnp.int32, (SEL_BLOCK, GQ), 0)
        qi = lax.broadcasted_iota(jnp.int32, (SEL_BLOCK, GQ), 1) & (Q_BLOCK - 1)

        def tile(t, carry):
            dist, ok = dist_fn(lead_ids, t, row, qi)
            v = jnp.broadcast_to(rows_ref[0, 0:1, :], (SEL_BLOCK, GQ))
            for k in range(1, REL_BUCKETS):
                v = jnp.where(dist >= thr[k], rows_ref[0, k:k + 1, :], v)
            o_ref[(0,) * (nl + 1) + (t,)] = jnp.where(ok, v, NEG)
            return carry

        lax.fori_loop(0, n_tiles, tile, 0)

    return pl.pallas_call(
        body,
        grid=tuple(lead) + (N_KV,),
        in_specs=[pl.BlockSpec((1, REL_BUCKETS, GQ), lambda *i: (i[nl], 0, 0))],
        out_specs=pl.BlockSpec((1,) * (nl + 1) + (n_tiles, SEL_BLOCK, GQ),
                               lambda *i: tuple(i) + (0, 0, 0)),
        out_shape=jax.ShapeDtypeStruct(tuple(lead) + (N_KV, n_tiles, SEL_BLOCK, GQ), F32),
        compiler_params=_cparams(nl + 1),
        name=name,
    )(bias_rows)


def _bias_banks(rel_bias, t):
    n_cmp_rows = t // CMP_STRIDE
    rows = rel_bias.reshape(REL_BUCKETS, N_KV, GQA).transpose(1, 0, 2)
    rows = jnp.repeat(rows, Q_BLOCK, axis=2) * LOG2E

    def sel_dist(lead, tile, row, qi):
        dist = SEL_BLOCK * tile + qi - row
        return dist, dist >= 0

    sbank = _bank_call(rows, (), NEAR_TILES + 1, sel_dist, "bias_bank_sel")

    def win_dist(lead, tile, row, qi):
        dist = SEL_BLOCK * tile + qi - row
        return dist, (dist >= 0) & (dist < WINDOW)

    wbank = _bank_call(rows, (), WIN_BLOCKS + 2, win_dist, "bias_bank_win")

    cmp_c = n_cmp_rows - 4
    shifts = tuple(int((-(cmp_c - 4 * p)) % 8) for p in range(2))
    n_tiles = (cmp_c + n_cmp_rows + 8 + SEL_BLOCK - 1) // SEL_BLOCK

    def cmp_dist(lead, tile, row, qi):
        e = SEL_BLOCK * tile + row - jnp.where(lead[0] == 0, shifts[0], shifts[1])
        dist = qi - CMP_STRIDE * e + (CMP_STRIDE * cmp_c - (CMP_BLOCK - 1))
        return dist, (dist >= 0) & (e >= 0)

    ebank = _bank_call(rows, (2,), n_tiles, cmp_dist, "bias_bank_cmp")
    ebank = ebank.reshape(2, N_KV, n_tiles * SEL_BLOCK, GQ)
    return ebank, sbank, wbank, shifts


def _overlap_t(t):
    n_cmp_rows = t // CMP_STRIDE
    n_sb = t // SEL_BLOCK
    c_start = np.arange(n_cmp_rows)[None, :] * CMP_STRIDE
    s_start = np.arange(n_sb)[:, None] * SEL_BLOCK
    ov = np.clip(np.minimum(c_start + CMP_BLOCK, s_start + SEL_BLOCK)
                 - np.maximum(c_start, s_start), 0, None) / CMP_BLOCK
    ov[:, n_cmp_rows - 1] = 0.0
    return jnp.asarray(ov, dtype=BF16)


def _layer(x, c_pad, w_in, q_norm, k_norm, cmp_pe_k, cmp_w1_k, cmp_w2_k, cmp_pe_v, cmp_w1_v,
           cmp_w2_v, rel_bias, conv_w, w_out, norm1, norm2, w_ada, b_ada, w_ff1, w_ff2):
    b, t, d = x.shape
    scale = HEAD_DIM ** -0.5

    mod = _mod_call(c_pad, w_ada, b_ada)[:b].reshape(b, N_MOD, d)

    qn_col = (jnp.tile(q_norm, N_HEADS) * (scale * LOG2E)).reshape(ATTN_W, 1)
    kn_t = jnp.tile(k_norm, N_KV).reshape(1, KV_W)
    bdq = _block_diag_ones(ATTN_W)
    bdk = _block_diag_ones(KV_W)
    w_nat, w_tr = _pack_w_in(w_in)
    qt, kc_raw, vc_raw, ks, vst, kw, vwt, gt, conv = _inproj_call(
        x, mod, norm1.reshape(1, d), w_nat, w_tr, qn_col, kn_t, conv_w, bdq, bdk)

    kc, vct = _compress_call(
        kc_raw, vc_raw, _expand_pe(cmp_pe_k), _expand_pe(cmp_pe_v), _expand_w1(cmp_w1_k), _expand_w1(cmp_w1_v),
        _expand_w2(cmp_w2_k), _expand_w2(cmp_w2_v).T, kn_t, bdk)

    ebank, sbank, wbank, eb_shift = _bias_banks(rel_bias, t)
    attn = _attn_call(qt, kc, vct, ks, vst, kw, vwt, gt, _overlap_t(t), ebank, sbank.astype(BF16),
                      wbank.astype(BF16), eb_shift)

    return _ffn_call(x, attn, conv, mod, norm2.reshape(1, d), w_out.astype(BF16),
                     w_ff1.astype(BF16), w_ff2.astype(BF16))


def kernel(x, c, w_in, q_norm, k_norm, cmp_pe_k, cmp_w1_k, cmp_w2_k, cmp_pe_v, cmp_w1_v, cmp_w2_v,
           rel_bias, conv_w, w_out, norm1, norm2, w_ada, b_ada, w_ff1, w_ff2):
    b = x.shape[0]
    c_pad = jnp.pad(c, ((0, (-b) % 8), (0, 0)))
    for l in range(w_in.shape[0]):
        x = _layer(x, c_pad, w_in[l], q_norm[l], k_norm[l], cmp_pe_k[l], cmp_w1_k[l], cmp_w2_k[l],
                   cmp_pe_v[l], cmp_w1_v[l], cmp_w2_v[l], rel_bias, conv_w[l], w_out[l],
                   norm1[l], norm2[l], w_ada[l], b_ada[l], w_ff1[l], w_ff2[l])
    return x
```

```python
import functools
import math

import numpy as np
import jax
import jax.numpy as jnp
from jax import lax
from jax.experimental import pallas as pl
from jax.experimental.pallas import tpu as pltpu

HEAD_DIM = 64
N_HEADS = 8
N_KV = 2
GQA = N_HEADS // N_KV
ATTN_W = N_HEADS * HEAD_DIM
KV_W = N_KV * HEAD_DIM
CONV_K = 3
CMP_BLOCK = 32
CMP_STRIDE = 16
SEL_BLOCK = 64
N_SELECT = 16
WINDOW = 512
Q_BLOCK = 64
REL_BUCKETS = 32
REL_MAX_DIST = 1024
N_MOD = 6
N_BRANCH = 3
EPS = 1e-6
NEG = -1e30

LANES = 128
SUBLANES = 8
GQ = GQA * Q_BLOCK
KEY_CHUNK = 256
BLOCKS_PER_CHUNK = KEY_CHUNK // SEL_BLOCK
V_TILE = 128
BF16_ROWS = 16
V_ROWS = HEAD_DIM + BF16_ROWS
V_TILE_ROWS = N_KV * V_ROWS
CMP_ROW_STEP = 64
WIN_BLOCKS = WINDOW // SEL_BLOCK
WIN_TILES = WINDOW // V_TILE + 1
NEAR_TILES = (REL_MAX_DIST + Q_BLOCK - 1) // SEL_BLOCK + 1
ROW_TILE = 512
INPROJ_TILE = 1024
INPROJ_SPLIT = 4
BIG_CHUNKS = 16
TAIL_CHUNKS = 1
assert BIG_CHUNKS % TAIL_CHUNKS == 0
MERGE_CHUNKS = 2
QK_AHEAD = 4
QK_DEPTH = 5
SMALLEST_IMPORTANCE = 2.0 ** -100
LOG2E = math.log2(math.e)
FF_CHUNK = 1024
VMEM_LIMIT = 56 * 1024 * 1024

F32 = jnp.float32
BF16 = jnp.bfloat16
_NT = (((1,), (1,)), ((), ()))


def _cparams(n_axes):
    return pltpu.CompilerParams(dimension_semantics=("arbitrary",) * n_axes,
                                vmem_limit_bytes=VMEM_LIMIT)


def _with_ones_rows(vt):
    ones = jnp.ones((BF16_ROWS, vt.shape[1]), vt.dtype)
    parts = []
    for g in range(N_KV):
        parts += [vt[g * HEAD_DIM:(g + 1) * HEAD_DIM, :], ones]
    return jnp.concatenate(parts, axis=0)


def _swap_halves(p0, p1):
    low = lax.broadcasted_iota(jnp.int32, p0.shape, 1) < LANES // 2
    return (jnp.where(low, p0, pltpu.roll(p1, LANES // 2, 1)),
            jnp.where(low, pltpu.roll(p0, LANES // 2, 1), p1))


def _mod_kernel(c_ref, w_ref, b_ref, o_ref):
    c = c_ref[...]
    a = c * jax.nn.sigmoid(c)
    o_ref[...] = jnp.dot(a, w_ref[...], preferred_element_type=F32,
                         precision=lax.Precision.HIGHEST) + b_ref[...]


def _mod_call(c_pad, w_ada, b_ada):
    rows, d = c_pad.shape
    n = w_ada.shape[1]
    tn = n // N_MOD
    return pl.pallas_call(
        _mod_kernel,
        grid=(n // tn,),
        in_specs=[pl.BlockSpec((rows, d), lambda j: (0, 0)),
                  pl.BlockSpec((d, tn), lambda j: (0, j)),
                  pl.BlockSpec((1, tn), lambda j: (0, j))],
        out_specs=pl.BlockSpec((rows, tn), lambda j: (0, j)),
        out_shape=jax.ShapeDtypeStruct((rows, n), F32),
        compiler_params=_cparams(1),
        name="adaln_mod",
    )(c_pad, w_ada, b_ada.reshape(1, n))


_N_KC, _N_VC, _N_KS, _N_KW, _N_CONV = 0, KV_W, 2 * KV_W, 3 * KV_W, 4 * KV_W
_T_Q, _T_VS, _T_VW, _T_G = 0, ATTN_W, ATTN_W + KV_W, ATTN_W + 2 * KV_W
_G_ROWS = GQA * SUBLANES


def _inproj_kernel(x_ref, mod_ref, n1_ref, wn_ref, wt_ref, qn_ref, kn_ref, cw_ref, bdq_ref, bdk_ref,
                   qt_out, kc_out, vc_out, ks_out, vst_out, kw_out, vwt_out, gt_out, conv_out,
                   carry, *, conv_w):
    t = pl.program_id(1)
    tm = x_ref.shape[1]
    sub = tm // INPROJ_SPLIT

    @pl.when(t == 0)
    def _():
        carry[...] = jnp.zeros_like(carry)

    prev2, prev1 = carry[6:7, :], carry[7:8, :]
    c_bg = _N_CONV + conv_w
    c_u = c_bg + conv_w

    for part in range(INPROJ_SPLIT):
        rows = slice(part * sub, (part + 1) * sub)
        x = x_ref[0, rows, :]
        ms = jnp.mean(x * x, axis=-1, keepdims=True)
        y = x * lax.rsqrt(ms + EPS) * n1_ref[...]
        h = (y * (1.0 + mod_ref[0, 1:2, :]) + mod_ref[0, 0:1, :]).astype(BF16)

        def proj(a, b):
            return jnp.dot(h, wn_ref[:, a:b], preferred_element_type=F32)

        def proj_t(a, b):
            return lax.dot_general(wt_ref[a:b, :], h, _NT, preferred_element_type=F32)

        def head_norm(v, gain):
            ssq = jnp.dot((v * v).astype(BF16), bdk_ref[...], preferred_element_type=F32)
            return v * lax.rsqrt(ssq * (1.0 / HEAD_DIM) + EPS) * gain

        qf = proj_t(_T_Q, _T_VS)
        vg = proj_t(_T_VS, _T_G + _G_ROWS)
        kv = proj(_N_KC, _N_CONV)
        cv = proj(_N_CONV, c_u + conv_w)

        ssq = jnp.dot(bdq_ref[...], (qf * qf).astype(BF16), preferred_element_type=F32)
        qf = qf * lax.rsqrt(ssq * (1.0 / HEAD_DIM) + EPS) * qn_ref[...]
        gf = jax.nn.sigmoid(vg[2 * KV_W:, :])
        for c in range(sub // LANES):
            blk = (part * sub) // Q_BLOCK + 2 * c
            cols = slice(c * LANES, (c + 1) * LANES)
            for g in range(N_KV):
                pc = [qf[(g * GQA + r) * HEAD_DIM:(g * GQA + r + 1) * HEAD_DIM, cols]
                      for r in range(GQA)]
                lo01, hi01 = _swap_halves(pc[0], pc[1])
                lo23, hi23 = _swap_halves(pc[2], pc[3])
                qt_out[0, blk, g] = jnp.concatenate([lo01, lo23], axis=1).astype(BF16)
                qt_out[0, blk + 1, g] = jnp.concatenate([hi01, hi23], axis=1).astype(BF16)
            pc = [gf[r * SUBLANES:(r + 1) * SUBLANES, cols] for r in range(GQA)]
            lo01, hi01 = _swap_halves(pc[0], pc[1])
            lo23, hi23 = _swap_halves(pc[2], pc[3])
            gt_out[0, blk] = jnp.concatenate([lo01, lo23], axis=1)
            gt_out[0, blk + 1] = jnp.concatenate([hi01, hi23], axis=1)

        vs_f = _with_ones_rows(vg[:KV_W, :].astype(BF16))
        vw_f = _with_ones_rows(vg[KV_W:2 * KV_W, :].astype(BF16))
        for j in range(sub // KEY_CHUNK):
            vst_out[0, (part * sub) // KEY_CHUNK + j] = vs_f[:, j * KEY_CHUNK:(j + 1) * KEY_CHUNK]
        for j in range(sub // V_TILE):
            vwt_out[0, (part * sub) // V_TILE + j] = vw_f[:, j * V_TILE:(j + 1) * V_TILE]

        kc_out[0, rows, :] = kv[:, _N_KC:_N_VC]
        vc_out[0, rows, :] = kv[:, _N_VC:_N_KS]
        ks_out[0, rows, :] = head_norm(kv[:, _N_KS:_N_KW], kn_ref[...]).astype(BF16)
        kw_out[0, rows, :] = head_norm(kv[:, _N_KW:_N_CONV], kn_ref[...]).astype(BF16)

        z = cv[:, :conv_w] * cv[:, 2 * conv_w:]
        row = lax.broadcasted_iota(jnp.int32, z.shape, 0)
        z1 = jnp.where(row == 0, prev1, pltpu.roll(z, 1, 0))
        z2 = jnp.where(row == 0, prev2, jnp.where(row == 1, prev1, pltpu.roll(z, 2, 0)))
        zc = cw_ref[0:1, :] * z2 + cw_ref[1:2, :] * z1 + cw_ref[2:3, :] * z
        conv_out[0, rows, :] = (cv[:, conv_w:2 * conv_w] * zc).astype(BF16)
        prev2, prev1 = z[sub - 2:sub - 1, :], z[sub - 1:sub, :]
        if part == INPROJ_SPLIT - 1:
            carry[...] = z[sub - SUBLANES:sub, :]


def _inproj_call(x, mod, norm1, w_nat, w_tr, qn_col, kn_t, conv_w, bdq, bdk):
    b, t, d = x.shape
    tm = min(INPROJ_TILE, t)
    cw = conv_w.shape[1]
    nq = t // Q_BLOCK
    row_spec = lambda w: pl.BlockSpec((1, tm, w), lambda i, j: (i, j, 0))
    full = lambda shp: pl.BlockSpec(shp, lambda i, j: (0,) * len(shp))
    vt_spec = lambda w: pl.BlockSpec((1, tm // w, V_TILE_ROWS, w), lambda i, j: (i, j, 0, 0))
    vt_shape = lambda w: jax.ShapeDtypeStruct((b, t // w, V_TILE_ROWS, w), BF16)
    kv = lambda dt: jax.ShapeDtypeStruct((b, t, KV_W), dt)
    out_specs = [pl.BlockSpec((1, tm // Q_BLOCK, N_KV, HEAD_DIM, GQ), lambda i, j: (i, j, 0, 0, 0)),
                 row_spec(KV_W), row_spec(KV_W), row_spec(KV_W), vt_spec(KEY_CHUNK), row_spec(KV_W),
                 vt_spec(V_TILE),
                 pl.BlockSpec((1, tm // Q_BLOCK, SUBLANES, GQ), lambda i, j: (i, j, 0, 0)),
                 row_spec(cw)]
    out_shape = [jax.ShapeDtypeStruct((b, nq, N_KV, HEAD_DIM, GQ), BF16),
                 kv(F32), kv(F32), kv(BF16), vt_shape(KEY_CHUNK), kv(BF16), vt_shape(V_TILE),
                 jax.ShapeDtypeStruct((b, nq, SUBLANES, GQ), F32),
                 jax.ShapeDtypeStruct((b, t, cw), BF16)]
    return pl.pallas_call(
        functools.partial(_inproj_kernel, conv_w=cw),
        grid=(b, t // tm),
        in_specs=[row_spec(d),
                  pl.BlockSpec((1, N_MOD, d), lambda i, j: (i, 0, 0)),
                  full((1, d)), full(w_nat.shape), full(w_tr.shape), full((ATTN_W, 1)),
                  full((1, KV_W)), full((CONV_K, cw)), full((ATTN_W, ATTN_W)), full((KV_W, KV_W))],
        out_specs=out_specs,
        out_shape=out_shape,
        scratch_shapes=[pltpu.VMEM((SUBLANES, cw), F32)],
        compiler_params=_cparams(2),
        name="inproj",
    )(x, mod, norm1, w_nat, w_tr, qn_col, kn_t, conv_w, bdq, bdk)


def _compress_kernel(kx_ref, vx_ref, pek_ref, pev_ref, w1k_ref, w1v_ref, w2k_ref, w2vt_ref,
                     kn_ref, bdk_ref, kc_out, vct_out):
    def hidden(x_ref, pe_ref, w1_ref):
        n = x_ref.shape[1] // CMP_STRIDE
        u = jnp.zeros((n, w1_ref.shape[2]), F32)
        v = jnp.zeros((n, w1_ref.shape[2]), F32)
        for r in range(0, CMP_STRIDE, 2):
            tok = [x_ref[0, pl.ds(r + d, n, stride=CMP_STRIDE), :] for d in range(2)]
            cols = slice(r * KV_W, (r + 2) * KV_W)
            for a, acc in ((0, "u"), (1, "v")):
                lhs = jnp.concatenate([tok[d] + pe_ref[a:a + 1, (r + d) * KV_W:(r + d + 1) * KV_W]
                                       for d in range(2)], axis=1).astype(BF16)
                prod = jnp.dot(lhs, w1_ref[a, cols, :], preferred_element_type=F32)
                if acc == "u":
                    u = u + prod
                else:
                    v = v + prod
        hid = u + pltpu.roll(v, n - 1, 0)
        return jax.nn.gelu(hid, approximate=True).astype(BF16)

    kc = jnp.dot(hidden(kx_ref, pek_ref, w1k_ref), w2k_ref[...], preferred_element_type=F32)
    ssq = jnp.dot((kc * kc).astype(BF16), bdk_ref[...], preferred_element_type=F32)
    kc_out[0] = (kc * lax.rsqrt(ssq * (1.0 / HEAD_DIM) + EPS) * kn_ref[...]).astype(BF16)
    vct_out[0] = _with_ones_rows(lax.dot_general(w2vt_ref[...], hidden(vx_ref, pev_ref, w1v_ref), _NT,
                                                 preferred_element_type=F32).astype(BF16))


def _compress_call(kx, vx, pek, pev, w1k, w1v, w2k, w2vt, kn_t, bdk):
    b, t, _ = kx.shape
    nrow = t // CMP_STRIDE
    wide, hid2 = w1k.shape[1], w1k.shape[2]
    full = lambda shp: pl.BlockSpec(shp, lambda i: (0,) * len(shp))
    xs = pl.BlockSpec((1, t, KV_W), lambda i: (i, 0, 0))
    return pl.pallas_call(
        _compress_kernel,
        grid=(b,),
        in_specs=[xs, xs, full((2, wide)), full((2, wide)), full((2, wide, hid2)),
                  full((2, wide, hid2)), full((hid2, KV_W)), full((KV_W, hid2)),
                  full((1, KV_W)), full((KV_W, KV_W))],
        out_specs=[pl.BlockSpec((1, nrow, KV_W), lambda i: (i, 0, 0)),
                   pl.BlockSpec((1, V_TILE_ROWS, nrow), lambda i: (i, 0, 0))],
        out_shape=[jax.ShapeDtypeStruct((b, nrow, KV_W), BF16),
                   jax.ShapeDtypeStruct((b, V_TILE_ROWS, nrow), BF16)],
        compiler_params=_cparams(1),
        name="compress",
    )(kx, vx, pek, pev, w1k, w1v, w2k, w2vt, kn_t, bdk)


def _attn_kernel(q_ref, kc_ref, vct_ref, ks_ref, vst_ref, kw_ref, vwt_ref, gt_ref, ovt_ref,
                 eb_ref, sb_ref, wb_ref, o_ref, sel_scr, acc_scr, out_scr, qp_scr, s_scr,
                 fin_scr, *, n_cmp_rows, eb_shift):
    n_sb = sel_scr.shape[1]
    write_out_only = pl.program_id(1) >= n_sb
    ci = jnp.minimum(pl.program_id(1), n_sb - 1)
    par = lax.rem(ci, 2)
    n_total = ks_ref.shape[1] // KEY_CHUNK
    vsl = lambda g: slice(g * V_ROWS, (g + 1) * V_ROWS)

    @pl.when((pl.program_id(0) == 0) & (pl.program_id(1) == 0))
    def _():
        def zero(g, carry):
            fin_scr[g] = jnp.zeros(fin_scr.shape[1:], F32)
            sel_scr[g] = jnp.zeros(sel_scr.shape[1:], F32)
            return carry
        lax.fori_loop(0, N_KV, zero, 0)

    def chunk_qk(g, c):
        kk = ks_ref[0, pl.ds(pl.multiple_of(c * KEY_CHUNK, KEY_CHUNK), KEY_CHUNK), :]
        return jnp.dot(kk, qp_scr[g], preferred_element_type=F32).astype(BF16)

    n_chunks = jnp.where(write_out_only, 0, ci // BLOCKS_PER_CHUNK + 1)
    tails_per_big = BIG_CHUNKS // TAIL_CHUNKS
    n_tails = (n_chunks + TAIL_CHUNKS - 1) // TAIL_CHUNKS
    n_big = n_tails // tails_per_big
    tail = lax.rem(n_tails, tails_per_big)
    tail_start = n_big * BIG_CHUNKS

    def big_units(i):
        return [(g, jnp.minimum(BIG_CHUNKS * i + j, n_total - 1))
                for g in range(N_KV) for j in range(BIG_CHUNKS)]

    def tail_units(size):
        return [(g, jnp.minimum(tail_start + j, n_total - 1))
                for g in range(N_KV) for j in range(size * TAIL_CHUNKS)]

    short = [s for s in range(1, tails_per_big) if s * TAIL_CHUNKS < QK_AHEAD]
    tail_options = [(tail == s, tail_units(s)) for s in short] + [(None, tail_units(tails_per_big - 1))]

    def pick_units(options):
        pad = lambda units: (units + [units[-1]] * QK_AHEAD)[:QK_AHEAD]
        units = pad(options[-1][1])
        for cond, cand in reversed(options[:-1]):
            units = [(jnp.where(cond, g_a, g_b), jnp.where(cond, c_a, c_b))
                     for (g_a, c_a), (g_b, c_b) in zip(pad(cand), units)]
        return [(g, jnp.minimum(c, n_total - 1)) for g, c in units]

    def prefetch_scores(units):
        for k in range(QK_AHEAD):
            s_scr[k] = chunk_qk(*units[k])

    def write_out():
        pieces = []
        for g in range(N_KV):
            for half in range(GQ // LANES):
                a = fin_scr[g, :, half * LANES:(half + 1) * LANES]
                stacked = jnp.concatenate([a, pltpu.roll(a, Q_BLOCK, 1)], axis=0)
                pieces.append(stacked.T[:Q_BLOCK, :])
        o_ref[0] = jnp.concatenate(pieces, axis=1).astype(BF16)

    def before_loops(rows):
        write_out()

        zeros_q = jnp.zeros((HEAD_DIM, GQ), BF16)
        q_padded = [jnp.concatenate([q_ref[0, 0, 0], zeros_q], axis=0),
                    jnp.concatenate([zeros_q, q_ref[0, 0, 1]], axis=0)]
        for g in range(N_KV):
            qp_scr[g] = q_padded[g]

        nsel = min(n_sb, rows * CMP_STRIDE // SEL_BLOCK)
        lane = lax.broadcasted_iota(jnp.int32, (nsel, LANES), 1)
        cmp_c = n_cmp_rows - 4
        e0 = cmp_c - 4 * ci + jnp.where(par == 0, eb_shift[0], eb_shift[1])
        e0 = pl.multiple_of(e0, 8)
        o_cmp = []
        imp = []
        w0 = ci // 2 - (WIN_TILES - 1)
        win_tiles = [jnp.maximum(w0 + j, 0) for j in range(WIN_TILES)]
        s_cmp = jnp.dot(kc_ref[0, :rows, :], jnp.concatenate(q_padded, axis=1),
                        preferred_element_type=F32)

        def window_scores():
            return [[jnp.dot(kw_ref[0, pl.ds(pl.multiple_of(tj * V_TILE, V_TILE), V_TILE), :],
                             q_padded[g], preferred_element_type=F32).astype(BF16) for tj in win_tiles]
                    for g in range(N_KV)]

        window_first = 2 * rows <= n_cmp_rows
        if window_first:
            s_win = window_scores()
        for g in range(N_KV):
            sc = s_cmp[:, g * GQ:(g + 1) * GQ] + eb_ref[par, g, pl.ds(e0, rows), :]
            m = jnp.max(sc, axis=0, keepdims=True)
            e = jnp.exp2(sc - m).astype(BF16)
            ov = jnp.dot(vct_ref[0, vsl(g), :rows], e, preferred_element_type=F32)
            inv = jnp.where(m > 0.5 * NEG, 1.0 / ov[HEAD_DIM:HEAD_DIM + 1, :], 0.0)
            o_cmp.append(ov[:HEAD_DIM, :] * inv)
            ir = jnp.dot(ovt_ref[:nsel, :rows], e, preferred_element_type=F32) * inv
            a = ir[:, :LANES] + ir[:, LANES:]
            imp.append(a + pltpu.roll(a, Q_BLOCK, 1))
        if not window_first:
            s_win = window_scores()

        prefetch_scores(pick_units([(n_big > 0, big_units(0))] + tail_options))

        jidx = lax.broadcasted_iota(jnp.int32, (nsel, LANES), 0)
        valid = jidx <= ci
        forced = (jidx == 0) | (jidx == ci) | (jidx == ci - 1)
        index_mask = (1 << max(1, (n_sb - 1).bit_length())) - 1
        imp_bits = lax.bitcast_convert_type(
            jnp.maximum(jnp.where(lane < Q_BLOCK, imp[0], imp[1]), SMALLEST_IMPORTANCE), jnp.int32)
        keyed = lax.bitcast_convert_type((imp_bits & ~index_mask) | (index_mask - jidx), F32)
        score = jnp.where(valid, jnp.where(forced, -2.0, keyed), -1.0)
        for _ in range(min(N_SELECT, nsel) - 3):
            score = jnp.where(score == jnp.max(score, axis=0, keepdims=True), -2.0, score)
        selneg = jnp.where((score < -1.5) & valid, 0.0, NEG)
        swapped = pltpu.roll(selneg, Q_BLOCK, 1)
        left = jnp.where(lane < Q_BLOCK, selneg, swapped)
        right = jnp.where(lane < Q_BLOCK, swapped, selneg)
        sel_scr[0, :nsel] = jnp.concatenate([left, left], axis=1)
        sel_scr[1, :nsel] = jnp.concatenate([right, right], axis=1)

        for g in range(N_KV):
            parts = []
            for i in range(2 * WIN_TILES):
                delta = par + WIN_BLOCKS - i
                ok = (delta >= 0) & (delta <= WIN_BLOCKS) & (delta <= ci)
                tile = jnp.where(ok, delta, WIN_BLOCKS + 1)
                half = s_win[g][i // 2][(i % 2) * SEL_BLOCK:(i % 2 + 1) * SEL_BLOCK, :]
                parts.append(half + wb_ref[g, tile])
            s = jnp.concatenate(parts, axis=0)
            m = jnp.max(s, axis=0, keepdims=True)
            pb = jnp.exp2(s - m)
            o_win = jnp.zeros((V_ROWS, GQ), F32)
            for j, tj in enumerate(win_tiles):
                o_win = o_win + jnp.dot(vwt_ref[0, tj, vsl(g), :], pb[j * V_TILE:(j + 1) * V_TILE, :],
                                        preferred_element_type=F32)
            w_scale = (gt_ref[0, 0, 2 * N_KV + g:2 * N_KV + g + 1, :]
                       * (1.0 / o_win[HEAD_DIM:HEAD_DIM + 1, :]))
            out_scr[g] = gt_ref[0, 0, g:g + 1, :] * o_cmp[g] + w_scale * o_win[:HEAD_DIM, :]
            acc_scr[g] = jnp.zeros((V_ROWS, GQ), F32)

    row_steps = [r for r in range(CMP_ROW_STEP, n_cmp_rows + 1, CMP_ROW_STEP)] or [n_cmp_rows]
    variant = jnp.minimum((4 * ci + 2) // CMP_ROW_STEP, len(row_steps) - 1)
    variant = jnp.where(write_out_only, len(row_steps), variant)
    lax.switch(variant, [functools.partial(before_loops, r) for r in row_steps] + [write_out])

    def chunk_softmax(s, g, c, extra):
        parts = []
        for i in range(BLOCKS_PER_CHUNK):
            kb = c * BLOCKS_PER_CHUNK + i
            blk = s[i * SEL_BLOCK:(i + 1) * SEL_BLOCK, :]
            mrow = sel_scr[g, pl.ds(kb, 1), :]
            if extra is not None:
                mrow = mrow + extra
            tile = jnp.clip(ci - kb, 0, NEAR_TILES)
            parts.append(blk + sb_ref[g, tile] + mrow.astype(BF16))
        s = jnp.concatenate(parts, axis=0)
        m_c = jnp.max(s, axis=0, keepdims=True)
        return m_c.astype(F32), jnp.exp2(s - m_c)

    def chunk_pv(pb, g, c):
        return jnp.dot(vst_ref[0, c, vsl(g), :], pb, preferred_element_type=F32)

    def merge_step(units, extras, next_units, carry):
        parked = min(QK_AHEAD, len(units))
        scores = {k: s_scr[k] for k in range(parked)}
        pending = list(range(parked, len(units) + (0 if next_units is None else QK_AHEAD)))

        def issue_scores():
            k = pending.pop(0)
            if k < len(units):
                scores[k] = chunk_qk(*units[k])
            else:
                s_scr[k - len(units)] = chunk_qk(*next_units[k - len(units)])

        for _ in range(min(QK_DEPTH - QK_AHEAD, len(pending))):
            issue_scores()
        results = []
        for k, (g, c) in enumerate(units):
            m_c, pb = chunk_softmax(scores.pop(k), g, c, extras[k])
            if pending:
                issue_scores()
            results.append((m_c, chunk_pv(pb, g, c)))
        per_g = len(units) // N_KV
        new = []
        for g in range(N_KV):
            m_run = carry[g]
            acc = acc_scr[g]
            for first in range(g * per_g, (g + 1) * per_g, MERGE_CHUNKS):
                stats = results[first:min(first + MERGE_CHUNKS, (g + 1) * per_g)]
                m_new = m_run
                for m_c, _ in stats:
                    m_new = jnp.maximum(m_new, m_c)
                acc = jnp.exp2(m_run - m_new) * acc
                for m_c, pv in stats:
                    acc = acc + jnp.exp2(m_c - m_new) * pv
                m_run = m_new
            acc_scr[g] = acc
            new.append(m_run)
        return tuple(new)

    def big_step(i, carry):
        nxt = pick_units([(i + 1 < n_big, big_units(i + 1))] + tail_options)
        extras = [jnp.where(BIG_CHUNKS * i + j < n_chunks, 0.0, NEG)
                  if j > BIG_CHUNKS - TAIL_CHUNKS else None
                  for _ in range(N_KV) for j in range(BIG_CHUNKS)]
        return merge_step(big_units(i), extras, nxt, carry)

    carry = lax.fori_loop(0, n_big, big_step, (jnp.full((1, GQ), NEG, F32),) * N_KV)

    def tail_step(size):
        extras = [None if j == 0 else jnp.where(tail_start + j < n_chunks, 0.0, NEG)
                  for _ in range(N_KV) for j in range(size * TAIL_CHUNKS)]
        merge_step(tail_units(size), extras, None, carry)

    def run_tail(lo, hi):
        if hi - lo == 1:
            if lo > 0:
                tail_step(lo)
        else:
            mid = (lo + hi) // 2
            lax.cond(tail < mid, functools.partial(run_tail, lo, mid), functools.partial(run_tail, mid, hi))

    run_tail(0, tails_per_big)

    for g in range(N_KV):
        acc = acc_scr[g]
        scale = gt_ref[0, 0, N_KV + g:N_KV + g + 1, :] * (1.0 / acc[HEAD_DIM:HEAD_DIM + 1, :])
        fin_scr[g] = out_scr[g] + scale * acc[:HEAD_DIM, :]


def _attn_call(qt, kc, vct, ks, vst, kw, vwt, gt, ovt, ebank, sbank, wbank, eb_shift):
    b, nq = qt.shape[0], qt.shape[1]
    t = ks.shape[1]
    n_cmp_rows = kc.shape[1]
    n_sb = t // SEL_BLOCK
    per_b = lambda shp: pl.BlockSpec(
        (1,) + shp[1:], lambda i, j: (jnp.minimum(i + j // nq, b - 1),) + (0,) * (len(shp) - 1))
    per_q = lambda shp: pl.BlockSpec(
        (1, 1) + shp[2:], lambda i, j: (i, jnp.minimum(j, nq - 1)) + (0,) * (len(shp) - 2))
    full = lambda shp: pl.BlockSpec(shp, lambda i, j: (0,) * len(shp))
    args = (qt, kc, vct, ks, vst, kw, vwt, gt, ovt, ebank, sbank, wbank)
    specs = [per_q(qt.shape), per_b(kc.shape), per_b(vct.shape), per_b(ks.shape), per_b(vst.shape),
             per_b(kw.shape), per_b(vwt.shape), per_q(gt.shape), full(ovt.shape),
             full(ebank.shape), full(sbank.shape), full(wbank.shape)]
    acc_like = pltpu.VMEM((N_KV, HEAD_DIM, GQ), F32)
    return pl.pallas_call(
        functools.partial(_attn_kernel, n_cmp_rows=n_cmp_rows, eb_shift=eb_shift),
        grid=(b, nq + 1),
        in_specs=specs,
        out_specs=pl.BlockSpec((1, Q_BLOCK, ATTN_W), lambda i, j: (i, jnp.maximum(j - 1, 0), 0)),
        out_shape=jax.ShapeDtypeStruct((b, t, ATTN_W), BF16),
        scratch_shapes=[pltpu.VMEM((N_KV, n_sb, GQ), F32),
                        pltpu.VMEM((N_KV, V_ROWS, GQ), F32),
                        acc_like,
                        pltpu.VMEM((N_KV, KV_W, GQ), BF16),
                        pltpu.VMEM((QK_AHEAD, KEY_CHUNK, GQ), BF16),
                        acc_like],
        compiler_params=_cparams(2),
        name="nsa_attention",
    )(*args)


def _ffn_kernel(x_ref, a_ref, c_ref, mod_ref, n2_ref, wo_ref, w1_ref, w2_ref, o_ref):
    aw = a_ref.shape[2]
    mix = jnp.dot(a_ref[0], wo_ref[0:aw, :], preferred_element_type=F32)
    mix = mix + jnp.dot(c_ref[0], wo_ref[aw:, :], preferred_element_type=F32)
    x1 = x_ref[0] + mod_ref[0, 2:3, :] * mix
    ms = jnp.mean(x1 * x1, axis=-1, keepdims=True)
    y = x1 * lax.rsqrt(ms + EPS) * n2_ref[...]
    h2 = (y * (1.0 + mod_ref[0, 4:5, :]) + mod_ref[0, 3:4, :]).astype(BF16)
    d_ff = w1_ref.shape[1]
    ff = jnp.zeros(x1.shape, F32)
    for j in range(d_ff // FF_CHUNK):
        a = jnp.dot(h2, w1_ref[:, j * FF_CHUNK:(j + 1) * FF_CHUNK], preferred_element_type=F32)
        a = jnp.maximum(a, 0.0)
        ff = ff + jnp.dot((a * a).astype(BF16), w2_ref[j * FF_CHUNK:(j + 1) * FF_CHUNK, :],
                          preferred_element_type=F32)
    o_ref[0] = x1 + mod_ref[0, 5:6, :] * ff


def _ffn_call(x, attn, conv, mod, norm2, w_out, w_ff1, w_ff2):
    b, t, d = x.shape
    tm = min(ROW_TILE, t)
    row_spec = lambda w: pl.BlockSpec((1, tm, w), lambda i, j: (i, j, 0))
    full = lambda shp: pl.BlockSpec(shp, lambda i, j: (0,) * len(shp),
                                    pipeline_mode=pl.Buffered(1))
    return pl.pallas_call(
        _ffn_kernel,
        grid=(b, t // tm),
        in_specs=[row_spec(d), row_spec(attn.shape[2]), row_spec(conv.shape[2]),
                  pl.BlockSpec((1, N_MOD, d), lambda i, j: (i, 0, 0)),
                  full((1, d)), full(w_out.shape), full(w_ff1.shape), full(w_ff2.shape)],
        out_specs=row_spec(d),
        out_shape=jax.ShapeDtypeStruct((b, t, d), F32),
        compiler_params=_cparams(2),
        name="outproj_mlp",
    )(x, attn, conv, mod, norm2, w_out, w_ff1, w_ff2)


def _block_diag_ones(n):
    idx = np.arange(n) // HEAD_DIM
    return jnp.asarray(idx[:, None] == idx[None, :], dtype=BF16)


def _pack_w_in(w_in):
    d = w_in.shape[0]
    conv_w = d - ATTN_W
    sizes = [ATTN_W] + [KV_W] * 6 + [N_BRANCH * N_HEADS] + [conv_w] * 3
    offs = np.concatenate([[0], np.cumsum(sizes)])
    part = lambda i: w_in[:, offs[i]:offs[i + 1]]
    q, kc, vc, ks, vs, kw, vw, g, cgate, bgate, u = (part(i) for i in range(11))
    w_nat = jnp.concatenate([kc, vc, ks, kw, cgate, bgate, u], axis=1).astype(BF16)
    gt = g.reshape(d, N_KV, GQA, N_BRANCH).transpose(2, 3, 1, 0).reshape(GQA, N_BRANCH * N_KV, d)
    gt = jnp.pad(gt, ((0, 0), (0, SUBLANES - N_BRANCH * N_KV), (0, 0))).reshape(_G_ROWS, d)
    w_tr = jnp.concatenate([q.T, vs.T, vw.T, gt], axis=0).astype(BF16)
    return w_nat, w_tr


def _expand_w1(w1):
    hid = w1.shape[1]
    w = w1.reshape(2, CMP_STRIDE, HEAD_DIM, hid).astype(BF16)
    zero = jnp.zeros_like(w)
    per_group = [jnp.concatenate([w if k == g else zero for k in range(N_KV)], axis=-1)
                 for g in range(N_KV)]
    return jnp.stack(per_group, axis=2).reshape(2, CMP_STRIDE * KV_W, N_KV * hid)


def _expand_w2(w2):
    hid = w2.shape[0]
    eye = jnp.eye(N_KV, dtype=w2.dtype).reshape(N_KV, 1, N_KV, 1)
    return (w2.reshape(1, hid, 1, HEAD_DIM) * eye).reshape(N_KV * hid, KV_W).astype(BF16)


def _expand_pe(pe):
    p = pe.reshape(2, CMP_STRIDE, 1, HEAD_DIM)
    return jnp.broadcast_to(p, (2, CMP_STRIDE, N_KV, HEAD_DIM)).reshape(2, CMP_STRIDE * KV_W)


def _bucket_thresholds():
    n = np.arange(2 * REL_MAX_DIST)
    max_exact = REL_BUCKETS // 2
    nf = np.maximum(n, max_exact).astype(np.float32)
    ratio = np.log(nf / np.float32(max_exact)) / np.float32(math.log(REL_MAX_DIST / max_exact))
    large = max_exact + (ratio * np.float32(REL_BUCKETS - max_exact)).astype(np.int32)
    table = np.where(n < max_exact, n, np.minimum(large, REL_BUCKETS - 1))
    return tuple(int(np.searchsorted(table, k, side="left")) for k in range(REL_BUCKETS))


def _bank_call(bias_rows, lead, n_tiles, dist_fn, name):
    nl = len(lead)
    thr = _bucket_thresholds()

    def body(rows_ref, o_ref):
        lead_ids = [pl.program_id(a) for a in range(nl)]
        row = lax.broadcasted_iota(jnp.int32, (SEL_BLOCK, GQ), 0)
        qi = lax.broadcasted_iota(jnp.int32, (SEL_BLOCK, GQ), 1) & (Q_BLOCK - 1)

        def tile(t, carry):
            dist, ok = dist_fn(lead_ids, t, row, qi)
            v = jnp.broadcast_to(rows_ref[0, 0:1, :], (SEL_BLOCK, GQ))
            for k in range(1, REL_BUCKETS):
                v = jnp.where(dist >= thr[k], rows_ref[0, k:k + 1, :], v)
            o_ref[(0,) * (nl + 1) + (t,)] = jnp.where(ok, v, NEG)
            return carry

        lax.fori_loop(0, n_tiles, tile, 0)

    return pl.pallas_call(
        body,
        grid=tuple(lead) + (N_KV,),
        in_specs=[pl.BlockSpec((1, REL_BUCKETS, GQ), lambda *i: (i[nl], 0, 0))],
        out_specs=pl.BlockSpec((1,) * (nl + 1) + (n_tiles, SEL_BLOCK, GQ),
                               lambda *i: tuple(i) + (0, 0, 0)),
        out_shape=jax.ShapeDtypeStruct(tuple(lead) + (N_KV, n_tiles, SEL_BLOCK, GQ), F32),
        compiler_params=_cparams(nl + 1),
        name=name,
    )(bias_rows)


def _bias_banks(rel_bias, t):
    n_cmp_rows = t // CMP_STRIDE
    rows = rel_bias.reshape(REL_BUCKETS, N_KV, GQA).transpose(1, 0, 2)
    rows = jnp.repeat(rows, Q_BLOCK, axis=2) * LOG2E

    def sel_dist(lead, tile, row, qi):
        dist = SEL_BLOCK * tile + qi - row
        return dist, dist >= 0

    sbank = _bank_call(rows, (), NEAR_TILES + 1, sel_dist, "bias_bank_sel")

    def win_dist(lead, tile, row, qi):
        dist = SEL_BLOCK * tile + qi - row
        return dist, (dist >= 0) & (dist < WINDOW)

    wbank = _bank_call(rows, (), WIN_BLOCKS + 2, win_dist, "bias_bank_win")

    cmp_c = n_cmp_rows - 4
    shifts = tuple(int((-(cmp_c - 4 * p)) % 8) for p in range(2))
    n_tiles = (cmp_c + n_cmp_rows + 8 + SEL_BLOCK - 1) // SEL_BLOCK

    def cmp_dist(lead, tile, row, qi):
        e = SEL_BLOCK * tile + row - jnp.where(lead[0] == 0, shifts[0], shifts[1])
        dist = qi - CMP_STRIDE * e + (CMP_STRIDE * cmp_c - (CMP_BLOCK - 1))
        return dist, (dist >= 0) & (e >= 0)

    ebank = _bank_call(rows, (2,), n_tiles, cmp_dist, "bias_bank_cmp")
    ebank = ebank.reshape(2, N_KV, n_tiles * SEL_BLOCK, GQ)
    return ebank, sbank, wbank, shifts


def _overlap_t(t):
    n_cmp_rows = t // CMP_STRIDE
    n_sb = t // SEL_BLOCK
    c_start = np.arange(n_cmp_rows)[None, :] * CMP_STRIDE
    s_start = np.arange(n_sb)[:, None] * SEL_BLOCK
    ov = np.clip(np.minimum(c_start + CMP_BLOCK, s_start + SEL_BLOCK)
                 - np.maximum(c_start, s_start), 0, None) / CMP_BLOCK
    ov[:, n_cmp_rows - 1] = 0.0
    return jnp.asarray(ov, dtype=BF16)


def _layer(x, c_pad, w_in, q_norm, k_norm, cmp_pe_k, cmp_w1_k, cmp_w2_k, cmp_pe_v, cmp_w1_v,
           cmp_w2_v, rel_bias, conv_w, w_out, norm1, norm2, w_ada, b_ada, w_ff1, w_ff2):
    b, t, d = x.shape
    scale = HEAD_DIM ** -0.5

    mod = _mod_call(c_pad, w_ada, b_ada)[:b].reshape(b, N_MOD, d)

    qn_col = (jnp.tile(q_norm, N_HEADS) * (scale * LOG2E)).reshape(ATTN_W, 1)
    kn_t = jnp.tile(k_norm, N_KV).reshape(1, KV_W)
    bdq = _block_diag_ones(ATTN_W)
    bdk = _block_diag_ones(KV_W)
    w_nat, w_tr = _pack_w_in(w_in)
    qt, kc_raw, vc_raw, ks, vst, kw, vwt, gt, conv = _inproj_call(
        x, mod, norm1.reshape(1, d), w_nat, w_tr, qn_col, kn_t, conv_w, bdq, bdk)

    kc, vct = _compress_call(
        kc_raw, vc_raw, _expand_pe(cmp_pe_k), _expand_pe(cmp_pe_v), _expand_w1(cmp_w1_k), _expand_w1(cmp_w1_v),
        _expand_w2(cmp_w2_k), _expand_w2(cmp_w2_v).T, kn_t, bdk)

    ebank, sbank, wbank, eb_shift = _bias_banks(rel_bias, t)
    attn = _attn_call(qt, kc, vct, ks, vst, kw, vwt, gt, _overlap_t(t), ebank, sbank.astype(BF16),
                      wbank.astype(BF16), eb_shift)

    return _ffn_call(x, attn, conv, mod, norm2.reshape(1, d), w_out.astype(BF16),
                     w_ff1.astype(BF16), w_ff2.astype(BF16))


def kernel(x, c, w_in, q_norm, k_norm, cmp_pe_k, cmp_w1_k, cmp_w2_k, cmp_pe_v, cmp_w1_v, cmp_w2_v,
           rel_bias, conv_w, w_out, norm1, norm2, w_ada, b_ada, w_ff1, w_ff2):
    b = x.shape[0]
    c_pad = jnp.pad(c, ((0, (-b) % 8), (0, 0)))
    for l in range(w_in.shape[0]):
        x = _layer(x, c_pad, w_in[l], q_norm[l], k_norm[l], cmp_pe_k[l], cmp_w1_k[l], cmp_w2_k[l],
                   cmp_pe_v[l], cmp_w1_v[l], cmp_w2_v[l], rel_bias, conv_w[l], w_out[l],
                   norm1[l], norm2[l], w_ada[l], b_ada[l], w_ff1[l], w_ff2[l])
    return x
```
